```python
import math
import jax
import jax.numpy as jnp
from jax import lax
import numpy as np

D_MODEL = 1024
BATCH = 2
SEQ = 8192
DEPTH = 2

CHUNK = 64
MEM_LEN = 256
EPS = 1e-6

SC_DIM = 512
SC_WIDTH = 3
SSM_HEADS = 16
SSM_HEAD_DIM = 64
SSM_INNER = SSM_HEADS * SSM_HEAD_DIM
SSM_GROUPS = 2
SSM_STATE = 128
SSM_CONV = 4
SSM_CONV_DIM = SSM_INNER + 2 * SSM_GROUPS * SSM_STATE
GDN_HEADS = 8
GDN_DK = 128
GDN_DV = 128
GDN_CONV = 4
GDN_QK = GDN_HEADS * GDN_DK
GDN_V = GDN_HEADS * GDN_DV
SB_HEADS = 8
SB_HEAD_DIM = 64
SB_DIM = SB_HEADS * SB_HEAD_DIM
SB_BLOCK = 128
XA_HEADS = 4
XA_HEAD_DIM = D_MODEL // XA_HEADS
MOE_GROUPS = 4
MOE_EXPERTS_PER_GROUP = 8
MOE_EXPERTS = MOE_GROUPS * MOE_EXPERTS_PER_GROUP
MOE_TOPK = 2
MOE_FF = 512
MOE_BLOCK = 128

EVEN_SPLITS = (SC_DIM, 2 * SC_DIM, 3 * SC_DIM, 3 * SC_DIM + SSM_INNER, 3 * SC_DIM + SSM_INNER + SSM_CONV_DIM)
EVEN_IN = 3 * SC_DIM + SSM_INNER + SSM_CONV_DIM + SSM_HEADS
EVEN_MIX = SC_DIM + SSM_INNER
GDN_QKV = 2 * GDN_QK + GDN_V
ODD_SPLITS = (GDN_QKV, GDN_QKV + GDN_V, GDN_QKV + GDN_V + GDN_HEADS, GDN_QKV + GDN_V + 2 * GDN_HEADS, GDN_QKV + GDN_V + 2 * GDN_HEADS + SB_DIM, GDN_QKV + GDN_V + 2 * GDN_HEADS + 2 * SB_DIM)
ODD_IN = GDN_QKV + GDN_V + 2 * GDN_HEADS + 3 * SB_DIM
ODD_MIX = GDN_V + SB_DIM

kernel_name = 'hybrid_conv_ssd_deltanet_stickbreak_hmoe'


def rmsnorm(x, g):
    xf = x.astype(jnp.float32)
    y = xf * lax.rsqrt(jnp.mean(xf * xf, axis=-1, keepdims=True) + EPS)
    return (y * g.astype(jnp.float32)).astype(x.dtype)


def l2norm(x):
    return x * lax.rsqrt(jnp.sum(x * x, axis=-1, keepdims=True) + EPS)


def causal_dwconv(x, w):
    width = w.shape[0]
    length = x.shape[1]
    xp = jnp.pad(x, ((0, 0), (width - 1, 0), (0, 0)))
    return sum(xp[:, j:j + length] * w[j] for j in range(width))


def short_conv_mixer(b_gate, c_gate, x_in, w_conv):
    return b_gate * causal_dwconv(c_gate * x_in, w_conv)


def ssd_mixer(z, xbc, dt, conv_w, conv_b, dt_bias, a_log, d_skip, norm_w):
    f32 = jnp.float32
    bsz, length, _ = z.shape
    nc = length // CHUNK
    hpg = SSM_HEADS // SSM_GROUPS
    xbc = jax.nn.silu(causal_dwconv(xbc, conv_w) + conv_b).astype(f32)
    xs, bm, cm = jnp.split(xbc, (SSM_INNER, SSM_INNER + SSM_GROUPS * SSM_STATE), axis=-1)
    xs = xs.reshape(bsz, nc, CHUNK, SSM_GROUPS, hpg, SSM_HEAD_DIM)
    bm = bm.reshape(bsz, nc, CHUNK, SSM_GROUPS, SSM_STATE)
    cm = cm.reshape(bsz, nc, CHUNK, SSM_GROUPS, SSM_STATE)
    dt = jax.nn.softplus(dt.astype(f32) + dt_bias.astype(f32)).reshape(bsz, nc, CHUNK, SSM_GROUPS, hpg)
    a = -jnp.exp(a_log.astype(f32)).reshape(SSM_GROUPS, hpg)
    a_cs = jnp.cumsum(dt * a, axis=2)
    causal = jnp.tril(jnp.ones((CHUNK, CHUNK), dtype=bool))
    seg = a_cs[:, :, :, None] - a_cs[:, :, None, :]
    decay = jnp.exp(jnp.where(causal[:, :, None, None], seg, -jnp.inf))
    xdt = xs * dt[..., None]
    cb = jnp.einsum('bctgn,bcsgn->bctsg', cm, bm)
    y_diag = jnp.einsum('bctsgh,bcsghp->bctghp', cb[..., None] * decay, xdt)
    to_end = jnp.exp(a_cs[:, :, -1:] - a_cs)
    states = jnp.einsum('bcsgn,bcsghp->bcghpn', bm, xdt * to_end[..., None])
    chunk_decay = jnp.exp(a_cs[:, :, -1])

    def step(s, inp):
        st, dec = inp
        return s * dec[..., None, None] + st, s

    s0 = jnp.zeros((bsz, SSM_GROUPS, hpg, SSM_HEAD_DIM, SSM_STATE), f32)
    _, prev = lax.scan(step, s0, (jnp.moveaxis(states, 1, 0), jnp.moveaxis(chunk_decay, 1, 0)))
    prev = jnp.moveaxis(prev, 0, 1)
    y_off = jnp.einsum('bctgn,bcghpn->bctghp', cm, prev) * jnp.exp(a_cs)[..., None]
    y = y_diag + y_off + xs * d_skip.astype(f32).reshape(SSM_GROUPS, hpg, 1)
    y = y.reshape(bsz, length, SSM_INNER) * jax.nn.silu(z.astype(f32))
    yg = y.reshape(bsz, length, SSM_GROUPS, SSM_INNER // SSM_GROUPS)
    yg = yg * lax.rsqrt(jnp.mean(yg * yg, axis=-1, keepdims=True) + EPS)
    return yg.reshape(bsz, length, SSM_INNER) * norm_w.astype(f32)


def gated_deltanet_mixer(qkv, z, a, b, conv_w, dt_bias, a_log, norm_w):
    f32 = jnp.float32
    bsz, length, _ = qkv.shape
    nc = length // CHUNK
    qkv = jax.nn.silu(causal_dwconv(qkv, conv_w)).astype(f32)
    q, k, v = jnp.split(qkv, (GDN_QK, 2 * GDN_QK), axis=-1)

    def to_chunks(t, d):
        return t.reshape(bsz, nc, CHUNK, GDN_HEADS, d).transpose(0, 3, 1, 2, 4)

    q = to_chunks(l2norm(q.reshape(bsz, length, GDN_HEADS, GDN_DK)) * GDN_DK ** -0.5, GDN_DK)
    k = to_chunks(l2norm(k.reshape(bsz, length, GDN_HEADS, GDN_DK)), GDN_DK)
    v = to_chunks(v, GDN_DV)
    beta = jax.nn.sigmoid(b.astype(f32)).reshape(bsz, nc, CHUNK, GDN_HEADS).transpose(0, 3, 1, 2)
    g = -jnp.exp(a_log.astype(f32)) * jax.nn.softplus(a.astype(f32) + dt_bias.astype(f32))
    g = g.reshape(bsz, nc, CHUNK, GDN_HEADS).transpose(0, 3, 1, 2)
    gc = jnp.cumsum(g, axis=-1)
    incl = jnp.tril(jnp.ones((CHUNK, CHUNK), dtype=bool))
    strict = jnp.tril(jnp.ones((CHUNK, CHUNK), f32), -1)
    decay = jnp.exp(jnp.where(incl, gc[..., :, None] - gc[..., None, :], -jnp.inf))
    kb = k * beta[..., None]
    lower = jnp.einsum('bhctd,bhcsd->bhcts', kb, k) * decay * strict
    rhs = jnp.concatenate([v * beta[..., None], kb * jnp.exp(gc)[..., None]], axis=-1)
    sol = lax.linalg.triangular_solve(lower + jnp.eye(CHUNK, dtype=f32), rhs,
                                      left_side=True, lower=True, unit_diagonal=True)
    u, w = jnp.split(sol, (GDN_DV,), axis=-1)
    a_qk = jnp.einsum('bhctd,bhcsd->bhcts', q, k) * decay
    q_dec = q * jnp.exp(gc)[..., None]
    k_dec = k * jnp.exp(gc[..., -1:] - gc)[..., None]
    chunk_decay = jnp.exp(gc[..., -1])

    def step(s, inp):
        qd, kd, uc, wc, aqk, dec = inp
        v_new = uc - jnp.einsum('bhld,bhdv->bhlv', wc, s)
        o = jnp.einsum('bhld,bhdv->bhlv', qd, s) + jnp.einsum('bhts,bhsv->bhtv', aqk, v_new)
        s = s * dec[..., None, None] + jnp.einsum('bhld,bhlv->bhdv', kd, v_new)
        return s, o

    s0 = jnp.zeros((bsz, GDN_HEADS, GDN_DK, GDN_DV), f32)
    seq_in = (q_dec, k_dec, u, w, a_qk, chunk_decay)
    _, o = lax.scan(step, s0, (jnp.moveaxis(seq_in[0], 2, 0), jnp.moveaxis(seq_in[1], 2, 0),
                               jnp.moveaxis(seq_in[2], 2, 0), jnp.moveaxis(seq_in[3], 2, 0),
                               jnp.moveaxis(seq_in[4], 2, 0), jnp.moveaxis(seq_in[5], 2, 0)))
    o = o.transpose(1, 0, 3, 2, 4).reshape(bsz, length, GDN_HEADS, GDN_DV)
    o = o * lax.rsqrt(jnp.mean(o * o, axis=-1, keepdims=True) + EPS) * norm_w.astype(f32)
    o = o * jax.nn.silu(z.astype(f32).reshape(bsz, length, GDN_HEADS, GDN_DV))
    return o.reshape(bsz, length, GDN_V)


def stick_breaking_mixer(q, k, v):
    bsz, length, _ = q.shape
    nb = length // SB_BLOCK

    def heads(t):
        return t.reshape(bsz, length, SB_HEADS, SB_HEAD_DIM).transpose(0, 2, 1, 3)

    q, k, v = heads(q), heads(k), heads(v)
    q_blocks = q.reshape(bsz, SB_HEADS, nb, SB_BLOCK, SB_HEAD_DIM).transpose(2, 0, 1, 3, 4)
    k_pos = jnp.arange(length)
    scale = SB_HEAD_DIM ** -0.5

    def block(args):
        qb, blk = args
        logits = jnp.einsum('bhqd,bhkd->bhqk', qb, k).astype(jnp.float32) * scale
        q_pos = blk * SB_BLOCK + jnp.arange(SB_BLOCK)
        earlier = k_pos[None, :] < q_pos[:, None]
        log_keep = jnp.where(earlier, jax.nn.log_sigmoid(-logits), 0.0)
        log_between = lax.cumsum(log_keep, axis=3, reverse=True) - log_keep
        log_w = jnp.where(earlier, jax.nn.log_sigmoid(logits) + log_between, -jnp.inf)
        return jnp.einsum('bhqk,bhkd->bhqd', jnp.exp(log_w).astype(v.dtype), v)

    out = lax.map(block, (q_blocks, jnp.arange(nb)))
    return out.transpose(1, 0, 3, 2, 4).reshape(bsz, length, SB_DIM)


def memory_cross_attention(h, memn, wq, wk, wv, wo):
    bsz, length, _ = h.shape
    m = memn.shape[1]
    q = (h @ wq).reshape(bsz, length, XA_HEADS, XA_HEAD_DIM)
    k = (memn @ wk).reshape(bsz, m, XA_HEADS, XA_HEAD_DIM)
    v = (memn @ wv).reshape(bsz, m, XA_HEADS, XA_HEAD_DIM)
    s = jnp.einsum('blhd,bmhd->bhlm', q, k).astype(jnp.float32) * XA_HEAD_DIM ** -0.5
    p = jax.nn.softmax(s, axis=-1).astype(v.dtype)
    o = jnp.einsum('bhlm,bmhd->blhd', p, v).reshape(bsz, length, D_MODEL)
    return o @ wo


def hierarchical_moe(x, w_group, b_group, w_expert, b_expert, w_gate, w_up, w_down):
    f32 = jnp.float32
    bsz, length, d = x.shape
    n_tok = bsz * length
    xf = x.reshape(n_tok, d)
    g_prob = jax.nn.softmax((xf @ w_group).astype(f32) + b_group.astype(f32), axis=-1)
    g_sel = jnp.argmax(g_prob, axis=-1)
    g_w = jnp.take_along_axis(g_prob, g_sel[:, None], axis=-1)
    e_logits = ((xf @ w_expert).astype(f32) + b_expert.astype(f32)).reshape(n_tok, MOE_GROUPS, MOE_EXPERTS_PER_GROUP)
    e_logits = jnp.take_along_axis(e_logits, g_sel[:, None, None], axis=1)[:, 0]
    top_p, top_i = lax.top_k(jax.nn.softmax(e_logits, axis=-1), MOE_TOPK)
    gate = g_w * top_p / jnp.sum(top_p, axis=-1, keepdims=True)
    eid = (g_sel[:, None] * MOE_EXPERTS_PER_GROUP + top_i).reshape(-1).astype(jnp.int32)
    tok = jnp.repeat(jnp.arange(n_tok, dtype=jnp.int32), MOE_TOPK)
    gw = gate.reshape(-1)
    n_assign = n_tok * MOE_TOPK
    order = jnp.argsort(eid)
    s_eid, s_tok, s_w = eid[order], tok[order], gw[order]
    counts = jax.ops.segment_sum(jnp.ones_like(eid), eid, num_segments=MOE_EXPERTS)
    starts = jnp.cumsum(counts) - counts
    padded = (counts + MOE_BLOCK - 1) // MOE_BLOCK * MOE_BLOCK
    pad_end = jnp.cumsum(padded)
    pad_start = pad_end - padded
    dest = pad_start[s_eid] + jnp.arange(n_assign, dtype=jnp.int32) - starts[s_eid]
    n_blocks = -(-(n_assign + MOE_EXPERTS * (MOE_BLOCK - 1)) // MOE_BLOCK)
    n_slots = n_blocks * MOE_BLOCK
    slot_tok = jnp.full((n_slots,), n_tok, jnp.int32).at[dest].set(s_tok)
    slot_w = jnp.zeros((n_slots,), f32).at[dest].set(s_w)
    block_eid = jnp.minimum(jnp.searchsorted(pad_end, jnp.arange(n_blocks, dtype=jnp.int32) * MOE_BLOCK, side='right'),
                            MOE_EXPERTS - 1)
    x_pad = jnp.concatenate([xf, jnp.zeros((1, d), xf.dtype)], axis=0)
    xs = x_pad[slot_tok].reshape(n_blocks, MOE_BLOCK, d)

    def expert_block(args):
        xb, e = args
        return (jax.nn.silu(xb @ w_gate[e]) * (xb @ w_up[e])) @ w_down[e]

    ys = lax.map(expert_block, (xs, block_eid)).reshape(n_slots, d)
    y = jnp.zeros((n_tok + 1, d), f32).at[slot_tok].add(ys.astype(f32) * slot_w[:, None])[:n_tok]
    return y.reshape(bsz, length, d).astype(x.dtype)


def even_mixer(u, w_in, sc_conv, ssm_conv_w, ssm_conv_b, ssm_dt_bias, ssm_a_log, ssm_d, ssm_norm, w_out):
    sc_b, sc_c, sc_x, z, xbc, dt = jnp.split(u @ w_in, EVEN_SPLITS, axis=-1)
    y_a = short_conv_mixer(sc_b, sc_c, sc_x, sc_conv)
    y_b = ssd_mixer(z, xbc, dt, ssm_conv_w, ssm_conv_b, ssm_dt_bias, ssm_a_log, ssm_d, ssm_norm).astype(u.dtype)
    return jnp.concatenate([y_a, y_b], axis=-1) @ w_out


def odd_mixer(u, w_in, gdn_conv, gdn_dt_bias, gdn_a_log, gdn_norm, w_out):
    qkv, z, a, b, sq, sk, sv = jnp.split(u @ w_in, ODD_SPLITS, axis=-1)
    y_c = gated_deltanet_mixer(qkv, z, a, b, gdn_conv, gdn_dt_bias, gdn_a_log, gdn_norm).astype(u.dtype)
    y_d = stick_breaking_mixer(sq, sk, sv).astype(u.dtype)
    return jnp.concatenate([y_c, y_d], axis=-1) @ w_out


def setup_inputs(seed: int = 0) -> dict:
    key = jax.random.key(seed)
    keys = iter(jax.random.split(key, 48))
    f32 = jnp.float32
    n_even = (DEPTH + 1) // 2
    n_odd = DEPTH // 2

    def normal(shape, scale):
        return jax.random.normal(next(keys), shape, f32) * scale

    def gain(shape):
        return 1.0 + normal(shape, 0.05)

    def dt_bias(shape):
        dt = jnp.exp(jax.random.uniform(next(keys), shape, f32, math.log(1e-3), math.log(1e-1)))
        return dt + jnp.log(-jnp.expm1(-dt))

    def a_log(shape):
        return jnp.log(jax.random.uniform(next(keys), shape, f32, 1.0, 16.0))

    sd = D_MODEL ** -0.5
    return {
        'x': normal((BATCH, SEQ, D_MODEL), 1.0),
        'mem': normal((BATCH, MEM_LEN, D_MODEL), 1.0),
        'mem_norm': gain((D_MODEL,)),
        'final_norm': gain((D_MODEL,)),
        'norm_mix': gain((DEPTH, D_MODEL)),
        'norm_xa': gain((DEPTH, D_MODEL)),
        'norm_ffn': gain((DEPTH, D_MODEL)),
        'xa_wq': normal((DEPTH, D_MODEL, D_MODEL), sd),
        'xa_wk': normal((DEPTH, D_MODEL, D_MODEL), sd),
        'xa_wv': normal((DEPTH, D_MODEL, D_MODEL), sd),
        'xa_wo': normal((DEPTH, D_MODEL, D_MODEL), sd),
        'moe_w_group': normal((DEPTH, D_MODEL, MOE_GROUPS), sd),
        'moe_b_group': normal((DEPTH, MOE_GROUPS), 0.01),
        'moe_w_expert': normal((DEPTH, D_MODEL, MOE_EXPERTS), sd),
        'moe_b_expert': normal((DEPTH, MOE_EXPERTS), 0.01),
        'moe_w_gate': normal((DEPTH, MOE_EXPERTS, D_MODEL, MOE_FF), sd),
        'moe_w_up': normal((DEPTH, MOE_EXPERTS, D_MODEL, MOE_FF), sd),
        'moe_w_down': normal((DEPTH, MOE_EXPERTS, MOE_FF, D_MODEL), MOE_FF ** -0.5),
        'ev_w_in': normal((n_even, D_MODEL, EVEN_IN), sd),
        'ev_sc_conv': normal((n_even, SC_WIDTH, SC_DIM), SC_WIDTH ** -0.5),
        'ev_ssm_conv_w': normal((n_even, SSM_CONV, SSM_CONV_DIM), SSM_CONV ** -0.5),
        'ev_ssm_conv_b': normal((n_even, SSM_CONV_DIM), 0.02),
        'ev_ssm_dt_bias': dt_bias((n_even, SSM_HEADS)),
        'ev_ssm_a_log': a_log((n_even, SSM_HEADS)),
        'ev_ssm_d': gain((n_even, SSM_HEADS)),
        'ev_ssm_norm': gain((n_even, SSM_INNER)),
        'ev_w_out': normal((n_even, EVEN_MIX, D_MODEL), EVEN_MIX ** -0.5),
        'od_w_in': normal((n_odd, D_MODEL, ODD_IN), sd),
        'od_gdn_conv': normal((n_odd, GDN_CONV, GDN_QKV), GDN_CONV ** -0.5),
        'od_gdn_dt_bias': dt_bias((n_odd, GDN_HEADS)),
        'od_gdn_a_log': a_log((n_odd, GDN_HEADS)),
        'od_gdn_norm': gain((n_odd, GDN_DV)),
        'od_w_out': normal((n_odd, ODD_MIX, D_MODEL), ODD_MIX ** -0.5),
    }


def reference(x, mem, mem_norm, final_norm, norm_mix, norm_xa, norm_ffn,
              xa_wq, xa_wk, xa_wv, xa_wo,
              moe_w_group, moe_b_group, moe_w_expert, moe_b_expert, moe_w_gate, moe_w_up, moe_w_down,
              ev_w_in, ev_sc_conv, ev_ssm_conv_w, ev_ssm_conv_b, ev_ssm_dt_bias, ev_ssm_a_log, ev_ssm_d,
              ev_ssm_norm, ev_w_out,
              od_w_in, od_gdn_conv, od_gdn_dt_bias, od_gdn_a_log, od_gdn_norm, od_w_out):
    memn = rmsnorm(mem, mem_norm)
    h = x
    for layer in range(DEPTH):
        i = layer // 2
        u = rmsnorm(h, norm_mix[layer])
        if layer % 2 == 0:
            mix = even_mixer(u, ev_w_in[i], ev_sc_conv[i], ev_ssm_conv_w[i], ev_ssm_conv_b[i],
                             ev_ssm_dt_bias[i], ev_ssm_a_log[i], ev_ssm_d[i], ev_ssm_norm[i], ev_w_out[i])
        else:
            mix = odd_mixer(u, od_w_in[i], od_gdn_conv[i], od_gdn_dt_bias[i], od_gdn_a_log[i],
                            od_gdn_norm[i], od_w_out[i])
        h = h + mix.astype(h.dtype)
        h = h + memory_cross_attention(rmsnorm(h, norm_xa[layer]), memn, xa_wq[layer], xa_wk[layer],
                                       xa_wv[layer], xa_wo[layer]).astype(h.dtype)
        h = h + hierarchical_moe(rmsnorm(h, norm_ffn[layer]), moe_w_group[layer], moe_b_group[layer],
                                 moe_w_expert[layer], moe_b_expert[layer], moe_w_gate[layer],
                                 moe_w_up[layer], moe_w_down[layer])
    return rmsnorm(h, final_norm)
```

```python
import functools

import jax
import jax.numpy as jnp
import numpy as np
from jax import lax
from jax.experimental import pallas as pl
from jax.experimental.pallas import tpu as pltpu

F32 = jnp.float32
BF16 = jnp.bfloat16
EPS = 1e-6

D_MODEL = 1024
MEM_LEN = 256
SC_DIM = 512
SSM_HEADS = 16
SSM_HEAD_DIM = 64
SSM_INNER = 1024
SSM_GROUPS = 2
SSM_STATE = 128
SSM_XBC = SSM_INNER + 2 * SSM_GROUPS * SSM_STATE
SSD_CHUNK = 128
GDN_HEADS = 8
GDN_D = 128
GDN_CHUNK = 64
GDN_TILE = 128
SB_HEADS = 8
SB_HEAD_DIM = 64
SB_DIM = 512
SB_BLOCK = 128
XA_HEADS = 4
XA_HEAD_DIM = 256
MOE_GROUPS = 4
MOE_PER_GROUP = 8
MOE_EXPERTS = 32
MOE_FF = 512
MOE_ROWS = 256
HALO = 8
LANES = 128
SB_LOG_ZERO = -104.0
VMEM_LIMIT = 56 * 1024 * 1024


def _cparams(*sem):
    return pltpu.CompilerParams(dimension_semantics=sem, vmem_limit_bytes=VMEM_LIMIT)


def _mm(a, b):
    return jnp.dot(a.astype(BF16), b.astype(BF16), preferred_element_type=F32)


def _mm_nt(a, b):
    return lax.dot_general(a.astype(BF16), b.astype(BF16), (((1,), (1,)), ((), ())),
                           preferred_element_type=F32)


def _split_bf16(x, n):
    parts, r = [], x
    for _ in range(n):
        p = r.astype(BF16)
        parts.append(p)
        r = r - p.astype(F32)
    return parts


def _mm_sel_rhs(x, sel, n=3):
    return sum(jnp.dot(p, sel, preferred_element_type=F32) for p in _split_bf16(x, n))


def _mm_sel_lhs(sel, x, n=3):
    return sum(jnp.dot(sel, p, preferred_element_type=F32) for p in _split_bf16(x, n))


def _silu(x):
    return x * jax.nn.sigmoid(x)


def _softplus(x):
    return jnp.maximum(x, 0.0) + jnp.log(1.0 + jnp.exp(-jnp.abs(x)))


def _rms(x, g):
    return x * lax.rsqrt(jnp.mean(x * x, axis=-1, keepdims=True) + EPS) * g


def _rms_matmul_kernel(x_ref, g_ref, w_ref, ws_ref, o_ref, os_ref, xn_ref):
    @pl.when(pl.program_id(1) == 0)
    def _():
        xn = _rms(x_ref[...], g_ref[...]).astype(BF16)
        xn_ref[...] = xn
        os_ref[...] = jnp.dot(xn, ws_ref[...], preferred_element_type=F32)

    o_ref[...] = jnp.dot(xn_ref[...], w_ref[...], preferred_element_type=F32)


def rms_matmul(x, g, w, ws, tm=512, tn=512):
    m, k = x.shape
    n = w.shape[1]
    tm = min(tm, m)
    return pl.pallas_call(
        _rms_matmul_kernel,
        grid=(m // tm, n // tn),
        in_specs=[
            pl.BlockSpec((tm, k), lambda i, j: (i, 0)),
            pl.BlockSpec((1, k), lambda i, j: (0, 0)),
            pl.BlockSpec((k, tn), lambda i, j: (0, j)),
            pl.BlockSpec((k, LANES), lambda i, j: (0, 0)),
        ],
        out_specs=[
            pl.BlockSpec((tm, tn), lambda i, j: (i, j)),
            pl.BlockSpec((tm, LANES), lambda i, j: (i, 0)),
        ],
        out_shape=[jax.ShapeDtypeStruct((m, n), F32), jax.ShapeDtypeStruct((m, LANES), F32)],
        scratch_shapes=[pltpu.VMEM((tm, k), BF16)],
        compiler_params=_cparams("parallel", "arbitrary"),
        name="rms_matmul",
    )(x, g.reshape(1, k), w, ws)


def _causal_conv(ext_ref, w_ref, rows):
    width = w_ref.shape[0]
    acc = None
    for j in range(width):
        start = HALO - (width - 1) + j
        term = w_ref[j:j + 1, :] * ext_ref[start:start + rows, :]
        acc = term if acc is None else acc + term
    return acc


def _halo_index(rows):
    step = rows // HALO
    return lambda i: jnp.maximum(i * step - 1, 0)


def _sc_kernel(b_ref, c_ref, x_ref, ch_ref, xh_ref, w_ref, o_ref, ext_ref):
    rows = o_ref.shape[0]
    first = pl.program_id(1) == 0
    ext_ref[0:HALO, :] = jnp.where(first, 0.0, ch_ref[...] * xh_ref[...])
    ext_ref[HALO:, :] = c_ref[...] * x_ref[...]
    o_ref[...] = b_ref[...] * _causal_conv(ext_ref, w_ref, rows)


def short_conv_mixer(y3, w, tl=512):
    bsz, length, _ = y3.shape
    tl = min(tl, length)
    hidx = _halo_index(tl)
    return pl.pallas_call(
        _sc_kernel,
        grid=(bsz, length // tl),
        in_specs=[
            pl.BlockSpec((None, tl, SC_DIM), lambda b, i: (b, i, 0)),
            pl.BlockSpec((None, tl, SC_DIM), lambda b, i: (b, i, 1)),
            pl.BlockSpec((None, tl, SC_DIM), lambda b, i: (b, i, 2)),
            pl.BlockSpec((None, HALO, SC_DIM), lambda b, i: (b, hidx(i), 1)),
            pl.BlockSpec((None, HALO, SC_DIM), lambda b, i: (b, hidx(i), 2)),
            pl.BlockSpec(w.shape, lambda b, i: (0, 0)),
        ],
        out_specs=pl.BlockSpec((None, tl, SC_DIM), lambda b, i: (b, i, 0)),
        out_shape=jax.ShapeDtypeStruct((bsz, length, SC_DIM), F32),
        scratch_shapes=[pltpu.VMEM((tl + HALO, SC_DIM), F32)],
        compiler_params=_cparams("parallel", "arbitrary"),
        name="short_conv_mixer",
    )(y3, y3, y3, y3, y3, w)


def _ssd_kernel(xbc_ref, halo_ref, z_ref, dt_ref, dtt_ref, cw_ref, cb_ref, dtb_r_ref, dtb_c_ref,
                alog_r_ref, alog_c_ref, d_ref, nw_ref, tri_ref, trit_ref, eh_ref, eq_ref,
                o_ref, ext_ref, s_ref):
    q = SSD_CHUNK
    hpg = SSM_HEADS // SSM_GROUPS
    gw = hpg * SSM_HEAD_DIM
    first = pl.program_id(1) == 0

    @pl.when(first)
    def _():
        s_ref[...] = jnp.zeros_like(s_ref)

    ext_ref[0:HALO, :] = jnp.where(first, 0.0, halo_ref[...])
    ext_ref[HALO:, :] = xbc_ref[...]
    xbc = _silu(_causal_conv(ext_ref, cw_ref, q) + cb_ref[...])
    xs = xbc[:, :SSM_INNER]
    bm = xbc[:, SSM_INNER:SSM_INNER + SSM_GROUPS * SSM_STATE]
    cm = xbc[:, SSM_INNER + SSM_GROUPS * SSM_STATE:]

    dt = _softplus(dt_ref[...] + dtb_r_ref[...])
    acs = _mm_sel_lhs(tri_ref[...], dt * -jnp.exp(alog_r_ref[...]))
    dtt = _softplus(dtt_ref[...] + dtb_c_ref[...])
    acst = _mm_sel_rhs(dtt * -jnp.exp(alog_c_ref[...]), trit_ref[...])
    dt_full = _mm_sel_rhs(dt, eh_ref[...])
    acs_full = _mm_sel_rhs(acs, eh_ref[...])
    acs_col = _mm_sel_rhs(acs, eq_ref[...])

    xdt = xs * dt_full
    acs_last = acs_full[q - 1:q, :]
    xw = xdt * jnp.exp(acs_last - acs_full)
    chunk_decay = jnp.exp(acs_last)

    row = lax.broadcasted_iota(jnp.int32, (q, q), 0)
    col = lax.broadcasted_iota(jnp.int32, (q, q), 1)
    causal = row >= col
    lane = lax.broadcasted_iota(jnp.int32, (q, 2 * SSM_HEAD_DIM), 1)

    y_diag, y_off = [], []
    for g in range(SSM_GROUPS):
        bm_g = bm[:, g * SSM_STATE:(g + 1) * SSM_STATE]
        cm_g = cm[:, g * SSM_STATE:(g + 1) * SSM_STATE]
        cb_g = _mm_nt(cm_g, bm_g)
        state = s_ref[g]
        y_off.append(_mm(cm_g, state))
        s_ref[g] = state * chunk_decay[:, g * gw:(g + 1) * gw] + _mm(bm_g.T, xw[:, g * gw:(g + 1) * gw])
        for pair in range(hpg // 2):
            h0 = g * hpg + 2 * pair
            xdt_pair = xdt[:, h0 * SSM_HEAD_DIM:(h0 + 2) * SSM_HEAD_DIM]
            outs = []
            for h in (h0, h0 + 1):
                seg = acs_col[:, h * q:(h + 1) * q] - acst[h:h + 1, :]
                decay = jnp.where(causal, jnp.exp(seg), 0.0)
                outs.append(_mm(cb_g * decay, xdt_pair))
            y_diag.append(jnp.where(lane < SSM_HEAD_DIM, outs[0], outs[1]))
    y = (jnp.concatenate(y_diag, axis=1) + jnp.concatenate(y_off, axis=1) * jnp.exp(acs_full)
         + xs * d_ref[...])
    y = y * _silu(z_ref[...])
    halves = []
    for g in range(SSM_GROUPS):
        yg = y[:, g * gw:(g + 1) * gw]
        halves.append(yg * lax.rsqrt(jnp.mean(yg * yg, axis=-1, keepdims=True) + EPS))
    o_ref[...] = jnp.concatenate(halves, axis=1) * nw_ref[...]


def _pad_lanes(v, fill=0.0):
    return jnp.pad(v.astype(F32), (0, LANES - v.shape[0]), constant_values=fill).reshape(1, LANES)


def _pad_col(v, rows=16):
    return jnp.pad(v.astype(F32), (0, rows - v.shape[0])).reshape(rows, 1)


def ssd_mixer(y3, small3, smallt, conv_w, conv_b, dt_bias, a_log, d_skip, norm_w):
    bsz, length, _ = y3.shape
    q = SSD_CHUNK
    hidx = _halo_index(q)
    tri = jnp.asarray(np.tril(np.ones((q, q), np.float32)), BF16)
    trit = jnp.asarray(np.triu(np.ones((q, q), np.float32)), BF16)
    heads = np.arange(LANES)[:, None]
    eh = jnp.asarray(heads == (np.arange(SSM_INNER)[None, :] // SSM_HEAD_DIM), BF16)
    eq = jnp.asarray(heads == (np.arange(SSM_HEADS * q)[None, :] // q), BF16)
    d_full = jnp.repeat(d_skip.astype(F32), SSM_HEAD_DIM).reshape(1, SSM_INNER)
    const = lambda a: pl.BlockSpec(a.shape, lambda b, c: (0,) * a.ndim)
    args = [conv_w, conv_b.reshape(1, -1), _pad_lanes(dt_bias), _pad_col(dt_bias), _pad_lanes(a_log),
            _pad_col(a_log), d_full, norm_w.reshape(1, -1), tri, trit, eh, eq]
    return pl.pallas_call(
        _ssd_kernel,
        grid=(bsz, length // q),
        in_specs=[
            pl.BlockSpec((None, q, SSM_XBC), lambda b, c: (b, c, 1)),
            pl.BlockSpec((None, HALO, SSM_XBC), lambda b, c: (b, hidx(c), 1)),
            pl.BlockSpec((None, q, SSM_INNER), lambda b, c: (b, c, 3)),
            pl.BlockSpec((None, q, LANES), lambda b, c: (b, c, 0)),
            pl.BlockSpec((None, 16, q), lambda b, c: (b, 0, c)),
        ] + [const(a) for a in args],
        out_specs=pl.BlockSpec((None, q, SSM_INNER), lambda b, c: (b, c, 0)),
        out_shape=jax.ShapeDtypeStruct((bsz, length, SSM_INNER), F32),
        scratch_shapes=[pltpu.VMEM((q + HALO, SSM_XBC), F32),
                        pltpu.VMEM((SSM_GROUPS, SSM_STATE, SSM_INNER // SSM_GROUPS), F32)],
        compiler_params=_cparams("parallel", "arbitrary"),
        name="ssd_mixer",
    )(y3, y3, y3, small3, smallt, *args)


def _unit_lower_inverse(a, row, col):
    eye = jnp.where(row == col, 1.0, 0.0)
    blk = lambda n: (row >> (n.bit_length() - 1)) == (col >> (n.bit_length() - 1))
    x = jnp.where(blk(16), -a, 0.0)
    t = eye + x
    p = x
    for _ in range(3):
        p = _mm(p, p)
        t = t + _mm(t, p)
    for n in (16, 32):
        off = jnp.where(blk(2 * n) & jnp.logical_not(blk(n)), a, 0.0)
        t = t - _mm(_mm(t, off), t)
    return t


def _gdn_kernel(qkv_ref, halo_ref, z_ref, ab_ref, abt_ref, cw_ref, dtb_r_ref, dtb_c_ref, alog_r_ref,
                alog_c_ref, nw_ref, tri_ref, trit_ref, eg_ref, eb_ref, o_ref, ext_ref, s_ref):
    n = GDN_TILE
    c = GDN_CHUNK
    d = GDN_D
    hd = GDN_HEADS * d
    first = pl.program_id(1) == 0

    @pl.when(first)
    def _():
        s_ref[...] = jnp.zeros_like(s_ref)

    ext_ref[0:HALO, :] = jnp.where(first, 0.0, halo_ref[...])
    ext_ref[HALO:, :] = qkv_ref[...]
    qkv = _silu(_causal_conv(ext_ref, cw_ref, n))
    z = z_ref[...]

    ab = ab_ref[...]
    g = -jnp.exp(alog_r_ref[...]) * _softplus(ab + dtb_r_ref[...])
    gc_full = _mm_sel_rhs(_mm_sel_lhs(tri_ref[...], g), eg_ref[...])
    beta_full = _mm_sel_rhs(jax.nn.sigmoid(ab), eb_ref[...])
    gt = -jnp.exp(alog_c_ref[...]) * _softplus(abt_ref[...] + dtb_c_ref[...])
    gct = _mm_sel_rhs(gt, trit_ref[...])

    row = lax.broadcasted_iota(jnp.int32, (n, n), 0)
    col = lax.broadcasted_iota(jnp.int32, (n, n), 1)
    same = (row >> (c.bit_length() - 1)) == (col >> (c.bit_length() - 1))
    incl = same & (row >= col)
    strict = same & (row > col)
    zeros_half = jnp.zeros((c, d), F32)

    outs = []
    for h in range(GDN_HEADS):
        sl = slice(h * d, (h + 1) * d)
        qh, kh, vh = qkv[:, sl], qkv[:, hd + h * d:hd + (h + 1) * d], qkv[:, 2 * hd + h * d:2 * hd + (h + 1) * d]
        qn = qh * lax.rsqrt(jnp.sum(qh * qh, axis=-1, keepdims=True) + EPS) * (d ** -0.5)
        kn = kh * lax.rsqrt(jnp.sum(kh * kh, axis=-1, keepdims=True) + EPS)
        gcol = gc_full[:, sl]
        beta = beta_full[:, sl]
        edec = jnp.exp(gcol - gct[h:h + 1, :])
        kb = kn * beta
        egc = jnp.exp(gcol)
        lower = jnp.where(strict, _mm_nt(kb, kn) * edec, 0.0)
        tinv = _unit_lower_inverse(lower, row, col)
        sol = _mm(tinv, jnp.concatenate([vh * beta, kb * egc], axis=1))
        u, w = sol[:, :d], sol[:, d:]
        aqk = jnp.where(incl, _mm_nt(qn, kn) * edec, 0.0)
        qd = qn * egc
        glast = [gcol[c - 1:c, :], gcol[n - 1:n, :]]
        kd = kn * jnp.exp(jnp.concatenate([jnp.broadcast_to(glast[0], (c, d)),
                                           jnp.broadcast_to(glast[1], (c, d))], axis=0) - gcol)
        kdt = kd.T
        s0 = s_ref[h]
        v0 = u[:c] - _mm(w[:c], s0)
        s1 = s0 * jnp.exp(glast[0]) + _mm(kdt, jnp.concatenate([v0, zeros_half], axis=0))
        v1 = u[c:] - _mm(w[c:], s1)
        s_ref[h] = s1 * jnp.exp(glast[1]) + _mm(kdt, jnp.concatenate([zeros_half, v1], axis=0))
        o = (jnp.concatenate([_mm(qd[:c], s0), _mm(qd[c:], s1)], axis=0)
             + _mm(aqk, jnp.concatenate([v0, v1], axis=0)))
        o = o * lax.rsqrt(jnp.mean(o * o, axis=-1, keepdims=True) + EPS) * nw_ref[...]
        outs.append(o * _silu(z[:, sl]))
    o_ref[...] = jnp.concatenate(outs, axis=1)


def gated_deltanet_mixer(y3, small3, smallt, conv_w, dt_bias, a_log, norm_w):
    bsz, length, _ = y3.shape
    n = GDN_TILE
    hd = GDN_HEADS * GDN_D
    hidx = _halo_index(n)
    idx = np.arange(n)
    same = (idx[:, None] // GDN_CHUNK) == (idx[None, :] // GDN_CHUNK)
    tri = jnp.asarray(same & (idx[:, None] >= idx[None, :]), BF16)
    trit = jnp.asarray(same & (idx[:, None] <= idx[None, :]), BF16)
    lanes = np.arange(LANES)[:, None]
    heads = np.arange(hd)[None, :] // GDN_D
    eg = jnp.asarray(lanes == heads, BF16)
    eb = jnp.asarray(lanes == heads + GDN_HEADS, BF16)
    const = lambda a: pl.BlockSpec(a.shape, lambda b, c: (0,) * a.ndim)
    args = [conv_w, _pad_lanes(dt_bias), _pad_col(dt_bias), _pad_lanes(a_log), _pad_col(a_log),
            norm_w.reshape(1, -1), tri, trit, eg, eb]
    return pl.pallas_call(
        _gdn_kernel,
        grid=(bsz, length // n),
        in_specs=[
            pl.BlockSpec((None, n, 3 * hd), lambda b, c: (b, c, 0)),
            pl.BlockSpec((None, HALO, 3 * hd), lambda b, c: (b, hidx(c), 0)),
            pl.BlockSpec((None, n, hd), lambda b, c: (b, c, 3)),
            pl.BlockSpec((None, n, LANES), lambda b, c: (b, c, 0)),
            pl.BlockSpec((None, 16, n), lambda b, c: (b, 0, c)),
        ] + [const(a) for a in args],
        out_specs=pl.BlockSpec((None, n, hd), lambda b, c: (b, c, 0)),
        out_shape=jax.ShapeDtypeStruct((bsz, length, hd), F32),
        scratch_shapes=[pltpu.VMEM((n + HALO, 3 * hd), F32), pltpu.VMEM((GDN_HEADS, GDN_D, GDN_D), F32)],
        compiler_params=_cparams("parallel", "arbitrary"),
        name="gated_deltanet",
    )(y3, y3, y3, small3, smallt, *args)


def _sb_kernel(q_ref, k_ref, v_ref, upper_ref, o_ref):
    blk = SB_BLOCK
    i = pl.program_id(2)
    q2 = q_ref[...] * (SB_HEAD_DIM ** -0.5)
    lane = lax.broadcasted_iota(jnp.int32, (blk, 2 * SB_HEAD_DIM), 1)
    first_head = lane < SB_HEAD_DIM
    qs = (jnp.where(first_head, q2, 0.0).astype(BF16), jnp.where(first_head, 0.0, q2).astype(BF16))
    row = lax.broadcasted_iota(jnp.int32, (blk, blk), 0)
    col = lax.broadcasted_iota(jnp.int32, (blk, blk), 1)
    earlier = col < row
    upper = upper_ref[...]

    def key_block(kb, carry, diagonal):
        start = pl.multiple_of(kb * blk, blk)
        k = k_ref[pl.ds(start, blk), :].astype(BF16)
        v = v_ref[pl.ds(start, blk), :].astype(BF16)
        new = []
        for qh, (acc, stick) in zip(qs, carry):
            logits = lax.dot_general(qh, k, (((1,), (1,)), ((), ())), preferred_element_type=F32)
            log_keep = -_softplus(logits)
            if diagonal:
                log_keep = jnp.where(earlier, log_keep, 0.0)
            between = _mm_sel_rhs(log_keep, upper, 2) + stick
            p = jnp.exp(logits + log_keep + between)
            if diagonal:
                p = jnp.where(earlier, p, 0.0)
            acc = acc + jnp.dot(p.astype(BF16), v, preferred_element_type=F32)
            stick = stick + jnp.sum(log_keep, axis=-1, keepdims=True)
            new.append((acc, stick))
        return tuple(new)

    zero = (jnp.zeros((blk, 2 * SB_HEAD_DIM), F32), jnp.zeros((blk, 1), F32))
    carry = key_block(i, (zero, zero), True)

    def alive(state):
        kb, carry = state
        longest = jnp.max(jnp.maximum(carry[0][1], carry[1][1]))
        return (kb >= 0) & (longest > SB_LOG_ZERO)

    def body(state):
        kb, carry = state
        return kb - 1, key_block(kb, carry, False)

    _, carry = lax.while_loop(alive, body, (i - 1, carry))
    o_ref[...] = jnp.where(first_head, carry[0][0], carry[1][0])


def stick_breaking_mixer(y3, col0):
    bsz, length, _ = y3.shape
    blk = SB_BLOCK
    pairs = SB_HEADS // 2
    idx = np.arange(blk)
    upper = jnp.asarray(idx[:, None] > idx[None, :], BF16)
    return pl.pallas_call(
        _sb_kernel,
        grid=(bsz, pairs, length // blk),
        in_specs=[
            pl.BlockSpec((None, blk, 2 * SB_HEAD_DIM), lambda b, p, i: (b, i, col0 + p)),
            pl.BlockSpec((None, length, 2 * SB_HEAD_DIM), lambda b, p, i: (b, 0, col0 + pairs + p)),
            pl.BlockSpec((None, length, 2 * SB_HEAD_DIM), lambda b, p, i: (b, 0, col0 + 2 * pairs + p)),
            pl.BlockSpec((blk, blk), lambda b, p, i: (0, 0)),
        ],
        out_specs=pl.BlockSpec((None, blk, 2 * SB_HEAD_DIM), lambda b, p, i: (b, i, p)),
        out_shape=jax.ShapeDtypeStruct((bsz, length, SB_DIM), F32),
        compiler_params=_cparams("parallel", "parallel", "arbitrary"),
        name="stick_breaking",
    )(y3, y3, y3, upper)


def _proj_res_kernel(a_ref, b_ref, h_ref, wa_ref, wb_ref, o_ref):
    o_ref[...] = h_ref[...] + (jnp.dot(a_ref[...].astype(BF16), wa_ref[...], preferred_element_type=F32)
                               + jnp.dot(b_ref[...].astype(BF16), wb_ref[...], preferred_element_type=F32))


def proj_residual(ya, yb, h, wa, wb, tm=512):
    m, d = h.shape
    tm = min(tm, m)
    return pl.pallas_call(
        _proj_res_kernel,
        grid=(m // tm,),
        in_specs=[
            pl.BlockSpec((tm, ya.shape[1]), lambda i: (i, 0)),
            pl.BlockSpec((tm, yb.shape[1]), lambda i: (i, 0)),
            pl.BlockSpec((tm, d), lambda i: (i, 0)),
            pl.BlockSpec(wa.shape, lambda i: (0, 0)),
            pl.BlockSpec(wb.shape, lambda i: (0, 0)),
        ],
        out_specs=pl.BlockSpec((tm, d), lambda i: (i, 0)),
        out_shape=jax.ShapeDtypeStruct((m, d), F32),
        compiler_params=_cparams("parallel"),
        name="proj_residual",
    )(ya, yb, h, wa, wb)


def _xattn_kernel(h_ref, g_ref, wq_ref, kt_ref, v_ref, wo_ref, o_ref):
    h = h_ref[...]
    u = _rms(h, g_ref[...]).astype(BF16)
    q = jnp.dot(u, wq_ref[...], preferred_element_type=F32)
    heads = []
    for hd in range(XA_HEADS):
        sl = slice(hd * XA_HEAD_DIM, (hd + 1) * XA_HEAD_DIM)
        s = jnp.dot(q[:, sl].astype(BF16), kt_ref[sl, :], preferred_element_type=F32) * (XA_HEAD_DIM ** -0.5)
        p = jnp.exp(s - jnp.max(s, axis=-1, keepdims=True))
        p = p / jnp.sum(p, axis=-1, keepdims=True)
        heads.append(jnp.dot(p.astype(BF16), v_ref[:, sl], preferred_element_type=F32))
    o = jnp.concatenate(heads, axis=1).astype(BF16)
    o_ref[...] = h + jnp.dot(o, wo_ref[...], preferred_element_type=F32)


def cross_attention_residual(h3, g, wq, kt, v, wo, tl=512):
    bsz, length, d = h3.shape
    tl = min(tl, length)
    return pl.pallas_call(
        _xattn_kernel,
        grid=(bsz, length // tl),
        in_specs=[
            pl.BlockSpec((None, tl, d), lambda b, i: (b, i, 0)),
            pl.BlockSpec((1, d), lambda b, i: (0, 0)),
            pl.BlockSpec((d, d), lambda b, i: (0, 0)),
            pl.BlockSpec((None, d, MEM_LEN), lambda b, i: (b, 0, 0)),
            pl.BlockSpec((None, MEM_LEN, d), lambda b, i: (b, 0, 0)),
            pl.BlockSpec((d, d), lambda b, i: (0, 0)),
        ],
        out_specs=pl.BlockSpec((None, tl, d), lambda b, i: (b, i, 0)),
        out_shape=jax.ShapeDtypeStruct((bsz, length, d), F32),
        compiler_params=_cparams("parallel", "parallel"),
        name="cross_attention",
    )(h3, g.reshape(1, d), wq, kt, v, wo)


def _router_kernel(h_ref, g_ref, whi_ref, wlo_ref, b_ref, xn_ref, r_ref):
    xn = _rms(h_ref[...], g_ref[...])
    x_hi = xn.astype(BF16)
    xn_ref[...] = x_hi
    x_lo = (xn - x_hi.astype(F32)).astype(BF16)
    logits = (jnp.dot(x_hi, whi_ref[...], preferred_element_type=F32)
              + jnp.dot(x_lo, whi_ref[...], preferred_element_type=F32)
              + jnp.dot(x_hi, wlo_ref[...], preferred_element_type=F32) + b_ref[...])
    lane = lax.broadcasted_iota(jnp.int32, logits.shape, 1).astype(F32)
    neg = -1e30
    none = float(LANES)

    def top(vals):
        best = jnp.max(vals, axis=-1, keepdims=True)
        where = jnp.min(jnp.where(vals == best, lane, none), axis=-1, keepdims=True)
        return best, where

    gl = jnp.where(lane < MOE_GROUPS, logits, neg)
    gbest, gsel = top(gl)
    gprob = 1.0 / jnp.sum(jnp.exp(gl - gbest), axis=-1, keepdims=True)
    lo = MOE_GROUPS + gsel * MOE_PER_GROUP
    el = jnp.where((lane >= lo) & (lane < lo + MOE_PER_GROUP), logits, neg)
    m1, i1 = top(el)
    m2, i2 = top(jnp.where(lane == i1, neg, el))
    e = jnp.exp(m2 - m1)
    gate1 = gprob / (1.0 + e)
    gate2 = gprob * e / (1.0 + e)
    r_ref[...] = jnp.where(lane == 0, i1 - MOE_GROUPS,
                           jnp.where(lane == 1, i2 - MOE_GROUPS,
                                     jnp.where(lane == 2, gate1, jnp.where(lane == 3, gate2, 0.0))))


def moe_router(h, g, w_hi, w_lo, bias, tm=512):
    m, d = h.shape
    tm = min(tm, m)
    return pl.pallas_call(
        _router_kernel,
        grid=(m // tm,),
        in_specs=[
            pl.BlockSpec((tm, d), lambda i: (i, 0)),
            pl.BlockSpec((1, d), lambda i: (0, 0)),
            pl.BlockSpec((d, LANES), lambda i: (0, 0)),
            pl.BlockSpec((d, LANES), lambda i: (0, 0)),
            pl.BlockSpec((1, LANES), lambda i: (0, 0)),
        ],
        out_specs=[pl.BlockSpec((tm, d), lambda i: (i, 0)), pl.BlockSpec((tm, LANES), lambda i: (i, 0))],
        out_shape=[jax.ShapeDtypeStruct((m, d), BF16), jax.ShapeDtypeStruct((m, LANES), F32)],
        compiler_params=_cparams("parallel"),
        name="moe_router",
    )(h, g.reshape(1, d), w_hi, w_lo, bias)


def _expert_kernel(beid_ref, nused_ref, x_ref, sw_ref, wg_ref, wu_ref, wd_ref, o_ref, wgb_ref, wub_ref, wdb_ref):
    i = pl.program_id(0)
    changed = (i == 0) | (beid_ref[i] != beid_ref[jnp.maximum(i - 1, 0)])

    @pl.when(changed)
    def _():
        wgb_ref[...] = wg_ref[...].astype(BF16)
        wub_ref[...] = wu_ref[...].astype(BF16)
        wdb_ref[...] = wd_ref[...].astype(BF16)

    @pl.when(i < nused_ref[0])
    def _():
        x = x_ref[...]
        gate = jnp.dot(x, wgb_ref[...], preferred_element_type=F32)
        up = jnp.dot(x, wub_ref[...], preferred_element_type=F32)
        act = (_silu(gate) * up).astype(BF16)
        o_ref[...] = jnp.dot(act, wdb_ref[...], preferred_element_type=F32) * sw_ref[...]

    @pl.when(i >= nused_ref[0])
    def _():
        o_ref[...] = jnp.zeros_like(o_ref)


def moe_experts(block_eid, n_used, xs, slot_w, w_gate, w_up, w_down):
    n_slots, d = xs.shape
    rows = MOE_ROWS
    ff = w_gate.shape[2]
    grid_spec = pltpu.PrefetchScalarGridSpec(
        num_scalar_prefetch=2,
        grid=(n_slots // rows,),
        in_specs=[
            pl.BlockSpec((rows, d), lambda i, be, nu: (i, 0)),
            pl.BlockSpec((rows, 1), lambda i, be, nu: (i, 0)),
            pl.BlockSpec((None, d, ff), lambda i, be, nu: (be[i], 0, 0)),
            pl.BlockSpec((None, d, ff), lambda i, be, nu: (be[i], 0, 0)),
            pl.BlockSpec((None, ff, d), lambda i, be, nu: (be[i], 0, 0)),
        ],
        out_specs=pl.BlockSpec((rows, d), lambda i, be, nu: (i, 0)),
        scratch_shapes=[pltpu.VMEM((d, ff), BF16), pltpu.VMEM((d, ff), BF16), pltpu.VMEM((ff, d), BF16)],
    )
    return pl.pallas_call(
        _expert_kernel,
        grid_spec=grid_spec,
        out_shape=jax.ShapeDtypeStruct((n_slots, d), F32),
        compiler_params=_cparams("arbitrary"),
        name="moe_experts",
    )(block_eid, n_used, xs, slot_w, w_gate, w_up, w_down)


def _combine_kernel(h_ref, y0_ref, y1_ref, g_ref, o_ref, *, final_norm):
    h = h_ref[...] + (y0_ref[...] + y1_ref[...])
    o_ref[...] = _rms(h, g_ref[...]) if final_norm else h


def moe_combine(h, y0, y1, g, final_norm, tm=512):
    m, d = h.shape
    tm = min(tm, m)
    spec = pl.BlockSpec((tm, d), lambda i: (i, 0))
    return pl.pallas_call(
        functools.partial(_combine_kernel, final_norm=final_norm),
        grid=(m // tm,),
        in_specs=[spec, spec, spec, pl.BlockSpec((1, d), lambda i: (0, 0))],
        out_specs=spec,
        out_shape=jax.ShapeDtypeStruct((m, d), F32),
        compiler_params=_cparams("parallel"),
        name="moe_combine",
    )(h, y0, y1, g.reshape(1, d))


def _pad_cols(w):
    return jnp.pad(w, ((0, 0), (0, LANES - w.shape[1])))


def _dispatch(route, n_tok):
    rows = MOE_ROWS
    eid = route[:, 0:2].astype(jnp.int32).reshape(-1)
    gate = route[:, 2:4].reshape(-1)
    n_assign = eid.shape[0]
    onehot = (eid[:, None] == jnp.arange(MOE_EXPERTS, dtype=jnp.int32)[None, :]).astype(jnp.int32)
    rank = jnp.take_along_axis(jnp.cumsum(onehot, axis=0), eid[:, None], axis=1)[:, 0] - 1
    counts = jnp.sum(onehot, axis=0)
    padded = (counts + rows - 1) // rows * rows
    pad_end = jnp.cumsum(padded)
    dest = (pad_end - padded)[eid] + rank
    n_blocks = -(-(n_assign + MOE_EXPERTS * (rows - 1)) // rows)
    n_slots = n_blocks * rows
    tok = jnp.arange(n_assign, dtype=jnp.int32) // 2
    slot_tok = jnp.full((n_slots,), n_tok, jnp.int32).at[dest].set(tok)
    slot_w = jnp.zeros((n_slots,), F32).at[dest].set(gate)
    block_eid = jnp.minimum(
        jnp.searchsorted(pad_end, jnp.arange(n_blocks, dtype=jnp.int32) * rows, side='right'),
        MOE_EXPERTS - 1).astype(jnp.int32)
    n_used = (pad_end[-1] // rows).astype(jnp.int32).reshape(1)
    return slot_tok, slot_w.reshape(n_slots, 1), block_eid, n_used, dest.reshape(n_tok, 2)


def _moe_layer(h, norm_g, w_group, b_group, w_expert, b_expert, w_gate, w_up, w_down, final_g):
    n_tok, d = h.shape
    w_r = _pad_cols(jnp.concatenate([w_group, w_expert], axis=1))
    w_hi = w_r.astype(BF16)
    w_lo = (w_r - w_hi.astype(F32)).astype(BF16)
    bias = _pad_lanes(jnp.concatenate([b_group, b_expert]))
    xn, route = moe_router(h, norm_g, w_hi, w_lo, bias)
    slot_tok, slot_w, block_eid, n_used, pos = _dispatch(route, n_tok)
    xs = jnp.concatenate([xn, jnp.zeros((1, d), xn.dtype)], axis=0)[slot_tok]
    ys = moe_experts(block_eid, n_used, xs, slot_w, w_gate, w_up, w_down)
    g = norm_g if final_g is None else final_g
    return moe_combine(h, ys[pos[:, 0]], ys[pos[:, 1]], g, final_g is not None)


def _memory_kv(memn_in, mem_norm, wk, wv):
    bsz, m, d = memn_in.shape
    w = jnp.concatenate([wk, wv], axis=1).astype(BF16)
    kv, _ = rms_matmul(memn_in.reshape(bsz * m, d), mem_norm, w, jnp.zeros((d, LANES), BF16))
    k = kv[:, :d].reshape(bsz, m, d)
    v = kv[:, d:].reshape(bsz, m, d)
    return jnp.swapaxes(k, 1, 2).astype(BF16), v.astype(BF16)


def kernel(x, mem, mem_norm, final_norm, norm_mix, norm_xa, norm_ffn, xa_wq, xa_wk, xa_wv, xa_wo, moe_w_group, moe_b_group, moe_w_expert, moe_b_expert, moe_w_gate, moe_w_up, moe_w_down, ev_w_in, ev_sc_conv, ev_ssm_conv_w, ev_ssm_conv_b, ev_ssm_dt_bias, ev_ssm_a_log, ev_ssm_d, ev_ssm_norm, ev_w_out, od_w_in, od_gdn_conv, od_gdn_dt_bias, od_gdn_a_log, od_gdn_norm, od_w_out):
    bsz, length, d = x.shape
    n_tok = bsz * length
    depth = norm_mix.shape[0]
    h = x.reshape(n_tok, d)
    for layer in range(depth):
        i = layer // 2
        if layer % 2 == 0:
            w = ev_w_in[i]
            w_main = jnp.concatenate([w[:, :3 * SC_DIM], w[:, 3 * SC_DIM + SSM_INNER:3 * SC_DIM + SSM_INNER + SSM_XBC],
                                      w[:, 3 * SC_DIM:3 * SC_DIM + SSM_INNER]], axis=1).astype(BF16)
            w_small = _pad_cols(w[:, 3 * SC_DIM + SSM_INNER + SSM_XBC:]).astype(BF16)
            y, small = rms_matmul(h, norm_mix[layer], w_main, w_small)
            y3 = y.reshape(bsz, length, -1)
            small3 = small.reshape(bsz, length, LANES)
            smallt = jnp.swapaxes(small3[:, :, :16], 1, 2)
            ya = short_conv_mixer(y3, ev_sc_conv[i])
            yb = ssd_mixer(y3, small3, smallt, ev_ssm_conv_w[i], ev_ssm_conv_b[i], ev_ssm_dt_bias[i],
                           ev_ssm_a_log[i], ev_ssm_d[i], ev_ssm_norm[i])
            w_out = ev_w_out[i].astype(BF16)
            split = SC_DIM
        else:
            w = od_w_in[i]
            qkvz = 4 * GDN_HEADS * GDN_D
            w_main = jnp.concatenate([w[:, :qkvz], w[:, qkvz + 2 * GDN_HEADS:]], axis=1).astype(BF16)
            w_small = _pad_cols(w[:, qkvz:qkvz + 2 * GDN_HEADS]).astype(BF16)
            y, small = rms_matmul(h, norm_mix[layer], w_main, w_small)
            y3 = y.reshape(bsz, length, -1)
            small3 = small.reshape(bsz, length, LANES)
            smallt = jnp.swapaxes(small3[:, :, :16], 1, 2)
            ya = gated_deltanet_mixer(y3, small3, smallt, od_gdn_conv[i], od_gdn_dt_bias[i], od_gdn_a_log[i],
                                      od_gdn_norm[i])
            yb = stick_breaking_mixer(y3, qkvz // LANES)
            w_out = od_w_out[i].astype(BF16)
            split = GDN_HEADS * GDN_D
        h = proj_residual(ya.reshape(n_tok, -1), yb.reshape(n_tok, -1), h, w_out[:split], w_out[split:])
        kt, v = _memory_kv(mem, mem_norm, xa_wk[layer], xa_wv[layer])
        h = cross_attention_residual(h.reshape(bsz, length, d), norm_xa[layer], xa_wq[layer].astype(BF16), kt, v,
                                     xa_wo[layer].astype(BF16)).reshape(n_tok, d)
        h = _moe_layer(h, norm_ffn[layer], moe_w_group[layer], moe_b_group[layer], moe_w_expert[layer],
                       moe_b_expert[layer], moe_w_gate[layer], moe_w_up[layer], moe_w_down[layer],
                       final_norm if layer == depth - 1 else None)
    return h.reshape(bsz, length, d)
```

```python
import functools

import jax
import jax.numpy as jnp
import numpy as np
from jax import lax
from jax.experimental import pallas as pl
from jax.experimental.pallas import tpu as pltpu

F32 = jnp.float32
BF16 = jnp.bfloat16
EPS = 1e-6

D_MODEL = 1024
MEM_LEN = 256
SC_DIM = 512
SSM_HEADS = 16
SSM_HEAD_DIM = 64
SSM_INNER = 1024
SSM_GROUPS = 2
SSM_STATE = 128
SSM_XBC = SSM_INNER + 2 * SSM_GROUPS * SSM_STATE
SSD_CHUNK = 128
GDN_HEADS = 8
GDN_D = 128
GDN_CHUNK = 64
GDN_TILE = 128
SB_HEADS = 8
SB_HEAD_DIM = 64
SB_DIM = 512
SB_BLOCK = 128
SB_STEP_HEADS = 4
XA_HEADS = 4
XA_HEAD_DIM = 256
MOE_GROUPS = 4
MOE_PER_GROUP = 8
MOE_EXPERTS = 32
MOE_FF = 512
MOE_ROWS = 256
HALO = 8
LANES = 128
SB_LOG_ZERO = -104.0
VMEM_LIMIT = 56 * 1024 * 1024


def _cparams(*sem):
    return pltpu.CompilerParams(dimension_semantics=sem, vmem_limit_bytes=VMEM_LIMIT)


def _mm(a, b):
    return jnp.dot(a.astype(BF16), b.astype(BF16), preferred_element_type=F32)


def _mm_nt(a, b):
    return lax.dot_general(a.astype(BF16), b.astype(BF16), (((1,), (1,)), ((), ())),
                           preferred_element_type=F32)


def _split_bf16(x, n):
    parts, r = [], x
    for _ in range(n):
        p = r.astype(BF16)
        parts.append(p)
        r = r - p.astype(F32)
    return parts


def _mm_sel_rhs(x, sel, n=3):
    return sum(jnp.dot(p, sel, preferred_element_type=F32) for p in _split_bf16(x, n))


def _mm_sel_lhs(sel, x, n=3):
    return sum(jnp.dot(sel, p, preferred_element_type=F32) for p in _split_bf16(x, n))


def _silu(x):
    return x * jax.nn.sigmoid(x)


def _softplus(x):
    return jnp.maximum(x, 0.0) + jnp.log(1.0 + jnp.exp(-jnp.abs(x)))


def _rms(x, g):
    return x * lax.rsqrt(jnp.mean(x * x, axis=-1, keepdims=True) + EPS) * g


def _rms_matmul_kernel(x_ref, g_ref, w_ref, ws_ref, o_ref, os_ref, xn_ref):
    @pl.when(pl.program_id(1) == 0)
    def _():
        xn = _rms(x_ref[...], g_ref[...]).astype(BF16)
        xn_ref[...] = xn
        os_ref[...] = jnp.dot(xn, ws_ref[...], preferred_element_type=F32)

    o_ref[...] = jnp.dot(xn_ref[...], w_ref[...], preferred_element_type=F32)


def rms_matmul(x, g, w, ws, tm=512, tn=512):
    m, k = x.shape
    n = w.shape[1]
    tm = min(tm, m)
    return pl.pallas_call(
        _rms_matmul_kernel,
        grid=(m // tm, n // tn),
        in_specs=[
            pl.BlockSpec((tm, k), lambda i, j: (i, 0)),
            pl.BlockSpec((1, k), lambda i, j: (0, 0)),
            pl.BlockSpec((k, tn), lambda i, j: (0, j)),
            pl.BlockSpec((k, LANES), lambda i, j: (0, 0)),
        ],
        out_specs=[
            pl.BlockSpec((tm, tn), lambda i, j: (i, j)),
            pl.BlockSpec((tm, LANES), lambda i, j: (i, 0)),
        ],
        out_shape=[jax.ShapeDtypeStruct((m, n), F32), jax.ShapeDtypeStruct((m, LANES), F32)],
        scratch_shapes=[pltpu.VMEM((tm, k), BF16)],
        compiler_params=_cparams("parallel", "arbitrary"),
        name="rms_matmul",
    )(x, g.reshape(1, k), w, ws)


def _causal_conv(ext_ref, w_ref, rows):
    width = w_ref.shape[0]
    acc = None
    for j in range(width):
        start = HALO - (width - 1) + j
        term = w_ref[j:j + 1, :] * ext_ref[start:start + rows, :]
        acc = term if acc is None else acc + term
    return acc


def _halo_index(rows):
    step = rows // HALO
    return lambda i: jnp.maximum(i * step - 1, 0)


def _sc_kernel(b_ref, c_ref, x_ref, ch_ref, xh_ref, w_ref, o_ref, ext_ref):
    rows = o_ref.shape[0]
    first = pl.program_id(1) == 0
    ext_ref[0:HALO, :] = jnp.where(first, 0.0, ch_ref[...] * xh_ref[...])
    ext_ref[HALO:, :] = c_ref[...] * x_ref[...]
    o_ref[...] = b_ref[...] * _causal_conv(ext_ref, w_ref, rows)


def short_conv_mixer(y3, w, tl=512):
    bsz, length, _ = y3.shape
    tl = min(tl, length)
    hidx = _halo_index(tl)
    return pl.pallas_call(
        _sc_kernel,
        grid=(bsz, length // tl),
        in_specs=[
            pl.BlockSpec((None, tl, SC_DIM), lambda b, i: (b, i, 0)),
            pl.BlockSpec((None, tl, SC_DIM), lambda b, i: (b, i, 1)),
            pl.BlockSpec((None, tl, SC_DIM), lambda b, i: (b, i, 2)),
            pl.BlockSpec((None, HALO, SC_DIM), lambda b, i: (b, hidx(i), 1)),
            pl.BlockSpec((None, HALO, SC_DIM), lambda b, i: (b, hidx(i), 2)),
            pl.BlockSpec(w.shape, lambda b, i: (0, 0)),
        ],
        out_specs=pl.BlockSpec((None, tl, SC_DIM), lambda b, i: (b, i, 0)),
        out_shape=jax.ShapeDtypeStruct((bsz, length, SC_DIM), F32),
        scratch_shapes=[pltpu.VMEM((tl + HALO, SC_DIM), F32)],
        compiler_params=_cparams("parallel", "arbitrary"),
        name="short_conv_mixer",
    )(y3, y3, y3, y3, y3, w)


def _ssd_kernel(xbc_ref, halo_ref, z_ref, dt_ref, dtt_ref, cw_ref, cb_ref, dtb_r_ref, dtb_c_ref,
                alog_r_ref, alog_c_ref, d_ref, nw_ref, tri_ref, trit_ref, eh_ref, eq_ref,
                o_ref, ext_ref, s_ref):
    q = SSD_CHUNK
    hpg = SSM_HEADS // SSM_GROUPS
    gw = hpg * SSM_HEAD_DIM
    first = pl.program_id(1) == 0

    @pl.when(first)
    def _():
        s_ref[...] = jnp.zeros_like(s_ref)

    ext_ref[0:HALO, :] = jnp.where(first, 0.0, halo_ref[...])
    ext_ref[HALO:, :] = xbc_ref[...]
    xbc = _silu(_causal_conv(ext_ref, cw_ref, q) + cb_ref[...])
    xs = xbc[:, :SSM_INNER]
    bm = xbc[:, SSM_INNER:SSM_INNER + SSM_GROUPS * SSM_STATE]
    cm = xbc[:, SSM_INNER + SSM_GROUPS * SSM_STATE:]

    dt = _softplus(dt_ref[...] + dtb_r_ref[...])
    acs = _mm_sel_lhs(tri_ref[...], dt * -jnp.exp(alog_r_ref[...]))
    dtt = _softplus(dtt_ref[...] + dtb_c_ref[...])
    acst = _mm_sel_rhs(dtt * -jnp.exp(alog_c_ref[...]), trit_ref[...])
    dt_full = _mm_sel_rhs(dt, eh_ref[...])
    acs_full = _mm_sel_rhs(acs, eh_ref[...])
    acs_col = _mm_sel_rhs(acs, eq_ref[...])

    xdt = xs * dt_full
    acs_last = acs_full[q - 1:q, :]
    xw = xdt * jnp.exp(acs_last - acs_full)
    chunk_decay = jnp.exp(acs_last)

    row = lax.broadcasted_iota(jnp.int32, (q, q), 0)
    col = lax.broadcasted_iota(jnp.int32, (q, q), 1)
    causal = row >= col
    lane = lax.broadcasted_iota(jnp.int32, (q, 2 * SSM_HEAD_DIM), 1)

    y_diag, y_off = [], []
    for g in range(SSM_GROUPS):
        bm_g = bm[:, g * SSM_STATE:(g + 1) * SSM_STATE]
        cm_g = cm[:, g * SSM_STATE:(g + 1) * SSM_STATE]
        cb_g = _mm_nt(cm_g, bm_g)
        state = s_ref[g]
        y_off.append(_mm(cm_g, state))
        s_ref[g] = state * chunk_decay[:, g * gw:(g + 1) * gw] + _mm(bm_g.T, xw[:, g * gw:(g + 1) * gw])
        for pair in range(hpg // 2):
            h0 = g * hpg + 2 * pair
            xdt_pair = xdt[:, h0 * SSM_HEAD_DIM:(h0 + 2) * SSM_HEAD_DIM]
            outs = []
            for h in (h0, h0 + 1):
                seg = acs_col[:, h * q:(h + 1) * q] - acst[h:h + 1, :]
                decay = jnp.where(causal, jnp.exp(seg), 0.0)
                outs.append(_mm(cb_g * decay, xdt_pair))
            y_diag.append(jnp.where(lane < SSM_HEAD_DIM, outs[0], outs[1]))
    y = (jnp.concatenate(y_diag, axis=1) + jnp.concatenate(y_off, axis=1) * jnp.exp(acs_full)
         + xs * d_ref[...])
    y = y * _silu(z_ref[...])
    halves = []
    for g in range(SSM_GROUPS):
        yg = y[:, g * gw:(g + 1) * gw]
        halves.append(yg * lax.rsqrt(jnp.mean(yg * yg, axis=-1, keepdims=True) + EPS))
    o_ref[...] = jnp.concatenate(halves, axis=1) * nw_ref[...]


def _pad_lanes(v, fill=0.0):
    return jnp.pad(v.astype(F32), (0, LANES - v.shape[0]), constant_values=fill).reshape(1, LANES)


def _pad_col(v, rows=16):
    return jnp.pad(v.astype(F32), (0, rows - v.shape[0])).reshape(rows, 1)


def ssd_mixer(y3, small3, smallt, conv_w, conv_b, dt_bias, a_log, d_skip, norm_w):
    bsz, length, _ = y3.shape
    q = SSD_CHUNK
    hidx = _halo_index(q)
    tri = jnp.asarray(np.tril(np.ones((q, q), np.float32)), BF16)
    trit = jnp.asarray(np.triu(np.ones((q, q), np.float32)), BF16)
    heads = np.arange(LANES)[:, None]
    eh = jnp.asarray(heads == (np.arange(SSM_INNER)[None, :] // SSM_HEAD_DIM), BF16)
    eq = jnp.asarray(heads == (np.arange(SSM_HEADS * q)[None, :] // q), BF16)
    d_full = jnp.repeat(d_skip.astype(F32), SSM_HEAD_DIM).reshape(1, SSM_INNER)
    const = lambda a: pl.BlockSpec(a.shape, lambda b, c: (0,) * a.ndim)
    args = [conv_w, conv_b.reshape(1, -1), _pad_lanes(dt_bias), _pad_col(dt_bias), _pad_lanes(a_log),
            _pad_col(a_log), d_full, norm_w.reshape(1, -1), tri, trit, eh, eq]
    return pl.pallas_call(
        _ssd_kernel,
        grid=(bsz, length // q),
        in_specs=[
            pl.BlockSpec((None, q, SSM_XBC), lambda b, c: (b, c, 1)),
            pl.BlockSpec((None, HALO, SSM_XBC), lambda b, c: (b, hidx(c), 1)),
            pl.BlockSpec((None, q, SSM_INNER), lambda b, c: (b, c, 3)),
            pl.BlockSpec((None, q, LANES), lambda b, c: (b, c, 0)),
            pl.BlockSpec((None, 16, q), lambda b, c: (b, 0, c)),
        ] + [const(a) for a in args],
        out_specs=pl.BlockSpec((None, q, SSM_INNER), lambda b, c: (b, c, 0)),
        out_shape=jax.ShapeDtypeStruct((bsz, length, SSM_INNER), F32),
        scratch_shapes=[pltpu.VMEM((q + HALO, SSM_XBC), F32),
                        pltpu.VMEM((SSM_GROUPS, SSM_STATE, SSM_INNER // SSM_GROUPS), F32)],
        compiler_params=_cparams("parallel", "arbitrary"),
        name="ssd_mixer",
    )(y3, y3, y3, small3, smallt, *args)


def _unit_lower_inverse(mats, row, col):
    eye = jnp.where(row == col, 1.0, 0.0)
    blk = lambda n: (row >> (n.bit_length() - 1)) == (col >> (n.bit_length() - 1))
    p = [jnp.where(blk(16), -a, 0.0) for a in mats]
    t = [eye + x for x in p]
    for _ in range(3):
        p = [_mm(x, x) for x in p]
        t = [y + _mm(y, x) for y, x in zip(t, p)]
    for n in (16, 32):
        band = blk(2 * n) & jnp.logical_not(blk(n))
        left = [_mm(y, jnp.where(band, a, 0.0)) for y, a in zip(t, mats)]
        t = [y - _mm(x, y) for y, x in zip(t, left)]
    return t


def _gdn_kernel(qkv_ref, halo_ref, z_ref, ab_ref, abt_ref, cw_ref, dtb_r_ref, dtb_c_ref, alog_r_ref,
                alog_c_ref, nw_ref, tri_ref, trit_ref, eg_ref, eb_ref, o_ref, ext_ref, s_ref):
    n = GDN_TILE
    c = GDN_CHUNK
    d = GDN_D
    hd = GDN_HEADS * d
    first = pl.program_id(1) == 0

    @pl.when(first)
    def _():
        s_ref[...] = jnp.zeros_like(s_ref)

    ext_ref[0:HALO, :] = jnp.where(first, 0.0, halo_ref[...])
    ext_ref[HALO:, :] = qkv_ref[...]
    qkv = _silu(_causal_conv(ext_ref, cw_ref, n))
    z = z_ref[...]

    ab = ab_ref[...]
    g = -jnp.exp(alog_r_ref[...]) * _softplus(ab + dtb_r_ref[...])
    gc_full = _mm_sel_rhs(_mm_sel_lhs(tri_ref[...], g), eg_ref[...])
    beta_full = _mm_sel_rhs(jax.nn.sigmoid(ab), eb_ref[...])
    gt = -jnp.exp(alog_c_ref[...]) * _softplus(abt_ref[...] + dtb_c_ref[...])
    gct = _mm_sel_rhs(gt, trit_ref[...])

    row = lax.broadcasted_iota(jnp.int32, (n, n), 0)
    col = lax.broadcasted_iota(jnp.int32, (n, n), 1)
    same = (row >> (c.bit_length() - 1)) == (col >> (c.bit_length() - 1))
    incl = same & (row >= col)
    strict = same & (row > col)
    zeros_half = jnp.zeros((c, d), F32)

    heads = range(GDN_HEADS)
    sl = [slice(h * d, (h + 1) * d) for h in heads]
    l2n = lambda x: x * lax.rsqrt(jnp.sum(x * x, axis=-1, keepdims=True) + EPS)
    qn = [l2n(qkv[:, sl[h]]) * (d ** -0.5) for h in heads]
    kn = [l2n(qkv[:, hd + h * d:hd + (h + 1) * d]) for h in heads]
    vh = [qkv[:, 2 * hd + h * d:2 * hd + (h + 1) * d] for h in heads]
    gcol = [gc_full[:, sl[h]] for h in heads]
    beta = [beta_full[:, sl[h]] for h in heads]
    edec = [jnp.exp(gcol[h] - gct[h:h + 1, :]) for h in heads]
    egc = [jnp.exp(x) for x in gcol]
    kb = [kn[h] * beta[h] for h in heads]
    lower = [jnp.where(strict, _mm_nt(kb[h], kn[h]) * edec[h], 0.0) for h in heads]
    aqk = [jnp.where(incl, _mm_nt(qn[h], kn[h]) * edec[h], 0.0) for h in heads]
    tinv = _unit_lower_inverse(lower, row, col)
    sol = [_mm(tinv[h], jnp.concatenate([vh[h] * beta[h], kb[h] * egc[h]], axis=1)) for h in heads]
    qd = [qn[h] * egc[h] for h in heads]
    glast = [(gcol[h][c - 1:c, :], gcol[h][n - 1:n, :]) for h in heads]
    kdt = [(kn[h] * jnp.exp(jnp.concatenate([jnp.broadcast_to(glast[h][0], (c, d)),
                                             jnp.broadcast_to(glast[h][1], (c, d))], axis=0) - gcol[h])).T
           for h in heads]
    s0 = [s_ref[h] for h in heads]
    v0 = [sol[h][:c, :d] - _mm(sol[h][:c, d:], s0[h]) for h in heads]
    s1 = [s0[h] * jnp.exp(glast[h][0]) + _mm(kdt[h], jnp.concatenate([v0[h], zeros_half], axis=0)) for h in heads]
    v1 = [sol[h][c:, :d] - _mm(sol[h][c:, d:], s1[h]) for h in heads]
    for h in heads:
        s_ref[h] = s1[h] * jnp.exp(glast[h][1]) + _mm(kdt[h], jnp.concatenate([zeros_half, v1[h]], axis=0))
    outs = []
    for h in heads:
        o = (jnp.concatenate([_mm(qd[h][:c], s0[h]), _mm(qd[h][c:], s1[h])], axis=0)
             + _mm(aqk[h], jnp.concatenate([v0[h], v1[h]], axis=0)))
        o = o * lax.rsqrt(jnp.mean(o * o, axis=-1, keepdims=True) + EPS) * nw_ref[...]
        outs.append(o * _silu(z[:, sl[h]]))
    o_ref[...] = jnp.concatenate(outs, axis=1)


def gated_deltanet_mixer(y3, small3, smallt, conv_w, dt_bias, a_log, norm_w):
    bsz, length, _ = y3.shape
    n = GDN_TILE
    hd = GDN_HEADS * GDN_D
    hidx = _halo_index(n)
    idx = np.arange(n)
    same = (idx[:, None] // GDN_CHUNK) == (idx[None, :] // GDN_CHUNK)
    tri = jnp.asarray(same & (idx[:, None] >= idx[None, :]), BF16)
    trit = jnp.asarray(same & (idx[:, None] <= idx[None, :]), BF16)
    lanes = np.arange(LANES)[:, None]
    heads = np.arange(hd)[None, :] // GDN_D
    eg = jnp.asarray(lanes == heads, BF16)
    eb = jnp.asarray(lanes == heads + GDN_HEADS, BF16)
    const = lambda a: pl.BlockSpec(a.shape, lambda b, c: (0,) * a.ndim)
    args = [conv_w, _pad_lanes(dt_bias), _pad_col(dt_bias), _pad_lanes(a_log), _pad_col(a_log),
            norm_w.reshape(1, -1), tri, trit, eg, eb]
    return pl.pallas_call(
        _gdn_kernel,
        grid=(bsz, length // n),
        in_specs=[
            pl.BlockSpec((None, n, 3 * hd), lambda b, c: (b, c, 0)),
            pl.BlockSpec((None, HALO, 3 * hd), lambda b, c: (b, hidx(c), 0)),
            pl.BlockSpec((None, n, hd), lambda b, c: (b, c, 3)),
            pl.BlockSpec((None, n, LANES), lambda b, c: (b, c, 0)),
            pl.BlockSpec((None, 16, n), lambda b, c: (b, 0, c)),
        ] + [const(a) for a in args],
        out_specs=pl.BlockSpec((None, n, hd), lambda b, c: (b, c, 0)),
        out_shape=jax.ShapeDtypeStruct((bsz, length, hd), F32),
        scratch_shapes=[pltpu.VMEM((n + HALO, 3 * hd), F32), pltpu.VMEM((GDN_HEADS, GDN_D, GDN_D), F32)],
        compiler_params=_cparams("parallel", "arbitrary"),
        name="gated_deltanet",
    )(y3, y3, y3, small3, smallt, *args)


def _sb_kernel(q_ref, k_ref, v_ref, upper_ref, o_ref):
    blk = SB_BLOCK
    pair_w = 2 * SB_HEAD_DIM
    n_pairs = SB_STEP_HEADS // 2
    i = pl.program_id(2)
    q = q_ref[...] * (SB_HEAD_DIM ** -0.5)
    lane = lax.broadcasted_iota(jnp.int32, (blk, pair_w), 1)
    first_head = lane < SB_HEAD_DIM
    qs = []
    for p in range(n_pairs):
        q2 = q[:, p * pair_w:(p + 1) * pair_w]
        qs += [jnp.where(first_head, q2, 0.0).astype(BF16), jnp.where(first_head, 0.0, q2).astype(BF16)]
    row = lax.broadcasted_iota(jnp.int32, (blk, blk), 0)
    col = lax.broadcasted_iota(jnp.int32, (blk, blk), 1)
    earlier = col < row
    upper = upper_ref[...]
    heads = range(SB_STEP_HEADS)

    def key_block(kb, accs, sticks, diagonal):
        start = pl.multiple_of(kb * blk, blk)
        k = k_ref[pl.ds(start, blk), :]
        v = v_ref[pl.ds(start, blk), :]
        kp = [k[:, p * pair_w:(p + 1) * pair_w] for p in range(n_pairs)]
        vp = [v[:, p * pair_w:(p + 1) * pair_w] for p in range(n_pairs)]
        logits = [lax.dot_general(qs[h], kp[h // 2], (((1,), (1,)), ((), ())), preferred_element_type=F32)
                  for h in heads]
        log_keep = [-_softplus(x) for x in logits]
        if diagonal:
            log_keep = [jnp.where(earlier, x, 0.0) for x in log_keep]
        between = [_mm_sel_rhs(log_keep[h], upper, 2) + sticks[h] for h in heads]
        w = [jnp.exp(logits[h] + log_keep[h] + between[h]) for h in heads]
        if diagonal:
            w = [jnp.where(earlier, x, 0.0) for x in w]
        pv = [jnp.dot(w[h].astype(BF16), vp[h // 2], preferred_element_type=F32) for h in heads]
        accs = tuple(accs[p] + jnp.where(first_head, pv[2 * p], pv[2 * p + 1]) for p in range(n_pairs))
        sticks = tuple(sticks[h] + jnp.sum(log_keep[h], axis=-1, keepdims=True) for h in heads)
        return accs, sticks

    accs = tuple(jnp.zeros((blk, pair_w), F32) for _ in range(n_pairs))
    sticks = tuple(jnp.zeros((blk, 1), F32) for _ in heads)
    accs, sticks = key_block(i, accs, sticks, True)

    def alive(state):
        kb, _, sticks = state
        longest = sticks[0]
        for s in sticks[1:]:
            longest = jnp.maximum(longest, s)
        return (kb >= 0) & (jnp.max(longest) > SB_LOG_ZERO)

    def body(state):
        kb, accs, sticks = state
        accs, sticks = key_block(kb, accs, sticks, False)
        return kb - 1, accs, sticks

    _, accs, _ = lax.while_loop(alive, body, (i - 1, accs, sticks))
    o_ref[...] = jnp.concatenate(accs, axis=1)


def stick_breaking_mixer(y3, col0):
    bsz, length, _ = y3.shape
    blk = SB_BLOCK
    step_w = SB_STEP_HEADS * SB_HEAD_DIM
    steps = SB_DIM // step_w
    k16 = y3[:, :, col0 + SB_DIM:col0 + 2 * SB_DIM].astype(BF16)
    v16 = y3[:, :, col0 + 2 * SB_DIM:col0 + 3 * SB_DIM].astype(BF16)
    idx = np.arange(blk)
    upper = jnp.asarray(idx[:, None] > idx[None, :], BF16)
    return pl.pallas_call(
        _sb_kernel,
        grid=(bsz, steps, length // blk),
        in_specs=[
            pl.BlockSpec((None, blk, step_w), lambda b, p, i: (b, i, col0 // step_w + p)),
            pl.BlockSpec((None, length, step_w), lambda b, p, i: (b, 0, p)),
            pl.BlockSpec((None, length, step_w), lambda b, p, i: (b, 0, p)),
            pl.BlockSpec((blk, blk), lambda b, p, i: (0, 0)),
        ],
        out_specs=pl.BlockSpec((None, blk, step_w), lambda b, p, i: (b, i, p)),
        out_shape=jax.ShapeDtypeStruct((bsz, length, SB_DIM), F32),
        compiler_params=_cparams("parallel", "parallel", "arbitrary"),
        name="stick_breaking",
    )(y3, k16, v16, upper)


def _proj_res_kernel(a_ref, b_ref, h_ref, wa_ref, wb_ref, o_ref):
    o_ref[...] = h_ref[...] + (jnp.dot(a_ref[...].astype(BF16), wa_ref[...], preferred_element_type=F32)
                               + jnp.dot(b_ref[...].astype(BF16), wb_ref[...], preferred_element_type=F32))


def proj_residual(ya, yb, h, wa, wb, tm=512):
    m, d = h.shape
    tm = min(tm, m)
    return pl.pallas_call(
        _proj_res_kernel,
        grid=(m // tm,),
        in_specs=[
            pl.BlockSpec((tm, ya.shape[1]), lambda i: (i, 0)),
            pl.BlockSpec((tm, yb.shape[1]), lambda i: (i, 0)),
            pl.BlockSpec((tm, d), lambda i: (i, 0)),
            pl.BlockSpec(wa.shape, lambda i: (0, 0)),
            pl.BlockSpec(wb.shape, lambda i: (0, 0)),
        ],
        out_specs=pl.BlockSpec((tm, d), lambda i: (i, 0)),
        out_shape=jax.ShapeDtypeStruct((m, d), F32),
        compiler_params=_cparams("parallel"),
        name="proj_residual",
    )(ya, yb, h, wa, wb)


def _xattn_kernel(h_ref, g_ref, wq_ref, kt_ref, v_ref, wo_ref, o_ref):
    h = h_ref[...]
    u = _rms(h, g_ref[...]).astype(BF16)
    q = jnp.dot(u, wq_ref[...], preferred_element_type=F32)
    heads = []
    for hd in range(XA_HEADS):
        sl = slice(hd * XA_HEAD_DIM, (hd + 1) * XA_HEAD_DIM)
        s = jnp.dot(q[:, sl].astype(BF16), kt_ref[sl, :], preferred_element_type=F32) * (XA_HEAD_DIM ** -0.5)
        p = jnp.exp(s - jnp.max(s, axis=-1, keepdims=True))
        p = p / jnp.sum(p, axis=-1, keepdims=True)
        heads.append(jnp.dot(p.astype(BF16), v_ref[:, sl], preferred_element_type=F32))
    o = jnp.concatenate(heads, axis=1).astype(BF16)
    o_ref[...] = h + jnp.dot(o, wo_ref[...], preferred_element_type=F32)


def cross_attention_residual(h3, g, wq, kt, v, wo, tl=512):
    bsz, length, d = h3.shape
    tl = min(tl, length)
    return pl.pallas_call(
        _xattn_kernel,
        grid=(bsz, length // tl),
        in_specs=[
            pl.BlockSpec((None, tl, d), lambda b, i: (b, i, 0)),
            pl.BlockSpec((1, d), lambda b, i: (0, 0)),
            pl.BlockSpec((d, d), lambda b, i: (0, 0)),
            pl.BlockSpec((None, d, MEM_LEN), lambda b, i: (b, 0, 0)),
            pl.BlockSpec((None, MEM_LEN, d), lambda b, i: (b, 0, 0)),
            pl.BlockSpec((d, d), lambda b, i: (0, 0)),
        ],
        out_specs=pl.BlockSpec((None, tl, d), lambda b, i: (b, i, 0)),
        out_shape=jax.ShapeDtypeStruct((bsz, length, d), F32),
        compiler_params=_cparams("parallel", "parallel"),
        name="cross_attention",
    )(h3, g.reshape(1, d), wq, kt, v, wo)


def _router_kernel(h_ref, g_ref, whi_ref, wlo_ref, b_ref, before_ref, xn_ref, r_ref, cnt_ref, run_ref):
    @pl.when(pl.program_id(0) == 0)
    def _():
        run_ref[...] = jnp.zeros_like(run_ref)

    xn = _rms(h_ref[...], g_ref[...])
    xn_ref[...] = xn
    x_hi = xn.astype(BF16)
    x_lo = (xn - x_hi.astype(F32)).astype(BF16)
    logits = (jnp.dot(x_hi, whi_ref[...], preferred_element_type=F32)
              + jnp.dot(x_lo, whi_ref[...], preferred_element_type=F32)
              + jnp.dot(x_hi, wlo_ref[...], preferred_element_type=F32) + b_ref[...])
    lane = lax.broadcasted_iota(jnp.int32, logits.shape, 1).astype(F32)
    neg = -1e30
    none = float(LANES)

    def top(vals):
        best = jnp.max(vals, axis=-1, keepdims=True)
        where = jnp.min(jnp.where(vals == best, lane, none), axis=-1, keepdims=True)
        return best, where

    gl = jnp.where(lane < MOE_GROUPS, logits, neg)
    gbest, gsel = top(gl)
    gprob = 1.0 / jnp.sum(jnp.exp(gl - gbest), axis=-1, keepdims=True)
    lo = MOE_GROUPS + gsel * MOE_PER_GROUP
    el = jnp.where((lane >= lo) & (lane < lo + MOE_PER_GROUP), logits, neg)
    m1, i1 = top(el)
    m2, i2 = top(jnp.where(lane == i1, neg, el))
    e = jnp.exp(m2 - m1)
    gate1 = gprob / (1.0 + e)
    gate2 = gprob * e / (1.0 + e)

    hot1 = lane == i1
    hot2 = lane == i2
    one1 = jnp.where(hot1, 1.0, 0.0)
    one2 = jnp.where(hot2, 1.0, 0.0)
    before = before_ref[...]
    prefix1 = jnp.dot(before, one1.astype(BF16), preferred_element_type=F32)
    prefix2 = jnp.dot(before, one2.astype(BF16), preferred_element_type=F32)
    total1 = jnp.sum(one1, axis=0, keepdims=True)
    running = run_ref[...]
    rank1 = jnp.sum(jnp.where(hot1, prefix1 + running, 0.0), axis=-1, keepdims=True)
    rank2 = jnp.sum(jnp.where(hot2, prefix2 + (running + total1), 0.0), axis=-1, keepdims=True)
    running = running + total1 + jnp.sum(one2, axis=0, keepdims=True)
    run_ref[...] = running
    cnt_ref[...] = running

    fields = (i1 - MOE_GROUPS, i2 - MOE_GROUPS, gate1, gate2, rank1, rank2)
    out = jnp.zeros_like(logits)
    for k, val in enumerate(fields):
        out = jnp.where(lane == k, val, out)
    r_ref[...] = out


def moe_router(h, g, w_hi, w_lo, bias, tm=512):
    m, d = h.shape
    tm = min(tm, m)
    idx = np.arange(tm)
    before = jnp.asarray(idx[:, None] > idx[None, :], BF16)
    return pl.pallas_call(
        _router_kernel,
        grid=(m // tm,),
        in_specs=[
            pl.BlockSpec((tm, d), lambda i: (i, 0)),
            pl.BlockSpec((1, d), lambda i: (0, 0)),
            pl.BlockSpec((d, LANES), lambda i: (0, 0)),
            pl.BlockSpec((d, LANES), lambda i: (0, 0)),
            pl.BlockSpec((1, LANES), lambda i: (0, 0)),
            pl.BlockSpec((tm, tm), lambda i: (0, 0)),
        ],
        out_specs=[pl.BlockSpec((tm, d), lambda i: (i, 0)), pl.BlockSpec((tm, LANES), lambda i: (i, 0)),
                   pl.BlockSpec((1, LANES), lambda i: (0, 0))],
        out_shape=[jax.ShapeDtypeStruct((m, d), F32), jax.ShapeDtypeStruct((m, LANES), F32),
                   jax.ShapeDtypeStruct((1, LANES), F32)],
        scratch_shapes=[pltpu.VMEM((1, LANES), F32)],
        compiler_params=_cparams("arbitrary"),
        name="moe_router",
    )(h, g.reshape(1, d), w_hi, w_lo, bias, before)


def _expert_kernel(beid_ref, nused_ref, x_ref, wg_ref, wu_ref, wd_ref, o_ref, wgb_ref, wub_ref, wdb_ref):
    i = pl.program_id(0)
    changed = (i == 0) | (beid_ref[i] != beid_ref[jnp.maximum(i - 1, 0)])

    @pl.when(changed)
    def _():
        wgb_ref[...] = wg_ref[...].astype(BF16)
        wub_ref[...] = wu_ref[...].astype(BF16)
        wdb_ref[...] = wd_ref[...].astype(BF16)

    @pl.when(i < nused_ref[0])
    def _():
        x = x_ref[...].astype(BF16)
        gate = jnp.dot(x, wgb_ref[...], preferred_element_type=F32)
        up = jnp.dot(x, wub_ref[...], preferred_element_type=F32)
        act = (_silu(gate) * up).astype(BF16)
        o_ref[...] = jnp.dot(act, wdb_ref[...], preferred_element_type=F32)

    @pl.when(i >= nused_ref[0])
    def _():
        o_ref[...] = jnp.zeros_like(o_ref)


def moe_experts(block_eid, n_used, xs, w_gate, w_up, w_down, layer):
    n_slots, d = xs.shape
    rows = MOE_ROWS
    ff = w_gate.shape[3]
    grid_spec = pltpu.PrefetchScalarGridSpec(
        num_scalar_prefetch=2,
        grid=(n_slots // rows,),
        in_specs=[
            pl.BlockSpec((rows, d), lambda i, be, nu: (i, 0)),
            pl.BlockSpec((None, None, d, ff), lambda i, be, nu: (layer, be[i], 0, 0)),
            pl.BlockSpec((None, None, d, ff), lambda i, be, nu: (layer, be[i], 0, 0)),
            pl.BlockSpec((None, None, ff, d), lambda i, be, nu: (layer, be[i], 0, 0)),
        ],
        out_specs=pl.BlockSpec((rows, d), lambda i, be, nu: (i, 0)),
        scratch_shapes=[pltpu.VMEM((d, ff), BF16), pltpu.VMEM((d, ff), BF16), pltpu.VMEM((ff, d), BF16)],
    )
    return pl.pallas_call(
        _expert_kernel,
        grid_spec=grid_spec,
        out_shape=jax.ShapeDtypeStruct((n_slots, d), F32),
        compiler_params=_cparams("arbitrary"),
        name="moe_experts",
    )(block_eid, n_used, xs, w_gate, w_up, w_down)


def _combine_kernel(h_ref, y0_ref, y1_ref, r_ref, g_ref, o_ref, *, final_norm):
    route = r_ref[...]
    h = h_ref[...] + (route[:, 2:3] * y0_ref[...] + route[:, 3:4] * y1_ref[...])
    o_ref[...] = _rms(h, g_ref[...]) if final_norm else h


def moe_combine(h, y0, y1, route, g, final_norm, tm=512):
    m, d = h.shape
    tm = min(tm, m)
    spec = pl.BlockSpec((tm, d), lambda i: (i, 0))
    return pl.pallas_call(
        functools.partial(_combine_kernel, final_norm=final_norm),
        grid=(m // tm,),
        in_specs=[spec, spec, spec, pl.BlockSpec((tm, LANES), lambda i: (i, 0)),
                  pl.BlockSpec((1, d), lambda i: (0, 0))],
        out_specs=spec,
        out_shape=jax.ShapeDtypeStruct((m, d), F32),
        compiler_params=_cparams("parallel"),
        name="moe_combine",
    )(h, y0, y1, route, g.reshape(1, d))


def _pad_cols(w):
    return jnp.pad(w, ((0, 0), (0, LANES - w.shape[1])))


def _dispatch(route, counts, n_tok):
    rows = MOE_ROWS
    n_assign = 2 * n_tok
    counts = counts[0, MOE_GROUPS:MOE_GROUPS + MOE_EXPERTS].astype(jnp.int32)
    padded = (counts + rows - 1) // rows * rows
    pad_end = jnp.cumsum(padded)
    pad_start = pad_end - padded
    eid = route[:, 0:2].astype(jnp.int32)
    rank = route[:, 4:6].astype(jnp.int32)
    hot = eid[:, :, None] == jnp.arange(MOE_EXPERTS, dtype=jnp.int32)
    dest = jnp.sum(jnp.where(hot, pad_start, 0), axis=-1) + rank
    n_blocks = -(-(n_assign + MOE_EXPERTS * (rows - 1)) // rows)
    tok = jnp.arange(n_assign, dtype=jnp.int32) // 2
    slot_tok = jnp.zeros((n_blocks * rows,), jnp.int32).at[dest.reshape(-1)].set(tok)
    block_start = jnp.arange(n_blocks, dtype=jnp.int32) * rows
    block_eid = jnp.minimum(jnp.sum(block_start[:, None] >= pad_end[None, :], axis=-1), MOE_EXPERTS - 1)
    n_used = (pad_end[-1] // rows).astype(jnp.int32).reshape(1)
    return slot_tok, block_eid.astype(jnp.int32), n_used, dest


def _moe_layer(h, norm_g, w_group, b_group, w_expert, b_expert, w_gate, w_up, w_down, layer, final_g):
    n_tok, d = h.shape
    w_r = _pad_cols(jnp.concatenate([w_group, w_expert], axis=1))
    w_hi = w_r.astype(BF16)
    w_lo = (w_r - w_hi.astype(F32)).astype(BF16)
    bias = _pad_lanes(jnp.concatenate([b_group, b_expert]))
    xn, route, counts = moe_router(h, norm_g, w_hi, w_lo, bias)
    slot_tok, block_eid, n_used, dest = _dispatch(route, counts, n_tok)
    ys = moe_experts(block_eid, n_used, xn[slot_tok], w_gate, w_up, w_down, layer)
    g = norm_g if final_g is None else final_g
    return moe_combine(h, ys[dest[:, 0]], ys[dest[:, 1]], route, g, final_g is not None)


def _memory_kv(memn_in, mem_norm, wk, wv):
    bsz, m, d = memn_in.shape
    w = jnp.concatenate([wk, wv], axis=1).astype(BF16)
    kv, _ = rms_matmul(memn_in.reshape(bsz * m, d), mem_norm, w, jnp.zeros((d, LANES), BF16))
    k = kv[:, :d].reshape(bsz, m, d)
    v = kv[:, d:].reshape(bsz, m, d)
    return jnp.swapaxes(k, 1, 2).astype(BF16), v.astype(BF16)


def kernel(x, mem, mem_norm, final_norm, norm_mix, norm_xa, norm_ffn, xa_wq, xa_wk, xa_wv, xa_wo, moe_w_group, moe_b_group, moe_w_expert, moe_b_expert, moe_w_gate, moe_w_up, moe_w_down, ev_w_in, ev_sc_conv, ev_ssm_conv_w, ev_ssm_conv_b, ev_ssm_dt_bias, ev_ssm_a_log, ev_ssm_d, ev_ssm_norm, ev_w_out, od_w_in, od_gdn_conv, od_gdn_dt_bias, od_gdn_a_log, od_gdn_norm, od_w_out):
    bsz, length, d = x.shape
    n_tok = bsz * length
    depth = norm_mix.shape[0]
    h = x.reshape(n_tok, d)
    for layer in range(depth):
        i = layer // 2
        if layer % 2 == 0:
            w = ev_w_in[i]
            w_main = jnp.concatenate([w[:, :3 * SC_DIM], w[:, 3 * SC_DIM + SSM_INNER:3 * SC_DIM + SSM_INNER + SSM_XBC],
                                      w[:, 3 * SC_DIM:3 * SC_DIM + SSM_INNER]], axis=1).astype(BF16)
            w_small = _pad_cols(w[:, 3 * SC_DIM + SSM_INNER + SSM_XBC:]).astype(BF16)
            y, small = rms_matmul(h, norm_mix[layer], w_main, w_small, tm=1024, tn=1024)
            y3 = y.reshape(bsz, length, -1)
            small3 = small.reshape(bsz, length, LANES)
            smallt = jnp.swapaxes(small3[:, :, :16], 1, 2)
            ya = short_conv_mixer(y3, ev_sc_conv[i])
            yb = ssd_mixer(y3, small3, smallt, ev_ssm_conv_w[i], ev_ssm_conv_b[i], ev_ssm_dt_bias[i],
                           ev_ssm_a_log[i], ev_ssm_d[i], ev_ssm_norm[i])
            w_out = ev_w_out[i].astype(BF16)
            split = SC_DIM
        else:
            w = od_w_in[i]
            qkvz = 4 * GDN_HEADS * GDN_D
            w_main = jnp.concatenate([w[:, :qkvz], w[:, qkvz + 2 * GDN_HEADS:]], axis=1).astype(BF16)
            w_small = _pad_cols(w[:, qkvz:qkvz + 2 * GDN_HEADS]).astype(BF16)
            y, small = rms_matmul(h, norm_mix[layer], w_main, w_small, tm=512, tn=2816)
            y3 = y.reshape(bsz, length, -1)
            small3 = small.reshape(bsz, length, LANES)
            smallt = jnp.swapaxes(small3[:, :, :16], 1, 2)
            ya = gated_deltanet_mixer(y3, small3, smallt, od_gdn_conv[i], od_gdn_dt_bias[i], od_gdn_a_log[i],
                                      od_gdn_norm[i])
            yb = stick_breaking_mixer(y3, qkvz)
            w_out = od_w_out[i].astype(BF16)
            split = GDN_HEADS * GDN_D
        h = proj_residual(ya.reshape(n_tok, -1), yb.reshape(n_tok, -1), h, w_out[:split], w_out[split:])
        kt, v = _memory_kv(mem, mem_norm, xa_wk[layer], xa_wv[layer])
        h = cross_attention_residual(h.reshape(bsz, length, d), norm_xa[layer], xa_wq[layer].astype(BF16), kt, v,
                                     xa_wo[layer].astype(BF16)).reshape(n_tok, d)
        h = _moe_layer(h, norm_ffn[layer], moe_w_group[layer], moe_b_group[layer], moe_w_expert[layer],
                       moe_b_expert[layer], moe_w_gate, moe_w_up, moe_w_down, layer,
                       final_norm if layer == depth - 1 else None)
    return h.reshape(bsz, length, d)
```

```python
import functools

import jax
import jax.numpy as jnp
import numpy as np
from jax import lax
from jax.experimental import pallas as pl
from jax.experimental.pallas import tpu as pltpu

F32 = jnp.float32
BF16 = jnp.bfloat16
EPS = 1e-6

D_MODEL = 1024
MEM_LEN = 256
SC_DIM = 512
SSM_HEADS = 16
SSM_HEAD_DIM = 64
SSM_INNER = 1024
SSM_GROUPS = 2
SSM_STATE = 128
SSM_XBC = SSM_INNER + 2 * SSM_GROUPS * SSM_STATE
SSD_CHUNK = 128
GDN_HEADS = 8
GDN_D = 128
GDN_CHUNK = 64
GDN_TILE = 128
SB_HEADS = 8
SB_HEAD_DIM = 64
SB_DIM = 512
SB_BLOCK = 128
SB_STEP_HEADS = 4
XA_HEADS = 4
XA_HEAD_DIM = 256
MOE_GROUPS = 4
MOE_PER_GROUP = 8
MOE_EXPERTS = 32
MOE_FF = 512
MOE_ROWS = 256
HALO = 8
LANES = 128
SB_LOG_ZERO = -104.0
VMEM_LIMIT = 56 * 1024 * 1024


def _cparams(*sem):
    return pltpu.CompilerParams(dimension_semantics=sem, vmem_limit_bytes=VMEM_LIMIT)


def _mm(a, b):
    return jnp.dot(a.astype(BF16), b.astype(BF16), preferred_element_type=F32)


def _mm_nt(a, b):
    return lax.dot_general(a.astype(BF16), b.astype(BF16), (((1,), (1,)), ((), ())),
                           preferred_element_type=F32)


def _split_bf16(x, n):
    parts, r = [], x
    for _ in range(n):
        p = r.astype(BF16)
        parts.append(p)
        r = r - p.astype(F32)
    return parts


def _mm_sel_rhs(x, sel, n=3):
    return sum(jnp.dot(p, sel, preferred_element_type=F32) for p in _split_bf16(x, n))


def _mm_sel_lhs(sel, x, n=3):
    return sum(jnp.dot(sel, p, preferred_element_type=F32) for p in _split_bf16(x, n))


def _silu(x):
    return x * jax.nn.sigmoid(x)


def _softplus(x):
    return jnp.maximum(x, 0.0) + jnp.log(1.0 + jnp.exp(-jnp.abs(x)))


def _rms(x, g):
    return x * lax.rsqrt(jnp.mean(x * x, axis=-1, keepdims=True) + EPS) * g


def _rms_matmul_kernel(x_ref, g_ref, w_ref, ws_ref, o_ref, os_ref):
    xn = _rms(x_ref[...], g_ref[...]).astype(BF16)
    o_ref[...] = jnp.dot(xn, w_ref[...], preferred_element_type=F32)
    os_ref[...] = jnp.dot(xn, ws_ref[...], preferred_element_type=F32)


def rms_matmul(x, g, w, ws, tm=512, tn=512):
    m, k = x.shape
    n = w.shape[1]
    tm = min(tm, m)
    main, small = pl.pallas_call(
        _rms_matmul_kernel,
        grid=(n // tn, m // tm),
        in_specs=[
            pl.BlockSpec((tm, k), lambda j, i: (i, 0)),
            pl.BlockSpec((1, k), lambda j, i: (0, 0)),
            pl.BlockSpec((k, tn), lambda j, i: (0, j)),
            pl.BlockSpec((k, LANES), lambda j, i: (0, 0)),
        ],
        out_specs=[
            pl.BlockSpec((tm, tn), lambda j, i: (i, j)),
            pl.BlockSpec((None, tm, LANES), lambda j, i: (j, i, 0)),
        ],
        out_shape=[jax.ShapeDtypeStruct((m, n), F32), jax.ShapeDtypeStruct((n // tn, m, LANES), F32)],
        compiler_params=_cparams("parallel", "parallel"),
        name="rms_matmul",
    )(x, g.reshape(1, k), w, ws)
    return main, small[0]


def _causal_conv(ext_ref, w_ref, rows):
    width = w_ref.shape[0]
    acc = None
    for j in range(width):
        start = HALO - (width - 1) + j
        term = w_ref[j:j + 1, :] * ext_ref[start:start + rows, :]
        acc = term if acc is None else acc + term
    return acc


def _halo_index(rows):
    step = rows // HALO
    return lambda i: jnp.maximum(i * step - 1, 0)


def _sc_kernel(b_ref, c_ref, x_ref, ch_ref, xh_ref, w_ref, o_ref, ext_ref):
    rows = o_ref.shape[0]
    first = pl.program_id(1) == 0
    ext_ref[0:HALO, :] = jnp.where(first, 0.0, ch_ref[...] * xh_ref[...])
    ext_ref[HALO:, :] = c_ref[...] * x_ref[...]
    o_ref[...] = b_ref[...] * _causal_conv(ext_ref, w_ref, rows)


def short_conv_mixer(y3, w, tl=512):
    bsz, length, _ = y3.shape
    tl = min(tl, length)
    hidx = _halo_index(tl)
    return pl.pallas_call(
        _sc_kernel,
        grid=(bsz, length // tl),
        in_specs=[
            pl.BlockSpec((None, tl, SC_DIM), lambda b, i: (b, i, 0)),
            pl.BlockSpec((None, tl, SC_DIM), lambda b, i: (b, i, 1)),
            pl.BlockSpec((None, tl, SC_DIM), lambda b, i: (b, i, 2)),
            pl.BlockSpec((None, HALO, SC_DIM), lambda b, i: (b, hidx(i), 1)),
            pl.BlockSpec((None, HALO, SC_DIM), lambda b, i: (b, hidx(i), 2)),
            pl.BlockSpec(w.shape, lambda b, i: (0, 0)),
        ],
        out_specs=pl.BlockSpec((None, tl, SC_DIM), lambda b, i: (b, i, 0)),
        out_shape=jax.ShapeDtypeStruct((bsz, length, SC_DIM), F32),
        scratch_shapes=[pltpu.VMEM((tl + HALO, SC_DIM), F32)],
        compiler_params=_cparams("parallel", "arbitrary"),
        name="short_conv_mixer",
    )(y3, y3, y3, y3, y3, w)


def _ssd_kernel(xbc_ref, halo_ref, z_ref, dt_ref, dtt_ref, cw_ref, cb_ref, dtb_r_ref, dtb_c_ref,
                alog_r_ref, alog_c_ref, d_ref, nw_ref, tri_ref, trit_ref, eh_ref, eq_ref,
                o_ref, ext_ref, s_ref):
    q = SSD_CHUNK
    hpg = SSM_HEADS // SSM_GROUPS
    gw = hpg * SSM_HEAD_DIM
    first = pl.program_id(1) == 0

    @pl.when(first)
    def _():
        s_ref[...] = jnp.zeros_like(s_ref)

    ext_ref[0:HALO, :] = jnp.where(first, 0.0, halo_ref[...])
    ext_ref[HALO:, :] = xbc_ref[...]
    xbc = _silu(_causal_conv(ext_ref, cw_ref, q) + cb_ref[...])
    xs = xbc[:, :SSM_INNER]
    bm = xbc[:, SSM_INNER:SSM_INNER + SSM_GROUPS * SSM_STATE]
    cm = xbc[:, SSM_INNER + SSM_GROUPS * SSM_STATE:]

    dt = _softplus(dt_ref[...] + dtb_r_ref[...])
    acs = _mm_sel_lhs(tri_ref[...], dt * -jnp.exp(alog_r_ref[...]))
    dtt = _softplus(dtt_ref[...] + dtb_c_ref[...])
    acst = _mm_sel_rhs(dtt * -jnp.exp(alog_c_ref[...]), trit_ref[...])
    dt_full = _mm_sel_rhs(dt, eh_ref[...])
    acs_full = _mm_sel_rhs(acs, eh_ref[...])
    acs_col = _mm_sel_rhs(acs, eq_ref[...])

    xdt = xs * dt_full
    acs_last = acs_full[q - 1:q, :]
    xw = xdt * jnp.exp(acs_last - acs_full)
    chunk_decay = jnp.exp(acs_last)

    row = lax.broadcasted_iota(jnp.int32, (q, q), 0)
    col = lax.broadcasted_iota(jnp.int32, (q, q), 1)
    causal = row >= col
    lane = lax.broadcasted_iota(jnp.int32, (q, 2 * SSM_HEAD_DIM), 1)

    y_diag, y_off = [], []
    for g in range(SSM_GROUPS):
        bm_g = bm[:, g * SSM_STATE:(g + 1) * SSM_STATE]
        cm_g = cm[:, g * SSM_STATE:(g + 1) * SSM_STATE]
        cb_g = _mm_nt(cm_g, bm_g)
        state = s_ref[g]
        y_off.append(_mm(cm_g, state))
        s_ref[g] = state * chunk_decay[:, g * gw:(g + 1) * gw] + _mm(bm_g.T, xw[:, g * gw:(g + 1) * gw])
        for pair in range(hpg // 2):
            h0 = g * hpg + 2 * pair
            xdt_pair = xdt[:, h0 * SSM_HEAD_DIM:(h0 + 2) * SSM_HEAD_DIM]
            outs = []
            for h in (h0, h0 + 1):
                seg = acs_col[:, h * q:(h + 1) * q] - acst[h:h + 1, :]
                decay = jnp.where(causal, jnp.exp(seg), 0.0)
                outs.append(_mm(cb_g * decay, xdt_pair))
            y_diag.append(jnp.where(lane < SSM_HEAD_DIM, outs[0], outs[1]))
    y = (jnp.concatenate(y_diag, axis=1) + jnp.concatenate(y_off, axis=1) * jnp.exp(acs_full)
         + xs * d_ref[...])
    y = y * _silu(z_ref[...])
    halves = []
    for g in range(SSM_GROUPS):
        yg = y[:, g * gw:(g + 1) * gw]
        halves.append(yg * lax.rsqrt(jnp.mean(yg * yg, axis=-1, keepdims=True) + EPS))
    o_ref[...] = jnp.concatenate(halves, axis=1) * nw_ref[...]


def _pad_lanes(v, fill=0.0):
    return jnp.pad(v.astype(F32), (0, LANES - v.shape[0]), constant_values=fill).reshape(1, LANES)


def _pad_col(v, rows=16):
    return jnp.pad(v.astype(F32), (0, rows - v.shape[0])).reshape(rows, 1)


def ssd_mixer(y3, small3, smallt, conv_w, conv_b, dt_bias, a_log, d_skip, norm_w):
    bsz, length, _ = y3.shape
    q = SSD_CHUNK
    hidx = _halo_index(q)
    tri = jnp.asarray(np.tril(np.ones((q, q), np.float32)), BF16)
    trit = jnp.asarray(np.triu(np.ones((q, q), np.float32)), BF16)
    heads = np.arange(LANES)[:, None]
    eh = jnp.asarray(heads == (np.arange(SSM_INNER)[None, :] // SSM_HEAD_DIM), BF16)
    eq = jnp.asarray(heads == (np.arange(SSM_HEADS * q)[None, :] // q), BF16)
    d_full = jnp.repeat(d_skip.astype(F32), SSM_HEAD_DIM).reshape(1, SSM_INNER)
    const = lambda a: pl.BlockSpec(a.shape, lambda b, c: (0,) * a.ndim)
    args = [conv_w, conv_b.reshape(1, -1), _pad_lanes(dt_bias), _pad_col(dt_bias), _pad_lanes(a_log),
            _pad_col(a_log), d_full, norm_w.reshape(1, -1), tri, trit, eh, eq]
    return pl.pallas_call(
        _ssd_kernel,
        grid=(bsz, length // q),
        in_specs=[
            pl.BlockSpec((None, q, SSM_XBC), lambda b, c: (b, c, 1)),
            pl.BlockSpec((None, HALO, SSM_XBC), lambda b, c: (b, hidx(c), 1)),
            pl.BlockSpec((None, q, SSM_INNER), lambda b, c: (b, c, 3)),
            pl.BlockSpec((None, q, LANES), lambda b, c: (b, c, 0)),
            pl.BlockSpec((None, 16, q), lambda b, c: (b, 0, c)),
        ] + [const(a) for a in args],
        out_specs=pl.BlockSpec((None, q, SSM_INNER), lambda b, c: (b, c, 0)),
        out_shape=jax.ShapeDtypeStruct((bsz, length, SSM_INNER), F32),
        scratch_shapes=[pltpu.VMEM((q + HALO, SSM_XBC), F32),
                        pltpu.VMEM((SSM_GROUPS, SSM_STATE, SSM_INNER // SSM_GROUPS), F32)],
        compiler_params=_cparams("parallel", "arbitrary"),
        name="ssd_mixer",
    )(y3, y3, y3, small3, smallt, *args)


def _unit_lower_inverse(mats, row, col):
    eye = jnp.where(row == col, 1.0, 0.0)
    blk = lambda n: (row >> (n.bit_length() - 1)) == (col >> (n.bit_length() - 1))
    p = [jnp.where(blk(16), -a, 0.0) for a in mats]
    t = [eye + x for x in p]
    for _ in range(3):
        p = [_mm(x, x) for x in p]
        t = [y + _mm(y, x) for y, x in zip(t, p)]
    for n in (16, 32):
        band = blk(2 * n) & jnp.logical_not(blk(n))
        left = [_mm(y, jnp.where(band, a, 0.0)) for y, a in zip(t, mats)]
        t = [y - _mm(x, y) for y, x in zip(t, left)]
    return t


def _gdn_kernel(qkv_ref, halo_ref, z_ref, ab_ref, abt_ref, cw_ref, dtb_r_ref, dtb_c_ref, alog_r_ref,
                alog_c_ref, nw_ref, tri_ref, trit_ref, eg_ref, eb_ref, o_ref, ext_ref, s_ref):
    n = GDN_TILE
    c = GDN_CHUNK
    d = GDN_D
    hd = GDN_HEADS * d
    first = pl.program_id(1) == 0

    @pl.when(first)
    def _():
        s_ref[...] = jnp.zeros_like(s_ref)

    ext_ref[0:HALO, :] = jnp.where(first, 0.0, halo_ref[...])
    ext_ref[HALO:, :] = qkv_ref[...]
    qkv = _silu(_causal_conv(ext_ref, cw_ref, n))
    z = z_ref[...]

    ab = ab_ref[...]
    g = -jnp.exp(alog_r_ref[...]) * _softplus(ab + dtb_r_ref[...])
    gc_full = _mm_sel_rhs(_mm_sel_lhs(tri_ref[...], g), eg_ref[...])
    beta_full = _mm_sel_rhs(jax.nn.sigmoid(ab), eb_ref[...])
    gt = -jnp.exp(alog_c_ref[...]) * _softplus(abt_ref[...] + dtb_c_ref[...])
    gct = _mm_sel_rhs(gt, trit_ref[...])

    row = lax.broadcasted_iota(jnp.int32, (n, n), 0)
    col = lax.broadcasted_iota(jnp.int32, (n, n), 1)
    same = (row >> (c.bit_length() - 1)) == (col >> (c.bit_length() - 1))
    incl = same & (row >= col)
    strict = same & (row > col)
    zeros_half = jnp.zeros((c, d), F32)

    heads = range(GDN_HEADS)
    sl = [slice(h * d, (h + 1) * d) for h in heads]
    l2n = lambda x: x * lax.rsqrt(jnp.sum(x * x, axis=-1, keepdims=True) + EPS)
    qn = [l2n(qkv[:, sl[h]]) * (d ** -0.5) for h in heads]
    kn = [l2n(qkv[:, hd + h * d:hd + (h + 1) * d]) for h in heads]
    vh = [qkv[:, 2 * hd + h * d:2 * hd + (h + 1) * d] for h in heads]
    gcol = [gc_full[:, sl[h]] for h in heads]
    beta = [beta_full[:, sl[h]] for h in heads]
    edec = [jnp.exp(gcol[h] - gct[h:h + 1, :]) for h in heads]
    egc = [jnp.exp(x) for x in gcol]
    kb = [kn[h] * beta[h] for h in heads]
    lower = [jnp.where(strict, _mm_nt(kb[h], kn[h]) * edec[h], 0.0) for h in heads]
    aqk = [jnp.where(incl, _mm_nt(qn[h], kn[h]) * edec[h], 0.0) for h in heads]
    tinv = _unit_lower_inverse(lower, row, col)
    sol = [_mm(tinv[h], jnp.concatenate([vh[h] * beta[h], kb[h] * egc[h]], axis=1)) for h in heads]
    qd = [qn[h] * egc[h] for h in heads]
    glast = [(gcol[h][c - 1:c, :], gcol[h][n - 1:n, :]) for h in heads]
    kdt = [(kn[h] * jnp.exp(jnp.concatenate([jnp.broadcast_to(glast[h][0], (c, d)),
                                             jnp.broadcast_to(glast[h][1], (c, d))], axis=0) - gcol[h])).T
           for h in heads]
    s0 = [s_ref[h] for h in heads]
    v0 = [sol[h][:c, :d] - _mm(sol[h][:c, d:], s0[h]) for h in heads]
    s1 = [s0[h] * jnp.exp(glast[h][0]) + _mm(kdt[h], jnp.concatenate([v0[h], zeros_half], axis=0)) for h in heads]
    v1 = [sol[h][c:, :d] - _mm(sol[h][c:, d:], s1[h]) for h in heads]
    for h in heads:
        s_ref[h] = s1[h] * jnp.exp(glast[h][1]) + _mm(kdt[h], jnp.concatenate([zeros_half, v1[h]], axis=0))
    outs = []
    for h in heads:
        o = (jnp.concatenate([_mm(qd[h][:c], s0[h]), _mm(qd[h][c:], s1[h])], axis=0)
             + _mm(aqk[h], jnp.concatenate([v0[h], v1[h]], axis=0)))
        o = o * lax.rsqrt(jnp.mean(o * o, axis=-1, keepdims=True) + EPS) * nw_ref[...]
        outs.append(o * _silu(z[:, sl[h]]))
    o_ref[...] = jnp.concatenate(outs, axis=1)


def gated_deltanet_mixer(y3, small3, smallt, conv_w, dt_bias, a_log, norm_w):
    bsz, length, _ = y3.shape
    n = GDN_TILE
    hd = GDN_HEADS * GDN_D
    hidx = _halo_index(n)
    idx = np.arange(n)
    same = (idx[:, None] // GDN_CHUNK) == (idx[None, :] // GDN_CHUNK)
    tri = jnp.asarray(same & (idx[:, None] >= idx[None, :]), BF16)
    trit = jnp.asarray(same & (idx[:, None] <= idx[None, :]), BF16)
    lanes = np.arange(LANES)[:, None]
    heads = np.arange(hd)[None, :] // GDN_D
    eg = jnp.asarray(lanes == heads, BF16)
    eb = jnp.asarray(lanes == heads + GDN_HEADS, BF16)
    const = lambda a: pl.BlockSpec(a.shape, lambda b, c: (0,) * a.ndim)
    args = [conv_w, _pad_lanes(dt_bias), _pad_col(dt_bias), _pad_lanes(a_log), _pad_col(a_log),
            norm_w.reshape(1, -1), tri, trit, eg, eb]
    return pl.pallas_call(
        _gdn_kernel,
        grid=(bsz, length // n),
        in_specs=[
            pl.BlockSpec((None, n, 3 * hd), lambda b, c: (b, c, 0)),
            pl.BlockSpec((None, HALO, 3 * hd), lambda b, c: (b, hidx(c), 0)),
            pl.BlockSpec((None, n, hd), lambda b, c: (b, c, 3)),
            pl.BlockSpec((None, n, LANES), lambda b, c: (b, c, 0)),
            pl.BlockSpec((None, 16, n), lambda b, c: (b, 0, c)),
        ] + [const(a) for a in args],
        out_specs=pl.BlockSpec((None, n, hd), lambda b, c: (b, c, 0)),
        out_shape=jax.ShapeDtypeStruct((bsz, length, hd), F32),
        scratch_shapes=[pltpu.VMEM((n + HALO, 3 * hd), F32), pltpu.VMEM((GDN_HEADS, GDN_D, GDN_D), F32)],
        compiler_params=_cparams("parallel", "arbitrary"),
        name="gated_deltanet",
    )(y3, y3, y3, small3, smallt, *args)


def _sb_kernel(q_ref, k_ref, v_ref, upper_ref, o_ref):
    blk = SB_BLOCK
    pair_w = 2 * SB_HEAD_DIM
    n_pairs = SB_STEP_HEADS // 2
    i = pl.program_id(2)
    q = q_ref[...] * (SB_HEAD_DIM ** -0.5)
    lane = lax.broadcasted_iota(jnp.int32, (blk, pair_w), 1)
    first_head = lane < SB_HEAD_DIM
    qs = []
    for p in range(n_pairs):
        q2 = q[:, p * pair_w:(p + 1) * pair_w]
        qs += [jnp.where(first_head, q2, 0.0).astype(BF16), jnp.where(first_head, 0.0, q2).astype(BF16)]
    row = lax.broadcasted_iota(jnp.int32, (blk, blk), 0)
    col = lax.broadcasted_iota(jnp.int32, (blk, blk), 1)
    earlier = col < row
    upper = upper_ref[...]
    heads = range(SB_STEP_HEADS)

    def key_block(kb, accs, sticks, diagonal):
        start = pl.multiple_of(kb * blk, blk)
        k = k_ref[pl.ds(start, blk), :].astype(BF16)
        v = v_ref[pl.ds(start, blk), :].astype(BF16)
        kp = [k[:, p * pair_w:(p + 1) * pair_w] for p in range(n_pairs)]
        vp = [v[:, p * pair_w:(p + 1) * pair_w] for p in range(n_pairs)]
        logits = [lax.dot_general(qs[h], kp[h // 2], (((1,), (1,)), ((), ())), preferred_element_type=F32)
                  for h in heads]
        log_keep = [-_softplus(x) for x in logits]
        if diagonal:
            log_keep = [jnp.where(earlier, x, 0.0) for x in log_keep]
        between = [_mm_sel_rhs(log_keep[h], upper, 2) + sticks[h] for h in heads]
        w = [jnp.exp(logits[h] + log_keep[h] + between[h]) for h in heads]
        if diagonal:
            w = [jnp.where(earlier, x, 0.0) for x in w]
        pv = [jnp.dot(w[h].astype(BF16), vp[h // 2], preferred_element_type=F32) for h in heads]
        accs = tuple(accs[p] + jnp.where(first_head, pv[2 * p], pv[2 * p + 1]) for p in range(n_pairs))
        sticks = tuple(sticks[h] + jnp.sum(log_keep[h], axis=-1, keepdims=True) for h in heads)
        return accs, sticks

    accs = tuple(jnp.zeros((blk, pair_w), F32) for _ in range(n_pairs))
    sticks = tuple(jnp.zeros((blk, 1), F32) for _ in heads)
    accs, sticks = key_block(i, accs, sticks, True)

    def alive(state):
        kb, _, sticks = state
        longest = sticks[0]
        for s in sticks[1:]:
            longest = jnp.maximum(longest, s)
        return (kb >= 0) & (jnp.max(longest) > SB_LOG_ZERO)

    def body(state):
        kb, accs, sticks = state
        accs, sticks = key_block(kb, accs, sticks, False)
        return kb - 1, accs, sticks

    _, accs, _ = lax.while_loop(alive, body, (i - 1, accs, sticks))
    o_ref[...] = jnp.concatenate(accs, axis=1)


def stick_breaking_mixer(y3, col0):
    bsz, length, _ = y3.shape
    blk = SB_BLOCK
    step_w = SB_STEP_HEADS * SB_HEAD_DIM
    steps = SB_DIM // step_w
    q0 = col0 // step_w
    idx = np.arange(blk)
    upper = jnp.asarray(idx[:, None] > idx[None, :], BF16)
    resident = lambda off: pl.BlockSpec((None, length, step_w), lambda b, p, i: (b, 0, q0 + off + p),
                                        pipeline_mode=pl.Buffered(1))
    return pl.pallas_call(
        _sb_kernel,
        grid=(bsz, steps, length // blk),
        in_specs=[
            pl.BlockSpec((None, blk, step_w), lambda b, p, i: (b, i, q0 + p)),
            resident(steps),
            resident(2 * steps),
            pl.BlockSpec((blk, blk), lambda b, p, i: (0, 0)),
        ],
        out_specs=pl.BlockSpec((None, blk, step_w), lambda b, p, i: (b, i, p)),
        out_shape=jax.ShapeDtypeStruct((bsz, length, SB_DIM), F32),
        compiler_params=_cparams("parallel", "parallel", "arbitrary"),
        name="stick_breaking",
    )(y3, y3, y3, upper)


def _proj_res_kernel(a_ref, b_ref, h_ref, wa_ref, wb_ref, o_ref):
    o_ref[...] = h_ref[...] + (jnp.dot(a_ref[...].astype(BF16), wa_ref[...], preferred_element_type=F32)
                               + jnp.dot(b_ref[...].astype(BF16), wb_ref[...], preferred_element_type=F32))


def proj_residual(ya, yb, h, wa, wb, tm=512):
    m, d = h.shape
    tm = min(tm, m)
    return pl.pallas_call(
        _proj_res_kernel,
        grid=(m // tm,),
        in_specs=[
            pl.BlockSpec((tm, ya.shape[1]), lambda i: (i, 0)),
            pl.BlockSpec((tm, yb.shape[1]), lambda i: (i, 0)),
            pl.BlockSpec((tm, d), lambda i: (i, 0)),
            pl.BlockSpec(wa.shape, lambda i: (0, 0)),
            pl.BlockSpec(wb.shape, lambda i: (0, 0)),
        ],
        out_specs=pl.BlockSpec((tm, d), lambda i: (i, 0)),
        out_shape=jax.ShapeDtypeStruct((m, d), F32),
        compiler_params=_cparams("parallel"),
        name="proj_residual",
    )(ya, yb, h, wa, wb)


def _xattn_kernel(h_ref, g_ref, wq_ref, kt_ref, v_ref, wo_ref, o_ref):
    h = h_ref[...]
    u = _rms(h, g_ref[...]).astype(BF16)
    q = jnp.dot(u, wq_ref[...], preferred_element_type=F32)
    heads = []
    for hd in range(XA_HEADS):
        sl = slice(hd * XA_HEAD_DIM, (hd + 1) * XA_HEAD_DIM)
        s = jnp.dot(q[:, sl].astype(BF16), kt_ref[sl, :], preferred_element_type=F32) * (XA_HEAD_DIM ** -0.5)
        p = jnp.exp(s - jnp.max(s, axis=-1, keepdims=True))
        p = p / jnp.sum(p, axis=-1, keepdims=True)
        heads.append(jnp.dot(p.astype(BF16), v_ref[:, sl], preferred_element_type=F32))
    o = jnp.concatenate(heads, axis=1).astype(BF16)
    o_ref[...] = h + jnp.dot(o, wo_ref[...], preferred_element_type=F32)


def cross_attention_residual(h3, g, wq, kt, v, wo, tl=512):
    bsz, length, d = h3.shape
    tl = min(tl, length)
    return pl.pallas_call(
        _xattn_kernel,
        grid=(bsz, length // tl),
        in_specs=[
            pl.BlockSpec((None, tl, d), lambda b, i: (b, i, 0)),
            pl.BlockSpec((1, d), lambda b, i: (0, 0)),
            pl.BlockSpec((d, d), lambda b, i: (0, 0)),
            pl.BlockSpec((None, d, MEM_LEN), lambda b, i: (b, 0, 0)),
            pl.BlockSpec((None, MEM_LEN, d), lambda b, i: (b, 0, 0)),
            pl.BlockSpec((d, d), lambda b, i: (0, 0)),
        ],
        out_specs=pl.BlockSpec((None, tl, d), lambda b, i: (b, i, 0)),
        out_shape=jax.ShapeDtypeStruct((bsz, length, d), F32),
        compiler_params=_cparams("parallel", "parallel"),
        name="cross_attention",
    )(h3, g.reshape(1, d), wq, kt, v, wo)


def _router_kernel(h_ref, g_ref, whi_ref, wlo_ref, b_ref, before_ref, xn_ref, r_ref, cnt_ref, run_ref):
    @pl.when(pl.program_id(0) == 0)
    def _():
        run_ref[...] = jnp.zeros_like(run_ref)

    xn = _rms(h_ref[...], g_ref[...])
    xn_ref[...] = xn
    x_hi = xn.astype(BF16)
    x_lo = (xn - x_hi.astype(F32)).astype(BF16)
    logits = (jnp.dot(x_hi, whi_ref[...], preferred_element_type=F32)
              + jnp.dot(x_lo, whi_ref[...], preferred_element_type=F32)
              + jnp.dot(x_hi, wlo_ref[...], preferred_element_type=F32) + b_ref[...])
    lane = lax.broadcasted_iota(jnp.int32, logits.shape, 1).astype(F32)
    neg = -1e30
    none = float(LANES)

    def top(vals):
        best = jnp.max(vals, axis=-1, keepdims=True)
        where = jnp.min(jnp.where(vals == best, lane, none), axis=-1, keepdims=True)
        return best, where

    gl = jnp.where(lane < MOE_GROUPS, logits, neg)
    gbest, gsel = top(gl)
    gprob = 1.0 / jnp.sum(jnp.exp(gl - gbest), axis=-1, keepdims=True)
    lo = MOE_GROUPS + gsel * MOE_PER_GROUP
    el = jnp.where((lane >= lo) & (lane < lo + MOE_PER_GROUP), logits, neg)
    m1, i1 = top(el)
    m2, i2 = top(jnp.where(lane == i1, neg, el))
    e = jnp.exp(m2 - m1)
    gate1 = gprob / (1.0 + e)
    gate2 = gprob * e / (1.0 + e)

    hot1 = lane == i1
    hot2 = lane == i2
    one1 = jnp.where(hot1, 1.0, 0.0)
    one2 = jnp.where(hot2, 1.0, 0.0)
    before = before_ref[...]
    prefix1 = jnp.dot(before, one1.astype(BF16), preferred_element_type=F32)
    prefix2 = jnp.dot(before, one2.astype(BF16), preferred_element_type=F32)
    total1 = jnp.sum(one1, axis=0, keepdims=True)
    running = run_ref[...]
    rank1 = jnp.sum(jnp.where(hot1, prefix1 + running, 0.0), axis=-1, keepdims=True)
    rank2 = jnp.sum(jnp.where(hot2, prefix2 + (running + total1), 0.0), axis=-1, keepdims=True)
    running = running + total1 + jnp.sum(one2, axis=0, keepdims=True)
    run_ref[...] = running
    cnt_ref[...] = running

    fields = (i1 - MOE_GROUPS, i2 - MOE_GROUPS, gate1, gate2, rank1, rank2)
    out = jnp.zeros_like(logits)
    for k, val in enumerate(fields):
        out = jnp.where(lane == k, val, out)
    r_ref[...] = out


def moe_router(h, g, w_hi, w_lo, bias, tm=512):
    m, d = h.shape
    tm = min(tm, m)
    idx = np.arange(tm)
    before = jnp.asarray(idx[:, None] > idx[None, :], BF16)
    return pl.pallas_call(
        _router_kernel,
        grid=(m // tm,),
        in_specs=[
            pl.BlockSpec((tm, d), lambda i: (i, 0)),
            pl.BlockSpec((1, d), lambda i: (0, 0)),
            pl.BlockSpec((d, LANES), lambda i: (0, 0)),
            pl.BlockSpec((d, LANES), lambda i: (0, 0)),
            pl.BlockSpec((1, LANES), lambda i: (0, 0)),
            pl.BlockSpec((tm, tm), lambda i: (0, 0)),
        ],
        out_specs=[pl.BlockSpec((tm, d), lambda i: (i, 0)), pl.BlockSpec((tm, LANES), lambda i: (i, 0)),
                   pl.BlockSpec((1, LANES), lambda i: (0, 0))],
        out_shape=[jax.ShapeDtypeStruct((m, d), F32), jax.ShapeDtypeStruct((m, LANES), F32),
                   jax.ShapeDtypeStruct((1, LANES), F32)],
        scratch_shapes=[pltpu.VMEM((1, LANES), F32)],
        compiler_params=_cparams("arbitrary"),
        name="moe_router",
    )(h, g.reshape(1, d), w_hi, w_lo, bias, before)


def _expert_kernel(beid_ref, nused_ref, x_ref, wg_ref, wu_ref, wd_ref, o_ref, wgb_ref, wub_ref, wdb_ref):
    i = pl.program_id(0)
    changed = (i == 0) | (beid_ref[i] != beid_ref[jnp.maximum(i - 1, 0)])

    @pl.when(changed)
    def _():
        wgb_ref[...] = wg_ref[...].astype(BF16)
        wub_ref[...] = wu_ref[...].astype(BF16)
        wdb_ref[...] = wd_ref[...].astype(BF16)

    @pl.when(i < nused_ref[0])
    def _():
        x = x_ref[...].astype(BF16)
        gate = jnp.dot(x, wgb_ref[...], preferred_element_type=F32)
        up = jnp.dot(x, wub_ref[...], preferred_element_type=F32)
        act = (_silu(gate) * up).astype(BF16)
        o_ref[...] = jnp.dot(act, wdb_ref[...], preferred_element_type=F32)

    @pl.when(i >= nused_ref[0])
    def _():
        o_ref[...] = jnp.zeros_like(o_ref)


def moe_experts(block_eid, n_used, xs, w_gate, w_up, w_down, layer):
    n_slots, d = xs.shape
    rows = MOE_ROWS
    ff = w_gate.shape[3]
    grid_spec = pltpu.PrefetchScalarGridSpec(
        num_scalar_prefetch=2,
        grid=(n_slots // rows,),
        in_specs=[
            pl.BlockSpec((rows, d), lambda i, be, nu: (i, 0)),
            pl.BlockSpec((None, None, d, ff), lambda i, be, nu: (layer, be[i], 0, 0)),
            pl.BlockSpec((None, None, d, ff), lambda i, be, nu: (layer, be[i], 0, 0)),
            pl.BlockSpec((None, None, ff, d), lambda i, be, nu: (layer, be[i], 0, 0)),
        ],
        out_specs=pl.BlockSpec((rows, d), lambda i, be, nu: (i, 0)),
        scratch_shapes=[pltpu.VMEM((d, ff), BF16), pltpu.VMEM((d, ff), BF16), pltpu.VMEM((ff, d), BF16)],
    )
    return pl.pallas_call(
        _expert_kernel,
        grid_spec=grid_spec,
        out_shape=jax.ShapeDtypeStruct((n_slots, d), F32),
        compiler_params=_cparams("arbitrary"),
        name="moe_experts",
    )(block_eid, n_used, xs, w_gate, w_up, w_down)


def _combine_kernel(h_ref, y0_ref, y1_ref, r_ref, g_ref, o_ref, *, final_norm):
    route = r_ref[...]
    h = h_ref[...] + (route[:, 2:3] * y0_ref[...] + route[:, 3:4] * y1_ref[...])
    o_ref[...] = _rms(h, g_ref[...]) if final_norm else h


def moe_combine(h, y0, y1, route, g, final_norm, tm=512):
    m, d = h.shape
    tm = min(tm, m)
    spec = pl.BlockSpec((tm, d), lambda i: (i, 0))
    return pl.pallas_call(
        functools.partial(_combine_kernel, final_norm=final_norm),
        grid=(m // tm,),
        in_specs=[spec, spec, spec, pl.BlockSpec((tm, LANES), lambda i: (i, 0)),
                  pl.BlockSpec((1, d), lambda i: (0, 0))],
        out_specs=spec,
        out_shape=jax.ShapeDtypeStruct((m, d), F32),
        compiler_params=_cparams("parallel"),
        name="moe_combine",
    )(h, y0, y1, route, g.reshape(1, d))


def _pad_cols(w):
    return jnp.pad(w, ((0, 0), (0, LANES - w.shape[1])))


def _dispatch(route, counts, n_tok):
    rows = MOE_ROWS
    n_assign = 2 * n_tok
    counts = counts[0, MOE_GROUPS:MOE_GROUPS + MOE_EXPERTS].astype(jnp.int32)
    padded = (counts + rows - 1) // rows * rows
    pad_end = jnp.cumsum(padded)
    pad_start = pad_end - padded
    eid = route[:, 0:2].astype(jnp.int32)
    rank = route[:, 4:6].astype(jnp.int32)
    hot = eid[:, :, None] == jnp.arange(MOE_EXPERTS, dtype=jnp.int32)
    dest = jnp.sum(jnp.where(hot, pad_start, 0), axis=-1) + rank
    n_blocks = -(-(n_assign + MOE_EXPERTS * (rows - 1)) // rows)
    tok = jnp.arange(n_assign, dtype=jnp.int32) // 2
    filler = jnp.arange(n_blocks * rows, dtype=jnp.int32) % n_tok
    slot_tok = filler.at[dest.reshape(-1)].set(tok, unique_indices=True)
    block_start = jnp.arange(n_blocks, dtype=jnp.int32) * rows
    block_eid = jnp.minimum(jnp.sum(block_start[:, None] >= pad_end[None, :], axis=-1), MOE_EXPERTS - 1)
    n_used = (pad_end[-1] // rows).astype(jnp.int32).reshape(1)
    return slot_tok, block_eid.astype(jnp.int32), n_used, dest


def _moe_layer(h, norm_g, w_group, b_group, w_expert, b_expert, w_gate, w_up, w_down, layer, final_g):
    n_tok, d = h.shape
    w_r = _pad_cols(jnp.concatenate([w_group, w_expert], axis=1))
    w_hi = w_r.astype(BF16)
    w_lo = (w_r - w_hi.astype(F32)).astype(BF16)
    bias = _pad_lanes(jnp.concatenate([b_group, b_expert]))
    xn, route, counts = moe_router(h, norm_g, w_hi, w_lo, bias)
    slot_tok, block_eid, n_used, dest = _dispatch(route, counts, n_tok)
    ys = moe_experts(block_eid, n_used, xn[slot_tok], w_gate, w_up, w_down, layer)
    g = norm_g if final_g is None else final_g
    return moe_combine(h, ys[dest[:, 0]], ys[dest[:, 1]], route, g, final_g is not None)


def _memory_kv(memn_in, mem_norm, wk, wv):
    bsz, m, d = memn_in.shape
    w = jnp.concatenate([wk, wv], axis=1).astype(BF16)
    kv, _ = rms_matmul(memn_in.reshape(bsz * m, d), mem_norm, w, jnp.zeros((d, LANES), BF16))
    k = kv[:, :d].reshape(bsz, m, d)
    v = kv[:, d:].reshape(bsz, m, d)
    return jnp.swapaxes(k, 1, 2).astype(BF16), v.astype(BF16)


def kernel(x, mem, mem_norm, final_norm, norm_mix, norm_xa, norm_ffn, xa_wq, xa_wk, xa_wv, xa_wo, moe_w_group, moe_b_group, moe_w_expert, moe_b_expert, moe_w_gate, moe_w_up, moe_w_down, ev_w_in, ev_sc_conv, ev_ssm_conv_w, ev_ssm_conv_b, ev_ssm_dt_bias, ev_ssm_a_log, ev_ssm_d, ev_ssm_norm, ev_w_out, od_w_in, od_gdn_conv, od_gdn_dt_bias, od_gdn_a_log, od_gdn_norm, od_w_out):
    bsz, length, d = x.shape
    n_tok = bsz * length
    depth = norm_mix.shape[0]
    h = x.reshape(n_tok, d)
    for layer in range(depth):
        i = layer // 2
        if layer % 2 == 0:
            w = ev_w_in[i]
            w_main = jnp.concatenate([w[:, :3 * SC_DIM], w[:, 3 * SC_DIM + SSM_INNER:3 * SC_DIM + SSM_INNER + SSM_XBC],
                                      w[:, 3 * SC_DIM:3 * SC_DIM + SSM_INNER]], axis=1).astype(BF16)
            w_small = _pad_cols(w[:, 3 * SC_DIM + SSM_INNER + SSM_XBC:]).astype(BF16)
            y, small = rms_matmul(h, norm_mix[layer], w_main, w_small, tm=512, tn=2048)
            y3 = y.reshape(bsz, length, -1)
            small3 = small.reshape(bsz, length, LANES)
            smallt = jnp.swapaxes(small3[:, :, :16], 1, 2)
            ya = short_conv_mixer(y3, ev_sc_conv[i])
            yb = ssd_mixer(y3, small3, smallt, ev_ssm_conv_w[i], ev_ssm_conv_b[i], ev_ssm_dt_bias[i],
                           ev_ssm_a_log[i], ev_ssm_d[i], ev_ssm_norm[i])
            w_out = ev_w_out[i].astype(BF16)
            split = SC_DIM
        else:
            w = od_w_in[i]
            qkvz = 4 * GDN_HEADS * GDN_D
            w_main = jnp.concatenate([w[:, :qkvz], w[:, qkvz + 2 * GDN_HEADS:]], axis=1).astype(BF16)
            w_small = _pad_cols(w[:, qkvz:qkvz + 2 * GDN_HEADS]).astype(BF16)
            y, small = rms_matmul(h, norm_mix[layer], w_main, w_small, tm=512, tn=2816)
            y3 = y.reshape(bsz, length, -1)
            small3 = small.reshape(bsz, length, LANES)
            smallt = jnp.swapaxes(small3[:, :, :16], 1, 2)
            ya = gated_deltanet_mixer(y3, small3, smallt, od_gdn_conv[i], od_gdn_dt_bias[i], od_gdn_a_log[i],
                                      od_gdn_norm[i])
            yb = stick_breaking_mixer(y3, qkvz)
            w_out = od_w_out[i].astype(BF16)
            split = GDN_HEADS * GDN_D
        h = proj_residual(ya.reshape(n_tok, -1), yb.reshape(n_tok, -1), h, w_out[:split], w_out[split:])
        kt, v = _memory_kv(mem, mem_norm, xa_wk[layer], xa_wv[layer])
        h = cross_attention_residual(h.reshape(bsz, length, d), norm_xa[layer], xa_wq[layer].astype(BF16), kt, v,
                                     xa_wo[layer].astype(BF16)).reshape(n_tok, d)
        h = _moe_layer(h, norm_ffn[layer], moe_w_group[layer], moe_b_group[layer], moe_w_expert[layer],
                       moe_b_expert[layer], moe_w_gate, moe_w_up, moe_w_down, layer,
                       final_norm if layer == depth - 1 else None)
    return h.reshape(bsz, length, d)
```

```python
import functools

import jax
import jax.numpy as jnp
import numpy as np
from jax import lax
from jax.experimental import pallas as pl
from jax.experimental.pallas import tpu as pltpu
from jax.experimental.pallas import tpu_sc as plsc

F32 = jnp.float32
BF16 = jnp.bfloat16
EPS = 1e-6

D_MODEL = 1024
MEM_LEN = 256
SC_DIM = 512
SSM_HEADS = 16
SSM_HEAD_DIM = 64
SSM_INNER = 1024
SSM_GROUPS = 2
SSM_STATE = 128
SSM_XBC = SSM_INNER + 2 * SSM_GROUPS * SSM_STATE
SSD_CHUNK = 128
GDN_HEADS = 8
GDN_D = 128
GDN_CHUNK = 64
GDN_TILE = 128
SB_HEADS = 8
SB_HEAD_DIM = 64
SB_DIM = 512
SB_BLOCK = 128
SB_STEP_HEADS = 4
XA_HEADS = 4
XA_HEAD_DIM = 256
MOE_GROUPS = 4
MOE_PER_GROUP = 8
MOE_EXPERTS = 32
MOE_FF = 512
MOE_ROWS = 256
SC_CORES = 2
SC_SUBCORES = 16
SC_WORKERS = SC_CORES * SC_SUBCORES
SC_CHUNK = 32
HALO = 8
LANES = 128
SB_LOG_ZERO = -104.0
VMEM_LIMIT = 56 * 1024 * 1024


def _cparams(*sem):
    return pltpu.CompilerParams(dimension_semantics=sem, vmem_limit_bytes=VMEM_LIMIT)


def _mm(a, b):
    return jnp.dot(a.astype(BF16), b.astype(BF16), preferred_element_type=F32)


def _mm_nt(a, b):
    return lax.dot_general(a.astype(BF16), b.astype(BF16), (((1,), (1,)), ((), ())),
                           preferred_element_type=F32)


def _split_bf16(x, n):
    parts, r = [], x
    for _ in range(n):
        p = r.astype(BF16)
        parts.append(p)
        r = r - p.astype(F32)
    return parts


def _mm_sel_rhs(x, sel, n=3):
    return sum(jnp.dot(p, sel, preferred_element_type=F32) for p in _split_bf16(x, n))


def _mm_sel_lhs(sel, x, n=3):
    return sum(jnp.dot(sel, p, preferred_element_type=F32) for p in _split_bf16(x, n))


def _silu(x):
    return x * jax.nn.sigmoid(x)


def _softplus(x):
    return jnp.maximum(x, 0.0) + jnp.log(1.0 + jnp.exp(-jnp.abs(x)))


def _rms(x, g):
    return x * lax.rsqrt(jnp.mean(x * x, axis=-1, keepdims=True) + EPS) * g


def _rms_matmul_kernel(x_ref, g_ref, w_ref, ws_ref, o_ref, os_ref):
    xn = _rms(x_ref[...], g_ref[...]).astype(BF16)
    o_ref[...] = jnp.dot(xn, w_ref[...], preferred_element_type=F32)
    os_ref[...] = jnp.dot(xn, ws_ref[...], preferred_element_type=F32)


def rms_matmul(x, g, w, ws, tm=512, tn=512):
    m, k = x.shape
    n = w.shape[1]
    tm = min(tm, m)
    main, small = pl.pallas_call(
        _rms_matmul_kernel,
        grid=(n // tn, m // tm),
        in_specs=[
            pl.BlockSpec((tm, k), lambda j, i: (i, 0)),
            pl.BlockSpec((1, k), lambda j, i: (0, 0)),
            pl.BlockSpec((k, tn), lambda j, i: (0, j)),
            pl.BlockSpec((k, LANES), lambda j, i: (0, 0)),
        ],
        out_specs=[
            pl.BlockSpec((tm, tn), lambda j, i: (i, j)),
            pl.BlockSpec((None, tm, LANES), lambda j, i: (j, i, 0)),
        ],
        out_shape=[jax.ShapeDtypeStruct((m, n), F32), jax.ShapeDtypeStruct((n // tn, m, LANES), F32)],
        compiler_params=_cparams("parallel", "parallel"),
        name="rms_matmul",
    )(x, g.reshape(1, k), w, ws)
    return main, small[0]


def _causal_conv(ext_ref, w_ref, rows):
    width = w_ref.shape[0]
    acc = None
    for j in range(width):
        start = HALO - (width - 1) + j
        term = w_ref[j:j + 1, :] * ext_ref[start:start + rows, :]
        acc = term if acc is None else acc + term
    return acc


def _halo_index(rows):
    step = rows // HALO
    return lambda i: jnp.maximum(i * step - 1, 0)


def _sc_kernel(b_ref, c_ref, x_ref, ch_ref, xh_ref, w_ref, o_ref, ext_ref):
    rows = o_ref.shape[0]
    first = pl.program_id(1) == 0
    ext_ref[0:HALO, :] = jnp.where(first, 0.0, ch_ref[...] * xh_ref[...])
    ext_ref[HALO:, :] = c_ref[...] * x_ref[...]
    o_ref[...] = b_ref[...] * _causal_conv(ext_ref, w_ref, rows)


def short_conv_mixer(y3, w, tl=512):
    bsz, length, _ = y3.shape
    tl = min(tl, length)
    hidx = _halo_index(tl)
    return pl.pallas_call(
        _sc_kernel,
        grid=(bsz, length // tl),
        in_specs=[
            pl.BlockSpec((None, tl, SC_DIM), lambda b, i: (b, i, 0)),
            pl.BlockSpec((None, tl, SC_DIM), lambda b, i: (b, i, 1)),
            pl.BlockSpec((None, tl, SC_DIM), lambda b, i: (b, i, 2)),
            pl.BlockSpec((None, HALO, SC_DIM), lambda b, i: (b, hidx(i), 1)),
            pl.BlockSpec((None, HALO, SC_DIM), lambda b, i: (b, hidx(i), 2)),
            pl.BlockSpec(w.shape, lambda b, i: (0, 0)),
        ],
        out_specs=pl.BlockSpec((None, tl, SC_DIM), lambda b, i: (b, i, 0)),
        out_shape=jax.ShapeDtypeStruct((bsz, length, SC_DIM), F32),
        scratch_shapes=[pltpu.VMEM((tl + HALO, SC_DIM), F32)],
        compiler_params=_cparams("parallel", "arbitrary"),
        name="short_conv_mixer",
    )(y3, y3, y3, y3, y3, w)


def _ssd_kernel(xbc_ref, halo_ref, z_ref, dt_ref, dtt_ref, cw_ref, cb_ref, dtb_r_ref, dtb_c_ref,
                alog_r_ref, alog_c_ref, d_ref, nw_ref, tri_ref, trit_ref, eh_ref, eq_ref,
                o_ref, ext_ref, s_ref):
    q = SSD_CHUNK
    hpg = SSM_HEADS // SSM_GROUPS
    gw = hpg * SSM_HEAD_DIM
    first = pl.program_id(1) == 0

    @pl.when(first)
    def _():
        s_ref[...] = jnp.zeros_like(s_ref)

    ext_ref[0:HALO, :] = jnp.where(first, 0.0, halo_ref[...])
    ext_ref[HALO:, :] = xbc_ref[...]
    xbc = _silu(_causal_conv(ext_ref, cw_ref, q) + cb_ref[...])
    xs = xbc[:, :SSM_INNER]
    bm = xbc[:, SSM_INNER:SSM_INNER + SSM_GROUPS * SSM_STATE]
    cm = xbc[:, SSM_INNER + SSM_GROUPS * SSM_STATE:]

    dt = _softplus(dt_ref[...] + dtb_r_ref[...])
    acs = _mm_sel_lhs(tri_ref[...], dt * -jnp.exp(alog_r_ref[...]))
    dtt = _softplus(dtt_ref[...] + dtb_c_ref[...])
    acst = _mm_sel_rhs(dtt * -jnp.exp(alog_c_ref[...]), trit_ref[...])
    dt_full = _mm_sel_rhs(dt, eh_ref[...])
    acs_full = _mm_sel_rhs(acs, eh_ref[...])
    acs_col = _mm_sel_rhs(acs, eq_ref[...])

    xdt = xs * dt_full
    acs_last = acs_full[q - 1:q, :]
    xw = xdt * jnp.exp(acs_last - acs_full)
    chunk_decay = jnp.exp(acs_last)

    row = lax.broadcasted_iota(jnp.int32, (q, q), 0)
    col = lax.broadcasted_iota(jnp.int32, (q, q), 1)
    causal = row >= col
    lane = lax.broadcasted_iota(jnp.int32, (q, 2 * SSM_HEAD_DIM), 1)

    y_diag, y_off = [], []
    for g in range(SSM_GROUPS):
        bm_g = bm[:, g * SSM_STATE:(g + 1) * SSM_STATE]
        cm_g = cm[:, g * SSM_STATE:(g + 1) * SSM_STATE]
        cb_g = _mm_nt(cm_g, bm_g)
        state = s_ref[g]
        y_off.append(_mm(cm_g, state))
        s_ref[g] = state * chunk_decay[:, g * gw:(g + 1) * gw] + _mm(bm_g.T, xw[:, g * gw:(g + 1) * gw])
        for pair in range(hpg // 2):
            h0 = g * hpg + 2 * pair
            xdt_pair = xdt[:, h0 * SSM_HEAD_DIM:(h0 + 2) * SSM_HEAD_DIM]
            outs = []
            for h in (h0, h0 + 1):
                seg = acs_col[:, h * q:(h + 1) * q] - acst[h:h + 1, :]
                decay = jnp.where(causal, jnp.exp(seg), 0.0)
                outs.append(_mm(cb_g * decay, xdt_pair))
            y_diag.append(jnp.where(lane < SSM_HEAD_DIM, outs[0], outs[1]))
    y = (jnp.concatenate(y_diag, axis=1) + jnp.concatenate(y_off, axis=1) * jnp.exp(acs_full)
         + xs * d_ref[...])
    y = y * _silu(z_ref[...])
    halves = []
    for g in range(SSM_GROUPS):
        yg = y[:, g * gw:(g + 1) * gw]
        halves.append(yg * lax.rsqrt(jnp.mean(yg * yg, axis=-1, keepdims=True) + EPS))
    o_ref[...] = jnp.concatenate(halves, axis=1) * nw_ref[...]


def _pad_lanes(v, fill=0.0):
    return jnp.pad(v.astype(F32), (0, LANES - v.shape[0]), constant_values=fill).reshape(1, LANES)


def _pad_col(v, rows=16):
    return jnp.pad(v.astype(F32), (0, rows - v.shape[0])).reshape(rows, 1)


def ssd_mixer(y3, small3, smallt, conv_w, conv_b, dt_bias, a_log, d_skip, norm_w):
    bsz, length, _ = y3.shape
    q = SSD_CHUNK
    hidx = _halo_index(q)
    tri = jnp.asarray(np.tril(np.ones((q, q), np.float32)), BF16)
    trit = jnp.asarray(np.triu(np.ones((q, q), np.float32)), BF16)
    heads = np.arange(LANES)[:, None]
    eh = jnp.asarray(heads == (np.arange(SSM_INNER)[None, :] // SSM_HEAD_DIM), BF16)
    eq = jnp.asarray(heads == (np.arange(SSM_HEADS * q)[None, :] // q), BF16)
    d_full = jnp.repeat(d_skip.astype(F32), SSM_HEAD_DIM).reshape(1, SSM_INNER)
    const = lambda a: pl.BlockSpec(a.shape, lambda b, c: (0,) * a.ndim)
    args = [conv_w, conv_b.reshape(1, -1), _pad_lanes(dt_bias), _pad_col(dt_bias), _pad_lanes(a_log),
            _pad_col(a_log), d_full, norm_w.reshape(1, -1), tri, trit, eh, eq]
    return pl.pallas_call(
        _ssd_kernel,
        grid=(bsz, length // q),
        in_specs=[
            pl.BlockSpec((None, q, SSM_XBC), lambda b, c: (b, c, 1)),
            pl.BlockSpec((None, HALO, SSM_XBC), lambda b, c: (b, hidx(c), 1)),
            pl.BlockSpec((None, q, SSM_INNER), lambda b, c: (b, c, 3)),
            pl.BlockSpec((None, q, LANES), lambda b, c: (b, c, 0)),
            pl.BlockSpec((None, 16, q), lambda b, c: (b, 0, c)),
        ] + [const(a) for a in args],
        out_specs=pl.BlockSpec((None, q, SSM_INNER), lambda b, c: (b, c, 0)),
        out_shape=jax.ShapeDtypeStruct((bsz, length, SSM_INNER), F32),
        scratch_shapes=[pltpu.VMEM((q + HALO, SSM_XBC), F32),
                        pltpu.VMEM((SSM_GROUPS, SSM_STATE, SSM_INNER // SSM_GROUPS), F32)],
        compiler_params=_cparams("parallel", "arbitrary"),
        name="ssd_mixer",
    )(y3, y3, y3, small3, smallt, *args)


def _unit_lower_inverse(mats, row, col):
    eye = jnp.where(row == col, 1.0, 0.0)
    blk = lambda n: (row >> (n.bit_length() - 1)) == (col >> (n.bit_length() - 1))
    p = [jnp.where(blk(16), -a, 0.0) for a in mats]
    t = [eye + x for x in p]
    for _ in range(3):
        p = [_mm(x, x) for x in p]
        t = [y + _mm(y, x) for y, x in zip(t, p)]
    for n in (16, 32):
        band = blk(2 * n) & jnp.logical_not(blk(n))
        left = [_mm(y, jnp.where(band, a, 0.0)) for y, a in zip(t, mats)]
        t = [y - _mm(x, y) for y, x in zip(t, left)]
    return t


def _gdn_kernel(qkv_ref, halo_ref, z_ref, ab_ref, abt_ref, cw_ref, dtb_r_ref, dtb_c_ref, alog_r_ref,
                alog_c_ref, nw_ref, tri_ref, trit_ref, eg_ref, eb_ref, o_ref, ext_ref, s_ref):
    n = GDN_TILE
    c = GDN_CHUNK
    d = GDN_D
    hd = GDN_HEADS * d
    first = pl.program_id(1) == 0

    @pl.when(first)
    def _():
        s_ref[...] = jnp.zeros_like(s_ref)

    ext_ref[0:HALO, :] = jnp.where(first, 0.0, halo_ref[...])
    ext_ref[HALO:, :] = qkv_ref[...]
    qkv = _silu(_causal_conv(ext_ref, cw_ref, n))
    z = z_ref[...]

    ab = ab_ref[...]
    g = -jnp.exp(alog_r_ref[...]) * _softplus(ab + dtb_r_ref[...])
    gc_full = _mm_sel_rhs(_mm_sel_lhs(tri_ref[...], g), eg_ref[...])
    beta_full = _mm_sel_rhs(jax.nn.sigmoid(ab), eb_ref[...])
    gt = -jnp.exp(alog_c_ref[...]) * _softplus(abt_ref[...] + dtb_c_ref[...])
    gct = _mm_sel_rhs(gt, trit_ref[...])

    row = lax.broadcasted_iota(jnp.int32, (n, n), 0)
    col = lax.broadcasted_iota(jnp.int32, (n, n), 1)
    same = (row >> (c.bit_length() - 1)) == (col >> (c.bit_length() - 1))
    incl = same & (row >= col)
    strict = same & (row > col)
    zeros_half = jnp.zeros((c, d), F32)

    heads = range(GDN_HEADS)
    sl = [slice(h * d, (h + 1) * d) for h in heads]
    l2n = lambda x: x * lax.rsqrt(jnp.sum(x * x, axis=-1, keepdims=True) + EPS)
    qn = [l2n(qkv[:, sl[h]]) * (d ** -0.5) for h in heads]
    kn = [l2n(qkv[:, hd + h * d:hd + (h + 1) * d]) for h in heads]
    vh = [qkv[:, 2 * hd + h * d:2 * hd + (h + 1) * d] for h in heads]
    gcol = [gc_full[:, sl[h]] for h in heads]
    beta = [beta_full[:, sl[h]] for h in heads]
    edec = [jnp.exp(gcol[h] - gct[h:h + 1, :]) for h in heads]
    egc = [jnp.exp(x) for x in gcol]
    kb = [kn[h] * beta[h] for h in heads]
    lower = [jnp.where(strict, _mm_nt(kb[h], kn[h]) * edec[h], 0.0) for h in heads]
    aqk = [jnp.where(incl, _mm_nt(qn[h], kn[h]) * edec[h], 0.0) for h in heads]
    tinv = _unit_lower_inverse(lower, row, col)
    sol = [_mm(tinv[h], jnp.concatenate([vh[h] * beta[h], kb[h] * egc[h]], axis=1)) for h in heads]
    qd = [qn[h] * egc[h] for h in heads]
    glast = [(gcol[h][c - 1:c, :], gcol[h][n - 1:n, :]) for h in heads]
    kdt = [(kn[h] * jnp.exp(jnp.concatenate([jnp.broadcast_to(glast[h][0], (c, d)),
                                             jnp.broadcast_to(glast[h][1], (c, d))], axis=0) - gcol[h])).T
           for h in heads]
    s0 = [s_ref[h] for h in heads]
    v0 = [sol[h][:c, :d] - _mm(sol[h][:c, d:], s0[h]) for h in heads]
    s1 = [s0[h] * jnp.exp(glast[h][0]) + _mm(kdt[h], jnp.concatenate([v0[h], zeros_half], axis=0)) for h in heads]
    v1 = [sol[h][c:, :d] - _mm(sol[h][c:, d:], s1[h]) for h in heads]
    for h in heads:
        s_ref[h] = s1[h] * jnp.exp(glast[h][1]) + _mm(kdt[h], jnp.concatenate([zeros_half, v1[h]], axis=0))
    outs = []
    for h in heads:
        o = (jnp.concatenate([_mm(qd[h][:c], s0[h]), _mm(qd[h][c:], s1[h])], axis=0)
             + _mm(aqk[h], jnp.concatenate([v0[h], v1[h]], axis=0)))
        o = o * lax.rsqrt(jnp.mean(o * o, axis=-1, keepdims=True) + EPS) * nw_ref[...]
        outs.append(o * _silu(z[:, sl[h]]))
    o_ref[...] = jnp.concatenate(outs, axis=1)


def gated_deltanet_mixer(y3, small3, smallt, conv_w, dt_bias, a_log, norm_w):
    bsz, length, _ = y3.shape
    n = GDN_TILE
    hd = GDN_HEADS * GDN_D
    hidx = _halo_index(n)
    idx = np.arange(n)
    same = (idx[:, None] // GDN_CHUNK) == (idx[None, :] // GDN_CHUNK)
    tri = jnp.asarray(same & (idx[:, None] >= idx[None, :]), BF16)
    trit = jnp.asarray(same & (idx[:, None] <= idx[None, :]), BF16)
    lanes = np.arange(LANES)[:, None]
    heads = np.arange(hd)[None, :] // GDN_D
    eg = jnp.asarray(lanes == heads, BF16)
    eb = jnp.asarray(lanes == heads + GDN_HEADS, BF16)
    const = lambda a: pl.BlockSpec(a.shape, lambda b, c: (0,) * a.ndim)
    args = [conv_w, _pad_lanes(dt_bias), _pad_col(dt_bias), _pad_lanes(a_log), _pad_col(a_log),
            norm_w.reshape(1, -1), tri, trit, eg, eb]
    return pl.pallas_call(
        _gdn_kernel,
        grid=(bsz, length // n),
        in_specs=[
            pl.BlockSpec((None, n, 3 * hd), lambda b, c: (b, c, 0)),
            pl.BlockSpec((None, HALO, 3 * hd), lambda b, c: (b, hidx(c), 0)),
            pl.BlockSpec((None, n, hd), lambda b, c: (b, c, 3)),
            pl.BlockSpec((None, n, LANES), lambda b, c: (b, c, 0)),
            pl.BlockSpec((None, 16, n), lambda b, c: (b, 0, c)),
        ] + [const(a) for a in args],
        out_specs=pl.BlockSpec((None, n, hd), lambda b, c: (b, c, 0)),
        out_shape=jax.ShapeDtypeStruct((bsz, length, hd), F32),
        scratch_shapes=[pltpu.VMEM((n + HALO, 3 * hd), F32), pltpu.VMEM((GDN_HEADS, GDN_D, GDN_D), F32)],
        compiler_params=_cparams("parallel", "arbitrary"),
        name="gated_deltanet",
    )(y3, y3, y3, small3, smallt, *args)


def _sb_kernel(q_ref, k_ref, v_ref, upper_ref, o_ref):
    blk = SB_BLOCK
    pair_w = 2 * SB_HEAD_DIM
    n_pairs = SB_STEP_HEADS // 2
    i = pl.program_id(2)
    q = q_ref[...] * (SB_HEAD_DIM ** -0.5)
    lane = lax.broadcasted_iota(jnp.int32, (blk, pair_w), 1)
    first_head = lane < SB_HEAD_DIM
    qs = []
    for p in range(n_pairs):
        q2 = q[:, p * pair_w:(p + 1) * pair_w]
        qs += [jnp.where(first_head, q2, 0.0).astype(BF16), jnp.where(first_head, 0.0, q2).astype(BF16)]
    row = lax.broadcasted_iota(jnp.int32, (blk, blk), 0)
    col = lax.broadcasted_iota(jnp.int32, (blk, blk), 1)
    earlier = col < row
    upper = upper_ref[...]
    heads = range(SB_STEP_HEADS)

    def key_block(kb, accs, sticks, diagonal):
        start = pl.multiple_of(kb * blk, blk)
        k = k_ref[pl.ds(start, blk), :].astype(BF16)
        v = v_ref[pl.ds(start, blk), :].astype(BF16)
        kp = [k[:, p * pair_w:(p + 1) * pair_w] for p in range(n_pairs)]
        vp = [v[:, p * pair_w:(p + 1) * pair_w] for p in range(n_pairs)]
        logits = [lax.dot_general(qs[h], kp[h // 2], (((1,), (1,)), ((), ())), preferred_element_type=F32)
                  for h in heads]
        log_keep = [-_softplus(x) for x in logits]
        if diagonal:
            log_keep = [jnp.where(earlier, x, 0.0) for x in log_keep]
        between = [_mm_sel_rhs(log_keep[h], upper, 2) + sticks[h] for h in heads]
        w = [jnp.exp(logits[h] + log_keep[h] + between[h]) for h in heads]
        if diagonal:
            w = [jnp.where(earlier, x, 0.0) for x in w]
        pv = [jnp.dot(w[h].astype(BF16), vp[h // 2], preferred_element_type=F32) for h in heads]
        accs = tuple(accs[p] + jnp.where(first_head, pv[2 * p], pv[2 * p + 1]) for p in range(n_pairs))
        sticks = tuple(sticks[h] + jnp.sum(log_keep[h], axis=-1, keepdims=True) for h in heads)
        return accs, sticks

    accs = tuple(jnp.zeros((blk, pair_w), F32) for _ in range(n_pairs))
    sticks = tuple(jnp.zeros((blk, 1), F32) for _ in heads)
    accs, sticks = key_block(i, accs, sticks, True)

    def alive(state):
        kb, _, sticks = state
        longest = sticks[0]
        for s in sticks[1:]:
            longest = jnp.maximum(longest, s)
        return (kb >= 0) & (jnp.max(longest) > SB_LOG_ZERO)

    def body(state):
        kb, accs, sticks = state
        accs, sticks = key_block(kb, accs, sticks, False)
        return kb - 1, accs, sticks

    _, accs, _ = lax.while_loop(alive, body, (i - 1, accs, sticks))
    o_ref[...] = jnp.concatenate(accs, axis=1)


def stick_breaking_mixer(y3, col0):
    bsz, length, _ = y3.shape
    blk = SB_BLOCK
    step_w = SB_STEP_HEADS * SB_HEAD_DIM
    steps = SB_DIM // step_w
    q0 = col0 // step_w
    idx = np.arange(blk)
    upper = jnp.asarray(idx[:, None] > idx[None, :], BF16)
    resident = lambda off: pl.BlockSpec((None, length, step_w), lambda b, p, i: (b, 0, q0 + off + p),
                                        pipeline_mode=pl.Buffered(1))
    return pl.pallas_call(
        _sb_kernel,
        grid=(bsz, steps, length // blk),
        in_specs=[
            pl.BlockSpec((None, blk, step_w), lambda b, p, i: (b, i, q0 + p)),
            resident(steps),
            resident(2 * steps),
            pl.BlockSpec((blk, blk), lambda b, p, i: (0, 0)),
        ],
        out_specs=pl.BlockSpec((None, blk, step_w), lambda b, p, i: (b, i, p)),
        out_shape=jax.ShapeDtypeStruct((bsz, length, SB_DIM), F32),
        compiler_params=_cparams("parallel", "parallel", "arbitrary"),
        name="stick_breaking",
    )(y3, y3, y3, upper)


def _proj_res_kernel(a_ref, b_ref, h_ref, wa_ref, wb_ref, o_ref):
    o_ref[...] = h_ref[...] + (jnp.dot(a_ref[...].astype(BF16), wa_ref[...], preferred_element_type=F32)
                               + jnp.dot(b_ref[...].astype(BF16), wb_ref[...], preferred_element_type=F32))


def proj_residual(ya, yb, h, wa, wb, tm=512):
    m, d = h.shape
    tm = min(tm, m)
    return pl.pallas_call(
        _proj_res_kernel,
        grid=(m // tm,),
        in_specs=[
            pl.BlockSpec((tm, ya.shape[1]), lambda i: (i, 0)),
            pl.BlockSpec((tm, yb.shape[1]), lambda i: (i, 0)),
            pl.BlockSpec((tm, d), lambda i: (i, 0)),
            pl.BlockSpec(wa.shape, lambda i: (0, 0)),
            pl.BlockSpec(wb.shape, lambda i: (0, 0)),
        ],
        out_specs=pl.BlockSpec((tm, d), lambda i: (i, 0)),
        out_shape=jax.ShapeDtypeStruct((m, d), F32),
        compiler_params=_cparams("parallel"),
        name="proj_residual",
    )(ya, yb, h, wa, wb)


def _xattn_kernel(h_ref, g_ref, wq_ref, kt_ref, v_ref, wo_ref, o_ref):
    h = h_ref[...]
    u = _rms(h, g_ref[...]).astype(BF16)
    q = jnp.dot(u, wq_ref[...], preferred_element_type=F32)
    heads = []
    for hd in range(XA_HEADS):
        sl = slice(hd * XA_HEAD_DIM, (hd + 1) * XA_HEAD_DIM)
        s = jnp.dot(q[:, sl].astype(BF16), kt_ref[sl, :], preferred_element_type=F32) * (XA_HEAD_DIM ** -0.5)
        p = jnp.exp(s - jnp.max(s, axis=-1, keepdims=True))
        p = p / jnp.sum(p, axis=-1, keepdims=True)
        heads.append(jnp.dot(p.astype(BF16), v_ref[:, sl], preferred_element_type=F32))
    o = jnp.concatenate(heads, axis=1).astype(BF16)
    o_ref[...] = h + jnp.dot(o, wo_ref[...], preferred_element_type=F32)


def cross_attention_residual(h3, g, wq, kt, v, wo, tl=512):
    bsz, length, d = h3.shape
    tl = min(tl, length)
    return pl.pallas_call(
        _xattn_kernel,
        grid=(bsz, length // tl),
        in_specs=[
            pl.BlockSpec((None, tl, d), lambda b, i: (b, i, 0)),
            pl.BlockSpec((1, d), lambda b, i: (0, 0)),
            pl.BlockSpec((d, d), lambda b, i: (0, 0)),
            pl.BlockSpec((None, d, MEM_LEN), lambda b, i: (b, 0, 0)),
            pl.BlockSpec((None, MEM_LEN, d), lambda b, i: (b, 0, 0)),
            pl.BlockSpec((d, d), lambda b, i: (0, 0)),
        ],
        out_specs=pl.BlockSpec((None, tl, d), lambda b, i: (b, i, 0)),
        out_shape=jax.ShapeDtypeStruct((bsz, length, d), F32),
        compiler_params=_cparams("parallel", "parallel"),
        name="cross_attention",
    )(h3, g.reshape(1, d), wq, kt, v, wo)


def _router_kernel(h_ref, g_ref, whi_ref, wlo_ref, b_ref, before_ref, xn_ref, r_ref, cnt_ref, run_ref):
    @pl.when(pl.program_id(0) == 0)
    def _():
        run_ref[...] = jnp.zeros_like(run_ref)

    xn = _rms(h_ref[...], g_ref[...])
    xn_ref[...] = xn
    x_hi = xn.astype(BF16)
    x_lo = (xn - x_hi.astype(F32)).astype(BF16)
    logits = (jnp.dot(x_hi, whi_ref[...], preferred_element_type=F32)
              + jnp.dot(x_lo, whi_ref[...], preferred_element_type=F32)
              + jnp.dot(x_hi, wlo_ref[...], preferred_element_type=F32) + b_ref[...])
    lane = lax.broadcasted_iota(jnp.int32, logits.shape, 1).astype(F32)
    neg = -1e30
    none = float(LANES)

    def top(vals):
        best = jnp.max(vals, axis=-1, keepdims=True)
        where = jnp.min(jnp.where(vals == best, lane, none), axis=-1, keepdims=True)
        return best, where

    gl = jnp.where(lane < MOE_GROUPS, logits, neg)
    gbest, gsel = top(gl)
    gprob = 1.0 / jnp.sum(jnp.exp(gl - gbest), axis=-1, keepdims=True)
    lo = MOE_GROUPS + gsel * MOE_PER_GROUP
    el = jnp.where((lane >= lo) & (lane < lo + MOE_PER_GROUP), logits, neg)
    m1, i1 = top(el)
    m2, i2 = top(jnp.where(lane == i1, neg, el))
    e = jnp.exp(m2 - m1)
    gate1 = gprob / (1.0 + e)
    gate2 = gprob * e / (1.0 + e)

    hot1 = lane == i1
    hot2 = lane == i2
    one1 = jnp.where(hot1, 1.0, 0.0)
    one2 = jnp.where(hot2, 1.0, 0.0)
    before = before_ref[...]
    prefix1 = jnp.dot(before, one1.astype(BF16), preferred_element_type=F32)
    prefix2 = jnp.dot(before, one2.astype(BF16), preferred_element_type=F32)
    total1 = jnp.sum(one1, axis=0, keepdims=True)
    running = run_ref[...]
    rank1 = jnp.sum(jnp.where(hot1, prefix1 + running, 0.0), axis=-1, keepdims=True)
    rank2 = jnp.sum(jnp.where(hot2, prefix2 + (running + total1), 0.0), axis=-1, keepdims=True)
    running = running + total1 + jnp.sum(one2, axis=0, keepdims=True)
    run_ref[...] = running
    cnt_ref[...] = running

    fields = (i1 - MOE_GROUPS, i2 - MOE_GROUPS, gate1, gate2, rank1, rank2)
    out = jnp.zeros_like(logits)
    for k, val in enumerate(fields):
        out = jnp.where(lane == k, val, out)
    r_ref[...] = out


def moe_router(h, g, w_hi, w_lo, bias, tm=512):
    m, d = h.shape
    tm = min(tm, m)
    idx = np.arange(tm)
    before = jnp.asarray(idx[:, None] > idx[None, :], BF16)
    return pl.pallas_call(
        _router_kernel,
        grid=(m // tm,),
        in_specs=[
            pl.BlockSpec((tm, d), lambda i: (i, 0)),
            pl.BlockSpec((1, d), lambda i: (0, 0)),
            pl.BlockSpec((d, LANES), lambda i: (0, 0)),
            pl.BlockSpec((d, LANES), lambda i: (0, 0)),
            pl.BlockSpec((1, LANES), lambda i: (0, 0)),
            pl.BlockSpec((tm, tm), lambda i: (0, 0)),
        ],
        out_specs=[pl.BlockSpec((tm, d), lambda i: (i, 0)), pl.BlockSpec((tm, LANES), lambda i: (i, 0)),
                   pl.BlockSpec((1, LANES), lambda i: (0, 0))],
        out_shape=[jax.ShapeDtypeStruct((m, d), F32), jax.ShapeDtypeStruct((m, LANES), F32),
                   jax.ShapeDtypeStruct((1, LANES), F32)],
        scratch_shapes=[pltpu.VMEM((1, LANES), F32)],
        compiler_params=_cparams("arbitrary"),
        name="moe_router",
    )(h, g.reshape(1, d), w_hi, w_lo, bias, before)


def _expert_kernel(beid_ref, valid_ref, x_ref, wg_ref, wu_ref, wd_ref, o_ref, wgb_ref, wub_ref, wdb_ref):
    i = pl.program_id(0)
    changed = (i == 0) | (beid_ref[i] != beid_ref[jnp.maximum(i - 1, 0)])
    valid = valid_ref[i]

    @pl.when(changed)
    def _():
        wgb_ref[...] = wg_ref[...].astype(BF16)
        wub_ref[...] = wu_ref[...].astype(BF16)
        wdb_ref[...] = wd_ref[...].astype(BF16)

    @pl.when(valid > 0)
    def _():
        row = lax.broadcasted_iota(jnp.int32, x_ref.shape, 0)
        x = jnp.where(row < valid, x_ref[...], 0.0).astype(BF16)
        gate = jnp.dot(x, wgb_ref[...], preferred_element_type=F32)
        up = jnp.dot(x, wub_ref[...], preferred_element_type=F32)
        act = (_silu(gate) * up).astype(BF16)
        o_ref[...] = jnp.dot(act, wdb_ref[...], preferred_element_type=F32)

    @pl.when(valid == 0)
    def _():
        o_ref[...] = jnp.zeros_like(o_ref)


def moe_experts(block_eid, block_valid, xs, w_gate, w_up, w_down, layer):
    n_slots, d = xs.shape
    rows = MOE_ROWS
    ff = w_gate.shape[3]
    grid_spec = pltpu.PrefetchScalarGridSpec(
        num_scalar_prefetch=2,
        grid=(n_slots // rows,),
        in_specs=[
            pl.BlockSpec((rows, d), lambda i, be, nv: (i, 0)),
            pl.BlockSpec((None, None, d, ff), lambda i, be, nv: (layer, be[i], 0, 0)),
            pl.BlockSpec((None, None, d, ff), lambda i, be, nv: (layer, be[i], 0, 0)),
            pl.BlockSpec((None, None, ff, d), lambda i, be, nv: (layer, be[i], 0, 0)),
        ],
        out_specs=pl.BlockSpec((rows, d), lambda i, be, nv: (i, 0)),
        scratch_shapes=[pltpu.VMEM((d, ff), BF16), pltpu.VMEM((d, ff), BF16), pltpu.VMEM((ff, d), BF16)],
    )
    return pl.pallas_call(
        _expert_kernel,
        grid_spec=grid_spec,
        out_shape=jax.ShapeDtypeStruct((n_slots, d), F32),
        compiler_params=_cparams("arbitrary"),
        name="moe_experts",
    )(block_eid, block_valid, xs, w_gate, w_up, w_down)


def _sc_mesh():
    return plsc.VectorSubcoreMesh(core_axis_name="c", subcore_axis_name="s",
                                  num_cores=SC_CORES, num_subcores=SC_SUBCORES)


def _sc_worker():
    return lax.axis_index("s") * SC_CORES + lax.axis_index("c")


def sc_scatter_rows(x, dest0, dest1, n_slots):
    n_tok, d = x.shape
    per_worker = n_tok // SC_WORKERS
    n_chunks = per_worker // SC_CHUNK
    shape3 = (SC_WORKERS, n_chunks, SC_CHUNK)

    @functools.partial(
        pl.kernel, mesh=_sc_mesh(), out_type=jax.ShapeDtypeStruct((n_slots, d), x.dtype),
        scratch_types=[pltpu.VMEM((n_chunks, SC_CHUNK), jnp.int32), pltpu.VMEM((n_chunks, SC_CHUNK), jnp.int32),
                       pltpu.VMEM((SC_CHUNK, d), x.dtype)],
        name="moe_scatter_rows")
    def scatter(x_hbm, d0_hbm, d1_hbm, out_hbm, i0_v, i1_v, rows_v):
        wid = _sc_worker()
        pltpu.sync_copy(d0_hbm.at[wid], i0_v)
        pltpu.sync_copy(d1_hbm.at[wid], i1_v)

        @pl.loop(0, n_chunks)
        def _(j):
            start = pl.multiple_of(wid * per_worker + j * SC_CHUNK, SC_CHUNK)
            pltpu.sync_copy(x_hbm.at[pl.ds(start, SC_CHUNK)], rows_v)
            pltpu.sync_copy(rows_v, out_hbm.at[i0_v.at[j]])
            pltpu.sync_copy(rows_v, out_hbm.at[i1_v.at[j]])

    return scatter(x, dest0.reshape(shape3), dest1.reshape(shape3))


def sc_gather_rows(table, idx):
    n_out = idx.shape[0]
    d = table.shape[1]
    per_worker = n_out // SC_WORKERS
    n_chunks = per_worker // SC_CHUNK

    @functools.partial(
        pl.kernel, mesh=_sc_mesh(), out_type=jax.ShapeDtypeStruct((n_out, d), table.dtype),
        scratch_types=[pltpu.VMEM((n_chunks, SC_CHUNK), jnp.int32), pltpu.VMEM((SC_CHUNK, d), table.dtype)],
        name="moe_gather_rows")
    def gather(table_hbm, idx_hbm, out_hbm, idx_v, rows_v):
        wid = _sc_worker()
        pltpu.sync_copy(idx_hbm.at[wid], idx_v)

        @pl.loop(0, n_chunks)
        def _(j):
            start = pl.multiple_of(wid * per_worker + j * SC_CHUNK, SC_CHUNK)
            pltpu.sync_copy(table_hbm.at[idx_v.at[j]], rows_v)
            pltpu.sync_copy(rows_v, out_hbm.at[pl.ds(start, SC_CHUNK)])

    return gather(table, idx.reshape(SC_WORKERS, n_chunks, SC_CHUNK))


def _combine_kernel(h_ref, y0_ref, y1_ref, r_ref, g_ref, o_ref, *, final_norm):
    route = r_ref[...]
    h = h_ref[...] + (route[:, 2:3] * y0_ref[...] + route[:, 3:4] * y1_ref[...])
    o_ref[...] = _rms(h, g_ref[...]) if final_norm else h


def moe_combine(h, y01, route, g, final_norm, tm=512):
    m, d = h.shape
    tm = min(tm, m)
    spec = pl.BlockSpec((tm, d), lambda i: (i, 0))
    second = pl.BlockSpec((tm, d), lambda i: (i + m // tm, 0))
    return pl.pallas_call(
        functools.partial(_combine_kernel, final_norm=final_norm),
        grid=(m // tm,),
        in_specs=[spec, spec, second, pl.BlockSpec((tm, LANES), lambda i: (i, 0)),
                  pl.BlockSpec((1, d), lambda i: (0, 0))],
        out_specs=spec,
        out_shape=jax.ShapeDtypeStruct((m, d), F32),
        compiler_params=_cparams("parallel"),
        name="moe_combine",
    )(h, y01, y01, route, g.reshape(1, d))


def _pad_cols(w):
    return jnp.pad(w, ((0, 0), (0, LANES - w.shape[1])))


def _dispatch(route, counts, n_tok):
    rows = MOE_ROWS
    counts = counts[0, MOE_GROUPS:MOE_GROUPS + MOE_EXPERTS].astype(jnp.int32)
    padded = (counts + rows - 1) // rows * rows
    pad_end = jnp.cumsum(padded)
    pad_start = pad_end - padded
    eid = route[:, 0:2].astype(jnp.int32)
    rank = route[:, 4:6].astype(jnp.int32)
    hot = eid[:, :, None] == jnp.arange(MOE_EXPERTS, dtype=jnp.int32)
    dest = jnp.sum(jnp.where(hot, pad_start, 0), axis=-1) + rank
    n_blocks = -(-(2 * n_tok + MOE_EXPERTS * (rows - 1)) // rows)
    block_start = jnp.arange(n_blocks, dtype=jnp.int32) * rows
    block_eid = jnp.minimum(jnp.sum(block_start[:, None] >= pad_end[None, :], axis=-1), MOE_EXPERTS - 1)
    filled = (pad_start + counts)[block_eid]
    block_valid = jnp.clip(filled - block_start, 0, rows)
    return dest[:, 0], dest[:, 1], block_eid.astype(jnp.int32), block_valid.astype(jnp.int32), n_blocks * rows


def _moe_layer(h, norm_g, w_group, b_group, w_expert, b_expert, w_gate, w_up, w_down, layer, final_g):
    n_tok, d = h.shape
    w_r = _pad_cols(jnp.concatenate([w_group, w_expert], axis=1))
    w_hi = w_r.astype(BF16)
    w_lo = (w_r - w_hi.astype(F32)).astype(BF16)
    bias = _pad_lanes(jnp.concatenate([b_group, b_expert]))
    xn, route, counts = moe_router(h, norm_g, w_hi, w_lo, bias)
    dest0, dest1, block_eid, block_valid, n_slots = _dispatch(route, counts, n_tok)
    xs = sc_scatter_rows(xn, dest0, dest1, n_slots)
    ys = moe_experts(block_eid, block_valid, xs, w_gate, w_up, w_down, layer)
    y01 = sc_gather_rows(ys, jnp.concatenate([dest0, dest1]))
    g = norm_g if final_g is None else final_g
    return moe_combine(h, y01, route, g, final_g is not None)


def _memory_kv(memn_in, mem_norm, wk, wv):
    bsz, m, d = memn_in.shape
    w = jnp.concatenate([wk, wv], axis=1).astype(BF16)
    kv, _ = rms_matmul(memn_in.reshape(bsz * m, d), mem_norm, w, jnp.zeros((d, LANES), BF16))
    k = kv[:, :d].reshape(bsz, m, d)
    v = kv[:, d:].reshape(bsz, m, d)
    return jnp.swapaxes(k, 1, 2).astype(BF16), v.astype(BF16)


def kernel(x, mem, mem_norm, final_norm, norm_mix, norm_xa, norm_ffn, xa_wq, xa_wk, xa_wv, xa_wo, moe_w_group, moe_b_group, moe_w_expert, moe_b_expert, moe_w_gate, moe_w_up, moe_w_down, ev_w_in, ev_sc_conv, ev_ssm_conv_w, ev_ssm_conv_b, ev_ssm_dt_bias, ev_ssm_a_log, ev_ssm_d, ev_ssm_norm, ev_w_out, od_w_in, od_gdn_conv, od_gdn_dt_bias, od_gdn_a_log, od_gdn_norm, od_w_out):
    bsz, length, d = x.shape
    n_tok = bsz * length
    depth = norm_mix.shape[0]
    h = x.reshape(n_tok, d)
    for layer in range(depth):
        i = layer // 2
        if layer % 2 == 0:
            w = ev_w_in[i]
            w_main = jnp.concatenate([w[:, :3 * SC_DIM], w[:, 3 * SC_DIM + SSM_INNER:3 * SC_DIM + SSM_INNER + SSM_XBC],
                                      w[:, 3 * SC_DIM:3 * SC_DIM + SSM_INNER]], axis=1).astype(BF16)
            w_small = _pad_cols(w[:, 3 * SC_DIM + SSM_INNER + SSM_XBC:]).astype(BF16)
            y, small = rms_matmul(h, norm_mix[layer], w_main, w_small, tm=512, tn=2048)
            y3 = y.reshape(bsz, length, -1)
            small3 = small.reshape(bsz, length, LANES)
            smallt = jnp.swapaxes(small3[:, :, :16], 1, 2)
            ya = short_conv_mixer(y3, ev_sc_conv[i])
            yb = ssd_mixer(y3, small3, smallt, ev_ssm_conv_w[i], ev_ssm_conv_b[i], ev_ssm_dt_bias[i],
                           ev_ssm_a_log[i], ev_ssm_d[i], ev_ssm_norm[i])
            w_out = ev_w_out[i].astype(BF16)
            split = SC_DIM
        else:
            w = od_w_in[i]
            qkvz = 4 * GDN_HEADS * GDN_D
            w_main = jnp.concatenate([w[:, :qkvz], w[:, qkvz + 2 * GDN_HEADS:]], axis=1).astype(BF16)
            w_small = _pad_cols(w[:, qkvz:qkvz + 2 * GDN_HEADS]).astype(BF16)
            y, small = rms_matmul(h, norm_mix[layer], w_main, w_small, tm=512, tn=2816)
            y3 = y.reshape(bsz, length, -1)
            small3 = small.reshape(bsz, length, LANES)
            smallt = jnp.swapaxes(small3[:, :, :16], 1, 2)
            ya = gated_deltanet_mixer(y3, small3, smallt, od_gdn_conv[i], od_gdn_dt_bias[i], od_gdn_a_log[i],
                                      od_gdn_norm[i])
            yb = stick_breaking_mixer(y3, qkvz)
            w_out = od_w_out[i].astype(BF16)
            split = GDN_HEADS * GDN_D
        h = proj_residual(ya.reshape(n_tok, -1), yb.reshape(n_tok, -1), h, w_out[:split], w_out[split:])
        kt, v = _memory_kv(mem, mem_norm, xa_wk[layer], xa_wv[layer])
        h = cross_attention_residual(h.reshape(bsz, length, d), norm_xa[layer], xa_wq[layer].astype(BF16), kt, v,
                                     xa_wo[layer].astype(BF16)).reshape(n_tok, d)
        h = _moe_layer(h, norm_ffn[layer], moe_w_group[layer], moe_b_group[layer], moe_w_expert[layer],
                       moe_b_expert[layer], moe_w_gate, moe_w_up, moe_w_down, layer,
                       final_norm if layer == depth - 1 else None)
    return h.reshape(bsz, length, d)
```

```python
import functools

import jax
import jax.numpy as jnp
import numpy as np
from jax import lax
from jax.experimental import pallas as pl
from jax.experimental.pallas import tpu as pltpu
from jax.experimental.pallas import tpu_sc as plsc

F32 = jnp.float32
BF16 = jnp.bfloat16
EPS = 1e-6

D_MODEL = 1024
MEM_LEN = 256
SC_DIM = 512
SSM_HEADS = 16
SSM_HEAD_DIM = 64
SSM_INNER = 1024
SSM_GROUPS = 2
SSM_STATE = 128
SSM_XBC = SSM_INNER + 2 * SSM_GROUPS * SSM_STATE
SSD_CHUNK = 128
GDN_HEADS = 8
GDN_D = 128
GDN_CHUNK = 64
GDN_TILE = 128
SB_HEADS = 8
SB_HEAD_DIM = 64
SB_DIM = 512
SB_BLOCK = 128
SB_STEP_HEADS = 8
XA_HEADS = 4
XA_HEAD_DIM = 256
MOE_GROUPS = 4
MOE_PER_GROUP = 8
MOE_EXPERTS = 32
MOE_FF = 512
MOE_ROWS = 256
SC_CORES = 2
SC_SUBCORES = 16
SC_WORKERS = SC_CORES * SC_SUBCORES
SC_CHUNK = 32
HALO = 8
LANES = 128
SB_LOG_ZERO = -104.0
VMEM_LIMIT = 56 * 1024 * 1024


def _cparams(*sem):
    return pltpu.CompilerParams(dimension_semantics=sem, vmem_limit_bytes=VMEM_LIMIT)


def _mm(a, b):
    return jnp.dot(a.astype(BF16), b.astype(BF16), preferred_element_type=F32)


def _mm_nt(a, b):
    return lax.dot_general(a.astype(BF16), b.astype(BF16), (((1,), (1,)), ((), ())),
                           preferred_element_type=F32)


def _split_bf16(x, n):
    parts, r = [], x
    for _ in range(n):
        p = r.astype(BF16)
        parts.append(p)
        r = r - p.astype(F32)
    return parts


def _mm_sel_rhs(x, sel, n=3):
    return sum(jnp.dot(p, sel, preferred_element_type=F32) for p in _split_bf16(x, n))


def _mm_sel_lhs(sel, x, n=3):
    return sum(jnp.dot(sel, p, preferred_element_type=F32) for p in _split_bf16(x, n))


def _silu(x):
    return x * jax.nn.sigmoid(x)


def _softplus(x):
    return jnp.maximum(x, 0.0) + jnp.log(1.0 + jnp.exp(-jnp.abs(x)))


def _rms(x, g):
    return x * lax.rsqrt(jnp.mean(x * x, axis=-1, keepdims=True) + EPS) * g


def _rms_matmul_kernel(x_ref, g_ref, w_ref, ws_ref, o_ref, os_ref):
    xn = _rms(x_ref[...], g_ref[...]).astype(BF16)
    o_ref[...] = jnp.dot(xn, w_ref[...], preferred_element_type=F32)
    os_ref[...] = jnp.dot(xn, ws_ref[...], preferred_element_type=F32)


def rms_matmul(x, g, w, ws, tm=512, tn=512):
    m, k = x.shape
    n = w.shape[1]
    tm = min(tm, m)
    main, small = pl.pallas_call(
        _rms_matmul_kernel,
        grid=(n // tn, m // tm),
        in_specs=[
            pl.BlockSpec((tm, k), lambda j, i: (i, 0)),
            pl.BlockSpec((1, k), lambda j, i: (0, 0)),
            pl.BlockSpec((k, tn), lambda j, i: (0, j)),
            pl.BlockSpec((k, LANES), lambda j, i: (0, 0)),
        ],
        out_specs=[
            pl.BlockSpec((tm, tn), lambda j, i: (i, j)),
            pl.BlockSpec((None, tm, LANES), lambda j, i: (j, i, 0)),
        ],
        out_shape=[jax.ShapeDtypeStruct((m, n), F32), jax.ShapeDtypeStruct((n // tn, m, LANES), F32)],
        compiler_params=_cparams("parallel", "parallel"),
        name="rms_matmul",
    )(x, g.reshape(1, k), w, ws)
    return main, small[0]


def _causal_conv(ext_ref, w_ref, rows):
    width = w_ref.shape[0]
    acc = None
    for j in range(width):
        start = HALO - (width - 1) + j
        term = w_ref[j:j + 1, :] * ext_ref[start:start + rows, :]
        acc = term if acc is None else acc + term
    return acc


def _halo_index(rows):
    step = rows // HALO
    return lambda i: jnp.maximum(i * step - 1, 0)


def _sc_kernel(b_ref, c_ref, x_ref, ch_ref, xh_ref, w_ref, o_ref, ext_ref):
    rows = o_ref.shape[0]
    first = pl.program_id(1) == 0
    ext_ref[0:HALO, :] = jnp.where(first, 0.0, ch_ref[...] * xh_ref[...])
    ext_ref[HALO:, :] = c_ref[...] * x_ref[...]
    o_ref[...] = b_ref[...] * _causal_conv(ext_ref, w_ref, rows)


def short_conv_mixer(y3, w, tl=512):
    bsz, length, _ = y3.shape
    tl = min(tl, length)
    hidx = _halo_index(tl)
    return pl.pallas_call(
        _sc_kernel,
        grid=(bsz, length // tl),
        in_specs=[
            pl.BlockSpec((None, tl, SC_DIM), lambda b, i: (b, i, 0)),
            pl.BlockSpec((None, tl, SC_DIM), lambda b, i: (b, i, 1)),
            pl.BlockSpec((None, tl, SC_DIM), lambda b, i: (b, i, 2)),
            pl.BlockSpec((None, HALO, SC_DIM), lambda b, i: (b, hidx(i), 1)),
            pl.BlockSpec((None, HALO, SC_DIM), lambda b, i: (b, hidx(i), 2)),
            pl.BlockSpec(w.shape, lambda b, i: (0, 0)),
        ],
        out_specs=pl.BlockSpec((None, tl, SC_DIM), lambda b, i: (b, i, 0)),
        out_shape=jax.ShapeDtypeStruct((bsz, length, SC_DIM), F32),
        scratch_shapes=[pltpu.VMEM((tl + HALO, SC_DIM), F32)],
        compiler_params=_cparams("parallel", "arbitrary"),
        name="short_conv_mixer",
    )(y3, y3, y3, y3, y3, w)


def _ssd_kernel(xbc_ref, halo_ref, z_ref, dt_ref, dtt_ref, cw_ref, cb_ref, dtb_r_ref, dtb_c_ref,
                alog_r_ref, alog_c_ref, d_ref, nw_ref, tri_ref, trit_ref, eh_ref, eq_ref,
                o_ref, ext_ref, s_ref):
    q = SSD_CHUNK
    hpg = SSM_HEADS // SSM_GROUPS
    gw = hpg * SSM_HEAD_DIM
    first = pl.program_id(1) == 0

    @pl.when(first)
    def _():
        s_ref[...] = jnp.zeros_like(s_ref)

    ext_ref[0:HALO, :] = jnp.where(first, 0.0, halo_ref[...])
    ext_ref[HALO:, :] = xbc_ref[...]
    xbc = _silu(_causal_conv(ext_ref, cw_ref, q) + cb_ref[...])
    xs = xbc[:, :SSM_INNER]
    bm = xbc[:, SSM_INNER:SSM_INNER + SSM_GROUPS * SSM_STATE]
    cm = xbc[:, SSM_INNER + SSM_GROUPS * SSM_STATE:]

    dt = _softplus(dt_ref[...] + dtb_r_ref[...])
    acs = _mm_sel_lhs(tri_ref[...], dt * -jnp.exp(alog_r_ref[...]))
    dtt = _softplus(dtt_ref[...] + dtb_c_ref[...])
    acst = _mm_sel_rhs(dtt * -jnp.exp(alog_c_ref[...]), trit_ref[...])
    dt_full = _mm_sel_rhs(dt, eh_ref[...])
    acs_full = _mm_sel_rhs(acs, eh_ref[...])
    acs_col = _mm_sel_rhs(acs, eq_ref[...])

    xdt = xs * dt_full
    acs_last = acs_full[q - 1:q, :]
    xw = xdt * jnp.exp(acs_last - acs_full)
    chunk_decay = jnp.exp(acs_last)

    row = lax.broadcasted_iota(jnp.int32, (q, q), 0)
    col = lax.broadcasted_iota(jnp.int32, (q, q), 1)
    causal = row >= col
    lane = lax.broadcasted_iota(jnp.int32, (q, 2 * SSM_HEAD_DIM), 1)

    y_diag, y_off = [], []
    for g in range(SSM_GROUPS):
        bm_g = bm[:, g * SSM_STATE:(g + 1) * SSM_STATE]
        cm_g = cm[:, g * SSM_STATE:(g + 1) * SSM_STATE]
        cb_g = _mm_nt(cm_g, bm_g)
        state = s_ref[g]
        y_off.append(_mm(cm_g, state))
        s_ref[g] = state * chunk_decay[:, g * gw:(g + 1) * gw] + _mm(bm_g.T, xw[:, g * gw:(g + 1) * gw])
        for pair in range(hpg // 2):
            h0 = g * hpg + 2 * pair
            xdt_pair = xdt[:, h0 * SSM_HEAD_DIM:(h0 + 2) * SSM_HEAD_DIM]
            outs = []
            for h in (h0, h0 + 1):
                seg = acs_col[:, h * q:(h + 1) * q] - acst[h:h + 1, :]
                decay = jnp.where(causal, jnp.exp(seg), 0.0)
                outs.append(_mm(cb_g * decay, xdt_pair))
            y_diag.append(jnp.where(lane < SSM_HEAD_DIM, outs[0], outs[1]))
    y = (jnp.concatenate(y_diag, axis=1) + jnp.concatenate(y_off, axis=1) * jnp.exp(acs_full)
         + xs * d_ref[...])
    y = y * _silu(z_ref[...])
    halves = []
    for g in range(SSM_GROUPS):
        yg = y[:, g * gw:(g + 1) * gw]
        halves.append(yg * lax.rsqrt(jnp.mean(yg * yg, axis=-1, keepdims=True) + EPS))
    o_ref[...] = jnp.concatenate(halves, axis=1) * nw_ref[...]


def _pad_lanes(v, fill=0.0):
    return jnp.pad(v.astype(F32), (0, LANES - v.shape[0]), constant_values=fill).reshape(1, LANES)


def _pad_col(v, rows=16):
    return jnp.pad(v.astype(F32), (0, rows - v.shape[0])).reshape(rows, 1)


def ssd_mixer(y3, small3, smallt, conv_w, conv_b, dt_bias, a_log, d_skip, norm_w):
    bsz, length, _ = y3.shape
    q = SSD_CHUNK
    hidx = _halo_index(q)
    tri = jnp.asarray(np.tril(np.ones((q, q), np.float32)), BF16)
    trit = jnp.asarray(np.triu(np.ones((q, q), np.float32)), BF16)
    heads = np.arange(LANES)[:, None]
    eh = jnp.asarray(heads == (np.arange(SSM_INNER)[None, :] // SSM_HEAD_DIM), BF16)
    eq = jnp.asarray(heads == (np.arange(SSM_HEADS * q)[None, :] // q), BF16)
    d_full = jnp.repeat(d_skip.astype(F32), SSM_HEAD_DIM).reshape(1, SSM_INNER)
    const = lambda a: pl.BlockSpec(a.shape, lambda b, c: (0,) * a.ndim)
    args = [conv_w, conv_b.reshape(1, -1), _pad_lanes(dt_bias), _pad_col(dt_bias), _pad_lanes(a_log),
            _pad_col(a_log), d_full, norm_w.reshape(1, -1), tri, trit, eh, eq]
    return pl.pallas_call(
        _ssd_kernel,
        grid=(bsz, length // q),
        in_specs=[
            pl.BlockSpec((None, q, SSM_XBC), lambda b, c: (b, c, 1)),
            pl.BlockSpec((None, HALO, SSM_XBC), lambda b, c: (b, hidx(c), 1)),
            pl.BlockSpec((None, q, SSM_INNER), lambda b, c: (b, c, 3)),
            pl.BlockSpec((None, q, LANES), lambda b, c: (b, c, 0)),
            pl.BlockSpec((None, 16, q), lambda b, c: (b, 0, c)),
        ] + [const(a) for a in args],
        out_specs=pl.BlockSpec((None, q, SSM_INNER), lambda b, c: (b, c, 0)),
        out_shape=jax.ShapeDtypeStruct((bsz, length, SSM_INNER), F32),
        scratch_shapes=[pltpu.VMEM((q + HALO, SSM_XBC), F32),
                        pltpu.VMEM((SSM_GROUPS, SSM_STATE, SSM_INNER // SSM_GROUPS), F32)],
        compiler_params=_cparams("parallel", "arbitrary"),
        name="ssd_mixer",
    )(y3, y3, y3, small3, smallt, *args)


def _unit_lower_inverse(mats, row, col):
    eye = jnp.where(row == col, 1.0, 0.0)
    blk = lambda n: (row >> (n.bit_length() - 1)) == (col >> (n.bit_length() - 1))
    p = [jnp.where(blk(16), -a, 0.0) for a in mats]
    t = [eye + x for x in p]
    for _ in range(3):
        p = [_mm(x, x) for x in p]
        t = [y + _mm(y, x) for y, x in zip(t, p)]
    for n in (16, 32):
        band = blk(2 * n) & jnp.logical_not(blk(n))
        left = [_mm(y, jnp.where(band, a, 0.0)) for y, a in zip(t, mats)]
        t = [y - _mm(x, y) for y, x in zip(t, left)]
    return t


def _gdn_kernel(qkv_ref, halo_ref, z_ref, ab_ref, abt_ref, cw_ref, dtb_r_ref, dtb_c_ref, alog_r_ref,
                alog_c_ref, nw_ref, tri_ref, trit_ref, eg_ref, eb_ref, o_ref, ext_ref, s_ref):
    n = GDN_TILE
    c = GDN_CHUNK
    d = GDN_D
    hd = GDN_HEADS * d
    first = pl.program_id(1) == 0

    @pl.when(first)
    def _():
        s_ref[...] = jnp.zeros_like(s_ref)

    ext_ref[0:HALO, :] = jnp.where(first, 0.0, halo_ref[...])
    ext_ref[HALO:, :] = qkv_ref[...]
    qkv = _silu(_causal_conv(ext_ref, cw_ref, n))
    z = z_ref[...]

    ab = ab_ref[...]
    g = -jnp.exp(alog_r_ref[...]) * _softplus(ab + dtb_r_ref[...])
    gc_full = _mm_sel_rhs(_mm_sel_lhs(tri_ref[...], g), eg_ref[...])
    beta_full = _mm_sel_rhs(jax.nn.sigmoid(ab), eb_ref[...])
    gt = -jnp.exp(alog_c_ref[...]) * _softplus(abt_ref[...] + dtb_c_ref[...])
    gct = _mm_sel_rhs(gt, trit_ref[...])

    row = lax.broadcasted_iota(jnp.int32, (n, n), 0)
    col = lax.broadcasted_iota(jnp.int32, (n, n), 1)
    same = (row >> (c.bit_length() - 1)) == (col >> (c.bit_length() - 1))
    incl = same & (row >= col)
    strict = same & (row > col)
    zeros_half = jnp.zeros((c, d), F32)

    heads = range(GDN_HEADS)
    sl = [slice(h * d, (h + 1) * d) for h in heads]
    l2n = lambda x: x * lax.rsqrt(jnp.sum(x * x, axis=-1, keepdims=True) + EPS)
    qn = [l2n(qkv[:, sl[h]]) * (d ** -0.5) for h in heads]
    kn = [l2n(qkv[:, hd + h * d:hd + (h + 1) * d]) for h in heads]
    vh = [qkv[:, 2 * hd + h * d:2 * hd + (h + 1) * d] for h in heads]
    gcol = [gc_full[:, sl[h]] for h in heads]
    beta = [beta_full[:, sl[h]] for h in heads]
    edec = [jnp.exp(gcol[h] - gct[h:h + 1, :]) for h in heads]
    egc = [jnp.exp(x) for x in gcol]
    kb = [kn[h] * beta[h] for h in heads]
    lower = [jnp.where(strict, _mm_nt(kb[h], kn[h]) * edec[h], 0.0) for h in heads]
    aqk = [jnp.where(incl, _mm_nt(qn[h], kn[h]) * edec[h], 0.0) for h in heads]
    tinv = _unit_lower_inverse(lower, row, col)
    sol = [_mm(tinv[h], jnp.concatenate([vh[h] * beta[h], kb[h] * egc[h]], axis=1)) for h in heads]
    qd = [qn[h] * egc[h] for h in heads]
    glast = [(gcol[h][c - 1:c, :], gcol[h][n - 1:n, :]) for h in heads]
    kdt = [(kn[h] * jnp.exp(jnp.concatenate([jnp.broadcast_to(glast[h][0], (c, d)),
                                             jnp.broadcast_to(glast[h][1], (c, d))], axis=0) - gcol[h])).T
           for h in heads]
    s0 = [s_ref[h] for h in heads]
    v0 = [sol[h][:c, :d] - _mm(sol[h][:c, d:], s0[h]) for h in heads]
    s1 = [s0[h] * jnp.exp(glast[h][0]) + _mm(kdt[h], jnp.concatenate([v0[h], zeros_half], axis=0)) for h in heads]
    v1 = [sol[h][c:, :d] - _mm(sol[h][c:, d:], s1[h]) for h in heads]
    for h in heads:
        s_ref[h] = s1[h] * jnp.exp(glast[h][1]) + _mm(kdt[h], jnp.concatenate([zeros_half, v1[h]], axis=0))
    outs = []
    for h in heads:
        o = (jnp.concatenate([_mm(qd[h][:c], s0[h]), _mm(qd[h][c:], s1[h])], axis=0)
             + _mm(aqk[h], jnp.concatenate([v0[h], v1[h]], axis=0)))
        o = o * lax.rsqrt(jnp.mean(o * o, axis=-1, keepdims=True) + EPS) * nw_ref[...]
        outs.append(o * _silu(z[:, sl[h]]))
    o_ref[...] = jnp.concatenate(outs, axis=1)


def gated_deltanet_mixer(y3, small3, smallt, conv_w, dt_bias, a_log, norm_w):
    bsz, length, _ = y3.shape
    n = GDN_TILE
    hd = GDN_HEADS * GDN_D
    hidx = _halo_index(n)
    idx = np.arange(n)
    same = (idx[:, None] // GDN_CHUNK) == (idx[None, :] // GDN_CHUNK)
    tri = jnp.asarray(same & (idx[:, None] >= idx[None, :]), BF16)
    trit = jnp.asarray(same & (idx[:, None] <= idx[None, :]), BF16)
    lanes = np.arange(LANES)[:, None]
    heads = np.arange(hd)[None, :] // GDN_D
    eg = jnp.asarray(lanes == heads, BF16)
    eb = jnp.asarray(lanes == heads + GDN_HEADS, BF16)
    const = lambda a: pl.BlockSpec(a.shape, lambda b, c: (0,) * a.ndim)
    args = [conv_w, _pad_lanes(dt_bias), _pad_col(dt_bias), _pad_lanes(a_log), _pad_col(a_log),
            norm_w.reshape(1, -1), tri, trit, eg, eb]
    return pl.pallas_call(
        _gdn_kernel,
        grid=(bsz, length // n),
        in_specs=[
            pl.BlockSpec((None, n, 3 * hd), lambda b, c: (b, c, 0)),
            pl.BlockSpec((None, HALO, 3 * hd), lambda b, c: (b, hidx(c), 0)),
            pl.BlockSpec((None, n, hd), lambda b, c: (b, c, 3)),
            pl.BlockSpec((None, n, LANES), lambda b, c: (b, c, 0)),
            pl.BlockSpec((None, 16, n), lambda b, c: (b, 0, c)),
        ] + [const(a) for a in args],
        out_specs=pl.BlockSpec((None, n, hd), lambda b, c: (b, c, 0)),
        out_shape=jax.ShapeDtypeStruct((bsz, length, hd), F32),
        scratch_shapes=[pltpu.VMEM((n + HALO, 3 * hd), F32), pltpu.VMEM((GDN_HEADS, GDN_D, GDN_D), F32)],
        compiler_params=_cparams("parallel", "arbitrary"),
        name="gated_deltanet",
    )(y3, y3, y3, small3, smallt, *args)


def _sb_kernel(q_ref, k_ref, v_ref, upper_ref, o_ref):
    blk = SB_BLOCK
    pair_w = 2 * SB_HEAD_DIM
    n_pairs = SB_STEP_HEADS // 2
    i = pl.program_id(2)
    q = q_ref[...] * (SB_HEAD_DIM ** -0.5)
    lane = lax.broadcasted_iota(jnp.int32, (blk, pair_w), 1)
    first_head = lane < SB_HEAD_DIM
    qs = []
    for p in range(n_pairs):
        q2 = q[:, p * pair_w:(p + 1) * pair_w]
        qs += [jnp.where(first_head, q2, 0.0).astype(BF16), jnp.where(first_head, 0.0, q2).astype(BF16)]
    row = lax.broadcasted_iota(jnp.int32, (blk, blk), 0)
    col = lax.broadcasted_iota(jnp.int32, (blk, blk), 1)
    earlier = col < row
    upper = upper_ref[...]
    heads = range(SB_STEP_HEADS)

    def key_block(kb, accs, sticks, diagonal):
        start = pl.multiple_of(kb * blk, blk)
        k = k_ref[pl.ds(start, blk), :].astype(BF16)
        v = v_ref[pl.ds(start, blk), :].astype(BF16)
        kp = [k[:, p * pair_w:(p + 1) * pair_w] for p in range(n_pairs)]
        vp = [v[:, p * pair_w:(p + 1) * pair_w] for p in range(n_pairs)]
        logits = [lax.dot_general(qs[h], kp[h // 2], (((1,), (1,)), ((), ())), preferred_element_type=F32)
                  for h in heads]
        log_keep = [-_softplus(x) for x in logits]
        if diagonal:
            log_keep = [jnp.where(earlier, x, 0.0) for x in log_keep]
        between = [_mm_sel_rhs(log_keep[h], upper, 2) + sticks[h] for h in heads]
        w = [jnp.exp(logits[h] + log_keep[h] + between[h]) for h in heads]
        if diagonal:
            w = [jnp.where(earlier, x, 0.0) for x in w]
        pv = [jnp.dot(w[h].astype(BF16), vp[h // 2], preferred_element_type=F32) for h in heads]
        accs = tuple(accs[p] + jnp.where(first_head, pv[2 * p], pv[2 * p + 1]) for p in range(n_pairs))
        sticks = tuple(sticks[h] + jnp.sum(log_keep[h], axis=-1, keepdims=True) for h in heads)
        return accs, sticks

    accs = tuple(jnp.zeros((blk, pair_w), F32) for _ in range(n_pairs))
    sticks = tuple(jnp.zeros((blk, 1), F32) for _ in heads)
    accs, sticks = key_block(i, accs, sticks, True)

    def alive(state):
        kb, _, sticks = state
        longest = sticks[0]
        for s in sticks[1:]:
            longest = jnp.maximum(longest, s)
        return (kb >= 0) & (jnp.max(longest) > SB_LOG_ZERO)

    def body(state):
        kb, accs, sticks = state
        accs, sticks = key_block(kb, accs, sticks, False)
        return kb - 1, accs, sticks

    _, accs, _ = lax.while_loop(alive, body, (i - 1, accs, sticks))
    o_ref[...] = jnp.concatenate(accs, axis=1)


def stick_breaking_mixer(y3, col0):
    bsz, length, _ = y3.shape
    blk = SB_BLOCK
    step_w = SB_STEP_HEADS * SB_HEAD_DIM
    steps = SB_DIM // step_w
    q0 = col0 // step_w
    idx = np.arange(blk)
    upper = jnp.asarray(idx[:, None] > idx[None, :], BF16)
    resident = lambda off: pl.BlockSpec((None, length, step_w), lambda b, p, i: (b, 0, q0 + off + p),
                                        pipeline_mode=pl.Buffered(1))
    return pl.pallas_call(
        _sb_kernel,
        grid=(bsz, steps, length // blk),
        in_specs=[
            pl.BlockSpec((None, blk, step_w), lambda b, p, i: (b, i, q0 + p)),
            resident(steps),
            resident(2 * steps),
            pl.BlockSpec((blk, blk), lambda b, p, i: (0, 0)),
        ],
        out_specs=pl.BlockSpec((None, blk, step_w), lambda b, p, i: (b, i, p)),
        out_shape=jax.ShapeDtypeStruct((bsz, length, SB_DIM), F32),
        compiler_params=_cparams("parallel", "parallel", "arbitrary"),
        name="stick_breaking",
    )(y3, y3, y3, upper)


def _mixer_out(a_ref, b_ref, h_ref, wa_ref, wb_ref):
    return h_ref[...] + (jnp.dot(a_ref[...].astype(BF16), wa_ref[...], preferred_element_type=F32)
                         + jnp.dot(b_ref[...].astype(BF16), wb_ref[...], preferred_element_type=F32))


def _cross_attention(h, g_ref, wq_ref, kt_ref, v_ref, wo_ref):
    u = _rms(h, g_ref[...]).astype(BF16)
    q = jnp.dot(u, wq_ref[...], preferred_element_type=F32)
    heads = []
    for hd in range(XA_HEADS):
        sl = slice(hd * XA_HEAD_DIM, (hd + 1) * XA_HEAD_DIM)
        s = jnp.dot(q[:, sl].astype(BF16), kt_ref[sl, :], preferred_element_type=F32) * (XA_HEAD_DIM ** -0.5)
        p = jnp.exp(s - jnp.max(s, axis=-1, keepdims=True))
        p = p / jnp.sum(p, axis=-1, keepdims=True)
        heads.append(jnp.dot(p.astype(BF16), v_ref[:, sl], preferred_element_type=F32))
    o = jnp.concatenate(heads, axis=1).astype(BF16)
    return h + jnp.dot(o, wo_ref[...], preferred_element_type=F32)


def _route(xn, whi_ref, wlo_ref, b_ref, before_ref, run_ref):
    x_hi = xn.astype(BF16)
    x_lo = (xn - x_hi.astype(F32)).astype(BF16)
    logits = (jnp.dot(x_hi, whi_ref[...], preferred_element_type=F32)
              + jnp.dot(x_lo, whi_ref[...], preferred_element_type=F32)
              + jnp.dot(x_hi, wlo_ref[...], preferred_element_type=F32) + b_ref[...])
    lane = lax.broadcasted_iota(jnp.int32, logits.shape, 1).astype(F32)
    neg = -1e30
    none = float(LANES)

    def top(vals):
        best = jnp.max(vals, axis=-1, keepdims=True)
        where = jnp.min(jnp.where(vals == best, lane, none), axis=-1, keepdims=True)
        return best, where

    gl = jnp.where(lane < MOE_GROUPS, logits, neg)
    gbest, gsel = top(gl)
    gprob = 1.0 / jnp.sum(jnp.exp(gl - gbest), axis=-1, keepdims=True)
    lo = MOE_GROUPS + gsel * MOE_PER_GROUP
    el = jnp.where((lane >= lo) & (lane < lo + MOE_PER_GROUP), logits, neg)
    m1, i1 = top(el)
    m2, i2 = top(jnp.where(lane == i1, neg, el))
    e = jnp.exp(m2 - m1)
    gate1 = gprob / (1.0 + e)
    gate2 = gprob * e / (1.0 + e)

    hot1 = lane == i1
    hot2 = lane == i2
    one1 = jnp.where(hot1, 1.0, 0.0)
    one2 = jnp.where(hot2, 1.0, 0.0)
    before = before_ref[...]
    prefix1 = jnp.dot(before, one1.astype(BF16), preferred_element_type=F32)
    prefix2 = jnp.dot(before, one2.astype(BF16), preferred_element_type=F32)
    total1 = jnp.sum(one1, axis=0, keepdims=True)
    running = run_ref[...]
    rank1 = jnp.sum(jnp.where(hot1, prefix1 + running, 0.0), axis=-1, keepdims=True)
    rank2 = jnp.sum(jnp.where(hot2, prefix2 + (running + total1), 0.0), axis=-1, keepdims=True)
    running = running + total1 + jnp.sum(one2, axis=0, keepdims=True)
    run_ref[...] = running

    fields = (i1 - MOE_GROUPS, i2 - MOE_GROUPS, gate1, gate2, rank1, rank2)
    out = jnp.zeros_like(logits)
    for k, val in enumerate(fields):
        out = jnp.where(lane == k, val, out)
    return out


def _post_mixer_kernel(a_ref, b_ref, h_ref, wa_ref, wb_ref, gxa_ref, wq_ref, kt_ref, v_ref, wo_ref,
                       gffn_ref, whi_ref, wlo_ref, bias_ref, before_ref,
                       h_out_ref, xn_ref, r_ref, cnt_ref, run_ref):
    @pl.when(pl.program_id(0) == 0)
    def _():
        run_ref[...] = jnp.zeros_like(run_ref)

    h = _mixer_out(a_ref, b_ref, h_ref, wa_ref, wb_ref)
    h = _cross_attention(h, gxa_ref, wq_ref, kt_ref, v_ref, wo_ref)
    h_out_ref[...] = h
    xn = _rms(h, gffn_ref[...])
    xn_ref[...] = xn
    r_ref[...] = _route(xn, whi_ref, wlo_ref, bias_ref, before_ref, run_ref)
    cnt_ref[...] = run_ref[...]


def post_mixer(ya, yb, h, wa, wb, g_xa, wq, kt, v, wo, g_ffn, w_hi, w_lo, bias, tm=512):
    m, d = h.shape
    tiles_per_batch = m // kt.shape[0] // tm
    idx = np.arange(tm)
    before = jnp.asarray(idx[:, None] > idx[None, :], BF16)
    rows = lambda w: pl.BlockSpec((tm, w), lambda i: (i, 0))
    const = lambda a: pl.BlockSpec(a.shape, lambda i: (0,) * a.ndim)
    per_batch = lambda a: pl.BlockSpec((None,) + a.shape[1:], lambda i: (i // tiles_per_batch, 0, 0))
    g_xa, g_ffn = g_xa.reshape(1, d), g_ffn.reshape(1, d)
    return pl.pallas_call(
        _post_mixer_kernel,
        grid=(m // tm,),
        in_specs=[rows(ya.shape[1]), rows(yb.shape[1]), rows(d), const(wa), const(wb), const(g_xa), const(wq),
                  per_batch(kt), per_batch(v), const(wo), const(g_ffn), const(w_hi), const(w_lo), const(bias),
                  const(before)],
        out_specs=[rows(d), rows(d), rows(LANES), pl.BlockSpec((1, LANES), lambda i: (0, 0))],
        out_shape=[jax.ShapeDtypeStruct((m, d), F32), jax.ShapeDtypeStruct((m, d), F32),
                   jax.ShapeDtypeStruct((m, LANES), F32), jax.ShapeDtypeStruct((1, LANES), F32)],
        scratch_shapes=[pltpu.VMEM((1, LANES), F32)],
        compiler_params=_cparams("arbitrary"),
        name="post_mixer",
    )(ya, yb, h, wa, wb, g_xa, wq, kt, v, wo, g_ffn, w_hi, w_lo, bias, before)


def _expert_kernel(beid_ref, valid_ref, x_ref, wg_ref, wu_ref, wd_ref, o_ref, wgb_ref, wub_ref, wdb_ref):
    i = pl.program_id(0)
    changed = (i == 0) | (beid_ref[i] != beid_ref[jnp.maximum(i - 1, 0)])
    valid = valid_ref[i]

    @pl.when(changed)
    def _():
        wgb_ref[...] = wg_ref[...].astype(BF16)
        wub_ref[...] = wu_ref[...].astype(BF16)
        wdb_ref[...] = wd_ref[...].astype(BF16)

    @pl.when(valid > 0)
    def _():
        row = lax.broadcasted_iota(jnp.int32, x_ref.shape, 0)
        x = jnp.where(row < valid, x_ref[...], 0.0).astype(BF16)
        gate = jnp.dot(x, wgb_ref[...], preferred_element_type=F32)
        up = jnp.dot(x, wub_ref[...], preferred_element_type=F32)
        act = (_silu(gate) * up).astype(BF16)
        o_ref[...] = jnp.dot(act, wdb_ref[...], preferred_element_type=F32)

    @pl.when(valid == 0)
    def _():
        o_ref[...] = jnp.zeros_like(o_ref)


def moe_experts(block_eid, block_valid, xs, w_gate, w_up, w_down, layer):
    n_slots, d = xs.shape
    rows = MOE_ROWS
    ff = w_gate.shape[3]
    grid_spec = pltpu.PrefetchScalarGridSpec(
        num_scalar_prefetch=2,
        grid=(n_slots // rows,),
        in_specs=[
            pl.BlockSpec((rows, d), lambda i, be, nv: (i, 0)),
            pl.BlockSpec((None, None, d, ff), lambda i, be, nv: (layer, be[i], 0, 0)),
            pl.BlockSpec((None, None, d, ff), lambda i, be, nv: (layer, be[i], 0, 0)),
            pl.BlockSpec((None, None, ff, d), lambda i, be, nv: (layer, be[i], 0, 0)),
        ],
        out_specs=pl.BlockSpec((rows, d), lambda i, be, nv: (i, 0)),
        scratch_shapes=[pltpu.VMEM((d, ff), BF16), pltpu.VMEM((d, ff), BF16), pltpu.VMEM((ff, d), BF16)],
    )
    return pl.pallas_call(
        _expert_kernel,
        grid_spec=grid_spec,
        out_shape=jax.ShapeDtypeStruct((n_slots, d), F32),
        compiler_params=_cparams("arbitrary"),
        name="moe_experts",
    )(block_eid, block_valid, xs, w_gate, w_up, w_down)


def _sc_mesh():
    return plsc.VectorSubcoreMesh(core_axis_name="c", subcore_axis_name="s",
                                  num_cores=SC_CORES, num_subcores=SC_SUBCORES)


def _sc_worker():
    return lax.axis_index("s") * SC_CORES + lax.axis_index("c")


def sc_scatter_rows(x, dest0, dest1, n_slots):
    n_tok, d = x.shape
    per_worker = n_tok // SC_WORKERS
    n_chunks = per_worker // SC_CHUNK
    shape3 = (SC_WORKERS, n_chunks, SC_CHUNK)

    @functools.partial(
        pl.kernel, mesh=_sc_mesh(), out_type=jax.ShapeDtypeStruct((n_slots, d), x.dtype),
        scratch_types=[pltpu.VMEM((n_chunks, SC_CHUNK), jnp.int32), pltpu.VMEM((n_chunks, SC_CHUNK), jnp.int32),
                       pltpu.VMEM((SC_CHUNK, d), x.dtype)],
        name="moe_scatter_rows")
    def scatter(x_hbm, d0_hbm, d1_hbm, out_hbm, i0_v, i1_v, rows_v):
        wid = _sc_worker()
        pltpu.sync_copy(d0_hbm.at[wid], i0_v)
        pltpu.sync_copy(d1_hbm.at[wid], i1_v)

        @pl.loop(0, n_chunks)
        def _(j):
            start = pl.multiple_of(wid * per_worker + j * SC_CHUNK, SC_CHUNK)
            pltpu.sync_copy(x_hbm.at[pl.ds(start, SC_CHUNK)], rows_v)
            pltpu.sync_copy(rows_v, out_hbm.at[i0_v.at[j]])
            pltpu.sync_copy(rows_v, out_hbm.at[i1_v.at[j]])

    return scatter(x, dest0.reshape(shape3), dest1.reshape(shape3))


def sc_gather_rows(table, idx):
    n_out = idx.shape[0]
    d = table.shape[1]
    per_worker = n_out // SC_WORKERS
    n_chunks = per_worker // SC_CHUNK

    @functools.partial(
        pl.kernel, mesh=_sc_mesh(), out_type=jax.ShapeDtypeStruct((n_out, d), table.dtype),
        scratch_types=[pltpu.VMEM((n_chunks, SC_CHUNK), jnp.int32), pltpu.VMEM((SC_CHUNK, d), table.dtype)],
        name="moe_gather_rows")
    def gather(table_hbm, idx_hbm, out_hbm, idx_v, rows_v):
        wid = _sc_worker()
        pltpu.sync_copy(idx_hbm.at[wid], idx_v)

        @pl.loop(0, n_chunks)
        def _(j):
            start = pl.multiple_of(wid * per_worker + j * SC_CHUNK, SC_CHUNK)
            pltpu.sync_copy(table_hbm.at[idx_v.at[j]], rows_v)
            pltpu.sync_copy(rows_v, out_hbm.at[pl.ds(start, SC_CHUNK)])

    return gather(table, idx.reshape(SC_WORKERS, n_chunks, SC_CHUNK))


def _combine_kernel(h_ref, y0_ref, y1_ref, r_ref, g_ref, o_ref, *, final_norm):
    route = r_ref[...]
    h = h_ref[...] + (route[:, 2:3] * y0_ref[...] + route[:, 3:4] * y1_ref[...])
    o_ref[...] = _rms(h, g_ref[...]) if final_norm else h


def moe_combine(h, y01, route, g, final_norm, tm=512):
    m, d = h.shape
    tm = min(tm, m)
    spec = pl.BlockSpec((tm, d), lambda i: (i, 0))
    second = pl.BlockSpec((tm, d), lambda i: (i + m // tm, 0))
    return pl.pallas_call(
        functools.partial(_combine_kernel, final_norm=final_norm),
        grid=(m // tm,),
        in_specs=[spec, spec, second, pl.BlockSpec((tm, LANES), lambda i: (i, 0)),
                  pl.BlockSpec((1, d), lambda i: (0, 0))],
        out_specs=spec,
        out_shape=jax.ShapeDtypeStruct((m, d), F32),
        compiler_params=_cparams("parallel"),
        name="moe_combine",
    )(h, y01, y01, route, g.reshape(1, d))


def _pad_cols(w):
    return jnp.pad(w, ((0, 0), (0, LANES - w.shape[1])))


def _dispatch(route, counts, n_tok):
    rows = MOE_ROWS
    counts = counts[0, MOE_GROUPS:MOE_GROUPS + MOE_EXPERTS].astype(jnp.int32)
    padded = (counts + rows - 1) // rows * rows
    pad_end = jnp.cumsum(padded)
    pad_start = pad_end - padded
    eid = route[:, 0:2].astype(jnp.int32)
    rank = route[:, 4:6].astype(jnp.int32)
    hot = eid[:, :, None] == jnp.arange(MOE_EXPERTS, dtype=jnp.int32)
    dest = jnp.sum(jnp.where(hot, pad_start, 0), axis=-1) + rank
    n_blocks = -(-(2 * n_tok + MOE_EXPERTS * (rows - 1)) // rows)
    block_start = jnp.arange(n_blocks, dtype=jnp.int32) * rows
    block_eid = jnp.minimum(jnp.sum(block_start[:, None] >= pad_end[None, :], axis=-1), MOE_EXPERTS - 1)
    filled = (pad_start + counts)[block_eid]
    block_valid = jnp.clip(filled - block_start, 0, rows)
    return dest[:, 0], dest[:, 1], block_eid.astype(jnp.int32), block_valid.astype(jnp.int32), n_blocks * rows


def _router_weights(w_group, b_group, w_expert, b_expert):
    w_r = _pad_cols(jnp.concatenate([w_group, w_expert], axis=1))
    w_hi = w_r.astype(BF16)
    w_lo = (w_r - w_hi.astype(F32)).astype(BF16)
    return w_hi, w_lo, _pad_lanes(jnp.concatenate([b_group, b_expert]))


def _moe_layer(h, xn, route, counts, w_gate, w_up, w_down, layer, final_g):
    n_tok, d = h.shape
    dest0, dest1, block_eid, block_valid, n_slots = _dispatch(route, counts, n_tok)
    xs = sc_scatter_rows(xn, dest0, dest1, n_slots)
    ys = moe_experts(block_eid, block_valid, xs, w_gate, w_up, w_down, layer)
    y01 = sc_gather_rows(ys, jnp.concatenate([dest0, dest1]))
    g = jnp.ones((d,), F32) if final_g is None else final_g
    return moe_combine(h, y01, route, g, final_g is not None)


def _memory_kv(memn_in, mem_norm, wk, wv):
    bsz, m, d = memn_in.shape
    w = jnp.concatenate([wk, wv], axis=1).astype(BF16)
    kv, _ = rms_matmul(memn_in.reshape(bsz * m, d), mem_norm, w, jnp.zeros((d, LANES), BF16))
    k = kv[:, :d].reshape(bsz, m, d)
    v = kv[:, d:].reshape(bsz, m, d)
    return jnp.swapaxes(k, 1, 2).astype(BF16), v.astype(BF16)


def kernel(x, mem, mem_norm, final_norm, norm_mix, norm_xa, norm_ffn, xa_wq, xa_wk, xa_wv, xa_wo, moe_w_group, moe_b_group, moe_w_expert, moe_b_expert, moe_w_gate, moe_w_up, moe_w_down, ev_w_in, ev_sc_conv, ev_ssm_conv_w, ev_ssm_conv_b, ev_ssm_dt_bias, ev_ssm_a_log, ev_ssm_d, ev_ssm_norm, ev_w_out, od_w_in, od_gdn_conv, od_gdn_dt_bias, od_gdn_a_log, od_gdn_norm, od_w_out):
    bsz, length, d = x.shape
    n_tok = bsz * length
    depth = norm_mix.shape[0]
    h = x.reshape(n_tok, d)
    for layer in range(depth):
        i = layer // 2
        if layer % 2 == 0:
            w = ev_w_in[i]
            w_main = jnp.concatenate([w[:, :3 * SC_DIM], w[:, 3 * SC_DIM + SSM_INNER:3 * SC_DIM + SSM_INNER + SSM_XBC],
                                      w[:, 3 * SC_DIM:3 * SC_DIM + SSM_INNER]], axis=1).astype(BF16)
            w_small = _pad_cols(w[:, 3 * SC_DIM + SSM_INNER + SSM_XBC:]).astype(BF16)
            y, small = rms_matmul(h, norm_mix[layer], w_main, w_small, tm=512, tn=2048)
            y3 = y.reshape(bsz, length, -1)
            small3 = small.reshape(bsz, length, LANES)
            smallt = jnp.swapaxes(small3[:, :, :16], 1, 2)
            ya = short_conv_mixer(y3, ev_sc_conv[i])
            yb = ssd_mixer(y3, small3, smallt, ev_ssm_conv_w[i], ev_ssm_conv_b[i], ev_ssm_dt_bias[i],
                           ev_ssm_a_log[i], ev_ssm_d[i], ev_ssm_norm[i])
            w_out = ev_w_out[i].astype(BF16)
            split = SC_DIM
        else:
            w = od_w_in[i]
            qkvz = 4 * GDN_HEADS * GDN_D
            w_main = jnp.concatenate([w[:, :qkvz], w[:, qkvz + 2 * GDN_HEADS:]], axis=1).astype(BF16)
            w_small = _pad_cols(w[:, qkvz:qkvz + 2 * GDN_HEADS]).astype(BF16)
            y, small = rms_matmul(h, norm_mix[layer], w_main, w_small, tm=512, tn=2816)
            y3 = y.reshape(bsz, length, -1)
            small3 = small.reshape(bsz, length, LANES)
            smallt = jnp.swapaxes(small3[:, :, :16], 1, 2)
            ya = gated_deltanet_mixer(y3, small3, smallt, od_gdn_conv[i], od_gdn_dt_bias[i], od_gdn_a_log[i],
                                      od_gdn_norm[i])
            yb = stick_breaking_mixer(y3, qkvz)
            w_out = od_w_out[i].astype(BF16)
            split = GDN_HEADS * GDN_D
        kt, v = _memory_kv(mem, mem_norm, xa_wk[layer], xa_wv[layer])
        w_hi, w_lo, bias = _router_weights(moe_w_group[layer], moe_b_group[layer], moe_w_expert[layer],
                                           moe_b_expert[layer])
        h, xn, route, counts = post_mixer(
            ya.reshape(n_tok, -1), yb.reshape(n_tok, -1), h, w_out[:split], w_out[split:], norm_xa[layer],
            xa_wq[layer].astype(BF16), kt, v, xa_wo[layer].astype(BF16), norm_ffn[layer], w_hi, w_lo, bias)
        h = _moe_layer(h, xn, route, counts, moe_w_gate, moe_w_up, moe_w_down, layer,
                       final_norm if layer == depth - 1 else None)
    return h.reshape(bsz, length, d)
```

```python
import functools

import jax
import jax.numpy as jnp
import numpy as np
from jax import lax
from jax.experimental import pallas as pl
from jax.experimental.pallas import tpu as pltpu
from jax.experimental.pallas import tpu_sc as plsc

F32 = jnp.float32
BF16 = jnp.bfloat16
EPS = 1e-6

D_MODEL = 1024
MEM_LEN = 256
SC_DIM = 512
SSM_HEADS = 16
SSM_HEAD_DIM = 64
SSM_INNER = 1024
SSM_GROUPS = 2
SSM_STATE = 128
SSM_XBC = SSM_INNER + 2 * SSM_GROUPS * SSM_STATE
SSD_CHUNK = 128
GDN_HEADS = 8
GDN_D = 128
GDN_CHUNK = 64
GDN_TILE = 128
SB_HEADS = 8
SB_HEAD_DIM = 64
SB_DIM = 512
SB_BLOCK = 128
SB_STEP_HEADS = 8
XA_HEADS = 4
XA_HEAD_DIM = 256
MOE_GROUPS = 4
MOE_PER_GROUP = 8
MOE_EXPERTS = 32
MOE_FF = 512
MOE_ROWS = 256
SC_CORES = 2
SC_SUBCORES = 16
SC_WORKERS = SC_CORES * SC_SUBCORES
SC_CHUNK = 32
HALO = 8
CONV_CHUNK = 512
LANES = 128
SB_LOG_ZERO = -104.0
VMEM_LIMIT = 56 * 1024 * 1024


def _cparams(*sem):
    return pltpu.CompilerParams(dimension_semantics=sem, vmem_limit_bytes=VMEM_LIMIT)


def _mm(a, b):
    return jnp.dot(a.astype(BF16), b.astype(BF16), preferred_element_type=F32)


def _mm_nt(a, b):
    return lax.dot_general(a.astype(BF16), b.astype(BF16), (((1,), (1,)), ((), ())),
                           preferred_element_type=F32)


def _split_bf16(x, n):
    parts, r = [], x
    for _ in range(n):
        p = r.astype(BF16)
        parts.append(p)
        r = r - p.astype(F32)
    return parts


def _mm_sel_rhs(x, sel, n=3):
    return sum(jnp.dot(p, sel, preferred_element_type=F32) for p in _split_bf16(x, n))


def _mm_sel_lhs(sel, x, n=3):
    return sum(jnp.dot(sel, p, preferred_element_type=F32) for p in _split_bf16(x, n))


def _silu(x):
    return x * jax.nn.sigmoid(x)


def _softplus(x):
    return jnp.maximum(x, 0.0) + jnp.log(1.0 + jnp.exp(-jnp.abs(x)))


def _rms(x, g):
    return x * lax.rsqrt(jnp.mean(x * x, axis=-1, keepdims=True) + EPS) * g


def _rms_matmul_kernel(x_ref, g_ref, w_ref, ws_ref, o_ref, os_ref):
    xn = _rms(x_ref[...], g_ref[...]).astype(BF16)
    o_ref[...] = jnp.dot(xn, w_ref[...], preferred_element_type=F32)
    os_ref[...] = jnp.dot(xn, ws_ref[...], preferred_element_type=F32)


def rms_matmul(x, g, w, ws, tm=512, tn=512):
    m, k = x.shape
    n = w.shape[1]
    tm = min(tm, m)
    main, small = pl.pallas_call(
        _rms_matmul_kernel,
        grid=(n // tn, m // tm),
        in_specs=[
            pl.BlockSpec((tm, k), lambda j, i: (i, 0)),
            pl.BlockSpec((1, k), lambda j, i: (0, 0)),
            pl.BlockSpec((k, tn), lambda j, i: (0, j)),
            pl.BlockSpec((k, LANES), lambda j, i: (0, 0)),
        ],
        out_specs=[
            pl.BlockSpec((tm, tn), lambda j, i: (i, j)),
            pl.BlockSpec((None, tm, LANES), lambda j, i: (j, i, 0)),
        ],
        out_shape=[jax.ShapeDtypeStruct((m, n), F32), jax.ShapeDtypeStruct((n // tn, m, LANES), F32)],
        compiler_params=_cparams("parallel", "parallel"),
        name="rms_matmul",
    )(x, g.reshape(1, k), w, ws)
    return main, small[0]


def _causal_conv(ext_ref, w_ref, rows):
    width = w_ref.shape[0]
    ext = ext_ref[...]
    acc = None
    for j in range(width):
        shift = width - 1 - j
        moved = ext if shift == 0 else pltpu.roll(ext, shift, axis=0)
        term = w_ref[j:j + 1, :] * moved[HALO:HALO + rows, :]
        acc = term if acc is None else acc + term
    return acc


def _rms_matmul_conv_kernel(x_ref, g_ref, w_ref, cw_ref, cb_ref, o_ref, *ext_refs, tiles_per_seq):
    tm = x_ref.shape[0]
    starts_sequence = pl.program_id(1) % tiles_per_seq == 0

    @pl.when(starts_sequence)
    def _():
        for ext_ref in ext_refs:
            ext_ref[0:HALO, :] = jnp.zeros((HALO, CONV_CHUNK), F32)

    @pl.when(jnp.logical_not(starts_sequence))
    def _():
        for ext_ref in ext_refs:
            ext_ref[0:HALO, :] = ext_ref[tm:tm + HALO, :]

    xn = _rms(x_ref[...], g_ref[...]).astype(BF16)
    for c, ext_ref in enumerate(ext_refs):
        cols = slice(c * CONV_CHUNK, (c + 1) * CONV_CHUNK)
        ext_ref[HALO:, :] = jnp.dot(xn, w_ref[:, cols], preferred_element_type=F32)
        o_ref[:, cols] = _silu(_causal_conv(ext_ref, cw_ref.at[:, cols], tm) + cb_ref[:, cols])


def rms_matmul_conv(x, g, w, conv_w, conv_b, seq_len, tm=512, tn=1536):
    m, k = x.shape
    n = w.shape[1]
    cols = lambda rows: pl.BlockSpec((rows, tn), lambda j, i: (0, j))
    return pl.pallas_call(
        functools.partial(_rms_matmul_conv_kernel, tiles_per_seq=seq_len // tm),
        grid=(n // tn, m // tm),
        in_specs=[
            pl.BlockSpec((tm, k), lambda j, i: (i, 0)),
            pl.BlockSpec((1, k), lambda j, i: (0, 0)),
            cols(k), cols(conv_w.shape[0]), cols(1),
        ],
        out_specs=pl.BlockSpec((tm, tn), lambda j, i: (i, j)),
        out_shape=jax.ShapeDtypeStruct((m, n), F32),
        scratch_shapes=[pltpu.VMEM((tm + HALO, CONV_CHUNK), F32)] * (tn // CONV_CHUNK),
        compiler_params=_cparams("arbitrary", "arbitrary"),
        name="rms_matmul_conv",
    )(x, g.reshape(1, k), w, conv_w, conv_b.reshape(1, n))


def _halo_index(rows):
    step = rows // HALO
    return lambda i: jnp.maximum(i * step - 1, 0)


def _sc_kernel(b_ref, c_ref, x_ref, ch_ref, xh_ref, w_ref, o_ref, ext_ref):
    rows = o_ref.shape[0]
    first = pl.program_id(1) == 0
    ext_ref[0:HALO, :] = jnp.where(first, 0.0, ch_ref[...] * xh_ref[...])
    ext_ref[HALO:, :] = c_ref[...] * x_ref[...]
    o_ref[...] = b_ref[...] * _causal_conv(ext_ref, w_ref, rows)


def short_conv_mixer(y3, w, col0, tl=512):
    bsz, length, _ = y3.shape
    tl = min(tl, length)
    hidx = _halo_index(tl)
    return pl.pallas_call(
        _sc_kernel,
        grid=(bsz, length // tl),
        in_specs=[
            pl.BlockSpec((None, tl, SC_DIM), lambda b, i: (b, i, col0)),
            pl.BlockSpec((None, tl, SC_DIM), lambda b, i: (b, i, col0 + 1)),
            pl.BlockSpec((None, tl, SC_DIM), lambda b, i: (b, i, col0 + 2)),
            pl.BlockSpec((None, HALO, SC_DIM), lambda b, i: (b, hidx(i), col0 + 1)),
            pl.BlockSpec((None, HALO, SC_DIM), lambda b, i: (b, hidx(i), col0 + 2)),
            pl.BlockSpec(w.shape, lambda b, i: (0, 0)),
        ],
        out_specs=pl.BlockSpec((None, tl, SC_DIM), lambda b, i: (b, i, 0)),
        out_shape=jax.ShapeDtypeStruct((bsz, length, SC_DIM), F32),
        scratch_shapes=[pltpu.VMEM((tl + HALO, SC_DIM), F32)],
        compiler_params=_cparams("parallel", "arbitrary"),
        name="short_conv_mixer",
    )(y3, y3, y3, y3, y3, w)


def _ssd_kernel(xbc_ref, z_ref, dt_ref, dtt_ref, dtb_r_ref, dtb_c_ref,
                alog_r_ref, alog_c_ref, d_ref, nw_ref, tri_ref, trit_ref, eh_ref, eq_ref,
                o_ref, s_ref):
    q = SSD_CHUNK
    hpg = SSM_HEADS // SSM_GROUPS
    gw = hpg * SSM_HEAD_DIM

    @pl.when(pl.program_id(1) == 0)
    def _():
        s_ref[...] = jnp.zeros_like(s_ref)

    xbc = xbc_ref[...]
    xs = xbc[:, :SSM_INNER]
    bm = xbc[:, SSM_INNER:SSM_INNER + SSM_GROUPS * SSM_STATE]
    cm = xbc[:, SSM_INNER + SSM_GROUPS * SSM_STATE:]

    dt = _softplus(dt_ref[...] + dtb_r_ref[...])
    acs = _mm_sel_lhs(tri_ref[...], dt * -jnp.exp(alog_r_ref[...]))
    dtt = _softplus(dtt_ref[...] + dtb_c_ref[...])
    acst = _mm_sel_rhs(dtt * -jnp.exp(alog_c_ref[...]), trit_ref[...])
    dt_full = _mm_sel_rhs(dt, eh_ref[...])
    acs_full = _mm_sel_rhs(acs, eh_ref[...])
    acs_col = _mm_sel_rhs(acs, eq_ref[...])

    xdt = xs * dt_full
    acs_last = acs_full[q - 1:q, :]
    xw = xdt * jnp.exp(acs_last - acs_full)
    chunk_decay = jnp.exp(acs_last)

    row = lax.broadcasted_iota(jnp.int32, (q, q), 0)
    col = lax.broadcasted_iota(jnp.int32, (q, q), 1)
    causal = row >= col
    lane = lax.broadcasted_iota(jnp.int32, (q, 2 * SSM_HEAD_DIM), 1)

    y_diag, y_off = [], []
    for g in range(SSM_GROUPS):
        bm_g = bm[:, g * SSM_STATE:(g + 1) * SSM_STATE]
        cm_g = cm[:, g * SSM_STATE:(g + 1) * SSM_STATE]
        cb_g = _mm_nt(cm_g, bm_g)
        state = s_ref[g]
        y_off.append(_mm(cm_g, state))
        s_ref[g] = state * chunk_decay[:, g * gw:(g + 1) * gw] + _mm(bm_g.T, xw[:, g * gw:(g + 1) * gw])
        for pair in range(hpg // 2):
            h0 = g * hpg + 2 * pair
            xdt_pair = xdt[:, h0 * SSM_HEAD_DIM:(h0 + 2) * SSM_HEAD_DIM]
            outs = []
            for h in (h0, h0 + 1):
                seg = acs_col[:, h * q:(h + 1) * q] - acst[h:h + 1, :]
                decay = jnp.where(causal, jnp.exp(seg), 0.0)
                outs.append(_mm(cb_g * decay, xdt_pair))
            y_diag.append(jnp.where(lane < SSM_HEAD_DIM, outs[0], outs[1]))
    y = (jnp.concatenate(y_diag, axis=1) + jnp.concatenate(y_off, axis=1) * jnp.exp(acs_full)
         + xs * d_ref[...])
    y = y * _silu(z_ref[...])
    halves = []
    for g in range(SSM_GROUPS):
        yg = y[:, g * gw:(g + 1) * gw]
        halves.append(yg * lax.rsqrt(jnp.mean(yg * yg, axis=-1, keepdims=True) + EPS))
    o_ref[...] = jnp.concatenate(halves, axis=1) * nw_ref[...]


def _pad_lanes(v, fill=0.0):
    return jnp.pad(v.astype(F32), (0, LANES - v.shape[0]), constant_values=fill).reshape(1, LANES)


def _pad_col(v, rows=16):
    return jnp.pad(v.astype(F32), (0, rows - v.shape[0])).reshape(rows, 1)


def ssd_mixer(xbc3, y3, small3, smallt, dt_bias, a_log, d_skip, norm_w):
    bsz, length, _ = y3.shape
    q = SSD_CHUNK
    tri = jnp.asarray(np.tril(np.ones((q, q), np.float32)), BF16)
    trit = jnp.asarray(np.triu(np.ones((q, q), np.float32)), BF16)
    heads = np.arange(LANES)[:, None]
    eh = jnp.asarray(heads == (np.arange(SSM_INNER)[None, :] // SSM_HEAD_DIM), BF16)
    eq = jnp.asarray(heads == (np.arange(SSM_HEADS * q)[None, :] // q), BF16)
    d_full = jnp.repeat(d_skip.astype(F32), SSM_HEAD_DIM).reshape(1, SSM_INNER)
    const = lambda a: pl.BlockSpec(a.shape, lambda b, c: (0,) * a.ndim)
    args = [_pad_lanes(dt_bias), _pad_col(dt_bias), _pad_lanes(a_log),
            _pad_col(a_log), d_full, norm_w.reshape(1, -1), tri, trit, eh, eq]
    return pl.pallas_call(
        _ssd_kernel,
        grid=(bsz, length // q),
        in_specs=[
            pl.BlockSpec((None, q, SSM_XBC), lambda b, c: (b, c, 0)),
            pl.BlockSpec((None, q, SSM_INNER), lambda b, c: (b, c, 0)),
            pl.BlockSpec((None, q, LANES), lambda b, c: (b, c, 0)),
            pl.BlockSpec((None, 16, q), lambda b, c: (b, 0, c)),
        ] + [const(a) for a in args],
        out_specs=pl.BlockSpec((None, q, SSM_INNER), lambda b, c: (b, c, 0)),
        out_shape=jax.ShapeDtypeStruct((bsz, length, SSM_INNER), F32),
        scratch_shapes=[pltpu.VMEM((SSM_GROUPS, SSM_STATE, SSM_INNER // SSM_GROUPS), F32)],
        compiler_params=_cparams("parallel", "arbitrary"),
        name="ssd_mixer",
    )(xbc3, y3, small3, smallt, *args)


def _unit_lower_inverse(mats, row, col):
    eye = jnp.where(row == col, 1.0, 0.0)
    blk = lambda n: (row >> (n.bit_length() - 1)) == (col >> (n.bit_length() - 1))
    p = [jnp.where(blk(16), -a, 0.0) for a in mats]
    t = [eye + x for x in p]
    for _ in range(3):
        p = [_mm(x, x) for x in p]
        t = [y + _mm(y, x) for y, x in zip(t, p)]
    for n in (16, 32):
        band = blk(2 * n) & jnp.logical_not(blk(n))
        left = [_mm(y, jnp.where(band, a, 0.0)) for y, a in zip(t, mats)]
        t = [y - _mm(x, y) for y, x in zip(t, left)]
    return t


def _gdn_kernel(qkv_ref, z_ref, ab_ref, abt_ref, dtb_r_ref, dtb_c_ref, alog_r_ref,
                alog_c_ref, nw_ref, tri_ref, trit_ref, eg_ref, eb_ref, o_ref, s_ref):
    n = GDN_TILE
    c = GDN_CHUNK
    d = GDN_D
    hd = GDN_HEADS * d

    @pl.when(pl.program_id(1) == 0)
    def _():
        s_ref[...] = jnp.zeros_like(s_ref)

    qkv = qkv_ref[...]
    z = z_ref[...]

    ab = ab_ref[...]
    g = -jnp.exp(alog_r_ref[...]) * _softplus(ab + dtb_r_ref[...])
    gc_full = _mm_sel_rhs(_mm_sel_lhs(tri_ref[...], g), eg_ref[...])
    beta_full = _mm_sel_rhs(jax.nn.sigmoid(ab), eb_ref[...])
    gt = -jnp.exp(alog_c_ref[...]) * _softplus(abt_ref[...] + dtb_c_ref[...])
    gct = _mm_sel_rhs(gt, trit_ref[...])

    row = lax.broadcasted_iota(jnp.int32, (n, n), 0)
    col = lax.broadcasted_iota(jnp.int32, (n, n), 1)
    same = (row >> (c.bit_length() - 1)) == (col >> (c.bit_length() - 1))
    incl = same & (row >= col)
    strict = same & (row > col)
    zeros_half = jnp.zeros((c, d), F32)

    heads = range(GDN_HEADS)
    sl = [slice(h * d, (h + 1) * d) for h in heads]
    l2n = lambda x: x * lax.rsqrt(jnp.sum(x * x, axis=-1, keepdims=True) + EPS)
    qn = [l2n(qkv[:, sl[h]]) * (d ** -0.5) for h in heads]
    kn = [l2n(qkv[:, hd + h * d:hd + (h + 1) * d]) for h in heads]
    vh = [qkv[:, 2 * hd + h * d:2 * hd + (h + 1) * d] for h in heads]
    gcol = [gc_full[:, sl[h]] for h in heads]
    beta = [beta_full[:, sl[h]] for h in heads]
    edec = [jnp.exp(gcol[h] - gct[h:h + 1, :]) for h in heads]
    egc = [jnp.exp(x) for x in gcol]
    kb = [kn[h] * beta[h] for h in heads]
    lower = [jnp.where(strict, _mm_nt(kb[h], kn[h]) * edec[h], 0.0) for h in heads]
    aqk = [jnp.where(incl, _mm_nt(qn[h], kn[h]) * edec[h], 0.0) for h in heads]
    tinv = _unit_lower_inverse(lower, row, col)
    sol = [_mm(tinv[h], jnp.concatenate([vh[h] * beta[h], kb[h] * egc[h]], axis=1)) for h in heads]
    qd = [qn[h] * egc[h] for h in heads]
    glast = [(gcol[h][c - 1:c, :], gcol[h][n - 1:n, :]) for h in heads]
    kdt = [(kn[h] * jnp.exp(jnp.concatenate([jnp.broadcast_to(glast[h][0], (c, d)),
                                             jnp.broadcast_to(glast[h][1], (c, d))], axis=0) - gcol[h])).T
           for h in heads]
    s0 = [s_ref[h] for h in heads]
    v0 = [sol[h][:c, :d] - _mm(sol[h][:c, d:], s0[h]) for h in heads]
    s1 = [s0[h] * jnp.exp(glast[h][0]) + _mm(kdt[h], jnp.concatenate([v0[h], zeros_half], axis=0)) for h in heads]
    v1 = [sol[h][c:, :d] - _mm(sol[h][c:, d:], s1[h]) for h in heads]
    for h in heads:
        s_ref[h] = s1[h] * jnp.exp(glast[h][1]) + _mm(kdt[h], jnp.concatenate([zeros_half, v1[h]], axis=0))
    outs = []
    for h in heads:
        o = (jnp.concatenate([_mm(qd[h][:c], s0[h]), _mm(qd[h][c:], s1[h])], axis=0)
             + _mm(aqk[h], jnp.concatenate([v0[h], v1[h]], axis=0)))
        o = o * lax.rsqrt(jnp.mean(o * o, axis=-1, keepdims=True) + EPS) * nw_ref[...]
        outs.append(o * _silu(z[:, sl[h]]))
    o_ref[...] = jnp.concatenate(outs, axis=1)


def gated_deltanet_mixer(qkv3, y3, small3, smallt, dt_bias, a_log, norm_w):
    bsz, length, _ = y3.shape
    n = GDN_TILE
    hd = GDN_HEADS * GDN_D
    idx = np.arange(n)
    same = (idx[:, None] // GDN_CHUNK) == (idx[None, :] // GDN_CHUNK)
    tri = jnp.asarray(same & (idx[:, None] >= idx[None, :]), BF16)
    trit = jnp.asarray(same & (idx[:, None] <= idx[None, :]), BF16)
    lanes = np.arange(LANES)[:, None]
    heads = np.arange(hd)[None, :] // GDN_D
    eg = jnp.asarray(lanes == heads, BF16)
    eb = jnp.asarray(lanes == heads + GDN_HEADS, BF16)
    const = lambda a: pl.BlockSpec(a.shape, lambda b, c: (0,) * a.ndim)
    args = [_pad_lanes(dt_bias), _pad_col(dt_bias), _pad_lanes(a_log), _pad_col(a_log),
            norm_w.reshape(1, -1), tri, trit, eg, eb]
    return pl.pallas_call(
        _gdn_kernel,
        grid=(bsz, length // n),
        in_specs=[
            pl.BlockSpec((None, n, 3 * hd), lambda b, c: (b, c, 0)),
            pl.BlockSpec((None, n, hd), lambda b, c: (b, c, 0)),
            pl.BlockSpec((None, n, LANES), lambda b, c: (b, c, 0)),
            pl.BlockSpec((None, 16, n), lambda b, c: (b, 0, c)),
        ] + [const(a) for a in args],
        out_specs=pl.BlockSpec((None, n, hd), lambda b, c: (b, c, 0)),
        out_shape=jax.ShapeDtypeStruct((bsz, length, hd), F32),
        scratch_shapes=[pltpu.VMEM((GDN_HEADS, GDN_D, GDN_D), F32)],
        compiler_params=_cparams("parallel", "arbitrary"),
        name="gated_deltanet",
    )(qkv3, y3, small3, smallt, *args)


def _sb_kernel(q_ref, k_ref, v_ref, upper_ref, o_ref):
    blk = SB_BLOCK
    pair_w = 2 * SB_HEAD_DIM
    n_pairs = SB_STEP_HEADS // 2
    i = pl.program_id(2)
    q = q_ref[...] * (SB_HEAD_DIM ** -0.5)
    lane = lax.broadcasted_iota(jnp.int32, (blk, pair_w), 1)
    first_head = lane < SB_HEAD_DIM
    qs = []
    for p in range(n_pairs):
        q2 = q[:, p * pair_w:(p + 1) * pair_w]
        qs += [jnp.where(first_head, q2, 0.0).astype(BF16), jnp.where(first_head, 0.0, q2).astype(BF16)]
    row = lax.broadcasted_iota(jnp.int32, (blk, blk), 0)
    col = lax.broadcasted_iota(jnp.int32, (blk, blk), 1)
    earlier = col < row
    upper = upper_ref[...]
    heads = range(SB_STEP_HEADS)

    def key_block(kb, accs, sticks, diagonal):
        start = pl.multiple_of(kb * blk, blk)
        k = k_ref[pl.ds(start, blk), :].astype(BF16)
        v = v_ref[pl.ds(start, blk), :].astype(BF16)
        kp = [k[:, p * pair_w:(p + 1) * pair_w] for p in range(n_pairs)]
        vp = [v[:, p * pair_w:(p + 1) * pair_w] for p in range(n_pairs)]
        logits = [lax.dot_general(qs[h], kp[h // 2], (((1,), (1,)), ((), ())), preferred_element_type=F32)
                  for h in heads]
        log_keep = [-_softplus(x) for x in logits]
        if diagonal:
            log_keep = [jnp.where(earlier, x, 0.0) for x in log_keep]
        between = [_mm_sel_rhs(log_keep[h], upper, 2) + sticks[h] for h in heads]
        w = [jnp.exp(logits[h] + log_keep[h] + between[h]) for h in heads]
        if diagonal:
            w = [jnp.where(earlier, x, 0.0) for x in w]
        pv = [jnp.dot(w[h].astype(BF16), vp[h // 2], preferred_element_type=F32) for h in heads]
        accs = tuple(accs[p] + jnp.where(first_head, pv[2 * p], pv[2 * p + 1]) for p in range(n_pairs))
        sticks = tuple(sticks[h] + jnp.sum(log_keep[h], axis=-1, keepdims=True) for h in heads)
        return accs, sticks

    accs = tuple(jnp.zeros((blk, pair_w), F32) for _ in range(n_pairs))
    sticks = tuple(jnp.zeros((blk, 1), F32) for _ in heads)
    accs, sticks = key_block(i, accs, sticks, True)

    def alive(state):
        kb, _, sticks = state
        longest = sticks[0]
        for s in sticks[1:]:
            longest = jnp.maximum(longest, s)
        return (kb >= 0) & (jnp.max(longest) > SB_LOG_ZERO)

    def body(state):
        kb, accs, sticks = state
        accs, sticks = key_block(kb, accs, sticks, False)
        return kb - 1, accs, sticks

    _, accs, _ = lax.while_loop(alive, body, (i - 1, accs, sticks))
    o_ref[...] = jnp.concatenate(accs, axis=1)


def stick_breaking_mixer(y3, col0):
    bsz, length, _ = y3.shape
    blk = SB_BLOCK
    step_w = SB_STEP_HEADS * SB_HEAD_DIM
    steps = SB_DIM // step_w
    q0 = col0 // step_w
    idx = np.arange(blk)
    upper = jnp.asarray(idx[:, None] > idx[None, :], BF16)
    resident = lambda off: pl.BlockSpec((None, length, step_w), lambda b, p, i: (b, 0, q0 + off + p),
                                        pipeline_mode=pl.Buffered(1))
    return pl.pallas_call(
        _sb_kernel,
        grid=(bsz, steps, length // blk),
        in_specs=[
            pl.BlockSpec((None, blk, step_w), lambda b, p, i: (b, i, q0 + p)),
            resident(steps),
            resident(2 * steps),
            pl.BlockSpec((blk, blk), lambda b, p, i: (0, 0)),
        ],
        out_specs=pl.BlockSpec((None, blk, step_w), lambda b, p, i: (b, i, p)),
        out_shape=jax.ShapeDtypeStruct((bsz, length, SB_DIM), F32),
        compiler_params=_cparams("parallel", "parallel", "arbitrary"),
        name="stick_breaking",
    )(y3, y3, y3, upper)


def _mixer_out(a_ref, b_ref, h_ref, wa_ref, wb_ref):
    return h_ref[...] + (jnp.dot(a_ref[...].astype(BF16), wa_ref[...], preferred_element_type=F32)
                         + jnp.dot(b_ref[...].astype(BF16), wb_ref[...], preferred_element_type=F32))


def _cross_attention(h, g_ref, wq_ref, kt_ref, v_ref, wo_ref):
    u = _rms(h, g_ref[...]).astype(BF16)
    q = jnp.dot(u, wq_ref[...], preferred_element_type=F32)
    heads = []
    for hd in range(XA_HEADS):
        sl = slice(hd * XA_HEAD_DIM, (hd + 1) * XA_HEAD_DIM)
        s = jnp.dot(q[:, sl].astype(BF16), kt_ref[sl, :], preferred_element_type=F32) * (XA_HEAD_DIM ** -0.5)
        p = jnp.exp(s - jnp.max(s, axis=-1, keepdims=True))
        p = p / jnp.sum(p, axis=-1, keepdims=True)
        heads.append(jnp.dot(p.astype(BF16), v_ref[:, sl], preferred_element_type=F32))
    o = jnp.concatenate(heads, axis=1).astype(BF16)
    return h + jnp.dot(o, wo_ref[...], preferred_element_type=F32)


def _route(xn, whi_ref, wlo_ref, b_ref, before_ref, run_ref):
    x_hi = xn.astype(BF16)
    x_lo = (xn - x_hi.astype(F32)).astype(BF16)
    logits = (jnp.dot(x_hi, whi_ref[...], preferred_element_type=F32)
              + jnp.dot(x_lo, whi_ref[...], preferred_element_type=F32)
              + jnp.dot(x_hi, wlo_ref[...], preferred_element_type=F32) + b_ref[...])
    lane = lax.broadcasted_iota(jnp.int32, logits.shape, 1).astype(F32)
    neg = -1e30
    none = float(LANES)

    def top(vals):
        best = jnp.max(vals, axis=-1, keepdims=True)
        where = jnp.min(jnp.where(vals == best, lane, none), axis=-1, keepdims=True)
        return best, where

    gl = jnp.where(lane < MOE_GROUPS, logits, neg)
    gbest, gsel = top(gl)
    gprob = 1.0 / jnp.sum(jnp.exp(gl - gbest), axis=-1, keepdims=True)
    lo = MOE_GROUPS + gsel * MOE_PER_GROUP
    el = jnp.where((lane >= lo) & (lane < lo + MOE_PER_GROUP), logits, neg)
    m1, i1 = top(el)
    m2, i2 = top(jnp.where(lane == i1, neg, el))
    e = jnp.exp(m2 - m1)
    gate1 = gprob / (1.0 + e)
    gate2 = gprob * e / (1.0 + e)

    hot1 = lane == i1
    hot2 = lane == i2
    one1 = jnp.where(hot1, 1.0, 0.0)
    one2 = jnp.where(hot2, 1.0, 0.0)
    before = before_ref[...]
    prefix1 = jnp.dot(before, one1.astype(BF16), preferred_element_type=F32)
    prefix2 = jnp.dot(before, one2.astype(BF16), preferred_element_type=F32)
    total1 = jnp.sum(one1, axis=0, keepdims=True)
    running = run_ref[...]
    rank1 = jnp.sum(jnp.where(hot1, prefix1 + running, 0.0), axis=-1, keepdims=True)
    rank2 = jnp.sum(jnp.where(hot2, prefix2 + (running + total1), 0.0), axis=-1, keepdims=True)
    running = running + total1 + jnp.sum(one2, axis=0, keepdims=True)
    run_ref[...] = running

    fields = (i1 - MOE_GROUPS, i2 - MOE_GROUPS, gate1, gate2, rank1, rank2)
    out = jnp.zeros_like(logits)
    for k, val in enumerate(fields):
        out = jnp.where(lane == k, val, out)
    return out


def _post_mixer_kernel(a_ref, b_ref, h_ref, wa_ref, wb_ref, gxa_ref, wq_ref, kt_ref, v_ref, wo_ref,
                       gffn_ref, whi_ref, wlo_ref, bias_ref, before_ref,
                       h_out_ref, xn_ref, r_ref, cnt_ref, run_ref):
    @pl.when(pl.program_id(0) == 0)
    def _():
        run_ref[...] = jnp.zeros_like(run_ref)

    h = _mixer_out(a_ref, b_ref, h_ref, wa_ref, wb_ref)
    h = _cross_attention(h, gxa_ref, wq_ref, kt_ref, v_ref, wo_ref)
    h_out_ref[...] = h
    xn = _rms(h, gffn_ref[...])
    xn_ref[...] = xn
    r_ref[...] = _route(xn, whi_ref, wlo_ref, bias_ref, before_ref, run_ref)
    cnt_ref[...] = run_ref[...]


def post_mixer(ya, yb, h, wa, wb, g_xa, wq, kt, v, wo, g_ffn, w_hi, w_lo, bias, tm=512):
    m, d = h.shape
    tiles_per_batch = m // kt.shape[0] // tm
    idx = np.arange(tm)
    before = jnp.asarray(idx[:, None] > idx[None, :], BF16)
    rows = lambda w: pl.BlockSpec((tm, w), lambda i: (i, 0))
    const = lambda a: pl.BlockSpec(a.shape, lambda i: (0,) * a.ndim)
    per_batch = lambda a: pl.BlockSpec((None,) + a.shape[1:], lambda i: (i // tiles_per_batch, 0, 0))
    g_xa, g_ffn = g_xa.reshape(1, d), g_ffn.reshape(1, d)
    return pl.pallas_call(
        _post_mixer_kernel,
        grid=(m // tm,),
        in_specs=[rows(ya.shape[1]), rows(yb.shape[1]), rows(d), const(wa), const(wb), const(g_xa), const(wq),
                  per_batch(kt), per_batch(v), const(wo), const(g_ffn), const(w_hi), const(w_lo), const(bias),
                  const(before)],
        out_specs=[rows(d), rows(d), rows(LANES), pl.BlockSpec((1, LANES), lambda i: (0, 0))],
        out_shape=[jax.ShapeDtypeStruct((m, d), F32), jax.ShapeDtypeStruct((m, d), F32),
                   jax.ShapeDtypeStruct((m, LANES), F32), jax.ShapeDtypeStruct((1, LANES), F32)],
        scratch_shapes=[pltpu.VMEM((1, LANES), F32)],
        compiler_params=_cparams("arbitrary"),
        name="post_mixer",
    )(ya, yb, h, wa, wb, g_xa, wq, kt, v, wo, g_ffn, w_hi, w_lo, bias, before)


def _expert_kernel(beid_ref, valid_ref, x_ref, wg_ref, wu_ref, wd_ref, o_ref, wgb_ref, wub_ref, wdb_ref):
    i = pl.program_id(0)
    changed = (i == 0) | (beid_ref[i] != beid_ref[jnp.maximum(i - 1, 0)])
    valid = valid_ref[i]

    @pl.when(changed)
    def _():
        wgb_ref[...] = wg_ref[...].astype(BF16)
        wub_ref[...] = wu_ref[...].astype(BF16)
        wdb_ref[...] = wd_ref[...].astype(BF16)

    @pl.when(valid > 0)
    def _():
        row = lax.broadcasted_iota(jnp.int32, x_ref.shape, 0)
        x = jnp.where(row < valid, x_ref[...], 0.0).astype(BF16)
        gate = jnp.dot(x, wgb_ref[...], preferred_element_type=F32)
        up = jnp.dot(x, wub_ref[...], preferred_element_type=F32)
        act = (_silu(gate) * up).astype(BF16)
        o_ref[...] = jnp.dot(act, wdb_ref[...], preferred_element_type=F32)

    @pl.when(valid == 0)
    def _():
        o_ref[...] = jnp.zeros_like(o_ref)


def moe_experts(block_eid, block_valid, xs, w_gate, w_up, w_down, layer):
    n_slots, d = xs.shape
    rows = MOE_ROWS
    ff = w_gate.shape[3]
    grid_spec = pltpu.PrefetchScalarGridSpec(
        num_scalar_prefetch=2,
        grid=(n_slots // rows,),
        in_specs=[
            pl.BlockSpec((rows, d), lambda i, be, nv: (i, 0)),
            pl.BlockSpec((None, None, d, ff), lambda i, be, nv: (layer, be[i], 0, 0)),
            pl.BlockSpec((None, None, d, ff), lambda i, be, nv: (layer, be[i], 0, 0)),
            pl.BlockSpec((None, None, ff, d), lambda i, be, nv: (layer, be[i], 0, 0)),
        ],
        out_specs=pl.BlockSpec((rows, d), lambda i, be, nv: (i, 0)),
        scratch_shapes=[pltpu.VMEM((d, ff), BF16), pltpu.VMEM((d, ff), BF16), pltpu.VMEM((ff, d), BF16)],
    )
    return pl.pallas_call(
        _expert_kernel,
        grid_spec=grid_spec,
        out_shape=jax.ShapeDtypeStruct((n_slots, d), F32),
        compiler_params=_cparams("arbitrary"),
        name="moe_experts",
    )(block_eid, block_valid, xs, w_gate, w_up, w_down)


def _sc_mesh():
    return plsc.VectorSubcoreMesh(core_axis_name="c", subcore_axis_name="s",
                                  num_cores=SC_CORES, num_subcores=SC_SUBCORES)


def _sc_worker():
    return lax.axis_index("s") * SC_CORES + lax.axis_index("c")


def sc_scatter_rows(x, dest0, dest1, n_slots):
    n_tok, d = x.shape
    per_worker = n_tok // SC_WORKERS
    n_chunks = per_worker // SC_CHUNK
    shape3 = (SC_WORKERS, n_chunks, SC_CHUNK)

    @functools.partial(
        pl.kernel, mesh=_sc_mesh(), out_type=jax.ShapeDtypeStruct((n_slots, d), x.dtype),
        scratch_types=[pltpu.VMEM((n_chunks, SC_CHUNK), jnp.int32), pltpu.VMEM((n_chunks, SC_CHUNK), jnp.int32),
                       pltpu.VMEM((SC_CHUNK, d), x.dtype)],
        name="moe_scatter_rows")
    def scatter(x_hbm, d0_hbm, d1_hbm, out_hbm, i0_v, i1_v, rows_v):
        wid = _sc_worker()
        pltpu.sync_copy(d0_hbm.at[wid], i0_v)
        pltpu.sync_copy(d1_hbm.at[wid], i1_v)

        @pl.loop(0, n_chunks)
        def _(j):
            start = pl.multiple_of(wid * per_worker + j * SC_CHUNK, SC_CHUNK)
            pltpu.sync_copy(x_hbm.at[pl.ds(start, SC_CHUNK)], rows_v)
            pltpu.sync_copy(rows_v, out_hbm.at[i0_v.at[j]])
            pltpu.sync_copy(rows_v, out_hbm.at[i1_v.at[j]])

    return scatter(x, dest0.reshape(shape3), dest1.reshape(shape3))


def sc_gather_rows(table, idx):
    n_out = idx.shape[0]
    d = table.shape[1]
    per_worker = n_out // SC_WORKERS
    n_chunks = per_worker // SC_CHUNK

    @functools.partial(
        pl.kernel, mesh=_sc_mesh(), out_type=jax.ShapeDtypeStruct((n_out, d), table.dtype),
        scratch_types=[pltpu.VMEM((n_chunks, SC_CHUNK), jnp.int32), pltpu.VMEM((SC_CHUNK, d), table.dtype)],
        name="moe_gather_rows")
    def gather(table_hbm, idx_hbm, out_hbm, idx_v, rows_v):
        wid = _sc_worker()
        pltpu.sync_copy(idx_hbm.at[wid], idx_v)

        @pl.loop(0, n_chunks)
        def _(j):
            start = pl.multiple_of(wid * per_worker + j * SC_CHUNK, SC_CHUNK)
            pltpu.sync_copy(table_hbm.at[idx_v.at[j]], rows_v)
            pltpu.sync_copy(rows_v, out_hbm.at[pl.ds(start, SC_CHUNK)])

    return gather(table, idx.reshape(SC_WORKERS, n_chunks, SC_CHUNK))


def _combine_kernel(h_ref, y0_ref, y1_ref, r_ref, g_ref, o_ref, *, final_norm):
    route = r_ref[...]
    h = h_ref[...] + (route[:, 2:3] * y0_ref[...] + route[:, 3:4] * y1_ref[...])
    o_ref[...] = _rms(h, g_ref[...]) if final_norm else h


def moe_combine(h, y01, route, g, final_norm, tm=512):
    m, d = h.shape
    tm = min(tm, m)
    spec = pl.BlockSpec((tm, d), lambda i: (i, 0))
    second = pl.BlockSpec((tm, d), lambda i: (i + m // tm, 0))
    return pl.pallas_call(
        functools.partial(_combine_kernel, final_norm=final_norm),
        grid=(m // tm,),
        in_specs=[spec, spec, second, pl.BlockSpec((tm, LANES), lambda i: (i, 0)),
                  pl.BlockSpec((1, d), lambda i: (0, 0))],
        out_specs=spec,
        out_shape=jax.ShapeDtypeStruct((m, d), F32),
        compiler_params=_cparams("parallel"),
        name="moe_combine",
    )(h, y01, y01, route, g.reshape(1, d))


def _pad_cols(w):
    return jnp.pad(w, ((0, 0), (0, LANES - w.shape[1])))


def _dispatch(route, counts, n_tok):
    rows = MOE_ROWS
    counts = counts[0, MOE_GROUPS:MOE_GROUPS + MOE_EXPERTS].astype(jnp.int32)
    padded = (counts + rows - 1) // rows * rows
    pad_end = jnp.cumsum(padded)
    pad_start = pad_end - padded
    eid = route[:, 0:2].astype(jnp.int32)
    rank = route[:, 4:6].astype(jnp.int32)
    hot = eid[:, :, None] == jnp.arange(MOE_EXPERTS, dtype=jnp.int32)
    dest = jnp.sum(jnp.where(hot, pad_start, 0), axis=-1) + rank
    n_blocks = -(-(2 * n_tok + MOE_EXPERTS * (rows - 1)) // rows)
    block_start = jnp.arange(n_blocks, dtype=jnp.int32) * rows
    block_eid = jnp.minimum(jnp.sum(block_start[:, None] >= pad_end[None, :], axis=-1), MOE_EXPERTS - 1)
    filled = (pad_start + counts)[block_eid]
    block_valid = jnp.clip(filled - block_start, 0, rows)
    return dest[:, 0], dest[:, 1], block_eid.astype(jnp.int32), block_valid.astype(jnp.int32), n_blocks * rows


def _router_weights(w_group, b_group, w_expert, b_expert):
    w_r = _pad_cols(jnp.concatenate([w_group, w_expert], axis=1))
    w_hi = w_r.astype(BF16)
    w_lo = (w_r - w_hi.astype(F32)).astype(BF16)
    return w_hi, w_lo, _pad_lanes(jnp.concatenate([b_group, b_expert]))


def _moe_layer(h, xn, route, counts, w_gate, w_up, w_down, layer, final_g):
    n_tok, d = h.shape
    dest0, dest1, block_eid, block_valid, n_slots = _dispatch(route, counts, n_tok)
    xs = sc_scatter_rows(xn, dest0, dest1, n_slots)
    ys = moe_experts(block_eid, block_valid, xs, w_gate, w_up, w_down, layer)
    y01 = sc_gather_rows(ys, jnp.concatenate([dest0, dest1]))
    g = jnp.ones((d,), F32) if final_g is None else final_g
    return moe_combine(h, y01, route, g, final_g is not None)


def _memory_kv(memn_in, mem_norm, wk, wv):
    bsz, m, d = memn_in.shape
    w = jnp.concatenate([wk, wv], axis=1).astype(BF16)
    kv, _ = rms_matmul(memn_in.reshape(bsz * m, d), mem_norm, w, jnp.zeros((d, LANES), BF16))
    k = kv[:, :d].reshape(bsz, m, d)
    v = kv[:, d:].reshape(bsz, m, d)
    return jnp.swapaxes(k, 1, 2).astype(BF16), v.astype(BF16)


def kernel(x, mem, mem_norm, final_norm, norm_mix, norm_xa, norm_ffn, xa_wq, xa_wk, xa_wv, xa_wo, moe_w_group, moe_b_group, moe_w_expert, moe_b_expert, moe_w_gate, moe_w_up, moe_w_down, ev_w_in, ev_sc_conv, ev_ssm_conv_w, ev_ssm_conv_b, ev_ssm_dt_bias, ev_ssm_a_log, ev_ssm_d, ev_ssm_norm, ev_w_out, od_w_in, od_gdn_conv, od_gdn_dt_bias, od_gdn_a_log, od_gdn_norm, od_w_out):
    bsz, length, d = x.shape
    n_tok = bsz * length
    depth = norm_mix.shape[0]
    h = x.reshape(n_tok, d)
    for layer in range(depth):
        i = layer // 2
        if layer % 2 == 0:
            w = ev_w_in[i]
            z0 = 3 * SC_DIM
            xbc0 = z0 + SSM_INNER
            w_conv = w[:, xbc0:xbc0 + SSM_XBC].astype(BF16)
            w_main = jnp.concatenate([w[:, z0:xbc0], w[:, :z0]], axis=1).astype(BF16)
            w_small = _pad_cols(w[:, xbc0 + SSM_XBC:]).astype(BF16)
            xbc = rms_matmul_conv(h, norm_mix[layer], w_conv, ev_ssm_conv_w[i], ev_ssm_conv_b[i], length)
            y, small = rms_matmul(h, norm_mix[layer], w_main, w_small, tm=512, tn=w_main.shape[1])
            y3 = y.reshape(bsz, length, -1)
            small3 = small.reshape(bsz, length, LANES)
            smallt = jnp.swapaxes(small3[:, :, :16], 1, 2)
            ya = short_conv_mixer(y3, ev_sc_conv[i], SSM_INNER // SC_DIM)
            yb = ssd_mixer(xbc.reshape(bsz, length, -1), y3, small3, smallt, ev_ssm_dt_bias[i],
                           ev_ssm_a_log[i], ev_ssm_d[i], ev_ssm_norm[i])
            w_out = ev_w_out[i].astype(BF16)
            split = SC_DIM
        else:
            w = od_w_in[i]
            qkv_w = 3 * GDN_HEADS * GDN_D
            z_end = qkv_w + GDN_HEADS * GDN_D
            w_conv = w[:, :qkv_w].astype(BF16)
            w_main = jnp.concatenate([w[:, qkv_w:z_end], w[:, z_end + 2 * GDN_HEADS:]], axis=1).astype(BF16)
            w_small = _pad_cols(w[:, z_end:z_end + 2 * GDN_HEADS]).astype(BF16)
            qkv = rms_matmul_conv(h, norm_mix[layer], w_conv, od_gdn_conv[i], jnp.zeros((qkv_w,), F32), length)
            y, small = rms_matmul(h, norm_mix[layer], w_main, w_small, tm=512, tn=w_main.shape[1])
            y3 = y.reshape(bsz, length, -1)
            small3 = small.reshape(bsz, length, LANES)
            smallt = jnp.swapaxes(small3[:, :, :16], 1, 2)
            ya = gated_deltanet_mixer(qkv.reshape(bsz, length, -1), y3, small3, smallt, od_gdn_dt_bias[i],
                                      od_gdn_a_log[i], od_gdn_norm[i])
            yb = stick_breaking_mixer(y3, GDN_HEADS * GDN_D)
            w_out = od_w_out[i].astype(BF16)
            split = GDN_HEADS * GDN_D
        kt, v = _memory_kv(mem, mem_norm, xa_wk[layer], xa_wv[layer])
        w_hi, w_lo, bias = _router_weights(moe_w_group[layer], moe_b_group[layer], moe_w_expert[layer],
                                           moe_b_expert[layer])
        h, xn, route, counts = post_mixer(
            ya.reshape(n_tok, -1), yb.reshape(n_tok, -1), h, w_out[:split], w_out[split:], norm_xa[layer],
            xa_wq[layer].astype(BF16), kt, v, xa_wo[layer].astype(BF16), norm_ffn[layer], w_hi, w_lo, bias)
        h = _moe_layer(h, xn, route, counts, moe_w_gate, moe_w_up, moe_w_down, layer,
                       final_norm if layer == depth - 1 else None)
    return h.reshape(bsz, length, d)
```

```python
import functools

import jax
import jax.numpy as jnp
import numpy as np
from jax import lax
from jax.experimental import pallas as pl
from jax.experimental.pallas import tpu as pltpu
from jax.experimental.pallas import tpu_sc as plsc

F32 = jnp.float32
BF16 = jnp.bfloat16
EPS = 1e-6

D_MODEL = 1024
MEM_LEN = 256
SC_DIM = 512
SSM_HEADS = 16
SSM_HEAD_DIM = 64
SSM_INNER = 1024
SSM_GROUPS = 2
SSM_STATE = 128
SSM_XBC = SSM_INNER + 2 * SSM_GROUPS * SSM_STATE
SSD_CHUNK = 128
GDN_HEADS = 8
GDN_D = 128
GDN_CHUNK = 64
GDN_TILE = 128
SB_HEADS = 8
SB_HEAD_DIM = 64
SB_DIM = 512
SB_BLOCK = 128
SB_STEP_HEADS = 8
XA_HEADS = 4
XA_HEAD_DIM = 256
MOE_GROUPS = 4
MOE_PER_GROUP = 8
MOE_EXPERTS = 32
MOE_FF = 512
MOE_ROWS = 512
POST_GROUPS = 1
SC_CORES = 2
SC_SUBCORES = 16
SC_WORKERS = SC_CORES * SC_SUBCORES
SC_CHUNK = 32
HALO = 8
CONV_CHUNK = 512
LANES = 128
SB_LOG_ZERO = -104.0
VMEM_LIMIT = 56 * 1024 * 1024


def _cparams(*sem):
    return pltpu.CompilerParams(dimension_semantics=sem, vmem_limit_bytes=VMEM_LIMIT)


def _mm(a, b):
    return jnp.dot(a.astype(BF16), b.astype(BF16), preferred_element_type=F32)


def _mm_nt(a, b):
    return lax.dot_general(a.astype(BF16), b.astype(BF16), (((1,), (1,)), ((), ())),
                           preferred_element_type=F32)


def _split_bf16(x, n):
    parts, r = [], x
    for _ in range(n):
        p = r.astype(BF16)
        parts.append(p)
        r = r - p.astype(F32)
    return parts


def _mm_sel_rhs(x, sel, n=3):
    return sum(jnp.dot(p, sel, preferred_element_type=F32) for p in _split_bf16(x, n))


def _mm_sel_lhs(sel, x, n=3):
    return sum(jnp.dot(sel, p, preferred_element_type=F32) for p in _split_bf16(x, n))


def _silu(x):
    return x * jax.nn.sigmoid(x)


def _softplus(x):
    return jnp.maximum(x, 0.0) + jnp.log(1.0 + jnp.exp(-jnp.abs(x)))


def _rms(x, g):
    return x * lax.rsqrt(jnp.mean(x * x, axis=-1, keepdims=True) + EPS) * g


def _rms_matmul_kernel(x_ref, g_ref, w_ref, ws_ref, o_ref, os_ref):
    xn = _rms(x_ref[...], g_ref[...]).astype(BF16)
    o_ref[...] = jnp.dot(xn, w_ref[...], preferred_element_type=F32)
    os_ref[...] = jnp.dot(xn, ws_ref[...], preferred_element_type=F32)


def rms_matmul(x, g, w, ws, tm=512, tn=512):
    m, k = x.shape
    n = w.shape[1]
    tm = min(tm, m)
    main, small = pl.pallas_call(
        _rms_matmul_kernel,
        grid=(n // tn, m // tm),
        in_specs=[
            pl.BlockSpec((tm, k), lambda j, i: (i, 0)),
            pl.BlockSpec((1, k), lambda j, i: (0, 0)),
            pl.BlockSpec((k, tn), lambda j, i: (0, j)),
            pl.BlockSpec((k, LANES), lambda j, i: (0, 0)),
        ],
        out_specs=[
            pl.BlockSpec((tm, tn), lambda j, i: (i, j)),
            pl.BlockSpec((None, tm, LANES), lambda j, i: (j, i, 0)),
        ],
        out_shape=[jax.ShapeDtypeStruct((m, n), F32), jax.ShapeDtypeStruct((n // tn, m, LANES), F32)],
        compiler_params=_cparams("parallel", "parallel"),
        name="rms_matmul",
    )(x, g.reshape(1, k), w, ws)
    return main, small[0]


def _causal_conv(ext_ref, w_ref, rows):
    width = w_ref.shape[0]
    ext = ext_ref[...]
    acc = None
    for j in range(width):
        shift = width - 1 - j
        moved = ext if shift == 0 else pltpu.roll(ext, shift, axis=0)
        term = w_ref[j:j + 1, :] * moved[HALO:HALO + rows, :]
        acc = term if acc is None else acc + term
    return acc


def _rms_matmul_conv_kernel(x_ref, g_ref, w_ref, cw_ref, cb_ref, o_ref, *ext_refs, tiles_per_seq):
    tm = x_ref.shape[0]
    starts_sequence = pl.program_id(1) % tiles_per_seq == 0

    @pl.when(starts_sequence)
    def _():
        for ext_ref in ext_refs:
            ext_ref[0:HALO, :] = jnp.zeros((HALO, CONV_CHUNK), F32)

    @pl.when(jnp.logical_not(starts_sequence))
    def _():
        for ext_ref in ext_refs:
            ext_ref[0:HALO, :] = ext_ref[tm:tm + HALO, :]

    xn = _rms(x_ref[...], g_ref[...]).astype(BF16)
    for c, ext_ref in enumerate(ext_refs):
        cols = slice(c * CONV_CHUNK, (c + 1) * CONV_CHUNK)
        ext_ref[HALO:, :] = jnp.dot(xn, w_ref[:, cols], preferred_element_type=F32)
        o_ref[:, cols] = _silu(_causal_conv(ext_ref, cw_ref.at[:, cols], tm) + cb_ref[:, cols])


def rms_matmul_conv(x, g, w, conv_w, conv_b, seq_len, tm=1024, tn=1536):
    m, k = x.shape
    n = w.shape[1]
    cols = lambda rows: pl.BlockSpec((rows, tn), lambda j, i: (0, j))
    return pl.pallas_call(
        functools.partial(_rms_matmul_conv_kernel, tiles_per_seq=seq_len // tm),
        grid=(n // tn, m // tm),
        in_specs=[
            pl.BlockSpec((tm, k), lambda j, i: (i, 0)),
            pl.BlockSpec((1, k), lambda j, i: (0, 0)),
            cols(k), cols(conv_w.shape[0]), cols(1),
        ],
        out_specs=pl.BlockSpec((tm, tn), lambda j, i: (i, j)),
        out_shape=jax.ShapeDtypeStruct((m, n), F32),
        scratch_shapes=[pltpu.VMEM((tm + HALO, CONV_CHUNK), F32)] * (tn // CONV_CHUNK),
        compiler_params=_cparams("arbitrary", "arbitrary"),
        name="rms_matmul_conv",
    )(x, g.reshape(1, k), w, conv_w, conv_b.reshape(1, n))


def _halo_index(rows):
    step = rows // HALO
    return lambda i: jnp.maximum(i * step - 1, 0)


def _sc_kernel(b_ref, c_ref, x_ref, ch_ref, xh_ref, w_ref, o_ref, ext_ref):
    rows = o_ref.shape[0]
    first = pl.program_id(1) == 0
    ext_ref[0:HALO, :] = jnp.where(first, 0.0, ch_ref[...] * xh_ref[...])
    ext_ref[HALO:, :] = c_ref[...] * x_ref[...]
    o_ref[...] = b_ref[...] * _causal_conv(ext_ref, w_ref, rows)


def short_conv_mixer(y3, w, col0, tl=512):
    bsz, length, _ = y3.shape
    tl = min(tl, length)
    hidx = _halo_index(tl)
    return pl.pallas_call(
        _sc_kernel,
        grid=(bsz, length // tl),
        in_specs=[
            pl.BlockSpec((None, tl, SC_DIM), lambda b, i: (b, i, col0)),
            pl.BlockSpec((None, tl, SC_DIM), lambda b, i: (b, i, col0 + 1)),
            pl.BlockSpec((None, tl, SC_DIM), lambda b, i: (b, i, col0 + 2)),
            pl.BlockSpec((None, HALO, SC_DIM), lambda b, i: (b, hidx(i), col0 + 1)),
            pl.BlockSpec((None, HALO, SC_DIM), lambda b, i: (b, hidx(i), col0 + 2)),
            pl.BlockSpec(w.shape, lambda b, i: (0, 0)),
        ],
        out_specs=pl.BlockSpec((None, tl, SC_DIM), lambda b, i: (b, i, 0)),
        out_shape=jax.ShapeDtypeStruct((bsz, length, SC_DIM), F32),
        scratch_shapes=[pltpu.VMEM((tl + HALO, SC_DIM), F32)],
        compiler_params=_cparams("parallel", "arbitrary"),
        name="short_conv_mixer",
    )(y3, y3, y3, y3, y3, w)


def _ssd_kernel(xbc_ref, z_ref, dt_ref, dtt_ref, dtb_r_ref, dtb_c_ref,
                alog_r_ref, alog_c_ref, d_ref, nw_ref, tri_ref, trit_ref, eh_ref, eq_ref,
                o_ref, s_ref):
    q = SSD_CHUNK
    hpg = SSM_HEADS // SSM_GROUPS
    gw = hpg * SSM_HEAD_DIM

    @pl.when(pl.program_id(1) == 0)
    def _():
        s_ref[...] = jnp.zeros_like(s_ref)

    xbc = xbc_ref[...]
    xs = xbc[:, :SSM_INNER]
    bm = xbc[:, SSM_INNER:SSM_INNER + SSM_GROUPS * SSM_STATE]
    cm = xbc[:, SSM_INNER + SSM_GROUPS * SSM_STATE:]

    dt = _softplus(dt_ref[...] + dtb_r_ref[...])
    acs = _mm_sel_lhs(tri_ref[...], dt * -jnp.exp(alog_r_ref[...]))
    dtt = _softplus(dtt_ref[...] + dtb_c_ref[...])
    acst = _mm_sel_rhs(dtt * -jnp.exp(alog_c_ref[...]), trit_ref[...])
    dt_full = _mm_sel_rhs(dt, eh_ref[...])
    acs_full = _mm_sel_rhs(acs, eh_ref[...])
    acs_col = _mm_sel_rhs(acs, eq_ref[...])

    xdt = xs * dt_full
    acs_last = acs_full[q - 1:q, :]
    xw = xdt * jnp.exp(acs_last - acs_full)
    chunk_decay = jnp.exp(acs_last)

    row = lax.broadcasted_iota(jnp.int32, (q, q), 0)
    col = lax.broadcasted_iota(jnp.int32, (q, q), 1)
    causal = row >= col
    lane = lax.broadcasted_iota(jnp.int32, (q, 2 * SSM_HEAD_DIM), 1)

    y_diag, y_off = [], []
    for g in range(SSM_GROUPS):
        bm_g = bm[:, g * SSM_STATE:(g + 1) * SSM_STATE]
        cm_g = cm[:, g * SSM_STATE:(g + 1) * SSM_STATE]
        cb_g = _mm_nt(cm_g, bm_g)
        state = s_ref[g]
        y_off.append(_mm(cm_g, state))
        s_ref[g] = state * chunk_decay[:, g * gw:(g + 1) * gw] + _mm(bm_g.T, xw[:, g * gw:(g + 1) * gw])
        for pair in range(hpg // 2):
            h0 = g * hpg + 2 * pair
            xdt_pair = xdt[:, h0 * SSM_HEAD_DIM:(h0 + 2) * SSM_HEAD_DIM]
            outs = []
            for h in (h0, h0 + 1):
                seg = acs_col[:, h * q:(h + 1) * q] - acst[h:h + 1, :]
                decay = jnp.where(causal, jnp.exp(seg), 0.0)
                outs.append(_mm(cb_g * decay, xdt_pair))
            y_diag.append(jnp.where(lane < SSM_HEAD_DIM, outs[0], outs[1]))
    y = (jnp.concatenate(y_diag, axis=1) + jnp.concatenate(y_off, axis=1) * jnp.exp(acs_full)
         + xs * d_ref[...])
    y = y * _silu(z_ref[...])
    halves = []
    for g in range(SSM_GROUPS):
        yg = y[:, g * gw:(g + 1) * gw]
        halves.append(yg * lax.rsqrt(jnp.mean(yg * yg, axis=-1, keepdims=True) + EPS))
    o_ref[...] = jnp.concatenate(halves, axis=1) * nw_ref[...]


def _pad_lanes(v, fill=0.0):
    return jnp.pad(v.astype(F32), (0, LANES - v.shape[0]), constant_values=fill).reshape(1, LANES)


def _pad_col(v, rows=16):
    return jnp.pad(v.astype(F32), (0, rows - v.shape[0])).reshape(rows, 1)


def ssd_mixer(xbc3, y3, small3, smallt, dt_bias, a_log, d_skip, norm_w):
    bsz, length, _ = y3.shape
    q = SSD_CHUNK
    tri = jnp.asarray(np.tril(np.ones((q, q), np.float32)), BF16)
    trit = jnp.asarray(np.triu(np.ones((q, q), np.float32)), BF16)
    heads = np.arange(LANES)[:, None]
    eh = jnp.asarray(heads == (np.arange(SSM_INNER)[None, :] // SSM_HEAD_DIM), BF16)
    eq = jnp.asarray(heads == (np.arange(SSM_HEADS * q)[None, :] // q), BF16)
    d_full = jnp.repeat(d_skip.astype(F32), SSM_HEAD_DIM).reshape(1, SSM_INNER)
    const = lambda a: pl.BlockSpec(a.shape, lambda b, c: (0,) * a.ndim)
    args = [_pad_lanes(dt_bias), _pad_col(dt_bias), _pad_lanes(a_log),
            _pad_col(a_log), d_full, norm_w.reshape(1, -1), tri, trit, eh, eq]
    return pl.pallas_call(
        _ssd_kernel,
        grid=(bsz, length // q),
        in_specs=[
            pl.BlockSpec((None, q, SSM_XBC), lambda b, c: (b, c, 0)),
            pl.BlockSpec((None, q, SSM_INNER), lambda b, c: (b, c, 0)),
            pl.BlockSpec((None, q, LANES), lambda b, c: (b, c, 0)),
            pl.BlockSpec((None, 16, q), lambda b, c: (b, 0, c)),
        ] + [const(a) for a in args],
        out_specs=pl.BlockSpec((None, q, SSM_INNER), lambda b, c: (b, c, 0)),
        out_shape=jax.ShapeDtypeStruct((bsz, length, SSM_INNER), F32),
        scratch_shapes=[pltpu.VMEM((SSM_GROUPS, SSM_STATE, SSM_INNER // SSM_GROUPS), F32)],
        compiler_params=_cparams("parallel", "arbitrary"),
        name="ssd_mixer",
    )(xbc3, y3, small3, smallt, *args)


def _unit_lower_inverse(mats, row, col):
    eye = jnp.where(row == col, 1.0, 0.0)
    blk = lambda n: (row >> (n.bit_length() - 1)) == (col >> (n.bit_length() - 1))
    p = [jnp.where(blk(16), -a, 0.0) for a in mats]
    t = [eye + x for x in p]
    for _ in range(3):
        p = [_mm(x, x) for x in p]
        t = [y + _mm(y, x) for y, x in zip(t, p)]
    for n in (16, 32):
        band = blk(2 * n) & jnp.logical_not(blk(n))
        left = [_mm(y, jnp.where(band, a, 0.0)) for y, a in zip(t, mats)]
        t = [y - _mm(x, y) for y, x in zip(t, left)]
    return t


def _gdn_kernel(qkv_ref, z_ref, ab_ref, abt_ref, dtb_r_ref, dtb_c_ref, alog_r_ref,
                alog_c_ref, nw_ref, tri_ref, trit_ref, eg_ref, eb_ref, o_ref, s_ref):
    n = GDN_TILE
    c = GDN_CHUNK
    d = GDN_D
    hd = GDN_HEADS * d

    @pl.when(pl.program_id(1) == 0)
    def _():
        s_ref[...] = jnp.zeros_like(s_ref)

    qkv = qkv_ref[...]
    z = z_ref[...]

    ab = ab_ref[...]
    g = -jnp.exp(alog_r_ref[...]) * _softplus(ab + dtb_r_ref[...])
    gc_full = _mm_sel_rhs(_mm_sel_lhs(tri_ref[...], g), eg_ref[...])
    beta_full = _mm_sel_rhs(jax.nn.sigmoid(ab), eb_ref[...])
    gt = -jnp.exp(alog_c_ref[...]) * _softplus(abt_ref[...] + dtb_c_ref[...])
    gct = _mm_sel_rhs(gt, trit_ref[...])

    row = lax.broadcasted_iota(jnp.int32, (n, n), 0)
    col = lax.broadcasted_iota(jnp.int32, (n, n), 1)
    same = (row >> (c.bit_length() - 1)) == (col >> (c.bit_length() - 1))
    incl = same & (row >= col)
    strict = same & (row > col)
    zeros_half = jnp.zeros((c, d), F32)

    heads = range(GDN_HEADS)
    sl = [slice(h * d, (h + 1) * d) for h in heads]
    l2n = lambda x: x * lax.rsqrt(jnp.sum(x * x, axis=-1, keepdims=True) + EPS)
    qn = [l2n(qkv[:, sl[h]]) * (d ** -0.5) for h in heads]
    kn = [l2n(qkv[:, hd + h * d:hd + (h + 1) * d]) for h in heads]
    vh = [qkv[:, 2 * hd + h * d:2 * hd + (h + 1) * d] for h in heads]
    gcol = [gc_full[:, sl[h]] for h in heads]
    beta = [beta_full[:, sl[h]] for h in heads]
    edec = [jnp.exp(gcol[h] - gct[h:h + 1, :]) for h in heads]
    egc = [jnp.exp(x) for x in gcol]
    kb = [kn[h] * beta[h] for h in heads]
    lower = [jnp.where(strict, _mm_nt(kb[h], kn[h]) * edec[h], 0.0) for h in heads]
    aqk = [jnp.where(incl, _mm_nt(qn[h], kn[h]) * edec[h], 0.0) for h in heads]
    tinv = _unit_lower_inverse(lower, row, col)
    sol = [_mm(tinv[h], jnp.concatenate([vh[h] * beta[h], kb[h] * egc[h]], axis=1)) for h in heads]
    qd = [qn[h] * egc[h] for h in heads]
    glast = [(gcol[h][c - 1:c, :], gcol[h][n - 1:n, :]) for h in heads]
    kdt = [(kn[h] * jnp.exp(jnp.concatenate([jnp.broadcast_to(glast[h][0], (c, d)),
                                             jnp.broadcast_to(glast[h][1], (c, d))], axis=0) - gcol[h])).T
           for h in heads]
    s0 = [s_ref[h] for h in heads]
    v0 = [sol[h][:c, :d] - _mm(sol[h][:c, d:], s0[h]) for h in heads]
    s1 = [s0[h] * jnp.exp(glast[h][0]) + _mm(kdt[h], jnp.concatenate([v0[h], zeros_half], axis=0)) for h in heads]
    v1 = [sol[h][c:, :d] - _mm(sol[h][c:, d:], s1[h]) for h in heads]
    for h in heads:
        s_ref[h] = s1[h] * jnp.exp(glast[h][1]) + _mm(kdt[h], jnp.concatenate([zeros_half, v1[h]], axis=0))
    outs = []
    for h in heads:
        o = (jnp.concatenate([_mm(qd[h][:c], s0[h]), _mm(qd[h][c:], s1[h])], axis=0)
             + _mm(aqk[h], jnp.concatenate([v0[h], v1[h]], axis=0)))
        o = o * lax.rsqrt(jnp.mean(o * o, axis=-1, keepdims=True) + EPS) * nw_ref[...]
        outs.append(o * _silu(z[:, sl[h]]))
    o_ref[...] = jnp.concatenate(outs, axis=1)


def gated_deltanet_mixer(qkv3, y3, small3, smallt, dt_bias, a_log, norm_w):
    bsz, length, _ = y3.shape
    n = GDN_TILE
    hd = GDN_HEADS * GDN_D
    idx = np.arange(n)
    same = (idx[:, None] // GDN_CHUNK) == (idx[None, :] // GDN_CHUNK)
    tri = jnp.asarray(same & (idx[:, None] >= idx[None, :]), BF16)
    trit = jnp.asarray(same & (idx[:, None] <= idx[None, :]), BF16)
    lanes = np.arange(LANES)[:, None]
    heads = np.arange(hd)[None, :] // GDN_D
    eg = jnp.asarray(lanes == heads, BF16)
    eb = jnp.asarray(lanes == heads + GDN_HEADS, BF16)
    const = lambda a: pl.BlockSpec(a.shape, lambda b, c: (0,) * a.ndim)
    args = [_pad_lanes(dt_bias), _pad_col(dt_bias), _pad_lanes(a_log), _pad_col(a_log),
            norm_w.reshape(1, -1), tri, trit, eg, eb]
    return pl.pallas_call(
        _gdn_kernel,
        grid=(bsz, length // n),
        in_specs=[
            pl.BlockSpec((None, n, 3 * hd), lambda b, c: (b, c, 0)),
            pl.BlockSpec((None, n, hd), lambda b, c: (b, c, 0)),
            pl.BlockSpec((None, n, LANES), lambda b, c: (b, c, 0)),
            pl.BlockSpec((None, 16, n), lambda b, c: (b, 0, c)),
        ] + [const(a) for a in args],
        out_specs=pl.BlockSpec((None, n, hd), lambda b, c: (b, c, 0)),
        out_shape=jax.ShapeDtypeStruct((bsz, length, hd), F32),
        scratch_shapes=[pltpu.VMEM((GDN_HEADS, GDN_D, GDN_D), F32)],
        compiler_params=_cparams("parallel", "arbitrary"),
        name="gated_deltanet",
    )(qkv3, y3, small3, smallt, *args)


def _sb_kernel(q_ref, k_ref, v_ref, upper_ref, o_ref):
    blk = SB_BLOCK
    pair_w = 2 * SB_HEAD_DIM
    n_pairs = SB_STEP_HEADS // 2
    i = pl.program_id(2)
    q = q_ref[...] * (SB_HEAD_DIM ** -0.5)
    lane = lax.broadcasted_iota(jnp.int32, (blk, pair_w), 1)
    first_head = lane < SB_HEAD_DIM
    qs = []
    for p in range(n_pairs):
        q2 = q[:, p * pair_w:(p + 1) * pair_w]
        qs += [jnp.where(first_head, q2, 0.0).astype(BF16), jnp.where(first_head, 0.0, q2).astype(BF16)]
    row = lax.broadcasted_iota(jnp.int32, (blk, blk), 0)
    col = lax.broadcasted_iota(jnp.int32, (blk, blk), 1)
    earlier = col < row
    upper = upper_ref[...]
    heads = range(SB_STEP_HEADS)

    def key_block(kb, accs, sticks, diagonal):
        start = pl.multiple_of(kb * blk, blk)
        k = k_ref[pl.ds(start, blk), :].astype(BF16)
        v = v_ref[pl.ds(start, blk), :].astype(BF16)
        kp = [k[:, p * pair_w:(p + 1) * pair_w] for p in range(n_pairs)]
        vp = [v[:, p * pair_w:(p + 1) * pair_w] for p in range(n_pairs)]
        logits = [lax.dot_general(qs[h], kp[h // 2], (((1,), (1,)), ((), ())), preferred_element_type=F32)
                  for h in heads]
        log_keep = [-_softplus(x) for x in logits]
        if diagonal:
            log_keep = [jnp.where(earlier, x, 0.0) for x in log_keep]
        between = [_mm_sel_rhs(log_keep[h], upper, 2) + sticks[h] for h in heads]
        w = [jnp.exp(logits[h] + log_keep[h] + between[h]) for h in heads]
        if diagonal:
            w = [jnp.where(earlier, x, 0.0) for x in w]
        pv = [jnp.dot(w[h].astype(BF16), vp[h // 2], preferred_element_type=F32) for h in heads]
        accs = tuple(accs[p] + jnp.where(first_head, pv[2 * p], pv[2 * p + 1]) for p in range(n_pairs))
        sticks = tuple(sticks[h] + jnp.sum(log_keep[h], axis=-1, keepdims=True) for h in heads)
        return accs, sticks

    accs = tuple(jnp.zeros((blk, pair_w), F32) for _ in range(n_pairs))
    sticks = tuple(jnp.zeros((blk, 1), F32) for _ in heads)
    accs, sticks = key_block(i, accs, sticks, True)

    def alive(state):
        kb, _, sticks = state
        longest = sticks[0]
        for s in sticks[1:]:
            longest = jnp.maximum(longest, s)
        return (kb >= 0) & (jnp.max(longest) > SB_LOG_ZERO)

    def body(state):
        kb, accs, sticks = state
        accs, sticks = key_block(kb, accs, sticks, False)
        return kb - 1, accs, sticks

    _, accs, _ = lax.while_loop(alive, body, (i - 1, accs, sticks))
    o_ref[...] = jnp.concatenate(accs, axis=1)


def stick_breaking_mixer(y3, col0):
    bsz, length, _ = y3.shape
    blk = SB_BLOCK
    step_w = SB_STEP_HEADS * SB_HEAD_DIM
    steps = SB_DIM // step_w
    q0 = col0 // step_w
    idx = np.arange(blk)
    upper = jnp.asarray(idx[:, None] > idx[None, :], BF16)
    resident = lambda off: pl.BlockSpec((None, length, step_w), lambda b, p, i: (b, 0, q0 + off + p),
                                        pipeline_mode=pl.Buffered(1))
    return pl.pallas_call(
        _sb_kernel,
        grid=(bsz, steps, length // blk),
        in_specs=[
            pl.BlockSpec((None, blk, step_w), lambda b, p, i: (b, i, q0 + p)),
            resident(steps),
            resident(2 * steps),
            pl.BlockSpec((blk, blk), lambda b, p, i: (0, 0)),
        ],
        out_specs=pl.BlockSpec((None, blk, step_w), lambda b, p, i: (b, i, p)),
        out_shape=jax.ShapeDtypeStruct((bsz, length, SB_DIM), F32),
        compiler_params=_cparams("parallel", "parallel", "arbitrary"),
        name="stick_breaking",
    )(y3, y3, y3, upper)


def _mixer_out(a_ref, b_ref, h_ref, wa_ref, wb_ref, rows):
    return h_ref[rows, :] + (jnp.dot(a_ref[rows, :].astype(BF16), wa_ref[...], preferred_element_type=F32)
                             + jnp.dot(b_ref[rows, :].astype(BF16), wb_ref[...], preferred_element_type=F32))


def _cross_attention(hs, g_ref, wq_ref, kt_ref, v_ref, wo_ref):
    us = [_rms(h, g_ref[...]).astype(BF16) for h in hs]
    qs = [jnp.dot(u, wq_ref[...], preferred_element_type=F32) for u in us]
    heads = [[] for _ in hs]
    for hd in range(XA_HEADS):
        sl = slice(hd * XA_HEAD_DIM, (hd + 1) * XA_HEAD_DIM)
        ss = [jnp.dot(q[:, sl].astype(BF16), kt_ref[sl, :], preferred_element_type=F32) * (XA_HEAD_DIM ** -0.5)
              for q in qs]
        ps = [jnp.exp(s - jnp.max(s, axis=-1, keepdims=True)) for s in ss]
        ps = [p / jnp.sum(p, axis=-1, keepdims=True) for p in ps]
        for k, p in enumerate(ps):
            heads[k].append(jnp.dot(p.astype(BF16), v_ref[:, sl], preferred_element_type=F32))
    os_ = [jnp.concatenate(hk, axis=1).astype(BF16) for hk in heads]
    return [h + jnp.dot(o, wo_ref[...], preferred_element_type=F32) for h, o in zip(hs, os_)]


def _route(xn, whi_ref, wlo_ref, b_ref, before_ref, run_ref):
    x_hi = xn.astype(BF16)
    x_lo = (xn - x_hi.astype(F32)).astype(BF16)
    logits = (jnp.dot(x_hi, whi_ref[...], preferred_element_type=F32)
              + jnp.dot(x_lo, whi_ref[...], preferred_element_type=F32)
              + jnp.dot(x_hi, wlo_ref[...], preferred_element_type=F32) + b_ref[...])
    lane = lax.broadcasted_iota(jnp.int32, logits.shape, 1).astype(F32)
    neg = -1e30
    none = float(LANES)

    def top(vals):
        best = jnp.max(vals, axis=-1, keepdims=True)
        where = jnp.min(jnp.where(vals == best, lane, none), axis=-1, keepdims=True)
        return best, where

    gl = jnp.where(lane < MOE_GROUPS, logits, neg)
    gbest, gsel = top(gl)
    gprob = 1.0 / jnp.sum(jnp.exp(gl - gbest), axis=-1, keepdims=True)
    lo = MOE_GROUPS + gsel * MOE_PER_GROUP
    el = jnp.where((lane >= lo) & (lane < lo + MOE_PER_GROUP), logits, neg)
    m1, i1 = top(el)
    m2, i2 = top(jnp.where(lane == i1, neg, el))
    e = jnp.exp(m2 - m1)
    gate1 = gprob / (1.0 + e)
    gate2 = gprob * e / (1.0 + e)

    hot1 = lane == i1
    hot2 = lane == i2
    one1 = jnp.where(hot1, 1.0, 0.0)
    one2 = jnp.where(hot2, 1.0, 0.0)
    before = before_ref[...]
    prefix1 = jnp.dot(before, one1.astype(BF16), preferred_element_type=F32)
    prefix2 = jnp.dot(before, one2.astype(BF16), preferred_element_type=F32)
    total1 = jnp.sum(one1, axis=0, keepdims=True)
    running = run_ref[...]
    rank1 = jnp.sum(jnp.where(hot1, prefix1 + running, 0.0), axis=-1, keepdims=True)
    rank2 = jnp.sum(jnp.where(hot2, prefix2 + (running + total1), 0.0), axis=-1, keepdims=True)
    running = running + total1 + jnp.sum(one2, axis=0, keepdims=True)
    run_ref[...] = running

    fields = (i1 - MOE_GROUPS, i2 - MOE_GROUPS, gate1, gate2, rank1, rank2)
    out = jnp.zeros_like(logits)
    for k, val in enumerate(fields):
        out = jnp.where(lane == k, val, out)
    return out


def _post_mixer_kernel(a_ref, b_ref, h_ref, wa_ref, wb_ref, gxa_ref, wq_ref, kt_ref, v_ref, wo_ref,
                       gffn_ref, whi_ref, wlo_ref, bias_ref, before_ref,
                       h_out_ref, xn_ref, r_ref, cnt_ref, run_ref):
    @pl.when(pl.program_id(0) == 0)
    def _():
        run_ref[...] = jnp.zeros_like(run_ref)

    tm = h_ref.shape[0]
    groups = [slice(k * tm // POST_GROUPS, (k + 1) * tm // POST_GROUPS) for k in range(POST_GROUPS)]
    hs = [_mixer_out(a_ref, b_ref, h_ref, wa_ref, wb_ref, rows) for rows in groups]
    hs = _cross_attention(hs, gxa_ref, wq_ref, kt_ref, v_ref, wo_ref)
    for rows, h in zip(groups, hs):
        h_out_ref[rows, :] = h
        xn_ref[rows, :] = _rms(h, gffn_ref[...])
    r_ref[...] = _route(xn_ref[...], whi_ref, wlo_ref, bias_ref, before_ref, run_ref)
    cnt_ref[...] = run_ref[...]


def post_mixer(ya, yb, h, wa, wb, g_xa, wq, kt, v, wo, g_ffn, w_hi, w_lo, bias, tm=512):
    m, d = h.shape
    tiles_per_batch = m // kt.shape[0] // tm
    idx = np.arange(tm)
    before = jnp.asarray(idx[:, None] > idx[None, :], BF16)
    rows = lambda w: pl.BlockSpec((tm, w), lambda i: (i, 0))
    const = lambda a: pl.BlockSpec(a.shape, lambda i: (0,) * a.ndim, pipeline_mode=pl.Buffered(1))
    per_batch = lambda a: pl.BlockSpec((None,) + a.shape[1:], lambda i: (i // tiles_per_batch, 0, 0))
    g_xa, g_ffn = g_xa.reshape(1, d), g_ffn.reshape(1, d)
    return pl.pallas_call(
        _post_mixer_kernel,
        grid=(m // tm,),
        in_specs=[rows(ya.shape[1]), rows(yb.shape[1]), rows(d), const(wa), const(wb), const(g_xa), const(wq),
                  per_batch(kt), per_batch(v), const(wo), const(g_ffn), const(w_hi), const(w_lo), const(bias),
                  const(before)],
        out_specs=[rows(d), rows(d), rows(LANES), pl.BlockSpec((1, LANES), lambda i: (0, 0))],
        out_shape=[jax.ShapeDtypeStruct((m, d), F32), jax.ShapeDtypeStruct((m, d), F32),
                   jax.ShapeDtypeStruct((m, LANES), F32), jax.ShapeDtypeStruct((1, LANES), F32)],
        scratch_shapes=[pltpu.VMEM((1, LANES), F32)],
        compiler_params=_cparams("arbitrary"),
        name="post_mixer",
    )(ya, yb, h, wa, wb, g_xa, wq, kt, v, wo, g_ffn, w_hi, w_lo, bias, before)


def _expert_kernel(beid_ref, valid_ref, x_ref, wg_ref, wu_ref, wd_ref, o_ref, wgb_ref, wub_ref, wdb_ref):
    i = pl.program_id(0)
    changed = (i == 0) | (beid_ref[i] != beid_ref[jnp.maximum(i - 1, 0)])
    valid = valid_ref[i]

    @pl.when(changed)
    def _():
        wgb_ref[...] = wg_ref[...].astype(BF16)
        wub_ref[...] = wu_ref[...].astype(BF16)
        wdb_ref[...] = wd_ref[...].astype(BF16)

    half = MOE_ROWS // 2

    def ffn(n_halves):
        row = lax.broadcasted_iota(jnp.int32, (half, x_ref.shape[1]), 0)
        xs = [jnp.where(row + k * half < valid, x_ref[k * half:(k + 1) * half, :], 0.0).astype(BF16)
              for k in range(n_halves)]
        gates = [jnp.dot(x, wgb_ref[...], preferred_element_type=F32) for x in xs]
        ups = [jnp.dot(x, wub_ref[...], preferred_element_type=F32) for x in xs]
        acts = [(_silu(g) * u).astype(BF16) for g, u in zip(gates, ups)]
        for k, act in enumerate(acts):
            o_ref[k * half:(k + 1) * half, :] = jnp.dot(act, wdb_ref[...], preferred_element_type=F32)

    @pl.when(valid > half)
    def _():
        ffn(2)

    @pl.when((valid > 0) & (valid <= half))
    def _():
        ffn(1)
        o_ref[half:, :] = jnp.zeros((half, o_ref.shape[1]), F32)

    @pl.when(valid == 0)
    def _():
        o_ref[...] = jnp.zeros_like(o_ref)


def moe_experts(block_eid, block_valid, xs, w_gate, w_up, w_down, layer):
    n_slots, d = xs.shape
    rows = MOE_ROWS
    ff = w_gate.shape[3]
    grid_spec = pltpu.PrefetchScalarGridSpec(
        num_scalar_prefetch=2,
        grid=(n_slots // rows,),
        in_specs=[
            pl.BlockSpec((rows, d), lambda i, be, nv: (i, 0)),
            pl.BlockSpec((None, None, d, ff), lambda i, be, nv: (layer, be[i], 0, 0)),
            pl.BlockSpec((None, None, d, ff), lambda i, be, nv: (layer, be[i], 0, 0)),
            pl.BlockSpec((None, None, ff, d), lambda i, be, nv: (layer, be[i], 0, 0)),
        ],
        out_specs=pl.BlockSpec((rows, d), lambda i, be, nv: (i, 0)),
        scratch_shapes=[pltpu.VMEM((d, ff), BF16), pltpu.VMEM((d, ff), BF16), pltpu.VMEM((ff, d), BF16)],
    )
    return pl.pallas_call(
        _expert_kernel,
        grid_spec=grid_spec,
        out_shape=jax.ShapeDtypeStruct((n_slots, d), F32),
        compiler_params=_cparams("arbitrary"),
        name="moe_experts",
    )(block_eid, block_valid, xs, w_gate, w_up, w_down)


def _sc_mesh():
    return plsc.VectorSubcoreMesh(core_axis_name="c", subcore_axis_name="s",
                                  num_cores=SC_CORES, num_subcores=SC_SUBCORES)


def _sc_worker():
    return lax.axis_index("s") * SC_CORES + lax.axis_index("c")


def sc_scatter_rows(x, dest0, dest1, n_slots):
    n_tok, d = x.shape
    per_worker = n_tok // SC_WORKERS
    n_chunks = per_worker // SC_CHUNK
    shape3 = (SC_WORKERS, n_chunks, SC_CHUNK)

    @functools.partial(
        pl.kernel, mesh=_sc_mesh(), out_type=jax.ShapeDtypeStruct((n_slots, d), x.dtype),
        scratch_types=[pltpu.VMEM((n_chunks, SC_CHUNK), jnp.int32), pltpu.VMEM((n_chunks, SC_CHUNK), jnp.int32),
                       pltpu.VMEM((SC_CHUNK, d), x.dtype)],
        name="moe_scatter_rows")
    def scatter(x_hbm, d0_hbm, d1_hbm, out_hbm, i0_v, i1_v, rows_v):
        wid = _sc_worker()
        pltpu.sync_copy(d0_hbm.at[wid], i0_v)
        pltpu.sync_copy(d1_hbm.at[wid], i1_v)

        @pl.loop(0, n_chunks)
        def _(j):
            start = pl.multiple_of(wid * per_worker + j * SC_CHUNK, SC_CHUNK)
            pltpu.sync_copy(x_hbm.at[pl.ds(start, SC_CHUNK)], rows_v)
            pltpu.sync_copy(rows_v, out_hbm.at[i0_v.at[j]])
            pltpu.sync_copy(rows_v, out_hbm.at[i1_v.at[j]])

    return scatter(x, dest0.reshape(shape3), dest1.reshape(shape3))


def sc_gather_rows(table, idx):
    n_out = idx.shape[0]
    d = table.shape[1]
    per_worker = n_out // SC_WORKERS
    n_chunks = per_worker // SC_CHUNK

    @functools.partial(
        pl.kernel, mesh=_sc_mesh(), out_type=jax.ShapeDtypeStruct((n_out, d), table.dtype),
        scratch_types=[pltpu.VMEM((n_chunks, SC_CHUNK), jnp.int32), pltpu.VMEM((SC_CHUNK, d), table.dtype)],
        name="moe_gather_rows")
    def gather(table_hbm, idx_hbm, out_hbm, idx_v, rows_v):
        wid = _sc_worker()
        pltpu.sync_copy(idx_hbm.at[wid], idx_v)

        @pl.loop(0, n_chunks)
        def _(j):
            start = pl.multiple_of(wid * per_worker + j * SC_CHUNK, SC_CHUNK)
            pltpu.sync_copy(table_hbm.at[idx_v.at[j]], rows_v)
            pltpu.sync_copy(rows_v, out_hbm.at[pl.ds(start, SC_CHUNK)])

    return gather(table, idx.reshape(SC_WORKERS, n_chunks, SC_CHUNK))


def _combine_kernel(h_ref, y0_ref, y1_ref, r_ref, g_ref, o_ref, *, final_norm):
    route = r_ref[...]
    h = h_ref[...] + (route[:, 2:3] * y0_ref[...] + route[:, 3:4] * y1_ref[...])
    o_ref[...] = _rms(h, g_ref[...]) if final_norm else h


def moe_combine(h, y01, route, g, final_norm, tm=512):
    m, d = h.shape
    tm = min(tm, m)
    spec = pl.BlockSpec((tm, d), lambda i: (i, 0))
    second = pl.BlockSpec((tm, d), lambda i: (i + m // tm, 0))
    return pl.pallas_call(
        functools.partial(_combine_kernel, final_norm=final_norm),
        grid=(m // tm,),
        in_specs=[spec, spec, second, pl.BlockSpec((tm, LANES), lambda i: (i, 0)),
                  pl.BlockSpec((1, d), lambda i: (0, 0))],
        out_specs=spec,
        out_shape=jax.ShapeDtypeStruct((m, d), F32),
        compiler_params=_cparams("parallel"),
        name="moe_combine",
    )(h, y01, y01, route, g.reshape(1, d))


def _pad_cols(w):
    return jnp.pad(w, ((0, 0), (0, LANES - w.shape[1])))


def _dispatch(route, counts, n_tok):
    rows = MOE_ROWS
    counts = counts[0, MOE_GROUPS:MOE_GROUPS + MOE_EXPERTS].astype(jnp.int32)
    padded = (counts + rows - 1) // rows * rows
    pad_end = jnp.cumsum(padded)
    pad_start = pad_end - padded
    eid = route[:, 0:2].astype(jnp.int32)
    rank = route[:, 4:6].astype(jnp.int32)
    hot = eid[:, :, None] == jnp.arange(MOE_EXPERTS, dtype=jnp.int32)
    dest = jnp.sum(jnp.where(hot, pad_start, 0), axis=-1) + rank
    n_blocks = -(-(2 * n_tok + MOE_EXPERTS * (rows - 1)) // rows)
    block_start = jnp.arange(n_blocks, dtype=jnp.int32) * rows
    block_eid = jnp.minimum(jnp.sum(block_start[:, None] >= pad_end[None, :], axis=-1), MOE_EXPERTS - 1)
    filled = (pad_start + counts)[block_eid]
    block_valid = jnp.clip(filled - block_start, 0, rows)
    return dest[:, 0], dest[:, 1], block_eid.astype(jnp.int32), block_valid.astype(jnp.int32), n_blocks * rows


def _router_weights(w_group, b_group, w_expert, b_expert):
    w_r = _pad_cols(jnp.concatenate([w_group, w_expert], axis=1))
    w_hi = w_r.astype(BF16)
    w_lo = (w_r - w_hi.astype(F32)).astype(BF16)
    return w_hi, w_lo, _pad_lanes(jnp.concatenate([b_group, b_expert]))


def _moe_layer(h, xn, route, counts, w_gate, w_up, w_down, layer, final_g):
    n_tok, d = h.shape
    dest0, dest1, block_eid, block_valid, n_slots = _dispatch(route, counts, n_tok)
    xs = sc_scatter_rows(xn, dest0, dest1, n_slots)
    ys = moe_experts(block_eid, block_valid, xs, w_gate, w_up, w_down, layer)
    y01 = sc_gather_rows(ys, jnp.concatenate([dest0, dest1]))
    g = jnp.ones((d,), F32) if final_g is None else final_g
    return moe_combine(h, y01, route, g, final_g is not None)


def _memory_kv(memn_in, mem_norm, wk, wv):
    bsz, m, d = memn_in.shape
    w = jnp.concatenate([wk, wv], axis=1).astype(BF16)
    kv, _ = rms_matmul(memn_in.reshape(bsz * m, d), mem_norm, w, jnp.zeros((d, LANES), BF16))
    k = kv[:, :d].reshape(bsz, m, d)
    v = kv[:, d:].reshape(bsz, m, d)
    return jnp.swapaxes(k, 1, 2).astype(BF16), v.astype(BF16)


def kernel(x, mem, mem_norm, final_norm, norm_mix, norm_xa, norm_ffn, xa_wq, xa_wk, xa_wv, xa_wo, moe_w_group, moe_b_group, moe_w_expert, moe_b_expert, moe_w_gate, moe_w_up, moe_w_down, ev_w_in, ev_sc_conv, ev_ssm_conv_w, ev_ssm_conv_b, ev_ssm_dt_bias, ev_ssm_a_log, ev_ssm_d, ev_ssm_norm, ev_w_out, od_w_in, od_gdn_conv, od_gdn_dt_bias, od_gdn_a_log, od_gdn_norm, od_w_out):
    bsz, length, d = x.shape
    n_tok = bsz * length
    depth = norm_mix.shape[0]
    h = x.reshape(n_tok, d)
    for layer in range(depth):
        i = layer // 2
        if layer % 2 == 0:
            w = ev_w_in[i]
            z0 = 3 * SC_DIM
            xbc0 = z0 + SSM_INNER
            w_conv = w[:, xbc0:xbc0 + SSM_XBC].astype(BF16)
            w_main = jnp.concatenate([w[:, z0:xbc0], w[:, :z0]], axis=1).astype(BF16)
            w_small = _pad_cols(w[:, xbc0 + SSM_XBC:]).astype(BF16)
            xbc = rms_matmul_conv(h, norm_mix[layer], w_conv, ev_ssm_conv_w[i], ev_ssm_conv_b[i], length)
            y, small = rms_matmul(h, norm_mix[layer], w_main, w_small, tm=1024, tn=w_main.shape[1])
            y3 = y.reshape(bsz, length, -1)
            small3 = small.reshape(bsz, length, LANES)
            smallt = jnp.swapaxes(small3[:, :, :16], 1, 2)
            ya = short_conv_mixer(y3, ev_sc_conv[i], SSM_INNER // SC_DIM)
            yb = ssd_mixer(xbc.reshape(bsz, length, -1), y3, small3, smallt, ev_ssm_dt_bias[i],
                           ev_ssm_a_log[i], ev_ssm_d[i], ev_ssm_norm[i])
            w_out = ev_w_out[i].astype(BF16)
            split = SC_DIM
        else:
            w = od_w_in[i]
            qkv_w = 3 * GDN_HEADS * GDN_D
            z_end = qkv_w + GDN_HEADS * GDN_D
            w_conv = w[:, :qkv_w].astype(BF16)
            w_main = jnp.concatenate([w[:, qkv_w:z_end], w[:, z_end + 2 * GDN_HEADS:]], axis=1).astype(BF16)
            w_small = _pad_cols(w[:, z_end:z_end + 2 * GDN_HEADS]).astype(BF16)
            qkv = rms_matmul_conv(h, norm_mix[layer], w_conv, od_gdn_conv[i], jnp.zeros((qkv_w,), F32), length)
            y, small = rms_matmul(h, norm_mix[layer], w_main, w_small, tm=1024, tn=w_main.shape[1])
            y3 = y.reshape(bsz, length, -1)
            small3 = small.reshape(bsz, length, LANES)
            smallt = jnp.swapaxes(small3[:, :, :16], 1, 2)
            ya = gated_deltanet_mixer(qkv.reshape(bsz, length, -1), y3, small3, smallt, od_gdn_dt_bias[i],
                                      od_gdn_a_log[i], od_gdn_norm[i])
            yb = stick_breaking_mixer(y3, GDN_HEADS * GDN_D)
            w_out = od_w_out[i].astype(BF16)
            split = GDN_HEADS * GDN_D
        kt, v = _memory_kv(mem, mem_norm, xa_wk[layer], xa_wv[layer])
        w_hi, w_lo, bias = _router_weights(moe_w_group[layer], moe_b_group[layer], moe_w_expert[layer],
                                           moe_b_expert[layer])
        h, xn, route, counts = post_mixer(
            ya.reshape(n_tok, -1), yb.reshape(n_tok, -1), h, w_out[:split], w_out[split:], norm_xa[layer],
            xa_wq[layer].astype(BF16), kt, v, xa_wo[layer].astype(BF16), norm_ffn[layer], w_hi, w_lo, bias)
        h = _moe_layer(h, xn, route, counts, moe_w_gate, moe_w_up, moe_w_down, layer,
                       final_norm if layer == depth - 1 else None)
    return h.reshape(bsz, length, d)
```

```python
import functools

import jax
import jax.numpy as jnp
import numpy as np
from jax import lax
from jax.experimental import pallas as pl
from jax.experimental.pallas import tpu as pltpu
from jax.experimental.pallas import tpu_sc as plsc

F32 = jnp.float32
BF16 = jnp.bfloat16
EPS = 1e-6

D_MODEL = 1024
MEM_LEN = 256
SC_DIM = 512
SSM_HEADS = 16
SSM_HEAD_DIM = 64
SSM_INNER = 1024
SSM_GROUPS = 2
SSM_STATE = 128
SSM_XBC = SSM_INNER + 2 * SSM_GROUPS * SSM_STATE
SSD_CHUNK = 128
GDN_HEADS = 8
GDN_D = 128
GDN_CHUNK = 64
GDN_TILE = 128
SB_HEADS = 8
SB_HEAD_DIM = 64
SB_DIM = 512
SB_BLOCK = 128
SB_STEP_HEADS = 8
XA_HEADS = 4
XA_HEAD_DIM = 256
MOE_GROUPS = 4
MOE_PER_GROUP = 8
MOE_EXPERTS = 32
MOE_FF = 512
MOE_ROWS = 512
POST_GROUPS = 1
SC_CORES = 2
SC_SUBCORES = 16
SC_WORKERS = SC_CORES * SC_SUBCORES
SC_CHUNK = 32
HALO = 8
CONV_CHUNK = 512
LANES = 128
SB_LOG_ZERO = -104.0
VMEM_LIMIT = 56 * 1024 * 1024


def _cparams(*sem):
    return pltpu.CompilerParams(dimension_semantics=sem, vmem_limit_bytes=VMEM_LIMIT)


def _mm(a, b):
    return jnp.dot(a.astype(BF16), b.astype(BF16), preferred_element_type=F32)


def _mm_nt(a, b):
    return lax.dot_general(a.astype(BF16), b.astype(BF16), (((1,), (1,)), ((), ())),
                           preferred_element_type=F32)


def _split_bf16(x, n):
    parts, r = [], x
    for _ in range(n):
        p = r.astype(BF16)
        parts.append(p)
        r = r - p.astype(F32)
    return parts


def _mm_sel_rhs(x, sel, n=3):
    return sum(jnp.dot(p, sel, preferred_element_type=F32) for p in _split_bf16(x, n))


def _mm_sel_lhs(sel, x, n=3):
    return sum(jnp.dot(sel, p, preferred_element_type=F32) for p in _split_bf16(x, n))


def _silu(x):
    return x * jax.nn.sigmoid(x)


def _softplus(x):
    return jnp.maximum(x, 0.0) + jnp.log(1.0 + jnp.exp(-jnp.abs(x)))


def _rms(x, g):
    return x * lax.rsqrt(jnp.mean(x * x, axis=-1, keepdims=True) + EPS) * g


def _rms_matmul_kernel(x_ref, g_ref, w_ref, ws_ref, o_ref, os_ref):
    xn = _rms(x_ref[...], g_ref[...]).astype(BF16)
    o_ref[...] = jnp.dot(xn, w_ref[...], preferred_element_type=F32)
    os_ref[...] = jnp.dot(xn, ws_ref[...], preferred_element_type=F32)


def rms_matmul(x, g, w, ws, tm=512, tn=512):
    m, k = x.shape
    n = w.shape[1]
    tm = min(tm, m)
    main, small = pl.pallas_call(
        _rms_matmul_kernel,
        grid=(n // tn, m // tm),
        in_specs=[
            pl.BlockSpec((tm, k), lambda j, i: (i, 0)),
            pl.BlockSpec((1, k), lambda j, i: (0, 0)),
            pl.BlockSpec((k, tn), lambda j, i: (0, j)),
            pl.BlockSpec((k, LANES), lambda j, i: (0, 0)),
        ],
        out_specs=[
            pl.BlockSpec((tm, tn), lambda j, i: (i, j)),
            pl.BlockSpec((None, tm, LANES), lambda j, i: (j, i, 0)),
        ],
        out_shape=[jax.ShapeDtypeStruct((m, n), F32), jax.ShapeDtypeStruct((n // tn, m, LANES), F32)],
        compiler_params=_cparams("parallel", "parallel"),
        name="rms_matmul",
    )(x, g.reshape(1, k), w, ws)
    return main, small[0]


def _causal_conv(ext_ref, w_ref, rows):
    width = w_ref.shape[0]
    ext = ext_ref[...]
    acc = None
    for j in range(width):
        shift = width - 1 - j
        moved = ext if shift == 0 else pltpu.roll(ext, shift, axis=0)
        term = w_ref[j:j + 1, :] * moved[HALO:HALO + rows, :]
        acc = term if acc is None else acc + term
    return acc


def _rms_matmul_conv_kernel(x_ref, g_ref, w_ref, cw_ref, cb_ref, o_ref, *ext_refs, tiles_per_seq):
    tm = x_ref.shape[0]
    starts_sequence = pl.program_id(1) % tiles_per_seq == 0

    @pl.when(starts_sequence)
    def _():
        for ext_ref in ext_refs:
            ext_ref[0:HALO, :] = jnp.zeros((HALO, CONV_CHUNK), F32)

    @pl.when(jnp.logical_not(starts_sequence))
    def _():
        for ext_ref in ext_refs:
            ext_ref[0:HALO, :] = ext_ref[tm:tm + HALO, :]

    xn = _rms(x_ref[...], g_ref[...]).astype(BF16)
    for c, ext_ref in enumerate(ext_refs):
        cols = slice(c * CONV_CHUNK, (c + 1) * CONV_CHUNK)
        ext_ref[HALO:, :] = jnp.dot(xn, w_ref[:, cols], preferred_element_type=F32)
        o_ref[:, cols] = _silu(_causal_conv(ext_ref, cw_ref.at[:, cols], tm) + cb_ref[:, cols])


def rms_matmul_conv(x, g, w, conv_w, conv_b, seq_len, tm=1024, tn=1536):
    m, k = x.shape
    n = w.shape[1]
    cols = lambda rows: pl.BlockSpec((rows, tn), lambda j, i: (0, j))
    return pl.pallas_call(
        functools.partial(_rms_matmul_conv_kernel, tiles_per_seq=seq_len // tm),
        grid=(n // tn, m // tm),
        in_specs=[
            pl.BlockSpec((tm, k), lambda j, i: (i, 0)),
            pl.BlockSpec((1, k), lambda j, i: (0, 0)),
            cols(k), cols(conv_w.shape[0]), cols(1),
        ],
        out_specs=pl.BlockSpec((tm, tn), lambda j, i: (i, j)),
        out_shape=jax.ShapeDtypeStruct((m, n), F32),
        scratch_shapes=[pltpu.VMEM((tm + HALO, CONV_CHUNK), F32)] * (tn // CONV_CHUNK),
        compiler_params=_cparams("arbitrary", "arbitrary"),
        name="rms_matmul_conv",
    )(x, g.reshape(1, k), w, conv_w, conv_b.reshape(1, n))


def _halo_index(rows):
    step = rows // HALO
    return lambda i: jnp.maximum(i * step - 1, 0)


def _sc_kernel(b_ref, c_ref, x_ref, ch_ref, xh_ref, w_ref, o_ref, ext_ref):
    rows = o_ref.shape[0]
    first = pl.program_id(1) == 0
    ext_ref[0:HALO, :] = jnp.where(first, 0.0, ch_ref[...] * xh_ref[...])
    ext_ref[HALO:, :] = c_ref[...] * x_ref[...]
    o_ref[...] = b_ref[...] * _causal_conv(ext_ref, w_ref, rows)


def short_conv_mixer(y3, w, col0, tl=512):
    bsz, length, _ = y3.shape
    tl = min(tl, length)
    hidx = _halo_index(tl)
    return pl.pallas_call(
        _sc_kernel,
        grid=(bsz, length // tl),
        in_specs=[
            pl.BlockSpec((None, tl, SC_DIM), lambda b, i: (b, i, col0)),
            pl.BlockSpec((None, tl, SC_DIM), lambda b, i: (b, i, col0 + 1)),
            pl.BlockSpec((None, tl, SC_DIM), lambda b, i: (b, i, col0 + 2)),
            pl.BlockSpec((None, HALO, SC_DIM), lambda b, i: (b, hidx(i), col0 + 1)),
            pl.BlockSpec((None, HALO, SC_DIM), lambda b, i: (b, hidx(i), col0 + 2)),
            pl.BlockSpec(w.shape, lambda b, i: (0, 0)),
        ],
        out_specs=pl.BlockSpec((None, tl, SC_DIM), lambda b, i: (b, i, 0)),
        out_shape=jax.ShapeDtypeStruct((bsz, length, SC_DIM), F32),
        scratch_shapes=[pltpu.VMEM((tl + HALO, SC_DIM), F32)],
        compiler_params=_cparams("parallel", "arbitrary"),
        name="short_conv_mixer",
    )(y3, y3, y3, y3, y3, w)


def _ssd_kernel(xbc_ref, z_ref, dt_ref, dtt_ref, dtb_r_ref, dtb_c_ref,
                alog_r_ref, alog_c_ref, d_ref, nw_ref, tri_ref, trit_ref, eh_ref, eq_ref,
                o_ref, s_ref):
    q = SSD_CHUNK
    hpg = SSM_HEADS // SSM_GROUPS
    gw = hpg * SSM_HEAD_DIM

    @pl.when(pl.program_id(1) == 0)
    def _():
        s_ref[...] = jnp.zeros_like(s_ref)

    xbc = xbc_ref[...]
    xs = xbc[:, :SSM_INNER]
    bm = xbc[:, SSM_INNER:SSM_INNER + SSM_GROUPS * SSM_STATE]
    cm = xbc[:, SSM_INNER + SSM_GROUPS * SSM_STATE:]

    dt = _softplus(dt_ref[...] + dtb_r_ref[...])
    acs = _mm_sel_lhs(tri_ref[...], dt * -jnp.exp(alog_r_ref[...]))
    dtt = _softplus(dtt_ref[...] + dtb_c_ref[...])
    acst = _mm_sel_rhs(dtt * -jnp.exp(alog_c_ref[...]), trit_ref[...])
    dt_full = _mm_sel_rhs(dt, eh_ref[...])
    acs_full = _mm_sel_rhs(acs, eh_ref[...])
    acs_col = _mm_sel_rhs(acs, eq_ref[...])

    xdt = xs * dt_full
    acs_last = acs_full[q - 1:q, :]
    xw = xdt * jnp.exp(acs_last - acs_full)
    chunk_decay = jnp.exp(acs_last)

    row = lax.broadcasted_iota(jnp.int32, (q, q), 0)
    col = lax.broadcasted_iota(jnp.int32, (q, q), 1)
    causal = row >= col
    lane = lax.broadcasted_iota(jnp.int32, (q, 2 * SSM_HEAD_DIM), 1)

    y_diag, y_off = [], []
    for g in range(SSM_GROUPS):
        bm_g = bm[:, g * SSM_STATE:(g + 1) * SSM_STATE]
        cm_g = cm[:, g * SSM_STATE:(g + 1) * SSM_STATE]
        cb_g = _mm_nt(cm_g, bm_g)
        state = s_ref[g]
        y_off.append(_mm(cm_g, state))
        s_ref[g] = state * chunk_decay[:, g * gw:(g + 1) * gw] + _mm(bm_g.T, xw[:, g * gw:(g + 1) * gw])
        for pair in range(hpg // 2):
            h0 = g * hpg + 2 * pair
            xdt_pair = xdt[:, h0 * SSM_HEAD_DIM:(h0 + 2) * SSM_HEAD_DIM]
            outs = []
            for h in (h0, h0 + 1):
                seg = acs_col[:, h * q:(h + 1) * q] - acst[h:h + 1, :]
                decay = jnp.where(causal, jnp.exp(seg), 0.0)
                outs.append(_mm(cb_g * decay, xdt_pair))
            y_diag.append(jnp.where(lane < SSM_HEAD_DIM, outs[0], outs[1]))
    y = (jnp.concatenate(y_diag, axis=1) + jnp.concatenate(y_off, axis=1) * jnp.exp(acs_full)
         + xs * d_ref[...])
    y = y * _silu(z_ref[...])
    halves = []
    for g in range(SSM_GROUPS):
        yg = y[:, g * gw:(g + 1) * gw]
        halves.append(yg * lax.rsqrt(jnp.mean(yg * yg, axis=-1, keepdims=True) + EPS))
    o_ref[...] = jnp.concatenate(halves, axis=1) * nw_ref[...]


def _pad_lanes(v, fill=0.0):
    return jnp.pad(v.astype(F32), (0, LANES - v.shape[0]), constant_values=fill).reshape(1, LANES)


def _pad_col(v, rows=16):
    return jnp.pad(v.astype(F32), (0, rows - v.shape[0])).reshape(rows, 1)


def ssd_mixer(xbc3, y3, small3, smallt, dt_bias, a_log, d_skip, norm_w):
    bsz, length, _ = y3.shape
    q = SSD_CHUNK
    tri = jnp.asarray(np.tril(np.ones((q, q), np.float32)), BF16)
    trit = jnp.asarray(np.triu(np.ones((q, q), np.float32)), BF16)
    heads = np.arange(LANES)[:, None]
    eh = jnp.asarray(heads == (np.arange(SSM_INNER)[None, :] // SSM_HEAD_DIM), BF16)
    eq = jnp.asarray(heads == (np.arange(SSM_HEADS * q)[None, :] // q), BF16)
    d_full = jnp.repeat(d_skip.astype(F32), SSM_HEAD_DIM).reshape(1, SSM_INNER)
    const = lambda a: pl.BlockSpec(a.shape, lambda b, c: (0,) * a.ndim)
    args = [_pad_lanes(dt_bias), _pad_col(dt_bias), _pad_lanes(a_log),
            _pad_col(a_log), d_full, norm_w.reshape(1, -1), tri, trit, eh, eq]
    return pl.pallas_call(
        _ssd_kernel,
        grid=(bsz, length // q),
        in_specs=[
            pl.BlockSpec((None, q, SSM_XBC), lambda b, c: (b, c, 0)),
            pl.BlockSpec((None, q, SSM_INNER), lambda b, c: (b, c, 0)),
            pl.BlockSpec((None, q, LANES), lambda b, c: (b, c, 0)),
            pl.BlockSpec((None, 16, q), lambda b, c: (b, 0, c)),
        ] + [const(a) for a in args],
        out_specs=pl.BlockSpec((None, q, SSM_INNER), lambda b, c: (b, c, 0)),
        out_shape=jax.ShapeDtypeStruct((bsz, length, SSM_INNER), F32),
        scratch_shapes=[pltpu.VMEM((SSM_GROUPS, SSM_STATE, SSM_INNER // SSM_GROUPS), F32)],
        compiler_params=_cparams("parallel", "arbitrary"),
        name="ssd_mixer",
    )(xbc3, y3, small3, smallt, *args)


def _unit_lower_inverse(mats, row, col):
    eye = jnp.where(row == col, 1.0, 0.0)
    blk = lambda n: (row >> (n.bit_length() - 1)) == (col >> (n.bit_length() - 1))
    p = [jnp.where(blk(16), -a, 0.0) for a in mats]
    t = [eye + x for x in p]
    for _ in range(3):
        p = [_mm(x, x) for x in p]
        t = [y + _mm(y, x) for y, x in zip(t, p)]
    for n in (16, 32):
        band = blk(2 * n) & jnp.logical_not(blk(n))
        left = [_mm(y, jnp.where(band, a, 0.0)) for y, a in zip(t, mats)]
        t = [y - _mm(x, y) for y, x in zip(t, left)]
    return t


def _gdn_kernel(qkv_ref, z_ref, ab_ref, abt_ref, dtb_r_ref, dtb_c_ref, alog_r_ref,
                alog_c_ref, nw_ref, tri_ref, trit_ref, eg_ref, eb_ref, o_ref, s_ref):
    n = GDN_TILE
    c = GDN_CHUNK
    d = GDN_D
    hd = GDN_HEADS * d

    @pl.when(pl.program_id(1) == 0)
    def _():
        s_ref[...] = jnp.zeros_like(s_ref)

    qkv = qkv_ref[...]
    z = z_ref[...]

    ab = ab_ref[...]
    g = -jnp.exp(alog_r_ref[...]) * _softplus(ab + dtb_r_ref[...])
    gc_full = _mm_sel_rhs(_mm_sel_lhs(tri_ref[...], g), eg_ref[...])
    beta_full = _mm_sel_rhs(jax.nn.sigmoid(ab), eb_ref[...])
    gt = -jnp.exp(alog_c_ref[...]) * _softplus(abt_ref[...] + dtb_c_ref[...])
    gct = _mm_sel_rhs(gt, trit_ref[...])

    row = lax.broadcasted_iota(jnp.int32, (n, n), 0)
    col = lax.broadcasted_iota(jnp.int32, (n, n), 1)
    same = (row >> (c.bit_length() - 1)) == (col >> (c.bit_length() - 1))
    incl = same & (row >= col)
    strict = same & (row > col)
    zeros_half = jnp.zeros((c, d), F32)

    heads = range(GDN_HEADS)
    sl = [slice(h * d, (h + 1) * d) for h in heads]
    l2n = lambda x: x * lax.rsqrt(jnp.sum(x * x, axis=-1, keepdims=True) + EPS)
    qn = [l2n(qkv[:, sl[h]]) * (d ** -0.5) for h in heads]
    kn = [l2n(qkv[:, hd + h * d:hd + (h + 1) * d]) for h in heads]
    vh = [qkv[:, 2 * hd + h * d:2 * hd + (h + 1) * d] for h in heads]
    gcol = [gc_full[:, sl[h]] for h in heads]
    beta = [beta_full[:, sl[h]] for h in heads]
    edec = [jnp.exp(gcol[h] - gct[h:h + 1, :]) for h in heads]
    egc = [jnp.exp(x) for x in gcol]
    kb = [kn[h] * beta[h] for h in heads]
    lower = [jnp.where(strict, _mm_nt(kb[h], kn[h]) * edec[h], 0.0) for h in heads]
    aqk = [jnp.where(incl, _mm_nt(qn[h], kn[h]) * edec[h], 0.0) for h in heads]
    tinv = _unit_lower_inverse(lower, row, col)
    sol = [_mm(tinv[h], jnp.concatenate([vh[h] * beta[h], kb[h] * egc[h]], axis=1)) for h in heads]
    qd = [qn[h] * egc[h] for h in heads]
    glast = [(gcol[h][c - 1:c, :], gcol[h][n - 1:n, :]) for h in heads]
    kdt = [(kn[h] * jnp.exp(jnp.concatenate([jnp.broadcast_to(glast[h][0], (c, d)),
                                             jnp.broadcast_to(glast[h][1], (c, d))], axis=0) - gcol[h])).T
           for h in heads]
    s0 = [s_ref[h] for h in heads]
    v0 = [sol[h][:c, :d] - _mm(sol[h][:c, d:], s0[h]) for h in heads]
    s1 = [s0[h] * jnp.exp(glast[h][0]) + _mm(kdt[h], jnp.concatenate([v0[h], zeros_half], axis=0)) for h in heads]
    v1 = [sol[h][c:, :d] - _mm(sol[h][c:, d:], s1[h]) for h in heads]
    for h in heads:
        s_ref[h] = s1[h] * jnp.exp(glast[h][1]) + _mm(kdt[h], jnp.concatenate([zeros_half, v1[h]], axis=0))
    outs = []
    for h in heads:
        o = (jnp.concatenate([_mm(qd[h][:c], s0[h]), _mm(qd[h][c:], s1[h])], axis=0)
             + _mm(aqk[h], jnp.concatenate([v0[h], v1[h]], axis=0)))
        o = o * lax.rsqrt(jnp.mean(o * o, axis=-1, keepdims=True) + EPS) * nw_ref[...]
        outs.append(o * _silu(z[:, sl[h]]))
    o_ref[...] = jnp.concatenate(outs, axis=1)


def gated_deltanet_mixer(qkv3, y3, small3, smallt, dt_bias, a_log, norm_w):
    bsz, length, _ = y3.shape
    n = GDN_TILE
    hd = GDN_HEADS * GDN_D
    idx = np.arange(n)
    same = (idx[:, None] // GDN_CHUNK) == (idx[None, :] // GDN_CHUNK)
    tri = jnp.asarray(same & (idx[:, None] >= idx[None, :]), BF16)
    trit = jnp.asarray(same & (idx[:, None] <= idx[None, :]), BF16)
    lanes = np.arange(LANES)[:, None]
    heads = np.arange(hd)[None, :] // GDN_D
    eg = jnp.asarray(lanes == heads, BF16)
    eb = jnp.asarray(lanes == heads + GDN_HEADS, BF16)
    const = lambda a: pl.BlockSpec(a.shape, lambda b, c: (0,) * a.ndim)
    args = [_pad_lanes(dt_bias), _pad_col(dt_bias), _pad_lanes(a_log), _pad_col(a_log),
            norm_w.reshape(1, -1), tri, trit, eg, eb]
    return pl.pallas_call(
        _gdn_kernel,
        grid=(bsz, length // n),
        in_specs=[
            pl.BlockSpec((None, n, 3 * hd), lambda b, c: (b, c, 0)),
            pl.BlockSpec((None, n, hd), lambda b, c: (b, c, 0)),
            pl.BlockSpec((None, n, LANES), lambda b, c: (b, c, 0)),
            pl.BlockSpec((None, 16, n), lambda b, c: (b, 0, c)),
        ] + [const(a) for a in args],
        out_specs=pl.BlockSpec((None, n, hd), lambda b, c: (b, c, 0)),
        out_shape=jax.ShapeDtypeStruct((bsz, length, hd), F32),
        scratch_shapes=[pltpu.VMEM((GDN_HEADS, GDN_D, GDN_D), F32)],
        compiler_params=_cparams("parallel", "arbitrary"),
        name="gated_deltanet",
    )(qkv3, y3, small3, smallt, *args)


def _sb_kernel(q_ref, k_ref, v_ref, upper_ref, o_ref):
    blk = SB_BLOCK
    pair_w = 2 * SB_HEAD_DIM
    n_pairs = SB_STEP_HEADS // 2
    i = pl.program_id(2)
    q = q_ref[...] * (SB_HEAD_DIM ** -0.5)
    lane = lax.broadcasted_iota(jnp.int32, (blk, pair_w), 1)
    first_head = lane < SB_HEAD_DIM
    qs = []
    for p in range(n_pairs):
        q2 = q[:, p * pair_w:(p + 1) * pair_w]
        qs += [jnp.where(first_head, q2, 0.0).astype(BF16), jnp.where(first_head, 0.0, q2).astype(BF16)]
    row = lax.broadcasted_iota(jnp.int32, (blk, blk), 0)
    col = lax.broadcasted_iota(jnp.int32, (blk, blk), 1)
    earlier = col < row
    upper = upper_ref[...]
    heads = range(SB_STEP_HEADS)

    def key_block(kb, accs, sticks, diagonal):
        start = pl.multiple_of(kb * blk, blk)
        k = k_ref[pl.ds(start, blk), :].astype(BF16)
        v = v_ref[pl.ds(start, blk), :].astype(BF16)
        kp = [k[:, p * pair_w:(p + 1) * pair_w] for p in range(n_pairs)]
        vp = [v[:, p * pair_w:(p + 1) * pair_w] for p in range(n_pairs)]
        logits = [lax.dot_general(qs[h], kp[h // 2], (((1,), (1,)), ((), ())), preferred_element_type=F32)
                  for h in heads]
        log_keep = [-_softplus(x) for x in logits]
        if diagonal:
            log_keep = [jnp.where(earlier, x, 0.0) for x in log_keep]
        between = [_mm_sel_rhs(log_keep[h], upper, 2) + sticks[h] for h in heads]
        w = [jnp.exp(logits[h] + log_keep[h] + between[h]) for h in heads]
        if diagonal:
            w = [jnp.where(earlier, x, 0.0) for x in w]
        pv = [jnp.dot(w[h].astype(BF16), vp[h // 2], preferred_element_type=F32) for h in heads]
        accs = tuple(accs[p] + jnp.where(first_head, pv[2 * p], pv[2 * p + 1]) for p in range(n_pairs))
        sticks = tuple(sticks[h] + jnp.sum(log_keep[h], axis=-1, keepdims=True) for h in heads)
        return accs, sticks

    accs = tuple(jnp.zeros((blk, pair_w), F32) for _ in range(n_pairs))
    sticks = tuple(jnp.zeros((blk, 1), F32) for _ in heads)
    accs, sticks = key_block(i, accs, sticks, True)

    def alive(state):
        kb, _, sticks = state
        longest = sticks[0]
        for s in sticks[1:]:
            longest = jnp.maximum(longest, s)
        return (kb >= 0) & (jnp.max(longest) > SB_LOG_ZERO)

    def body(state):
        kb, accs, sticks = state
        accs, sticks = key_block(kb, accs, sticks, False)
        return kb - 1, accs, sticks

    _, accs, _ = lax.while_loop(alive, body, (i - 1, accs, sticks))
    o_ref[...] = jnp.concatenate(accs, axis=1)


def stick_breaking_mixer(y3, col0):
    bsz, length, _ = y3.shape
    blk = SB_BLOCK
    step_w = SB_STEP_HEADS * SB_HEAD_DIM
    steps = SB_DIM // step_w
    q0 = col0 // step_w
    idx = np.arange(blk)
    upper = jnp.asarray(idx[:, None] > idx[None, :], BF16)
    resident = lambda off: pl.BlockSpec((None, length, step_w), lambda b, p, i: (b, 0, q0 + off + p),
                                        pipeline_mode=pl.Buffered(1))
    return pl.pallas_call(
        _sb_kernel,
        grid=(bsz, steps, length // blk),
        in_specs=[
            pl.BlockSpec((None, blk, step_w), lambda b, p, i: (b, i, q0 + p)),
            resident(steps),
            resident(2 * steps),
            pl.BlockSpec((blk, blk), lambda b, p, i: (0, 0)),
        ],
        out_specs=pl.BlockSpec((None, blk, step_w), lambda b, p, i: (b, i, p)),
        out_shape=jax.ShapeDtypeStruct((bsz, length, SB_DIM), F32),
        compiler_params=_cparams("parallel", "parallel", "arbitrary"),
        name="stick_breaking",
    )(y3, y3, y3, upper)


def _mixer_out(a_ref, b_ref, h_ref, wa_ref, wb_ref, rows):
    return h_ref[rows, :] + (jnp.dot(a_ref[rows, :].astype(BF16), wa_ref[...], preferred_element_type=F32)
                             + jnp.dot(b_ref[rows, :].astype(BF16), wb_ref[...], preferred_element_type=F32))


def _cross_attention(hs, g_ref, wq_ref, kt_ref, v_ref, wo_ref):
    us = [_rms(h, g_ref[...]).astype(BF16) for h in hs]
    qs = [jnp.dot(u, wq_ref[...], preferred_element_type=F32) for u in us]
    heads = [[] for _ in hs]
    for hd in range(XA_HEADS):
        sl = slice(hd * XA_HEAD_DIM, (hd + 1) * XA_HEAD_DIM)
        ss = [jnp.dot(q[:, sl].astype(BF16), kt_ref[sl, :], preferred_element_type=F32) * (XA_HEAD_DIM ** -0.5)
              for q in qs]
        ps = [jnp.exp(s - jnp.max(s, axis=-1, keepdims=True)) for s in ss]
        ps = [p / jnp.sum(p, axis=-1, keepdims=True) for p in ps]
        for k, p in enumerate(ps):
            heads[k].append(jnp.dot(p.astype(BF16), v_ref[:, sl], preferred_element_type=F32))
    os_ = [jnp.concatenate(hk, axis=1).astype(BF16) for hk in heads]
    return [h + jnp.dot(o, wo_ref[...], preferred_element_type=F32) for h, o in zip(hs, os_)]


def _route(xn, whi_ref, wlo_ref, b_ref, before_ref, run_ref):
    x_hi = xn.astype(BF16)
    x_lo = (xn - x_hi.astype(F32)).astype(BF16)
    logits = (jnp.dot(x_hi, whi_ref[...], preferred_element_type=F32)
              + jnp.dot(x_lo, whi_ref[...], preferred_element_type=F32)
              + jnp.dot(x_hi, wlo_ref[...], preferred_element_type=F32) + b_ref[...])
    lane = lax.broadcasted_iota(jnp.int32, logits.shape, 1).astype(F32)
    neg = -1e30
    none = float(LANES)

    def top(vals):
        best = jnp.max(vals, axis=-1, keepdims=True)
        where = jnp.min(jnp.where(vals == best, lane, none), axis=-1, keepdims=True)
        return best, where

    gl = jnp.where(lane < MOE_GROUPS, logits, neg)
    gbest, gsel = top(gl)
    gprob = 1.0 / jnp.sum(jnp.exp(gl - gbest), axis=-1, keepdims=True)
    lo = MOE_GROUPS + gsel * MOE_PER_GROUP
    el = jnp.where((lane >= lo) & (lane < lo + MOE_PER_GROUP), logits, neg)
    m1, i1 = top(el)
    m2, i2 = top(jnp.where(lane == i1, neg, el))
    e = jnp.exp(m2 - m1)
    gate1 = gprob / (1.0 + e)
    gate2 = gprob * e / (1.0 + e)

    hot1 = lane == i1
    hot2 = lane == i2
    one1 = jnp.where(hot1, 1.0, 0.0)
    one2 = jnp.where(hot2, 1.0, 0.0)
    before = before_ref[...]
    prefix1 = jnp.dot(before, one1.astype(BF16), preferred_element_type=F32)
    prefix2 = jnp.dot(before, one2.astype(BF16), preferred_element_type=F32)
    total1 = jnp.sum(one1, axis=0, keepdims=True)
    running = run_ref[...]
    rank1 = jnp.sum(jnp.where(hot1, prefix1 + running, 0.0), axis=-1, keepdims=True)
    rank2 = jnp.sum(jnp.where(hot2, prefix2 + (running + total1), 0.0), axis=-1, keepdims=True)
    running = running + total1 + jnp.sum(one2, axis=0, keepdims=True)
    run_ref[...] = running

    fields = (i1 - MOE_GROUPS, i2 - MOE_GROUPS, gate1, gate2, rank1, rank2)
    out = jnp.zeros_like(logits)
    for k, val in enumerate(fields):
        out = jnp.where(lane == k, val, out)
    return out


def _post_mixer_kernel(a_ref, b_ref, h_ref, wa_ref, wb_ref, gxa_ref, wq_ref, kt_ref, v_ref, wo_ref,
                       gffn_ref, whi_ref, wlo_ref, bias_ref, before_ref,
                       h_out_ref, xn_ref, r_ref, cnt_ref, run_ref):
    @pl.when(pl.program_id(0) == 0)
    def _():
        run_ref[...] = jnp.zeros_like(run_ref)

    tm = h_ref.shape[0]
    groups = [slice(k * tm // POST_GROUPS, (k + 1) * tm // POST_GROUPS) for k in range(POST_GROUPS)]
    hs = [_mixer_out(a_ref, b_ref, h_ref, wa_ref, wb_ref, rows) for rows in groups]
    hs = _cross_attention(hs, gxa_ref, wq_ref, kt_ref, v_ref, wo_ref)
    for rows, h in zip(groups, hs):
        h_out_ref[rows, :] = h
        xn_ref[rows, :] = _rms(h, gffn_ref[...])
    r_ref[...] = _route(xn_ref[...], whi_ref, wlo_ref, bias_ref, before_ref, run_ref)
    cnt_ref[...] = run_ref[...]


def post_mixer(ya, yb, h, wa, wb, g_xa, wq, kt, v, wo, g_ffn, w_hi, w_lo, bias, tm=512):
    m, d = h.shape
    tiles_per_batch = m // kt.shape[0] // tm
    idx = np.arange(tm)
    before = jnp.asarray(idx[:, None] > idx[None, :], BF16)
    rows = lambda w: pl.BlockSpec((tm, w), lambda i: (i, 0))
    const = lambda a: pl.BlockSpec(a.shape, lambda i: (0,) * a.ndim, pipeline_mode=pl.Buffered(1))
    per_batch = lambda a: pl.BlockSpec((None,) + a.shape[1:], lambda i: (i // tiles_per_batch, 0, 0))
    g_xa, g_ffn = g_xa.reshape(1, d), g_ffn.reshape(1, d)
    return pl.pallas_call(
        _post_mixer_kernel,
        grid=(m // tm,),
        in_specs=[rows(ya.shape[1]), rows(yb.shape[1]), rows(d), const(wa), const(wb), const(g_xa), const(wq),
                  per_batch(kt), per_batch(v), const(wo), const(g_ffn), const(w_hi), const(w_lo), const(bias),
                  const(before)],
        out_specs=[rows(d), rows(d), rows(LANES), pl.BlockSpec((1, LANES), lambda i: (0, 0))],
        out_shape=[jax.ShapeDtypeStruct((m, d), F32), jax.ShapeDtypeStruct((m, d), F32),
                   jax.ShapeDtypeStruct((m, LANES), F32), jax.ShapeDtypeStruct((1, LANES), F32)],
        scratch_shapes=[pltpu.VMEM((1, LANES), F32)],
        compiler_params=_cparams("arbitrary"),
        name="post_mixer",
    )(ya, yb, h, wa, wb, g_xa, wq, kt, v, wo, g_ffn, w_hi, w_lo, bias, before)


def _expert_kernel(beid_ref, valid_ref, first_ref, slot_ref, next_ref, x_ref, wg_hbm, wu_hbm, wd_hbm, o_ref,
                   wg32_ref, wu32_ref, wd32_ref, wgb_ref, wub_ref, wdb_ref, sem_ref, *, layer):
    i = pl.program_id(0)
    valid = valid_ref[i]

    def weight_copies(expert, slot):
        return (pltpu.make_async_copy(wg_hbm.at[layer, expert], wg32_ref.at[slot], sem_ref.at[slot, 0]),
                pltpu.make_async_copy(wu_hbm.at[layer, expert], wu32_ref.at[slot], sem_ref.at[slot, 1]),
                pltpu.make_async_copy(wd_hbm.at[layer, expert], wd32_ref.at[slot], sem_ref.at[slot, 2]))

    @pl.when(i == 0)
    def _():
        for copy in weight_copies(beid_ref[0], 0):
            copy.start()

    @pl.when(first_ref[i] == 1)
    def _():
        slot = slot_ref[i]
        for copy in weight_copies(beid_ref[i], slot):
            copy.wait()
        wgb_ref[...] = wg32_ref[slot].astype(BF16)
        wub_ref[...] = wu32_ref[slot].astype(BF16)
        wdb_ref[...] = wd32_ref[slot].astype(BF16)

        @pl.when(next_ref[i] >= 0)
        def _():
            for copy in weight_copies(next_ref[i], 1 - slot):
                copy.start()

    half = MOE_ROWS // 2

    def ffn(n_halves):
        row = lax.broadcasted_iota(jnp.int32, (half, x_ref.shape[1]), 0)
        xs = [jnp.where(row + k * half < valid, x_ref[k * half:(k + 1) * half, :], 0.0).astype(BF16)
              for k in range(n_halves)]
        gates = [jnp.dot(x, wgb_ref[...], preferred_element_type=F32) for x in xs]
        ups = [jnp.dot(x, wub_ref[...], preferred_element_type=F32) for x in xs]
        acts = [(_silu(g) * u).astype(BF16) for g, u in zip(gates, ups)]
        for k, act in enumerate(acts):
            o_ref[k * half:(k + 1) * half, :] = jnp.dot(act, wdb_ref[...], preferred_element_type=F32)

    @pl.when(valid > half)
    def _():
        ffn(2)

    @pl.when((valid > 0) & (valid <= half))
    def _():
        ffn(1)
        o_ref[half:, :] = jnp.zeros((half, o_ref.shape[1]), F32)

    @pl.when(valid == 0)
    def _():
        o_ref[...] = jnp.zeros_like(o_ref)


def moe_experts(blocks, xs, w_gate, w_up, w_down, layer):
    n_slots, d = xs.shape
    rows = MOE_ROWS
    ff = w_gate.shape[3]
    grid_spec = pltpu.PrefetchScalarGridSpec(
        num_scalar_prefetch=len(blocks),
        grid=(n_slots // rows,),
        in_specs=[
            pl.BlockSpec((rows, d), lambda i, *_: (i, 0)),
            pl.BlockSpec(memory_space=pl.ANY),
            pl.BlockSpec(memory_space=pl.ANY),
            pl.BlockSpec(memory_space=pl.ANY),
        ],
        out_specs=pl.BlockSpec((rows, d), lambda i, *_: (i, 0)),
        scratch_shapes=[pltpu.VMEM((2, d, ff), F32), pltpu.VMEM((2, d, ff), F32), pltpu.VMEM((2, ff, d), F32),
                        pltpu.VMEM((d, ff), BF16), pltpu.VMEM((d, ff), BF16), pltpu.VMEM((ff, d), BF16),
                        pltpu.SemaphoreType.DMA((2, 3))],
    )
    return pl.pallas_call(
        functools.partial(_expert_kernel, layer=layer),
        grid_spec=grid_spec,
        out_shape=jax.ShapeDtypeStruct((n_slots, d), F32),
        compiler_params=_cparams("arbitrary"),
        name="moe_experts",
    )(*blocks, xs, w_gate, w_up, w_down)


def _sc_mesh():
    return plsc.VectorSubcoreMesh(core_axis_name="c", subcore_axis_name="s",
                                  num_cores=SC_CORES, num_subcores=SC_SUBCORES)


def _sc_worker():
    return lax.axis_index("s") * SC_CORES + lax.axis_index("c")


def sc_scatter_rows(x, dest0, dest1, n_slots):
    n_tok, d = x.shape
    per_worker = n_tok // SC_WORKERS
    n_chunks = per_worker // SC_CHUNK
    shape3 = (SC_WORKERS, n_chunks, SC_CHUNK)

    @functools.partial(
        pl.kernel, mesh=_sc_mesh(), out_type=jax.ShapeDtypeStruct((n_slots, d), x.dtype),
        scratch_types=[pltpu.VMEM((n_chunks, SC_CHUNK), jnp.int32), pltpu.VMEM((n_chunks, SC_CHUNK), jnp.int32),
                       pltpu.VMEM((SC_CHUNK, d), x.dtype)],
        name="moe_scatter_rows")
    def scatter(x_hbm, d0_hbm, d1_hbm, out_hbm, i0_v, i1_v, rows_v):
        wid = _sc_worker()
        pltpu.sync_copy(d0_hbm.at[wid], i0_v)
        pltpu.sync_copy(d1_hbm.at[wid], i1_v)

        @pl.loop(0, n_chunks)
        def _(j):
            start = pl.multiple_of(wid * per_worker + j * SC_CHUNK, SC_CHUNK)
            pltpu.sync_copy(x_hbm.at[pl.ds(start, SC_CHUNK)], rows_v)
            pltpu.sync_copy(rows_v, out_hbm.at[i0_v.at[j]])
            pltpu.sync_copy(rows_v, out_hbm.at[i1_v.at[j]])

    return scatter(x, dest0.reshape(shape3), dest1.reshape(shape3))


def sc_gather_rows(table, idx):
    n_out = idx.shape[0]
    d = table.shape[1]
    per_worker = n_out // SC_WORKERS
    n_chunks = per_worker // SC_CHUNK

    @functools.partial(
        pl.kernel, mesh=_sc_mesh(), out_type=jax.ShapeDtypeStruct((n_out, d), table.dtype),
        scratch_types=[pltpu.VMEM((n_chunks, SC_CHUNK), jnp.int32), pltpu.VMEM((SC_CHUNK, d), table.dtype)],
        name="moe_gather_rows")
    def gather(table_hbm, idx_hbm, out_hbm, idx_v, rows_v):
        wid = _sc_worker()
        pltpu.sync_copy(idx_hbm.at[wid], idx_v)

        @pl.loop(0, n_chunks)
        def _(j):
            start = pl.multiple_of(wid * per_worker + j * SC_CHUNK, SC_CHUNK)
            pltpu.sync_copy(table_hbm.at[idx_v.at[j]], rows_v)
            pltpu.sync_copy(rows_v, out_hbm.at[pl.ds(start, SC_CHUNK)])

    return gather(table, idx.reshape(SC_WORKERS, n_chunks, SC_CHUNK))


def _combine_kernel(h_ref, y0_ref, y1_ref, r_ref, g_ref, o_ref, *, final_norm):
    route = r_ref[...]
    h = h_ref[...] + (route[:, 2:3] * y0_ref[...] + route[:, 3:4] * y1_ref[...])
    o_ref[...] = _rms(h, g_ref[...]) if final_norm else h


def moe_combine(h, y01, route, g, final_norm, tm=512):
    m, d = h.shape
    tm = min(tm, m)
    spec = pl.BlockSpec((tm, d), lambda i: (i, 0))
    second = pl.BlockSpec((tm, d), lambda i: (i + m // tm, 0))
    return pl.pallas_call(
        functools.partial(_combine_kernel, final_norm=final_norm),
        grid=(m // tm,),
        in_specs=[spec, spec, second, pl.BlockSpec((tm, LANES), lambda i: (i, 0)),
                  pl.BlockSpec((1, d), lambda i: (0, 0))],
        out_specs=spec,
        out_shape=jax.ShapeDtypeStruct((m, d), F32),
        compiler_params=_cparams("parallel"),
        name="moe_combine",
    )(h, y01, y01, route, g.reshape(1, d))


def _pad_cols(w):
    return jnp.pad(w, ((0, 0), (0, LANES - w.shape[1])))


def _dispatch(route, counts, n_tok):
    rows = MOE_ROWS
    counts = counts[0, MOE_GROUPS:MOE_GROUPS + MOE_EXPERTS].astype(jnp.int32)
    padded = (counts + rows - 1) // rows * rows
    pad_end = jnp.cumsum(padded)
    pad_start = pad_end - padded
    eid = route[:, 0:2].astype(jnp.int32)
    rank = route[:, 4:6].astype(jnp.int32)
    hot = eid[:, :, None] == jnp.arange(MOE_EXPERTS, dtype=jnp.int32)
    dest = jnp.sum(jnp.where(hot, pad_start, 0), axis=-1) + rank
    n_blocks = -(-(2 * n_tok + MOE_EXPERTS * (rows - 1)) // rows)
    block_start = jnp.arange(n_blocks, dtype=jnp.int32) * rows
    block_eid = jnp.minimum(jnp.sum(block_start[:, None] >= pad_end[None, :], axis=-1), MOE_EXPERTS - 1)
    filled = (pad_start + counts)[block_eid]
    block_valid = jnp.clip(filled - block_start, 0, rows)
    used = block_valid > 0
    first = used & jnp.concatenate([jnp.ones((1,), bool), block_eid[1:] != block_eid[:-1]])
    slot = (jnp.cumsum(first) - 1) % 2
    order = jnp.arange(n_blocks, dtype=jnp.int32)
    later_first = lax.cummin(jnp.where(first, order, n_blocks)[::-1])[::-1]
    following = jnp.concatenate([later_first[1:], jnp.full((1,), n_blocks, jnp.int32)])
    block_next = jnp.where(following < n_blocks, block_eid[jnp.minimum(following, n_blocks - 1)], -1)
    blocks = tuple(a.astype(jnp.int32) for a in (block_eid, block_valid, first, slot, block_next))
    return dest[:, 0], dest[:, 1], blocks, n_blocks * rows


def _router_weights(w_group, b_group, w_expert, b_expert):
    w_r = _pad_cols(jnp.concatenate([w_group, w_expert], axis=1))
    w_hi = w_r.astype(BF16)
    w_lo = (w_r - w_hi.astype(F32)).astype(BF16)
    return w_hi, w_lo, _pad_lanes(jnp.concatenate([b_group, b_expert]))


def _moe_layer(h, xn, route, counts, w_gate, w_up, w_down, layer, final_g):
    n_tok, d = h.shape
    dest0, dest1, blocks, n_slots = _dispatch(route, counts, n_tok)
    xs = sc_scatter_rows(xn, dest0, dest1, n_slots)
    ys = moe_experts(blocks, xs, w_gate, w_up, w_down, layer)
    y01 = sc_gather_rows(ys, jnp.concatenate([dest0, dest1]))
    g = jnp.ones((d,), F32) if final_g is None else final_g
    return moe_combine(h, y01, route, g, final_g is not None)


def _memory_kv(memn_in, mem_norm, wk, wv):
    bsz, m, d = memn_in.shape
    w = jnp.concatenate([wk, wv], axis=1).astype(BF16)
    kv, _ = rms_matmul(memn_in.reshape(bsz * m, d), mem_norm, w, jnp.zeros((d, LANES), BF16))
    k = kv[:, :d].reshape(bsz, m, d)
    v = kv[:, d:].reshape(bsz, m, d)
    return jnp.swapaxes(k, 1, 2).astype(BF16), v.astype(BF16)


def kernel(x, mem, mem_norm, final_norm, norm_mix, norm_xa, norm_ffn, xa_wq, xa_wk, xa_wv, xa_wo, moe_w_group, moe_b_group, moe_w_expert, moe_b_expert, moe_w_gate, moe_w_up, moe_w_down, ev_w_in, ev_sc_conv, ev_ssm_conv_w, ev_ssm_conv_b, ev_ssm_dt_bias, ev_ssm_a_log, ev_ssm_d, ev_ssm_norm, ev_w_out, od_w_in, od_gdn_conv, od_gdn_dt_bias, od_gdn_a_log, od_gdn_norm, od_w_out):
    bsz, length, d = x.shape
    n_tok = bsz * length
    depth = norm_mix.shape[0]
    h = x.reshape(n_tok, d)
    for layer in range(depth):
        i = layer // 2
        if layer % 2 == 0:
            w = ev_w_in[i]
            z0 = 3 * SC_DIM
            xbc0 = z0 + SSM_INNER
            w_conv = w[:, xbc0:xbc0 + SSM_XBC].astype(BF16)
            w_main = jnp.concatenate([w[:, z0:xbc0], w[:, :z0]], axis=1).astype(BF16)
            w_small = _pad_cols(w[:, xbc0 + SSM_XBC:]).astype(BF16)
            xbc = rms_matmul_conv(h, norm_mix[layer], w_conv, ev_ssm_conv_w[i], ev_ssm_conv_b[i], length)
            y, small = rms_matmul(h, norm_mix[layer], w_main, w_small, tm=1024, tn=w_main.shape[1])
            y3 = y.reshape(bsz, length, -1)
            small3 = small.reshape(bsz, length, LANES)
            smallt = jnp.swapaxes(small3[:, :, :16], 1, 2)
            ya = short_conv_mixer(y3, ev_sc_conv[i], SSM_INNER // SC_DIM)
            yb = ssd_mixer(xbc.reshape(bsz, length, -1), y3, small3, smallt, ev_ssm_dt_bias[i],
                           ev_ssm_a_log[i], ev_ssm_d[i], ev_ssm_norm[i])
            w_out = ev_w_out[i].astype(BF16)
            split = SC_DIM
        else:
            w = od_w_in[i]
            qkv_w = 3 * GDN_HEADS * GDN_D
            z_end = qkv_w + GDN_HEADS * GDN_D
            w_conv = w[:, :qkv_w].astype(BF16)
            w_main = jnp.concatenate([w[:, qkv_w:z_end], w[:, z_end + 2 * GDN_HEADS:]], axis=1).astype(BF16)
            w_small = _pad_cols(w[:, z_end:z_end + 2 * GDN_HEADS]).astype(BF16)
            qkv = rms_matmul_conv(h, norm_mix[layer], w_conv, od_gdn_conv[i], jnp.zeros((qkv_w,), F32), length)
            y, small = rms_matmul(h, norm_mix[layer], w_main, w_small, tm=1024, tn=w_main.shape[1])
            y3 = y.reshape(bsz, length, -1)
            small3 = small.reshape(bsz, length, LANES)
            smallt = jnp.swapaxes(small3[:, :, :16], 1, 2)
            ya = gated_deltanet_mixer(qkv.reshape(bsz, length, -1), y3, small3, smallt, od_gdn_dt_bias[i],
                                      od_gdn_a_log[i], od_gdn_norm[i])
            yb = stick_breaking_mixer(y3, GDN_HEADS * GDN_D)
            w_out = od_w_out[i].astype(BF16)
            split = GDN_HEADS * GDN_D
        kt, v = _memory_kv(mem, mem_norm, xa_wk[layer], xa_wv[layer])
        w_hi, w_lo, bias = _router_weights(moe_w_group[layer], moe_b_group[layer], moe_w_expert[layer],
                                           moe_b_expert[layer])
        h, xn, route, counts = post_mixer(
            ya.reshape(n_tok, -1), yb.reshape(n_tok, -1), h, w_out[:split], w_out[split:], norm_xa[layer],
            xa_wq[layer].astype(BF16), kt, v, xa_wo[layer].astype(BF16), norm_ffn[layer], w_hi, w_lo, bias)
        h = _moe_layer(h, xn, route, counts, moe_w_gate, moe_w_up, moe_w_down, layer,
                       final_norm if layer == depth - 1 else None)
    return h.reshape(bsz, length, d)
```

```python
import functools

import jax
import jax.numpy as jnp
import numpy as np
from jax import lax
from jax.experimental import pallas as pl
from jax.experimental.pallas import tpu as pltpu
from jax.experimental.pallas import tpu_sc as plsc

F32 = jnp.float32
BF16 = jnp.bfloat16
EPS = 1e-6

D_MODEL = 1024
MEM_LEN = 256
SC_DIM = 512
SSM_HEADS = 16
SSM_HEAD_DIM = 64
SSM_INNER = 1024
SSM_GROUPS = 2
SSM_STATE = 128
SSM_XBC = SSM_INNER + 2 * SSM_GROUPS * SSM_STATE
SSD_CHUNK = 128
GDN_HEADS = 8
GDN_D = 128
GDN_CHUNK = 64
GDN_TILE = 128
SB_HEADS = 8
SB_HEAD_DIM = 64
SB_DIM = 512
SB_BLOCK = 128
SB_STEP_HEADS = 8
XA_HEADS = 4
XA_HEAD_DIM = 256
MOE_GROUPS = 4
MOE_PER_GROUP = 8
MOE_EXPERTS = 32
MOE_FF = 512
MOE_ROWS = 512
POST_GROUPS = 1
SC_CORES = 2
SC_SUBCORES = 16
SC_WORKERS = SC_CORES * SC_SUBCORES
SC_CHUNK = 32
HALO = 8
CONV_CHUNK = 512
LANES = 128
SB_LOG_ZERO = -104.0
VMEM_LIMIT = 56 * 1024 * 1024


def _cparams(*sem):
    return pltpu.CompilerParams(dimension_semantics=sem, vmem_limit_bytes=VMEM_LIMIT)


def _mm(a, b):
    return jnp.dot(a.astype(BF16), b.astype(BF16), preferred_element_type=F32)


def _mm_nt(a, b):
    return lax.dot_general(a.astype(BF16), b.astype(BF16), (((1,), (1,)), ((), ())),
                           preferred_element_type=F32)


def _split_bf16(x, n):
    parts, r = [], x
    for _ in range(n):
        p = r.astype(BF16)
        parts.append(p)
        r = r - p.astype(F32)
    return parts


def _mm_sel_rhs(x, sel, n=3):
    return sum(jnp.dot(p, sel, preferred_element_type=F32) for p in _split_bf16(x, n))


def _mm_sel_lhs(sel, x, n=3):
    return sum(jnp.dot(sel, p, preferred_element_type=F32) for p in _split_bf16(x, n))


def _silu(x):
    return x * jax.nn.sigmoid(x)


def _softplus(x):
    return jnp.maximum(x, 0.0) + jnp.log(1.0 + jnp.exp(-jnp.abs(x)))


def _rms(x, g):
    return x * lax.rsqrt(jnp.mean(x * x, axis=-1, keepdims=True) + EPS) * g


def _rms_matmul_kernel(x_ref, g_ref, w_ref, ws_ref, o_ref, os_ref):
    xn = _rms(x_ref[...], g_ref[...]).astype(BF16)
    o_ref[...] = jnp.dot(xn, w_ref[...], preferred_element_type=F32)
    os_ref[...] = jnp.dot(xn, ws_ref[...], preferred_element_type=F32)


def rms_matmul(x, g, w, ws, tm=512, tn=512):
    m, k = x.shape
    n = w.shape[1]
    tm = min(tm, m)
    main, small = pl.pallas_call(
        _rms_matmul_kernel,
        grid=(n // tn, m // tm),
        in_specs=[
            pl.BlockSpec((tm, k), lambda j, i: (i, 0)),
            pl.BlockSpec((1, k), lambda j, i: (0, 0)),
            pl.BlockSpec((k, tn), lambda j, i: (0, j)),
            pl.BlockSpec((k, LANES), lambda j, i: (0, 0)),
        ],
        out_specs=[
            pl.BlockSpec((tm, tn), lambda j, i: (i, j)),
            pl.BlockSpec((None, tm, LANES), lambda j, i: (j, i, 0)),
        ],
        out_shape=[jax.ShapeDtypeStruct((m, n), F32), jax.ShapeDtypeStruct((n // tn, m, LANES), F32)],
        compiler_params=_cparams("parallel", "parallel"),
        name="rms_matmul",
    )(x, g.reshape(1, k), w, ws)
    return main, small[0]


def _causal_conv(ext_ref, w_ref, rows):
    width = w_ref.shape[0]
    ext = ext_ref[...]
    acc = None
    for j in range(width):
        shift = width - 1 - j
        moved = ext if shift == 0 else pltpu.roll(ext, shift, axis=0)
        term = w_ref[j:j + 1, :] * moved[HALO:HALO + rows, :]
        acc = term if acc is None else acc + term
    return acc


def _rms_matmul_conv_kernel(x_ref, g_ref, w_ref, cw_ref, cb_ref, o_ref, *ext_refs, tiles_per_seq):
    tm = x_ref.shape[0]
    starts_sequence = pl.program_id(1) % tiles_per_seq == 0

    @pl.when(starts_sequence)
    def _():
        for ext_ref in ext_refs:
            ext_ref[0:HALO, :] = jnp.zeros((HALO, CONV_CHUNK), F32)

    @pl.when(jnp.logical_not(starts_sequence))
    def _():
        for ext_ref in ext_refs:
            ext_ref[0:HALO, :] = ext_ref[tm:tm + HALO, :]

    xn = _rms(x_ref[...], g_ref[...]).astype(BF16)
    for c, ext_ref in enumerate(ext_refs):
        cols = slice(c * CONV_CHUNK, (c + 1) * CONV_CHUNK)
        ext_ref[HALO:, :] = jnp.dot(xn, w_ref[:, cols], preferred_element_type=F32)
        o_ref[:, cols] = _silu(_causal_conv(ext_ref, cw_ref.at[:, cols], tm) + cb_ref[:, cols])


def rms_matmul_conv(x, g, w, conv_w, conv_b, seq_len, tm=1024, tn=1536):
    m, k = x.shape
    n = w.shape[1]
    cols = lambda rows: pl.BlockSpec((rows, tn), lambda j, i: (0, j))
    return pl.pallas_call(
        functools.partial(_rms_matmul_conv_kernel, tiles_per_seq=seq_len // tm),
        grid=(n // tn, m // tm),
        in_specs=[
            pl.BlockSpec((tm, k), lambda j, i: (i, 0)),
            pl.BlockSpec((1, k), lambda j, i: (0, 0)),
            cols(k), cols(conv_w.shape[0]), cols(1),
        ],
        out_specs=pl.BlockSpec((tm, tn), lambda j, i: (i, j)),
        out_shape=jax.ShapeDtypeStruct((m, n), F32),
        scratch_shapes=[pltpu.VMEM((tm + HALO, CONV_CHUNK), F32)] * (tn // CONV_CHUNK),
        compiler_params=_cparams("arbitrary", "arbitrary"),
        name="rms_matmul_conv",
    )(x, g.reshape(1, k), w, conv_w, conv_b.reshape(1, n))


def _halo_index(rows):
    step = rows // HALO
    return lambda i: jnp.maximum(i * step - 1, 0)


def _sc_kernel(b_ref, c_ref, x_ref, ch_ref, xh_ref, w_ref, o_ref, ext_ref):
    rows = o_ref.shape[0]
    first = pl.program_id(1) == 0
    ext_ref[0:HALO, :] = jnp.where(first, 0.0, ch_ref[...] * xh_ref[...])
    ext_ref[HALO:, :] = c_ref[...] * x_ref[...]
    o_ref[...] = b_ref[...] * _causal_conv(ext_ref, w_ref, rows)


def short_conv_mixer(y3, w, col0, tl=512):
    bsz, length, _ = y3.shape
    tl = min(tl, length)
    hidx = _halo_index(tl)
    return pl.pallas_call(
        _sc_kernel,
        grid=(bsz, length // tl),
        in_specs=[
            pl.BlockSpec((None, tl, SC_DIM), lambda b, i: (b, i, col0)),
            pl.BlockSpec((None, tl, SC_DIM), lambda b, i: (b, i, col0 + 1)),
            pl.BlockSpec((None, tl, SC_DIM), lambda b, i: (b, i, col0 + 2)),
            pl.BlockSpec((None, HALO, SC_DIM), lambda b, i: (b, hidx(i), col0 + 1)),
            pl.BlockSpec((None, HALO, SC_DIM), lambda b, i: (b, hidx(i), col0 + 2)),
            pl.BlockSpec(w.shape, lambda b, i: (0, 0)),
        ],
        out_specs=pl.BlockSpec((None, tl, SC_DIM), lambda b, i: (b, i, 0)),
        out_shape=jax.ShapeDtypeStruct((bsz, length, SC_DIM), F32),
        scratch_shapes=[pltpu.VMEM((tl + HALO, SC_DIM), F32)],
        compiler_params=_cparams("parallel", "arbitrary"),
        name="short_conv_mixer",
    )(y3, y3, y3, y3, y3, w)


def _ssd_kernel(xbc_ref, z_ref, dt_ref, dtt_ref, dtb_r_ref, dtb_c_ref,
                alog_r_ref, alog_c_ref, d_ref, nw_ref, tri_ref, trit_ref, eh_ref, eq_ref,
                o_ref, s_ref):
    q = SSD_CHUNK
    hpg = SSM_HEADS // SSM_GROUPS
    gw = hpg * SSM_HEAD_DIM

    @pl.when(pl.program_id(1) == 0)
    def _():
        s_ref[...] = jnp.zeros_like(s_ref)

    xbc = xbc_ref[...]
    xs = xbc[:, :SSM_INNER]
    bm = xbc[:, SSM_INNER:SSM_INNER + SSM_GROUPS * SSM_STATE]
    cm = xbc[:, SSM_INNER + SSM_GROUPS * SSM_STATE:]

    dt = _softplus(dt_ref[...] + dtb_r_ref[...])
    acs = _mm_sel_lhs(tri_ref[...], dt * -jnp.exp(alog_r_ref[...]))
    dtt = _softplus(dtt_ref[...] + dtb_c_ref[...])
    acst = _mm_sel_rhs(dtt * -jnp.exp(alog_c_ref[...]), trit_ref[...])
    dt_full = _mm_sel_rhs(dt, eh_ref[...])
    acs_full = _mm_sel_rhs(acs, eh_ref[...])
    acs_col = _mm_sel_rhs(acs, eq_ref[...])

    xdt = xs * dt_full
    acs_last = acs_full[q - 1:q, :]
    xw = xdt * jnp.exp(acs_last - acs_full)
    chunk_decay = jnp.exp(acs_last)

    row = lax.broadcasted_iota(jnp.int32, (q, q), 0)
    col = lax.broadcasted_iota(jnp.int32, (q, q), 1)
    causal = row >= col
    lane = lax.broadcasted_iota(jnp.int32, (q, 2 * SSM_HEAD_DIM), 1)

    y_diag, y_off = [], []
    for g in range(SSM_GROUPS):
        bm_g = bm[:, g * SSM_STATE:(g + 1) * SSM_STATE]
        cm_g = cm[:, g * SSM_STATE:(g + 1) * SSM_STATE]
        cb_g = _mm_nt(cm_g, bm_g)
        state = s_ref[g]
        y_off.append(_mm(cm_g, state))
        s_ref[g] = state * chunk_decay[:, g * gw:(g + 1) * gw] + _mm(bm_g.T, xw[:, g * gw:(g + 1) * gw])
        for pair in range(hpg // 2):
            h0 = g * hpg + 2 * pair
            xdt_pair = xdt[:, h0 * SSM_HEAD_DIM:(h0 + 2) * SSM_HEAD_DIM]
            outs = []
            for h in (h0, h0 + 1):
                seg = acs_col[:, h * q:(h + 1) * q] - acst[h:h + 1, :]
                decay = jnp.where(causal, jnp.exp(seg), 0.0)
                outs.append(_mm(cb_g * decay, xdt_pair))
            y_diag.append(jnp.where(lane < SSM_HEAD_DIM, outs[0], outs[1]))
    y = (jnp.concatenate(y_diag, axis=1) + jnp.concatenate(y_off, axis=1) * jnp.exp(acs_full)
         + xs * d_ref[...])
    y = y * _silu(z_ref[...])
    halves = []
    for g in range(SSM_GROUPS):
        yg = y[:, g * gw:(g + 1) * gw]
        halves.append(yg * lax.rsqrt(jnp.mean(yg * yg, axis=-1, keepdims=True) + EPS))
    o_ref[...] = jnp.concatenate(halves, axis=1) * nw_ref[...]


def _pad_lanes(v, fill=0.0):
    return jnp.pad(v.astype(F32), (0, LANES - v.shape[0]), constant_values=fill).reshape(1, LANES)


def _pad_col(v, rows=16):
    return jnp.pad(v.astype(F32), (0, rows - v.shape[0])).reshape(rows, 1)


def ssd_mixer(xbc3, y3, small3, smallt, dt_bias, a_log, d_skip, norm_w):
    bsz, length, _ = y3.shape
    q = SSD_CHUNK
    tri = jnp.asarray(np.tril(np.ones((q, q), np.float32)), BF16)
    trit = jnp.asarray(np.triu(np.ones((q, q), np.float32)), BF16)
    heads = np.arange(LANES)[:, None]
    eh = jnp.asarray(heads == (np.arange(SSM_INNER)[None, :] // SSM_HEAD_DIM), BF16)
    eq = jnp.asarray(heads == (np.arange(SSM_HEADS * q)[None, :] // q), BF16)
    d_full = jnp.repeat(d_skip.astype(F32), SSM_HEAD_DIM).reshape(1, SSM_INNER)
    const = lambda a: pl.BlockSpec(a.shape, lambda b, c: (0,) * a.ndim)
    args = [_pad_lanes(dt_bias), _pad_col(dt_bias), _pad_lanes(a_log),
            _pad_col(a_log), d_full, norm_w.reshape(1, -1), tri, trit, eh, eq]
    return pl.pallas_call(
        _ssd_kernel,
        grid=(bsz, length // q),
        in_specs=[
            pl.BlockSpec((None, q, SSM_XBC), lambda b, c: (b, c, 0)),
            pl.BlockSpec((None, q, SSM_INNER), lambda b, c: (b, c, 0)),
            pl.BlockSpec((None, q, LANES), lambda b, c: (b, c, 0)),
            pl.BlockSpec((None, 16, q), lambda b, c: (b, 0, c)),
        ] + [const(a) for a in args],
        out_specs=pl.BlockSpec((None, q, SSM_INNER), lambda b, c: (b, c, 0)),
        out_shape=jax.ShapeDtypeStruct((bsz, length, SSM_INNER), F32),
        scratch_shapes=[pltpu.VMEM((SSM_GROUPS, SSM_STATE, SSM_INNER // SSM_GROUPS), F32)],
        compiler_params=_cparams("parallel", "arbitrary"),
        name="ssd_mixer",
    )(xbc3, y3, small3, smallt, *args)


def _unit_lower_inverse(mats, row, col):
    eye = jnp.where(row == col, 1.0, 0.0)
    blk = lambda n: (row >> (n.bit_length() - 1)) == (col >> (n.bit_length() - 1))
    p = [jnp.where(blk(16), -a, 0.0) for a in mats]
    t = [eye + x for x in p]
    for _ in range(3):
        p = [_mm(x, x) for x in p]
        t = [y + _mm(y, x) for y, x in zip(t, p)]
    for n in (16, 32):
        band = blk(2 * n) & jnp.logical_not(blk(n))
        left = [_mm(y, jnp.where(band, a, 0.0)) for y, a in zip(t, mats)]
        t = [y - _mm(x, y) for y, x in zip(t, left)]
    return t


def _gdn_kernel(qkv_ref, z_ref, ab_ref, abt_ref, dtb_r_ref, dtb_c_ref, alog_r_ref,
                alog_c_ref, nw_ref, tri_ref, trit_ref, eg_ref, eb_ref, o_ref, s_ref):
    n = GDN_TILE
    c = GDN_CHUNK
    d = GDN_D
    hd = GDN_HEADS * d

    @pl.when(pl.program_id(1) == 0)
    def _():
        s_ref[...] = jnp.zeros_like(s_ref)

    qkv = qkv_ref[...]
    z = z_ref[...]

    ab = ab_ref[...]
    g = -jnp.exp(alog_r_ref[...]) * _softplus(ab + dtb_r_ref[...])
    gc_full = _mm_sel_rhs(_mm_sel_lhs(tri_ref[...], g), eg_ref[...])
    beta_full = _mm_sel_rhs(jax.nn.sigmoid(ab), eb_ref[...])
    gt = -jnp.exp(alog_c_ref[...]) * _softplus(abt_ref[...] + dtb_c_ref[...])
    gct = _mm_sel_rhs(gt, trit_ref[...])

    row = lax.broadcasted_iota(jnp.int32, (n, n), 0)
    col = lax.broadcasted_iota(jnp.int32, (n, n), 1)
    same = (row >> (c.bit_length() - 1)) == (col >> (c.bit_length() - 1))
    incl = same & (row >= col)
    strict = same & (row > col)
    zeros_half = jnp.zeros((c, d), F32)

    heads = range(GDN_HEADS)
    sl = [slice(h * d, (h + 1) * d) for h in heads]
    l2n = lambda x: x * lax.rsqrt(jnp.sum(x * x, axis=-1, keepdims=True) + EPS)
    qn = [l2n(qkv[:, sl[h]]) * (d ** -0.5) for h in heads]
    kn = [l2n(qkv[:, hd + h * d:hd + (h + 1) * d]) for h in heads]
    vh = [qkv[:, 2 * hd + h * d:2 * hd + (h + 1) * d] for h in heads]
    gcol = [gc_full[:, sl[h]] for h in heads]
    beta = [beta_full[:, sl[h]] for h in heads]
    edec = [jnp.exp(gcol[h] - gct[h:h + 1, :]) for h in heads]
    egc = [jnp.exp(x) for x in gcol]
    kb = [kn[h] * beta[h] for h in heads]
    lower = [jnp.where(strict, _mm_nt(kb[h], kn[h]) * edec[h], 0.0) for h in heads]
    aqk = [jnp.where(incl, _mm_nt(qn[h], kn[h]) * edec[h], 0.0) for h in heads]
    tinv = _unit_lower_inverse(lower, row, col)
    sol = [_mm(tinv[h], jnp.concatenate([vh[h] * beta[h], kb[h] * egc[h]], axis=1)) for h in heads]
    qd = [qn[h] * egc[h] for h in heads]
    glast = [(gcol[h][c - 1:c, :], gcol[h][n - 1:n, :]) for h in heads]
    kdt = [(kn[h] * jnp.exp(jnp.concatenate([jnp.broadcast_to(glast[h][0], (c, d)),
                                             jnp.broadcast_to(glast[h][1], (c, d))], axis=0) - gcol[h])).T
           for h in heads]
    s0 = [s_ref[h] for h in heads]
    v0 = [sol[h][:c, :d] - _mm(sol[h][:c, d:], s0[h]) for h in heads]
    s1 = [s0[h] * jnp.exp(glast[h][0]) + _mm(kdt[h], jnp.concatenate([v0[h], zeros_half], axis=0)) for h in heads]
    v1 = [sol[h][c:, :d] - _mm(sol[h][c:, d:], s1[h]) for h in heads]
    for h in heads:
        s_ref[h] = s1[h] * jnp.exp(glast[h][1]) + _mm(kdt[h], jnp.concatenate([zeros_half, v1[h]], axis=0))
    outs = []
    for h in heads:
        o = (jnp.concatenate([_mm(qd[h][:c], s0[h]), _mm(qd[h][c:], s1[h])], axis=0)
             + _mm(aqk[h], jnp.concatenate([v0[h], v1[h]], axis=0)))
        o = o * lax.rsqrt(jnp.mean(o * o, axis=-1, keepdims=True) + EPS) * nw_ref[...]
        outs.append(o * _silu(z[:, sl[h]]))
    o_ref[...] = jnp.concatenate(outs, axis=1)


def gated_deltanet_mixer(qkv3, y3, small3, smallt, dt_bias, a_log, norm_w):
    bsz, length, _ = y3.shape
    n = GDN_TILE
    hd = GDN_HEADS * GDN_D
    idx = np.arange(n)
    same = (idx[:, None] // GDN_CHUNK) == (idx[None, :] // GDN_CHUNK)
    tri = jnp.asarray(same & (idx[:, None] >= idx[None, :]), BF16)
    trit = jnp.asarray(same & (idx[:, None] <= idx[None, :]), BF16)
    lanes = np.arange(LANES)[:, None]
    heads = np.arange(hd)[None, :] // GDN_D
    eg = jnp.asarray(lanes == heads, BF16)
    eb = jnp.asarray(lanes == heads + GDN_HEADS, BF16)
    const = lambda a: pl.BlockSpec(a.shape, lambda b, c: (0,) * a.ndim)
    args = [_pad_lanes(dt_bias), _pad_col(dt_bias), _pad_lanes(a_log), _pad_col(a_log),
            norm_w.reshape(1, -1), tri, trit, eg, eb]
    return pl.pallas_call(
        _gdn_kernel,
        grid=(bsz, length // n),
        in_specs=[
            pl.BlockSpec((None, n, 3 * hd), lambda b, c: (b, c, 0)),
            pl.BlockSpec((None, n, hd), lambda b, c: (b, c, 0)),
            pl.BlockSpec((None, n, LANES), lambda b, c: (b, c, 0)),
            pl.BlockSpec((None, 16, n), lambda b, c: (b, 0, c)),
        ] + [const(a) for a in args],
        out_specs=pl.BlockSpec((None, n, hd), lambda b, c: (b, c, 0)),
        out_shape=jax.ShapeDtypeStruct((bsz, length, hd), F32),
        scratch_shapes=[pltpu.VMEM((GDN_HEADS, GDN_D, GDN_D), F32)],
        compiler_params=_cparams("parallel", "arbitrary"),
        name="gated_deltanet",
    )(qkv3, y3, small3, smallt, *args)


def _sb_kernel(q_ref, k_ref, v_ref, upper_ref, o_ref):
    blk = SB_BLOCK
    pair_w = 2 * SB_HEAD_DIM
    n_pairs = SB_STEP_HEADS // 2
    i = pl.program_id(2)
    q = q_ref[...] * (SB_HEAD_DIM ** -0.5)
    lane = lax.broadcasted_iota(jnp.int32, (blk, pair_w), 1)
    first_head = lane < SB_HEAD_DIM
    qs = []
    for p in range(n_pairs):
        q2 = q[:, p * pair_w:(p + 1) * pair_w]
        qs += [jnp.where(first_head, q2, 0.0).astype(BF16), jnp.where(first_head, 0.0, q2).astype(BF16)]
    row = lax.broadcasted_iota(jnp.int32, (blk, blk), 0)
    col = lax.broadcasted_iota(jnp.int32, (blk, blk), 1)
    earlier = col < row
    upper = upper_ref[...]
    heads = range(SB_STEP_HEADS)

    def key_block(kb, accs, sticks, diagonal):
        start = pl.multiple_of(kb * blk, blk)
        k = k_ref[pl.ds(start, blk), :].astype(BF16)
        v = v_ref[pl.ds(start, blk), :].astype(BF16)
        kp = [k[:, p * pair_w:(p + 1) * pair_w] for p in range(n_pairs)]
        vp = [v[:, p * pair_w:(p + 1) * pair_w] for p in range(n_pairs)]
        logits = [lax.dot_general(qs[h], kp[h // 2], (((1,), (1,)), ((), ())), preferred_element_type=F32)
                  for h in heads]
        log_keep = [-_softplus(x) for x in logits]
        if diagonal:
            log_keep = [jnp.where(earlier, x, 0.0) for x in log_keep]
        between = [_mm_sel_rhs(log_keep[h], upper, 2) + sticks[h] for h in heads]
        w = [jnp.exp(logits[h] + log_keep[h] + between[h]) for h in heads]
        if diagonal:
            w = [jnp.where(earlier, x, 0.0) for x in w]
        pv = [jnp.dot(w[h].astype(BF16), vp[h // 2], preferred_element_type=F32) for h in heads]
        accs = tuple(accs[p] + jnp.where(first_head, pv[2 * p], pv[2 * p + 1]) for p in range(n_pairs))
        sticks = tuple(sticks[h] + jnp.sum(log_keep[h], axis=-1, keepdims=True) for h in heads)
        return accs, sticks

    accs = tuple(jnp.zeros((blk, pair_w), F32) for _ in range(n_pairs))
    sticks = tuple(jnp.zeros((blk, 1), F32) for _ in heads)
    accs, sticks = key_block(i, accs, sticks, True)

    def alive(state):
        kb, _, sticks = state
        longest = sticks[0]
        for s in sticks[1:]:
            longest = jnp.maximum(longest, s)
        return (kb >= 0) & (jnp.max(longest) > SB_LOG_ZERO)

    def body(state):
        kb, accs, sticks = state
        accs, sticks = key_block(kb, accs, sticks, False)
        return kb - 1, accs, sticks

    _, accs, _ = lax.while_loop(alive, body, (i - 1, accs, sticks))
    o_ref[...] = jnp.concatenate(accs, axis=1)


def stick_breaking_mixer(y3, col0):
    bsz, length, _ = y3.shape
    blk = SB_BLOCK
    step_w = SB_STEP_HEADS * SB_HEAD_DIM
    steps = SB_DIM // step_w
    q0 = col0 // step_w
    idx = np.arange(blk)
    upper = jnp.asarray(idx[:, None] > idx[None, :], BF16)
    resident = lambda off: pl.BlockSpec((None, length, step_w), lambda b, p, i: (b, 0, q0 + off + p),
                                        pipeline_mode=pl.Buffered(1))
    return pl.pallas_call(
        _sb_kernel,
        grid=(bsz, steps, length // blk),
        in_specs=[
            pl.BlockSpec((None, blk, step_w), lambda b, p, i: (b, i, q0 + p)),
            resident(steps),
            resident(2 * steps),
            pl.BlockSpec((blk, blk), lambda b, p, i: (0, 0)),
        ],
        out_specs=pl.BlockSpec((None, blk, step_w), lambda b, p, i: (b, i, p)),
        out_shape=jax.ShapeDtypeStruct((bsz, length, SB_DIM), F32),
        compiler_params=_cparams("parallel", "parallel", "arbitrary"),
        name="stick_breaking",
    )(y3, y3, y3, upper)


def _mixer_out(a_ref, b_ref, h_ref, wa_ref, wb_ref, rows):
    return h_ref[rows, :] + (jnp.dot(a_ref[rows, :].astype(BF16), wa_ref[...], preferred_element_type=F32)
                             + jnp.dot(b_ref[rows, :].astype(BF16), wb_ref[...], preferred_element_type=F32))


def _cross_attention(hs, g_ref, wq_ref, kt_ref, v_ref, wo_ref):
    us = [_rms(h, g_ref[...]).astype(BF16) for h in hs]
    qs = [jnp.dot(u, wq_ref[...], preferred_element_type=F32) for u in us]
    heads = [[] for _ in hs]
    for hd in range(XA_HEADS):
        sl = slice(hd * XA_HEAD_DIM, (hd + 1) * XA_HEAD_DIM)
        ss = [jnp.dot(q[:, sl].astype(BF16), kt_ref[sl, :], preferred_element_type=F32) * (XA_HEAD_DIM ** -0.5)
              for q in qs]
        ps = [jnp.exp(s - jnp.max(s, axis=-1, keepdims=True)) for s in ss]
        ps = [p / jnp.sum(p, axis=-1, keepdims=True) for p in ps]
        for k, p in enumerate(ps):
            heads[k].append(jnp.dot(p.astype(BF16), v_ref[:, sl], preferred_element_type=F32))
    os_ = [jnp.concatenate(hk, axis=1).astype(BF16) for hk in heads]
    return [h + jnp.dot(o, wo_ref[...], preferred_element_type=F32) for h, o in zip(hs, os_)]


def _route(xn, whi_ref, wlo_ref, b_ref, before_ref, run_ref):
    x_hi = xn.astype(BF16)
    x_lo = (xn - x_hi.astype(F32)).astype(BF16)
    logits = (jnp.dot(x_hi, whi_ref[...], preferred_element_type=F32)
              + jnp.dot(x_lo, whi_ref[...], preferred_element_type=F32)
              + jnp.dot(x_hi, wlo_ref[...], preferred_element_type=F32) + b_ref[...])
    lane = lax.broadcasted_iota(jnp.int32, logits.shape, 1).astype(F32)
    neg = -1e30
    none = float(LANES)

    def top(vals):
        best = jnp.max(vals, axis=-1, keepdims=True)
        where = jnp.min(jnp.where(vals == best, lane, none), axis=-1, keepdims=True)
        return best, where

    gl = jnp.where(lane < MOE_GROUPS, logits, neg)
    gbest, gsel = top(gl)
    gprob = 1.0 / jnp.sum(jnp.exp(gl - gbest), axis=-1, keepdims=True)
    lo = MOE_GROUPS + gsel * MOE_PER_GROUP
    el = jnp.where((lane >= lo) & (lane < lo + MOE_PER_GROUP), logits, neg)
    m1, i1 = top(el)
    m2, i2 = top(jnp.where(lane == i1, neg, el))
    e = jnp.exp(m2 - m1)
    gate1 = gprob / (1.0 + e)
    gate2 = gprob * e / (1.0 + e)

    hot1 = lane == i1
    hot2 = lane == i2
    one1 = jnp.where(hot1, 1.0, 0.0)
    one2 = jnp.where(hot2, 1.0, 0.0)
    before = before_ref[...]
    prefix1 = jnp.dot(before, one1.astype(BF16), preferred_element_type=F32)
    prefix2 = jnp.dot(before, one2.astype(BF16), preferred_element_type=F32)
    total1 = jnp.sum(one1, axis=0, keepdims=True)
    running = run_ref[...]
    rank1 = jnp.sum(jnp.where(hot1, prefix1 + running, 0.0), axis=-1, keepdims=True)
    rank2 = jnp.sum(jnp.where(hot2, prefix2 + (running + total1), 0.0), axis=-1, keepdims=True)
    running = running + total1 + jnp.sum(one2, axis=0, keepdims=True)
    run_ref[...] = running

    fields = (i1 - MOE_GROUPS, i2 - MOE_GROUPS, gate1, gate2, rank1, rank2)
    out = jnp.zeros_like(logits)
    for k, val in enumerate(fields):
        out = jnp.where(lane == k, val, out)
    return out


def _post_mixer_kernel(a_ref, b_ref, h_ref, wa_ref, wb_ref, gxa_ref, wq_ref, kt_ref, v_ref, wo_ref,
                       gffn_ref, whi_ref, wlo_ref, bias_ref, before_ref,
                       h_out_ref, xn_ref, r_ref, cnt_ref, run_ref, *, tiles_per_batch):
    @pl.when(pl.program_id(0) % tiles_per_batch == 0)
    def _():
        run_ref[...] = jnp.zeros_like(run_ref)

    tm = h_ref.shape[0]
    groups = [slice(k * tm // POST_GROUPS, (k + 1) * tm // POST_GROUPS) for k in range(POST_GROUPS)]
    hs = [_mixer_out(a_ref, b_ref, h_ref, wa_ref, wb_ref, rows) for rows in groups]
    hs = _cross_attention(hs, gxa_ref, wq_ref, kt_ref, v_ref, wo_ref)
    for rows, h in zip(groups, hs):
        h_out_ref[rows, :] = h
        xn_ref[rows, :] = _rms(h, gffn_ref[...])
    r_ref[...] = _route(xn_ref[...], whi_ref, wlo_ref, bias_ref, before_ref, run_ref)
    cnt_ref[...] = run_ref[...]


def post_mixer(ya, yb, h, wa, wb, g_xa, wq, kt, v, wo, g_ffn, w_hi, w_lo, bias, tm=512):
    m, d = h.shape
    bsz = kt.shape[0]
    tiles_per_batch = m // bsz // tm
    idx = np.arange(tm)
    before = jnp.asarray(idx[:, None] > idx[None, :], BF16)
    rows = lambda w: pl.BlockSpec((tm, w), lambda i: (i, 0))
    const = lambda a: pl.BlockSpec(a.shape, lambda i: (0,) * a.ndim, pipeline_mode=pl.Buffered(1))
    per_batch = lambda a: pl.BlockSpec((None,) + a.shape[1:], lambda i: (i // tiles_per_batch, 0, 0))
    g_xa, g_ffn = g_xa.reshape(1, d), g_ffn.reshape(1, d)
    return pl.pallas_call(
        functools.partial(_post_mixer_kernel, tiles_per_batch=tiles_per_batch),
        grid=(m // tm,),
        in_specs=[rows(ya.shape[1]), rows(yb.shape[1]), rows(d), const(wa), const(wb), const(g_xa), const(wq),
                  per_batch(kt), per_batch(v), const(wo), const(g_ffn), const(w_hi), const(w_lo), const(bias),
                  const(before)],
        out_specs=[rows(d), rows(d), rows(LANES),
                   pl.BlockSpec((None, 1, LANES), lambda i: (i // tiles_per_batch, 0, 0))],
        out_shape=[jax.ShapeDtypeStruct((m, d), F32), jax.ShapeDtypeStruct((m, d), F32),
                   jax.ShapeDtypeStruct((m, LANES), F32), jax.ShapeDtypeStruct((bsz, 1, LANES), F32)],
        scratch_shapes=[pltpu.VMEM((1, LANES), F32)],
        compiler_params=_cparams("arbitrary"),
        name="post_mixer",
    )(ya, yb, h, wa, wb, g_xa, wq, kt, v, wo, g_ffn, w_hi, w_lo, bias, before)


def _expert_kernel(beid_ref, valid_ref, first_ref, slot_ref, next_ref, x_ref, wg_hbm, wu_hbm, wd_hbm, o_ref,
                   wg32_ref, wu32_ref, wd32_ref, wgb_ref, wub_ref, wdb_ref, sem_ref, *, layer):
    i = pl.program_id(0)
    valid = valid_ref[i]

    def weight_copies(expert, slot):
        return (pltpu.make_async_copy(wg_hbm.at[layer, expert], wg32_ref.at[slot], sem_ref.at[slot, 0]),
                pltpu.make_async_copy(wu_hbm.at[layer, expert], wu32_ref.at[slot], sem_ref.at[slot, 1]),
                pltpu.make_async_copy(wd_hbm.at[layer, expert], wd32_ref.at[slot], sem_ref.at[slot, 2]))

    @pl.when(i == 0)
    def _():
        for copy in weight_copies(beid_ref[0], 0):
            copy.start()

    @pl.when(first_ref[i] == 1)
    def _():
        slot = slot_ref[i]
        for copy in weight_copies(beid_ref[i], slot):
            copy.wait()
        wgb_ref[...] = wg32_ref[slot].astype(BF16)
        wub_ref[...] = wu32_ref[slot].astype(BF16)
        wdb_ref[...] = wd32_ref[slot].astype(BF16)

        @pl.when(next_ref[i] >= 0)
        def _():
            for copy in weight_copies(next_ref[i], 1 - slot):
                copy.start()

    half = MOE_ROWS // 2

    def ffn(n_halves):
        row = lax.broadcasted_iota(jnp.int32, (half, x_ref.shape[1]), 0)
        xs = [jnp.where(row + k * half < valid, x_ref[k * half:(k + 1) * half, :], 0.0).astype(BF16)
              for k in range(n_halves)]
        gates = [jnp.dot(x, wgb_ref[...], preferred_element_type=F32) for x in xs]
        ups = [jnp.dot(x, wub_ref[...], preferred_element_type=F32) for x in xs]
        acts = [(_silu(g) * u).astype(BF16) for g, u in zip(gates, ups)]
        for k, act in enumerate(acts):
            o_ref[k * half:(k + 1) * half, :] = jnp.dot(act, wdb_ref[...], preferred_element_type=F32)

    @pl.when(valid > half)
    def _():
        ffn(2)

    @pl.when((valid > 0) & (valid <= half))
    def _():
        ffn(1)
        o_ref[half:, :] = jnp.zeros((half, o_ref.shape[1]), F32)

    @pl.when(valid == 0)
    def _():
        o_ref[...] = jnp.zeros_like(o_ref)


def moe_experts(blocks, xs, w_gate, w_up, w_down, layer):
    n_slots, d = xs.shape
    rows = MOE_ROWS
    ff = w_gate.shape[3]
    grid_spec = pltpu.PrefetchScalarGridSpec(
        num_scalar_prefetch=len(blocks),
        grid=(n_slots // rows,),
        in_specs=[
            pl.BlockSpec((rows, d), lambda i, *_: (i, 0)),
            pl.BlockSpec(memory_space=pl.ANY),
            pl.BlockSpec(memory_space=pl.ANY),
            pl.BlockSpec(memory_space=pl.ANY),
        ],
        out_specs=pl.BlockSpec((rows, d), lambda i, *_: (i, 0)),
        scratch_shapes=[pltpu.VMEM((2, d, ff), F32), pltpu.VMEM((2, d, ff), F32), pltpu.VMEM((2, ff, d), F32),
                        pltpu.VMEM((d, ff), BF16), pltpu.VMEM((d, ff), BF16), pltpu.VMEM((ff, d), BF16),
                        pltpu.SemaphoreType.DMA((2, 3))],
    )
    return pl.pallas_call(
        functools.partial(_expert_kernel, layer=layer),
        grid_spec=grid_spec,
        out_shape=jax.ShapeDtypeStruct((n_slots, d), F32),
        compiler_params=_cparams("arbitrary"),
        name="moe_experts",
    )(*blocks, xs, w_gate, w_up, w_down)


def _sc_mesh():
    return plsc.VectorSubcoreMesh(core_axis_name="c", subcore_axis_name="s",
                                  num_cores=SC_CORES, num_subcores=SC_SUBCORES)


def _sc_worker():
    return lax.axis_index("s") * SC_CORES + lax.axis_index("c")


def sc_scatter_rows(x, row0, dest0, dest1, n_slots):
    n_tok, d = dest0.shape[0], x.shape[1]
    per_worker = n_tok // SC_WORKERS
    n_chunks = per_worker // SC_CHUNK
    shape3 = (SC_WORKERS, n_chunks, SC_CHUNK)

    @functools.partial(
        pl.kernel, mesh=_sc_mesh(), out_type=jax.ShapeDtypeStruct((n_slots, d), x.dtype),
        scratch_types=[pltpu.VMEM((n_chunks, SC_CHUNK), jnp.int32), pltpu.VMEM((n_chunks, SC_CHUNK), jnp.int32),
                       pltpu.VMEM((SC_CHUNK, d), x.dtype)],
        name="moe_scatter_rows")
    def scatter(x_hbm, d0_hbm, d1_hbm, out_hbm, i0_v, i1_v, rows_v):
        wid = _sc_worker()
        pltpu.sync_copy(d0_hbm.at[wid], i0_v)
        pltpu.sync_copy(d1_hbm.at[wid], i1_v)

        @pl.loop(0, n_chunks)
        def _(j):
            start = pl.multiple_of(row0 + wid * per_worker + j * SC_CHUNK, SC_CHUNK)
            pltpu.sync_copy(x_hbm.at[pl.ds(start, SC_CHUNK)], rows_v)
            pltpu.sync_copy(rows_v, out_hbm.at[i0_v.at[j]])
            pltpu.sync_copy(rows_v, out_hbm.at[i1_v.at[j]])

    return scatter(x, dest0.reshape(shape3), dest1.reshape(shape3))


def sc_gather_rows(table, idx):
    n_out = idx.shape[0]
    d = table.shape[1]
    per_worker = n_out // SC_WORKERS
    n_chunks = per_worker // SC_CHUNK

    @functools.partial(
        pl.kernel, mesh=_sc_mesh(), out_type=jax.ShapeDtypeStruct((n_out, d), table.dtype),
        scratch_types=[pltpu.VMEM((n_chunks, SC_CHUNK), jnp.int32), pltpu.VMEM((SC_CHUNK, d), table.dtype)],
        name="moe_gather_rows")
    def gather(table_hbm, idx_hbm, out_hbm, idx_v, rows_v):
        wid = _sc_worker()
        pltpu.sync_copy(idx_hbm.at[wid], idx_v)

        @pl.loop(0, n_chunks)
        def _(j):
            start = pl.multiple_of(wid * per_worker + j * SC_CHUNK, SC_CHUNK)
            pltpu.sync_copy(table_hbm.at[idx_v.at[j]], rows_v)
            pltpu.sync_copy(rows_v, out_hbm.at[pl.ds(start, SC_CHUNK)])

    return gather(table, idx.reshape(SC_WORKERS, n_chunks, SC_CHUNK))


def _combine_kernel(h_ref, y0_ref, y1_ref, r_ref, g_ref, o_ref, *, final_norm):
    route = r_ref[...]
    h = h_ref[...] + (route[:, 2:3] * y0_ref[...] + route[:, 3:4] * y1_ref[...])
    o_ref[...] = _rms(h, g_ref[...]) if final_norm else h


def moe_combine(h, row0, y01, route, g, final_norm, tm=512):
    d = h.shape[1]
    m = y01.shape[0] // 2
    tm = min(tm, m)
    t0 = row0 // tm
    mine = lambda w: pl.BlockSpec((tm, w), lambda i: (t0 + i, 0))
    return pl.pallas_call(
        functools.partial(_combine_kernel, final_norm=final_norm),
        grid=(m // tm,),
        in_specs=[mine(d), pl.BlockSpec((tm, d), lambda i: (i, 0)), pl.BlockSpec((tm, d), lambda i: (i + m // tm, 0)),
                  mine(LANES), pl.BlockSpec((1, d), lambda i: (0, 0))],
        out_specs=mine(d),
        out_shape=jax.ShapeDtypeStruct(h.shape, F32),
        input_output_aliases={0: 0},
        compiler_params=_cparams("parallel"),
        name="moe_combine",
    )(h, y01, y01, route, g.reshape(1, d))


def _pad_cols(w):
    return jnp.pad(w, ((0, 0), (0, LANES - w.shape[1])))


def _dispatch(route, counts, n_tok):
    rows = MOE_ROWS
    counts = counts[0, MOE_GROUPS:MOE_GROUPS + MOE_EXPERTS].astype(jnp.int32)
    padded = (counts + rows - 1) // rows * rows
    pad_end = jnp.cumsum(padded)
    pad_start = pad_end - padded
    eid = route[:, 0:2].astype(jnp.int32)
    rank = route[:, 4:6].astype(jnp.int32)
    hot = eid[:, :, None] == jnp.arange(MOE_EXPERTS, dtype=jnp.int32)
    dest = jnp.sum(jnp.where(hot, pad_start, 0), axis=-1) + rank
    n_blocks = -(-(2 * n_tok + MOE_EXPERTS * (rows - 1)) // rows)
    block_start = jnp.arange(n_blocks, dtype=jnp.int32) * rows
    block_eid = jnp.minimum(jnp.sum(block_start[:, None] >= pad_end[None, :], axis=-1), MOE_EXPERTS - 1)
    filled = (pad_start + counts)[block_eid]
    block_valid = jnp.clip(filled - block_start, 0, rows)
    used = block_valid > 0
    first = used & jnp.concatenate([jnp.ones((1,), bool), block_eid[1:] != block_eid[:-1]])
    slot = (jnp.cumsum(first) - 1) % 2
    order = jnp.arange(n_blocks, dtype=jnp.int32)
    later_first = lax.cummin(jnp.where(first, order, n_blocks)[::-1])[::-1]
    following = jnp.concatenate([later_first[1:], jnp.full((1,), n_blocks, jnp.int32)])
    block_next = jnp.where(following < n_blocks, block_eid[jnp.minimum(following, n_blocks - 1)], -1)
    blocks = tuple(a.astype(jnp.int32) for a in (block_eid, block_valid, first, slot, block_next))
    return dest[:, 0], dest[:, 1], blocks, n_blocks * rows


def _router_weights(w_group, b_group, w_expert, b_expert):
    w_r = _pad_cols(jnp.concatenate([w_group, w_expert], axis=1))
    w_hi = w_r.astype(BF16)
    w_lo = (w_r - w_hi.astype(F32)).astype(BF16)
    return w_hi, w_lo, _pad_lanes(jnp.concatenate([b_group, b_expert]))


def _moe_layer(h, xn, route, counts, w_gate, w_up, w_down, layer, final_g):
    n_tok, d = h.shape
    parts = counts.shape[0]
    per_part = n_tok // parts
    g = jnp.ones((d,), F32) if final_g is None else final_g
    plans = [_dispatch(route[p * per_part:(p + 1) * per_part], counts[p], per_part) for p in range(parts)]
    xs = [sc_scatter_rows(xn, p * per_part, dest0, dest1, n_slots)
          for p, (dest0, dest1, _, n_slots) in enumerate(plans)]
    y01 = []
    for (dest0, dest1, blocks, _), x in zip(plans, xs):
        ys = moe_experts(blocks, x, w_gate, w_up, w_down, layer)
        y01.append(sc_gather_rows(ys, jnp.concatenate([dest0, dest1])))
    for p, y in enumerate(y01):
        h = moe_combine(h, p * per_part, y, route, g, final_g is not None)
    return h


def _memory_kv(memn_in, mem_norm, wk, wv):
    bsz, m, d = memn_in.shape
    w = jnp.concatenate([wk, wv], axis=1).astype(BF16)
    kv, _ = rms_matmul(memn_in.reshape(bsz * m, d), mem_norm, w, jnp.zeros((d, LANES), BF16))
    k = kv[:, :d].reshape(bsz, m, d)
    v = kv[:, d:].reshape(bsz, m, d)
    return jnp.swapaxes(k, 1, 2).astype(BF16), v.astype(BF16)


def kernel(x, mem, mem_norm, final_norm, norm_mix, norm_xa, norm_ffn, xa_wq, xa_wk, xa_wv, xa_wo, moe_w_group, moe_b_group, moe_w_expert, moe_b_expert, moe_w_gate, moe_w_up, moe_w_down, ev_w_in, ev_sc_conv, ev_ssm_conv_w, ev_ssm_conv_b, ev_ssm_dt_bias, ev_ssm_a_log, ev_ssm_d, ev_ssm_norm, ev_w_out, od_w_in, od_gdn_conv, od_gdn_dt_bias, od_gdn_a_log, od_gdn_norm, od_w_out):
    bsz, length, d = x.shape
    n_tok = bsz * length
    depth = norm_mix.shape[0]
    h = x.reshape(n_tok, d)
    for layer in range(depth):
        i = layer // 2
        if layer % 2 == 0:
            w = ev_w_in[i]
            z0 = 3 * SC_DIM
            xbc0 = z0 + SSM_INNER
            w_conv = w[:, xbc0:xbc0 + SSM_XBC].astype(BF16)
            w_main = jnp.concatenate([w[:, z0:xbc0], w[:, :z0]], axis=1).astype(BF16)
            w_small = _pad_cols(w[:, xbc0 + SSM_XBC:]).astype(BF16)
            xbc = rms_matmul_conv(h, norm_mix[layer], w_conv, ev_ssm_conv_w[i], ev_ssm_conv_b[i], length)
            y, small = rms_matmul(h, norm_mix[layer], w_main, w_small, tm=1024, tn=w_main.shape[1])
            y3 = y.reshape(bsz, length, -1)
            small3 = small.reshape(bsz, length, LANES)
            smallt = jnp.swapaxes(small3[:, :, :16], 1, 2)
            ya = short_conv_mixer(y3, ev_sc_conv[i], SSM_INNER // SC_DIM)
            yb = ssd_mixer(xbc.reshape(bsz, length, -1), y3, small3, smallt, ev_ssm_dt_bias[i],
                           ev_ssm_a_log[i], ev_ssm_d[i], ev_ssm_norm[i])
            w_out = ev_w_out[i].astype(BF16)
            split = SC_DIM
        else:
            w = od_w_in[i]
            qkv_w = 3 * GDN_HEADS * GDN_D
            z_end = qkv_w + GDN_HEADS * GDN_D
            w_conv = w[:, :qkv_w].astype(BF16)
            w_main = jnp.concatenate([w[:, qkv_w:z_end], w[:, z_end + 2 * GDN_HEADS:]], axis=1).astype(BF16)
            w_small = _pad_cols(w[:, z_end:z_end + 2 * GDN_HEADS]).astype(BF16)
            qkv = rms_matmul_conv(h, norm_mix[layer], w_conv, od_gdn_conv[i], jnp.zeros((qkv_w,), F32), length)
            y, small = rms_matmul(h, norm_mix[layer], w_main, w_small, tm=1024, tn=w_main.shape[1])
            y3 = y.reshape(bsz, length, -1)
            small3 = small.reshape(bsz, length, LANES)
            smallt = jnp.swapaxes(small3[:, :, :16], 1, 2)
            ya = gated_deltanet_mixer(qkv.reshape(bsz, length, -1), y3, small3, smallt, od_gdn_dt_bias[i],
                                      od_gdn_a_log[i], od_gdn_norm[i])
            yb = stick_breaking_mixer(y3, GDN_HEADS * GDN_D)
            w_out = od_w_out[i].astype(BF16)
            split = GDN_HEADS * GDN_D
        kt, v = _memory_kv(mem, mem_norm, xa_wk[layer], xa_wv[layer])
        w_hi, w_lo, bias = _router_weights(moe_w_group[layer], moe_b_group[layer], moe_w_expert[layer],
                                           moe_b_expert[layer])
        h, xn, route, counts = post_mixer(
            ya.reshape(n_tok, -1), yb.reshape(n_tok, -1), h, w_out[:split], w_out[split:], norm_xa[layer],
            xa_wq[layer].astype(BF16), kt, v, xa_wo[layer].astype(BF16), norm_ffn[layer], w_hi, w_lo, bias)
        h = _moe_layer(h, xn, route, counts, moe_w_gate, moe_w_up, moe_w_down, layer,
                       final_norm if layer == depth - 1 else None)
    return h.reshape(bsz, length, d)
```

```python
import functools

import jax
import jax.numpy as jnp
import numpy as np
from jax import lax
from jax.experimental import pallas as pl
from jax.experimental.pallas import tpu as pltpu
from jax.experimental.pallas import tpu_sc as plsc

F32 = jnp.float32
BF16 = jnp.bfloat16
EPS = 1e-6

D_MODEL = 1024
MEM_LEN = 256
SC_DIM = 512
SSM_HEADS = 16
SSM_HEAD_DIM = 64
SSM_INNER = 1024
SSM_GROUPS = 2
SSM_STATE = 128
SSM_XBC = SSM_INNER + 2 * SSM_GROUPS * SSM_STATE
SSD_CHUNK = 128
GDN_HEADS = 8
GDN_D = 128
GDN_CHUNK = 64
GDN_TILE = 128
SB_HEADS = 8
SB_HEAD_DIM = 64
SB_DIM = 512
SB_BLOCK = 128
SB_STEP_HEADS = 8
XA_HEADS = 4
XA_HEAD_DIM = 256
MOE_GROUPS = 4
MOE_PER_GROUP = 8
MOE_EXPERTS = 32
MOE_FF = 512
MOE_ROWS = 512
POST_GROUPS = 1
COMBINE_PARTS = 4
SC_CORES = 2
SC_SUBCORES = 16
SC_WORKERS = SC_CORES * SC_SUBCORES
SC_CHUNK = 32
HALO = 8
CONV_CHUNK = 512
LANES = 128
SB_LOG_ZERO = -104.0
VMEM_LIMIT = 56 * 1024 * 1024


def _cparams(*sem):
    return pltpu.CompilerParams(dimension_semantics=sem, vmem_limit_bytes=VMEM_LIMIT)


def _mm(a, b):
    return jnp.dot(a.astype(BF16), b.astype(BF16), preferred_element_type=F32)


def _mm_nt(a, b):
    return lax.dot_general(a.astype(BF16), b.astype(BF16), (((1,), (1,)), ((), ())),
                           preferred_element_type=F32)


def _split_bf16(x, n):
    parts, r = [], x
    for _ in range(n):
        p = r.astype(BF16)
        parts.append(p)
        r = r - p.astype(F32)
    return parts


def _mm_sel_rhs(x, sel, n=3):
    return sum(jnp.dot(p, sel, preferred_element_type=F32) for p in _split_bf16(x, n))


def _mm_sel_lhs(sel, x, n=3):
    return sum(jnp.dot(sel, p, preferred_element_type=F32) for p in _split_bf16(x, n))


def _silu(x):
    return x * jax.nn.sigmoid(x)


def _softplus(x):
    return jnp.maximum(x, 0.0) + jnp.log(1.0 + jnp.exp(-jnp.abs(x)))


def _rms(x, g):
    return x * lax.rsqrt(jnp.mean(x * x, axis=-1, keepdims=True) + EPS) * g


def _rms_matmul_kernel(x_ref, g_ref, w_ref, ws_ref, o_ref, os_ref):
    xn = _rms(x_ref[...], g_ref[...]).astype(BF16)
    o_ref[...] = jnp.dot(xn, w_ref[...], preferred_element_type=F32)
    os_ref[...] = jnp.dot(xn, ws_ref[...], preferred_element_type=F32)


def rms_matmul(x, g, w, ws, tm=512, tn=512):
    m, k = x.shape
    n = w.shape[1]
    tm = min(tm, m)
    main, small = pl.pallas_call(
        _rms_matmul_kernel,
        grid=(n // tn, m // tm),
        in_specs=[
            pl.BlockSpec((tm, k), lambda j, i: (i, 0)),
            pl.BlockSpec((1, k), lambda j, i: (0, 0)),
            pl.BlockSpec((k, tn), lambda j, i: (0, j)),
            pl.BlockSpec((k, LANES), lambda j, i: (0, 0)),
        ],
        out_specs=[
            pl.BlockSpec((tm, tn), lambda j, i: (i, j)),
            pl.BlockSpec((None, tm, LANES), lambda j, i: (j, i, 0)),
        ],
        out_shape=[jax.ShapeDtypeStruct((m, n), F32), jax.ShapeDtypeStruct((n // tn, m, LANES), F32)],
        compiler_params=_cparams("parallel", "parallel"),
        name="rms_matmul",
    )(x, g.reshape(1, k), w, ws)
    return main, small[0]


def _causal_conv(ext_ref, w_ref, rows):
    width = w_ref.shape[0]
    ext = ext_ref[...]
    acc = None
    for j in range(width):
        shift = width - 1 - j
        moved = ext if shift == 0 else pltpu.roll(ext, shift, axis=0)
        term = w_ref[j:j + 1, :] * moved[HALO:HALO + rows, :]
        acc = term if acc is None else acc + term
    return acc


def _rms_matmul_conv_kernel(x_ref, g_ref, w_ref, cw_ref, cb_ref, o_ref, *ext_refs, tiles_per_seq):
    tm = x_ref.shape[0]
    starts_sequence = pl.program_id(1) % tiles_per_seq == 0

    @pl.when(starts_sequence)
    def _():
        for ext_ref in ext_refs:
            ext_ref[0:HALO, :] = jnp.zeros((HALO, CONV_CHUNK), F32)

    @pl.when(jnp.logical_not(starts_sequence))
    def _():
        for ext_ref in ext_refs:
            ext_ref[0:HALO, :] = ext_ref[tm:tm + HALO, :]

    xn = _rms(x_ref[...], g_ref[...]).astype(BF16)
    for c, ext_ref in enumerate(ext_refs):
        cols = slice(c * CONV_CHUNK, (c + 1) * CONV_CHUNK)
        ext_ref[HALO:, :] = jnp.dot(xn, w_ref[:, cols], preferred_element_type=F32)
        o_ref[:, cols] = _silu(_causal_conv(ext_ref, cw_ref.at[:, cols], tm) + cb_ref[:, cols])


def rms_matmul_conv(x, g, w, conv_w, conv_b, seq_len, tm=1024, tn=1536):
    m, k = x.shape
    n = w.shape[1]
    cols = lambda rows: pl.BlockSpec((rows, tn), lambda j, i: (0, j))
    return pl.pallas_call(
        functools.partial(_rms_matmul_conv_kernel, tiles_per_seq=seq_len // tm),
        grid=(n // tn, m // tm),
        in_specs=[
            pl.BlockSpec((tm, k), lambda j, i: (i, 0)),
            pl.BlockSpec((1, k), lambda j, i: (0, 0)),
            cols(k), cols(conv_w.shape[0]), cols(1),
        ],
        out_specs=pl.BlockSpec((tm, tn), lambda j, i: (i, j)),
        out_shape=jax.ShapeDtypeStruct((m, n), F32),
        scratch_shapes=[pltpu.VMEM((tm + HALO, CONV_CHUNK), F32)] * (tn // CONV_CHUNK),
        compiler_params=_cparams("arbitrary", "arbitrary"),
        name="rms_matmul_conv",
    )(x, g.reshape(1, k), w, conv_w, conv_b.reshape(1, n))


def _halo_index(rows):
    step = rows // HALO
    return lambda i: jnp.maximum(i * step - 1, 0)


def _sc_kernel(b_ref, c_ref, x_ref, ch_ref, xh_ref, w_ref, o_ref, ext_ref):
    rows = o_ref.shape[0]
    first = pl.program_id(1) == 0
    ext_ref[0:HALO, :] = jnp.where(first, 0.0, ch_ref[...] * xh_ref[...])
    ext_ref[HALO:, :] = c_ref[...] * x_ref[...]
    o_ref[...] = b_ref[...] * _causal_conv(ext_ref, w_ref, rows)


def short_conv_mixer(y3, w, col0, tl=512):
    bsz, length, _ = y3.shape
    tl = min(tl, length)
    hidx = _halo_index(tl)
    return pl.pallas_call(
        _sc_kernel,
        grid=(bsz, length // tl),
        in_specs=[
            pl.BlockSpec((None, tl, SC_DIM), lambda b, i: (b, i, col0)),
            pl.BlockSpec((None, tl, SC_DIM), lambda b, i: (b, i, col0 + 1)),
            pl.BlockSpec((None, tl, SC_DIM), lambda b, i: (b, i, col0 + 2)),
            pl.BlockSpec((None, HALO, SC_DIM), lambda b, i: (b, hidx(i), col0 + 1)),
            pl.BlockSpec((None, HALO, SC_DIM), lambda b, i: (b, hidx(i), col0 + 2)),
            pl.BlockSpec(w.shape, lambda b, i: (0, 0)),
        ],
        out_specs=pl.BlockSpec((None, tl, SC_DIM), lambda b, i: (b, i, 0)),
        out_shape=jax.ShapeDtypeStruct((bsz, length, SC_DIM), F32),
        scratch_shapes=[pltpu.VMEM((tl + HALO, SC_DIM), F32)],
        compiler_params=_cparams("parallel", "arbitrary"),
        name="short_conv_mixer",
    )(y3, y3, y3, y3, y3, w)


def _ssd_kernel(xbc_ref, z_ref, dt_ref, dtt_ref, dtb_r_ref, dtb_c_ref,
                alog_r_ref, alog_c_ref, d_ref, nw_ref, tri_ref, trit_ref, eh_ref, eq_ref,
                o_ref, s_ref):
    q = SSD_CHUNK
    hpg = SSM_HEADS // SSM_GROUPS
    gw = hpg * SSM_HEAD_DIM

    @pl.when(pl.program_id(1) == 0)
    def _():
        s_ref[...] = jnp.zeros_like(s_ref)

    xbc = xbc_ref[...]
    xs = xbc[:, :SSM_INNER]
    bm = xbc[:, SSM_INNER:SSM_INNER + SSM_GROUPS * SSM_STATE]
    cm = xbc[:, SSM_INNER + SSM_GROUPS * SSM_STATE:]

    dt = _softplus(dt_ref[...] + dtb_r_ref[...])
    acs = _mm_sel_lhs(tri_ref[...], dt * -jnp.exp(alog_r_ref[...]))
    dtt = _softplus(dtt_ref[...] + dtb_c_ref[...])
    acst = _mm_sel_rhs(dtt * -jnp.exp(alog_c_ref[...]), trit_ref[...])
    dt_full = _mm_sel_rhs(dt, eh_ref[...])
    acs_full = _mm_sel_rhs(acs, eh_ref[...])
    acs_col = _mm_sel_rhs(acs, eq_ref[...])

    xdt = xs * dt_full
    acs_last = acs_full[q - 1:q, :]
    xw = xdt * jnp.exp(acs_last - acs_full)
    chunk_decay = jnp.exp(acs_last)

    row = lax.broadcasted_iota(jnp.int32, (q, q), 0)
    col = lax.broadcasted_iota(jnp.int32, (q, q), 1)
    causal = row >= col
    lane = lax.broadcasted_iota(jnp.int32, (q, 2 * SSM_HEAD_DIM), 1)

    y_diag, y_off = [], []
    for g in range(SSM_GROUPS):
        bm_g = bm[:, g * SSM_STATE:(g + 1) * SSM_STATE]
        cm_g = cm[:, g * SSM_STATE:(g + 1) * SSM_STATE]
        cb_g = _mm_nt(cm_g, bm_g)
        state = s_ref[g]
        y_off.append(_mm(cm_g, state))
        s_ref[g] = state * chunk_decay[:, g * gw:(g + 1) * gw] + _mm(bm_g.T, xw[:, g * gw:(g + 1) * gw])
        for pair in range(hpg // 2):
            h0 = g * hpg + 2 * pair
            xdt_pair = xdt[:, h0 * SSM_HEAD_DIM:(h0 + 2) * SSM_HEAD_DIM]
            outs = []
            for h in (h0, h0 + 1):
                seg = acs_col[:, h * q:(h + 1) * q] - acst[h:h + 1, :]
                decay = jnp.where(causal, jnp.exp(seg), 0.0)
                outs.append(_mm(cb_g * decay, xdt_pair))
            y_diag.append(jnp.where(lane < SSM_HEAD_DIM, outs[0], outs[1]))
    y = (jnp.concatenate(y_diag, axis=1) + jnp.concatenate(y_off, axis=1) * jnp.exp(acs_full)
         + xs * d_ref[...])
    y = y * _silu(z_ref[...])
    halves = []
    for g in range(SSM_GROUPS):
        yg = y[:, g * gw:(g + 1) * gw]
        halves.append(yg * lax.rsqrt(jnp.mean(yg * yg, axis=-1, keepdims=True) + EPS))
    o_ref[...] = jnp.concatenate(halves, axis=1) * nw_ref[...]


def _pad_lanes(v, fill=0.0):
    return jnp.pad(v.astype(F32), (0, LANES - v.shape[0]), constant_values=fill).reshape(1, LANES)


def _pad_col(v, rows=16):
    return jnp.pad(v.astype(F32), (0, rows - v.shape[0])).reshape(rows, 1)


def ssd_mixer(xbc3, y3, small3, smallt, dt_bias, a_log, d_skip, norm_w):
    bsz, length, _ = y3.shape
    q = SSD_CHUNK
    tri = jnp.asarray(np.tril(np.ones((q, q), np.float32)), BF16)
    trit = jnp.asarray(np.triu(np.ones((q, q), np.float32)), BF16)
    heads = np.arange(LANES)[:, None]
    eh = jnp.asarray(heads == (np.arange(SSM_INNER)[None, :] // SSM_HEAD_DIM), BF16)
    eq = jnp.asarray(heads == (np.arange(SSM_HEADS * q)[None, :] // q), BF16)
    d_full = jnp.repeat(d_skip.astype(F32), SSM_HEAD_DIM).reshape(1, SSM_INNER)
    const = lambda a: pl.BlockSpec(a.shape, lambda b, c: (0,) * a.ndim)
    args = [_pad_lanes(dt_bias), _pad_col(dt_bias), _pad_lanes(a_log),
            _pad_col(a_log), d_full, norm_w.reshape(1, -1), tri, trit, eh, eq]
    return pl.pallas_call(
        _ssd_kernel,
        grid=(bsz, length // q),
        in_specs=[
            pl.BlockSpec((None, q, SSM_XBC), lambda b, c: (b, c, 0)),
            pl.BlockSpec((None, q, SSM_INNER), lambda b, c: (b, c, 0)),
            pl.BlockSpec((None, q, LANES), lambda b, c: (b, c, 0)),
            pl.BlockSpec((None, 16, q), lambda b, c: (b, 0, c)),
        ] + [const(a) for a in args],
        out_specs=pl.BlockSpec((None, q, SSM_INNER), lambda b, c: (b, c, 0)),
        out_shape=jax.ShapeDtypeStruct((bsz, length, SSM_INNER), F32),
        scratch_shapes=[pltpu.VMEM((SSM_GROUPS, SSM_STATE, SSM_INNER // SSM_GROUPS), F32)],
        compiler_params=_cparams("parallel", "arbitrary"),
        name="ssd_mixer",
    )(xbc3, y3, small3, smallt, *args)


def _unit_lower_inverse(mats, row, col):
    eye = jnp.where(row == col, 1.0, 0.0)
    blk = lambda n: (row >> (n.bit_length() - 1)) == (col >> (n.bit_length() - 1))
    p = [jnp.where(blk(16), -a, 0.0) for a in mats]
    t = [eye + x for x in p]
    for _ in range(3):
        p = [_mm(x, x) for x in p]
        t = [y + _mm(y, x) for y, x in zip(t, p)]
    for n in (16, 32):
        band = blk(2 * n) & jnp.logical_not(blk(n))
        left = [_mm(y, jnp.where(band, a, 0.0)) for y, a in zip(t, mats)]
        t = [y - _mm(x, y) for y, x in zip(t, left)]
    return t


def _gdn_kernel(qkv_ref, z_ref, ab_ref, abt_ref, dtb_r_ref, dtb_c_ref, alog_r_ref,
                alog_c_ref, nw_ref, tri_ref, trit_ref, eg_ref, eb_ref, o_ref, s_ref):
    n = GDN_TILE
    c = GDN_CHUNK
    d = GDN_D
    hd = GDN_HEADS * d

    @pl.when(pl.program_id(1) == 0)
    def _():
        s_ref[...] = jnp.zeros_like(s_ref)

    qkv = qkv_ref[...]
    z = z_ref[...]

    ab = ab_ref[...]
    g = -jnp.exp(alog_r_ref[...]) * _softplus(ab + dtb_r_ref[...])
    gc_full = _mm_sel_rhs(_mm_sel_lhs(tri_ref[...], g), eg_ref[...])
    beta_full = _mm_sel_rhs(jax.nn.sigmoid(ab), eb_ref[...])
    gt = -jnp.exp(alog_c_ref[...]) * _softplus(abt_ref[...] + dtb_c_ref[...])
    gct = _mm_sel_rhs(gt, trit_ref[...])

    row = lax.broadcasted_iota(jnp.int32, (n, n), 0)
    col = lax.broadcasted_iota(jnp.int32, (n, n), 1)
    same = (row >> (c.bit_length() - 1)) == (col >> (c.bit_length() - 1))
    incl = same & (row >= col)
    strict = same & (row > col)
    zeros_half = jnp.zeros((c, d), F32)

    heads = range(GDN_HEADS)
    sl = [slice(h * d, (h + 1) * d) for h in heads]
    l2n = lambda x: x * lax.rsqrt(jnp.sum(x * x, axis=-1, keepdims=True) + EPS)
    qn = [l2n(qkv[:, sl[h]]) * (d ** -0.5) for h in heads]
    kn = [l2n(qkv[:, hd + h * d:hd + (h + 1) * d]) for h in heads]
    vh = [qkv[:, 2 * hd + h * d:2 * hd + (h + 1) * d] for h in heads]
    gcol = [gc_full[:, sl[h]] for h in heads]
    beta = [beta_full[:, sl[h]] for h in heads]
    edec = [jnp.exp(gcol[h] - gct[h:h + 1, :]) for h in heads]
    egc = [jnp.exp(x) for x in gcol]
    kb = [kn[h] * beta[h] for h in heads]
    lower = [jnp.where(strict, _mm_nt(kb[h], kn[h]) * edec[h], 0.0) for h in heads]
    aqk = [jnp.where(incl, _mm_nt(qn[h], kn[h]) * edec[h], 0.0) for h in heads]
    tinv = _unit_lower_inverse(lower, row, col)
    sol = [_mm(tinv[h], jnp.concatenate([vh[h] * beta[h], kb[h] * egc[h]], axis=1)) for h in heads]
    qd = [qn[h] * egc[h] for h in heads]
    glast = [(gcol[h][c - 1:c, :], gcol[h][n - 1:n, :]) for h in heads]
    kdt = [(kn[h] * jnp.exp(jnp.concatenate([jnp.broadcast_to(glast[h][0], (c, d)),
                                             jnp.broadcast_to(glast[h][1], (c, d))], axis=0) - gcol[h])).T
           for h in heads]
    s0 = [s_ref[h] for h in heads]
    v0 = [sol[h][:c, :d] - _mm(sol[h][:c, d:], s0[h]) for h in heads]
    s1 = [s0[h] * jnp.exp(glast[h][0]) + _mm(kdt[h], jnp.concatenate([v0[h], zeros_half], axis=0)) for h in heads]
    v1 = [sol[h][c:, :d] - _mm(sol[h][c:, d:], s1[h]) for h in heads]
    for h in heads:
        s_ref[h] = s1[h] * jnp.exp(glast[h][1]) + _mm(kdt[h], jnp.concatenate([zeros_half, v1[h]], axis=0))
    outs = []
    for h in heads:
        o = (jnp.concatenate([_mm(qd[h][:c], s0[h]), _mm(qd[h][c:], s1[h])], axis=0)
             + _mm(aqk[h], jnp.concatenate([v0[h], v1[h]], axis=0)))
        o = o * lax.rsqrt(jnp.mean(o * o, axis=-1, keepdims=True) + EPS) * nw_ref[...]
        outs.append(o * _silu(z[:, sl[h]]))
    o_ref[...] = jnp.concatenate(outs, axis=1)


def gated_deltanet_mixer(qkv3, y3, small3, smallt, dt_bias, a_log, norm_w):
    bsz, length, _ = y3.shape
    n = GDN_TILE
    hd = GDN_HEADS * GDN_D
    idx = np.arange(n)
    same = (idx[:, None] // GDN_CHUNK) == (idx[None, :] // GDN_CHUNK)
    tri = jnp.asarray(same & (idx[:, None] >= idx[None, :]), BF16)
    trit = jnp.asarray(same & (idx[:, None] <= idx[None, :]), BF16)
    lanes = np.arange(LANES)[:, None]
    heads = np.arange(hd)[None, :] // GDN_D
    eg = jnp.asarray(lanes == heads, BF16)
    eb = jnp.asarray(lanes == heads + GDN_HEADS, BF16)
    const = lambda a: pl.BlockSpec(a.shape, lambda b, c: (0,) * a.ndim)
    args = [_pad_lanes(dt_bias), _pad_col(dt_bias), _pad_lanes(a_log), _pad_col(a_log),
            norm_w.reshape(1, -1), tri, trit, eg, eb]
    return pl.pallas_call(
        _gdn_kernel,
        grid=(bsz, length // n),
        in_specs=[
            pl.BlockSpec((None, n, 3 * hd), lambda b, c: (b, c, 0)),
            pl.BlockSpec((None, n, hd), lambda b, c: (b, c, 0)),
            pl.BlockSpec((None, n, LANES), lambda b, c: (b, c, 0)),
            pl.BlockSpec((None, 16, n), lambda b, c: (b, 0, c)),
        ] + [const(a) for a in args],
        out_specs=pl.BlockSpec((None, n, hd), lambda b, c: (b, c, 0)),
        out_shape=jax.ShapeDtypeStruct((bsz, length, hd), F32),
        scratch_shapes=[pltpu.VMEM((GDN_HEADS, GDN_D, GDN_D), F32)],
        compiler_params=_cparams("parallel", "arbitrary"),
        name="gated_deltanet",
    )(qkv3, y3, small3, smallt, *args)


def _sb_kernel(q_ref, k_ref, v_ref, upper_ref, o_ref):
    blk = SB_BLOCK
    pair_w = 2 * SB_HEAD_DIM
    n_pairs = SB_STEP_HEADS // 2
    i = pl.program_id(2)
    q = q_ref[...] * (SB_HEAD_DIM ** -0.5)
    lane = lax.broadcasted_iota(jnp.int32, (blk, pair_w), 1)
    first_head = lane < SB_HEAD_DIM
    qs = []
    for p in range(n_pairs):
        q2 = q[:, p * pair_w:(p + 1) * pair_w]
        qs += [jnp.where(first_head, q2, 0.0).astype(BF16), jnp.where(first_head, 0.0, q2).astype(BF16)]
    row = lax.broadcasted_iota(jnp.int32, (blk, blk), 0)
    col = lax.broadcasted_iota(jnp.int32, (blk, blk), 1)
    earlier = col < row
    upper = upper_ref[...]
    heads = range(SB_STEP_HEADS)

    def key_block(kb, accs, sticks, diagonal):
        start = pl.multiple_of(kb * blk, blk)
        k = k_ref[pl.ds(start, blk), :].astype(BF16)
        v = v_ref[pl.ds(start, blk), :].astype(BF16)
        kp = [k[:, p * pair_w:(p + 1) * pair_w] for p in range(n_pairs)]
        vp = [v[:, p * pair_w:(p + 1) * pair_w] for p in range(n_pairs)]
        logits = [lax.dot_general(qs[h], kp[h // 2], (((1,), (1,)), ((), ())), preferred_element_type=F32)
                  for h in heads]
        log_keep = [-_softplus(x) for x in logits]
        if diagonal:
            log_keep = [jnp.where(earlier, x, 0.0) for x in log_keep]
        between = [_mm_sel_rhs(log_keep[h], upper, 2) + sticks[h] for h in heads]
        w = [jnp.exp(logits[h] + log_keep[h] + between[h]) for h in heads]
        if diagonal:
            w = [jnp.where(earlier, x, 0.0) for x in w]
        pv = [jnp.dot(w[h].astype(BF16), vp[h // 2], preferred_element_type=F32) for h in heads]
        accs = tuple(accs[p] + jnp.where(first_head, pv[2 * p], pv[2 * p + 1]) for p in range(n_pairs))
        sticks = tuple(sticks[h] + jnp.sum(log_keep[h], axis=-1, keepdims=True) for h in heads)
        return accs, sticks

    accs = tuple(jnp.zeros((blk, pair_w), F32) for _ in range(n_pairs))
    sticks = tuple(jnp.zeros((blk, 1), F32) for _ in heads)
    accs, sticks = key_block(i, accs, sticks, True)

    def alive(state):
        kb, _, sticks = state
        longest = sticks[0]
        for s in sticks[1:]:
            longest = jnp.maximum(longest, s)
        return (kb >= 0) & (jnp.max(longest) > SB_LOG_ZERO)

    def body(state):
        kb, accs, sticks = state
        accs, sticks = key_block(kb, accs, sticks, False)
        return kb - 1, accs, sticks

    _, accs, _ = lax.while_loop(alive, body, (i - 1, accs, sticks))
    o_ref[...] = jnp.concatenate(accs, axis=1)


def stick_breaking_mixer(y3, col0):
    bsz, length, _ = y3.shape
    blk = SB_BLOCK
    step_w = SB_STEP_HEADS * SB_HEAD_DIM
    steps = SB_DIM // step_w
    q0 = col0 // step_w
    idx = np.arange(blk)
    upper = jnp.asarray(idx[:, None] > idx[None, :], BF16)
    resident = lambda off: pl.BlockSpec((None, length, step_w), lambda b, p, i: (b, 0, q0 + off + p),
                                        pipeline_mode=pl.Buffered(1))
    return pl.pallas_call(
        _sb_kernel,
        grid=(bsz, steps, length // blk),
        in_specs=[
            pl.BlockSpec((None, blk, step_w), lambda b, p, i: (b, i, q0 + p)),
            resident(steps),
            resident(2 * steps),
            pl.BlockSpec((blk, blk), lambda b, p, i: (0, 0)),
        ],
        out_specs=pl.BlockSpec((None, blk, step_w), lambda b, p, i: (b, i, p)),
        out_shape=jax.ShapeDtypeStruct((bsz, length, SB_DIM), F32),
        compiler_params=_cparams("parallel", "parallel", "arbitrary"),
        name="stick_breaking",
    )(y3, y3, y3, upper)


def _mixer_out(a_ref, b_ref, h_ref, wa_ref, wb_ref, rows):
    return h_ref[rows, :] + (jnp.dot(a_ref[rows, :].astype(BF16), wa_ref[...], preferred_element_type=F32)
                             + jnp.dot(b_ref[rows, :].astype(BF16), wb_ref[...], preferred_element_type=F32))


def _cross_attention(hs, g_ref, wq_ref, kt_ref, v_ref, wo_ref):
    us = [_rms(h, g_ref[...]).astype(BF16) for h in hs]
    qs = [jnp.dot(u, wq_ref[...], preferred_element_type=F32) for u in us]
    heads = [[] for _ in hs]
    for hd in range(XA_HEADS):
        sl = slice(hd * XA_HEAD_DIM, (hd + 1) * XA_HEAD_DIM)
        ss = [jnp.dot(q[:, sl].astype(BF16), kt_ref[sl, :], preferred_element_type=F32) * (XA_HEAD_DIM ** -0.5)
              for q in qs]
        ps = [jnp.exp(s - jnp.max(s, axis=-1, keepdims=True)) for s in ss]
        ps = [p / jnp.sum(p, axis=-1, keepdims=True) for p in ps]
        for k, p in enumerate(ps):
            heads[k].append(jnp.dot(p.astype(BF16), v_ref[:, sl], preferred_element_type=F32))
    os_ = [jnp.concatenate(hk, axis=1).astype(BF16) for hk in heads]
    return [h + jnp.dot(o, wo_ref[...], preferred_element_type=F32) for h, o in zip(hs, os_)]


def _route(xn, whi_ref, wlo_ref, b_ref, before_ref, run_ref):
    x_hi = xn.astype(BF16)
    x_lo = (xn - x_hi.astype(F32)).astype(BF16)
    logits = (jnp.dot(x_hi, whi_ref[...], preferred_element_type=F32)
              + jnp.dot(x_lo, whi_ref[...], preferred_element_type=F32)
              + jnp.dot(x_hi, wlo_ref[...], preferred_element_type=F32) + b_ref[...])
    lane = lax.broadcasted_iota(jnp.int32, logits.shape, 1).astype(F32)
    neg = -1e30
    none = float(LANES)

    def top(vals):
        best = jnp.max(vals, axis=-1, keepdims=True)
        where = jnp.min(jnp.where(vals == best, lane, none), axis=-1, keepdims=True)
        return best, where

    gl = jnp.where(lane < MOE_GROUPS, logits, neg)
    gbest, gsel = top(gl)
    gprob = 1.0 / jnp.sum(jnp.exp(gl - gbest), axis=-1, keepdims=True)
    lo = MOE_GROUPS + gsel * MOE_PER_GROUP
    el = jnp.where((lane >= lo) & (lane < lo + MOE_PER_GROUP), logits, neg)
    m1, i1 = top(el)
    m2, i2 = top(jnp.where(lane == i1, neg, el))
    e = jnp.exp(m2 - m1)
    gate1 = gprob / (1.0 + e)
    gate2 = gprob * e / (1.0 + e)

    hot1 = lane == i1
    hot2 = lane == i2
    one1 = jnp.where(hot1, 1.0, 0.0)
    one2 = jnp.where(hot2, 1.0, 0.0)
    before = before_ref[...]
    prefix1 = jnp.dot(before, one1.astype(BF16), preferred_element_type=F32)
    prefix2 = jnp.dot(before, one2.astype(BF16), preferred_element_type=F32)
    total1 = jnp.sum(one1, axis=0, keepdims=True)
    running = run_ref[...]
    rank1 = jnp.sum(jnp.where(hot1, prefix1 + running, 0.0), axis=-1, keepdims=True)
    rank2 = jnp.sum(jnp.where(hot2, prefix2 + (running + total1), 0.0), axis=-1, keepdims=True)
    running = running + total1 + jnp.sum(one2, axis=0, keepdims=True)
    run_ref[...] = running

    fields = (i1 - MOE_GROUPS, i2 - MOE_GROUPS, gate1, gate2, rank1, rank2)
    out = jnp.zeros_like(logits)
    for k, val in enumerate(fields):
        out = jnp.where(lane == k, val, out)
    return out


def _post_mixer_kernel(a_ref, b_ref, h_ref, wa_ref, wb_ref, gxa_ref, wq_ref, kt_ref, v_ref, wo_ref,
                       gffn_ref, whi_ref, wlo_ref, bias_ref, before_ref,
                       h_out_ref, xn_ref, r_ref, cnt_ref, run_ref):
    @pl.when(pl.program_id(0) == 0)
    def _():
        run_ref[...] = jnp.zeros_like(run_ref)

    tm = h_ref.shape[0]
    groups = [slice(k * tm // POST_GROUPS, (k + 1) * tm // POST_GROUPS) for k in range(POST_GROUPS)]
    hs = [_mixer_out(a_ref, b_ref, h_ref, wa_ref, wb_ref, rows) for rows in groups]
    hs = _cross_attention(hs, gxa_ref, wq_ref, kt_ref, v_ref, wo_ref)
    for rows, h in zip(groups, hs):
        h_out_ref[rows, :] = h
        xn_ref[rows, :] = _rms(h, gffn_ref[...])
    r_ref[...] = _route(xn_ref[...], whi_ref, wlo_ref, bias_ref, before_ref, run_ref)
    cnt_ref[...] = run_ref[...]


def post_mixer(ya, yb, h, wa, wb, g_xa, wq, kt, v, wo, g_ffn, w_hi, w_lo, bias, tm=512):
    m, d = h.shape
    tiles_per_batch = m // kt.shape[0] // tm
    idx = np.arange(tm)
    before = jnp.asarray(idx[:, None] > idx[None, :], BF16)
    rows = lambda w: pl.BlockSpec((tm, w), lambda i: (i, 0))
    const = lambda a: pl.BlockSpec(a.shape, lambda i: (0,) * a.ndim, pipeline_mode=pl.Buffered(1))
    per_batch = lambda a: pl.BlockSpec((None,) + a.shape[1:], lambda i: (i // tiles_per_batch, 0, 0))
    g_xa, g_ffn = g_xa.reshape(1, d), g_ffn.reshape(1, d)
    return pl.pallas_call(
        _post_mixer_kernel,
        grid=(m // tm,),
        in_specs=[rows(ya.shape[1]), rows(yb.shape[1]), rows(d), const(wa), const(wb), const(g_xa), const(wq),
                  per_batch(kt), per_batch(v), const(wo), const(g_ffn), const(w_hi), const(w_lo), const(bias),
                  const(before)],
        out_specs=[rows(d), rows(d), rows(LANES), pl.BlockSpec((1, LANES), lambda i: (0, 0))],
        out_shape=[jax.ShapeDtypeStruct((m, d), F32), jax.ShapeDtypeStruct((m, d), F32),
                   jax.ShapeDtypeStruct((m, LANES), F32), jax.ShapeDtypeStruct((1, LANES), F32)],
        scratch_shapes=[pltpu.VMEM((1, LANES), F32)],
        compiler_params=_cparams("arbitrary"),
        name="post_mixer",
    )(ya, yb, h, wa, wb, g_xa, wq, kt, v, wo, g_ffn, w_hi, w_lo, bias, before)


def _expert_kernel(beid_ref, valid_ref, first_ref, slot_ref, next_ref, x_ref, wg_hbm, wu_hbm, wd_hbm, o_ref,
                   wg32_ref, wu32_ref, wd32_ref, wgb_ref, wub_ref, wdb_ref, sem_ref, *, layer):
    i = pl.program_id(0)
    valid = valid_ref[i]

    def weight_copies(expert, slot):
        return (pltpu.make_async_copy(wg_hbm.at[layer, expert], wg32_ref.at[slot], sem_ref.at[slot, 0]),
                pltpu.make_async_copy(wu_hbm.at[layer, expert], wu32_ref.at[slot], sem_ref.at[slot, 1]),
                pltpu.make_async_copy(wd_hbm.at[layer, expert], wd32_ref.at[slot], sem_ref.at[slot, 2]))

    @pl.when(i == 0)
    def _():
        for copy in weight_copies(beid_ref[0], 0):
            copy.start()

    @pl.when(first_ref[i] == 1)
    def _():
        slot = slot_ref[i]
        for copy in weight_copies(beid_ref[i], slot):
            copy.wait()
        wgb_ref[...] = wg32_ref[slot].astype(BF16)
        wub_ref[...] = wu32_ref[slot].astype(BF16)
        wdb_ref[...] = wd32_ref[slot].astype(BF16)

        @pl.when(next_ref[i] >= 0)
        def _():
            for copy in weight_copies(next_ref[i], 1 - slot):
                copy.start()

    half = MOE_ROWS // 2

    def ffn(n_halves):
        row = lax.broadcasted_iota(jnp.int32, (half, x_ref.shape[1]), 0)
        xs = [jnp.where(row + k * half < valid, x_ref[k * half:(k + 1) * half, :], 0.0).astype(BF16)
              for k in range(n_halves)]
        gates = [jnp.dot(x, wgb_ref[...], preferred_element_type=F32) for x in xs]
        ups = [jnp.dot(x, wub_ref[...], preferred_element_type=F32) for x in xs]
        acts = [(_silu(g) * u).astype(BF16) for g, u in zip(gates, ups)]
        for k, act in enumerate(acts):
            o_ref[k * half:(k + 1) * half, :] = jnp.dot(act, wdb_ref[...], preferred_element_type=F32)

    @pl.when(valid > half)
    def _():
        ffn(2)

    @pl.when((valid > 0) & (valid <= half))
    def _():
        ffn(1)
        o_ref[half:, :] = jnp.zeros((half, o_ref.shape[1]), F32)

    @pl.when(valid == 0)
    def _():
        o_ref[...] = jnp.zeros_like(o_ref)


def moe_experts(blocks, xs, w_gate, w_up, w_down, layer):
    n_slots, d = xs.shape
    rows = MOE_ROWS
    ff = w_gate.shape[3]
    grid_spec = pltpu.PrefetchScalarGridSpec(
        num_scalar_prefetch=len(blocks),
        grid=(n_slots // rows,),
        in_specs=[
            pl.BlockSpec((rows, d), lambda i, *_: (i, 0)),
            pl.BlockSpec(memory_space=pl.ANY),
            pl.BlockSpec(memory_space=pl.ANY),
            pl.BlockSpec(memory_space=pl.ANY),
        ],
        out_specs=pl.BlockSpec((rows, d), lambda i, *_: (i, 0)),
        scratch_shapes=[pltpu.VMEM((2, d, ff), F32), pltpu.VMEM((2, d, ff), F32), pltpu.VMEM((2, ff, d), F32),
                        pltpu.VMEM((d, ff), BF16), pltpu.VMEM((d, ff), BF16), pltpu.VMEM((ff, d), BF16),
                        pltpu.SemaphoreType.DMA((2, 3))],
    )
    return pl.pallas_call(
        functools.partial(_expert_kernel, layer=layer),
        grid_spec=grid_spec,
        out_shape=jax.ShapeDtypeStruct((n_slots, d), F32),
        compiler_params=_cparams("arbitrary"),
        name="moe_experts",
    )(*blocks, xs, w_gate, w_up, w_down)


def _sc_mesh():
    return plsc.VectorSubcoreMesh(core_axis_name="c", subcore_axis_name="s",
                                  num_cores=SC_CORES, num_subcores=SC_SUBCORES)


def _sc_worker():
    return lax.axis_index("s") * SC_CORES + lax.axis_index("c")


def sc_scatter_rows(x, dest0, dest1, n_slots):
    n_tok, d = x.shape
    per_worker = n_tok // SC_WORKERS
    n_chunks = per_worker // SC_CHUNK
    shape3 = (SC_WORKERS, n_chunks, SC_CHUNK)

    @functools.partial(
        pl.kernel, mesh=_sc_mesh(), out_type=jax.ShapeDtypeStruct((n_slots, d), x.dtype),
        scratch_types=[pltpu.VMEM((n_chunks, SC_CHUNK), jnp.int32), pltpu.VMEM((n_chunks, SC_CHUNK), jnp.int32),
                       pltpu.VMEM((SC_CHUNK, d), x.dtype)],
        name="moe_scatter_rows")
    def scatter(x_hbm, d0_hbm, d1_hbm, out_hbm, i0_v, i1_v, rows_v):
        wid = _sc_worker()
        pltpu.sync_copy(d0_hbm.at[wid], i0_v)
        pltpu.sync_copy(d1_hbm.at[wid], i1_v)

        @pl.loop(0, n_chunks)
        def _(j):
            start = pl.multiple_of(wid * per_worker + j * SC_CHUNK, SC_CHUNK)
            pltpu.sync_copy(x_hbm.at[pl.ds(start, SC_CHUNK)], rows_v)
            pltpu.sync_copy(rows_v, out_hbm.at[i0_v.at[j]])
            pltpu.sync_copy(rows_v, out_hbm.at[i1_v.at[j]])

    return scatter(x, dest0.reshape(shape3), dest1.reshape(shape3))


def sc_gather_rows(table, idx):
    n_out = idx.shape[0]
    d = table.shape[1]
    per_worker = n_out // SC_WORKERS
    n_chunks = per_worker // SC_CHUNK

    @functools.partial(
        pl.kernel, mesh=_sc_mesh(), out_type=jax.ShapeDtypeStruct((n_out, d), table.dtype),
        scratch_types=[pltpu.VMEM((n_chunks, SC_CHUNK), jnp.int32), pltpu.VMEM((SC_CHUNK, d), table.dtype)],
        name="moe_gather_rows")
    def gather(table_hbm, idx_hbm, out_hbm, idx_v, rows_v):
        wid = _sc_worker()
        pltpu.sync_copy(idx_hbm.at[wid], idx_v)

        @pl.loop(0, n_chunks)
        def _(j):
            start = pl.multiple_of(wid * per_worker + j * SC_CHUNK, SC_CHUNK)
            pltpu.sync_copy(table_hbm.at[idx_v.at[j]], rows_v)
            pltpu.sync_copy(rows_v, out_hbm.at[pl.ds(start, SC_CHUNK)])

    return gather(table, idx.reshape(SC_WORKERS, n_chunks, SC_CHUNK))


def _combine_kernel(h_ref, y0_ref, y1_ref, r_ref, g_ref, o_ref, *, final_norm):
    route = r_ref[...]
    h = h_ref[...] + (route[:, 2:3] * y0_ref[...] + route[:, 3:4] * y1_ref[...])
    o_ref[...] = _rms(h, g_ref[...]) if final_norm else h


def moe_combine(h, row0, y01, route, g, final_norm, tm=512):
    d = h.shape[1]
    m = y01.shape[0] // 2
    tm = min(tm, m)
    t0 = row0 // tm
    mine = lambda w: pl.BlockSpec((tm, w), lambda i: (t0 + i, 0))
    return pl.pallas_call(
        functools.partial(_combine_kernel, final_norm=final_norm),
        grid=(m // tm,),
        in_specs=[mine(d), pl.BlockSpec((tm, d), lambda i: (i, 0)), pl.BlockSpec((tm, d), lambda i: (i + m // tm, 0)),
                  mine(LANES), pl.BlockSpec((1, d), lambda i: (0, 0))],
        out_specs=mine(d),
        out_shape=jax.ShapeDtypeStruct(h.shape, F32),
        input_output_aliases={0: 0},
        compiler_params=_cparams("parallel"),
        name="moe_combine",
    )(h, y01, y01, route, g.reshape(1, d))


def _pad_cols(w):
    return jnp.pad(w, ((0, 0), (0, LANES - w.shape[1])))


def _dispatch(route, counts, n_tok):
    rows = MOE_ROWS
    counts = counts[0, MOE_GROUPS:MOE_GROUPS + MOE_EXPERTS].astype(jnp.int32)
    padded = (counts + rows - 1) // rows * rows
    pad_end = jnp.cumsum(padded)
    pad_start = pad_end - padded
    eid = route[:, 0:2].astype(jnp.int32)
    rank = route[:, 4:6].astype(jnp.int32)
    hot = eid[:, :, None] == jnp.arange(MOE_EXPERTS, dtype=jnp.int32)
    dest = jnp.sum(jnp.where(hot, pad_start, 0), axis=-1) + rank
    n_blocks = -(-(2 * n_tok + MOE_EXPERTS * (rows - 1)) // rows)
    block_start = jnp.arange(n_blocks, dtype=jnp.int32) * rows
    block_eid = jnp.minimum(jnp.sum(block_start[:, None] >= pad_end[None, :], axis=-1), MOE_EXPERTS - 1)
    filled = (pad_start + counts)[block_eid]
    block_valid = jnp.clip(filled - block_start, 0, rows)
    used = block_valid > 0
    first = used & jnp.concatenate([jnp.ones((1,), bool), block_eid[1:] != block_eid[:-1]])
    slot = (jnp.cumsum(first) - 1) % 2
    order = jnp.arange(n_blocks, dtype=jnp.int32)
    later_first = lax.cummin(jnp.where(first, order, n_blocks)[::-1])[::-1]
    following = jnp.concatenate([later_first[1:], jnp.full((1,), n_blocks, jnp.int32)])
    block_next = jnp.where(following < n_blocks, block_eid[jnp.minimum(following, n_blocks - 1)], -1)
    blocks = tuple(a.astype(jnp.int32) for a in (block_eid, block_valid, first, slot, block_next))
    return dest[:, 0], dest[:, 1], blocks, n_blocks * rows


def _router_weights(w_group, b_group, w_expert, b_expert):
    w_r = _pad_cols(jnp.concatenate([w_group, w_expert], axis=1))
    w_hi = w_r.astype(BF16)
    w_lo = (w_r - w_hi.astype(F32)).astype(BF16)
    return w_hi, w_lo, _pad_lanes(jnp.concatenate([b_group, b_expert]))


def _moe_layer(h, xn, route, counts, w_gate, w_up, w_down, layer, final_g):
    n_tok, d = h.shape
    dest0, dest1, blocks, n_slots = _dispatch(route, counts, n_tok)
    xs = sc_scatter_rows(xn, dest0, dest1, n_slots)
    ys = moe_experts(blocks, xs, w_gate, w_up, w_down, layer)
    g = jnp.ones((d,), F32) if final_g is None else final_g
    per_part = n_tok // COMBINE_PARTS
    gathered = [sc_gather_rows(ys, jnp.concatenate([dest[p * per_part:(p + 1) * per_part] for dest in (dest0, dest1)]))
                for p in range(COMBINE_PARTS)]
    for p, y01 in enumerate(gathered):
        h = moe_combine(h, p * per_part, y01, route, g, final_g is not None)
    return h


def _memory_kv(memn_in, mem_norm, wk, wv):
    bsz, m, d = memn_in.shape
    w = jnp.concatenate([wk, wv], axis=1).astype(BF16)
    kv, _ = rms_matmul(memn_in.reshape(bsz * m, d), mem_norm, w, jnp.zeros((d, LANES), BF16))
    k = kv[:, :d].reshape(bsz, m, d)
    v = kv[:, d:].reshape(bsz, m, d)
    return jnp.swapaxes(k, 1, 2).astype(BF16), v.astype(BF16)


def kernel(x, mem, mem_norm, final_norm, norm_mix, norm_xa, norm_ffn, xa_wq, xa_wk, xa_wv, xa_wo, moe_w_group, moe_b_group, moe_w_expert, moe_b_expert, moe_w_gate, moe_w_up, moe_w_down, ev_w_in, ev_sc_conv, ev_ssm_conv_w, ev_ssm_conv_b, ev_ssm_dt_bias, ev_ssm_a_log, ev_ssm_d, ev_ssm_norm, ev_w_out, od_w_in, od_gdn_conv, od_gdn_dt_bias, od_gdn_a_log, od_gdn_norm, od_w_out):
    bsz, length, d = x.shape
    n_tok = bsz * length
    depth = norm_mix.shape[0]
    h = x.reshape(n_tok, d)
    for layer in range(depth):
        i = layer // 2
        if layer % 2 == 0:
            w = ev_w_in[i]
            z0 = 3 * SC_DIM
            xbc0 = z0 + SSM_INNER
            w_conv = w[:, xbc0:xbc0 + SSM_XBC].astype(BF16)
            w_main = jnp.concatenate([w[:, z0:xbc0], w[:, :z0]], axis=1).astype(BF16)
            w_small = _pad_cols(w[:, xbc0 + SSM_XBC:]).astype(BF16)
            xbc = rms_matmul_conv(h, norm_mix[layer], w_conv, ev_ssm_conv_w[i], ev_ssm_conv_b[i], length)
            y, small = rms_matmul(h, norm_mix[layer], w_main, w_small, tm=1024, tn=w_main.shape[1])
            y3 = y.reshape(bsz, length, -1)
            small3 = small.reshape(bsz, length, LANES)
            smallt = jnp.swapaxes(small3[:, :, :16], 1, 2)
            ya = short_conv_mixer(y3, ev_sc_conv[i], SSM_INNER // SC_DIM)
            yb = ssd_mixer(xbc.reshape(bsz, length, -1), y3, small3, smallt, ev_ssm_dt_bias[i],
                           ev_ssm_a_log[i], ev_ssm_d[i], ev_ssm_norm[i])
            w_out = ev_w_out[i].astype(BF16)
            split = SC_DIM
        else:
            w = od_w_in[i]
            qkv_w = 3 * GDN_HEADS * GDN_D
            z_end = qkv_w + GDN_HEADS * GDN_D
            w_conv = w[:, :qkv_w].astype(BF16)
            w_main = jnp.concatenate([w[:, qkv_w:z_end], w[:, z_end + 2 * GDN_HEADS:]], axis=1).astype(BF16)
            w_small = _pad_cols(w[:, z_end:z_end + 2 * GDN_HEADS]).astype(BF16)
            qkv = rms_matmul_conv(h, norm_mix[layer], w_conv, od_gdn_conv[i], jnp.zeros((qkv_w,), F32), length)
            y, small = rms_matmul(h, norm_mix[layer], w_main, w_small, tm=1024, tn=w_main.shape[1])
            y3 = y.reshape(bsz, length, -1)
            small3 = small.reshape(bsz, length, LANES)
            smallt = jnp.swapaxes(small3[:, :, :16], 1, 2)
            ya = gated_deltanet_mixer(qkv.reshape(bsz, length, -1), y3, small3, smallt, od_gdn_dt_bias[i],
                                      od_gdn_a_log[i], od_gdn_norm[i])
            yb = stick_breaking_mixer(y3, GDN_HEADS * GDN_D)
            w_out = od_w_out[i].astype(BF16)
            split = GDN_HEADS * GDN_D
        kt, v = _memory_kv(mem, mem_norm, xa_wk[layer], xa_wv[layer])
        w_hi, w_lo, bias = _router_weights(moe_w_group[layer], moe_b_group[layer], moe_w_expert[layer],
                                           moe_b_expert[layer])
        h, xn, route, counts = post_mixer(
            ya.reshape(n_tok, -1), yb.reshape(n_tok, -1), h, w_out[:split], w_out[split:], norm_xa[layer],
            xa_wq[layer].astype(BF16), kt, v, xa_wo[layer].astype(BF16), norm_ffn[layer], w_hi, w_lo, bias)
        h = _moe_layer(h, xn, route, counts, moe_w_gate, moe_w_up, moe_w_down, layer,
                       final_norm if layer == depth - 1 else None)
    return h.reshape(bsz, length, d)
```

```python
import functools

import jax
import jax.numpy as jnp
import numpy as np
from jax import lax
from jax.experimental import pallas as pl
from jax.experimental.pallas import tpu as pltpu
from jax.experimental.pallas import tpu_sc as plsc

F32 = jnp.float32
BF16 = jnp.bfloat16
EPS = 1e-6

D_MODEL = 1024
MEM_LEN = 256
SC_DIM = 512
SSM_HEADS = 16
SSM_HEAD_DIM = 64
SSM_INNER = 1024
SSM_GROUPS = 2
SSM_STATE = 128
SSM_XBC = SSM_INNER + 2 * SSM_GROUPS * SSM_STATE
SSD_CHUNK = 128
GDN_HEADS = 8
GDN_D = 128
GDN_CHUNK = 64
GDN_TILE = 128
SB_HEADS = 8
SB_HEAD_DIM = 64
SB_DIM = 512
SB_BLOCK = 128
SB_STEP_HEADS = 8
XA_HEADS = 4
XA_HEAD_DIM = 256
MOE_GROUPS = 4
MOE_PER_GROUP = 8
MOE_EXPERTS = 32
MOE_FF = 512
MOE_ROWS = 512
SC_CORES = 2
SC_SUBCORES = 16
SC_WORKERS = SC_CORES * SC_SUBCORES
SC_CHUNK = 64
HALO = 8
CONV_CHUNK = 512
LANES = 128
SB_LOG_ZERO = -104.0
VMEM_LIMIT = 56 * 1024 * 1024


def _cparams(*sem):
    return pltpu.CompilerParams(dimension_semantics=sem, vmem_limit_bytes=VMEM_LIMIT)


def _mm(a, b):
    return jnp.dot(a.astype(BF16), b.astype(BF16), preferred_element_type=F32)


def _mm_nt(a, b):
    return lax.dot_general(a.astype(BF16), b.astype(BF16), (((1,), (1,)), ((), ())),
                           preferred_element_type=F32)


def _split_bf16(x, n):
    parts, r = [], x
    for _ in range(n):
        p = r.astype(BF16)
        parts.append(p)
        r = r - p.astype(F32)
    return parts


def _mm_sel_rhs(x, sel, n=3):
    return sum(jnp.dot(p, sel, preferred_element_type=F32) for p in _split_bf16(x, n))


def _mm_sel_lhs(sel, x, n=3):
    return sum(jnp.dot(sel, p, preferred_element_type=F32) for p in _split_bf16(x, n))


def _pack_halves(x):
    n = x.shape[1] // 2
    lo = pltpu.bitcast(x[:, :n].astype(BF16).astype(F32), jnp.int32)
    hi = pltpu.bitcast(x[:, n:].astype(BF16).astype(F32), jnp.int32)
    return lax.shift_right_logical(lo, 16) | (hi & jnp.int32(-65536))


def _unpack_halves(p):
    lo = pltpu.bitcast(lax.shift_left(p, 16), F32)
    hi = pltpu.bitcast(p & jnp.int32(-65536), F32)
    return jnp.concatenate([lo, hi], axis=1)


def _silu(x):
    return x * jax.nn.sigmoid(x)


def _softplus(x):
    return jnp.maximum(x, 0.0) + jnp.log(1.0 + jnp.exp(-jnp.abs(x)))


def _rms(x, g):
    return x * lax.rsqrt(jnp.mean(x * x, axis=-1, keepdims=True) + EPS) * g


def _rms_matmul_kernel(x_ref, g_ref, w_ref, ws_ref, o_ref, os_ref):
    xn = _rms(x_ref[...], g_ref[...]).astype(BF16)
    o_ref[...] = jnp.dot(xn, w_ref[...], preferred_element_type=F32)
    os_ref[...] = jnp.dot(xn, ws_ref[...], preferred_element_type=F32)


def rms_matmul(x, g, w, ws, tm=512, tn=512):
    m, k = x.shape
    n = w.shape[1]
    tm = min(tm, m)
    main, small = pl.pallas_call(
        _rms_matmul_kernel,
        grid=(n // tn, m // tm),
        in_specs=[
            pl.BlockSpec((tm, k), lambda j, i: (i, 0)),
            pl.BlockSpec((1, k), lambda j, i: (0, 0)),
            pl.BlockSpec((k, tn), lambda j, i: (0, j)),
            pl.BlockSpec((k, LANES), lambda j, i: (0, 0)),
        ],
        out_specs=[
            pl.BlockSpec((tm, tn), lambda j, i: (i, j)),
            pl.BlockSpec((None, tm, LANES), lambda j, i: (j, i, 0)),
        ],
        out_shape=[jax.ShapeDtypeStruct((m, n), F32), jax.ShapeDtypeStruct((n // tn, m, LANES), F32)],
        compiler_params=_cparams("parallel", "parallel"),
        name="rms_matmul",
    )(x, g.reshape(1, k), w, ws)
    return main, small[0]


def _causal_conv(ext_ref, w_ref, rows):
    width = w_ref.shape[0]
    ext = ext_ref[...]
    acc = None
    for j in range(width):
        shift = width - 1 - j
        moved = ext if shift == 0 else pltpu.roll(ext, shift, axis=0)
        term = w_ref[j:j + 1, :] * moved[HALO:HALO + rows, :]
        acc = term if acc is None else acc + term
    return acc


def _rms_matmul_conv_kernel(x_ref, g_ref, w_ref, cw_ref, cb_ref, o_ref, *ext_refs, tiles_per_seq):
    tm = x_ref.shape[0]
    starts_sequence = pl.program_id(1) % tiles_per_seq == 0

    @pl.when(starts_sequence)
    def _():
        for ext_ref in ext_refs:
            ext_ref[0:HALO, :] = jnp.zeros((HALO, CONV_CHUNK), F32)

    @pl.when(jnp.logical_not(starts_sequence))
    def _():
        for ext_ref in ext_refs:
            ext_ref[0:HALO, :] = ext_ref[tm:tm + HALO, :]

    xn = _rms(x_ref[...], g_ref[...]).astype(BF16)
    for c, ext_ref in enumerate(ext_refs):
        cols = slice(c * CONV_CHUNK, (c + 1) * CONV_CHUNK)
        ext_ref[HALO:, :] = jnp.dot(xn, w_ref[:, cols], preferred_element_type=F32)
        o_ref[:, cols] = _silu(_causal_conv(ext_ref, cw_ref.at[:, cols], tm) + cb_ref[:, cols])


def rms_matmul_conv(x, g, w, conv_w, conv_b, seq_len, tm=1024, tn=1536):
    m, k = x.shape
    n = w.shape[1]
    cols = lambda rows: pl.BlockSpec((rows, tn), lambda j, i: (0, j))
    return pl.pallas_call(
        functools.partial(_rms_matmul_conv_kernel, tiles_per_seq=seq_len // tm),
        grid=(n // tn, m // tm),
        in_specs=[
            pl.BlockSpec((tm, k), lambda j, i: (i, 0)),
            pl.BlockSpec((1, k), lambda j, i: (0, 0)),
            cols(k), cols(conv_w.shape[0]), cols(1),
        ],
        out_specs=pl.BlockSpec((tm, tn), lambda j, i: (i, j)),
        out_shape=jax.ShapeDtypeStruct((m, n), F32),
        scratch_shapes=[pltpu.VMEM((tm + HALO, CONV_CHUNK), F32)] * (tn // CONV_CHUNK),
        compiler_params=_cparams("arbitrary", "arbitrary"),
        name="rms_matmul_conv",
    )(x, g.reshape(1, k), w, conv_w, conv_b.reshape(1, n))


def _halo_index(rows):
    step = rows // HALO
    return lambda i: jnp.maximum(i * step - 1, 0)


def _sc_kernel(b_ref, c_ref, x_ref, ch_ref, xh_ref, w_ref, o_ref, ext_ref):
    rows = o_ref.shape[0]
    first = pl.program_id(1) == 0
    ext_ref[0:HALO, :] = jnp.where(first, 0.0, ch_ref[...] * xh_ref[...])
    ext_ref[HALO:, :] = c_ref[...] * x_ref[...]
    o_ref[...] = b_ref[...] * _causal_conv(ext_ref, w_ref, rows)


def short_conv_mixer(y3, w, col0, tl=512):
    bsz, length, _ = y3.shape
    tl = min(tl, length)
    hidx = _halo_index(tl)
    return pl.pallas_call(
        _sc_kernel,
        grid=(bsz, length // tl),
        in_specs=[
            pl.BlockSpec((None, tl, SC_DIM), lambda b, i: (b, i, col0)),
            pl.BlockSpec((None, tl, SC_DIM), lambda b, i: (b, i, col0 + 1)),
            pl.BlockSpec((None, tl, SC_DIM), lambda b, i: (b, i, col0 + 2)),
            pl.BlockSpec((None, HALO, SC_DIM), lambda b, i: (b, hidx(i), col0 + 1)),
            pl.BlockSpec((None, HALO, SC_DIM), lambda b, i: (b, hidx(i), col0 + 2)),
            pl.BlockSpec(w.shape, lambda b, i: (0, 0)),
        ],
        out_specs=pl.BlockSpec((None, tl, SC_DIM), lambda b, i: (b, i, 0)),
        out_shape=jax.ShapeDtypeStruct((bsz, length, SC_DIM), F32),
        scratch_shapes=[pltpu.VMEM((tl + HALO, SC_DIM), F32)],
        compiler_params=_cparams("parallel", "arbitrary"),
        name="short_conv_mixer",
    )(y3, y3, y3, y3, y3, w)


def _ssd_kernel(xbc_ref, z_ref, dt_ref, dtt_ref, dtb_r_ref, dtb_c_ref,
                alog_r_ref, alog_c_ref, d_ref, nw_ref, tri_ref, trit_ref, eh_ref, eq_ref,
                o_ref, s_ref):
    q = SSD_CHUNK
    hpg = SSM_HEADS // SSM_GROUPS
    gw = hpg * SSM_HEAD_DIM

    @pl.when(pl.program_id(1) == 0)
    def _():
        s_ref[...] = jnp.zeros_like(s_ref)

    xbc = xbc_ref[...]
    xs = xbc[:, :SSM_INNER]
    bm = xbc[:, SSM_INNER:SSM_INNER + SSM_GROUPS * SSM_STATE]
    cm = xbc[:, SSM_INNER + SSM_GROUPS * SSM_STATE:]

    dt = _softplus(dt_ref[...] + dtb_r_ref[...])
    acs = _mm_sel_lhs(tri_ref[...], dt * -jnp.exp(alog_r_ref[...]))
    dtt = _softplus(dtt_ref[...] + dtb_c_ref[...])
    acst = _mm_sel_rhs(dtt * -jnp.exp(alog_c_ref[...]), trit_ref[...])
    dt_full = _mm_sel_rhs(dt, eh_ref[...])
    acs_full = _mm_sel_rhs(acs, eh_ref[...])
    acs_col = _mm_sel_rhs(acs, eq_ref[...])

    xdt = xs * dt_full
    acs_last = acs_full[q - 1:q, :]
    xw = xdt * jnp.exp(acs_last - acs_full)
    chunk_decay = jnp.exp(acs_last)

    row = lax.broadcasted_iota(jnp.int32, (q, q), 0)
    col = lax.broadcasted_iota(jnp.int32, (q, q), 1)
    causal = row >= col
    lane = lax.broadcasted_iota(jnp.int32, (q, 2 * SSM_HEAD_DIM), 1)

    y_diag, y_off = [], []
    for g in range(SSM_GROUPS):
        bm_g = bm[:, g * SSM_STATE:(g + 1) * SSM_STATE]
        cm_g = cm[:, g * SSM_STATE:(g + 1) * SSM_STATE]
        cb_g = _mm_nt(cm_g, bm_g)
        state = s_ref[g]
        y_off.append(_mm(cm_g, state))
        s_ref[g] = state * chunk_decay[:, g * gw:(g + 1) * gw] + _mm(bm_g.T, xw[:, g * gw:(g + 1) * gw])
        for pair in range(hpg // 2):
            h0 = g * hpg + 2 * pair
            xdt_pair = xdt[:, h0 * SSM_HEAD_DIM:(h0 + 2) * SSM_HEAD_DIM]
            outs = []
            for h in (h0, h0 + 1):
                seg = acs_col[:, h * q:(h + 1) * q] - acst[h:h + 1, :]
                decay = jnp.where(causal, jnp.exp(seg), 0.0)
                outs.append(_mm(cb_g * decay, xdt_pair))
            y_diag.append(jnp.where(lane < SSM_HEAD_DIM, outs[0], outs[1]))
    y = (jnp.concatenate(y_diag, axis=1) + jnp.concatenate(y_off, axis=1) * jnp.exp(acs_full)
         + xs * d_ref[...])
    y = y * _silu(z_ref[...])
    halves = []
    for g in range(SSM_GROUPS):
        yg = y[:, g * gw:(g + 1) * gw]
        halves.append(yg * lax.rsqrt(jnp.mean(yg * yg, axis=-1, keepdims=True) + EPS))
    o_ref[...] = jnp.concatenate(halves, axis=1) * nw_ref[...]


def _pad_lanes(v, fill=0.0):
    return jnp.pad(v.astype(F32), (0, LANES - v.shape[0]), constant_values=fill).reshape(1, LANES)


def _pad_col(v, rows=16):
    return jnp.pad(v.astype(F32), (0, rows - v.shape[0])).reshape(rows, 1)


def ssd_mixer(xbc3, y3, small3, smallt, dt_bias, a_log, d_skip, norm_w):
    bsz, length, _ = y3.shape
    q = SSD_CHUNK
    tri = jnp.asarray(np.tril(np.ones((q, q), np.float32)), BF16)
    trit = jnp.asarray(np.triu(np.ones((q, q), np.float32)), BF16)
    heads = np.arange(LANES)[:, None]
    eh = jnp.asarray(heads == (np.arange(SSM_INNER)[None, :] // SSM_HEAD_DIM), BF16)
    eq = jnp.asarray(heads == (np.arange(SSM_HEADS * q)[None, :] // q), BF16)
    d_full = jnp.repeat(d_skip.astype(F32), SSM_HEAD_DIM).reshape(1, SSM_INNER)
    const = lambda a: pl.BlockSpec(a.shape, lambda b, c: (0,) * a.ndim)
    args = [_pad_lanes(dt_bias), _pad_col(dt_bias), _pad_lanes(a_log),
            _pad_col(a_log), d_full, norm_w.reshape(1, -1), tri, trit, eh, eq]
    return pl.pallas_call(
        _ssd_kernel,
        grid=(bsz, length // q),
        in_specs=[
            pl.BlockSpec((None, q, SSM_XBC), lambda b, c: (b, c, 0)),
            pl.BlockSpec((None, q, SSM_INNER), lambda b, c: (b, c, 0)),
            pl.BlockSpec((None, q, LANES), lambda b, c: (b, c, 0)),
            pl.BlockSpec((None, 16, q), lambda b, c: (b, 0, c)),
        ] + [const(a) for a in args],
        out_specs=pl.BlockSpec((None, q, SSM_INNER), lambda b, c: (b, c, 0)),
        out_shape=jax.ShapeDtypeStruct((bsz, length, SSM_INNER), F32),
        scratch_shapes=[pltpu.VMEM((SSM_GROUPS, SSM_STATE, SSM_INNER // SSM_GROUPS), F32)],
        compiler_params=_cparams("parallel", "arbitrary"),
        name="ssd_mixer",
    )(xbc3, y3, small3, smallt, *args)


def _unit_lower_inverse(mats, row, col):
    eye = jnp.where(row == col, 1.0, 0.0)
    blk = lambda n: (row >> (n.bit_length() - 1)) == (col >> (n.bit_length() - 1))
    p = [jnp.where(blk(16), -a, 0.0) for a in mats]
    t = [eye + x for x in p]
    for _ in range(3):
        p = [_mm(x, x) for x in p]
        t = [y + _mm(y, x) for y, x in zip(t, p)]
    for n in (16, 32):
        band = blk(2 * n) & jnp.logical_not(blk(n))
        left = [_mm(y, jnp.where(band, a, 0.0)) for y, a in zip(t, mats)]
        t = [y - _mm(x, y) for y, x in zip(t, left)]
    return t


def _gdn_kernel(qkv_ref, z_ref, ab_ref, abt_ref, dtb_r_ref, dtb_c_ref, alog_r_ref,
                alog_c_ref, nw_ref, tri_ref, trit_ref, eg_ref, eb_ref, o_ref, s_ref):
    n = GDN_TILE
    c = GDN_CHUNK
    d = GDN_D
    hd = GDN_HEADS * d

    @pl.when(pl.program_id(1) == 0)
    def _():
        s_ref[...] = jnp.zeros_like(s_ref)

    qkv = qkv_ref[...]
    z = z_ref[...]

    ab = ab_ref[...]
    g = -jnp.exp(alog_r_ref[...]) * _softplus(ab + dtb_r_ref[...])
    gc_full = _mm_sel_rhs(_mm_sel_lhs(tri_ref[...], g), eg_ref[...])
    beta_full = _mm_sel_rhs(jax.nn.sigmoid(ab), eb_ref[...])
    gt = -jnp.exp(alog_c_ref[...]) * _softplus(abt_ref[...] + dtb_c_ref[...])
    gct = _mm_sel_rhs(gt, trit_ref[...])

    row = lax.broadcasted_iota(jnp.int32, (n, n), 0)
    col = lax.broadcasted_iota(jnp.int32, (n, n), 1)
    same = (row >> (c.bit_length() - 1)) == (col >> (c.bit_length() - 1))
    incl = same & (row >= col)
    strict = same & (row > col)
    zeros_half = jnp.zeros((c, d), F32)

    heads = range(GDN_HEADS)
    sl = [slice(h * d, (h + 1) * d) for h in heads]
    l2n = lambda x: x * lax.rsqrt(jnp.sum(x * x, axis=-1, keepdims=True) + EPS)
    qn = [l2n(qkv[:, sl[h]]) * (d ** -0.5) for h in heads]
    kn = [l2n(qkv[:, hd + h * d:hd + (h + 1) * d]) for h in heads]
    vh = [qkv[:, 2 * hd + h * d:2 * hd + (h + 1) * d] for h in heads]
    gcol = [gc_full[:, sl[h]] for h in heads]
    beta = [beta_full[:, sl[h]] for h in heads]
    edec = [jnp.exp(gcol[h] - gct[h:h + 1, :]) for h in heads]
    egc = [jnp.exp(x) for x in gcol]
    kb = [kn[h] * beta[h] for h in heads]
    lower = [jnp.where(strict, _mm_nt(kb[h], kn[h]) * edec[h], 0.0) for h in heads]
    aqk = [jnp.where(incl, _mm_nt(qn[h], kn[h]) * edec[h], 0.0) for h in heads]
    tinv = _unit_lower_inverse(lower, row, col)
    sol = [_mm(tinv[h], jnp.concatenate([vh[h] * beta[h], kb[h] * egc[h]], axis=1)) for h in heads]
    qd = [qn[h] * egc[h] for h in heads]
    glast = [(gcol[h][c - 1:c, :], gcol[h][n - 1:n, :]) for h in heads]
    kdt = [(kn[h] * jnp.exp(jnp.concatenate([jnp.broadcast_to(glast[h][0], (c, d)),
                                             jnp.broadcast_to(glast[h][1], (c, d))], axis=0) - gcol[h])).T
           for h in heads]
    s0 = [s_ref[h] for h in heads]
    v0 = [sol[h][:c, :d] - _mm(sol[h][:c, d:], s0[h]) for h in heads]
    s1 = [s0[h] * jnp.exp(glast[h][0]) + _mm(kdt[h], jnp.concatenate([v0[h], zeros_half], axis=0)) for h in heads]
    v1 = [sol[h][c:, :d] - _mm(sol[h][c:, d:], s1[h]) for h in heads]
    for h in heads:
        s_ref[h] = s1[h] * jnp.exp(glast[h][1]) + _mm(kdt[h], jnp.concatenate([zeros_half, v1[h]], axis=0))
    outs = []
    for h in heads:
        o = (jnp.concatenate([_mm(qd[h][:c], s0[h]), _mm(qd[h][c:], s1[h])], axis=0)
             + _mm(aqk[h], jnp.concatenate([v0[h], v1[h]], axis=0)))
        o = o * lax.rsqrt(jnp.mean(o * o, axis=-1, keepdims=True) + EPS) * nw_ref[...]
        outs.append(o * _silu(z[:, sl[h]]))
    o_ref[...] = jnp.concatenate(outs, axis=1)


def gated_deltanet_mixer(qkv3, y3, small3, smallt, dt_bias, a_log, norm_w):
    bsz, length, _ = y3.shape
    n = GDN_TILE
    hd = GDN_HEADS * GDN_D
    idx = np.arange(n)
    same = (idx[:, None] // GDN_CHUNK) == (idx[None, :] // GDN_CHUNK)
    tri = jnp.asarray(same & (idx[:, None] >= idx[None, :]), BF16)
    trit = jnp.asarray(same & (idx[:, None] <= idx[None, :]), BF16)
    lanes = np.arange(LANES)[:, None]
    heads = np.arange(hd)[None, :] // GDN_D
    eg = jnp.asarray(lanes == heads, BF16)
    eb = jnp.asarray(lanes == heads + GDN_HEADS, BF16)
    const = lambda a: pl.BlockSpec(a.shape, lambda b, c: (0,) * a.ndim)
    args = [_pad_lanes(dt_bias), _pad_col(dt_bias), _pad_lanes(a_log), _pad_col(a_log),
            norm_w.reshape(1, -1), tri, trit, eg, eb]
    return pl.pallas_call(
        _gdn_kernel,
        grid=(bsz, length // n),
        in_specs=[
            pl.BlockSpec((None, n, 3 * hd), lambda b, c: (b, c, 0)),
            pl.BlockSpec((None, n, hd), lambda b, c: (b, c, 0)),
            pl.BlockSpec((None, n, LANES), lambda b, c: (b, c, 0)),
            pl.BlockSpec((None, 16, n), lambda b, c: (b, 0, c)),
        ] + [const(a) for a in args],
        out_specs=pl.BlockSpec((None, n, hd), lambda b, c: (b, c, 0)),
        out_shape=jax.ShapeDtypeStruct((bsz, length, hd), F32),
        scratch_shapes=[pltpu.VMEM((GDN_HEADS, GDN_D, GDN_D), F32)],
        compiler_params=_cparams("parallel", "arbitrary"),
        name="gated_deltanet",
    )(qkv3, y3, small3, smallt, *args)


def _sb_kernel(q_ref, k_ref, v_ref, upper_ref, o_ref):
    blk = SB_BLOCK
    pair_w = 2 * SB_HEAD_DIM
    n_pairs = SB_STEP_HEADS // 2
    i = pl.program_id(2)
    q = q_ref[...] * (SB_HEAD_DIM ** -0.5)
    lane = lax.broadcasted_iota(jnp.int32, (blk, pair_w), 1)
    first_head = lane < SB_HEAD_DIM
    qs = []
    for p in range(n_pairs):
        q2 = q[:, p * pair_w:(p + 1) * pair_w]
        qs += [jnp.where(first_head, q2, 0.0).astype(BF16), jnp.where(first_head, 0.0, q2).astype(BF16)]
    row = lax.broadcasted_iota(jnp.int32, (blk, blk), 0)
    col = lax.broadcasted_iota(jnp.int32, (blk, blk), 1)
    earlier = col < row
    upper = upper_ref[...]
    heads = range(SB_STEP_HEADS)

    def key_block(kb, accs, sticks, diagonal):
        start = pl.multiple_of(kb * blk, blk)
        k = k_ref[pl.ds(start, blk), :].astype(BF16)
        v = v_ref[pl.ds(start, blk), :].astype(BF16)
        kp = [k[:, p * pair_w:(p + 1) * pair_w] for p in range(n_pairs)]
        vp = [v[:, p * pair_w:(p + 1) * pair_w] for p in range(n_pairs)]
        logits = [lax.dot_general(qs[h], kp[h // 2], (((1,), (1,)), ((), ())), preferred_element_type=F32)
                  for h in heads]
        log_keep = [-_softplus(x) for x in logits]
        if diagonal:
            log_keep = [jnp.where(earlier, x, 0.0) for x in log_keep]
        between = [_mm_sel_rhs(log_keep[h], upper, 2) + sticks[h] for h in heads]
        w = [jnp.exp(logits[h] + log_keep[h] + between[h]) for h in heads]
        if diagonal:
            w = [jnp.where(earlier, x, 0.0) for x in w]
        pv = [jnp.dot(w[h].astype(BF16), vp[h // 2], preferred_element_type=F32) for h in heads]
        accs = tuple(accs[p] + jnp.where(first_head, pv[2 * p], pv[2 * p + 1]) for p in range(n_pairs))
        sticks = tuple(sticks[h] + jnp.sum(log_keep[h], axis=-1, keepdims=True) for h in heads)
        return accs, sticks

    accs = tuple(jnp.zeros((blk, pair_w), F32) for _ in range(n_pairs))
    sticks = tuple(jnp.zeros((blk, 1), F32) for _ in heads)
    accs, sticks = key_block(i, accs, sticks, True)

    def alive(state):
        kb, _, sticks = state
        longest = sticks[0]
        for s in sticks[1:]:
            longest = jnp.maximum(longest, s)
        return (kb >= 0) & (jnp.max(longest) > SB_LOG_ZERO)

    def body(state):
        kb, accs, sticks = state
        accs, sticks = key_block(kb, accs, sticks, False)
        return kb - 1, accs, sticks

    _, accs, _ = lax.while_loop(alive, body, (i - 1, accs, sticks))
    o_ref[...] = jnp.concatenate(accs, axis=1)


def stick_breaking_mixer(y3, col0):
    bsz, length, _ = y3.shape
    blk = SB_BLOCK
    step_w = SB_STEP_HEADS * SB_HEAD_DIM
    steps = SB_DIM // step_w
    q0 = col0 // step_w
    idx = np.arange(blk)
    upper = jnp.asarray(idx[:, None] > idx[None, :], BF16)
    resident = lambda off: pl.BlockSpec((None, length, step_w), lambda b, p, i: (b, 0, q0 + off + p),
                                        pipeline_mode=pl.Buffered(1))
    return pl.pallas_call(
        _sb_kernel,
        grid=(bsz, steps, length // blk),
        in_specs=[
            pl.BlockSpec((None, blk, step_w), lambda b, p, i: (b, i, q0 + p)),
            resident(steps),
            resident(2 * steps),
            pl.BlockSpec((blk, blk), lambda b, p, i: (0, 0)),
        ],
        out_specs=pl.BlockSpec((None, blk, step_w), lambda b, p, i: (b, i, p)),
        out_shape=jax.ShapeDtypeStruct((bsz, length, SB_DIM), F32),
        compiler_params=_cparams("parallel", "parallel", "arbitrary"),
        name="stick_breaking",
    )(y3, y3, y3, upper)


def _mixer_out(a_ref, b_ref, h_ref, wa_ref, wb_ref, rows):
    return h_ref[rows, :] + (jnp.dot(a_ref[rows, :].astype(BF16), wa_ref[...], preferred_element_type=F32)
                             + jnp.dot(b_ref[rows, :].astype(BF16), wb_ref[...], preferred_element_type=F32))


def _cross_attention(hs, g_ref, wq_ref, kt_ref, v_ref, wo_ref):
    us = [_rms(h, g_ref[...]).astype(BF16) for h in hs]
    qs = [jnp.dot(u, wq_ref[...], preferred_element_type=F32) for u in us]
    heads = [[] for _ in hs]
    for hd in range(XA_HEADS):
        sl = slice(hd * XA_HEAD_DIM, (hd + 1) * XA_HEAD_DIM)
        ss = [jnp.dot(q[:, sl].astype(BF16), kt_ref[sl, :], preferred_element_type=F32) * (XA_HEAD_DIM ** -0.5)
              for q in qs]
        ps = [jnp.exp(s - jnp.max(s, axis=-1, keepdims=True)) for s in ss]
        ps = [p / jnp.sum(p, axis=-1, keepdims=True) for p in ps]
        for k, p in enumerate(ps):
            heads[k].append(jnp.dot(p.astype(BF16), v_ref[:, sl], preferred_element_type=F32))
    os_ = [jnp.concatenate(hk, axis=1).astype(BF16) for hk in heads]
    return [h + jnp.dot(o, wo_ref[...], preferred_element_type=F32) for h, o in zip(hs, os_)]


def _route(xn, whi_ref, wlo_ref, b_ref, before_ref, run_ref):
    x_hi = xn.astype(BF16)
    x_lo = (xn - x_hi.astype(F32)).astype(BF16)
    logits = (jnp.dot(x_hi, whi_ref[...], preferred_element_type=F32)
              + jnp.dot(x_lo, whi_ref[...], preferred_element_type=F32)
              + jnp.dot(x_hi, wlo_ref[...], preferred_element_type=F32) + b_ref[...])
    lane = lax.broadcasted_iota(jnp.int32, logits.shape, 1).astype(F32)
    neg = -1e30
    none = float(LANES)

    def top(vals):
        best = jnp.max(vals, axis=-1, keepdims=True)
        where = jnp.min(jnp.where(vals == best, lane, none), axis=-1, keepdims=True)
        return best, where

    gl = jnp.where(lane < MOE_GROUPS, logits, neg)
    gbest, gsel = top(gl)
    gprob = 1.0 / jnp.sum(jnp.exp(gl - gbest), axis=-1, keepdims=True)
    lo = MOE_GROUPS + gsel * MOE_PER_GROUP
    el = jnp.where((lane >= lo) & (lane < lo + MOE_PER_GROUP), logits, neg)
    m1, i1 = top(el)
    m2, i2 = top(jnp.where(lane == i1, neg, el))
    e = jnp.exp(m2 - m1)
    gate1 = gprob / (1.0 + e)
    gate2 = gprob * e / (1.0 + e)

    hot1 = lane == i1
    hot2 = lane == i2
    one1 = jnp.where(hot1, 1.0, 0.0)
    one2 = jnp.where(hot2, 1.0, 0.0)
    before = before_ref[...]
    prefix1 = jnp.dot(before, one1.astype(BF16), preferred_element_type=F32)
    prefix2 = jnp.dot(before, one2.astype(BF16), preferred_element_type=F32)
    total1 = jnp.sum(one1, axis=0, keepdims=True)
    running = run_ref[...]
    rank1 = jnp.sum(jnp.where(hot1, prefix1 + running, 0.0), axis=-1, keepdims=True)
    rank2 = jnp.sum(jnp.where(hot2, prefix2 + (running + total1), 0.0), axis=-1, keepdims=True)
    running = running + total1 + jnp.sum(one2, axis=0, keepdims=True)
    run_ref[...] = running

    fields = (i1 - MOE_GROUPS, i2 - MOE_GROUPS, gate1, gate2, rank1, rank2)
    out = jnp.zeros_like(logits)
    for k, val in enumerate(fields):
        out = jnp.where(lane == k, val, out)
    return out


def _post_mixer_kernel(a_ref, b_ref, h_ref, wa_ref, wb_ref, gxa_ref, wq_ref, kt_ref, v_ref, wo_ref,
                       gffn_ref, whi_ref, wlo_ref, bias_ref, before_ref,
                       h_out_ref, xn_ref, r_ref, cnt_ref, run_ref):
    @pl.when(pl.program_id(0) == 0)
    def _():
        run_ref[...] = jnp.zeros_like(run_ref)

    h = _mixer_out(a_ref, b_ref, h_ref, wa_ref, wb_ref, slice(None))
    h, = _cross_attention([h], gxa_ref, wq_ref, kt_ref, v_ref, wo_ref)
    h_out_ref[...] = h
    xn = _rms(h, gffn_ref[...])
    xn_ref[...] = _pack_halves(xn)
    r_ref[...] = _route(xn, whi_ref, wlo_ref, bias_ref, before_ref, run_ref)
    cnt_ref[...] = run_ref[...]


def post_mixer(ya, yb, h, wa, wb, g_xa, wq, kt, v, wo, g_ffn, w_hi, w_lo, bias, tm=512):
    m, d = h.shape
    tiles_per_batch = m // kt.shape[0] // tm
    idx = np.arange(tm)
    before = jnp.asarray(idx[:, None] > idx[None, :], BF16)
    rows = lambda w: pl.BlockSpec((tm, w), lambda i: (i, 0))
    const = lambda a: pl.BlockSpec(a.shape, lambda i: (0,) * a.ndim, pipeline_mode=pl.Buffered(1))
    per_batch = lambda a: pl.BlockSpec((None,) + a.shape[1:], lambda i: (i // tiles_per_batch, 0, 0))
    g_xa, g_ffn = g_xa.reshape(1, d), g_ffn.reshape(1, d)
    return pl.pallas_call(
        _post_mixer_kernel,
        grid=(m // tm,),
        in_specs=[rows(ya.shape[1]), rows(yb.shape[1]), rows(d), const(wa), const(wb), const(g_xa), const(wq),
                  per_batch(kt), per_batch(v), const(wo), const(g_ffn), const(w_hi), const(w_lo), const(bias),
                  const(before)],
        out_specs=[rows(d), rows(d // 2), rows(LANES), pl.BlockSpec((1, LANES), lambda i: (0, 0))],
        out_shape=[jax.ShapeDtypeStruct((m, d), F32), jax.ShapeDtypeStruct((m, d // 2), jnp.int32),
                   jax.ShapeDtypeStruct((m, LANES), F32), jax.ShapeDtypeStruct((1, LANES), F32)],
        scratch_shapes=[pltpu.VMEM((1, LANES), F32)],
        compiler_params=_cparams("arbitrary"),
        name="post_mixer",
    )(ya, yb, h, wa, wb, g_xa, wq, kt, v, wo, g_ffn, w_hi, w_lo, bias, before)


def _expert_kernel(beid_ref, valid_ref, first_ref, slot_ref, next_ref, x_ref, wg_hbm, wu_hbm, wd_hbm, o_ref,
                   wg32_ref, wu32_ref, wd32_ref, wgb_ref, wub_ref, wdb_ref, sem_ref, *, layer):
    i = pl.program_id(0)
    valid = valid_ref[i]

    def weight_copies(expert, slot):
        return (pltpu.make_async_copy(wg_hbm.at[layer, expert], wg32_ref.at[slot], sem_ref.at[slot, 0]),
                pltpu.make_async_copy(wu_hbm.at[layer, expert], wu32_ref.at[slot], sem_ref.at[slot, 1]),
                pltpu.make_async_copy(wd_hbm.at[layer, expert], wd32_ref.at[slot], sem_ref.at[slot, 2]))

    @pl.when(i == 0)
    def _():
        for copy in weight_copies(beid_ref[0], 0):
            copy.start()

    @pl.when(first_ref[i] == 1)
    def _():
        slot = slot_ref[i]
        for copy in weight_copies(beid_ref[i], slot):
            copy.wait()
        wgb_ref[...] = wg32_ref[slot].astype(BF16)
        wub_ref[...] = wu32_ref[slot].astype(BF16)
        wdb_ref[...] = wd32_ref[slot].astype(BF16)

        @pl.when(next_ref[i] >= 0)
        def _():
            for copy in weight_copies(next_ref[i], 1 - slot):
                copy.start()

    half = MOE_ROWS // 2

    def ffn(n_halves):
        row = lax.broadcasted_iota(jnp.int32, (half, 2 * x_ref.shape[1]), 0)
        xs = [jnp.where(row + k * half < valid, _unpack_halves(x_ref[k * half:(k + 1) * half, :]), 0.0).astype(BF16)
              for k in range(n_halves)]
        gates = [jnp.dot(x, wgb_ref[...], preferred_element_type=F32) for x in xs]
        ups = [jnp.dot(x, wub_ref[...], preferred_element_type=F32) for x in xs]
        acts = [(_silu(g) * u).astype(BF16) for g, u in zip(gates, ups)]
        for k, act in enumerate(acts):
            o_ref[k * half:(k + 1) * half, :] = _pack_halves(jnp.dot(act, wdb_ref[...], preferred_element_type=F32))

    @pl.when(valid > half)
    def _():
        ffn(2)

    @pl.when((valid > 0) & (valid <= half))
    def _():
        ffn(1)
        o_ref[half:, :] = jnp.zeros((half, o_ref.shape[1]), o_ref.dtype)

    @pl.when(valid == 0)
    def _():
        o_ref[...] = jnp.zeros_like(o_ref)


def moe_experts(blocks, xs, w_gate, w_up, w_down, layer):
    n_slots, packed = xs.shape
    d = 2 * packed
    rows = MOE_ROWS
    ff = w_gate.shape[3]
    grid_spec = pltpu.PrefetchScalarGridSpec(
        num_scalar_prefetch=len(blocks),
        grid=(n_slots // rows,),
        in_specs=[
            pl.BlockSpec((rows, packed), lambda i, *_: (i, 0)),
            pl.BlockSpec(memory_space=pl.ANY),
            pl.BlockSpec(memory_space=pl.ANY),
            pl.BlockSpec(memory_space=pl.ANY),
        ],
        out_specs=pl.BlockSpec((rows, packed), lambda i, *_: (i, 0)),
        scratch_shapes=[pltpu.VMEM((2, d, ff), F32), pltpu.VMEM((2, d, ff), F32), pltpu.VMEM((2, ff, d), F32),
                        pltpu.VMEM((d, ff), BF16), pltpu.VMEM((d, ff), BF16), pltpu.VMEM((ff, d), BF16),
                        pltpu.SemaphoreType.DMA((2, 3))],
    )
    return pl.pallas_call(
        functools.partial(_expert_kernel, layer=layer),
        grid_spec=grid_spec,
        out_shape=jax.ShapeDtypeStruct((n_slots, packed), jnp.int32),
        compiler_params=_cparams("arbitrary"),
        name="moe_experts",
    )(*blocks, xs, w_gate, w_up, w_down)


def _sc_mesh():
    return plsc.VectorSubcoreMesh(core_axis_name="c", subcore_axis_name="s",
                                  num_cores=SC_CORES, num_subcores=SC_SUBCORES)


def _sc_worker():
    return lax.axis_index("s") * SC_CORES + lax.axis_index("c")


def sc_scatter_rows(x, dest0, dest1, n_slots):
    n_tok, d = x.shape
    per_worker = n_tok // SC_WORKERS
    n_chunks = per_worker // SC_CHUNK
    shape3 = (SC_WORKERS, n_chunks, SC_CHUNK)

    @functools.partial(
        pl.kernel, mesh=_sc_mesh(), out_type=jax.ShapeDtypeStruct((n_slots, d), x.dtype),
        scratch_types=[pltpu.VMEM((n_chunks, SC_CHUNK), jnp.int32), pltpu.VMEM((n_chunks, SC_CHUNK), jnp.int32),
                       pltpu.VMEM((SC_CHUNK, d), x.dtype)],
        name="moe_scatter_rows")
    def scatter(x_hbm, d0_hbm, d1_hbm, out_hbm, i0_v, i1_v, rows_v):
        wid = _sc_worker()
        pltpu.sync_copy(d0_hbm.at[wid], i0_v)
        pltpu.sync_copy(d1_hbm.at[wid], i1_v)

        @pl.loop(0, n_chunks)
        def _(j):
            start = pl.multiple_of(wid * per_worker + j * SC_CHUNK, SC_CHUNK)
            pltpu.sync_copy(x_hbm.at[pl.ds(start, SC_CHUNK)], rows_v)
            pltpu.sync_copy(rows_v, out_hbm.at[i0_v.at[j]])
            pltpu.sync_copy(rows_v, out_hbm.at[i1_v.at[j]])

    return scatter(x, dest0.reshape(shape3), dest1.reshape(shape3))


def sc_gather_rows(table, idx):
    n_out = idx.shape[0]
    d = table.shape[1]
    per_worker = n_out // SC_WORKERS
    n_chunks = per_worker // SC_CHUNK

    @functools.partial(
        pl.kernel, mesh=_sc_mesh(), out_type=jax.ShapeDtypeStruct((n_out, d), table.dtype),
        scratch_types=[pltpu.VMEM((n_chunks, SC_CHUNK), jnp.int32), pltpu.VMEM((SC_CHUNK, d), table.dtype)],
        name="moe_gather_rows")
    def gather(table_hbm, idx_hbm, out_hbm, idx_v, rows_v):
        wid = _sc_worker()
        pltpu.sync_copy(idx_hbm.at[wid], idx_v)

        @pl.loop(0, n_chunks)
        def _(j):
            start = pl.multiple_of(wid * per_worker + j * SC_CHUNK, SC_CHUNK)
            pltpu.sync_copy(table_hbm.at[idx_v.at[j]], rows_v)
            pltpu.sync_copy(rows_v, out_hbm.at[pl.ds(start, SC_CHUNK)])

    return gather(table, idx.reshape(SC_WORKERS, n_chunks, SC_CHUNK))


def _combine_kernel(h_ref, y0_ref, y1_ref, r_ref, g_ref, o_ref, *, final_norm):
    route = r_ref[...]
    h = h_ref[...] + (route[:, 2:3] * _unpack_halves(y0_ref[...]) + route[:, 3:4] * _unpack_halves(y1_ref[...]))
    o_ref[...] = _rms(h, g_ref[...]) if final_norm else h


def moe_combine(h, y01, route, g, final_norm, tm=512):
    m, d = h.shape
    tm = min(tm, m)
    rows = lambda w: pl.BlockSpec((tm, w), lambda i: (i, 0))
    return pl.pallas_call(
        functools.partial(_combine_kernel, final_norm=final_norm),
        grid=(m // tm,),
        in_specs=[rows(d), rows(d // 2), pl.BlockSpec((tm, d // 2), lambda i: (i + m // tm, 0)), rows(LANES),
                  pl.BlockSpec((1, d), lambda i: (0, 0))],
        out_specs=rows(d),
        out_shape=jax.ShapeDtypeStruct((m, d), F32),
        compiler_params=_cparams("parallel"),
        name="moe_combine",
    )(h, y01, y01, route, g.reshape(1, d))


def _pad_cols(w):
    return jnp.pad(w, ((0, 0), (0, LANES - w.shape[1])))


def _dispatch(route, counts, n_tok):
    rows = MOE_ROWS
    counts = counts[0, MOE_GROUPS:MOE_GROUPS + MOE_EXPERTS].astype(jnp.int32)
    padded = (counts + rows - 1) // rows * rows
    pad_end = jnp.cumsum(padded)
    pad_start = pad_end - padded
    eid = route[:, 0:2].astype(jnp.int32)
    rank = route[:, 4:6].astype(jnp.int32)
    hot = eid[:, :, None] == jnp.arange(MOE_EXPERTS, dtype=jnp.int32)
    dest = jnp.sum(jnp.where(hot, pad_start, 0), axis=-1) + rank
    n_blocks = -(-(2 * n_tok + MOE_EXPERTS * (rows - 1)) // rows)
    block_start = jnp.arange(n_blocks, dtype=jnp.int32) * rows
    block_eid = jnp.minimum(jnp.sum(block_start[:, None] >= pad_end[None, :], axis=-1), MOE_EXPERTS - 1)
    filled = (pad_start + counts)[block_eid]
    block_valid = jnp.clip(filled - block_start, 0, rows)
    used = block_valid > 0
    first = used & jnp.concatenate([jnp.ones((1,), bool), block_eid[1:] != block_eid[:-1]])
    slot = (jnp.cumsum(first) - 1) % 2
    order = jnp.arange(n_blocks, dtype=jnp.int32)
    later_first = lax.cummin(jnp.where(first, order, n_blocks)[::-1])[::-1]
    following = jnp.concatenate([later_first[1:], jnp.full((1,), n_blocks, jnp.int32)])
    block_next = jnp.where(following < n_blocks, block_eid[jnp.minimum(following, n_blocks - 1)], -1)
    blocks = tuple(a.astype(jnp.int32) for a in (block_eid, block_valid, first, slot, block_next))
    return dest[:, 0], dest[:, 1], blocks, n_blocks * rows


def _router_weights(w_group, b_group, w_expert, b_expert):
    w_r = _pad_cols(jnp.concatenate([w_group, w_expert], axis=1))
    w_hi = w_r.astype(BF16)
    w_lo = (w_r - w_hi.astype(F32)).astype(BF16)
    return w_hi, w_lo, _pad_lanes(jnp.concatenate([b_group, b_expert]))


def _moe_layer(h, xn, route, counts, w_gate, w_up, w_down, layer, final_g):
    n_tok, d = h.shape
    dest0, dest1, blocks, n_slots = _dispatch(route, counts, n_tok)
    xs = sc_scatter_rows(xn, dest0, dest1, n_slots)
    ys = moe_experts(blocks, xs, w_gate, w_up, w_down, layer)
    y01 = sc_gather_rows(ys, jnp.concatenate([dest0, dest1]))
    g = jnp.ones((d,), F32) if final_g is None else final_g
    return moe_combine(h, y01, route, g, final_g is not None)


def _memory_kv(memn_in, mem_norm, wk, wv):
    bsz, m, d = memn_in.shape
    w = jnp.concatenate([wk, wv], axis=1).astype(BF16)
    kv, _ = rms_matmul(memn_in.reshape(bsz * m, d), mem_norm, w, jnp.zeros((d, LANES), BF16))
    k = kv[:, :d].reshape(bsz, m, d)
    v = kv[:, d:].reshape(bsz, m, d)
    return jnp.swapaxes(k, 1, 2).astype(BF16), v.astype(BF16)


def kernel(x, mem, mem_norm, final_norm, norm_mix, norm_xa, norm_ffn, xa_wq, xa_wk, xa_wv, xa_wo, moe_w_group, moe_b_group, moe_w_expert, moe_b_expert, moe_w_gate, moe_w_up, moe_w_down, ev_w_in, ev_sc_conv, ev_ssm_conv_w, ev_ssm_conv_b, ev_ssm_dt_bias, ev_ssm_a_log, ev_ssm_d, ev_ssm_norm, ev_w_out, od_w_in, od_gdn_conv, od_gdn_dt_bias, od_gdn_a_log, od_gdn_norm, od_w_out):
    bsz, length, d = x.shape
    n_tok = bsz * length
    depth = norm_mix.shape[0]
    h = x.reshape(n_tok, d)
    for layer in range(depth):
        i = layer // 2
        if layer % 2 == 0:
            w = ev_w_in[i]
            z0 = 3 * SC_DIM
            xbc0 = z0 + SSM_INNER
            w_conv = w[:, xbc0:xbc0 + SSM_XBC].astype(BF16)
            w_main = jnp.concatenate([w[:, z0:xbc0], w[:, :z0]], axis=1).astype(BF16)
            w_small = _pad_cols(w[:, xbc0 + SSM_XBC:]).astype(BF16)
            xbc = rms_matmul_conv(h, norm_mix[layer], w_conv, ev_ssm_conv_w[i], ev_ssm_conv_b[i], length)
            y, small = rms_matmul(h, norm_mix[layer], w_main, w_small, tm=1024, tn=w_main.shape[1])
            y3 = y.reshape(bsz, length, -1)
            small3 = small.reshape(bsz, length, LANES)
            smallt = jnp.swapaxes(small3[:, :, :16], 1, 2)
            ya = short_conv_mixer(y3, ev_sc_conv[i], SSM_INNER // SC_DIM)
            yb = ssd_mixer(xbc.reshape(bsz, length, -1), y3, small3, smallt, ev_ssm_dt_bias[i],
                           ev_ssm_a_log[i], ev_ssm_d[i], ev_ssm_norm[i])
            w_out = ev_w_out[i].astype(BF16)
            split = SC_DIM
        else:
            w = od_w_in[i]
            qkv_w = 3 * GDN_HEADS * GDN_D
            z_end = qkv_w + GDN_HEADS * GDN_D
            w_conv = w[:, :qkv_w].astype(BF16)
            w_main = jnp.concatenate([w[:, qkv_w:z_end], w[:, z_end + 2 * GDN_HEADS:]], axis=1).astype(BF16)
            w_small = _pad_cols(w[:, z_end:z_end + 2 * GDN_HEADS]).astype(BF16)
            qkv = rms_matmul_conv(h, norm_mix[layer], w_conv, od_gdn_conv[i], jnp.zeros((qkv_w,), F32), length)
            y, small = rms_matmul(h, norm_mix[layer], w_main, w_small, tm=1024, tn=w_main.shape[1])
            y3 = y.reshape(bsz, length, -1)
            small3 = small.reshape(bsz, length, LANES)
            smallt = jnp.swapaxes(small3[:, :, :16], 1, 2)
            ya = gated_deltanet_mixer(qkv.reshape(bsz, length, -1), y3, small3, smallt, od_gdn_dt_bias[i],
                                      od_gdn_a_log[i], od_gdn_norm[i])
            yb = stick_breaking_mixer(y3, GDN_HEADS * GDN_D)
            w_out = od_w_out[i].astype(BF16)
            split = GDN_HEADS * GDN_D
        kt, v = _memory_kv(mem, mem_norm, xa_wk[layer], xa_wv[layer])
        w_hi, w_lo, bias = _router_weights(moe_w_group[layer], moe_b_group[layer], moe_w_expert[layer],
                                           moe_b_expert[layer])
        h, xn, route, counts = post_mixer(
            ya.reshape(n_tok, -1), yb.reshape(n_tok, -1), h, w_out[:split], w_out[split:], norm_xa[layer],
            xa_wq[layer].astype(BF16), kt, v, xa_wo[layer].astype(BF16), norm_ffn[layer], w_hi, w_lo, bias)
        h = _moe_layer(h, xn, route, counts, moe_w_gate, moe_w_up, moe_w_down, layer,
                       final_norm if layer == depth - 1 else None)
    return h.reshape(bsz, length, d)
```

```python
import functools

import jax
import jax.numpy as jnp
import numpy as np
from jax import lax
from jax.experimental import pallas as pl
from jax.experimental.pallas import tpu as pltpu
from jax.experimental.pallas import tpu_sc as plsc

F32 = jnp.float32
BF16 = jnp.bfloat16
EPS = 1e-6

D_MODEL = 1024
MEM_LEN = 256
SC_DIM = 512
SSM_HEADS = 16
SSM_HEAD_DIM = 64
SSM_INNER = 1024
SSM_GROUPS = 2
SSM_STATE = 128
SSM_XBC = SSM_INNER + 2 * SSM_GROUPS * SSM_STATE
SSD_CHUNK = 128
GDN_HEADS = 8
GDN_D = 128
GDN_CHUNK = 64
GDN_TILE = 128
SB_HEADS = 8
SB_HEAD_DIM = 64
SB_DIM = 512
SB_BLOCK = 128
SB_STEP_HEADS = 8
XA_HEADS = 4
XA_HEAD_DIM = 256
MOE_GROUPS = 4
MOE_PER_GROUP = 8
MOE_EXPERTS = 32
MOE_FF = 512
MOE_ROWS = 512
SC_CORES = 2
SC_SUBCORES = 16
SC_WORKERS = SC_CORES * SC_SUBCORES
SC_CHUNK = 64
HALO = 8
CONV_CHUNK = 512
LANES = 128
SB_LOG_ZERO = -104.0
VMEM_LIMIT = 56 * 1024 * 1024


def _cparams(*sem):
    return pltpu.CompilerParams(dimension_semantics=sem, vmem_limit_bytes=VMEM_LIMIT)


def _mm(a, b):
    return jnp.dot(a.astype(BF16), b.astype(BF16), preferred_element_type=F32)


def _mm_nt(a, b):
    return lax.dot_general(a.astype(BF16), b.astype(BF16), (((1,), (1,)), ((), ())),
                           preferred_element_type=F32)


def _split_bf16(x, n):
    parts, r = [], x
    for _ in range(n):
        p = r.astype(BF16)
        parts.append(p)
        r = r - p.astype(F32)
    return parts


def _mm_sel_rhs(x, sel, n=3):
    return sum(jnp.dot(p, sel, preferred_element_type=F32) for p in _split_bf16(x, n))


def _mm_sel_lhs(sel, x, n=3):
    return sum(jnp.dot(sel, p, preferred_element_type=F32) for p in _split_bf16(x, n))


def _spread_heads(x, first, n_heads, width):
    rows = x.shape[0]
    col = lambda h: jnp.broadcast_to(x[:, first + h:first + h + 1], (rows, LANES))
    if width == LANES:
        return jnp.concatenate([col(h) for h in range(n_heads)], axis=1)
    left = lax.broadcasted_iota(jnp.int32, (rows, LANES), 1) < width
    return jnp.concatenate([jnp.where(left, col(h), col(h + 1)) for h in range(0, n_heads, 2)], axis=1)


def _pack_halves(x):
    n = x.shape[1] // 2
    lo = pltpu.bitcast(x[:, :n].astype(BF16).astype(F32), jnp.int32)
    hi = pltpu.bitcast(x[:, n:].astype(BF16).astype(F32), jnp.int32)
    return lax.shift_right_logical(lo, 16) | (hi & jnp.int32(-65536))


def _unpack_halves(p):
    lo = pltpu.bitcast(lax.shift_left(p, 16), F32)
    hi = pltpu.bitcast(p & jnp.int32(-65536), F32)
    return jnp.concatenate([lo, hi], axis=1)


def _silu(x):
    return x * jax.nn.sigmoid(x)


def _softplus(x):
    return jnp.maximum(x, 0.0) + jnp.log(1.0 + jnp.exp(-jnp.abs(x)))


def _rms(x, g):
    return x * lax.rsqrt(jnp.mean(x * x, axis=-1, keepdims=True) + EPS) * g


def _rms_matmul_kernel(x_ref, g_ref, w_ref, ws_ref, o_ref, os_ref):
    xn = _rms(x_ref[...], g_ref[...]).astype(BF16)
    o_ref[...] = jnp.dot(xn, w_ref[...], preferred_element_type=F32)
    os_ref[...] = jnp.dot(xn, ws_ref[...], preferred_element_type=F32)


def rms_matmul(x, g, w, ws, tm=512, tn=512):
    m, k = x.shape
    n = w.shape[1]
    tm = min(tm, m)
    main, small = pl.pallas_call(
        _rms_matmul_kernel,
        grid=(n // tn, m // tm),
        in_specs=[
            pl.BlockSpec((tm, k), lambda j, i: (i, 0)),
            pl.BlockSpec((1, k), lambda j, i: (0, 0)),
            pl.BlockSpec((k, tn), lambda j, i: (0, j)),
            pl.BlockSpec((k, LANES), lambda j, i: (0, 0)),
        ],
        out_specs=[
            pl.BlockSpec((tm, tn), lambda j, i: (i, j)),
            pl.BlockSpec((None, tm, LANES), lambda j, i: (j, i, 0)),
        ],
        out_shape=[jax.ShapeDtypeStruct((m, n), F32), jax.ShapeDtypeStruct((n // tn, m, LANES), F32)],
        compiler_params=_cparams("parallel", "parallel"),
        name="rms_matmul",
    )(x, g.reshape(1, k), w, ws)
    return main, small[0]


def _causal_conv(ext_ref, w_ref, rows):
    width = w_ref.shape[0]
    ext = ext_ref[...]
    acc = None
    for j in range(width):
        shift = width - 1 - j
        moved = ext if shift == 0 else pltpu.roll(ext, shift, axis=0)
        term = w_ref[j:j + 1, :] * moved[HALO:HALO + rows, :]
        acc = term if acc is None else acc + term
    return acc


def _rms_matmul_conv_kernel(x_ref, g_ref, w_ref, cw_ref, cb_ref, o_ref, *ext_refs, tiles_per_seq):
    tm = x_ref.shape[0]
    starts_sequence = pl.program_id(1) % tiles_per_seq == 0

    @pl.when(starts_sequence)
    def _():
        for ext_ref in ext_refs:
            ext_ref[0:HALO, :] = jnp.zeros((HALO, CONV_CHUNK), F32)

    @pl.when(jnp.logical_not(starts_sequence))
    def _():
        for ext_ref in ext_refs:
            ext_ref[0:HALO, :] = ext_ref[tm:tm + HALO, :]

    xn = _rms(x_ref[...], g_ref[...]).astype(BF16)
    for c, ext_ref in enumerate(ext_refs):
        cols = slice(c * CONV_CHUNK, (c + 1) * CONV_CHUNK)
        ext_ref[HALO:, :] = jnp.dot(xn, w_ref[:, cols], preferred_element_type=F32)
        o_ref[:, cols] = _silu(_causal_conv(ext_ref, cw_ref.at[:, cols], tm) + cb_ref[:, cols])


def rms_matmul_conv(x, g, w, conv_w, conv_b, seq_len, tm=1024, tn=1536):
    m, k = x.shape
    n = w.shape[1]
    cols = lambda rows: pl.BlockSpec((rows, tn), lambda j, i: (0, j))
    return pl.pallas_call(
        functools.partial(_rms_matmul_conv_kernel, tiles_per_seq=seq_len // tm),
        grid=(n // tn, m // tm),
        in_specs=[
            pl.BlockSpec((tm, k), lambda j, i: (i, 0)),
            pl.BlockSpec((1, k), lambda j, i: (0, 0)),
            cols(k), cols(conv_w.shape[0]), cols(1),
        ],
        out_specs=pl.BlockSpec((tm, tn), lambda j, i: (i, j)),
        out_shape=jax.ShapeDtypeStruct((m, n), F32),
        scratch_shapes=[pltpu.VMEM((tm + HALO, CONV_CHUNK), F32)] * (tn // CONV_CHUNK),
        compiler_params=_cparams("arbitrary", "arbitrary"),
        name="rms_matmul_conv",
    )(x, g.reshape(1, k), w, conv_w, conv_b.reshape(1, n))


def _halo_index(rows):
    step = rows // HALO
    return lambda i: jnp.maximum(i * step - 1, 0)


def _sc_kernel(b_ref, c_ref, x_ref, ch_ref, xh_ref, w_ref, o_ref, ext_ref):
    rows = o_ref.shape[0]
    first = pl.program_id(1) == 0
    ext_ref[0:HALO, :] = jnp.where(first, 0.0, ch_ref[...] * xh_ref[...])
    ext_ref[HALO:, :] = c_ref[...] * x_ref[...]
    o_ref[...] = b_ref[...] * _causal_conv(ext_ref, w_ref, rows)


def short_conv_mixer(y3, w, col0, tl=512):
    bsz, length, _ = y3.shape
    tl = min(tl, length)
    hidx = _halo_index(tl)
    return pl.pallas_call(
        _sc_kernel,
        grid=(bsz, length // tl),
        in_specs=[
            pl.BlockSpec((None, tl, SC_DIM), lambda b, i: (b, i, col0)),
            pl.BlockSpec((None, tl, SC_DIM), lambda b, i: (b, i, col0 + 1)),
            pl.BlockSpec((None, tl, SC_DIM), lambda b, i: (b, i, col0 + 2)),
            pl.BlockSpec((None, HALO, SC_DIM), lambda b, i: (b, hidx(i), col0 + 1)),
            pl.BlockSpec((None, HALO, SC_DIM), lambda b, i: (b, hidx(i), col0 + 2)),
            pl.BlockSpec(w.shape, lambda b, i: (0, 0)),
        ],
        out_specs=pl.BlockSpec((None, tl, SC_DIM), lambda b, i: (b, i, 0)),
        out_shape=jax.ShapeDtypeStruct((bsz, length, SC_DIM), F32),
        scratch_shapes=[pltpu.VMEM((tl + HALO, SC_DIM), F32)],
        compiler_params=_cparams("parallel", "arbitrary"),
        name="short_conv_mixer",
    )(y3, y3, y3, y3, y3, w)


def _ssd_kernel(xbc_ref, z_ref, dt_ref, dtt_ref, dtb_r_ref, dtb_c_ref,
                alog_r_ref, alog_c_ref, d_ref, nw_ref, tri_ref, trit_ref,
                o_ref, s_ref):
    q = SSD_CHUNK
    hpg = SSM_HEADS // SSM_GROUPS
    gw = hpg * SSM_HEAD_DIM

    @pl.when(pl.program_id(1) == 0)
    def _():
        s_ref[...] = jnp.zeros_like(s_ref)

    xbc = xbc_ref[...]
    xs = xbc[:, :SSM_INNER]
    bm = xbc[:, SSM_INNER:SSM_INNER + SSM_GROUPS * SSM_STATE]
    cm = xbc[:, SSM_INNER + SSM_GROUPS * SSM_STATE:]

    dt = _softplus(dt_ref[...] + dtb_r_ref[...])
    acs = _mm_sel_lhs(tri_ref[...], dt * -jnp.exp(alog_r_ref[...]))
    dtt = _softplus(dtt_ref[...] + dtb_c_ref[...])
    acst = _mm_sel_rhs(dtt * -jnp.exp(alog_c_ref[...]), trit_ref[...])
    dt_full = _spread_heads(dt, 0, SSM_HEADS, SSM_HEAD_DIM)
    acs_full = _spread_heads(acs, 0, SSM_HEADS, SSM_HEAD_DIM)
    acs_col = _spread_heads(acs, 0, SSM_HEADS, q)

    xdt = xs * dt_full
    acs_last = acs_full[q - 1:q, :]
    xw = xdt * jnp.exp(acs_last - acs_full)
    chunk_decay = jnp.exp(acs_last)

    row = lax.broadcasted_iota(jnp.int32, (q, q), 0)
    col = lax.broadcasted_iota(jnp.int32, (q, q), 1)
    causal = row >= col
    lane = lax.broadcasted_iota(jnp.int32, (q, 2 * SSM_HEAD_DIM), 1)

    y_diag, y_off = [], []
    for g in range(SSM_GROUPS):
        bm_g = bm[:, g * SSM_STATE:(g + 1) * SSM_STATE]
        cm_g = cm[:, g * SSM_STATE:(g + 1) * SSM_STATE]
        cb_g = _mm_nt(cm_g, bm_g)
        state = s_ref[g]
        y_off.append(_mm(cm_g, state))
        s_ref[g] = state * chunk_decay[:, g * gw:(g + 1) * gw] + _mm(bm_g.T, xw[:, g * gw:(g + 1) * gw])
        for pair in range(hpg // 2):
            h0 = g * hpg + 2 * pair
            xdt_pair = xdt[:, h0 * SSM_HEAD_DIM:(h0 + 2) * SSM_HEAD_DIM]
            outs = []
            for h in (h0, h0 + 1):
                seg = acs_col[:, h * q:(h + 1) * q] - acst[h:h + 1, :]
                decay = jnp.where(causal, jnp.exp(seg), 0.0)
                outs.append(_mm(cb_g * decay, xdt_pair))
            y_diag.append(jnp.where(lane < SSM_HEAD_DIM, outs[0], outs[1]))
    y = (jnp.concatenate(y_diag, axis=1) + jnp.concatenate(y_off, axis=1) * jnp.exp(acs_full)
         + xs * d_ref[...])
    y = y * _silu(z_ref[...])
    halves = []
    for g in range(SSM_GROUPS):
        yg = y[:, g * gw:(g + 1) * gw]
        halves.append(yg * lax.rsqrt(jnp.mean(yg * yg, axis=-1, keepdims=True) + EPS))
    o_ref[...] = jnp.concatenate(halves, axis=1) * nw_ref[...]


def _pad_lanes(v, fill=0.0):
    return jnp.pad(v.astype(F32), (0, LANES - v.shape[0]), constant_values=fill).reshape(1, LANES)


def _pad_col(v, rows=16):
    return jnp.pad(v.astype(F32), (0, rows - v.shape[0])).reshape(rows, 1)


def ssd_mixer(xbc3, y3, small3, smallt, dt_bias, a_log, d_skip, norm_w):
    bsz, length, _ = y3.shape
    q = SSD_CHUNK
    tri = jnp.asarray(np.tril(np.ones((q, q), np.float32)), BF16)
    trit = jnp.asarray(np.triu(np.ones((q, q), np.float32)), BF16)
    d_full = jnp.repeat(d_skip.astype(F32), SSM_HEAD_DIM).reshape(1, SSM_INNER)
    const = lambda a: pl.BlockSpec(a.shape, lambda b, c: (0,) * a.ndim)
    args = [_pad_lanes(dt_bias), _pad_col(dt_bias), _pad_lanes(a_log),
            _pad_col(a_log), d_full, norm_w.reshape(1, -1), tri, trit]
    return pl.pallas_call(
        _ssd_kernel,
        grid=(bsz, length // q),
        in_specs=[
            pl.BlockSpec((None, q, SSM_XBC), lambda b, c: (b, c, 0)),
            pl.BlockSpec((None, q, SSM_INNER), lambda b, c: (b, c, 0)),
            pl.BlockSpec((None, q, LANES), lambda b, c: (b, c, 0)),
            pl.BlockSpec((None, 16, q), lambda b, c: (b, 0, c)),
        ] + [const(a) for a in args],
        out_specs=pl.BlockSpec((None, q, SSM_INNER), lambda b, c: (b, c, 0)),
        out_shape=jax.ShapeDtypeStruct((bsz, length, SSM_INNER), F32),
        scratch_shapes=[pltpu.VMEM((SSM_GROUPS, SSM_STATE, SSM_INNER // SSM_GROUPS), F32)],
        compiler_params=_cparams("parallel", "arbitrary"),
        name="ssd_mixer",
    )(xbc3, y3, small3, smallt, *args)


def _unit_lower_inverse(mats, row, col):
    eye = jnp.where(row == col, 1.0, 0.0)
    blk = lambda n: (row >> (n.bit_length() - 1)) == (col >> (n.bit_length() - 1))
    p = [jnp.where(blk(16), -a, 0.0) for a in mats]
    t = [eye + x for x in p]
    for _ in range(3):
        p = [_mm(x, x) for x in p]
        t = [y + _mm(y, x) for y, x in zip(t, p)]
    for n in (16, 32):
        band = blk(2 * n) & jnp.logical_not(blk(n))
        left = [_mm(y, jnp.where(band, a, 0.0)) for y, a in zip(t, mats)]
        t = [y - _mm(x, y) for y, x in zip(t, left)]
    return t


def _gdn_kernel(qkv_ref, z_ref, ab_ref, abt_ref, dtb_r_ref, dtb_c_ref, alog_r_ref,
                alog_c_ref, nw_ref, tri_ref, trit_ref, o_ref, s_ref):
    n = GDN_TILE
    c = GDN_CHUNK
    d = GDN_D
    hd = GDN_HEADS * d

    @pl.when(pl.program_id(1) == 0)
    def _():
        s_ref[...] = jnp.zeros_like(s_ref)

    qkv = qkv_ref[...]
    z = z_ref[...]

    ab = ab_ref[...]
    g = -jnp.exp(alog_r_ref[...]) * _softplus(ab + dtb_r_ref[...])
    gc_full = _spread_heads(_mm_sel_lhs(tri_ref[...], g), 0, GDN_HEADS, d)
    beta_full = _spread_heads(jax.nn.sigmoid(ab), GDN_HEADS, GDN_HEADS, d)
    gt = -jnp.exp(alog_c_ref[...]) * _softplus(abt_ref[...] + dtb_c_ref[...])
    gct = _mm_sel_rhs(gt, trit_ref[...])

    row = lax.broadcasted_iota(jnp.int32, (n, n), 0)
    col = lax.broadcasted_iota(jnp.int32, (n, n), 1)
    same = (row >> (c.bit_length() - 1)) == (col >> (c.bit_length() - 1))
    incl = same & (row >= col)
    strict = same & (row > col)
    zeros_half = jnp.zeros((c, d), F32)

    heads = range(GDN_HEADS)
    sl = [slice(h * d, (h + 1) * d) for h in heads]
    l2n = lambda x: x * lax.rsqrt(jnp.sum(x * x, axis=-1, keepdims=True) + EPS)
    qn = [l2n(qkv[:, sl[h]]) * (d ** -0.5) for h in heads]
    kn = [l2n(qkv[:, hd + h * d:hd + (h + 1) * d]) for h in heads]
    vh = [qkv[:, 2 * hd + h * d:2 * hd + (h + 1) * d] for h in heads]
    gcol = [gc_full[:, sl[h]] for h in heads]
    beta = [beta_full[:, sl[h]] for h in heads]
    edec = [jnp.exp(gcol[h] - gct[h:h + 1, :]) for h in heads]
    egc = [jnp.exp(x) for x in gcol]
    kb = [kn[h] * beta[h] for h in heads]
    lower = [jnp.where(strict, _mm_nt(kb[h], kn[h]) * edec[h], 0.0) for h in heads]
    aqk = [jnp.where(incl, _mm_nt(qn[h], kn[h]) * edec[h], 0.0) for h in heads]
    tinv = _unit_lower_inverse(lower, row, col)
    sol = [_mm(tinv[h], jnp.concatenate([vh[h] * beta[h], kb[h] * egc[h]], axis=1)) for h in heads]
    qd = [qn[h] * egc[h] for h in heads]
    glast = [(gcol[h][c - 1:c, :], gcol[h][n - 1:n, :]) for h in heads]
    kdt = [(kn[h] * jnp.exp(jnp.concatenate([jnp.broadcast_to(glast[h][0], (c, d)),
                                             jnp.broadcast_to(glast[h][1], (c, d))], axis=0) - gcol[h])).T
           for h in heads]
    s0 = [s_ref[h] for h in heads]
    v0 = [sol[h][:c, :d] - _mm(sol[h][:c, d:], s0[h]) for h in heads]
    s1 = [s0[h] * jnp.exp(glast[h][0]) + _mm(kdt[h], jnp.concatenate([v0[h], zeros_half], axis=0)) for h in heads]
    v1 = [sol[h][c:, :d] - _mm(sol[h][c:, d:], s1[h]) for h in heads]
    for h in heads:
        s_ref[h] = s1[h] * jnp.exp(glast[h][1]) + _mm(kdt[h], jnp.concatenate([zeros_half, v1[h]], axis=0))
    outs = []
    for h in heads:
        o = (jnp.concatenate([_mm(qd[h][:c], s0[h]), _mm(qd[h][c:], s1[h])], axis=0)
             + _mm(aqk[h], jnp.concatenate([v0[h], v1[h]], axis=0)))
        o = o * lax.rsqrt(jnp.mean(o * o, axis=-1, keepdims=True) + EPS) * nw_ref[...]
        outs.append(o * _silu(z[:, sl[h]]))
    o_ref[...] = jnp.concatenate(outs, axis=1)


def gated_deltanet_mixer(qkv3, y3, small3, smallt, dt_bias, a_log, norm_w):
    bsz, length, _ = y3.shape
    n = GDN_TILE
    hd = GDN_HEADS * GDN_D
    idx = np.arange(n)
    same = (idx[:, None] // GDN_CHUNK) == (idx[None, :] // GDN_CHUNK)
    tri = jnp.asarray(same & (idx[:, None] >= idx[None, :]), BF16)
    trit = jnp.asarray(same & (idx[:, None] <= idx[None, :]), BF16)
    const = lambda a: pl.BlockSpec(a.shape, lambda b, c: (0,) * a.ndim)
    args = [_pad_lanes(dt_bias), _pad_col(dt_bias), _pad_lanes(a_log), _pad_col(a_log),
            norm_w.reshape(1, -1), tri, trit]
    return pl.pallas_call(
        _gdn_kernel,
        grid=(bsz, length // n),
        in_specs=[
            pl.BlockSpec((None, n, 3 * hd), lambda b, c: (b, c, 0)),
            pl.BlockSpec((None, n, hd), lambda b, c: (b, c, 0)),
            pl.BlockSpec((None, n, LANES), lambda b, c: (b, c, 0)),
            pl.BlockSpec((None, 16, n), lambda b, c: (b, 0, c)),
        ] + [const(a) for a in args],
        out_specs=pl.BlockSpec((None, n, hd), lambda b, c: (b, c, 0)),
        out_shape=jax.ShapeDtypeStruct((bsz, length, hd), F32),
        scratch_shapes=[pltpu.VMEM((GDN_HEADS, GDN_D, GDN_D), F32)],
        compiler_params=_cparams("parallel", "arbitrary"),
        name="gated_deltanet",
    )(qkv3, y3, small3, smallt, *args)


def _sb_kernel(q_ref, k_ref, v_ref, upper_ref, o_ref):
    blk = SB_BLOCK
    pair_w = 2 * SB_HEAD_DIM
    n_pairs = SB_STEP_HEADS // 2
    i = pl.program_id(2)
    q = q_ref[...] * (SB_HEAD_DIM ** -0.5)
    lane = lax.broadcasted_iota(jnp.int32, (blk, pair_w), 1)
    first_head = lane < SB_HEAD_DIM
    qs = []
    for p in range(n_pairs):
        q2 = q[:, p * pair_w:(p + 1) * pair_w]
        qs += [jnp.where(first_head, q2, 0.0).astype(BF16), jnp.where(first_head, 0.0, q2).astype(BF16)]
    row = lax.broadcasted_iota(jnp.int32, (blk, blk), 0)
    col = lax.broadcasted_iota(jnp.int32, (blk, blk), 1)
    earlier = col < row
    upper = upper_ref[...]
    heads = range(SB_STEP_HEADS)

    def key_block(kb, accs, sticks, diagonal):
        start = pl.multiple_of(kb * blk, blk)
        k = k_ref[pl.ds(start, blk), :].astype(BF16)
        v = v_ref[pl.ds(start, blk), :].astype(BF16)
        kp = [k[:, p * pair_w:(p + 1) * pair_w] for p in range(n_pairs)]
        vp = [v[:, p * pair_w:(p + 1) * pair_w] for p in range(n_pairs)]
        logits = [lax.dot_general(qs[h], kp[h // 2], (((1,), (1,)), ((), ())), preferred_element_type=F32)
                  for h in heads]
        log_keep = [-_softplus(x) for x in logits]
        if diagonal:
            log_keep = [jnp.where(earlier, x, 0.0) for x in log_keep]
        between = [_mm_sel_rhs(log_keep[h], upper, 2) + sticks[h] for h in heads]
        w = [jnp.exp(logits[h] + log_keep[h] + between[h]) for h in heads]
        if diagonal:
            w = [jnp.where(earlier, x, 0.0) for x in w]
        pv = [jnp.dot(w[h].astype(BF16), vp[h // 2], preferred_element_type=F32) for h in heads]
        accs = tuple(accs[p] + jnp.where(first_head, pv[2 * p], pv[2 * p + 1]) for p in range(n_pairs))
        sticks = tuple(sticks[h] + jnp.sum(log_keep[h], axis=-1, keepdims=True) for h in heads)
        return accs, sticks

    accs = tuple(jnp.zeros((blk, pair_w), F32) for _ in range(n_pairs))
    sticks = tuple(jnp.zeros((blk, 1), F32) for _ in heads)
    accs, sticks = key_block(i, accs, sticks, True)

    def alive(state):
        kb, _, sticks = state
        longest = sticks[0]
        for s in sticks[1:]:
            longest = jnp.maximum(longest, s)
        return (kb >= 0) & (jnp.max(longest) > SB_LOG_ZERO)

    def body(state):
        kb, accs, sticks = state
        accs, sticks = key_block(kb, accs, sticks, False)
        return kb - 1, accs, sticks

    _, accs, _ = lax.while_loop(alive, body, (i - 1, accs, sticks))
    o_ref[...] = jnp.concatenate(accs, axis=1)


def stick_breaking_mixer(y3, col0):
    bsz, length, _ = y3.shape
    blk = SB_BLOCK
    step_w = SB_STEP_HEADS * SB_HEAD_DIM
    steps = SB_DIM // step_w
    q0 = col0 // step_w
    idx = np.arange(blk)
    upper = jnp.asarray(idx[:, None] > idx[None, :], BF16)
    resident = lambda off: pl.BlockSpec((None, length, step_w), lambda b, p, i: (b, 0, q0 + off + p),
                                        pipeline_mode=pl.Buffered(1))
    return pl.pallas_call(
        _sb_kernel,
        grid=(bsz, steps, length // blk),
        in_specs=[
            pl.BlockSpec((None, blk, step_w), lambda b, p, i: (b, i, q0 + p)),
            resident(steps),
            resident(2 * steps),
            pl.BlockSpec((blk, blk), lambda b, p, i: (0, 0)),
        ],
        out_specs=pl.BlockSpec((None, blk, step_w), lambda b, p, i: (b, i, p)),
        out_shape=jax.ShapeDtypeStruct((bsz, length, SB_DIM), F32),
        compiler_params=_cparams("parallel", "parallel", "arbitrary"),
        name="stick_breaking",
    )(y3, y3, y3, upper)


def _mixer_out(a_ref, b_ref, h_ref, wa_ref, wb_ref, rows):
    return h_ref[rows, :] + (jnp.dot(a_ref[rows, :].astype(BF16), wa_ref[...], preferred_element_type=F32)
                             + jnp.dot(b_ref[rows, :].astype(BF16), wb_ref[...], preferred_element_type=F32))


def _cross_attention(hs, g_ref, wq_ref, kt_ref, v_ref, wo_ref):
    us = [_rms(h, g_ref[...]).astype(BF16) for h in hs]
    qs = [jnp.dot(u, wq_ref[...], preferred_element_type=F32) for u in us]
    heads = [[] for _ in hs]
    for hd in range(XA_HEADS):
        sl = slice(hd * XA_HEAD_DIM, (hd + 1) * XA_HEAD_DIM)
        ss = [jnp.dot(q[:, sl].astype(BF16), kt_ref[sl, :], preferred_element_type=F32) * (XA_HEAD_DIM ** -0.5)
              for q in qs]
        ps = [jnp.exp(s - jnp.max(s, axis=-1, keepdims=True)) for s in ss]
        ps = [p / jnp.sum(p, axis=-1, keepdims=True) for p in ps]
        for k, p in enumerate(ps):
            heads[k].append(jnp.dot(p.astype(BF16), v_ref[:, sl], preferred_element_type=F32))
    os_ = [jnp.concatenate(hk, axis=1).astype(BF16) for hk in heads]
    return [h + jnp.dot(o, wo_ref[...], preferred_element_type=F32) for h, o in zip(hs, os_)]


def _route(xn, whi_ref, wlo_ref, b_ref, before_ref, run_ref):
    x_hi = xn.astype(BF16)
    x_lo = (xn - x_hi.astype(F32)).astype(BF16)
    logits = (jnp.dot(x_hi, whi_ref[...], preferred_element_type=F32)
              + jnp.dot(x_lo, whi_ref[...], preferred_element_type=F32)
              + jnp.dot(x_hi, wlo_ref[...], preferred_element_type=F32) + b_ref[...])
    lane = lax.broadcasted_iota(jnp.int32, logits.shape, 1).astype(F32)
    neg = -1e30
    none = float(LANES)

    def top(vals):
        best = jnp.max(vals, axis=-1, keepdims=True)
        where = jnp.min(jnp.where(vals == best, lane, none), axis=-1, keepdims=True)
        return best, where

    gl = jnp.where(lane < MOE_GROUPS, logits, neg)
    gbest, gsel = top(gl)
    gprob = 1.0 / jnp.sum(jnp.exp(gl - gbest), axis=-1, keepdims=True)
    lo = MOE_GROUPS + gsel * MOE_PER_GROUP
    el = jnp.where((lane >= lo) & (lane < lo + MOE_PER_GROUP), logits, neg)
    m1, i1 = top(el)
    m2, i2 = top(jnp.where(lane == i1, neg, el))
    e = jnp.exp(m2 - m1)
    gate1 = gprob / (1.0 + e)
    gate2 = gprob * e / (1.0 + e)

    hot1 = lane == i1
    hot2 = lane == i2
    one1 = jnp.where(hot1, 1.0, 0.0)
    one2 = jnp.where(hot2, 1.0, 0.0)
    before = before_ref[...]
    prefix1 = jnp.dot(before, one1.astype(BF16), preferred_element_type=F32)
    prefix2 = jnp.dot(before, one2.astype(BF16), preferred_element_type=F32)
    total1 = jnp.sum(one1, axis=0, keepdims=True)
    running = run_ref[...]
    rank1 = jnp.sum(jnp.where(hot1, prefix1 + running, 0.0), axis=-1, keepdims=True)
    rank2 = jnp.sum(jnp.where(hot2, prefix2 + (running + total1), 0.0), axis=-1, keepdims=True)
    running = running + total1 + jnp.sum(one2, axis=0, keepdims=True)
    run_ref[...] = running

    fields = (i1 - MOE_GROUPS, i2 - MOE_GROUPS, gate1, gate2, rank1, rank2)
    out = jnp.zeros_like(logits)
    for k, val in enumerate(fields):
        out = jnp.where(lane == k, val, out)
    return out


def _post_mixer_kernel(a_ref, b_ref, h_ref, wa_ref, wb_ref, gxa_ref, wq_ref, kt_ref, v_ref, wo_ref,
                       gffn_ref, whi_ref, wlo_ref, bias_ref, before_ref,
                       h_out_ref, xn_ref, r_ref, cnt_ref, run_ref):
    @pl.when(pl.program_id(0) == 0)
    def _():
        run_ref[...] = jnp.zeros_like(run_ref)

    h = _mixer_out(a_ref, b_ref, h_ref, wa_ref, wb_ref, slice(None))
    h, = _cross_attention([h], gxa_ref, wq_ref, kt_ref, v_ref, wo_ref)
    h_out_ref[...] = h
    xn = _rms(h, gffn_ref[...])
    xn_ref[...] = _pack_halves(xn)
    r_ref[...] = _route(xn, whi_ref, wlo_ref, bias_ref, before_ref, run_ref)
    cnt_ref[...] = run_ref[...]


def post_mixer(ya, yb, h, wa, wb, g_xa, wq, kt, v, wo, g_ffn, w_hi, w_lo, bias, tm=512):
    m, d = h.shape
    tiles_per_batch = m // kt.shape[0] // tm
    idx = np.arange(tm)
    before = jnp.asarray(idx[:, None] > idx[None, :], BF16)
    rows = lambda w: pl.BlockSpec((tm, w), lambda i: (i, 0))
    const = lambda a: pl.BlockSpec(a.shape, lambda i: (0,) * a.ndim, pipeline_mode=pl.Buffered(1))
    per_batch = lambda a: pl.BlockSpec((None,) + a.shape[1:], lambda i: (i // tiles_per_batch, 0, 0))
    g_xa, g_ffn = g_xa.reshape(1, d), g_ffn.reshape(1, d)
    return pl.pallas_call(
        _post_mixer_kernel,
        grid=(m // tm,),
        in_specs=[rows(ya.shape[1]), rows(yb.shape[1]), rows(d), const(wa), const(wb), const(g_xa), const(wq),
                  per_batch(kt), per_batch(v), const(wo), const(g_ffn), const(w_hi), const(w_lo), const(bias),
                  const(before)],
        out_specs=[rows(d), rows(d // 2), rows(LANES), pl.BlockSpec((1, LANES), lambda i: (0, 0))],
        out_shape=[jax.ShapeDtypeStruct((m, d), F32), jax.ShapeDtypeStruct((m, d // 2), jnp.int32),
                   jax.ShapeDtypeStruct((m, LANES), F32), jax.ShapeDtypeStruct((1, LANES), F32)],
        scratch_shapes=[pltpu.VMEM((1, LANES), F32)],
        compiler_params=_cparams("arbitrary"),
        name="post_mixer",
    )(ya, yb, h, wa, wb, g_xa, wq, kt, v, wo, g_ffn, w_hi, w_lo, bias, before)


def _expert_kernel(beid_ref, valid_ref, first_ref, slot_ref, next_ref, x_ref, wg_hbm, wu_hbm, wd_hbm, o_ref,
                   wg32_ref, wu32_ref, wd32_ref, wgb_ref, wub_ref, wdb_ref, sem_ref, *, layer):
    i = pl.program_id(0)
    valid = valid_ref[i]

    def weight_copies(expert, slot):
        return (pltpu.make_async_copy(wg_hbm.at[layer, expert], wg32_ref.at[slot], sem_ref.at[slot, 0]),
                pltpu.make_async_copy(wu_hbm.at[layer, expert], wu32_ref.at[slot], sem_ref.at[slot, 1]),
                pltpu.make_async_copy(wd_hbm.at[layer, expert], wd32_ref.at[slot], sem_ref.at[slot, 2]))

    @pl.when(i == 0)
    def _():
        for copy in weight_copies(beid_ref[0], 0):
            copy.start()

    @pl.when(first_ref[i] == 1)
    def _():
        slot = slot_ref[i]
        for copy in weight_copies(beid_ref[i], slot):
            copy.wait()
        wgb_ref[...] = wg32_ref[slot].astype(BF16)
        wub_ref[...] = wu32_ref[slot].astype(BF16)
        wdb_ref[...] = wd32_ref[slot].astype(BF16)

        @pl.when(next_ref[i] >= 0)
        def _():
            for copy in weight_copies(next_ref[i], 1 - slot):
                copy.start()

    half = MOE_ROWS // 2

    def ffn(n_halves):
        row = lax.broadcasted_iota(jnp.int32, (half, 2 * x_ref.shape[1]), 0)
        xs = [jnp.where(row + k * half < valid, _unpack_halves(x_ref[k * half:(k + 1) * half, :]), 0.0).astype(BF16)
              for k in range(n_halves)]
        gates = [jnp.dot(x, wgb_ref[...], preferred_element_type=F32) for x in xs]
        ups = [jnp.dot(x, wub_ref[...], preferred_element_type=F32) for x in xs]
        acts = [(_silu(g) * u).astype(BF16) for g, u in zip(gates, ups)]
        for k, act in enumerate(acts):
            o_ref[k * half:(k + 1) * half, :] = _pack_halves(jnp.dot(act, wdb_ref[...], preferred_element_type=F32))

    @pl.when(valid > half)
    def _():
        ffn(2)

    @pl.when((valid > 0) & (valid <= half))
    def _():
        ffn(1)
        o_ref[half:, :] = jnp.zeros((half, o_ref.shape[1]), o_ref.dtype)

    @pl.when(valid == 0)
    def _():
        o_ref[...] = jnp.zeros_like(o_ref)


def moe_experts(blocks, xs, w_gate, w_up, w_down, layer):
    n_slots, packed = xs.shape
    d = 2 * packed
    rows = MOE_ROWS
    ff = w_gate.shape[3]
    grid_spec = pltpu.PrefetchScalarGridSpec(
        num_scalar_prefetch=len(blocks),
        grid=(n_slots // rows,),
        in_specs=[
            pl.BlockSpec((rows, packed), lambda i, *_: (i, 0)),
            pl.BlockSpec(memory_space=pl.ANY),
            pl.BlockSpec(memory_space=pl.ANY),
            pl.BlockSpec(memory_space=pl.ANY),
        ],
        out_specs=pl.BlockSpec((rows, packed), lambda i, *_: (i, 0)),
        scratch_shapes=[pltpu.VMEM((2, d, ff), F32), pltpu.VMEM((2, d, ff), F32), pltpu.VMEM((2, ff, d), F32),
                        pltpu.VMEM((d, ff), BF16), pltpu.VMEM((d, ff), BF16), pltpu.VMEM((ff, d), BF16),
                        pltpu.SemaphoreType.DMA((2, 3))],
    )
    return pl.pallas_call(
        functools.partial(_expert_kernel, layer=layer),
        grid_spec=grid_spec,
        out_shape=jax.ShapeDtypeStruct((n_slots, packed), jnp.int32),
        compiler_params=_cparams("arbitrary"),
        name="moe_experts",
    )(*blocks, xs, w_gate, w_up, w_down)


def _sc_mesh():
    return plsc.VectorSubcoreMesh(core_axis_name="c", subcore_axis_name="s",
                                  num_cores=SC_CORES, num_subcores=SC_SUBCORES)


def _sc_worker():
    return lax.axis_index("s") * SC_CORES + lax.axis_index("c")


def sc_scatter_rows(x, dest0, dest1, n_slots):
    n_tok, d = x.shape
    per_worker = n_tok // SC_WORKERS
    n_chunks = per_worker // SC_CHUNK
    shape3 = (SC_WORKERS, n_chunks, SC_CHUNK)

    @functools.partial(
        pl.kernel, mesh=_sc_mesh(), out_type=jax.ShapeDtypeStruct((n_slots, d), x.dtype),
        scratch_types=[pltpu.VMEM((n_chunks, SC_CHUNK), jnp.int32), pltpu.VMEM((n_chunks, SC_CHUNK), jnp.int32),
                       pltpu.VMEM((SC_CHUNK, d), x.dtype)],
        name="moe_scatter_rows")
    def scatter(x_hbm, d0_hbm, d1_hbm, out_hbm, i0_v, i1_v, rows_v):
        wid = _sc_worker()
        pltpu.sync_copy(d0_hbm.at[wid], i0_v)
        pltpu.sync_copy(d1_hbm.at[wid], i1_v)

        @pl.loop(0, n_chunks)
        def _(j):
            start = pl.multiple_of(wid * per_worker + j * SC_CHUNK, SC_CHUNK)
            pltpu.sync_copy(x_hbm.at[pl.ds(start, SC_CHUNK)], rows_v)
            pltpu.sync_copy(rows_v, out_hbm.at[i0_v.at[j]])
            pltpu.sync_copy(rows_v, out_hbm.at[i1_v.at[j]])

    return scatter(x, dest0.reshape(shape3), dest1.reshape(shape3))


def sc_gather_rows(table, idx):
    n_out = idx.shape[0]
    d = table.shape[1]
    per_worker = n_out // SC_WORKERS
    n_chunks = per_worker // SC_CHUNK

    @functools.partial(
        pl.kernel, mesh=_sc_mesh(), out_type=jax.ShapeDtypeStruct((n_out, d), table.dtype),
        scratch_types=[pltpu.VMEM((n_chunks, SC_CHUNK), jnp.int32), pltpu.VMEM((SC_CHUNK, d), table.dtype)],
        name="moe_gather_rows")
    def gather(table_hbm, idx_hbm, out_hbm, idx_v, rows_v):
        wid = _sc_worker()
        pltpu.sync_copy(idx_hbm.at[wid], idx_v)

        @pl.loop(0, n_chunks)
        def _(j):
            start = pl.multiple_of(wid * per_worker + j * SC_CHUNK, SC_CHUNK)
            pltpu.sync_copy(table_hbm.at[idx_v.at[j]], rows_v)
            pltpu.sync_copy(rows_v, out_hbm.at[pl.ds(start, SC_CHUNK)])

    return gather(table, idx.reshape(SC_WORKERS, n_chunks, SC_CHUNK))


def _combine_kernel(h_ref, y0_ref, y1_ref, r_ref, g_ref, o_ref, *, final_norm):
    route = r_ref[...]
    h = h_ref[...] + (route[:, 2:3] * _unpack_halves(y0_ref[...]) + route[:, 3:4] * _unpack_halves(y1_ref[...]))
    o_ref[...] = _rms(h, g_ref[...]) if final_norm else h


def moe_combine(h, y01, route, g, final_norm, tm=512):
    m, d = h.shape
    tm = min(tm, m)
    rows = lambda w: pl.BlockSpec((tm, w), lambda i: (i, 0))
    return pl.pallas_call(
        functools.partial(_combine_kernel, final_norm=final_norm),
        grid=(m // tm,),
        in_specs=[rows(d), rows(d // 2), pl.BlockSpec((tm, d // 2), lambda i: (i + m // tm, 0)), rows(LANES),
                  pl.BlockSpec((1, d), lambda i: (0, 0))],
        out_specs=rows(d),
        out_shape=jax.ShapeDtypeStruct((m, d), F32),
        compiler_params=_cparams("parallel"),
        name="moe_combine",
    )(h, y01, y01, route, g.reshape(1, d))


def _pad_cols(w):
    return jnp.pad(w, ((0, 0), (0, LANES - w.shape[1])))


def _dispatch(route, counts, n_tok):
    rows = MOE_ROWS
    counts = counts[0, MOE_GROUPS:MOE_GROUPS + MOE_EXPERTS].astype(jnp.int32)
    padded = (counts + rows - 1) // rows * rows
    pad_end = jnp.cumsum(padded)
    pad_start = pad_end - padded
    eid = route[:, 0:2].astype(jnp.int32)
    rank = route[:, 4:6].astype(jnp.int32)
    hot = eid[:, :, None] == jnp.arange(MOE_EXPERTS, dtype=jnp.int32)
    dest = jnp.sum(jnp.where(hot, pad_start, 0), axis=-1) + rank
    n_blocks = -(-(2 * n_tok + MOE_EXPERTS * (rows - 1)) // rows)
    block_start = jnp.arange(n_blocks, dtype=jnp.int32) * rows
    block_eid = jnp.minimum(jnp.sum(block_start[:, None] >= pad_end[None, :], axis=-1), MOE_EXPERTS - 1)
    filled = (pad_start + counts)[block_eid]
    block_valid = jnp.clip(filled - block_start, 0, rows)
    used = block_valid > 0
    first = used & jnp.concatenate([jnp.ones((1,), bool), block_eid[1:] != block_eid[:-1]])
    slot = (jnp.cumsum(first) - 1) % 2
    order = jnp.arange(n_blocks, dtype=jnp.int32)
    later_first = lax.cummin(jnp.where(first, order, n_blocks)[::-1])[::-1]
    following = jnp.concatenate([later_first[1:], jnp.full((1,), n_blocks, jnp.int32)])
    block_next = jnp.where(following < n_blocks, block_eid[jnp.minimum(following, n_blocks - 1)], -1)
    blocks = tuple(a.astype(jnp.int32) for a in (block_eid, block_valid, first, slot, block_next))
    return dest[:, 0], dest[:, 1], blocks, n_blocks * rows


def _router_weights(w_group, b_group, w_expert, b_expert):
    w_r = _pad_cols(jnp.concatenate([w_group, w_expert], axis=1))
    w_hi = w_r.astype(BF16)
    w_lo = (w_r - w_hi.astype(F32)).astype(BF16)
    return w_hi, w_lo, _pad_lanes(jnp.concatenate([b_group, b_expert]))


def _moe_layer(h, xn, route, counts, w_gate, w_up, w_down, layer, final_g):
    n_tok, d = h.shape
    dest0, dest1, blocks, n_slots = _dispatch(route, counts, n_tok)
    xs = sc_scatter_rows(xn, dest0, dest1, n_slots)
    ys = moe_experts(blocks, xs, w_gate, w_up, w_down, layer)
    y01 = sc_gather_rows(ys, jnp.concatenate([dest0, dest1]))
    g = jnp.ones((d,), F32) if final_g is None else final_g
    return moe_combine(h, y01, route, g, final_g is not None)


def _memory_kv(memn_in, mem_norm, wk, wv):
    bsz, m, d = memn_in.shape
    w = jnp.concatenate([wk, wv], axis=1).astype(BF16)
    kv, _ = rms_matmul(memn_in.reshape(bsz * m, d), mem_norm, w, jnp.zeros((d, LANES), BF16))
    k = kv[:, :d].reshape(bsz, m, d)
    v = kv[:, d:].reshape(bsz, m, d)
    return jnp.swapaxes(k, 1, 2).astype(BF16), v.astype(BF16)


def kernel(x, mem, mem_norm, final_norm, norm_mix, norm_xa, norm_ffn, xa_wq, xa_wk, xa_wv, xa_wo, moe_w_group, moe_b_group, moe_w_expert, moe_b_expert, moe_w_gate, moe_w_up, moe_w_down, ev_w_in, ev_sc_conv, ev_ssm_conv_w, ev_ssm_conv_b, ev_ssm_dt_bias, ev_ssm_a_log, ev_ssm_d, ev_ssm_norm, ev_w_out, od_w_in, od_gdn_conv, od_gdn_dt_bias, od_gdn_a_log, od_gdn_norm, od_w_out):
    bsz, length, d = x.shape
    n_tok = bsz * length
    depth = norm_mix.shape[0]
    h = x.reshape(n_tok, d)
    for layer in range(depth):
        i = layer // 2
        if layer % 2 == 0:
            w = ev_w_in[i]
            z0 = 3 * SC_DIM
            xbc0 = z0 + SSM_INNER
            w_conv = w[:, xbc0:xbc0 + SSM_XBC].astype(BF16)
            w_main = jnp.concatenate([w[:, z0:xbc0], w[:, :z0]], axis=1).astype(BF16)
            w_small = _pad_cols(w[:, xbc0 + SSM_XBC:]).astype(BF16)
            xbc = rms_matmul_conv(h, norm_mix[layer], w_conv, ev_ssm_conv_w[i], ev_ssm_conv_b[i], length)
            y, small = rms_matmul(h, norm_mix[layer], w_main, w_small, tm=1024, tn=w_main.shape[1])
            y3 = y.reshape(bsz, length, -1)
            small3 = small.reshape(bsz, length, LANES)
            smallt = jnp.swapaxes(small3[:, :, :16], 1, 2)
            ya = short_conv_mixer(y3, ev_sc_conv[i], SSM_INNER // SC_DIM)
            yb = ssd_mixer(xbc.reshape(bsz, length, -1), y3, small3, smallt, ev_ssm_dt_bias[i],
                           ev_ssm_a_log[i], ev_ssm_d[i], ev_ssm_norm[i])
            w_out = ev_w_out[i].astype(BF16)
            split = SC_DIM
        else:
            w = od_w_in[i]
            qkv_w = 3 * GDN_HEADS * GDN_D
            z_end = qkv_w + GDN_HEADS * GDN_D
            w_conv = w[:, :qkv_w].astype(BF16)
            w_main = jnp.concatenate([w[:, qkv_w:z_end], w[:, z_end + 2 * GDN_HEADS:]], axis=1).astype(BF16)
            w_small = _pad_cols(w[:, z_end:z_end + 2 * GDN_HEADS]).astype(BF16)
            qkv = rms_matmul_conv(h, norm_mix[layer], w_conv, od_gdn_conv[i], jnp.zeros((qkv_w,), F32), length)
            y, small = rms_matmul(h, norm_mix[layer], w_main, w_small, tm=1024, tn=w_main.shape[1])
            y3 = y.reshape(bsz, length, -1)
            small3 = small.reshape(bsz, length, LANES)
            smallt = jnp.swapaxes(small3[:, :, :16], 1, 2)
            ya = gated_deltanet_mixer(qkv.reshape(bsz, length, -1), y3, small3, smallt, od_gdn_dt_bias[i],
                                      od_gdn_a_log[i], od_gdn_norm[i])
            yb = stick_breaking_mixer(y3, GDN_HEADS * GDN_D)
            w_out = od_w_out[i].astype(BF16)
            split = GDN_HEADS * GDN_D
        kt, v = _memory_kv(mem, mem_norm, xa_wk[layer], xa_wv[layer])
        w_hi, w_lo, bias = _router_weights(moe_w_group[layer], moe_b_group[layer], moe_w_expert[layer],
                                           moe_b_expert[layer])
        h, xn, route, counts = post_mixer(
            ya.reshape(n_tok, -1), yb.reshape(n_tok, -1), h, w_out[:split], w_out[split:], norm_xa[layer],
            xa_wq[layer].astype(BF16), kt, v, xa_wo[layer].astype(BF16), norm_ffn[layer], w_hi, w_lo, bias)
        h = _moe_layer(h, xn, route, counts, moe_w_gate, moe_w_up, moe_w_down, layer,
                       final_norm if layer == depth - 1 else None)
    return h.reshape(bsz, length, d)
```

```python
import functools

import jax
import jax.numpy as jnp
import numpy as np
from jax import lax
from jax.experimental import pallas as pl
from jax.experimental.pallas import tpu as pltpu
from jax.experimental.pallas import tpu_sc as plsc

F32 = jnp.float32
BF16 = jnp.bfloat16
EPS = 1e-6

D_MODEL = 1024
MEM_LEN = 256
SC_DIM = 512
SSM_HEADS = 16
SSM_HEAD_DIM = 64
SSM_INNER = 1024
SSM_GROUPS = 2
SSM_STATE = 128
SSM_XBC = SSM_INNER + 2 * SSM_GROUPS * SSM_STATE
SSD_CHUNK = 128
GDN_HEADS = 8
GDN_D = 128
GDN_CHUNK = 64
GDN_TILE = 128
SB_HEADS = 8
SB_HEAD_DIM = 64
SB_DIM = 512
SB_BLOCK = 128
SB_STEP_HEADS = 8
XA_HEADS = 4
XA_HEAD_DIM = 256
MOE_GROUPS = 4
MOE_PER_GROUP = 8
MOE_EXPERTS = 32
MOE_FF = 512
MOE_ROWS = 512
SC_CORES = 2
SC_SUBCORES = 16
SC_WORKERS = SC_CORES * SC_SUBCORES
SC_CHUNK = 64
HALO = 8
CONV_CHUNK = 512
LANES = 128
SB_LOG_ZERO = -104.0
VMEM_LIMIT = 56 * 1024 * 1024


def _cparams(*sem):
    return pltpu.CompilerParams(dimension_semantics=sem, vmem_limit_bytes=VMEM_LIMIT)


def _mm(a, b):
    return jnp.dot(a.astype(BF16), b.astype(BF16), preferred_element_type=F32)


def _mm_nt(a, b):
    return lax.dot_general(a.astype(BF16), b.astype(BF16), (((1,), (1,)), ((), ())),
                           preferred_element_type=F32)


def _split_bf16(x, n):
    parts, r = [], x
    for _ in range(n):
        p = r.astype(BF16)
        parts.append(p)
        r = r - p.astype(F32)
    return parts


def _mm_sel_rhs(x, sel, n=3):
    return sum(jnp.dot(p, sel, preferred_element_type=F32) for p in _split_bf16(x, n))


def _mm_sel_lhs(sel, x, n=3):
    return sum(jnp.dot(sel, p, preferred_element_type=F32) for p in _split_bf16(x, n))


def _spread_heads(x, first, n_heads, width):
    rows = x.shape[0]
    col = lambda h: jnp.broadcast_to(x[:, first + h:first + h + 1], (rows, LANES))
    if width == LANES:
        return jnp.concatenate([col(h) for h in range(n_heads)], axis=1)
    left = lax.broadcasted_iota(jnp.int32, (rows, LANES), 1) < width
    return jnp.concatenate([jnp.where(left, col(h), col(h + 1)) for h in range(0, n_heads, 2)], axis=1)


def _pack_halves(x):
    n = x.shape[1] // 2
    lo = pltpu.bitcast(x[:, :n].astype(BF16).astype(F32), jnp.int32)
    hi = pltpu.bitcast(x[:, n:].astype(BF16).astype(F32), jnp.int32)
    return lax.shift_right_logical(lo, 16) | (hi & jnp.int32(-65536))


def _unpack_halves(p):
    lo = pltpu.bitcast(lax.shift_left(p, 16), F32)
    hi = pltpu.bitcast(p & jnp.int32(-65536), F32)
    return jnp.concatenate([lo, hi], axis=1)


def _silu(x):
    return x * jax.nn.sigmoid(x)


def _softplus(x):
    return jnp.maximum(x, 0.0) + jnp.log(1.0 + jnp.exp(-jnp.abs(x)))


def _rms(x, g):
    return x * lax.rsqrt(jnp.mean(x * x, axis=-1, keepdims=True) + EPS) * g


def _rms_matmul_kernel(x_ref, g_ref, w_ref, ws_ref, o_ref, os_ref):
    xn = _rms(x_ref[...], g_ref[...]).astype(BF16)
    o_ref[...] = jnp.dot(xn, w_ref[...], preferred_element_type=F32)
    os_ref[...] = jnp.dot(xn, ws_ref[...], preferred_element_type=F32)


def rms_matmul(x, g, w, ws, tm=512, tn=512):
    m, k = x.shape
    n = w.shape[1]
    tm = min(tm, m)
    main, small = pl.pallas_call(
        _rms_matmul_kernel,
        grid=(n // tn, m // tm),
        in_specs=[
            pl.BlockSpec((tm, k), lambda j, i: (i, 0)),
            pl.BlockSpec((1, k), lambda j, i: (0, 0)),
            pl.BlockSpec((k, tn), lambda j, i: (0, j)),
            pl.BlockSpec((k, LANES), lambda j, i: (0, 0)),
        ],
        out_specs=[
            pl.BlockSpec((tm, tn), lambda j, i: (i, j)),
            pl.BlockSpec((None, tm, LANES), lambda j, i: (j, i, 0)),
        ],
        out_shape=[jax.ShapeDtypeStruct((m, n), F32), jax.ShapeDtypeStruct((n // tn, m, LANES), F32)],
        compiler_params=_cparams("parallel", "parallel"),
        name="rms_matmul",
    )(x, g.reshape(1, k), w, ws)
    return main, small[0]


def _causal_conv(ext_ref, w_ref, rows):
    width = w_ref.shape[0]
    ext = ext_ref[...]
    acc = None
    for j in range(width):
        shift = width - 1 - j
        moved = ext if shift == 0 else pltpu.roll(ext, shift, axis=0)
        term = w_ref[j:j + 1, :] * moved[HALO:HALO + rows, :]
        acc = term if acc is None else acc + term
    return acc


def _rms_matmul_conv_kernel(x_ref, g_ref, w_ref, cw_ref, cb_ref, o_ref, *ext_refs, tiles_per_seq):
    tm = x_ref.shape[0]
    starts_sequence = pl.program_id(1) % tiles_per_seq == 0

    @pl.when(starts_sequence)
    def _():
        for ext_ref in ext_refs:
            ext_ref[0:HALO, :] = jnp.zeros((HALO, CONV_CHUNK), F32)

    @pl.when(jnp.logical_not(starts_sequence))
    def _():
        for ext_ref in ext_refs:
            ext_ref[0:HALO, :] = ext_ref[tm:tm + HALO, :]

    xn = _rms(x_ref[...], g_ref[...]).astype(BF16)
    for c, ext_ref in enumerate(ext_refs):
        cols = slice(c * CONV_CHUNK, (c + 1) * CONV_CHUNK)
        ext_ref[HALO:, :] = jnp.dot(xn, w_ref[:, cols], preferred_element_type=F32)
        o_ref[:, cols] = _silu(_causal_conv(ext_ref, cw_ref.at[:, cols], tm) + cb_ref[:, cols])


def rms_matmul_conv(x, g, w, conv_w, conv_b, seq_len, tm=1024, tn=1536):
    m, k = x.shape
    n = w.shape[1]
    cols = lambda rows: pl.BlockSpec((rows, tn), lambda j, i: (0, j))
    return pl.pallas_call(
        functools.partial(_rms_matmul_conv_kernel, tiles_per_seq=seq_len // tm),
        grid=(n // tn, m // tm),
        in_specs=[
            pl.BlockSpec((tm, k), lambda j, i: (i, 0)),
            pl.BlockSpec((1, k), lambda j, i: (0, 0)),
            cols(k), cols(conv_w.shape[0]), cols(1),
        ],
        out_specs=pl.BlockSpec((tm, tn), lambda j, i: (i, j)),
        out_shape=jax.ShapeDtypeStruct((m, n), F32),
        scratch_shapes=[pltpu.VMEM((tm + HALO, CONV_CHUNK), F32)] * (tn // CONV_CHUNK),
        compiler_params=_cparams("arbitrary", "arbitrary"),
        name="rms_matmul_conv",
    )(x, g.reshape(1, k), w, conv_w, conv_b.reshape(1, n))


def _halo_index(rows):
    step = rows // HALO
    return lambda i: jnp.maximum(i * step - 1, 0)


def _sc_kernel(b_ref, c_ref, x_ref, ch_ref, xh_ref, w_ref, o_ref, ext_ref):
    rows = o_ref.shape[0]
    first = pl.program_id(1) == 0
    ext_ref[0:HALO, :] = jnp.where(first, 0.0, ch_ref[...] * xh_ref[...])
    ext_ref[HALO:, :] = c_ref[...] * x_ref[...]
    o_ref[...] = b_ref[...] * _causal_conv(ext_ref, w_ref, rows)


def short_conv_mixer(y3, w, col0, tl=1024):
    bsz, length, _ = y3.shape
    tl = min(tl, length)
    hidx = _halo_index(tl)
    return pl.pallas_call(
        _sc_kernel,
        grid=(bsz, length // tl),
        in_specs=[
            pl.BlockSpec((None, tl, SC_DIM), lambda b, i: (b, i, col0)),
            pl.BlockSpec((None, tl, SC_DIM), lambda b, i: (b, i, col0 + 1)),
            pl.BlockSpec((None, tl, SC_DIM), lambda b, i: (b, i, col0 + 2)),
            pl.BlockSpec((None, HALO, SC_DIM), lambda b, i: (b, hidx(i), col0 + 1)),
            pl.BlockSpec((None, HALO, SC_DIM), lambda b, i: (b, hidx(i), col0 + 2)),
            pl.BlockSpec(w.shape, lambda b, i: (0, 0)),
        ],
        out_specs=pl.BlockSpec((None, tl, SC_DIM), lambda b, i: (b, i, 0)),
        out_shape=jax.ShapeDtypeStruct((bsz, length, SC_DIM), F32),
        scratch_shapes=[pltpu.VMEM((tl + HALO, SC_DIM), F32)],
        compiler_params=_cparams("parallel", "arbitrary"),
        name="short_conv_mixer",
    )(y3, y3, y3, y3, y3, w)


def _ssd_kernel(xbc_ref, z_ref, dt_ref, dtt_ref, dtb_r_ref, dtb_c_ref,
                alog_r_ref, alog_c_ref, d_ref, nw_ref, tri_ref, trit_ref,
                o_ref, s_ref):
    q = SSD_CHUNK
    hpg = SSM_HEADS // SSM_GROUPS
    gw = hpg * SSM_HEAD_DIM

    @pl.when(pl.program_id(1) == 0)
    def _():
        s_ref[...] = jnp.zeros_like(s_ref)

    xbc = xbc_ref[...]
    xs = xbc[:, :SSM_INNER]
    bm = xbc[:, SSM_INNER:SSM_INNER + SSM_GROUPS * SSM_STATE]
    cm = xbc[:, SSM_INNER + SSM_GROUPS * SSM_STATE:]

    dt = _softplus(dt_ref[...] + dtb_r_ref[...])
    acs = _mm_sel_lhs(tri_ref[...], dt * -jnp.exp(alog_r_ref[...]))
    dtt = _softplus(dtt_ref[...] + dtb_c_ref[...])
    acst = _mm_sel_rhs(dtt * -jnp.exp(alog_c_ref[...]), trit_ref[...])
    dt_full = _spread_heads(dt, 0, SSM_HEADS, SSM_HEAD_DIM)
    acs_full = _spread_heads(acs, 0, SSM_HEADS, SSM_HEAD_DIM)
    acs_col = _spread_heads(acs, 0, SSM_HEADS, q)

    xdt = xs * dt_full
    acs_last = acs_full[q - 1:q, :]
    xw = xdt * jnp.exp(acs_last - acs_full)
    chunk_decay = jnp.exp(acs_last)

    row = lax.broadcasted_iota(jnp.int32, (q, q), 0)
    col = lax.broadcasted_iota(jnp.int32, (q, q), 1)
    causal = row >= col
    lane = lax.broadcasted_iota(jnp.int32, (q, 2 * SSM_HEAD_DIM), 1)

    y_diag, y_off = [], []
    for g in range(SSM_GROUPS):
        bm_g = bm[:, g * SSM_STATE:(g + 1) * SSM_STATE]
        cm_g = cm[:, g * SSM_STATE:(g + 1) * SSM_STATE]
        cb_g = _mm_nt(cm_g, bm_g)
        state = s_ref[g]
        y_off.append(_mm(cm_g, state))
        s_ref[g] = state * chunk_decay[:, g * gw:(g + 1) * gw] + _mm(bm_g.T, xw[:, g * gw:(g + 1) * gw])
        for pair in range(hpg // 2):
            h0 = g * hpg + 2 * pair
            xdt_pair = xdt[:, h0 * SSM_HEAD_DIM:(h0 + 2) * SSM_HEAD_DIM]
            outs = []
            for h in (h0, h0 + 1):
                seg = acs_col[:, h * q:(h + 1) * q] - acst[h:h + 1, :]
                decay = jnp.where(causal, jnp.exp(seg), 0.0)
                outs.append(_mm(cb_g * decay, xdt_pair))
            y_diag.append(jnp.where(lane < SSM_HEAD_DIM, outs[0], outs[1]))
    y = (jnp.concatenate(y_diag, axis=1) + jnp.concatenate(y_off, axis=1) * jnp.exp(acs_full)
         + xs * d_ref[...])
    y = y * _silu(z_ref[...])
    halves = []
    for g in range(SSM_GROUPS):
        yg = y[:, g * gw:(g + 1) * gw]
        halves.append(yg * lax.rsqrt(jnp.mean(yg * yg, axis=-1, keepdims=True) + EPS))
    o_ref[...] = jnp.concatenate(halves, axis=1) * nw_ref[...]


def _pad_lanes(v, fill=0.0):
    return jnp.pad(v.astype(F32), (0, LANES - v.shape[0]), constant_values=fill).reshape(1, LANES)


def _pad_col(v, rows=16):
    return jnp.pad(v.astype(F32), (0, rows - v.shape[0])).reshape(rows, 1)


def ssd_mixer(xbc3, y3, small3, smallt, dt_bias, a_log, d_skip, norm_w):
    bsz, length, _ = y3.shape
    q = SSD_CHUNK
    tri = jnp.asarray(np.tril(np.ones((q, q), np.float32)), BF16)
    trit = jnp.asarray(np.triu(np.ones((q, q), np.float32)), BF16)
    d_full = jnp.repeat(d_skip.astype(F32), SSM_HEAD_DIM).reshape(1, SSM_INNER)
    const = lambda a: pl.BlockSpec(a.shape, lambda b, c: (0,) * a.ndim)
    args = [_pad_lanes(dt_bias), _pad_col(dt_bias), _pad_lanes(a_log),
            _pad_col(a_log), d_full, norm_w.reshape(1, -1), tri, trit]
    return pl.pallas_call(
        _ssd_kernel,
        grid=(bsz, length // q),
        in_specs=[
            pl.BlockSpec((None, q, SSM_XBC), lambda b, c: (b, c, 0)),
            pl.BlockSpec((None, q, SSM_INNER), lambda b, c: (b, c, 0)),
            pl.BlockSpec((None, q, LANES), lambda b, c: (b, c, 0)),
            pl.BlockSpec((None, 16, q), lambda b, c: (b, 0, c)),
        ] + [const(a) for a in args],
        out_specs=pl.BlockSpec((None, q, SSM_INNER), lambda b, c: (b, c, 0)),
        out_shape=jax.ShapeDtypeStruct((bsz, length, SSM_INNER), F32),
        scratch_shapes=[pltpu.VMEM((SSM_GROUPS, SSM_STATE, SSM_INNER // SSM_GROUPS), F32)],
        compiler_params=_cparams("parallel", "arbitrary"),
        name="ssd_mixer",
    )(xbc3, y3, small3, smallt, *args)


def _unit_lower_inverse(mats, row, col):
    eye = jnp.where(row == col, 1.0, 0.0)
    blk = lambda n: (row >> (n.bit_length() - 1)) == (col >> (n.bit_length() - 1))
    p = [jnp.where(blk(16), -a, 0.0) for a in mats]
    t = [eye + x for x in p]
    for _ in range(3):
        p = [_mm(x, x) for x in p]
        t = [y + _mm(y, x) for y, x in zip(t, p)]
    for n in (16, 32):
        band = blk(2 * n) & jnp.logical_not(blk(n))
        left = [_mm(y, jnp.where(band, a, 0.0)) for y, a in zip(t, mats)]
        t = [y - _mm(x, y) for y, x in zip(t, left)]
    return t


def _gdn_kernel(qkv_ref, z_ref, ab_ref, abt_ref, dtb_r_ref, dtb_c_ref, alog_r_ref,
                alog_c_ref, nw_ref, tri_ref, trit_ref, o_ref, s_ref):
    n = GDN_TILE
    c = GDN_CHUNK
    d = GDN_D
    hd = GDN_HEADS * d

    @pl.when(pl.program_id(1) == 0)
    def _():
        s_ref[...] = jnp.zeros_like(s_ref)

    qkv = qkv_ref[...]
    z = z_ref[...]

    ab = ab_ref[...]
    g = -jnp.exp(alog_r_ref[...]) * _softplus(ab + dtb_r_ref[...])
    gc_full = _spread_heads(_mm_sel_lhs(tri_ref[...], g), 0, GDN_HEADS, d)
    beta_full = _spread_heads(jax.nn.sigmoid(ab), GDN_HEADS, GDN_HEADS, d)
    gt = -jnp.exp(alog_c_ref[...]) * _softplus(abt_ref[...] + dtb_c_ref[...])
    gct = _mm_sel_rhs(gt, trit_ref[...])

    row = lax.broadcasted_iota(jnp.int32, (n, n), 0)
    col = lax.broadcasted_iota(jnp.int32, (n, n), 1)
    same = (row >> (c.bit_length() - 1)) == (col >> (c.bit_length() - 1))
    incl = same & (row >= col)
    strict = same & (row > col)
    zeros_half = jnp.zeros((c, d), F32)

    heads = range(GDN_HEADS)
    sl = [slice(h * d, (h + 1) * d) for h in heads]
    l2n = lambda x: x * lax.rsqrt(jnp.sum(x * x, axis=-1, keepdims=True) + EPS)
    qn = [l2n(qkv[:, sl[h]]) * (d ** -0.5) for h in heads]
    kn = [l2n(qkv[:, hd + h * d:hd + (h + 1) * d]) for h in heads]
    vh = [qkv[:, 2 * hd + h * d:2 * hd + (h + 1) * d] for h in heads]
    gcol = [gc_full[:, sl[h]] for h in heads]
    beta = [beta_full[:, sl[h]] for h in heads]
    edec = [jnp.exp(gcol[h] - gct[h:h + 1, :]) for h in heads]
    egc = [jnp.exp(x) for x in gcol]
    kb = [kn[h] * beta[h] for h in heads]
    lower = [jnp.where(strict, _mm_nt(kb[h], kn[h]) * edec[h], 0.0) for h in heads]
    aqk = [jnp.where(incl, _mm_nt(qn[h], kn[h]) * edec[h], 0.0) for h in heads]
    tinv = _unit_lower_inverse(lower, row, col)
    sol = [_mm(tinv[h], jnp.concatenate([vh[h] * beta[h], kb[h] * egc[h]], axis=1)) for h in heads]
    qd = [qn[h] * egc[h] for h in heads]
    glast = [(gcol[h][c - 1:c, :], gcol[h][n - 1:n, :]) for h in heads]
    kdt = [(kn[h] * jnp.exp(jnp.concatenate([jnp.broadcast_to(glast[h][0], (c, d)),
                                             jnp.broadcast_to(glast[h][1], (c, d))], axis=0) - gcol[h])).T
           for h in heads]
    s0 = [s_ref[h] for h in heads]
    v0 = [sol[h][:c, :d] - _mm(sol[h][:c, d:], s0[h]) for h in heads]
    s1 = [s0[h] * jnp.exp(glast[h][0]) + _mm(kdt[h], jnp.concatenate([v0[h], zeros_half], axis=0)) for h in heads]
    v1 = [sol[h][c:, :d] - _mm(sol[h][c:, d:], s1[h]) for h in heads]
    for h in heads:
        s_ref[h] = s1[h] * jnp.exp(glast[h][1]) + _mm(kdt[h], jnp.concatenate([zeros_half, v1[h]], axis=0))
    outs = []
    for h in heads:
        o = (jnp.concatenate([_mm(qd[h][:c], s0[h]), _mm(qd[h][c:], s1[h])], axis=0)
             + _mm(aqk[h], jnp.concatenate([v0[h], v1[h]], axis=0)))
        o = o * lax.rsqrt(jnp.mean(o * o, axis=-1, keepdims=True) + EPS) * nw_ref[...]
        outs.append(o * _silu(z[:, sl[h]]))
    o_ref[...] = jnp.concatenate(outs, axis=1)


def gated_deltanet_mixer(qkv3, y3, small3, smallt, dt_bias, a_log, norm_w):
    bsz, length, _ = y3.shape
    n = GDN_TILE
    hd = GDN_HEADS * GDN_D
    idx = np.arange(n)
    same = (idx[:, None] // GDN_CHUNK) == (idx[None, :] // GDN_CHUNK)
    tri = jnp.asarray(same & (idx[:, None] >= idx[None, :]), BF16)
    trit = jnp.asarray(same & (idx[:, None] <= idx[None, :]), BF16)
    const = lambda a: pl.BlockSpec(a.shape, lambda b, c: (0,) * a.ndim)
    args = [_pad_lanes(dt_bias), _pad_col(dt_bias), _pad_lanes(a_log), _pad_col(a_log),
            norm_w.reshape(1, -1), tri, trit]
    return pl.pallas_call(
        _gdn_kernel,
        grid=(bsz, length // n),
        in_specs=[
            pl.BlockSpec((None, n, 3 * hd), lambda b, c: (b, c, 0)),
            pl.BlockSpec((None, n, hd), lambda b, c: (b, c, 0)),
            pl.BlockSpec((None, n, LANES), lambda b, c: (b, c, 0)),
            pl.BlockSpec((None, 16, n), lambda b, c: (b, 0, c)),
        ] + [const(a) for a in args],
        out_specs=pl.BlockSpec((None, n, hd), lambda b, c: (b, c, 0)),
        out_shape=jax.ShapeDtypeStruct((bsz, length, hd), F32),
        scratch_shapes=[pltpu.VMEM((GDN_HEADS, GDN_D, GDN_D), F32)],
        compiler_params=_cparams("parallel", "arbitrary"),
        name="gated_deltanet",
    )(qkv3, y3, small3, smallt, *args)


def _sb_kernel(q_ref, k_ref, v_ref, upper_ref, o_ref):
    blk = SB_BLOCK
    pair_w = 2 * SB_HEAD_DIM
    n_pairs = SB_STEP_HEADS // 2
    i = pl.program_id(2)
    q = q_ref[...] * (SB_HEAD_DIM ** -0.5)
    lane = lax.broadcasted_iota(jnp.int32, (blk, pair_w), 1)
    first_head = lane < SB_HEAD_DIM
    qs = []
    for p in range(n_pairs):
        q2 = q[:, p * pair_w:(p + 1) * pair_w]
        qs += [jnp.where(first_head, q2, 0.0).astype(BF16), jnp.where(first_head, 0.0, q2).astype(BF16)]
    row = lax.broadcasted_iota(jnp.int32, (blk, blk), 0)
    col = lax.broadcasted_iota(jnp.int32, (blk, blk), 1)
    earlier = col < row
    upper = upper_ref[...]
    heads = range(SB_STEP_HEADS)

    def key_block(kb, accs, sticks, diagonal):
        start = pl.multiple_of(kb * blk, blk)
        k = k_ref[pl.ds(start, blk), :].astype(BF16)
        v = v_ref[pl.ds(start, blk), :].astype(BF16)
        kp = [k[:, p * pair_w:(p + 1) * pair_w] for p in range(n_pairs)]
        vp = [v[:, p * pair_w:(p + 1) * pair_w] for p in range(n_pairs)]
        logits = [lax.dot_general(qs[h], kp[h // 2], (((1,), (1,)), ((), ())), preferred_element_type=F32)
                  for h in heads]
        log_keep = [-_softplus(x) for x in logits]
        if diagonal:
            log_keep = [jnp.where(earlier, x, 0.0) for x in log_keep]
        between = [_mm_sel_rhs(log_keep[h], upper, 2) + sticks[h] for h in heads]
        w = [jnp.exp(logits[h] + log_keep[h] + between[h]) for h in heads]
        if diagonal:
            w = [jnp.where(earlier, x, 0.0) for x in w]
        pv = [jnp.dot(w[h].astype(BF16), vp[h // 2], preferred_element_type=F32) for h in heads]
        accs = tuple(accs[p] + jnp.where(first_head, pv[2 * p], pv[2 * p + 1]) for p in range(n_pairs))
        sticks = tuple(sticks[h] + jnp.sum(log_keep[h], axis=-1, keepdims=True) for h in heads)
        return accs, sticks

    accs = tuple(jnp.zeros((blk, pair_w), F32) for _ in range(n_pairs))
    sticks = tuple(jnp.zeros((blk, 1), F32) for _ in heads)
    accs, sticks = key_block(i, accs, sticks, True)

    def alive(state):
        kb, _, sticks = state
        longest = sticks[0]
        for s in sticks[1:]:
            longest = jnp.maximum(longest, s)
        return (kb >= 0) & (jnp.max(longest) > SB_LOG_ZERO)

    def body(state):
        kb, accs, sticks = state
        accs, sticks = key_block(kb, accs, sticks, False)
        return kb - 1, accs, sticks

    _, accs, _ = lax.while_loop(alive, body, (i - 1, accs, sticks))
    o_ref[...] = jnp.concatenate(accs, axis=1)


def stick_breaking_mixer(y3, col0):
    bsz, length, _ = y3.shape
    blk = SB_BLOCK
    step_w = SB_STEP_HEADS * SB_HEAD_DIM
    steps = SB_DIM // step_w
    q0 = col0 // step_w
    idx = np.arange(blk)
    upper = jnp.asarray(idx[:, None] > idx[None, :], BF16)
    resident = lambda off: pl.BlockSpec((None, length, step_w), lambda b, p, i: (b, 0, q0 + off + p),
                                        pipeline_mode=pl.Buffered(1))
    return pl.pallas_call(
        _sb_kernel,
        grid=(bsz, steps, length // blk),
        in_specs=[
            pl.BlockSpec((None, blk, step_w), lambda b, p, i: (b, i, q0 + p)),
            resident(steps),
            resident(2 * steps),
            pl.BlockSpec((blk, blk), lambda b, p, i: (0, 0)),
        ],
        out_specs=pl.BlockSpec((None, blk, step_w), lambda b, p, i: (b, i, p)),
        out_shape=jax.ShapeDtypeStruct((bsz, length, SB_DIM), F32),
        compiler_params=_cparams("parallel", "parallel", "arbitrary"),
        name="stick_breaking",
    )(y3, y3, y3, upper)


def _mixer_out(a_ref, b_ref, h_ref, wa_ref, wb_ref, rows):
    return h_ref[rows, :] + (jnp.dot(a_ref[rows, :].astype(BF16), wa_ref[...], preferred_element_type=F32)
                             + jnp.dot(b_ref[rows, :].astype(BF16), wb_ref[...], preferred_element_type=F32))


def _cross_attention(hs, g_ref, wq_ref, kt_ref, v_ref, wo_ref):
    us = [_rms(h, g_ref[...]).astype(BF16) for h in hs]
    qs = [jnp.dot(u, wq_ref[...], preferred_element_type=F32) for u in us]
    heads = [[] for _ in hs]
    for hd in range(XA_HEADS):
        sl = slice(hd * XA_HEAD_DIM, (hd + 1) * XA_HEAD_DIM)
        ss = [jnp.dot(q[:, sl].astype(BF16), kt_ref[sl, :], preferred_element_type=F32) * (XA_HEAD_DIM ** -0.5)
              for q in qs]
        ps = [jnp.exp(s - jnp.max(s, axis=-1, keepdims=True)) for s in ss]
        ps = [p / jnp.sum(p, axis=-1, keepdims=True) for p in ps]
        for k, p in enumerate(ps):
            heads[k].append(jnp.dot(p.astype(BF16), v_ref[:, sl], preferred_element_type=F32))
    os_ = [jnp.concatenate(hk, axis=1).astype(BF16) for hk in heads]
    return [h + jnp.dot(o, wo_ref[...], preferred_element_type=F32) for h, o in zip(hs, os_)]


def _route(xn, whi_ref, wlo_ref, b_ref, before_ref, run_ref):
    x_hi = xn.astype(BF16)
    x_lo = (xn - x_hi.astype(F32)).astype(BF16)
    logits = (jnp.dot(x_hi, whi_ref[...], preferred_element_type=F32)
              + jnp.dot(x_lo, whi_ref[...], preferred_element_type=F32)
              + jnp.dot(x_hi, wlo_ref[...], preferred_element_type=F32) + b_ref[...])
    lane = lax.broadcasted_iota(jnp.int32, logits.shape, 1).astype(F32)
    neg = -1e30
    none = float(LANES)

    def top(vals):
        best = jnp.max(vals, axis=-1, keepdims=True)
        where = jnp.min(jnp.where(vals == best, lane, none), axis=-1, keepdims=True)
        return best, where

    gl = jnp.where(lane < MOE_GROUPS, logits, neg)
    gbest, gsel = top(gl)
    gprob = 1.0 / jnp.sum(jnp.exp(gl - gbest), axis=-1, keepdims=True)
    lo = MOE_GROUPS + gsel * MOE_PER_GROUP
    el = jnp.where((lane >= lo) & (lane < lo + MOE_PER_GROUP), logits, neg)
    m1, i1 = top(el)
    m2, i2 = top(jnp.where(lane == i1, neg, el))
    e = jnp.exp(m2 - m1)
    gate1 = gprob / (1.0 + e)
    gate2 = gprob * e / (1.0 + e)

    hot1 = lane == i1
    hot2 = lane == i2
    one1 = jnp.where(hot1, 1.0, 0.0)
    one2 = jnp.where(hot2, 1.0, 0.0)
    before = before_ref[...]
    prefix1 = jnp.dot(before, one1.astype(BF16), preferred_element_type=F32)
    prefix2 = jnp.dot(before, one2.astype(BF16), preferred_element_type=F32)
    total1 = jnp.sum(one1, axis=0, keepdims=True)
    running = run_ref[...]
    rank1 = jnp.sum(jnp.where(hot1, prefix1 + running, 0.0), axis=-1, keepdims=True)
    rank2 = jnp.sum(jnp.where(hot2, prefix2 + (running + total1), 0.0), axis=-1, keepdims=True)
    running = running + total1 + jnp.sum(one2, axis=0, keepdims=True)
    run_ref[...] = running

    fields = (i1 - MOE_GROUPS, i2 - MOE_GROUPS, gate1, gate2, rank1, rank2)
    out = jnp.zeros_like(logits)
    for k, val in enumerate(fields):
        out = jnp.where(lane == k, val, out)
    return out


def _post_mixer_kernel(a_ref, b_ref, h_ref, wa_ref, wb_ref, gxa_ref, wq_ref, kt_ref, v_ref, wo_ref,
                       gffn_ref, whi_ref, wlo_ref, bias_ref, before_ref,
                       h_out_ref, xn_ref, r_ref, cnt_ref, run_ref):
    @pl.when(pl.program_id(0) == 0)
    def _():
        run_ref[...] = jnp.zeros_like(run_ref)

    h = _mixer_out(a_ref, b_ref, h_ref, wa_ref, wb_ref, slice(None))
    h, = _cross_attention([h], gxa_ref, wq_ref, kt_ref, v_ref, wo_ref)
    h_out_ref[...] = h
    xn = _rms(h, gffn_ref[...])
    xn_ref[...] = _pack_halves(xn)
    r_ref[...] = _route(xn, whi_ref, wlo_ref, bias_ref, before_ref, run_ref)
    cnt_ref[...] = run_ref[...]


def post_mixer(ya, yb, h, wa, wb, g_xa, wq, kt, v, wo, g_ffn, w_hi, w_lo, bias, tm=512):
    m, d = h.shape
    tiles_per_batch = m // kt.shape[0] // tm
    idx = np.arange(tm)
    before = jnp.asarray(idx[:, None] > idx[None, :], BF16)
    rows = lambda w: pl.BlockSpec((tm, w), lambda i: (i, 0))
    const = lambda a: pl.BlockSpec(a.shape, lambda i: (0,) * a.ndim, pipeline_mode=pl.Buffered(1))
    per_batch = lambda a: pl.BlockSpec((None,) + a.shape[1:], lambda i: (i // tiles_per_batch, 0, 0))
    g_xa, g_ffn = g_xa.reshape(1, d), g_ffn.reshape(1, d)
    return pl.pallas_call(
        _post_mixer_kernel,
        grid=(m // tm,),
        in_specs=[rows(ya.shape[1]), rows(yb.shape[1]), rows(d), const(wa), const(wb), const(g_xa), const(wq),
                  per_batch(kt), per_batch(v), const(wo), const(g_ffn), const(w_hi), const(w_lo), const(bias),
                  const(before)],
        out_specs=[rows(d), rows(d // 2), rows(LANES), pl.BlockSpec((1, LANES), lambda i: (0, 0))],
        out_shape=[jax.ShapeDtypeStruct((m, d), F32), jax.ShapeDtypeStruct((m, d // 2), jnp.int32),
                   jax.ShapeDtypeStruct((m, LANES), F32), jax.ShapeDtypeStruct((1, LANES), F32)],
        scratch_shapes=[pltpu.VMEM((1, LANES), F32)],
        compiler_params=_cparams("arbitrary"),
        name="post_mixer",
    )(ya, yb, h, wa, wb, g_xa, wq, kt, v, wo, g_ffn, w_hi, w_lo, bias, before)


def _expert_kernel(table_ref, x_ref, wg_hbm, wu_hbm, wd_hbm, o_ref,
                   wg32_ref, wu32_ref, wd32_ref, wgb_ref, wub_ref, wdb_ref, sem_ref, *, layer):
    i = pl.program_id(0)
    beid_ref, valid_ref, first_ref, slot_ref, next_ref = (table_ref.at[k] for k in range(5))
    valid = valid_ref[i]

    def weight_copies(expert, slot):
        return (pltpu.make_async_copy(wg_hbm.at[layer, expert], wg32_ref.at[slot], sem_ref.at[slot, 0]),
                pltpu.make_async_copy(wu_hbm.at[layer, expert], wu32_ref.at[slot], sem_ref.at[slot, 1]),
                pltpu.make_async_copy(wd_hbm.at[layer, expert], wd32_ref.at[slot], sem_ref.at[slot, 2]))

    @pl.when(i == 0)
    def _():
        for copy in weight_copies(beid_ref[0], 0):
            copy.start()

    @pl.when(first_ref[i] == 1)
    def _():
        slot = slot_ref[i]
        for copy in weight_copies(beid_ref[i], slot):
            copy.wait()
        wgb_ref[...] = wg32_ref[slot].astype(BF16)
        wub_ref[...] = wu32_ref[slot].astype(BF16)
        wdb_ref[...] = wd32_ref[slot].astype(BF16)

        @pl.when(next_ref[i] >= 0)
        def _():
            for copy in weight_copies(next_ref[i], 1 - slot):
                copy.start()

    half = MOE_ROWS // 2

    def ffn(n_halves):
        row = lax.broadcasted_iota(jnp.int32, (half, 2 * x_ref.shape[1]), 0)
        xs = [jnp.where(row + k * half < valid, _unpack_halves(x_ref[k * half:(k + 1) * half, :]), 0.0).astype(BF16)
              for k in range(n_halves)]
        gates = [jnp.dot(x, wgb_ref[...], preferred_element_type=F32) for x in xs]
        ups = [jnp.dot(x, wub_ref[...], preferred_element_type=F32) for x in xs]
        acts = [(_silu(g) * u).astype(BF16) for g, u in zip(gates, ups)]
        for k, act in enumerate(acts):
            o_ref[k * half:(k + 1) * half, :] = _pack_halves(jnp.dot(act, wdb_ref[...], preferred_element_type=F32))

    @pl.when(valid > half)
    def _():
        ffn(2)

    @pl.when((valid > 0) & (valid <= half))
    def _():
        ffn(1)
        o_ref[half:, :] = jnp.zeros((half, o_ref.shape[1]), o_ref.dtype)

    @pl.when(valid == 0)
    def _():
        o_ref[...] = jnp.zeros_like(o_ref)


def moe_experts(blocks, xs, w_gate, w_up, w_down, layer):
    n_slots, packed = xs.shape
    d = 2 * packed
    rows = MOE_ROWS
    ff = w_gate.shape[3]
    grid_spec = pltpu.PrefetchScalarGridSpec(
        num_scalar_prefetch=1,
        grid=(n_slots // rows,),
        in_specs=[
            pl.BlockSpec((rows, packed), lambda i, *_: (i, 0)),
            pl.BlockSpec(memory_space=pl.ANY),
            pl.BlockSpec(memory_space=pl.ANY),
            pl.BlockSpec(memory_space=pl.ANY),
        ],
        out_specs=pl.BlockSpec((rows, packed), lambda i, *_: (i, 0)),
        scratch_shapes=[pltpu.VMEM((2, d, ff), F32), pltpu.VMEM((2, d, ff), F32), pltpu.VMEM((2, ff, d), F32),
                        pltpu.VMEM((d, ff), BF16), pltpu.VMEM((d, ff), BF16), pltpu.VMEM((ff, d), BF16),
                        pltpu.SemaphoreType.DMA((2, 3))],
    )
    return pl.pallas_call(
        functools.partial(_expert_kernel, layer=layer),
        grid_spec=grid_spec,
        out_shape=jax.ShapeDtypeStruct((n_slots, packed), jnp.int32),
        compiler_params=_cparams("arbitrary"),
        name="moe_experts",
    )(blocks, xs, w_gate, w_up, w_down)


def _sc_mesh():
    return plsc.VectorSubcoreMesh(core_axis_name="c", subcore_axis_name="s",
                                  num_cores=SC_CORES, num_subcores=SC_SUBCORES)


def _sc_worker():
    return lax.axis_index("s") * SC_CORES + lax.axis_index("c")


def sc_scatter_rows(x, dest, n_slots):
    n_tok, d = x.shape
    per_worker = n_tok // SC_WORKERS
    n_chunks = per_worker // SC_CHUNK
    by_worker = dest.reshape(dest.shape[0] * SC_WORKERS, n_chunks, SC_CHUNK)

    @functools.partial(
        pl.kernel, mesh=_sc_mesh(), out_type=jax.ShapeDtypeStruct((n_slots, d), x.dtype),
        scratch_types=[pltpu.VMEM((n_chunks, SC_CHUNK), jnp.int32), pltpu.VMEM((n_chunks, SC_CHUNK), jnp.int32),
                       pltpu.VMEM((SC_CHUNK, d), x.dtype)],
        name="moe_scatter_rows")
    def scatter(x_hbm, dest_hbm, out_hbm, i0_v, i1_v, rows_v):
        wid = _sc_worker()
        pltpu.sync_copy(dest_hbm.at[wid], i0_v)
        pltpu.sync_copy(dest_hbm.at[SC_WORKERS + wid], i1_v)

        @pl.loop(0, n_chunks)
        def _(j):
            start = pl.multiple_of(wid * per_worker + j * SC_CHUNK, SC_CHUNK)
            pltpu.sync_copy(x_hbm.at[pl.ds(start, SC_CHUNK)], rows_v)
            pltpu.sync_copy(rows_v, out_hbm.at[i0_v.at[j]])
            pltpu.sync_copy(rows_v, out_hbm.at[i1_v.at[j]])

    return scatter(x, by_worker)


def sc_gather_rows(table, idx, n_out):
    d = table.shape[1]
    per_worker = n_out // SC_WORKERS
    n_chunks = per_worker // SC_CHUNK

    @functools.partial(
        pl.kernel, mesh=_sc_mesh(), out_type=jax.ShapeDtypeStruct((n_out, d), table.dtype),
        scratch_types=[pltpu.VMEM((n_chunks, SC_CHUNK), jnp.int32), pltpu.VMEM((SC_CHUNK, d), table.dtype)],
        name="moe_gather_rows")
    def gather(table_hbm, idx_hbm, out_hbm, idx_v, rows_v):
        wid = _sc_worker()
        pltpu.sync_copy(idx_hbm.at[wid], idx_v)

        @pl.loop(0, n_chunks)
        def _(j):
            start = pl.multiple_of(wid * per_worker + j * SC_CHUNK, SC_CHUNK)
            pltpu.sync_copy(table_hbm.at[idx_v.at[j]], rows_v)
            pltpu.sync_copy(rows_v, out_hbm.at[pl.ds(start, SC_CHUNK)])

    return gather(table, idx.reshape(-1, n_chunks, SC_CHUNK))


def _combine_kernel(h_ref, y0_ref, y1_ref, r_ref, g_ref, o_ref, *, final_norm):
    route = r_ref[...]
    h = h_ref[...] + (route[:, 2:3] * _unpack_halves(y0_ref[...]) + route[:, 3:4] * _unpack_halves(y1_ref[...]))
    o_ref[...] = _rms(h, g_ref[...]) if final_norm else h


def moe_combine(h, y01, route, g, final_norm, tm=1024):
    m, d = h.shape
    tm = min(tm, m)
    rows = lambda w: pl.BlockSpec((tm, w), lambda i: (i, 0))
    return pl.pallas_call(
        functools.partial(_combine_kernel, final_norm=final_norm),
        grid=(m // tm,),
        in_specs=[rows(d), rows(d // 2), pl.BlockSpec((tm, d // 2), lambda i: (i + m // tm, 0)), rows(LANES),
                  pl.BlockSpec((1, d), lambda i: (0, 0))],
        out_specs=rows(d),
        out_shape=jax.ShapeDtypeStruct((m, d), F32),
        compiler_params=_cparams("parallel"),
        name="moe_combine",
    )(h, y01, y01, route, g.reshape(1, d))


def _pad_cols(w):
    return jnp.pad(w, ((0, 0), (0, LANES - w.shape[1])))


def _plan_kernel(route_ref, cnt_ref, incl_ref, dest_ref, table_ref):
    f32_sum = lambda x, axis: jnp.sum(x, axis=axis, keepdims=True)
    lane = lax.broadcasted_iota(jnp.int32, (LANES, LANES), 1)
    sub = lax.broadcasted_iota(jnp.int32, (LANES, LANES), 0)
    incl = incl_ref[...]
    is_expert = (lane >= MOE_GROUPS) & (lane < MOE_GROUPS + MOE_EXPERTS)
    shift = MOE_ROWS.bit_length() - 1
    counts = jnp.broadcast_to(cnt_ref[...], (LANES, LANES)).astype(jnp.int32)
    padded = jnp.where(is_expert, ((counts + (MOE_ROWS - 1)) >> shift) << shift, 0)
    pad_end = _mm_sel_rhs(padded.astype(F32), incl)
    pad_start = pad_end - padded.astype(F32)

    route = route_ref[...]
    lane_t = lax.broadcasted_iota(jnp.int32, route.shape, 1)
    lane_f = lane_t.astype(F32)
    start_row = pad_start[0:1, :]
    slots = [f32_sum(jnp.where(lane_f == route[:, k:k + 1] + MOE_GROUPS, start_row, 0.0), 1) + route[:, 4 + k:5 + k]
             for k in range(2)]
    both = jnp.where(lane_t == 0, slots[0], jnp.where(lane_t == 1, slots[1], 0.0))
    dest_ref[...] = both.T[0:8, :].astype(jnp.int32)

    on_sub = lambda rows_equal: rows_equal.T
    expert_sub = (sub >= MOE_GROUPS) & (sub < MOE_GROUPS + MOE_EXPERTS)
    block_start = (lane * MOE_ROWS).astype(F32)
    eid = f32_sum(jnp.where(expert_sub & (on_sub(pad_end) <= block_start), 1.0, 0.0), 0)
    eid = jnp.minimum(eid, float(MOE_EXPERTS - 1))
    filled = on_sub(pad_start + counts.astype(F32))
    own = (sub - MOE_GROUPS).astype(F32) == eid
    valid = jnp.clip(f32_sum(jnp.where(own, filled, 0.0), 0) - block_start[0:1, :], 0.0, float(MOE_ROWS))
    eid_rows = jnp.broadcast_to(eid, (LANES, LANES))
    changed = (lane == 0) | (eid_rows != pltpu.roll(eid_rows, 1, axis=1))
    first = jnp.where((jnp.broadcast_to(valid, (LANES, LANES)) > 0) & changed, 1.0, 0.0)
    ordinal = _mm_sel_rhs(first, incl) - 1.0
    slot = ordinal - 2.0 * jnp.floor(ordinal * 0.5)
    later = (on_sub(first) > 0) & (sub > lane)
    nearest = jnp.min(jnp.where(later, sub, LANES), axis=0, keepdims=True)
    next_eid = f32_sum(jnp.where(sub == nearest, on_sub(eid_rows), 0.0), 0)
    next_eid = jnp.where(nearest < LANES, next_eid, -1.0)
    row8 = lax.broadcasted_iota(jnp.int32, (8, LANES), 0)
    table = jnp.zeros((8, LANES), F32)
    for k, val in enumerate((eid, valid, first[0:1, :], slot[0:1, :], next_eid)):
        table = jnp.where(row8 == k, val, table)
    table_ref[...] = table.astype(jnp.int32)


def moe_plan(route, counts, tm=2048):
    n_tok = route.shape[0]
    tm = min(tm, n_tok)
    idx = np.arange(LANES)
    incl = jnp.asarray(idx[:, None] <= idx[None, :], BF16)
    return pl.pallas_call(
        _plan_kernel,
        grid=(n_tok // tm,),
        in_specs=[pl.BlockSpec((tm, LANES), lambda i: (i, 0)), pl.BlockSpec((1, LANES), lambda i: (0, 0)),
                  pl.BlockSpec((LANES, LANES), lambda i: (0, 0))],
        out_specs=[pl.BlockSpec((8, tm), lambda i: (0, i)), pl.BlockSpec((8, LANES), lambda i: (0, 0))],
        out_shape=[jax.ShapeDtypeStruct((8, n_tok), jnp.int32), jax.ShapeDtypeStruct((8, LANES), jnp.int32)],
        compiler_params=_cparams("arbitrary"),
        name="moe_plan",
    )(route, counts, incl)


def _router_weights(w_group, b_group, w_expert, b_expert):
    w_r = _pad_cols(jnp.concatenate([w_group, w_expert], axis=1))
    w_hi = w_r.astype(BF16)
    w_lo = (w_r - w_hi.astype(F32)).astype(BF16)
    return w_hi, w_lo, _pad_lanes(jnp.concatenate([b_group, b_expert]))


def _moe_layer(h, xn, route, counts, w_gate, w_up, w_down, layer, final_g):
    n_tok, d = h.shape
    n_blocks = -(-(2 * n_tok + MOE_EXPERTS * (MOE_ROWS - 1)) // MOE_ROWS)
    dest, blocks = moe_plan(route, counts)
    xs = sc_scatter_rows(xn, dest, n_blocks * MOE_ROWS)
    ys = moe_experts(blocks, xs, w_gate, w_up, w_down, layer)
    y01 = sc_gather_rows(ys, dest, 2 * n_tok)
    g = jnp.ones((d,), F32) if final_g is None else final_g
    return moe_combine(h, y01, route, g, final_g is not None)


def _memory_kv(memn_in, mem_norm, wk, wv):
    bsz, m, d = memn_in.shape
    w = jnp.concatenate([wk, wv], axis=1).astype(BF16)
    kv, _ = rms_matmul(memn_in.reshape(bsz * m, d), mem_norm, w, jnp.zeros((d, LANES), BF16))
    k = kv[:, :d].reshape(bsz, m, d)
    v = kv[:, d:].reshape(bsz, m, d)
    return jnp.swapaxes(k, 1, 2).astype(BF16), v.astype(BF16)


def kernel(x, mem, mem_norm, final_norm, norm_mix, norm_xa, norm_ffn, xa_wq, xa_wk, xa_wv, xa_wo, moe_w_group, moe_b_group, moe_w_expert, moe_b_expert, moe_w_gate, moe_w_up, moe_w_down, ev_w_in, ev_sc_conv, ev_ssm_conv_w, ev_ssm_conv_b, ev_ssm_dt_bias, ev_ssm_a_log, ev_ssm_d, ev_ssm_norm, ev_w_out, od_w_in, od_gdn_conv, od_gdn_dt_bias, od_gdn_a_log, od_gdn_norm, od_w_out):
    bsz, length, d = x.shape
    n_tok = bsz * length
    depth = norm_mix.shape[0]
    h = x.reshape(n_tok, d)
    for layer in range(depth):
        i = layer // 2
        if layer % 2 == 0:
            w = ev_w_in[i]
            z0 = 3 * SC_DIM
            xbc0 = z0 + SSM_INNER
            w_conv = w[:, xbc0:xbc0 + SSM_XBC].astype(BF16)
            w_main = jnp.concatenate([w[:, z0:xbc0], w[:, :z0]], axis=1).astype(BF16)
            w_small = _pad_cols(w[:, xbc0 + SSM_XBC:]).astype(BF16)
            xbc = rms_matmul_conv(h, norm_mix[layer], w_conv, ev_ssm_conv_w[i], ev_ssm_conv_b[i], length)
            y, small = rms_matmul(h, norm_mix[layer], w_main, w_small, tm=1024, tn=w_main.shape[1])
            y3 = y.reshape(bsz, length, -1)
            small3 = small.reshape(bsz, length, LANES)
            smallt = jnp.swapaxes(small3[:, :, :16], 1, 2)
            ya = short_conv_mixer(y3, ev_sc_conv[i], SSM_INNER // SC_DIM)
            yb = ssd_mixer(xbc.reshape(bsz, length, -1), y3, small3, smallt, ev_ssm_dt_bias[i],
                           ev_ssm_a_log[i], ev_ssm_d[i], ev_ssm_norm[i])
            w_out = ev_w_out[i].astype(BF16)
            split = SC_DIM
        else:
            w = od_w_in[i]
            qkv_w = 3 * GDN_HEADS * GDN_D
            z_end = qkv_w + GDN_HEADS * GDN_D
            w_conv = w[:, :qkv_w].astype(BF16)
            w_main = jnp.concatenate([w[:, qkv_w:z_end], w[:, z_end + 2 * GDN_HEADS:]], axis=1).astype(BF16)
            w_small = _pad_cols(w[:, z_end:z_end + 2 * GDN_HEADS]).astype(BF16)
            qkv = rms_matmul_conv(h, norm_mix[layer], w_conv, od_gdn_conv[i], jnp.zeros((qkv_w,), F32), length)
            y, small = rms_matmul(h, norm_mix[layer], w_main, w_small, tm=1024, tn=w_main.shape[1])
            y3 = y.reshape(bsz, length, -1)
            small3 = small.reshape(bsz, length, LANES)
            smallt = jnp.swapaxes(small3[:, :, :16], 1, 2)
            ya = gated_deltanet_mixer(qkv.reshape(bsz, length, -1), y3, small3, smallt, od_gdn_dt_bias[i],
                                      od_gdn_a_log[i], od_gdn_norm[i])
            yb = stick_breaking_mixer(y3, GDN_HEADS * GDN_D)
            w_out = od_w_out[i].astype(BF16)
            split = GDN_HEADS * GDN_D
        kt, v = _memory_kv(mem, mem_norm, xa_wk[layer], xa_wv[layer])
        w_hi, w_lo, bias = _router_weights(moe_w_group[layer], moe_b_group[layer], moe_w_expert[layer],
                                           moe_b_expert[layer])
        h, xn, route, counts = post_mixer(
            ya.reshape(n_tok, -1), yb.reshape(n_tok, -1), h, w_out[:split], w_out[split:], norm_xa[layer],
            xa_wq[layer].astype(BF16), kt, v, xa_wo[layer].astype(BF16), norm_ffn[layer], w_hi, w_lo, bias)
        h = _moe_layer(h, xn, route, counts, moe_w_gate, moe_w_up, moe_w_down, layer,
                       final_norm if layer == depth - 1 else None)
    return h.reshape(bsz, length, d)
```

```python
import functools

import jax
import jax.numpy as jnp
import numpy as np
from jax import lax
from jax.experimental import pallas as pl
from jax.experimental.pallas import tpu as pltpu
from jax.experimental.pallas import tpu_sc as plsc

F32 = jnp.float32
BF16 = jnp.bfloat16
EPS = 1e-6

D_MODEL = 1024
MEM_LEN = 256
SC_DIM = 512
SSM_HEADS = 16
SSM_HEAD_DIM = 64
SSM_INNER = 1024
SSM_GROUPS = 2
SSM_STATE = 128
SSM_XBC = SSM_INNER + 2 * SSM_GROUPS * SSM_STATE
SSD_CHUNK = 128
GDN_HEADS = 8
GDN_D = 128
GDN_CHUNK = 64
GDN_TILE = 128
SB_HEADS = 8
SB_HEAD_DIM = 64
SB_DIM = 512
SB_BLOCK = 128
SB_STEP_HEADS = 8
XA_HEADS = 4
XA_HEAD_DIM = 256
MOE_GROUPS = 4
MOE_PER_GROUP = 8
MOE_EXPERTS = 32
MOE_FF = 512
MOE_ROWS = 512
SC_CORES = 2
SC_SUBCORES = 16
SC_WORKERS = SC_CORES * SC_SUBCORES
SC_CHUNK = 64
HALO = 8
CONV_CHUNK = 512
LANES = 128
SB_LOG_ZERO = -104.0
VMEM_LIMIT = 56 * 1024 * 1024


def _cparams(*sem):
    return pltpu.CompilerParams(dimension_semantics=sem, vmem_limit_bytes=VMEM_LIMIT)


def _mm(a, b):
    return jnp.dot(a.astype(BF16), b.astype(BF16), preferred_element_type=F32)


def _mm_nt(a, b):
    return lax.dot_general(a.astype(BF16), b.astype(BF16), (((1,), (1,)), ((), ())),
                           preferred_element_type=F32)


def _split_bf16(x, n):
    parts, r = [], x
    for _ in range(n):
        p = r.astype(BF16)
        parts.append(p)
        r = r - p.astype(F32)
    return parts


def _mm_sel_rhs(x, sel, n=3):
    return sum(jnp.dot(p, sel, preferred_element_type=F32) for p in _split_bf16(x, n))


def _mm_sel_lhs(sel, x, n=3):
    return sum(jnp.dot(sel, p, preferred_element_type=F32) for p in _split_bf16(x, n))


def _spread_heads(x, first, n_heads, width):
    rows = x.shape[0]
    col = lambda h: jnp.broadcast_to(x[:, first + h:first + h + 1], (rows, LANES))
    if width == LANES:
        return jnp.concatenate([col(h) for h in range(n_heads)], axis=1)
    left = lax.broadcasted_iota(jnp.int32, (rows, LANES), 1) < width
    return jnp.concatenate([jnp.where(left, col(h), col(h + 1)) for h in range(0, n_heads, 2)], axis=1)


def _pack_halves(x):
    n = x.shape[1] // 2
    lo = pltpu.bitcast(x[:, :n].astype(BF16).astype(F32), jnp.int32)
    hi = pltpu.bitcast(x[:, n:].astype(BF16).astype(F32), jnp.int32)
    return lax.shift_right_logical(lo, 16) | (hi & jnp.int32(-65536))


def _unpack_halves(p):
    lo = pltpu.bitcast(lax.shift_left(p, 16), F32)
    hi = pltpu.bitcast(p & jnp.int32(-65536), F32)
    return jnp.concatenate([lo, hi], axis=1)


def _silu(x):
    return x * jax.nn.sigmoid(x)


def _softplus(x):
    return jnp.maximum(x, 0.0) + jnp.log(1.0 + jnp.exp(-jnp.abs(x)))


def _rms(x, g):
    return x * lax.rsqrt(jnp.mean(x * x, axis=-1, keepdims=True) + EPS) * g


def _rms_matmul_kernel(x_ref, g_ref, w_ref, ws_ref, o_ref, os_ref):
    xn = _rms(x_ref[...], g_ref[...]).astype(BF16)
    o_ref[...] = jnp.dot(xn, w_ref[...], preferred_element_type=F32)
    os_ref[...] = jnp.dot(xn, ws_ref[...], preferred_element_type=F32)


def rms_matmul(x, g, w, ws, tm=512, tn=512):
    m, k = x.shape
    n = w.shape[1]
    tm = min(tm, m)
    main, small = pl.pallas_call(
        _rms_matmul_kernel,
        grid=(n // tn, m // tm),
        in_specs=[
            pl.BlockSpec((tm, k), lambda j, i: (i, 0)),
            pl.BlockSpec((1, k), lambda j, i: (0, 0)),
            pl.BlockSpec((k, tn), lambda j, i: (0, j)),
            pl.BlockSpec((k, LANES), lambda j, i: (0, 0)),
        ],
        out_specs=[
            pl.BlockSpec((tm, tn), lambda j, i: (i, j)),
            pl.BlockSpec((None, tm, LANES), lambda j, i: (j, i, 0)),
        ],
        out_shape=[jax.ShapeDtypeStruct((m, n), F32), jax.ShapeDtypeStruct((n // tn, m, LANES), F32)],
        compiler_params=_cparams("parallel", "parallel"),
        name="rms_matmul",
    )(x, g.reshape(1, k), w, ws)
    return main, small[0]


def _causal_conv(ext_ref, w_ref, rows):
    width = w_ref.shape[0]
    ext = ext_ref[...]
    acc = None
    for j in range(width):
        shift = width - 1 - j
        moved = ext if shift == 0 else pltpu.roll(ext, shift, axis=0)
        term = w_ref[j:j + 1, :] * moved[HALO:HALO + rows, :]
        acc = term if acc is None else acc + term
    return acc


def _rms_matmul_conv_kernel(x_ref, g_ref, w_ref, cw_ref, cb_ref, o_ref, *ext_refs, tiles_per_seq):
    tm = x_ref.shape[0]
    starts_sequence = pl.program_id(1) % tiles_per_seq == 0

    @pl.when(starts_sequence)
    def _():
        for ext_ref in ext_refs:
            ext_ref[0:HALO, :] = jnp.zeros((HALO, CONV_CHUNK), F32)

    @pl.when(jnp.logical_not(starts_sequence))
    def _():
        for ext_ref in ext_refs:
            ext_ref[0:HALO, :] = ext_ref[tm:tm + HALO, :]

    xn = _rms(x_ref[...], g_ref[...]).astype(BF16)
    for c, ext_ref in enumerate(ext_refs):
        cols = slice(c * CONV_CHUNK, (c + 1) * CONV_CHUNK)
        ext_ref[HALO:, :] = jnp.dot(xn, w_ref[:, cols], preferred_element_type=F32)
        o_ref[:, cols] = _silu(_causal_conv(ext_ref, cw_ref.at[:, cols], tm) + cb_ref[:, cols])


def rms_matmul_conv(x, g, w, conv_w, conv_b, seq_len, tm=1024, tn=1536):
    m, k = x.shape
    n = w.shape[1]
    cols = lambda rows: pl.BlockSpec((rows, tn), lambda j, i: (0, j))
    return pl.pallas_call(
        functools.partial(_rms_matmul_conv_kernel, tiles_per_seq=seq_len // tm),
        grid=(n // tn, m // tm),
        in_specs=[
            pl.BlockSpec((tm, k), lambda j, i: (i, 0)),
            pl.BlockSpec((1, k), lambda j, i: (0, 0)),
            cols(k), cols(conv_w.shape[0]), cols(1),
        ],
        out_specs=pl.BlockSpec((tm, tn), lambda j, i: (i, j)),
        out_shape=jax.ShapeDtypeStruct((m, n), F32),
        scratch_shapes=[pltpu.VMEM((tm + HALO, CONV_CHUNK), F32)] * (tn // CONV_CHUNK),
        compiler_params=_cparams("arbitrary", "arbitrary"),
        name="rms_matmul_conv",
    )(x, g.reshape(1, k), w, conv_w, conv_b.reshape(1, n))


def _halo_index(rows):
    step = rows // HALO
    return lambda i: jnp.maximum(i * step - 1, 0)


def _sc_kernel(b_ref, c_ref, x_ref, ch_ref, xh_ref, w_ref, o_ref, ext_ref):
    rows = o_ref.shape[0]
    first = pl.program_id(1) == 0
    ext_ref[0:HALO, :] = jnp.where(first, 0.0, ch_ref[...] * xh_ref[...])
    ext_ref[HALO:, :] = c_ref[...] * x_ref[...]
    o_ref[...] = b_ref[...] * _causal_conv(ext_ref, w_ref, rows)


def short_conv_mixer(y3, w, col0, tl=1024):
    bsz, length, _ = y3.shape
    tl = min(tl, length)
    hidx = _halo_index(tl)
    return pl.pallas_call(
        _sc_kernel,
        grid=(bsz, length // tl),
        in_specs=[
            pl.BlockSpec((None, tl, SC_DIM), lambda b, i: (b, i, col0)),
            pl.BlockSpec((None, tl, SC_DIM), lambda b, i: (b, i, col0 + 1)),
            pl.BlockSpec((None, tl, SC_DIM), lambda b, i: (b, i, col0 + 2)),
            pl.BlockSpec((None, HALO, SC_DIM), lambda b, i: (b, hidx(i), col0 + 1)),
            pl.BlockSpec((None, HALO, SC_DIM), lambda b, i: (b, hidx(i), col0 + 2)),
            pl.BlockSpec(w.shape, lambda b, i: (0, 0)),
        ],
        out_specs=pl.BlockSpec((None, tl, SC_DIM), lambda b, i: (b, i, 0)),
        out_shape=jax.ShapeDtypeStruct((bsz, length, SC_DIM), F32),
        scratch_shapes=[pltpu.VMEM((tl + HALO, SC_DIM), F32)],
        compiler_params=_cparams("parallel", "arbitrary"),
        name="short_conv_mixer",
    )(y3, y3, y3, y3, y3, w)


def _ssd_kernel(xbc_ref, z_ref, dt_ref, dtt_ref, dtb_r_ref, dtb_c_ref,
                alog_r_ref, alog_c_ref, d_ref, nw_ref, tri_ref, trit_ref,
                o_ref, s_ref):
    q = SSD_CHUNK
    hpg = SSM_HEADS // SSM_GROUPS
    gw = hpg * SSM_HEAD_DIM

    @pl.when(pl.program_id(1) == 0)
    def _():
        s_ref[...] = jnp.zeros_like(s_ref)

    xbc = xbc_ref[...]
    xs = xbc[:, :SSM_INNER]
    bm = xbc[:, SSM_INNER:SSM_INNER + SSM_GROUPS * SSM_STATE]
    cm = xbc[:, SSM_INNER + SSM_GROUPS * SSM_STATE:]

    dt = _softplus(dt_ref[...] + dtb_r_ref[...])
    acs = _mm_sel_lhs(tri_ref[...], dt * -jnp.exp(alog_r_ref[...]))
    dtt = _softplus(dtt_ref[...] + dtb_c_ref[...])
    acst = _mm_sel_rhs(dtt * -jnp.exp(alog_c_ref[...]), trit_ref[...])
    dt_full = _spread_heads(dt, 0, SSM_HEADS, SSM_HEAD_DIM)
    acs_full = _spread_heads(acs, 0, SSM_HEADS, SSM_HEAD_DIM)
    acs_col = _spread_heads(acs, 0, SSM_HEADS, q)

    xdt = xs * dt_full
    acs_last = acs_full[q - 1:q, :]
    xw = xdt * jnp.exp(acs_last - acs_full)
    chunk_decay = jnp.exp(acs_last)

    row = lax.broadcasted_iota(jnp.int32, (q, q), 0)
    col = lax.broadcasted_iota(jnp.int32, (q, q), 1)
    causal = row >= col
    lane = lax.broadcasted_iota(jnp.int32, (q, 2 * SSM_HEAD_DIM), 1)

    y_diag, y_off = [], []
    for g in range(SSM_GROUPS):
        bm_g = bm[:, g * SSM_STATE:(g + 1) * SSM_STATE]
        cm_g = cm[:, g * SSM_STATE:(g + 1) * SSM_STATE]
        cb_g = _mm_nt(cm_g, bm_g)
        state = s_ref[g]
        y_off.append(_mm(cm_g, state))
        s_ref[g] = state * chunk_decay[:, g * gw:(g + 1) * gw] + _mm(bm_g.T, xw[:, g * gw:(g + 1) * gw])
        for pair in range(hpg // 2):
            h0 = g * hpg + 2 * pair
            xdt_pair = xdt[:, h0 * SSM_HEAD_DIM:(h0 + 2) * SSM_HEAD_DIM]
            weights = []
            for h in (h0, h0 + 1):
                seg = acs_col[:, h * q:(h + 1) * q] - acst[h:h + 1, :]
                weights.append(cb_g * jnp.where(causal, jnp.exp(seg), 0.0))
            both = _mm(jnp.concatenate(weights, axis=0), xdt_pair)
            y_diag.append(jnp.where(lane < SSM_HEAD_DIM, both[:q], both[q:]))
    y = (jnp.concatenate(y_diag, axis=1) + jnp.concatenate(y_off, axis=1) * jnp.exp(acs_full)
         + xs * d_ref[...])
    y = y * _silu(z_ref[...])
    halves = []
    for g in range(SSM_GROUPS):
        yg = y[:, g * gw:(g + 1) * gw]
        halves.append(yg * lax.rsqrt(jnp.mean(yg * yg, axis=-1, keepdims=True) + EPS))
    o_ref[...] = jnp.concatenate(halves, axis=1) * nw_ref[...]


def _pad_lanes(v, fill=0.0):
    return jnp.pad(v.astype(F32), (0, LANES - v.shape[0]), constant_values=fill).reshape(1, LANES)


def _pad_col(v, rows=16):
    return jnp.pad(v.astype(F32), (0, rows - v.shape[0])).reshape(rows, 1)


def ssd_mixer(xbc3, y3, small3, smallt, dt_bias, a_log, d_skip, norm_w):
    bsz, length, _ = y3.shape
    q = SSD_CHUNK
    tri = jnp.asarray(np.tril(np.ones((q, q), np.float32)), BF16)
    trit = jnp.asarray(np.triu(np.ones((q, q), np.float32)), BF16)
    d_full = jnp.repeat(d_skip.astype(F32), SSM_HEAD_DIM).reshape(1, SSM_INNER)
    const = lambda a: pl.BlockSpec(a.shape, lambda b, c: (0,) * a.ndim)
    args = [_pad_lanes(dt_bias), _pad_col(dt_bias), _pad_lanes(a_log),
            _pad_col(a_log), d_full, norm_w.reshape(1, -1), tri, trit]
    return pl.pallas_call(
        _ssd_kernel,
        grid=(bsz, length // q),
        in_specs=[
            pl.BlockSpec((None, q, SSM_XBC), lambda b, c: (b, c, 0)),
            pl.BlockSpec((None, q, SSM_INNER), lambda b, c: (b, c, 0)),
            pl.BlockSpec((None, q, LANES), lambda b, c: (b, c, 0)),
            pl.BlockSpec((None, 16, q), lambda b, c: (b, 0, c)),
        ] + [const(a) for a in args],
        out_specs=pl.BlockSpec((None, q, SSM_INNER), lambda b, c: (b, c, 0)),
        out_shape=jax.ShapeDtypeStruct((bsz, length, SSM_INNER), F32),
        scratch_shapes=[pltpu.VMEM((SSM_GROUPS, SSM_STATE, SSM_INNER // SSM_GROUPS), F32)],
        compiler_params=_cparams("parallel", "arbitrary"),
        name="ssd_mixer",
    )(xbc3, y3, small3, smallt, *args)


def _unit_lower_inverse(mats, row, col):
    eye = jnp.where(row == col, 1.0, 0.0)
    blk = lambda n: (row >> (n.bit_length() - 1)) == (col >> (n.bit_length() - 1))
    size = row.shape[0]
    p = [jnp.where(blk(16), -a, 0.0) for a in mats]
    t = [eye + x for x in p]
    p = [_mm(x, x) for x in p]
    for _ in range(2):
        both = [_mm(jnp.concatenate([x, y], axis=0), x) for x, y in zip(p, t)]
        p = [b[:size] for b in both]
        t = [y + b[size:] for y, b in zip(t, both)]
    t = [y + _mm(y, x) for y, x in zip(t, p)]
    for n in (16, 32):
        band = blk(2 * n) & jnp.logical_not(blk(n))
        left = [_mm(y, jnp.where(band, a, 0.0)) for y, a in zip(t, mats)]
        t = [y - _mm(x, y) for y, x in zip(t, left)]
    return t


def _gdn_kernel(qkv_ref, z_ref, ab_ref, abt_ref, dtb_r_ref, dtb_c_ref, alog_r_ref,
                alog_c_ref, nw_ref, tri_ref, trit_ref, o_ref, s_ref):
    n = GDN_TILE
    c = GDN_CHUNK
    d = GDN_D
    hd = GDN_HEADS * d

    @pl.when(pl.program_id(1) == 0)
    def _():
        s_ref[...] = jnp.zeros_like(s_ref)

    qkv = qkv_ref[...]
    z = z_ref[...]

    ab = ab_ref[...]
    g = -jnp.exp(alog_r_ref[...]) * _softplus(ab + dtb_r_ref[...])
    gc_full = _spread_heads(_mm_sel_lhs(tri_ref[...], g), 0, GDN_HEADS, d)
    beta_full = _spread_heads(jax.nn.sigmoid(ab), GDN_HEADS, GDN_HEADS, d)
    gt = -jnp.exp(alog_c_ref[...]) * _softplus(abt_ref[...] + dtb_c_ref[...])
    gct = _mm_sel_rhs(gt, trit_ref[...])

    row = lax.broadcasted_iota(jnp.int32, (n, n), 0)
    col = lax.broadcasted_iota(jnp.int32, (n, n), 1)
    same = (row >> (c.bit_length() - 1)) == (col >> (c.bit_length() - 1))
    incl = same & (row >= col)
    strict = same & (row > col)
    zeros_half = jnp.zeros((c, d), F32)

    heads = range(GDN_HEADS)
    sl = [slice(h * d, (h + 1) * d) for h in heads]
    l2n = lambda x: x * lax.rsqrt(jnp.sum(x * x, axis=-1, keepdims=True) + EPS)
    qn = [l2n(qkv[:, sl[h]]) * (d ** -0.5) for h in heads]
    kn = [l2n(qkv[:, hd + h * d:hd + (h + 1) * d]) for h in heads]
    vh = [qkv[:, 2 * hd + h * d:2 * hd + (h + 1) * d] for h in heads]
    gcol = [gc_full[:, sl[h]] for h in heads]
    beta = [beta_full[:, sl[h]] for h in heads]
    edec = [jnp.exp(gcol[h] - gct[h:h + 1, :]) for h in heads]
    egc = [jnp.exp(x) for x in gcol]
    kb = [kn[h] * beta[h] for h in heads]
    on_k = [_mm_nt(jnp.concatenate([kb[h], qn[h]], axis=0), kn[h]) for h in heads]
    lower = [jnp.where(strict, on_k[h][:n] * edec[h], 0.0) for h in heads]
    aqk = [jnp.where(incl, on_k[h][n:] * edec[h], 0.0) for h in heads]
    tinv = _unit_lower_inverse(lower, row, col)
    sol = [_mm(tinv[h], jnp.concatenate([vh[h] * beta[h], kb[h] * egc[h]], axis=1)) for h in heads]
    qd = [qn[h] * egc[h] for h in heads]
    glast = [(gcol[h][c - 1:c, :], gcol[h][n - 1:n, :]) for h in heads]
    kdt = [(kn[h] * jnp.exp(jnp.concatenate([jnp.broadcast_to(glast[h][0], (c, d)),
                                             jnp.broadcast_to(glast[h][1], (c, d))], axis=0) - gcol[h])).T
           for h in heads]
    s0 = [s_ref[h] for h in heads]
    on_s0 = [_mm(jnp.concatenate([sol[h][:c, d:], qd[h][:c]], axis=0), s0[h]) for h in heads]
    v0 = [sol[h][:c, :d] - on_s0[h][:c] for h in heads]
    s1 = [s0[h] * jnp.exp(glast[h][0]) + _mm(kdt[h], jnp.concatenate([v0[h], zeros_half], axis=0)) for h in heads]
    on_s1 = [_mm(jnp.concatenate([sol[h][c:, d:], qd[h][c:]], axis=0), s1[h]) for h in heads]
    v1 = [sol[h][c:, :d] - on_s1[h][:c] for h in heads]
    for h in heads:
        s_ref[h] = s1[h] * jnp.exp(glast[h][1]) + _mm(kdt[h], jnp.concatenate([zeros_half, v1[h]], axis=0))
    outs = []
    for h in heads:
        o = (jnp.concatenate([on_s0[h][c:], on_s1[h][c:]], axis=0)
             + _mm(aqk[h], jnp.concatenate([v0[h], v1[h]], axis=0)))
        o = o * lax.rsqrt(jnp.mean(o * o, axis=-1, keepdims=True) + EPS) * nw_ref[...]
        outs.append(o * _silu(z[:, sl[h]]))
    o_ref[...] = jnp.concatenate(outs, axis=1)


def gated_deltanet_mixer(qkv3, y3, small3, smallt, dt_bias, a_log, norm_w):
    bsz, length, _ = y3.shape
    n = GDN_TILE
    hd = GDN_HEADS * GDN_D
    idx = np.arange(n)
    same = (idx[:, None] // GDN_CHUNK) == (idx[None, :] // GDN_CHUNK)
    tri = jnp.asarray(same & (idx[:, None] >= idx[None, :]), BF16)
    trit = jnp.asarray(same & (idx[:, None] <= idx[None, :]), BF16)
    const = lambda a: pl.BlockSpec(a.shape, lambda b, c: (0,) * a.ndim)
    args = [_pad_lanes(dt_bias), _pad_col(dt_bias), _pad_lanes(a_log), _pad_col(a_log),
            norm_w.reshape(1, -1), tri, trit]
    return pl.pallas_call(
        _gdn_kernel,
        grid=(bsz, length // n),
        in_specs=[
            pl.BlockSpec((None, n, 3 * hd), lambda b, c: (b, c, 0)),
            pl.BlockSpec((None, n, hd), lambda b, c: (b, c, 0)),
            pl.BlockSpec((None, n, LANES), lambda b, c: (b, c, 0)),
            pl.BlockSpec((None, 16, n), lambda b, c: (b, 0, c)),
        ] + [const(a) for a in args],
        out_specs=pl.BlockSpec((None, n, hd), lambda b, c: (b, c, 0)),
        out_shape=jax.ShapeDtypeStruct((bsz, length, hd), F32),
        scratch_shapes=[pltpu.VMEM((GDN_HEADS, GDN_D, GDN_D), F32)],
        compiler_params=_cparams("parallel", "arbitrary"),
        name="gated_deltanet",
    )(qkv3, y3, small3, smallt, *args)


def _sb_kernel(q_ref, k_ref, v_ref, upper_ref, o_ref):
    blk = SB_BLOCK
    pair_w = 2 * SB_HEAD_DIM
    n_pairs = SB_STEP_HEADS // 2
    i = pl.program_id(2)
    q = q_ref[...] * (SB_HEAD_DIM ** -0.5)
    lane = lax.broadcasted_iota(jnp.int32, (blk, pair_w), 1)
    first_head = lane < SB_HEAD_DIM
    qs = []
    for p in range(n_pairs):
        q2 = q[:, p * pair_w:(p + 1) * pair_w]
        qs += [jnp.where(first_head, q2, 0.0).astype(BF16), jnp.where(first_head, 0.0, q2).astype(BF16)]
    row = lax.broadcasted_iota(jnp.int32, (blk, blk), 0)
    col = lax.broadcasted_iota(jnp.int32, (blk, blk), 1)
    earlier = col < row
    upper = upper_ref[...]
    heads = range(SB_STEP_HEADS)

    def key_block(kb, accs, sticks, diagonal):
        start = pl.multiple_of(kb * blk, blk)
        k = k_ref[pl.ds(start, blk), :].astype(BF16)
        v = v_ref[pl.ds(start, blk), :].astype(BF16)
        kp = [k[:, p * pair_w:(p + 1) * pair_w] for p in range(n_pairs)]
        vp = [v[:, p * pair_w:(p + 1) * pair_w] for p in range(n_pairs)]
        logits = [lax.dot_general(qs[h], kp[h // 2], (((1,), (1,)), ((), ())), preferred_element_type=F32)
                  for h in heads]
        log_keep = [-_softplus(x) for x in logits]
        if diagonal:
            log_keep = [jnp.where(earlier, x, 0.0) for x in log_keep]
        between = [_mm_sel_rhs(log_keep[h], upper, 2) + sticks[h] for h in heads]
        w = [jnp.exp(logits[h] + log_keep[h] + between[h]) for h in heads]
        if diagonal:
            w = [jnp.where(earlier, x, 0.0) for x in w]
        pv = [jnp.dot(w[h].astype(BF16), vp[h // 2], preferred_element_type=F32) for h in heads]
        accs = tuple(accs[p] + jnp.where(first_head, pv[2 * p], pv[2 * p + 1]) for p in range(n_pairs))
        sticks = tuple(sticks[h] + jnp.sum(log_keep[h], axis=-1, keepdims=True) for h in heads)
        return accs, sticks

    accs = tuple(jnp.zeros((blk, pair_w), F32) for _ in range(n_pairs))
    sticks = tuple(jnp.zeros((blk, 1), F32) for _ in heads)
    accs, sticks = key_block(i, accs, sticks, True)

    def alive(state):
        kb, _, sticks = state
        longest = sticks[0]
        for s in sticks[1:]:
            longest = jnp.maximum(longest, s)
        return (kb >= 0) & (jnp.max(longest) > SB_LOG_ZERO)

    def body(state):
        kb, accs, sticks = state
        accs, sticks = key_block(kb, accs, sticks, False)
        return kb - 1, accs, sticks

    _, accs, _ = lax.while_loop(alive, body, (i - 1, accs, sticks))
    o_ref[...] = jnp.concatenate(accs, axis=1)


def stick_breaking_mixer(y3, col0):
    bsz, length, _ = y3.shape
    blk = SB_BLOCK
    step_w = SB_STEP_HEADS * SB_HEAD_DIM
    steps = SB_DIM // step_w
    q0 = col0 // step_w
    idx = np.arange(blk)
    upper = jnp.asarray(idx[:, None] > idx[None, :], BF16)
    resident = lambda off: pl.BlockSpec((None, length, step_w), lambda b, p, i: (b, 0, q0 + off + p),
                                        pipeline_mode=pl.Buffered(1))
    return pl.pallas_call(
        _sb_kernel,
        grid=(bsz, steps, length // blk),
        in_specs=[
            pl.BlockSpec((None, blk, step_w), lambda b, p, i: (b, i, q0 + p)),
            resident(steps),
            resident(2 * steps),
            pl.BlockSpec((blk, blk), lambda b, p, i: (0, 0)),
        ],
        out_specs=pl.BlockSpec((None, blk, step_w), lambda b, p, i: (b, i, p)),
        out_shape=jax.ShapeDtypeStruct((bsz, length, SB_DIM), F32),
        compiler_params=_cparams("parallel", "parallel", "arbitrary"),
        name="stick_breaking",
    )(y3, y3, y3, upper)


def _mixer_out(a_ref, b_ref, h_ref, wa_ref, wb_ref, rows):
    return h_ref[rows, :] + (jnp.dot(a_ref[rows, :].astype(BF16), wa_ref[...], preferred_element_type=F32)
                             + jnp.dot(b_ref[rows, :].astype(BF16), wb_ref[...], preferred_element_type=F32))


def _cross_attention(hs, g_ref, wq_ref, kt_ref, v_ref, wo_ref):
    us = [_rms(h, g_ref[...]).astype(BF16) for h in hs]
    qs = [jnp.dot(u, wq_ref[...], preferred_element_type=F32) for u in us]
    heads = [[] for _ in hs]
    for hd in range(XA_HEADS):
        sl = slice(hd * XA_HEAD_DIM, (hd + 1) * XA_HEAD_DIM)
        ss = [jnp.dot(q[:, sl].astype(BF16), kt_ref[sl, :], preferred_element_type=F32) * (XA_HEAD_DIM ** -0.5)
              for q in qs]
        ps = [jnp.exp(s - jnp.max(s, axis=-1, keepdims=True)) for s in ss]
        ps = [p / jnp.sum(p, axis=-1, keepdims=True) for p in ps]
        for k, p in enumerate(ps):
            heads[k].append(jnp.dot(p.astype(BF16), v_ref[:, sl], preferred_element_type=F32))
    os_ = [jnp.concatenate(hk, axis=1).astype(BF16) for hk in heads]
    return [h + jnp.dot(o, wo_ref[...], preferred_element_type=F32) for h, o in zip(hs, os_)]


def _route(xn, whi_ref, wlo_ref, b_ref, before_ref, run_ref):
    x_hi = xn.astype(BF16)
    x_lo = (xn - x_hi.astype(F32)).astype(BF16)
    logits = (jnp.dot(x_hi, whi_ref[...], preferred_element_type=F32)
              + jnp.dot(x_lo, whi_ref[...], preferred_element_type=F32)
              + jnp.dot(x_hi, wlo_ref[...], preferred_element_type=F32) + b_ref[...])
    lane = lax.broadcasted_iota(jnp.int32, logits.shape, 1).astype(F32)
    neg = -1e30
    none = float(LANES)

    def top(vals):
        best = jnp.max(vals, axis=-1, keepdims=True)
        where = jnp.min(jnp.where(vals == best, lane, none), axis=-1, keepdims=True)
        return best, where

    gl = jnp.where(lane < MOE_GROUPS, logits, neg)
    gbest, gsel = top(gl)
    gprob = 1.0 / jnp.sum(jnp.exp(gl - gbest), axis=-1, keepdims=True)
    lo = MOE_GROUPS + gsel * MOE_PER_GROUP
    el = jnp.where((lane >= lo) & (lane < lo + MOE_PER_GROUP), logits, neg)
    m1, i1 = top(el)
    m2, i2 = top(jnp.where(lane == i1, neg, el))
    e = jnp.exp(m2 - m1)
    gate1 = gprob / (1.0 + e)
    gate2 = gprob * e / (1.0 + e)

    hot1 = lane == i1
    hot2 = lane == i2
    one1 = jnp.where(hot1, 1.0, 0.0)
    one2 = jnp.where(hot2, 1.0, 0.0)
    before = before_ref[...]
    prefix1 = jnp.dot(before, one1.astype(BF16), preferred_element_type=F32)
    prefix2 = jnp.dot(before, one2.astype(BF16), preferred_element_type=F32)
    total1 = jnp.sum(one1, axis=0, keepdims=True)
    running = run_ref[...]
    rank1 = jnp.sum(jnp.where(hot1, prefix1 + running, 0.0), axis=-1, keepdims=True)
    rank2 = jnp.sum(jnp.where(hot2, prefix2 + (running + total1), 0.0), axis=-1, keepdims=True)
    running = running + total1 + jnp.sum(one2, axis=0, keepdims=True)
    run_ref[...] = running

    fields = (i1 - MOE_GROUPS, i2 - MOE_GROUPS, gate1, gate2, rank1, rank2)
    out = jnp.zeros_like(logits)
    for k, val in enumerate(fields):
        out = jnp.where(lane == k, val, out)
    return out


def _post_mixer_kernel(a_ref, b_ref, h_ref, wa_ref, wb_ref, gxa_ref, wq_ref, kt_ref, v_ref, wo_ref,
                       gffn_ref, whi_ref, wlo_ref, bias_ref, before_ref,
                       h_out_ref, xn_ref, r_ref, cnt_ref, run_ref):
    @pl.when(pl.program_id(0) == 0)
    def _():
        run_ref[...] = jnp.zeros_like(run_ref)

    h = _mixer_out(a_ref, b_ref, h_ref, wa_ref, wb_ref, slice(None))
    h, = _cross_attention([h], gxa_ref, wq_ref, kt_ref, v_ref, wo_ref)
    h_out_ref[...] = h
    xn = _rms(h, gffn_ref[...])
    xn_ref[...] = _pack_halves(xn)
    r_ref[...] = _route(xn, whi_ref, wlo_ref, bias_ref, before_ref, run_ref)
    cnt_ref[...] = run_ref[...]


def post_mixer(ya, yb, h, wa, wb, g_xa, wq, kt, v, wo, g_ffn, w_hi, w_lo, bias, tm=512):
    m, d = h.shape
    tiles_per_batch = m // kt.shape[0] // tm
    idx = np.arange(tm)
    before = jnp.asarray(idx[:, None] > idx[None, :], BF16)
    rows = lambda w: pl.BlockSpec((tm, w), lambda i: (i, 0))
    const = lambda a: pl.BlockSpec(a.shape, lambda i: (0,) * a.ndim, pipeline_mode=pl.Buffered(1))
    per_batch = lambda a: pl.BlockSpec((None,) + a.shape[1:], lambda i: (i // tiles_per_batch, 0, 0))
    g_xa, g_ffn = g_xa.reshape(1, d), g_ffn.reshape(1, d)
    return pl.pallas_call(
        _post_mixer_kernel,
        grid=(m // tm,),
        in_specs=[rows(ya.shape[1]), rows(yb.shape[1]), rows(d), const(wa), const(wb), const(g_xa), const(wq),
                  per_batch(kt), per_batch(v), const(wo), const(g_ffn), const(w_hi), const(w_lo), const(bias),
                  const(before)],
        out_specs=[rows(d), rows(d // 2), rows(LANES), pl.BlockSpec((1, LANES), lambda i: (0, 0))],
        out_shape=[jax.ShapeDtypeStruct((m, d), F32), jax.ShapeDtypeStruct((m, d // 2), jnp.int32),
                   jax.ShapeDtypeStruct((m, LANES), F32), jax.ShapeDtypeStruct((1, LANES), F32)],
        scratch_shapes=[pltpu.VMEM((1, LANES), F32)],
        compiler_params=_cparams("arbitrary"),
        name="post_mixer",
    )(ya, yb, h, wa, wb, g_xa, wq, kt, v, wo, g_ffn, w_hi, w_lo, bias, before)


def _expert_kernel(table_ref, x_ref, wg_hbm, wu_hbm, wd_hbm, o_ref,
                   wg32_ref, wu32_ref, wd32_ref, wgb_ref, wub_ref, wdb_ref, sem_ref, *, layer):
    i = pl.program_id(0)
    beid_ref, valid_ref, first_ref, slot_ref, next_ref = (table_ref.at[k] for k in range(5))
    valid = valid_ref[i]

    def weight_copies(expert, slot):
        return (pltpu.make_async_copy(wg_hbm.at[layer, expert], wg32_ref.at[slot], sem_ref.at[slot, 0]),
                pltpu.make_async_copy(wu_hbm.at[layer, expert], wu32_ref.at[slot], sem_ref.at[slot, 1]),
                pltpu.make_async_copy(wd_hbm.at[layer, expert], wd32_ref.at[slot], sem_ref.at[slot, 2]))

    @pl.when(i == 0)
    def _():
        for copy in weight_copies(beid_ref[0], 0):
            copy.start()

    @pl.when(first_ref[i] == 1)
    def _():
        slot = slot_ref[i]
        for copy in weight_copies(beid_ref[i], slot):
            copy.wait()
        wgb_ref[...] = wg32_ref[slot].astype(BF16)
        wub_ref[...] = wu32_ref[slot].astype(BF16)
        wdb_ref[...] = wd32_ref[slot].astype(BF16)

        @pl.when(next_ref[i] >= 0)
        def _():
            for copy in weight_copies(next_ref[i], 1 - slot):
                copy.start()

    half = MOE_ROWS // 2

    def ffn(n_halves):
        row = lax.broadcasted_iota(jnp.int32, (half, 2 * x_ref.shape[1]), 0)
        xs = [jnp.where(row + k * half < valid, _unpack_halves(x_ref[k * half:(k + 1) * half, :]), 0.0).astype(BF16)
              for k in range(n_halves)]
        gates = [jnp.dot(x, wgb_ref[...], preferred_element_type=F32) for x in xs]
        ups = [jnp.dot(x, wub_ref[...], preferred_element_type=F32) for x in xs]
        acts = [(_silu(g) * u).astype(BF16) for g, u in zip(gates, ups)]
        for k, act in enumerate(acts):
            o_ref[k * half:(k + 1) * half, :] = _pack_halves(jnp.dot(act, wdb_ref[...], preferred_element_type=F32))

    @pl.when(valid > half)
    def _():
        ffn(2)

    @pl.when((valid > 0) & (valid <= half))
    def _():
        ffn(1)
        o_ref[half:, :] = jnp.zeros((half, o_ref.shape[1]), o_ref.dtype)

    @pl.when(valid == 0)
    def _():
        o_ref[...] = jnp.zeros_like(o_ref)


def moe_experts(blocks, xs, w_gate, w_up, w_down, layer):
    n_slots, packed = xs.shape
    d = 2 * packed
    rows = MOE_ROWS
    ff = w_gate.shape[3]
    grid_spec = pltpu.PrefetchScalarGridSpec(
        num_scalar_prefetch=1,
        grid=(n_slots // rows,),
        in_specs=[
            pl.BlockSpec((rows, packed), lambda i, *_: (i, 0)),
            pl.BlockSpec(memory_space=pl.ANY),
            pl.BlockSpec(memory_space=pl.ANY),
            pl.BlockSpec(memory_space=pl.ANY),
        ],
        out_specs=pl.BlockSpec((rows, packed), lambda i, *_: (i, 0)),
        scratch_shapes=[pltpu.VMEM((2, d, ff), F32), pltpu.VMEM((2, d, ff), F32), pltpu.VMEM((2, ff, d), F32),
                        pltpu.VMEM((d, ff), BF16), pltpu.VMEM((d, ff), BF16), pltpu.VMEM((ff, d), BF16),
                        pltpu.SemaphoreType.DMA((2, 3))],
    )
    return pl.pallas_call(
        functools.partial(_expert_kernel, layer=layer),
        grid_spec=grid_spec,
        out_shape=jax.ShapeDtypeStruct((n_slots, packed), jnp.int32),
        compiler_params=_cparams("arbitrary"),
        name="moe_experts",
    )(blocks, xs, w_gate, w_up, w_down)


def _sc_mesh():
    return plsc.VectorSubcoreMesh(core_axis_name="c", subcore_axis_name="s",
                                  num_cores=SC_CORES, num_subcores=SC_SUBCORES)


def _sc_worker():
    return lax.axis_index("s") * SC_CORES + lax.axis_index("c")


def sc_scatter_rows(x, dest, n_slots):
    n_tok, d = x.shape
    per_worker = n_tok // SC_WORKERS
    n_chunks = per_worker // SC_CHUNK
    by_worker = dest.reshape(dest.shape[0] * SC_WORKERS, n_chunks, SC_CHUNK)

    @functools.partial(
        pl.kernel, mesh=_sc_mesh(), out_type=jax.ShapeDtypeStruct((n_slots, d), x.dtype),
        scratch_types=[pltpu.VMEM((n_chunks, SC_CHUNK), jnp.int32), pltpu.VMEM((n_chunks, SC_CHUNK), jnp.int32),
                       pltpu.VMEM((SC_CHUNK, d), x.dtype)],
        name="moe_scatter_rows")
    def scatter(x_hbm, dest_hbm, out_hbm, i0_v, i1_v, rows_v):
        wid = _sc_worker()
        pltpu.sync_copy(dest_hbm.at[wid], i0_v)
        pltpu.sync_copy(dest_hbm.at[SC_WORKERS + wid], i1_v)

        @pl.loop(0, n_chunks)
        def _(j):
            start = pl.multiple_of(wid * per_worker + j * SC_CHUNK, SC_CHUNK)
            pltpu.sync_copy(x_hbm.at[pl.ds(start, SC_CHUNK)], rows_v)
            pltpu.sync_copy(rows_v, out_hbm.at[i0_v.at[j]])
            pltpu.sync_copy(rows_v, out_hbm.at[i1_v.at[j]])

    return scatter(x, by_worker)


def sc_gather_rows(table, idx, n_out):
    d = table.shape[1]
    per_worker = n_out // SC_WORKERS
    n_chunks = per_worker // SC_CHUNK

    @functools.partial(
        pl.kernel, mesh=_sc_mesh(), out_type=jax.ShapeDtypeStruct((n_out, d), table.dtype),
        scratch_types=[pltpu.VMEM((n_chunks, SC_CHUNK), jnp.int32), pltpu.VMEM((SC_CHUNK, d), table.dtype)],
        name="moe_gather_rows")
    def gather(table_hbm, idx_hbm, out_hbm, idx_v, rows_v):
        wid = _sc_worker()
        pltpu.sync_copy(idx_hbm.at[wid], idx_v)

        @pl.loop(0, n_chunks)
        def _(j):
            start = pl.multiple_of(wid * per_worker + j * SC_CHUNK, SC_CHUNK)
            pltpu.sync_copy(table_hbm.at[idx_v.at[j]], rows_v)
            pltpu.sync_copy(rows_v, out_hbm.at[pl.ds(start, SC_CHUNK)])

    return gather(table, idx.reshape(-1, n_chunks, SC_CHUNK))


def _combine_kernel(h_ref, y0_ref, y1_ref, r_ref, g_ref, o_ref, *, final_norm):
    route = r_ref[...]
    h = h_ref[...] + (route[:, 2:3] * _unpack_halves(y0_ref[...]) + route[:, 3:4] * _unpack_halves(y1_ref[...]))
    o_ref[...] = _rms(h, g_ref[...]) if final_norm else h


def moe_combine(h, y01, route, g, final_norm, tm=1024):
    m, d = h.shape
    tm = min(tm, m)
    rows = lambda w: pl.BlockSpec((tm, w), lambda i: (i, 0))
    return pl.pallas_call(
        functools.partial(_combine_kernel, final_norm=final_norm),
        grid=(m // tm,),
        in_specs=[rows(d), rows(d // 2), pl.BlockSpec((tm, d // 2), lambda i: (i + m // tm, 0)), rows(LANES),
                  pl.BlockSpec((1, d), lambda i: (0, 0))],
        out_specs=rows(d),
        out_shape=jax.ShapeDtypeStruct((m, d), F32),
        compiler_params=_cparams("parallel"),
        name="moe_combine",
    )(h, y01, y01, route, g.reshape(1, d))


def _pad_cols(w):
    return jnp.pad(w, ((0, 0), (0, LANES - w.shape[1])))


def _plan_kernel(route_ref, cnt_ref, incl_ref, dest_ref, table_ref):
    f32_sum = lambda x, axis: jnp.sum(x, axis=axis, keepdims=True)
    lane = lax.broadcasted_iota(jnp.int32, (LANES, LANES), 1)
    sub = lax.broadcasted_iota(jnp.int32, (LANES, LANES), 0)
    incl = incl_ref[...]
    is_expert = (lane >= MOE_GROUPS) & (lane < MOE_GROUPS + MOE_EXPERTS)
    shift = MOE_ROWS.bit_length() - 1
    counts = jnp.broadcast_to(cnt_ref[...], (LANES, LANES)).astype(jnp.int32)
    padded = jnp.where(is_expert, ((counts + (MOE_ROWS - 1)) >> shift) << shift, 0)
    pad_end = _mm_sel_rhs(padded.astype(F32), incl)
    pad_start = pad_end - padded.astype(F32)

    route = route_ref[...]
    lane_t = lax.broadcasted_iota(jnp.int32, route.shape, 1)
    lane_f = lane_t.astype(F32)
    start_row = pad_start[0:1, :]
    slots = [f32_sum(jnp.where(lane_f == route[:, k:k + 1] + MOE_GROUPS, start_row, 0.0), 1) + route[:, 4 + k:5 + k]
             for k in range(2)]
    both = jnp.where(lane_t == 0, slots[0], jnp.where(lane_t == 1, slots[1], 0.0))
    dest_ref[...] = both.T[0:8, :].astype(jnp.int32)

    on_sub = lambda rows_equal: rows_equal.T
    expert_sub = (sub >= MOE_GROUPS) & (sub < MOE_GROUPS + MOE_EXPERTS)
    block_start = (lane * MOE_ROWS).astype(F32)
    eid = f32_sum(jnp.where(expert_sub & (on_sub(pad_end) <= block_start), 1.0, 0.0), 0)
    eid = jnp.minimum(eid, float(MOE_EXPERTS - 1))
    filled = on_sub(pad_start + counts.astype(F32))
    own = (sub - MOE_GROUPS).astype(F32) == eid
    valid = jnp.clip(f32_sum(jnp.where(own, filled, 0.0), 0) - block_start[0:1, :], 0.0, float(MOE_ROWS))
    eid_rows = jnp.broadcast_to(eid, (LANES, LANES))
    changed = (lane == 0) | (eid_rows != pltpu.roll(eid_rows, 1, axis=1))
    first = jnp.where((jnp.broadcast_to(valid, (LANES, LANES)) > 0) & changed, 1.0, 0.0)
    ordinal = _mm_sel_rhs(first, incl) - 1.0
    slot = ordinal - 2.0 * jnp.floor(ordinal * 0.5)
    later = (on_sub(first) > 0) & (sub > lane)
    nearest = jnp.min(jnp.where(later, sub, LANES), axis=0, keepdims=True)
    next_eid = f32_sum(jnp.where(sub == nearest, on_sub(eid_rows), 0.0), 0)
    next_eid = jnp.where(nearest < LANES, next_eid, -1.0)
    row8 = lax.broadcasted_iota(jnp.int32, (8, LANES), 0)
    table = jnp.zeros((8, LANES), F32)
    for k, val in enumerate((eid, valid, first[0:1, :], slot[0:1, :], next_eid)):
        table = jnp.where(row8 == k, val, table)
    table_ref[...] = table.astype(jnp.int32)


def moe_plan(route, counts, tm=2048):
    n_tok = route.shape[0]
    tm = min(tm, n_tok)
    idx = np.arange(LANES)
    incl = jnp.asarray(idx[:, None] <= idx[None, :], BF16)
    return pl.pallas_call(
        _plan_kernel,
        grid=(n_tok // tm,),
        in_specs=[pl.BlockSpec((tm, LANES), lambda i: (i, 0)), pl.BlockSpec((1, LANES), lambda i: (0, 0)),
                  pl.BlockSpec((LANES, LANES), lambda i: (0, 0))],
        out_specs=[pl.BlockSpec((8, tm), lambda i: (0, i)), pl.BlockSpec((8, LANES), lambda i: (0, 0))],
        out_shape=[jax.ShapeDtypeStruct((8, n_tok), jnp.int32), jax.ShapeDtypeStruct((8, LANES), jnp.int32)],
        compiler_params=_cparams("arbitrary"),
        name="moe_plan",
    )(route, counts, incl)


def _router_weights(w_group, b_group, w_expert, b_expert):
    w_r = _pad_cols(jnp.concatenate([w_group, w_expert], axis=1))
    w_hi = w_r.astype(BF16)
    w_lo = (w_r - w_hi.astype(F32)).astype(BF16)
    return w_hi, w_lo, _pad_lanes(jnp.concatenate([b_group, b_expert]))


def _moe_layer(h, xn, route, counts, w_gate, w_up, w_down, layer, final_g):
    n_tok, d = h.shape
    n_blocks = -(-(2 * n_tok + MOE_EXPERTS * (MOE_ROWS - 1)) // MOE_ROWS)
    dest, blocks = moe_plan(route, counts)
    xs = sc_scatter_rows(xn, dest, n_blocks * MOE_ROWS)
    ys = moe_experts(blocks, xs, w_gate, w_up, w_down, layer)
    y01 = sc_gather_rows(ys, dest, 2 * n_tok)
    g = jnp.ones((d,), F32) if final_g is None else final_g
    return moe_combine(h, y01, route, g, final_g is not None)


def _memory_kv(memn_in, mem_norm, wk, wv):
    bsz, m, d = memn_in.shape
    w = jnp.concatenate([wk, wv], axis=1).astype(BF16)
    kv, _ = rms_matmul(memn_in.reshape(bsz * m, d), mem_norm, w, jnp.zeros((d, LANES), BF16))
    k = kv[:, :d].reshape(bsz, m, d)
    v = kv[:, d:].reshape(bsz, m, d)
    return jnp.swapaxes(k, 1, 2).astype(BF16), v.astype(BF16)


def kernel(x, mem, mem_norm, final_norm, norm_mix, norm_xa, norm_ffn, xa_wq, xa_wk, xa_wv, xa_wo, moe_w_group, moe_b_group, moe_w_expert, moe_b_expert, moe_w_gate, moe_w_up, moe_w_down, ev_w_in, ev_sc_conv, ev_ssm_conv_w, ev_ssm_conv_b, ev_ssm_dt_bias, ev_ssm_a_log, ev_ssm_d, ev_ssm_norm, ev_w_out, od_w_in, od_gdn_conv, od_gdn_dt_bias, od_gdn_a_log, od_gdn_norm, od_w_out):
    bsz, length, d = x.shape
    n_tok = bsz * length
    depth = norm_mix.shape[0]
    h = x.reshape(n_tok, d)
    for layer in range(depth):
        i = layer // 2
        if layer % 2 == 0:
            w = ev_w_in[i]
            z0 = 3 * SC_DIM
            xbc0 = z0 + SSM_INNER
            w_conv = w[:, xbc0:xbc0 + SSM_XBC].astype(BF16)
            w_main = jnp.concatenate([w[:, z0:xbc0], w[:, :z0]], axis=1).astype(BF16)
            w_small = _pad_cols(w[:, xbc0 + SSM_XBC:]).astype(BF16)
            xbc = rms_matmul_conv(h, norm_mix[layer], w_conv, ev_ssm_conv_w[i], ev_ssm_conv_b[i], length)
            y, small = rms_matmul(h, norm_mix[layer], w_main, w_small, tm=1024, tn=w_main.shape[1])
            y3 = y.reshape(bsz, length, -1)
            small3 = small.reshape(bsz, length, LANES)
            smallt = jnp.swapaxes(small3[:, :, :16], 1, 2)
            ya = short_conv_mixer(y3, ev_sc_conv[i], SSM_INNER // SC_DIM)
            yb = ssd_mixer(xbc.reshape(bsz, length, -1), y3, small3, smallt, ev_ssm_dt_bias[i],
                           ev_ssm_a_log[i], ev_ssm_d[i], ev_ssm_norm[i])
            w_out = ev_w_out[i].astype(BF16)
            split = SC_DIM
        else:
            w = od_w_in[i]
            qkv_w = 3 * GDN_HEADS * GDN_D
            z_end = qkv_w + GDN_HEADS * GDN_D
            w_conv = w[:, :qkv_w].astype(BF16)
            w_main = jnp.concatenate([w[:, qkv_w:z_end], w[:, z_end + 2 * GDN_HEADS:]], axis=1).astype(BF16)
            w_small = _pad_cols(w[:, z_end:z_end + 2 * GDN_HEADS]).astype(BF16)
            qkv = rms_matmul_conv(h, norm_mix[layer], w_conv, od_gdn_conv[i], jnp.zeros((qkv_w,), F32), length)
            y, small = rms_matmul(h, norm_mix[layer], w_main, w_small, tm=1024, tn=w_main.shape[1])
            y3 = y.reshape(bsz, length, -1)
            small3 = small.reshape(bsz, length, LANES)
            smallt = jnp.swapaxes(small3[:, :, :16], 1, 2)
            ya = gated_deltanet_mixer(qkv.reshape(bsz, length, -1), y3, small3, smallt, od_gdn_dt_bias[i],
                                      od_gdn_a_log[i], od_gdn_norm[i])
            yb = stick_breaking_mixer(y3, GDN_HEADS * GDN_D)
            w_out = od_w_out[i].astype(BF16)
            split = GDN_HEADS * GDN_D
        kt, v = _memory_kv(mem, mem_norm, xa_wk[layer], xa_wv[layer])
        w_hi, w_lo, bias = _router_weights(moe_w_group[layer], moe_b_group[layer], moe_w_expert[layer],
                                           moe_b_expert[layer])
        h, xn, route, counts = post_mixer(
            ya.reshape(n_tok, -1), yb.reshape(n_tok, -1), h, w_out[:split], w_out[split:], norm_xa[layer],
            xa_wq[layer].astype(BF16), kt, v, xa_wo[layer].astype(BF16), norm_ffn[layer], w_hi, w_lo, bias)
        h = _moe_layer(h, xn, route, counts, moe_w_gate, moe_w_up, moe_w_down, layer,
                       final_norm if layer == depth - 1 else None)
    return h.reshape(bsz, length, d)
```

```python
import functools

import jax
import jax.numpy as jnp
import numpy as np
from jax import lax
from jax.experimental import pallas as pl
from jax.experimental.pallas import tpu as pltpu
from jax.experimental.pallas import tpu_sc as plsc

F32 = jnp.float32
BF16 = jnp.bfloat16
EPS = 1e-6

D_MODEL = 1024
MEM_LEN = 256
SC_DIM = 512
SSM_HEADS = 16
SSM_HEAD_DIM = 64
SSM_INNER = 1024
SSM_GROUPS = 2
SSM_STATE = 128
SSM_XBC = SSM_INNER + 2 * SSM_GROUPS * SSM_STATE
SSD_CHUNK = 128
SSD_STEP_ROWS = 512
GDN_HEADS = 8
GDN_D = 128
GDN_CHUNK = 64
GDN_TILE = 128
GDN_STEP_ROWS = 512
SB_HEADS = 8
SB_HEAD_DIM = 64
SB_DIM = 512
SB_BLOCK = 128
SB_STEP_HEADS = 8
XA_HEADS = 4
XA_HEAD_DIM = 256
MOE_GROUPS = 4
MOE_PER_GROUP = 8
MOE_EXPERTS = 32
MOE_FF = 512
MOE_ROWS = 512
SC_CORES = 2
SC_SUBCORES = 16
SC_WORKERS = SC_CORES * SC_SUBCORES
SC_CHUNK = 64
HALO = 8
CONV_CHUNK = 512
LANES = 128
SB_LOG_ZERO = -104.0
VMEM_LIMIT = 56 * 1024 * 1024


def _cparams(*sem):
    return pltpu.CompilerParams(dimension_semantics=sem, vmem_limit_bytes=VMEM_LIMIT)


def _mm(a, b):
    return jnp.dot(a.astype(BF16), b.astype(BF16), preferred_element_type=F32)


def _mm_nt(a, b):
    return lax.dot_general(a.astype(BF16), b.astype(BF16), (((1,), (1,)), ((), ())),
                           preferred_element_type=F32)


def _split_bf16(x, n):
    parts, r = [], x
    for _ in range(n):
        p = r.astype(BF16)
        parts.append(p)
        r = r - p.astype(F32)
    return parts


def _mm_sel_rhs(x, sel, n=3):
    return sum(jnp.dot(p, sel, preferred_element_type=F32) for p in _split_bf16(x, n))


def _mm_sel_lhs(sel, x, n=3):
    return sum(jnp.dot(sel, p, preferred_element_type=F32) for p in _split_bf16(x, n))


def _spread_heads(x, first, n_heads, width):
    rows = x.shape[0]
    col = lambda h: jnp.broadcast_to(x[:, first + h:first + h + 1], (rows, LANES))
    if width == LANES:
        return jnp.concatenate([col(h) for h in range(n_heads)], axis=1)
    left = lax.broadcasted_iota(jnp.int32, (rows, LANES), 1) < width
    return jnp.concatenate([jnp.where(left, col(h), col(h + 1)) for h in range(0, n_heads, 2)], axis=1)


def _pack_halves(x):
    n = x.shape[1] // 2
    lo = pltpu.bitcast(x[:, :n].astype(BF16).astype(F32), jnp.int32)
    hi = pltpu.bitcast(x[:, n:].astype(BF16).astype(F32), jnp.int32)
    return lax.shift_right_logical(lo, 16) | (hi & jnp.int32(-65536))


def _unpack_halves(p):
    lo = pltpu.bitcast(lax.shift_left(p, 16), F32)
    hi = pltpu.bitcast(p & jnp.int32(-65536), F32)
    return jnp.concatenate([lo, hi], axis=1)


def _silu(x):
    return x * jax.nn.sigmoid(x)


def _softplus(x):
    return jnp.maximum(x, 0.0) + jnp.log(1.0 + jnp.exp(-jnp.abs(x)))


def _rms(x, g):
    return x * lax.rsqrt(jnp.mean(x * x, axis=-1, keepdims=True) + EPS) * g


def _rms_matmul_kernel(x_ref, g_ref, w_ref, ws_ref, o_ref, os_ref):
    xn = _rms(x_ref[...], g_ref[...]).astype(BF16)
    o_ref[...] = jnp.dot(xn, w_ref[...], preferred_element_type=F32)
    os_ref[...] = jnp.dot(xn, ws_ref[...], preferred_element_type=F32)


def rms_matmul(x, g, w, ws, tm=512, tn=512):
    m, k = x.shape
    n = w.shape[1]
    tm = min(tm, m)
    main, small = pl.pallas_call(
        _rms_matmul_kernel,
        grid=(n // tn, m // tm),
        in_specs=[
            pl.BlockSpec((tm, k), lambda j, i: (i, 0)),
            pl.BlockSpec((1, k), lambda j, i: (0, 0)),
            pl.BlockSpec((k, tn), lambda j, i: (0, j)),
            pl.BlockSpec((k, LANES), lambda j, i: (0, 0)),
        ],
        out_specs=[
            pl.BlockSpec((tm, tn), lambda j, i: (i, j)),
            pl.BlockSpec((None, tm, LANES), lambda j, i: (j, i, 0)),
        ],
        out_shape=[jax.ShapeDtypeStruct((m, n), F32), jax.ShapeDtypeStruct((n // tn, m, LANES), F32)],
        compiler_params=_cparams("parallel", "parallel"),
        name="rms_matmul",
    )(x, g.reshape(1, k), w, ws)
    return main, small[0]


def _causal_conv(ext_ref, w_ref, rows):
    width = w_ref.shape[0]
    ext = ext_ref[...]
    acc = None
    for j in range(width):
        shift = width - 1 - j
        moved = ext if shift == 0 else pltpu.roll(ext, shift, axis=0)
        term = w_ref[j:j + 1, :] * moved[HALO:HALO + rows, :]
        acc = term if acc is None else acc + term
    return acc


def _rms_matmul_conv_kernel(x_ref, g_ref, w_ref, cw_ref, cb_ref, o_ref, *ext_refs, tiles_per_seq):
    tm = x_ref.shape[0]
    starts_sequence = pl.program_id(1) % tiles_per_seq == 0

    @pl.when(starts_sequence)
    def _():
        for ext_ref in ext_refs:
            ext_ref[0:HALO, :] = jnp.zeros((HALO, CONV_CHUNK), F32)

    @pl.when(jnp.logical_not(starts_sequence))
    def _():
        for ext_ref in ext_refs:
            ext_ref[0:HALO, :] = ext_ref[tm:tm + HALO, :]

    xn = _rms(x_ref[...], g_ref[...]).astype(BF16)
    for c, ext_ref in enumerate(ext_refs):
        cols = slice(c * CONV_CHUNK, (c + 1) * CONV_CHUNK)
        ext_ref[HALO:, :] = jnp.dot(xn, w_ref[:, cols], preferred_element_type=F32)
        o_ref[:, cols] = _silu(_causal_conv(ext_ref, cw_ref.at[:, cols], tm) + cb_ref[:, cols])


def rms_matmul_conv(x, g, w, conv_w, conv_b, seq_len, tm=1024, tn=1536):
    m, k = x.shape
    n = w.shape[1]
    cols = lambda rows: pl.BlockSpec((rows, tn), lambda j, i: (0, j))
    return pl.pallas_call(
        functools.partial(_rms_matmul_conv_kernel, tiles_per_seq=seq_len // tm),
        grid=(n // tn, m // tm),
        in_specs=[
            pl.BlockSpec((tm, k), lambda j, i: (i, 0)),
            pl.BlockSpec((1, k), lambda j, i: (0, 0)),
            cols(k), cols(conv_w.shape[0]), cols(1),
        ],
        out_specs=pl.BlockSpec((tm, tn), lambda j, i: (i, j)),
        out_shape=jax.ShapeDtypeStruct((m, n), F32),
        scratch_shapes=[pltpu.VMEM((tm + HALO, CONV_CHUNK), F32)] * (tn // CONV_CHUNK),
        compiler_params=_cparams("arbitrary", "arbitrary"),
        name="rms_matmul_conv",
    )(x, g.reshape(1, k), w, conv_w, conv_b.reshape(1, n))


def _rms_matmul_gated_kernel(x_ref, g_ref, wz_ref, wbcx_ref, ws_ref, cw_ref, z_ref, ya_ref, os_ref, ext_ref,
                             *, tiles_per_seq):
    tm = x_ref.shape[0]
    starts_sequence = pl.program_id(0) % tiles_per_seq == 0

    @pl.when(starts_sequence)
    def _():
        ext_ref[0:HALO, :] = jnp.zeros((HALO, SC_DIM), F32)

    @pl.when(jnp.logical_not(starts_sequence))
    def _():
        ext_ref[0:HALO, :] = ext_ref[tm:tm + HALO, :]

    xn = _rms(x_ref[...], g_ref[...]).astype(BF16)
    z_ref[...] = jnp.dot(xn, wz_ref[...], preferred_element_type=F32)
    os_ref[...] = jnp.dot(xn, ws_ref[...], preferred_element_type=F32)
    bcx = jnp.dot(xn, wbcx_ref[...], preferred_element_type=F32)
    ext_ref[HALO:, :] = bcx[:, SC_DIM:2 * SC_DIM] * bcx[:, 2 * SC_DIM:]
    ya_ref[...] = bcx[:, :SC_DIM] * _causal_conv(ext_ref, cw_ref, tm)


def rms_matmul_gated(x, g, w_z, w_bcx, w_small, conv_w, seq_len, tm=1024):
    m, k = x.shape
    const = lambda a: pl.BlockSpec(a.shape, lambda i: (0,) * a.ndim)
    rows = lambda w: pl.BlockSpec((tm, w), lambda i: (i, 0))
    g = g.reshape(1, k)
    return pl.pallas_call(
        functools.partial(_rms_matmul_gated_kernel, tiles_per_seq=seq_len // tm),
        grid=(m // tm,),
        in_specs=[rows(k), const(g), const(w_z), const(w_bcx), const(w_small), const(conv_w)],
        out_specs=[rows(w_z.shape[1]), rows(SC_DIM), rows(LANES)],
        out_shape=[jax.ShapeDtypeStruct((m, w_z.shape[1]), F32), jax.ShapeDtypeStruct((m, SC_DIM), F32),
                   jax.ShapeDtypeStruct((m, LANES), F32)],
        scratch_shapes=[pltpu.VMEM((tm + HALO, SC_DIM), F32)],
        compiler_params=_cparams("arbitrary"),
        name="rms_matmul_gated",
    )(x, g, w_z, w_bcx, w_small, conv_w)


def _ssd_kernel(xbc_ref, z_ref, dt_ref, dtt_ref, dtb_r_ref, dtb_c_ref,
                alog_r_ref, alog_c_ref, d_ref, nw_ref, tri_ref, trit_ref,
                o_ref, s_ref):
    q = SSD_CHUNK

    @pl.when(pl.program_id(1) == 0)
    def _():
        s_ref[...] = jnp.zeros_like(s_ref)

    for sub in range(xbc_ref.shape[0] // q):
        rows = slice(sub * q, (sub + 1) * q)
        _ssd_chunk(xbc_ref[rows, :], z_ref[rows, :], dt_ref[rows, :], dtt_ref[:, rows], dtb_r_ref, dtb_c_ref,
                   alog_r_ref, alog_c_ref, d_ref, nw_ref, tri_ref, trit_ref, o_ref.at[rows, :], s_ref)


def _ssd_chunk(xbc, z, dt_raw, dtt_raw, dtb_r_ref, dtb_c_ref, alog_r_ref, alog_c_ref, d_ref, nw_ref, tri_ref,
               trit_ref, o_ref, s_ref):
    q = SSD_CHUNK
    hpg = SSM_HEADS // SSM_GROUPS
    gw = hpg * SSM_HEAD_DIM
    xs = xbc[:, :SSM_INNER]
    bm = xbc[:, SSM_INNER:SSM_INNER + SSM_GROUPS * SSM_STATE]
    cm = xbc[:, SSM_INNER + SSM_GROUPS * SSM_STATE:]

    dt = _softplus(dt_raw + dtb_r_ref[...])
    acs = _mm_sel_lhs(tri_ref[...], dt * -jnp.exp(alog_r_ref[...]))
    dtt = _softplus(dtt_raw + dtb_c_ref[...])
    acst = _mm_sel_rhs(dtt * -jnp.exp(alog_c_ref[...]), trit_ref[...])
    dt_full = _spread_heads(dt, 0, SSM_HEADS, SSM_HEAD_DIM)
    acs_full = _spread_heads(acs, 0, SSM_HEADS, SSM_HEAD_DIM)
    acs_col = _spread_heads(acs, 0, SSM_HEADS, q)

    xdt = xs * dt_full
    acs_last = acs_full[q - 1:q, :]
    xw = xdt * jnp.exp(acs_last - acs_full)
    chunk_decay = jnp.exp(acs_last)

    row = lax.broadcasted_iota(jnp.int32, (q, q), 0)
    col = lax.broadcasted_iota(jnp.int32, (q, q), 1)
    causal = row >= col
    lane = lax.broadcasted_iota(jnp.int32, (q, 2 * SSM_HEAD_DIM), 1)

    y_diag, y_off = [], []
    for g in range(SSM_GROUPS):
        bm_g = bm[:, g * SSM_STATE:(g + 1) * SSM_STATE]
        cm_g = cm[:, g * SSM_STATE:(g + 1) * SSM_STATE]
        cb_g = _mm_nt(cm_g, bm_g)
        state = s_ref[g]
        y_off.append(_mm(cm_g, state))
        s_ref[g] = state * chunk_decay[:, g * gw:(g + 1) * gw] + _mm(bm_g.T, xw[:, g * gw:(g + 1) * gw])
        for pair in range(hpg // 2):
            h0 = g * hpg + 2 * pair
            xdt_pair = xdt[:, h0 * SSM_HEAD_DIM:(h0 + 2) * SSM_HEAD_DIM]
            weights = []
            for h in (h0, h0 + 1):
                seg = acs_col[:, h * q:(h + 1) * q] - acst[h:h + 1, :]
                weights.append(cb_g * jnp.where(causal, jnp.exp(seg), 0.0))
            both = _mm(jnp.concatenate(weights, axis=0), xdt_pair)
            y_diag.append(jnp.where(lane < SSM_HEAD_DIM, both[:q], both[q:]))
    y = (jnp.concatenate(y_diag, axis=1) + jnp.concatenate(y_off, axis=1) * jnp.exp(acs_full)
         + xs * d_ref[...])
    y = y * _silu(z)
    halves = []
    for g in range(SSM_GROUPS):
        yg = y[:, g * gw:(g + 1) * gw]
        halves.append(yg * lax.rsqrt(jnp.mean(yg * yg, axis=-1, keepdims=True) + EPS))
    o_ref[...] = jnp.concatenate(halves, axis=1) * nw_ref[...]


def _pad_lanes(v, fill=0.0):
    return jnp.pad(v.astype(F32), (0, LANES - v.shape[0]), constant_values=fill).reshape(1, LANES)


def _pad_col(v, rows=16):
    return jnp.pad(v.astype(F32), (0, rows - v.shape[0])).reshape(rows, 1)


def ssd_mixer(xbc3, y3, small3, smallt, dt_bias, a_log, d_skip, norm_w):
    bsz, length, _ = y3.shape
    q = SSD_CHUNK
    tri = jnp.asarray(np.tril(np.ones((q, q), np.float32)), BF16)
    trit = jnp.asarray(np.triu(np.ones((q, q), np.float32)), BF16)
    d_full = jnp.repeat(d_skip.astype(F32), SSM_HEAD_DIM).reshape(1, SSM_INNER)
    const = lambda a: pl.BlockSpec(a.shape, lambda b, c: (0,) * a.ndim)
    args = [_pad_lanes(dt_bias), _pad_col(dt_bias), _pad_lanes(a_log),
            _pad_col(a_log), d_full, norm_w.reshape(1, -1), tri, trit]
    rows = min(SSD_STEP_ROWS, length)
    return pl.pallas_call(
        _ssd_kernel,
        grid=(bsz, length // rows),
        in_specs=[
            pl.BlockSpec((None, rows, SSM_XBC), lambda b, c: (b, c, 0)),
            pl.BlockSpec((None, rows, SSM_INNER), lambda b, c: (b, c, 0)),
            pl.BlockSpec((None, rows, LANES), lambda b, c: (b, c, 0)),
            pl.BlockSpec((None, 16, rows), lambda b, c: (b, 0, c)),
        ] + [const(a) for a in args],
        out_specs=pl.BlockSpec((None, rows, SSM_INNER), lambda b, c: (b, c, 0)),
        out_shape=jax.ShapeDtypeStruct((bsz, length, SSM_INNER), F32),
        scratch_shapes=[pltpu.VMEM((SSM_GROUPS, SSM_STATE, SSM_INNER // SSM_GROUPS), F32)],
        compiler_params=_cparams("parallel", "arbitrary"),
        name="ssd_mixer",
    )(xbc3, y3, small3, smallt, *args)


def _unit_lower_inverse(mats, row, col):
    eye = jnp.where(row == col, 1.0, 0.0)
    blk = lambda n: (row >> (n.bit_length() - 1)) == (col >> (n.bit_length() - 1))
    size = row.shape[0]
    p = [jnp.where(blk(16), -a, 0.0) for a in mats]
    t = [eye + x for x in p]
    p = [_mm(x, x) for x in p]
    for _ in range(2):
        both = [_mm(jnp.concatenate([x, y], axis=0), x) for x, y in zip(p, t)]
        p = [b[:size] for b in both]
        t = [y + b[size:] for y, b in zip(t, both)]
    t = [y + _mm(y, x) for y, x in zip(t, p)]
    for n in (16, 32):
        band = blk(2 * n) & jnp.logical_not(blk(n))
        left = [_mm(y, jnp.where(band, a, 0.0)) for y, a in zip(t, mats)]
        t = [y - _mm(x, y) for y, x in zip(t, left)]
    return t


def _gdn_kernel(qkv_ref, z_ref, ab_ref, abt_ref, dtb_r_ref, dtb_c_ref, alog_r_ref,
                alog_c_ref, nw_ref, tri_ref, trit_ref, o_ref, s_ref):
    n = GDN_TILE

    @pl.when(pl.program_id(1) == 0)
    def _():
        s_ref[...] = jnp.zeros_like(s_ref)

    for sub in range(qkv_ref.shape[0] // n):
        rows = slice(sub * n, (sub + 1) * n)
        _gdn_tile(qkv_ref[rows, :], z_ref[rows, :], ab_ref[rows, :], abt_ref[:, rows], dtb_r_ref, dtb_c_ref,
                  alog_r_ref, alog_c_ref, nw_ref, tri_ref, trit_ref, o_ref.at[rows, :], s_ref)


def _gdn_tile(qkv, z, ab, abt, dtb_r_ref, dtb_c_ref, alog_r_ref, alog_c_ref, nw_ref, tri_ref, trit_ref, o_ref, s_ref):
    n = GDN_TILE
    c = GDN_CHUNK
    d = GDN_D
    hd = GDN_HEADS * d
    g = -jnp.exp(alog_r_ref[...]) * _softplus(ab + dtb_r_ref[...])
    gc_full = _spread_heads(_mm_sel_lhs(tri_ref[...], g), 0, GDN_HEADS, d)
    beta_full = _spread_heads(jax.nn.sigmoid(ab), GDN_HEADS, GDN_HEADS, d)
    gt = -jnp.exp(alog_c_ref[...]) * _softplus(abt + dtb_c_ref[...])
    gct = _mm_sel_rhs(gt, trit_ref[...])

    row = lax.broadcasted_iota(jnp.int32, (n, n), 0)
    col = lax.broadcasted_iota(jnp.int32, (n, n), 1)
    same = (row >> (c.bit_length() - 1)) == (col >> (c.bit_length() - 1))
    incl = same & (row >= col)
    strict = same & (row > col)
    zeros_half = jnp.zeros((c, d), F32)

    heads = range(GDN_HEADS)
    sl = [slice(h * d, (h + 1) * d) for h in heads]
    l2n = lambda x: x * lax.rsqrt(jnp.sum(x * x, axis=-1, keepdims=True) + EPS)
    qn = [l2n(qkv[:, sl[h]]) * (d ** -0.5) for h in heads]
    kn = [l2n(qkv[:, hd + h * d:hd + (h + 1) * d]) for h in heads]
    vh = [qkv[:, 2 * hd + h * d:2 * hd + (h + 1) * d] for h in heads]
    gcol = [gc_full[:, sl[h]] for h in heads]
    beta = [beta_full[:, sl[h]] for h in heads]
    edec = [jnp.exp(gcol[h] - gct[h:h + 1, :]) for h in heads]
    egc = [jnp.exp(x) for x in gcol]
    kb = [kn[h] * beta[h] for h in heads]
    on_k = [_mm_nt(jnp.concatenate([kb[h], qn[h]], axis=0), kn[h]) for h in heads]
    lower = [jnp.where(strict, on_k[h][:n] * edec[h], 0.0) for h in heads]
    aqk = [jnp.where(incl, on_k[h][n:] * edec[h], 0.0) for h in heads]
    tinv = _unit_lower_inverse(lower, row, col)
    sol = [_mm(tinv[h], jnp.concatenate([vh[h] * beta[h], kb[h] * egc[h]], axis=1)) for h in heads]
    qd = [qn[h] * egc[h] for h in heads]
    glast = [(gcol[h][c - 1:c, :], gcol[h][n - 1:n, :]) for h in heads]
    kdt = [(kn[h] * jnp.exp(jnp.concatenate([jnp.broadcast_to(glast[h][0], (c, d)),
                                             jnp.broadcast_to(glast[h][1], (c, d))], axis=0) - gcol[h])).T
           for h in heads]
    s0 = [s_ref[h] for h in heads]
    on_s0 = [_mm(jnp.concatenate([sol[h][:c, d:], qd[h][:c]], axis=0), s0[h]) for h in heads]
    v0 = [sol[h][:c, :d] - on_s0[h][:c] for h in heads]
    s1 = [s0[h] * jnp.exp(glast[h][0]) + _mm(kdt[h], jnp.concatenate([v0[h], zeros_half], axis=0)) for h in heads]
    on_s1 = [_mm(jnp.concatenate([sol[h][c:, d:], qd[h][c:]], axis=0), s1[h]) for h in heads]
    v1 = [sol[h][c:, :d] - on_s1[h][:c] for h in heads]
    for h in heads:
        s_ref[h] = s1[h] * jnp.exp(glast[h][1]) + _mm(kdt[h], jnp.concatenate([zeros_half, v1[h]], axis=0))
    outs = []
    for h in heads:
        o = (jnp.concatenate([on_s0[h][c:], on_s1[h][c:]], axis=0)
             + _mm(aqk[h], jnp.concatenate([v0[h], v1[h]], axis=0)))
        o = o * lax.rsqrt(jnp.mean(o * o, axis=-1, keepdims=True) + EPS) * nw_ref[...]
        outs.append(o * _silu(z[:, sl[h]]))
    o_ref[...] = jnp.concatenate(outs, axis=1)


def gated_deltanet_mixer(qkv3, y3, small3, smallt, dt_bias, a_log, norm_w):
    bsz, length, _ = y3.shape
    n = GDN_TILE
    hd = GDN_HEADS * GDN_D
    idx = np.arange(n)
    same = (idx[:, None] // GDN_CHUNK) == (idx[None, :] // GDN_CHUNK)
    tri = jnp.asarray(same & (idx[:, None] >= idx[None, :]), BF16)
    trit = jnp.asarray(same & (idx[:, None] <= idx[None, :]), BF16)
    const = lambda a: pl.BlockSpec(a.shape, lambda b, c: (0,) * a.ndim)
    args = [_pad_lanes(dt_bias), _pad_col(dt_bias), _pad_lanes(a_log), _pad_col(a_log),
            norm_w.reshape(1, -1), tri, trit]
    rows = min(GDN_STEP_ROWS, length)
    return pl.pallas_call(
        _gdn_kernel,
        grid=(bsz, length // rows),
        in_specs=[
            pl.BlockSpec((None, rows, 3 * hd), lambda b, c: (b, c, 0)),
            pl.BlockSpec((None, rows, hd), lambda b, c: (b, c, 0)),
            pl.BlockSpec((None, rows, LANES), lambda b, c: (b, c, 0)),
            pl.BlockSpec((None, 16, rows), lambda b, c: (b, 0, c)),
        ] + [const(a) for a in args],
        out_specs=pl.BlockSpec((None, rows, hd), lambda b, c: (b, c, 0)),
        out_shape=jax.ShapeDtypeStruct((bsz, length, hd), F32),
        scratch_shapes=[pltpu.VMEM((GDN_HEADS, GDN_D, GDN_D), F32)],
        compiler_params=_cparams("parallel", "arbitrary"),
        name="gated_deltanet",
    )(qkv3, y3, small3, smallt, *args)


def _sb_kernel(q_ref, k_ref, v_ref, upper_ref, o_ref):
    blk = SB_BLOCK
    pair_w = 2 * SB_HEAD_DIM
    n_pairs = SB_STEP_HEADS // 2
    i = pl.program_id(2)
    q = q_ref[...] * (SB_HEAD_DIM ** -0.5)
    lane = lax.broadcasted_iota(jnp.int32, (blk, pair_w), 1)
    first_head = lane < SB_HEAD_DIM
    qs = []
    for p in range(n_pairs):
        q2 = q[:, p * pair_w:(p + 1) * pair_w]
        qs += [jnp.where(first_head, q2, 0.0).astype(BF16), jnp.where(first_head, 0.0, q2).astype(BF16)]
    row = lax.broadcasted_iota(jnp.int32, (blk, blk), 0)
    col = lax.broadcasted_iota(jnp.int32, (blk, blk), 1)
    earlier = col < row
    upper = upper_ref[...]
    heads = range(SB_STEP_HEADS)

    def key_block(kb, accs, sticks, diagonal):
        start = pl.multiple_of(kb * blk, blk)
        k = k_ref[pl.ds(start, blk), :].astype(BF16)
        v = v_ref[pl.ds(start, blk), :].astype(BF16)
        kp = [k[:, p * pair_w:(p + 1) * pair_w] for p in range(n_pairs)]
        vp = [v[:, p * pair_w:(p + 1) * pair_w] for p in range(n_pairs)]
        logits = [lax.dot_general(qs[h], kp[h // 2], (((1,), (1,)), ((), ())), preferred_element_type=F32)
                  for h in heads]
        log_keep = [-_softplus(x) for x in logits]
        if diagonal:
            log_keep = [jnp.where(earlier, x, 0.0) for x in log_keep]
        between = [_mm_sel_rhs(log_keep[h], upper, 2) + sticks[h] for h in heads]
        w = [jnp.exp(logits[h] + log_keep[h] + between[h]) for h in heads]
        if diagonal:
            w = [jnp.where(earlier, x, 0.0) for x in w]
        pv = [jnp.dot(w[h].astype(BF16), vp[h // 2], preferred_element_type=F32) for h in heads]
        accs = tuple(accs[p] + jnp.where(first_head, pv[2 * p], pv[2 * p + 1]) for p in range(n_pairs))
        sticks = tuple(sticks[h] + jnp.sum(log_keep[h], axis=-1, keepdims=True) for h in heads)
        return accs, sticks

    accs = tuple(jnp.zeros((blk, pair_w), F32) for _ in range(n_pairs))
    sticks = tuple(jnp.zeros((blk, 1), F32) for _ in heads)
    accs, sticks = key_block(i, accs, sticks, True)

    def alive(state):
        kb, _, sticks = state
        longest = sticks[0]
        for s in sticks[1:]:
            longest = jnp.maximum(longest, s)
        return (kb >= 0) & (jnp.max(longest) > SB_LOG_ZERO)

    def body(state):
        kb, accs, sticks = state
        accs, sticks = key_block(kb, accs, sticks, False)
        return kb - 1, accs, sticks

    _, accs, _ = lax.while_loop(alive, body, (i - 1, accs, sticks))
    o_ref[...] = jnp.concatenate(accs, axis=1)


def stick_breaking_mixer(y3, col0):
    bsz, length, _ = y3.shape
    blk = SB_BLOCK
    step_w = SB_STEP_HEADS * SB_HEAD_DIM
    steps = SB_DIM // step_w
    q0 = col0 // step_w
    idx = np.arange(blk)
    upper = jnp.asarray(idx[:, None] > idx[None, :], BF16)
    resident = lambda off: pl.BlockSpec((None, length, step_w), lambda b, p, i: (b, 0, q0 + off + p),
                                        pipeline_mode=pl.Buffered(1))
    return pl.pallas_call(
        _sb_kernel,
        grid=(bsz, steps, length // blk),
        in_specs=[
            pl.BlockSpec((None, blk, step_w), lambda b, p, i: (b, i, q0 + p)),
            resident(steps),
            resident(2 * steps),
            pl.BlockSpec((blk, blk), lambda b, p, i: (0, 0)),
        ],
        out_specs=pl.BlockSpec((None, blk, step_w), lambda b, p, i: (b, i, p)),
        out_shape=jax.ShapeDtypeStruct((bsz, length, SB_DIM), F32),
        compiler_params=_cparams("parallel", "parallel", "arbitrary"),
        name="stick_breaking",
    )(y3, y3, y3, upper)


def _mixer_out(a_ref, b_ref, h_ref, wa_ref, wb_ref, rows):
    return h_ref[rows, :] + (jnp.dot(a_ref[rows, :].astype(BF16), wa_ref[...], preferred_element_type=F32)
                             + jnp.dot(b_ref[rows, :].astype(BF16), wb_ref[...], preferred_element_type=F32))


def _cross_attention(hs, g_ref, wq_ref, kt_ref, v_ref, wo_ref):
    us = [_rms(h, g_ref[...]).astype(BF16) for h in hs]
    qs = [jnp.dot(u, wq_ref[...], preferred_element_type=F32) for u in us]
    heads = [[] for _ in hs]
    for hd in range(XA_HEADS):
        sl = slice(hd * XA_HEAD_DIM, (hd + 1) * XA_HEAD_DIM)
        ss = [jnp.dot(q[:, sl].astype(BF16), kt_ref[sl, :], preferred_element_type=F32) * (XA_HEAD_DIM ** -0.5)
              for q in qs]
        ps = [jnp.exp(s - jnp.max(s, axis=-1, keepdims=True)) for s in ss]
        ps = [p / jnp.sum(p, axis=-1, keepdims=True) for p in ps]
        for k, p in enumerate(ps):
            heads[k].append(jnp.dot(p.astype(BF16), v_ref[:, sl], preferred_element_type=F32))
    os_ = [jnp.concatenate(hk, axis=1).astype(BF16) for hk in heads]
    return [h + jnp.dot(o, wo_ref[...], preferred_element_type=F32) for h, o in zip(hs, os_)]


def _route(xn, whi_ref, wlo_ref, b_ref, before_ref, run_ref):
    x_hi = xn.astype(BF16)
    x_lo = (xn - x_hi.astype(F32)).astype(BF16)
    logits = (jnp.dot(x_hi, whi_ref[...], preferred_element_type=F32)
              + jnp.dot(x_lo, whi_ref[...], preferred_element_type=F32)
              + jnp.dot(x_hi, wlo_ref[...], preferred_element_type=F32) + b_ref[...])
    lane = lax.broadcasted_iota(jnp.int32, logits.shape, 1).astype(F32)
    neg = -1e30
    none = float(LANES)

    def top(vals):
        best = jnp.max(vals, axis=-1, keepdims=True)
        where = jnp.min(jnp.where(vals == best, lane, none), axis=-1, keepdims=True)
        return best, where

    gl = jnp.where(lane < MOE_GROUPS, logits, neg)
    gbest, gsel = top(gl)
    gprob = 1.0 / jnp.sum(jnp.exp(gl - gbest), axis=-1, keepdims=True)
    lo = MOE_GROUPS + gsel * MOE_PER_GROUP
    el = jnp.where((lane >= lo) & (lane < lo + MOE_PER_GROUP), logits, neg)
    m1, i1 = top(el)
    m2, i2 = top(jnp.where(lane == i1, neg, el))
    e = jnp.exp(m2 - m1)
    gate1 = gprob / (1.0 + e)
    gate2 = gprob * e / (1.0 + e)

    hot1 = lane == i1
    hot2 = lane == i2
    one1 = jnp.where(hot1, 1.0, 0.0)
    one2 = jnp.where(hot2, 1.0, 0.0)
    before = before_ref[...]
    prefix1 = jnp.dot(before, one1.astype(BF16), preferred_element_type=F32)
    prefix2 = jnp.dot(before, one2.astype(BF16), preferred_element_type=F32)
    total1 = jnp.sum(one1, axis=0, keepdims=True)
    running = run_ref[...]
    rank1 = jnp.sum(jnp.where(hot1, prefix1 + running, 0.0), axis=-1, keepdims=True)
    rank2 = jnp.sum(jnp.where(hot2, prefix2 + (running + total1), 0.0), axis=-1, keepdims=True)
    running = running + total1 + jnp.sum(one2, axis=0, keepdims=True)
    run_ref[...] = running

    fields = (i1 - MOE_GROUPS, i2 - MOE_GROUPS, gate1, gate2, rank1, rank2)
    out = jnp.zeros_like(logits)
    for k, val in enumerate(fields):
        out = jnp.where(lane == k, val, out)
    return out


def _post_mixer_kernel(a_ref, b_ref, h_ref, wa_ref, wb_ref, gxa_ref, wq_ref, kt_ref, v_ref, wo_ref,
                       gffn_ref, whi_ref, wlo_ref, bias_ref, before_ref,
                       h_out_ref, xn_ref, r_ref, cnt_ref, run_ref):
    @pl.when(pl.program_id(0) == 0)
    def _():
        run_ref[...] = jnp.zeros_like(run_ref)

    h = _mixer_out(a_ref, b_ref, h_ref, wa_ref, wb_ref, slice(None))
    h, = _cross_attention([h], gxa_ref, wq_ref, kt_ref, v_ref, wo_ref)
    h_out_ref[...] = h
    xn = _rms(h, gffn_ref[...])
    xn_ref[...] = _pack_halves(xn)
    r_ref[...] = _route(xn, whi_ref, wlo_ref, bias_ref, before_ref, run_ref)
    cnt_ref[...] = run_ref[...]


def post_mixer(ya, yb, h, wa, wb, g_xa, wq, kt, v, wo, g_ffn, w_hi, w_lo, bias, tm=512):
    m, d = h.shape
    tiles_per_batch = m // kt.shape[0] // tm
    idx = np.arange(tm)
    before = jnp.asarray(idx[:, None] > idx[None, :], BF16)
    rows = lambda w: pl.BlockSpec((tm, w), lambda i: (i, 0))
    const = lambda a: pl.BlockSpec(a.shape, lambda i: (0,) * a.ndim, pipeline_mode=pl.Buffered(1))
    per_batch = lambda a: pl.BlockSpec((None,) + a.shape[1:], lambda i: (i // tiles_per_batch, 0, 0))
    g_xa, g_ffn = g_xa.reshape(1, d), g_ffn.reshape(1, d)
    return pl.pallas_call(
        _post_mixer_kernel,
        grid=(m // tm,),
        in_specs=[rows(ya.shape[1]), rows(yb.shape[1]), rows(d), const(wa), const(wb), const(g_xa), const(wq),
                  per_batch(kt), per_batch(v), const(wo), const(g_ffn), const(w_hi), const(w_lo), const(bias),
                  const(before)],
        out_specs=[rows(d), rows(d // 2), rows(LANES), pl.BlockSpec((1, LANES), lambda i: (0, 0))],
        out_shape=[jax.ShapeDtypeStruct((m, d), F32), jax.ShapeDtypeStruct((m, d // 2), jnp.int32),
                   jax.ShapeDtypeStruct((m, LANES), F32), jax.ShapeDtypeStruct((1, LANES), F32)],
        scratch_shapes=[pltpu.VMEM((1, LANES), F32)],
        compiler_params=_cparams("arbitrary"),
        name="post_mixer",
    )(ya, yb, h, wa, wb, g_xa, wq, kt, v, wo, g_ffn, w_hi, w_lo, bias, before)


def _expert_kernel(table_ref, x_ref, wg_hbm, wu_hbm, wd_hbm, o_ref,
                   wg32_ref, wu32_ref, wd32_ref, wgb_ref, wub_ref, wdb_ref, sem_ref, *, layer):
    i = pl.program_id(0)
    beid_ref, valid_ref, first_ref, slot_ref, next_ref = (table_ref.at[k] for k in range(5))
    valid = valid_ref[i]

    def weight_copies(expert, slot):
        return (pltpu.make_async_copy(wg_hbm.at[layer, expert], wg32_ref.at[slot], sem_ref.at[slot, 0]),
                pltpu.make_async_copy(wu_hbm.at[layer, expert], wu32_ref.at[slot], sem_ref.at[slot, 1]),
                pltpu.make_async_copy(wd_hbm.at[layer, expert], wd32_ref.at[slot], sem_ref.at[slot, 2]))

    @pl.when(i == 0)
    def _():
        for copy in weight_copies(beid_ref[0], 0):
            copy.start()

    @pl.when(first_ref[i] == 1)
    def _():
        slot = slot_ref[i]
        for copy in weight_copies(beid_ref[i], slot):
            copy.wait()
        wgb_ref[...] = wg32_ref[slot].astype(BF16)
        wub_ref[...] = wu32_ref[slot].astype(BF16)
        wdb_ref[...] = wd32_ref[slot].astype(BF16)

        @pl.when(next_ref[i] >= 0)
        def _():
            for copy in weight_copies(next_ref[i], 1 - slot):
                copy.start()

    half = MOE_ROWS // 2

    def ffn(n_halves):
        row = lax.broadcasted_iota(jnp.int32, (half, 2 * x_ref.shape[1]), 0)
        xs = [jnp.where(row + k * half < valid, _unpack_halves(x_ref[k * half:(k + 1) * half, :]), 0.0).astype(BF16)
              for k in range(n_halves)]
        gates = [jnp.dot(x, wgb_ref[...], preferred_element_type=F32) for x in xs]
        ups = [jnp.dot(x, wub_ref[...], preferred_element_type=F32) for x in xs]
        acts = [(_silu(g) * u).astype(BF16) for g, u in zip(gates, ups)]
        for k, act in enumerate(acts):
            o_ref[k * half:(k + 1) * half, :] = _pack_halves(jnp.dot(act, wdb_ref[...], preferred_element_type=F32))

    @pl.when(valid > half)
    def _():
        ffn(2)

    @pl.when((valid > 0) & (valid <= half))
    def _():
        ffn(1)
        o_ref[half:, :] = jnp.zeros((half, o_ref.shape[1]), o_ref.dtype)

    @pl.when(valid == 0)
    def _():
        o_ref[...] = jnp.zeros_like(o_ref)


def moe_experts(blocks, xs, w_gate, w_up, w_down, layer):
    n_slots, packed = xs.shape
    d = 2 * packed
    rows = MOE_ROWS
    ff = w_gate.shape[3]
    grid_spec = pltpu.PrefetchScalarGridSpec(
        num_scalar_prefetch=1,
        grid=(n_slots // rows,),
        in_specs=[
            pl.BlockSpec((rows, packed), lambda i, *_: (i, 0)),
            pl.BlockSpec(memory_space=pl.ANY),
            pl.BlockSpec(memory_space=pl.ANY),
            pl.BlockSpec(memory_space=pl.ANY),
        ],
        out_specs=pl.BlockSpec((rows, packed), lambda i, *_: (i, 0)),
        scratch_shapes=[pltpu.VMEM((2, d, ff), F32), pltpu.VMEM((2, d, ff), F32), pltpu.VMEM((2, ff, d), F32),
                        pltpu.VMEM((d, ff), BF16), pltpu.VMEM((d, ff), BF16), pltpu.VMEM((ff, d), BF16),
                        pltpu.SemaphoreType.DMA((2, 3))],
    )
    return pl.pallas_call(
        functools.partial(_expert_kernel, layer=layer),
        grid_spec=grid_spec,
        out_shape=jax.ShapeDtypeStruct((n_slots, packed), jnp.int32),
        compiler_params=_cparams("arbitrary"),
        name="moe_experts",
    )(blocks, xs, w_gate, w_up, w_down)


def _sc_mesh():
    return plsc.VectorSubcoreMesh(core_axis_name="c", subcore_axis_name="s",
                                  num_cores=SC_CORES, num_subcores=SC_SUBCORES)


def _sc_worker():
    return lax.axis_index("s") * SC_CORES + lax.axis_index("c")


def sc_scatter_rows(x, dest, n_slots):
    n_tok, d = x.shape
    per_worker = n_tok // SC_WORKERS
    n_chunks = per_worker // SC_CHUNK
    by_worker = dest.reshape(dest.shape[0] * SC_WORKERS, n_chunks, SC_CHUNK)

    @functools.partial(
        pl.kernel, mesh=_sc_mesh(), out_type=jax.ShapeDtypeStruct((n_slots, d), x.dtype),
        scratch_types=[pltpu.VMEM((n_chunks, SC_CHUNK), jnp.int32), pltpu.VMEM((n_chunks, SC_CHUNK), jnp.int32),
                       pltpu.VMEM((SC_CHUNK, d), x.dtype)],
        name="moe_scatter_rows")
    def scatter(x_hbm, dest_hbm, out_hbm, i0_v, i1_v, rows_v):
        wid = _sc_worker()
        pltpu.sync_copy(dest_hbm.at[wid], i0_v)
        pltpu.sync_copy(dest_hbm.at[SC_WORKERS + wid], i1_v)

        @pl.loop(0, n_chunks)
        def _(j):
            start = pl.multiple_of(wid * per_worker + j * SC_CHUNK, SC_CHUNK)
            pltpu.sync_copy(x_hbm.at[pl.ds(start, SC_CHUNK)], rows_v)
            pltpu.sync_copy(rows_v, out_hbm.at[i0_v.at[j]])
            pltpu.sync_copy(rows_v, out_hbm.at[i1_v.at[j]])

    return scatter(x, by_worker)


def sc_gather_rows(table, idx, n_out):
    d = table.shape[1]
    per_worker = n_out // SC_WORKERS
    n_chunks = per_worker // SC_CHUNK

    @functools.partial(
        pl.kernel, mesh=_sc_mesh(), out_type=jax.ShapeDtypeStruct((n_out, d), table.dtype),
        scratch_types=[pltpu.VMEM((n_chunks, SC_CHUNK), jnp.int32), pltpu.VMEM((SC_CHUNK, d), table.dtype)],
        name="moe_gather_rows")
    def gather(table_hbm, idx_hbm, out_hbm, idx_v, rows_v):
        wid = _sc_worker()
        pltpu.sync_copy(idx_hbm.at[wid], idx_v)

        @pl.loop(0, n_chunks)
        def _(j):
            start = pl.multiple_of(wid * per_worker + j * SC_CHUNK, SC_CHUNK)
            pltpu.sync_copy(table_hbm.at[idx_v.at[j]], rows_v)
            pltpu.sync_copy(rows_v, out_hbm.at[pl.ds(start, SC_CHUNK)])

    return gather(table, idx.reshape(-1, n_chunks, SC_CHUNK))


def _combine_kernel(h_ref, y0_ref, y1_ref, r_ref, g_ref, o_ref, *, final_norm):
    route = r_ref[...]
    h = h_ref[...] + (route[:, 2:3] * _unpack_halves(y0_ref[...]) + route[:, 3:4] * _unpack_halves(y1_ref[...]))
    o_ref[...] = _rms(h, g_ref[...]) if final_norm else h


def moe_combine(h, y01, route, g, final_norm, tm=1024):
    m, d = h.shape
    tm = min(tm, m)
    rows = lambda w: pl.BlockSpec((tm, w), lambda i: (i, 0))
    return pl.pallas_call(
        functools.partial(_combine_kernel, final_norm=final_norm),
        grid=(m // tm,),
        in_specs=[rows(d), rows(d // 2), pl.BlockSpec((tm, d // 2), lambda i: (i + m // tm, 0)), rows(LANES),
                  pl.BlockSpec((1, d), lambda i: (0, 0))],
        out_specs=rows(d),
        out_shape=jax.ShapeDtypeStruct((m, d), F32),
        compiler_params=_cparams("parallel"),
        name="moe_combine",
    )(h, y01, y01, route, g.reshape(1, d))


def _pad_cols(w):
    return jnp.pad(w, ((0, 0), (0, LANES - w.shape[1])))


def _plan_kernel(route_ref, cnt_ref, incl_ref, dest_ref, table_ref):
    f32_sum = lambda x, axis: jnp.sum(x, axis=axis, keepdims=True)
    lane = lax.broadcasted_iota(jnp.int32, (LANES, LANES), 1)
    sub = lax.broadcasted_iota(jnp.int32, (LANES, LANES), 0)
    incl = incl_ref[...]
    is_expert = (lane >= MOE_GROUPS) & (lane < MOE_GROUPS + MOE_EXPERTS)
    shift = MOE_ROWS.bit_length() - 1
    counts = jnp.broadcast_to(cnt_ref[...], (LANES, LANES)).astype(jnp.int32)
    padded = jnp.where(is_expert, ((counts + (MOE_ROWS - 1)) >> shift) << shift, 0)
    pad_end = _mm_sel_rhs(padded.astype(F32), incl)
    pad_start = pad_end - padded.astype(F32)

    route = route_ref[...]
    lane_t = lax.broadcasted_iota(jnp.int32, route.shape, 1)
    lane_f = lane_t.astype(F32)
    start_row = pad_start[0:1, :]
    slots = [f32_sum(jnp.where(lane_f == route[:, k:k + 1] + MOE_GROUPS, start_row, 0.0), 1) + route[:, 4 + k:5 + k]
             for k in range(2)]
    both = jnp.where(lane_t == 0, slots[0], jnp.where(lane_t == 1, slots[1], 0.0))
    dest_ref[...] = both.T[0:8, :].astype(jnp.int32)

    on_sub = lambda rows_equal: rows_equal.T
    expert_sub = (sub >= MOE_GROUPS) & (sub < MOE_GROUPS + MOE_EXPERTS)
    block_start = (lane * MOE_ROWS).astype(F32)
    eid = f32_sum(jnp.where(expert_sub & (on_sub(pad_end) <= block_start), 1.0, 0.0), 0)
    eid = jnp.minimum(eid, float(MOE_EXPERTS - 1))
    filled = on_sub(pad_start + counts.astype(F32))
    own = (sub - MOE_GROUPS).astype(F32) == eid
    valid = jnp.clip(f32_sum(jnp.where(own, filled, 0.0), 0) - block_start[0:1, :], 0.0, float(MOE_ROWS))
    eid_rows = jnp.broadcast_to(eid, (LANES, LANES))
    changed = (lane == 0) | (eid_rows != pltpu.roll(eid_rows, 1, axis=1))
    first = jnp.where((jnp.broadcast_to(valid, (LANES, LANES)) > 0) & changed, 1.0, 0.0)
    ordinal = _mm_sel_rhs(first, incl) - 1.0
    slot = ordinal - 2.0 * jnp.floor(ordinal * 0.5)
    later = (on_sub(first) > 0) & (sub > lane)
    nearest = jnp.min(jnp.where(later, sub, LANES), axis=0, keepdims=True)
    next_eid = f32_sum(jnp.where(sub == nearest, on_sub(eid_rows), 0.0), 0)
    next_eid = jnp.where(nearest < LANES, next_eid, -1.0)
    row8 = lax.broadcasted_iota(jnp.int32, (8, LANES), 0)
    table = jnp.zeros((8, LANES), F32)
    for k, val in enumerate((eid, valid, first[0:1, :], slot[0:1, :], next_eid)):
        table = jnp.where(row8 == k, val, table)
    table_ref[...] = table.astype(jnp.int32)


def moe_plan(route, counts, tm=2048):
    n_tok = route.shape[0]
    tm = min(tm, n_tok)
    idx = np.arange(LANES)
    incl = jnp.asarray(idx[:, None] <= idx[None, :], BF16)
    return pl.pallas_call(
        _plan_kernel,
        grid=(n_tok // tm,),
        in_specs=[pl.BlockSpec((tm, LANES), lambda i: (i, 0)), pl.BlockSpec((1, LANES), lambda i: (0, 0)),
                  pl.BlockSpec((LANES, LANES), lambda i: (0, 0))],
        out_specs=[pl.BlockSpec((8, tm), lambda i: (0, i)), pl.BlockSpec((8, LANES), lambda i: (0, 0))],
        out_shape=[jax.ShapeDtypeStruct((8, n_tok), jnp.int32), jax.ShapeDtypeStruct((8, LANES), jnp.int32)],
        compiler_params=_cparams("arbitrary"),
        name="moe_plan",
    )(route, counts, incl)


def _router_weights(w_group, b_group, w_expert, b_expert):
    w_r = _pad_cols(jnp.concatenate([w_group, w_expert], axis=1))
    w_hi = w_r.astype(BF16)
    w_lo = (w_r - w_hi.astype(F32)).astype(BF16)
    return w_hi, w_lo, _pad_lanes(jnp.concatenate([b_group, b_expert]))


def _moe_layer(h, xn, route, counts, w_gate, w_up, w_down, layer, final_g):
    n_tok, d = h.shape
    n_blocks = -(-(2 * n_tok + MOE_EXPERTS * (MOE_ROWS - 1)) // MOE_ROWS)
    dest, blocks = moe_plan(route, counts)
    xs = sc_scatter_rows(xn, dest, n_blocks * MOE_ROWS)
    ys = moe_experts(blocks, xs, w_gate, w_up, w_down, layer)
    y01 = sc_gather_rows(ys, dest, 2 * n_tok)
    g = jnp.ones((d,), F32) if final_g is None else final_g
    return moe_combine(h, y01, route, g, final_g is not None)


def _memory_kv(memn_in, mem_norm, wk, wv):
    bsz, m, d = memn_in.shape
    w = jnp.concatenate([wk, wv], axis=1).astype(BF16)
    kv, _ = rms_matmul(memn_in.reshape(bsz * m, d), mem_norm, w, jnp.zeros((d, LANES), BF16))
    k = kv[:, :d].reshape(bsz, m, d)
    v = kv[:, d:].reshape(bsz, m, d)
    return jnp.swapaxes(k, 1, 2).astype(BF16), v.astype(BF16)


def kernel(x, mem, mem_norm, final_norm, norm_mix, norm_xa, norm_ffn, xa_wq, xa_wk, xa_wv, xa_wo, moe_w_group, moe_b_group, moe_w_expert, moe_b_expert, moe_w_gate, moe_w_up, moe_w_down, ev_w_in, ev_sc_conv, ev_ssm_conv_w, ev_ssm_conv_b, ev_ssm_dt_bias, ev_ssm_a_log, ev_ssm_d, ev_ssm_norm, ev_w_out, od_w_in, od_gdn_conv, od_gdn_dt_bias, od_gdn_a_log, od_gdn_norm, od_w_out):
    bsz, length, d = x.shape
    n_tok = bsz * length
    depth = norm_mix.shape[0]
    h = x.reshape(n_tok, d)
    for layer in range(depth):
        i = layer // 2
        if layer % 2 == 0:
            w = ev_w_in[i]
            z0 = 3 * SC_DIM
            xbc0 = z0 + SSM_INNER
            w_conv = w[:, xbc0:xbc0 + SSM_XBC].astype(BF16)
            w_small = _pad_cols(w[:, xbc0 + SSM_XBC:]).astype(BF16)
            xbc = rms_matmul_conv(h, norm_mix[layer], w_conv, ev_ssm_conv_w[i], ev_ssm_conv_b[i], length)
            z, ya, small = rms_matmul_gated(h, norm_mix[layer], w[:, z0:xbc0].astype(BF16), w[:, :z0].astype(BF16),
                                            w_small, ev_sc_conv[i], length)
            small3 = small.reshape(bsz, length, LANES)
            smallt = jnp.swapaxes(small3[:, :, :16], 1, 2)
            yb = ssd_mixer(xbc.reshape(bsz, length, -1), z.reshape(bsz, length, -1), small3, smallt,
                           ev_ssm_dt_bias[i], ev_ssm_a_log[i], ev_ssm_d[i], ev_ssm_norm[i])
            w_out = ev_w_out[i].astype(BF16)
            split = SC_DIM
        else:
            w = od_w_in[i]
            qkv_w = 3 * GDN_HEADS * GDN_D
            z_end = qkv_w + GDN_HEADS * GDN_D
            w_conv = w[:, :qkv_w].astype(BF16)
            w_main = jnp.concatenate([w[:, qkv_w:z_end], w[:, z_end + 2 * GDN_HEADS:]], axis=1).astype(BF16)
            w_small = _pad_cols(w[:, z_end:z_end + 2 * GDN_HEADS]).astype(BF16)
            qkv = rms_matmul_conv(h, norm_mix[layer], w_conv, od_gdn_conv[i], jnp.zeros((qkv_w,), F32), length)
            y, small = rms_matmul(h, norm_mix[layer], w_main, w_small, tm=1024, tn=w_main.shape[1])
            y3 = y.reshape(bsz, length, -1)
            small3 = small.reshape(bsz, length, LANES)
            smallt = jnp.swapaxes(small3[:, :, :16], 1, 2)
            ya = gated_deltanet_mixer(qkv.reshape(bsz, length, -1), y3, small3, smallt, od_gdn_dt_bias[i],
                                      od_gdn_a_log[i], od_gdn_norm[i])
            yb = stick_breaking_mixer(y3, GDN_HEADS * GDN_D)
            w_out = od_w_out[i].astype(BF16)
            split = GDN_HEADS * GDN_D
        kt, v = _memory_kv(mem, mem_norm, xa_wk[layer], xa_wv[layer])
        w_hi, w_lo, bias = _router_weights(moe_w_group[layer], moe_b_group[layer], moe_w_expert[layer],
                                           moe_b_expert[layer])
        h, xn, route, counts = post_mixer(
            ya.reshape(n_tok, -1), yb.reshape(n_tok, -1), h, w_out[:split], w_out[split:], norm_xa[layer],
            xa_wq[layer].astype(BF16), kt, v, xa_wo[layer].astype(BF16), norm_ffn[layer], w_hi, w_lo, bias)
        h = _moe_layer(h, xn, route, counts, moe_w_gate, moe_w_up, moe_w_down, layer,
                       final_norm if layer == depth - 1 else None)
    return h.reshape(bsz, length, d)
```

```python
import functools

import jax
import jax.numpy as jnp
import numpy as np
from jax import lax
from jax.experimental import pallas as pl
from jax.experimental.pallas import tpu as pltpu
from jax.experimental.pallas import tpu_sc as plsc

F32 = jnp.float32
BF16 = jnp.bfloat16
EPS = 1e-6

D_MODEL = 1024
MEM_LEN = 256
SC_DIM = 512
SSM_HEADS = 16
SSM_HEAD_DIM = 64
SSM_INNER = 1024
SSM_GROUPS = 2
SSM_STATE = 128
SSM_XBC = SSM_INNER + 2 * SSM_GROUPS * SSM_STATE
SSD_CHUNK = 128
SSD_STEP_ROWS = 512
GDN_HEADS = 8
GDN_D = 128
GDN_CHUNK = 64
GDN_TILE = 128
GDN_STEP_ROWS = 512
SB_HEADS = 8
SB_HEAD_DIM = 64
SB_DIM = 512
SB_BLOCK = 128
SB_STEP_HEADS = 8
SB_EAGER_BLOCKS = 2
XA_HEADS = 4
XA_HEAD_DIM = 256
MOE_GROUPS = 4
MOE_PER_GROUP = 8
MOE_EXPERTS = 32
MOE_FF = 512
MOE_ROWS = 512
SC_CORES = 2
SC_SUBCORES = 16
SC_WORKERS = SC_CORES * SC_SUBCORES
SC_CHUNK = 64
HALO = 8
CONV_CHUNK = 512
LANES = 128
SB_LOG_ZERO = -104.0
VMEM_LIMIT = 56 * 1024 * 1024


def _cparams(*sem):
    return pltpu.CompilerParams(dimension_semantics=sem, vmem_limit_bytes=VMEM_LIMIT)


def _mm(a, b):
    return jnp.dot(a.astype(BF16), b.astype(BF16), preferred_element_type=F32)


def _mm_nt(a, b):
    return lax.dot_general(a.astype(BF16), b.astype(BF16), (((1,), (1,)), ((), ())),
                           preferred_element_type=F32)


def _split_bf16(x, n):
    parts, r = [], x
    for _ in range(n):
        p = r.astype(BF16)
        parts.append(p)
        r = r - p.astype(F32)
    return parts


def _mm_sel_rhs(x, sel, n=3):
    return sum(jnp.dot(p, sel, preferred_element_type=F32) for p in _split_bf16(x, n))


def _mm_sel_lhs(sel, x, n=3):
    return sum(jnp.dot(sel, p, preferred_element_type=F32) for p in _split_bf16(x, n))


def _spread_heads(x, first, n_heads, width):
    rows = x.shape[0]
    col = lambda h: jnp.broadcast_to(x[:, first + h:first + h + 1], (rows, LANES))
    if width == LANES:
        return jnp.concatenate([col(h) for h in range(n_heads)], axis=1)
    left = lax.broadcasted_iota(jnp.int32, (rows, LANES), 1) < width
    return jnp.concatenate([jnp.where(left, col(h), col(h + 1)) for h in range(0, n_heads, 2)], axis=1)


def _pack_halves(x):
    n = x.shape[1] // 2
    lo = pltpu.bitcast(x[:, :n].astype(BF16).astype(F32), jnp.int32)
    hi = pltpu.bitcast(x[:, n:].astype(BF16).astype(F32), jnp.int32)
    return lax.shift_right_logical(lo, 16) | (hi & jnp.int32(-65536))


def _unpack_halves(p):
    lo = pltpu.bitcast(lax.shift_left(p, 16), F32)
    hi = pltpu.bitcast(p & jnp.int32(-65536), F32)
    return jnp.concatenate([lo, hi], axis=1)


def _silu(x):
    return x * jax.nn.sigmoid(x)


def _softplus(x):
    return jnp.maximum(x, 0.0) + jnp.log(1.0 + jnp.exp(-jnp.abs(x)))


def _rms(x, g):
    return x * lax.rsqrt(jnp.mean(x * x, axis=-1, keepdims=True) + EPS) * g


def _rms_matmul_kernel(x_ref, g_ref, w_ref, ws_ref, o_ref, os_ref):
    xn = _rms(x_ref[...], g_ref[...]).astype(BF16)
    o_ref[...] = jnp.dot(xn, w_ref[...], preferred_element_type=F32)
    os_ref[...] = jnp.dot(xn, ws_ref[...], preferred_element_type=F32)


def rms_matmul(x, g, w, ws, tm=512, tn=512):
    m, k = x.shape
    n = w.shape[1]
    tm = min(tm, m)
    main, small = pl.pallas_call(
        _rms_matmul_kernel,
        grid=(n // tn, m // tm),
        in_specs=[
            pl.BlockSpec((tm, k), lambda j, i: (i, 0)),
            pl.BlockSpec((1, k), lambda j, i: (0, 0)),
            pl.BlockSpec((k, tn), lambda j, i: (0, j)),
            pl.BlockSpec((k, LANES), lambda j, i: (0, 0)),
        ],
        out_specs=[
            pl.BlockSpec((tm, tn), lambda j, i: (i, j)),
            pl.BlockSpec((None, tm, LANES), lambda j, i: (j, i, 0)),
        ],
        out_shape=[jax.ShapeDtypeStruct((m, n), F32), jax.ShapeDtypeStruct((n // tn, m, LANES), F32)],
        compiler_params=_cparams("parallel", "parallel"),
        name="rms_matmul",
    )(x, g.reshape(1, k), w, ws)
    return main, small[0]


def _causal_conv(ext_ref, w_ref, rows):
    width = w_ref.shape[0]
    ext = ext_ref[...]
    acc = None
    for j in range(width):
        shift = width - 1 - j
        moved = ext if shift == 0 else pltpu.roll(ext, shift, axis=0)
        term = w_ref[j:j + 1, :] * moved[HALO:HALO + rows, :]
        acc = term if acc is None else acc + term
    return acc


def _rms_matmul_conv_kernel(x_ref, g_ref, w_ref, cw_ref, cb_ref, o_ref, *ext_refs, tiles_per_seq):
    tm = x_ref.shape[0]
    starts_sequence = pl.program_id(1) % tiles_per_seq == 0

    @pl.when(starts_sequence)
    def _():
        for ext_ref in ext_refs:
            ext_ref[0:HALO, :] = jnp.zeros((HALO, CONV_CHUNK), F32)

    @pl.when(jnp.logical_not(starts_sequence))
    def _():
        for ext_ref in ext_refs:
            ext_ref[0:HALO, :] = ext_ref[tm:tm + HALO, :]

    xn = _rms(x_ref[...], g_ref[...]).astype(BF16)
    for c, ext_ref in enumerate(ext_refs):
        cols = slice(c * CONV_CHUNK, (c + 1) * CONV_CHUNK)
        ext_ref[HALO:, :] = jnp.dot(xn, w_ref[:, cols], preferred_element_type=F32)
        o_ref[:, cols] = _silu(_causal_conv(ext_ref, cw_ref.at[:, cols], tm) + cb_ref[:, cols])


def rms_matmul_conv(x, g, w, conv_w, conv_b, seq_len, tm=1024, tn=1536):
    m, k = x.shape
    n = w.shape[1]
    cols = lambda rows: pl.BlockSpec((rows, tn), lambda j, i: (0, j))
    return pl.pallas_call(
        functools.partial(_rms_matmul_conv_kernel, tiles_per_seq=seq_len // tm),
        grid=(n // tn, m // tm),
        in_specs=[
            pl.BlockSpec((tm, k), lambda j, i: (i, 0)),
            pl.BlockSpec((1, k), lambda j, i: (0, 0)),
            cols(k), cols(conv_w.shape[0]), cols(1),
        ],
        out_specs=pl.BlockSpec((tm, tn), lambda j, i: (i, j)),
        out_shape=jax.ShapeDtypeStruct((m, n), F32),
        scratch_shapes=[pltpu.VMEM((tm + HALO, CONV_CHUNK), F32)] * (tn // CONV_CHUNK),
        compiler_params=_cparams("arbitrary", "arbitrary"),
        name="rms_matmul_conv",
    )(x, g.reshape(1, k), w, conv_w, conv_b.reshape(1, n))


def _rms_matmul_gated_kernel(x_ref, g_ref, wz_ref, wbcx_ref, ws_ref, cw_ref, z_ref, ya_ref, os_ref, ext_ref,
                             *, tiles_per_seq):
    tm = x_ref.shape[0]
    starts_sequence = pl.program_id(0) % tiles_per_seq == 0

    @pl.when(starts_sequence)
    def _():
        ext_ref[0:HALO, :] = jnp.zeros((HALO, SC_DIM), F32)

    @pl.when(jnp.logical_not(starts_sequence))
    def _():
        ext_ref[0:HALO, :] = ext_ref[tm:tm + HALO, :]

    xn = _rms(x_ref[...], g_ref[...]).astype(BF16)
    z_ref[...] = jnp.dot(xn, wz_ref[...], preferred_element_type=F32)
    os_ref[...] = jnp.dot(xn, ws_ref[...], preferred_element_type=F32)
    bcx = jnp.dot(xn, wbcx_ref[...], preferred_element_type=F32)
    ext_ref[HALO:, :] = bcx[:, SC_DIM:2 * SC_DIM] * bcx[:, 2 * SC_DIM:]
    ya_ref[...] = bcx[:, :SC_DIM] * _causal_conv(ext_ref, cw_ref, tm)


def rms_matmul_gated(x, g, w_z, w_bcx, w_small, conv_w, seq_len, tm=1024):
    m, k = x.shape
    const = lambda a: pl.BlockSpec(a.shape, lambda i: (0,) * a.ndim)
    rows = lambda w: pl.BlockSpec((tm, w), lambda i: (i, 0))
    g = g.reshape(1, k)
    return pl.pallas_call(
        functools.partial(_rms_matmul_gated_kernel, tiles_per_seq=seq_len // tm),
        grid=(m // tm,),
        in_specs=[rows(k), const(g), const(w_z), const(w_bcx), const(w_small), const(conv_w)],
        out_specs=[rows(w_z.shape[1]), rows(SC_DIM), rows(LANES)],
        out_shape=[jax.ShapeDtypeStruct((m, w_z.shape[1]), F32), jax.ShapeDtypeStruct((m, SC_DIM), F32),
                   jax.ShapeDtypeStruct((m, LANES), F32)],
        scratch_shapes=[pltpu.VMEM((tm + HALO, SC_DIM), F32)],
        compiler_params=_cparams("arbitrary"),
        name="rms_matmul_gated",
    )(x, g, w_z, w_bcx, w_small, conv_w)


def _ssd_kernel(xbc_ref, z_ref, dt_ref, dtt_ref, dtb_r_ref, dtb_c_ref,
                alog_r_ref, alog_c_ref, d_ref, nw_ref, tri_ref, trit_ref,
                o_ref, s_ref):
    q = SSD_CHUNK

    @pl.when(pl.program_id(1) == 0)
    def _():
        s_ref[...] = jnp.zeros_like(s_ref)

    for sub in range(xbc_ref.shape[0] // q):
        rows = slice(sub * q, (sub + 1) * q)
        _ssd_chunk(xbc_ref[rows, :], z_ref[rows, :], dt_ref[rows, :], dtt_ref[:, rows], dtb_r_ref, dtb_c_ref,
                   alog_r_ref, alog_c_ref, d_ref, nw_ref, tri_ref, trit_ref, o_ref.at[rows, :], s_ref)


def _ssd_chunk(xbc, z, dt_raw, dtt_raw, dtb_r_ref, dtb_c_ref, alog_r_ref, alog_c_ref, d_ref, nw_ref, tri_ref,
               trit_ref, o_ref, s_ref):
    q = SSD_CHUNK
    hpg = SSM_HEADS // SSM_GROUPS
    gw = hpg * SSM_HEAD_DIM
    xs = xbc[:, :SSM_INNER]
    bm = xbc[:, SSM_INNER:SSM_INNER + SSM_GROUPS * SSM_STATE]
    cm = xbc[:, SSM_INNER + SSM_GROUPS * SSM_STATE:]

    dt = _softplus(dt_raw + dtb_r_ref[...])
    acs = _mm_sel_lhs(tri_ref[...], dt * -jnp.exp(alog_r_ref[...]))
    dtt = _softplus(dtt_raw + dtb_c_ref[...])
    acst = _mm_sel_rhs(dtt * -jnp.exp(alog_c_ref[...]), trit_ref[...])
    dt_full = _spread_heads(dt, 0, SSM_HEADS, SSM_HEAD_DIM)
    acs_full = _spread_heads(acs, 0, SSM_HEADS, SSM_HEAD_DIM)
    acs_col = _spread_heads(acs, 0, SSM_HEADS, q)

    xdt = xs * dt_full
    acs_last = acs_full[q - 1:q, :]
    xw = xdt * jnp.exp(acs_last - acs_full)
    chunk_decay = jnp.exp(acs_last)

    row = lax.broadcasted_iota(jnp.int32, (q, q), 0)
    col = lax.broadcasted_iota(jnp.int32, (q, q), 1)
    causal = row >= col
    lane = lax.broadcasted_iota(jnp.int32, (q, 2 * SSM_HEAD_DIM), 1)

    y_diag, y_off = [], []
    for g in range(SSM_GROUPS):
        bm_g = bm[:, g * SSM_STATE:(g + 1) * SSM_STATE]
        cm_g = cm[:, g * SSM_STATE:(g + 1) * SSM_STATE]
        cb_g = _mm_nt(cm_g, bm_g)
        state = s_ref[g]
        y_off.append(_mm(cm_g, state))
        s_ref[g] = state * chunk_decay[:, g * gw:(g + 1) * gw] + _mm(bm_g.T, xw[:, g * gw:(g + 1) * gw])
        for pair in range(hpg // 2):
            h0 = g * hpg + 2 * pair
            xdt_pair = xdt[:, h0 * SSM_HEAD_DIM:(h0 + 2) * SSM_HEAD_DIM]
            weights = []
            for h in (h0, h0 + 1):
                seg = acs_col[:, h * q:(h + 1) * q] - acst[h:h + 1, :]
                weights.append(cb_g * jnp.where(causal, jnp.exp(seg), 0.0))
            both = _mm(jnp.concatenate(weights, axis=0), xdt_pair)
            y_diag.append(jnp.where(lane < SSM_HEAD_DIM, both[:q], both[q:]))
    y = (jnp.concatenate(y_diag, axis=1) + jnp.concatenate(y_off, axis=1) * jnp.exp(acs_full)
         + xs * d_ref[...])
    y = y * _silu(z)
    halves = []
    for g in range(SSM_GROUPS):
        yg = y[:, g * gw:(g + 1) * gw]
        halves.append(yg * lax.rsqrt(jnp.mean(yg * yg, axis=-1, keepdims=True) + EPS))
    o_ref[...] = jnp.concatenate(halves, axis=1) * nw_ref[...]


def _pad_lanes(v, fill=0.0):
    return jnp.pad(v.astype(F32), (0, LANES - v.shape[0]), constant_values=fill).reshape(1, LANES)


def _pad_col(v, rows=16):
    return jnp.pad(v.astype(F32), (0, rows - v.shape[0])).reshape(rows, 1)


def ssd_mixer(xbc3, y3, small3, smallt, dt_bias, a_log, d_skip, norm_w):
    bsz, length, _ = y3.shape
    q = SSD_CHUNK
    tri = jnp.asarray(np.tril(np.ones((q, q), np.float32)), BF16)
    trit = jnp.asarray(np.triu(np.ones((q, q), np.float32)), BF16)
    d_full = jnp.repeat(d_skip.astype(F32), SSM_HEAD_DIM).reshape(1, SSM_INNER)
    const = lambda a: pl.BlockSpec(a.shape, lambda b, c: (0,) * a.ndim)
    args = [_pad_lanes(dt_bias), _pad_col(dt_bias), _pad_lanes(a_log),
            _pad_col(a_log), d_full, norm_w.reshape(1, -1), tri, trit]
    rows = min(SSD_STEP_ROWS, length)
    return pl.pallas_call(
        _ssd_kernel,
        grid=(bsz, length // rows),
        in_specs=[
            pl.BlockSpec((None, rows, SSM_XBC), lambda b, c: (b, c, 0)),
            pl.BlockSpec((None, rows, SSM_INNER), lambda b, c: (b, c, 0)),
            pl.BlockSpec((None, rows, LANES), lambda b, c: (b, c, 0)),
            pl.BlockSpec((None, 16, rows), lambda b, c: (b, 0, c)),
        ] + [const(a) for a in args],
        out_specs=pl.BlockSpec((None, rows, SSM_INNER), lambda b, c: (b, c, 0)),
        out_shape=jax.ShapeDtypeStruct((bsz, length, SSM_INNER), F32),
        scratch_shapes=[pltpu.VMEM((SSM_GROUPS, SSM_STATE, SSM_INNER // SSM_GROUPS), F32)],
        compiler_params=_cparams("parallel", "arbitrary"),
        name="ssd_mixer",
    )(xbc3, y3, small3, smallt, *args)


def _unit_lower_inverse(mats, row, col):
    eye = jnp.where(row == col, 1.0, 0.0)
    blk = lambda n: (row >> (n.bit_length() - 1)) == (col >> (n.bit_length() - 1))
    size = row.shape[0]
    p = [jnp.where(blk(16), -a, 0.0) for a in mats]
    t = [eye + x for x in p]
    p = [_mm(x, x) for x in p]
    for _ in range(2):
        both = [_mm(jnp.concatenate([x, y], axis=0), x) for x, y in zip(p, t)]
        p = [b[:size] for b in both]
        t = [y + b[size:] for y, b in zip(t, both)]
    t = [y + _mm(y, x) for y, x in zip(t, p)]
    for n in (16, 32):
        band = blk(2 * n) & jnp.logical_not(blk(n))
        left = [_mm(y, jnp.where(band, a, 0.0)) for y, a in zip(t, mats)]
        t = [y - _mm(x, y) for y, x in zip(t, left)]
    return t


def _gdn_kernel(qkv_ref, z_ref, ab_ref, abt_ref, dtb_r_ref, dtb_c_ref, alog_r_ref,
                alog_c_ref, nw_ref, tri_ref, trit_ref, o_ref, s_ref):
    n = GDN_TILE

    @pl.when(pl.program_id(1) == 0)
    def _():
        s_ref[...] = jnp.zeros_like(s_ref)

    for sub in range(qkv_ref.shape[0] // n):
        rows = slice(sub * n, (sub + 1) * n)
        _gdn_tile(qkv_ref[rows, :], z_ref[rows, :], ab_ref[rows, :], abt_ref[:, rows], dtb_r_ref, dtb_c_ref,
                  alog_r_ref, alog_c_ref, nw_ref, tri_ref, trit_ref, o_ref.at[rows, :], s_ref)


def _gdn_tile(qkv, z, ab, abt, dtb_r_ref, dtb_c_ref, alog_r_ref, alog_c_ref, nw_ref, tri_ref, trit_ref, o_ref, s_ref):
    n = GDN_TILE
    c = GDN_CHUNK
    d = GDN_D
    hd = GDN_HEADS * d
    g = -jnp.exp(alog_r_ref[...]) * _softplus(ab + dtb_r_ref[...])
    gc_full = _spread_heads(_mm_sel_lhs(tri_ref[...], g), 0, GDN_HEADS, d)
    beta_full = _spread_heads(jax.nn.sigmoid(ab), GDN_HEADS, GDN_HEADS, d)
    gt = -jnp.exp(alog_c_ref[...]) * _softplus(abt + dtb_c_ref[...])
    gct = _mm_sel_rhs(gt, trit_ref[...])

    row = lax.broadcasted_iota(jnp.int32, (n, n), 0)
    col = lax.broadcasted_iota(jnp.int32, (n, n), 1)
    same = (row >> (c.bit_length() - 1)) == (col >> (c.bit_length() - 1))
    incl = same & (row >= col)
    strict = same & (row > col)
    zeros_half = jnp.zeros((c, d), F32)

    heads = range(GDN_HEADS)
    sl = [slice(h * d, (h + 1) * d) for h in heads]
    l2n = lambda x: x * lax.rsqrt(jnp.sum(x * x, axis=-1, keepdims=True) + EPS)
    qn = [l2n(qkv[:, sl[h]]) * (d ** -0.5) for h in heads]
    kn = [l2n(qkv[:, hd + h * d:hd + (h + 1) * d]) for h in heads]
    vh = [qkv[:, 2 * hd + h * d:2 * hd + (h + 1) * d] for h in heads]
    gcol = [gc_full[:, sl[h]] for h in heads]
    beta = [beta_full[:, sl[h]] for h in heads]
    edec = [jnp.exp(gcol[h] - gct[h:h + 1, :]) for h in heads]
    egc = [jnp.exp(x) for x in gcol]
    kb = [kn[h] * beta[h] for h in heads]
    on_k = [_mm_nt(jnp.concatenate([kb[h], qn[h]], axis=0), kn[h]) for h in heads]
    lower = [jnp.where(strict, on_k[h][:n] * edec[h], 0.0) for h in heads]
    aqk = [jnp.where(incl, on_k[h][n:] * edec[h], 0.0) for h in heads]
    tinv = _unit_lower_inverse(lower, row, col)
    sol = [_mm(tinv[h], jnp.concatenate([vh[h] * beta[h], kb[h] * egc[h]], axis=1)) for h in heads]
    qd = [qn[h] * egc[h] for h in heads]
    glast = [(gcol[h][c - 1:c, :], gcol[h][n - 1:n, :]) for h in heads]
    kdt = [(kn[h] * jnp.exp(jnp.concatenate([jnp.broadcast_to(glast[h][0], (c, d)),
                                             jnp.broadcast_to(glast[h][1], (c, d))], axis=0) - gcol[h])).T
           for h in heads]
    s0 = [s_ref[h] for h in heads]
    on_s0 = [_mm(jnp.concatenate([sol[h][:c, d:], qd[h][:c]], axis=0), s0[h]) for h in heads]
    v0 = [sol[h][:c, :d] - on_s0[h][:c] for h in heads]
    s1 = [s0[h] * jnp.exp(glast[h][0]) + _mm(kdt[h], jnp.concatenate([v0[h], zeros_half], axis=0)) for h in heads]
    on_s1 = [_mm(jnp.concatenate([sol[h][c:, d:], qd[h][c:]], axis=0), s1[h]) for h in heads]
    v1 = [sol[h][c:, :d] - on_s1[h][:c] for h in heads]
    for h in heads:
        s_ref[h] = s1[h] * jnp.exp(glast[h][1]) + _mm(kdt[h], jnp.concatenate([zeros_half, v1[h]], axis=0))
    outs = []
    for h in heads:
        o = (jnp.concatenate([on_s0[h][c:], on_s1[h][c:]], axis=0)
             + _mm(aqk[h], jnp.concatenate([v0[h], v1[h]], axis=0)))
        o = o * lax.rsqrt(jnp.mean(o * o, axis=-1, keepdims=True) + EPS) * nw_ref[...]
        outs.append(o * _silu(z[:, sl[h]]))
    o_ref[...] = jnp.concatenate(outs, axis=1)


def gated_deltanet_mixer(qkv3, y3, small3, smallt, dt_bias, a_log, norm_w):
    bsz, length, _ = y3.shape
    n = GDN_TILE
    hd = GDN_HEADS * GDN_D
    idx = np.arange(n)
    same = (idx[:, None] // GDN_CHUNK) == (idx[None, :] // GDN_CHUNK)
    tri = jnp.asarray(same & (idx[:, None] >= idx[None, :]), BF16)
    trit = jnp.asarray(same & (idx[:, None] <= idx[None, :]), BF16)
    const = lambda a: pl.BlockSpec(a.shape, lambda b, c: (0,) * a.ndim)
    args = [_pad_lanes(dt_bias), _pad_col(dt_bias), _pad_lanes(a_log), _pad_col(a_log),
            norm_w.reshape(1, -1), tri, trit]
    rows = min(GDN_STEP_ROWS, length)
    return pl.pallas_call(
        _gdn_kernel,
        grid=(bsz, length // rows),
        in_specs=[
            pl.BlockSpec((None, rows, 3 * hd), lambda b, c: (b, c, 0)),
            pl.BlockSpec((None, rows, hd), lambda b, c: (b, c, 0)),
            pl.BlockSpec((None, rows, LANES), lambda b, c: (b, c, 0)),
            pl.BlockSpec((None, 16, rows), lambda b, c: (b, 0, c)),
        ] + [const(a) for a in args],
        out_specs=pl.BlockSpec((None, rows, hd), lambda b, c: (b, c, 0)),
        out_shape=jax.ShapeDtypeStruct((bsz, length, hd), F32),
        scratch_shapes=[pltpu.VMEM((GDN_HEADS, GDN_D, GDN_D), F32)],
        compiler_params=_cparams("parallel", "arbitrary"),
        name="gated_deltanet",
    )(qkv3, y3, small3, smallt, *args)


def _sb_kernel(q_ref, k_ref, v_ref, upper_ref, o_ref):
    blk = SB_BLOCK
    pair_w = 2 * SB_HEAD_DIM
    n_pairs = SB_STEP_HEADS // 2
    i = pl.program_id(2)
    q = q_ref[...] * (SB_HEAD_DIM ** -0.5)
    lane = lax.broadcasted_iota(jnp.int32, (blk, pair_w), 1)
    first_head = lane < SB_HEAD_DIM
    qs = []
    for p in range(n_pairs):
        q2 = q[:, p * pair_w:(p + 1) * pair_w]
        qs += [jnp.where(first_head, q2, 0.0).astype(BF16), jnp.where(first_head, 0.0, q2).astype(BF16)]
    row = lax.broadcasted_iota(jnp.int32, (blk, blk), 0)
    col = lax.broadcasted_iota(jnp.int32, (blk, blk), 1)
    earlier = col < row
    upper = upper_ref[...]
    heads = range(SB_STEP_HEADS)

    def local_part(kb, diagonal, exists=None):
        start = pl.multiple_of(kb * blk, blk)
        k = k_ref[pl.ds(start, blk), :].astype(BF16)
        v = v_ref[pl.ds(start, blk), :].astype(BF16)
        kp = [k[:, p * pair_w:(p + 1) * pair_w] for p in range(n_pairs)]
        vp = [v[:, p * pair_w:(p + 1) * pair_w] for p in range(n_pairs)]
        logits = [lax.dot_general(qs[h], kp[h // 2], (((1,), (1,)), ((), ())), preferred_element_type=F32)
                  for h in heads]
        keep = earlier if diagonal else None
        if exists is not None:
            keep = exists if keep is None else keep & exists
        log_keep = [-_softplus(x) for x in logits]
        if keep is not None:
            log_keep = [jnp.where(keep, x, 0.0) for x in log_keep]
        inside = [_mm_sel_rhs(x, upper, 2) for x in log_keep]
        totals = [jnp.sum(x, axis=-1, keepdims=True) for x in log_keep]
        return logits, log_keep, inside, totals, vp, keep

    def carried_part(local, accs, sticks):
        logits, log_keep, inside, totals, vp, keep = local
        w = [jnp.exp(logits[h] + log_keep[h] + inside[h] + sticks[h]) for h in heads]
        if keep is not None:
            w = [jnp.where(keep, x, 0.0) for x in w]
        pv = [jnp.dot(w[h].astype(BF16), vp[h // 2], preferred_element_type=F32) for h in heads]
        accs = tuple(accs[p] + jnp.where(first_head, pv[2 * p], pv[2 * p + 1]) for p in range(n_pairs))
        sticks = tuple(sticks[h] + totals[h] for h in heads)
        return accs, sticks

    accs = tuple(jnp.zeros((blk, pair_w), F32) for _ in range(n_pairs))
    sticks = tuple(jnp.zeros((blk, 1), F32) for _ in heads)
    eager = [local_part(i, True)]
    for back in range(1, SB_EAGER_BLOCKS + 1):
        eager.append(local_part(jnp.maximum(i - back, 0), False, exists=(row >= 0) & (i - back >= 0)))
    for local in eager:
        accs, sticks = carried_part(local, accs, sticks)

    def alive(state):
        kb, _, sticks = state
        longest = sticks[0]
        for s in sticks[1:]:
            longest = jnp.maximum(longest, s)
        return (kb >= 0) & (jnp.max(longest) > SB_LOG_ZERO)

    def body(state):
        kb, accs, sticks = state
        accs, sticks = carried_part(local_part(kb, False), accs, sticks)
        return kb - 1, accs, sticks

    _, accs, _ = lax.while_loop(alive, body, (i - 1 - SB_EAGER_BLOCKS, accs, sticks))
    o_ref[...] = jnp.concatenate(accs, axis=1)


def stick_breaking_mixer(y3, col0):
    bsz, length, _ = y3.shape
    blk = SB_BLOCK
    step_w = SB_STEP_HEADS * SB_HEAD_DIM
    steps = SB_DIM // step_w
    q0 = col0 // step_w
    idx = np.arange(blk)
    upper = jnp.asarray(idx[:, None] > idx[None, :], BF16)
    resident = lambda off: pl.BlockSpec((None, length, step_w), lambda b, p, i: (b, 0, q0 + off + p),
                                        pipeline_mode=pl.Buffered(1))
    return pl.pallas_call(
        _sb_kernel,
        grid=(bsz, steps, length // blk),
        in_specs=[
            pl.BlockSpec((None, blk, step_w), lambda b, p, i: (b, i, q0 + p)),
            resident(steps),
            resident(2 * steps),
            pl.BlockSpec((blk, blk), lambda b, p, i: (0, 0)),
        ],
        out_specs=pl.BlockSpec((None, blk, step_w), lambda b, p, i: (b, i, p)),
        out_shape=jax.ShapeDtypeStruct((bsz, length, SB_DIM), F32),
        compiler_params=_cparams("parallel", "parallel", "arbitrary"),
        name="stick_breaking",
    )(y3, y3, y3, upper)


def _mixer_out(a_ref, b_ref, h_ref, wa_ref, wb_ref, rows):
    return h_ref[rows, :] + (jnp.dot(a_ref[rows, :].astype(BF16), wa_ref[...], preferred_element_type=F32)
                             + jnp.dot(b_ref[rows, :].astype(BF16), wb_ref[...], preferred_element_type=F32))


def _cross_attention(hs, g_ref, wq_ref, kt_ref, v_ref, wo_ref):
    us = [_rms(h, g_ref[...]).astype(BF16) for h in hs]
    qs = [jnp.dot(u, wq_ref[...], preferred_element_type=F32) for u in us]
    heads = [[] for _ in hs]
    for hd in range(XA_HEADS):
        sl = slice(hd * XA_HEAD_DIM, (hd + 1) * XA_HEAD_DIM)
        ss = [jnp.dot(q[:, sl].astype(BF16), kt_ref[sl, :], preferred_element_type=F32) * (XA_HEAD_DIM ** -0.5)
              for q in qs]
        ps = [jnp.exp(s - jnp.max(s, axis=-1, keepdims=True)) for s in ss]
        ps = [p / jnp.sum(p, axis=-1, keepdims=True) for p in ps]
        for k, p in enumerate(ps):
            heads[k].append(jnp.dot(p.astype(BF16), v_ref[:, sl], preferred_element_type=F32))
    os_ = [jnp.concatenate(hk, axis=1).astype(BF16) for hk in heads]
    return [h + jnp.dot(o, wo_ref[...], preferred_element_type=F32) for h, o in zip(hs, os_)]


def _route(xn, whi_ref, wlo_ref, b_ref, before_ref, run_ref):
    x_hi = xn.astype(BF16)
    x_lo = (xn - x_hi.astype(F32)).astype(BF16)
    logits = (jnp.dot(x_hi, whi_ref[...], preferred_element_type=F32)
              + jnp.dot(x_lo, whi_ref[...], preferred_element_type=F32)
              + jnp.dot(x_hi, wlo_ref[...], preferred_element_type=F32) + b_ref[...])
    lane = lax.broadcasted_iota(jnp.int32, logits.shape, 1).astype(F32)
    neg = -1e30
    none = float(LANES)

    def top(vals):
        best = jnp.max(vals, axis=-1, keepdims=True)
        where = jnp.min(jnp.where(vals == best, lane, none), axis=-1, keepdims=True)
        return best, where

    gl = jnp.where(lane < MOE_GROUPS, logits, neg)
    gbest, gsel = top(gl)
    gprob = 1.0 / jnp.sum(jnp.exp(gl - gbest), axis=-1, keepdims=True)
    lo = MOE_GROUPS + gsel * MOE_PER_GROUP
    el = jnp.where((lane >= lo) & (lane < lo + MOE_PER_GROUP), logits, neg)
    m1, i1 = top(el)
    m2, i2 = top(jnp.where(lane == i1, neg, el))
    e = jnp.exp(m2 - m1)
    gate1 = gprob / (1.0 + e)
    gate2 = gprob * e / (1.0 + e)

    hot1 = lane == i1
    hot2 = lane == i2
    one1 = jnp.where(hot1, 1.0, 0.0)
    one2 = jnp.where(hot2, 1.0, 0.0)
    before = before_ref[...]
    prefix1 = jnp.dot(before, one1.astype(BF16), preferred_element_type=F32)
    prefix2 = jnp.dot(before, one2.astype(BF16), preferred_element_type=F32)
    total1 = jnp.sum(one1, axis=0, keepdims=True)
    running = run_ref[...]
    rank1 = jnp.sum(jnp.where(hot1, prefix1 + running, 0.0), axis=-1, keepdims=True)
    rank2 = jnp.sum(jnp.where(hot2, prefix2 + (running + total1), 0.0), axis=-1, keepdims=True)
    running = running + total1 + jnp.sum(one2, axis=0, keepdims=True)
    run_ref[...] = running

    fields = (i1 - MOE_GROUPS, i2 - MOE_GROUPS, gate1, gate2, rank1, rank2)
    out = jnp.zeros_like(logits)
    for k, val in enumerate(fields):
        out = jnp.where(lane == k, val, out)
    return out


def _post_mixer_kernel(a_ref, b_ref, h_ref, wa_ref, wb_ref, gxa_ref, wq_ref, kt_ref, v_ref, wo_ref,
                       gffn_ref, whi_ref, wlo_ref, bias_ref, before_ref,
                       h_out_ref, xn_ref, r_ref, cnt_ref, run_ref):
    @pl.when(pl.program_id(0) == 0)
    def _():
        run_ref[...] = jnp.zeros_like(run_ref)

    h = _mixer_out(a_ref, b_ref, h_ref, wa_ref, wb_ref, slice(None))
    h, = _cross_attention([h], gxa_ref, wq_ref, kt_ref, v_ref, wo_ref)
    h_out_ref[...] = h
    xn = _rms(h, gffn_ref[...])
    xn_ref[...] = _pack_halves(xn)
    r_ref[...] = _route(xn, whi_ref, wlo_ref, bias_ref, before_ref, run_ref)
    cnt_ref[...] = run_ref[...]


def post_mixer(ya, yb, h, wa, wb, g_xa, wq, kt, v, wo, g_ffn, w_hi, w_lo, bias, tm=512):
    m, d = h.shape
    tiles_per_batch = m // kt.shape[0] // tm
    idx = np.arange(tm)
    before = jnp.asarray(idx[:, None] > idx[None, :], BF16)
    rows = lambda w: pl.BlockSpec((tm, w), lambda i: (i, 0))
    const = lambda a: pl.BlockSpec(a.shape, lambda i: (0,) * a.ndim, pipeline_mode=pl.Buffered(1))
    per_batch = lambda a: pl.BlockSpec((None,) + a.shape[1:], lambda i: (i // tiles_per_batch, 0, 0))
    g_xa, g_ffn = g_xa.reshape(1, d), g_ffn.reshape(1, d)
    return pl.pallas_call(
        _post_mixer_kernel,
        grid=(m // tm,),
        in_specs=[rows(ya.shape[1]), rows(yb.shape[1]), rows(d), const(wa), const(wb), const(g_xa), const(wq),
                  per_batch(kt), per_batch(v), const(wo), const(g_ffn), const(w_hi), const(w_lo), const(bias),
                  const(before)],
        out_specs=[rows(d), rows(d // 2), rows(LANES), pl.BlockSpec((1, LANES), lambda i: (0, 0))],
        out_shape=[jax.ShapeDtypeStruct((m, d), F32), jax.ShapeDtypeStruct((m, d // 2), jnp.int32),
                   jax.ShapeDtypeStruct((m, LANES), F32), jax.ShapeDtypeStruct((1, LANES), F32)],
        scratch_shapes=[pltpu.VMEM((1, LANES), F32)],
        compiler_params=_cparams("arbitrary"),
        name="post_mixer",
    )(ya, yb, h, wa, wb, g_xa, wq, kt, v, wo, g_ffn, w_hi, w_lo, bias, before)


def _expert_kernel(table_ref, x_ref, wg_hbm, wu_hbm, wd_hbm, o_ref,
                   wg32_ref, wu32_ref, wd32_ref, wgb_ref, wub_ref, wdb_ref, sem_ref, *, layer):
    i = pl.program_id(0)
    beid_ref, valid_ref, first_ref, slot_ref, next_ref = (table_ref.at[k] for k in range(5))
    valid = valid_ref[i]

    def weight_copies(expert, slot):
        return (pltpu.make_async_copy(wg_hbm.at[layer, expert], wg32_ref.at[slot], sem_ref.at[slot, 0]),
                pltpu.make_async_copy(wu_hbm.at[layer, expert], wu32_ref.at[slot], sem_ref.at[slot, 1]),
                pltpu.make_async_copy(wd_hbm.at[layer, expert], wd32_ref.at[slot], sem_ref.at[slot, 2]))

    @pl.when(i == 0)
    def _():
        for copy in weight_copies(beid_ref[0], 0):
            copy.start()

    @pl.when(first_ref[i] == 1)
    def _():
        slot = slot_ref[i]
        for copy in weight_copies(beid_ref[i], slot):
            copy.wait()
        wgb_ref[...] = wg32_ref[slot].astype(BF16)
        wub_ref[...] = wu32_ref[slot].astype(BF16)
        wdb_ref[...] = wd32_ref[slot].astype(BF16)

        @pl.when(next_ref[i] >= 0)
        def _():
            for copy in weight_copies(next_ref[i], 1 - slot):
                copy.start()

    half = MOE_ROWS // 2

    def ffn(n_halves):
        row = lax.broadcasted_iota(jnp.int32, (half, 2 * x_ref.shape[1]), 0)
        xs = [jnp.where(row + k * half < valid, _unpack_halves(x_ref[k * half:(k + 1) * half, :]), 0.0).astype(BF16)
              for k in range(n_halves)]
        gates = [jnp.dot(x, wgb_ref[...], preferred_element_type=F32) for x in xs]
        ups = [jnp.dot(x, wub_ref[...], preferred_element_type=F32) for x in xs]
        acts = [(_silu(g) * u).astype(BF16) for g, u in zip(gates, ups)]
        for k, act in enumerate(acts):
            o_ref[k * half:(k + 1) * half, :] = _pack_halves(jnp.dot(act, wdb_ref[...], preferred_element_type=F32))

    @pl.when(valid > half)
    def _():
        ffn(2)

    @pl.when((valid > 0) & (valid <= half))
    def _():
        ffn(1)
        o_ref[half:, :] = jnp.zeros((half, o_ref.shape[1]), o_ref.dtype)

    @pl.when(valid == 0)
    def _():
        o_ref[...] = jnp.zeros_like(o_ref)


def moe_experts(blocks, xs, w_gate, w_up, w_down, layer):
    n_slots, packed = xs.shape
    d = 2 * packed
    rows = MOE_ROWS
    ff = w_gate.shape[3]
    grid_spec = pltpu.PrefetchScalarGridSpec(
        num_scalar_prefetch=1,
        grid=(n_slots // rows,),
        in_specs=[
            pl.BlockSpec((rows, packed), lambda i, *_: (i, 0)),
            pl.BlockSpec(memory_space=pl.ANY),
            pl.BlockSpec(memory_space=pl.ANY),
            pl.BlockSpec(memory_space=pl.ANY),
        ],
        out_specs=pl.BlockSpec((rows, packed), lambda i, *_: (i, 0)),
        scratch_shapes=[pltpu.VMEM((2, d, ff), F32), pltpu.VMEM((2, d, ff), F32), pltpu.VMEM((2, ff, d), F32),
                        pltpu.VMEM((d, ff), BF16), pltpu.VMEM((d, ff), BF16), pltpu.VMEM((ff, d), BF16),
                        pltpu.SemaphoreType.DMA((2, 3))],
    )
    return pl.pallas_call(
        functools.partial(_expert_kernel, layer=layer),
        grid_spec=grid_spec,
        out_shape=jax.ShapeDtypeStruct((n_slots, packed), jnp.int32),
        compiler_params=_cparams("arbitrary"),
        name="moe_experts",
    )(blocks, xs, w_gate, w_up, w_down)


def _sc_mesh():
    return plsc.VectorSubcoreMesh(core_axis_name="c", subcore_axis_name="s",
                                  num_cores=SC_CORES, num_subcores=SC_SUBCORES)


def _sc_worker():
    return lax.axis_index("s") * SC_CORES + lax.axis_index("c")


def sc_scatter_rows(x, dest, n_slots):
    n_tok, d = x.shape
    per_worker = n_tok // SC_WORKERS
    n_chunks = per_worker // SC_CHUNK
    by_worker = dest.reshape(dest.shape[0] * SC_WORKERS, n_chunks, SC_CHUNK)

    @functools.partial(
        pl.kernel, mesh=_sc_mesh(), out_type=jax.ShapeDtypeStruct((n_slots, d), x.dtype),
        scratch_types=[pltpu.VMEM((n_chunks, SC_CHUNK), jnp.int32), pltpu.VMEM((n_chunks, SC_CHUNK), jnp.int32),
                       pltpu.VMEM((SC_CHUNK, d), x.dtype)],
        name="moe_scatter_rows")
    def scatter(x_hbm, dest_hbm, out_hbm, i0_v, i1_v, rows_v):
        wid = _sc_worker()
        pltpu.sync_copy(dest_hbm.at[wid], i0_v)
        pltpu.sync_copy(dest_hbm.at[SC_WORKERS + wid], i1_v)

        @pl.loop(0, n_chunks)
        def _(j):
            start = pl.multiple_of(wid * per_worker + j * SC_CHUNK, SC_CHUNK)
            pltpu.sync_copy(x_hbm.at[pl.ds(start, SC_CHUNK)], rows_v)
            pltpu.sync_copy(rows_v, out_hbm.at[i0_v.at[j]])
            pltpu.sync_copy(rows_v, out_hbm.at[i1_v.at[j]])

    return scatter(x, by_worker)


def sc_gather_rows(table, idx, n_out):
    d = table.shape[1]
    per_worker = n_out // SC_WORKERS
    n_chunks = per_worker // SC_CHUNK

    @functools.partial(
        pl.kernel, mesh=_sc_mesh(), out_type=jax.ShapeDtypeStruct((n_out, d), table.dtype),
        scratch_types=[pltpu.VMEM((n_chunks, SC_CHUNK), jnp.int32), pltpu.VMEM((SC_CHUNK, d), table.dtype)],
        name="moe_gather_rows")
    def gather(table_hbm, idx_hbm, out_hbm, idx_v, rows_v):
        wid = _sc_worker()
        pltpu.sync_copy(idx_hbm.at[wid], idx_v)

        @pl.loop(0, n_chunks)
        def _(j):
            start = pl.multiple_of(wid * per_worker + j * SC_CHUNK, SC_CHUNK)
            pltpu.sync_copy(table_hbm.at[idx_v.at[j]], rows_v)
            pltpu.sync_copy(rows_v, out_hbm.at[pl.ds(start, SC_CHUNK)])

    return gather(table, idx.reshape(-1, n_chunks, SC_CHUNK))


def _combine_kernel(h_ref, y0_ref, y1_ref, r_ref, g_ref, o_ref, *, final_norm):
    route = r_ref[...]
    h = h_ref[...] + (route[:, 2:3] * _unpack_halves(y0_ref[...]) + route[:, 3:4] * _unpack_halves(y1_ref[...]))
    o_ref[...] = _rms(h, g_ref[...]) if final_norm else h


def moe_combine(h, y01, route, g, final_norm, tm=1024):
    m, d = h.shape
    tm = min(tm, m)
    rows = lambda w: pl.BlockSpec((tm, w), lambda i: (i, 0))
    return pl.pallas_call(
        functools.partial(_combine_kernel, final_norm=final_norm),
        grid=(m // tm,),
        in_specs=[rows(d), rows(d // 2), pl.BlockSpec((tm, d // 2), lambda i: (i + m // tm, 0)), rows(LANES),
                  pl.BlockSpec((1, d), lambda i: (0, 0))],
        out_specs=rows(d),
        out_shape=jax.ShapeDtypeStruct((m, d), F32),
        compiler_params=_cparams("parallel"),
        name="moe_combine",
    )(h, y01, y01, route, g.reshape(1, d))


def _pad_cols(w):
    return jnp.pad(w, ((0, 0), (0, LANES - w.shape[1])))


def _plan_kernel(route_ref, cnt_ref, incl_ref, dest_ref, table_ref):
    f32_sum = lambda x, axis: jnp.sum(x, axis=axis, keepdims=True)
    lane = lax.broadcasted_iota(jnp.int32, (LANES, LANES), 1)
    sub = lax.broadcasted_iota(jnp.int32, (LANES, LANES), 0)
    incl = incl_ref[...]
    is_expert = (lane >= MOE_GROUPS) & (lane < MOE_GROUPS + MOE_EXPERTS)
    shift = MOE_ROWS.bit_length() - 1
    counts = jnp.broadcast_to(cnt_ref[...], (LANES, LANES)).astype(jnp.int32)
    padded = jnp.where(is_expert, ((counts + (MOE_ROWS - 1)) >> shift) << shift, 0)
    pad_end = _mm_sel_rhs(padded.astype(F32), incl)
    pad_start = pad_end - padded.astype(F32)

    route = route_ref[...]
    lane_t = lax.broadcasted_iota(jnp.int32, route.shape, 1)
    lane_f = lane_t.astype(F32)
    start_row = pad_start[0:1, :]
    slots = [f32_sum(jnp.where(lane_f == route[:, k:k + 1] + MOE_GROUPS, start_row, 0.0), 1) + route[:, 4 + k:5 + k]
             for k in range(2)]
    both = jnp.where(lane_t == 0, slots[0], jnp.where(lane_t == 1, slots[1], 0.0))
    dest_ref[...] = both.T[0:8, :].astype(jnp.int32)

    on_sub = lambda rows_equal: rows_equal.T
    expert_sub = (sub >= MOE_GROUPS) & (sub < MOE_GROUPS + MOE_EXPERTS)
    block_start = (lane * MOE_ROWS).astype(F32)
    eid = f32_sum(jnp.where(expert_sub & (on_sub(pad_end) <= block_start), 1.0, 0.0), 0)
    eid = jnp.minimum(eid, float(MOE_EXPERTS - 1))
    filled = on_sub(pad_start + counts.astype(F32))
    own = (sub - MOE_GROUPS).astype(F32) == eid
    valid = jnp.clip(f32_sum(jnp.where(own, filled, 0.0), 0) - block_start[0:1, :], 0.0, float(MOE_ROWS))
    eid_rows = jnp.broadcast_to(eid, (LANES, LANES))
    changed = (lane == 0) | (eid_rows != pltpu.roll(eid_rows, 1, axis=1))
    first = jnp.where((jnp.broadcast_to(valid, (LANES, LANES)) > 0) & changed, 1.0, 0.0)
    ordinal = _mm_sel_rhs(first, incl) - 1.0
    slot = ordinal - 2.0 * jnp.floor(ordinal * 0.5)
    later = (on_sub(first) > 0) & (sub > lane)
    nearest = jnp.min(jnp.where(later, sub, LANES), axis=0, keepdims=True)
    next_eid = f32_sum(jnp.where(sub == nearest, on_sub(eid_rows), 0.0), 0)
    next_eid = jnp.where(nearest < LANES, next_eid, -1.0)
    row8 = lax.broadcasted_iota(jnp.int32, (8, LANES), 0)
    table = jnp.zeros((8, LANES), F32)
    for k, val in enumerate((eid, valid, first[0:1, :], slot[0:1, :], next_eid)):
        table = jnp.where(row8 == k, val, table)
    table_ref[...] = table.astype(jnp.int32)


def moe_plan(route, counts, tm=2048):
    n_tok = route.shape[0]
    tm = min(tm, n_tok)
    idx = np.arange(LANES)
    incl = jnp.asarray(idx[:, None] <= idx[None, :], BF16)
    return pl.pallas_call(
        _plan_kernel,
        grid=(n_tok // tm,),
        in_specs=[pl.BlockSpec((tm, LANES), lambda i: (i, 0)), pl.BlockSpec((1, LANES), lambda i: (0, 0)),
                  pl.BlockSpec((LANES, LANES), lambda i: (0, 0))],
        out_specs=[pl.BlockSpec((8, tm), lambda i: (0, i)), pl.BlockSpec((8, LANES), lambda i: (0, 0))],
        out_shape=[jax.ShapeDtypeStruct((8, n_tok), jnp.int32), jax.ShapeDtypeStruct((8, LANES), jnp.int32)],
        compiler_params=_cparams("arbitrary"),
        name="moe_plan",
    )(route, counts, incl)


def _router_weights(w_group, b_group, w_expert, b_expert):
    w_r = _pad_cols(jnp.concatenate([w_group, w_expert], axis=1))
    w_hi = w_r.astype(BF16)
    w_lo = (w_r - w_hi.astype(F32)).astype(BF16)
    return w_hi, w_lo, _pad_lanes(jnp.concatenate([b_group, b_expert]))


def _moe_layer(h, xn, route, counts, w_gate, w_up, w_down, layer, final_g):
    n_tok, d = h.shape
    n_blocks = -(-(2 * n_tok + MOE_EXPERTS * (MOE_ROWS - 1)) // MOE_ROWS)
    dest, blocks = moe_plan(route, counts)
    xs = sc_scatter_rows(xn, dest, n_blocks * MOE_ROWS)
    ys = moe_experts(blocks, xs, w_gate, w_up, w_down, layer)
    y01 = sc_gather_rows(ys, dest, 2 * n_tok)
    g = jnp.ones((d,), F32) if final_g is None else final_g
    return moe_combine(h, y01, route, g, final_g is not None)


def _memory_kv(memn_in, mem_norm, wk, wv):
    bsz, m, d = memn_in.shape
    w = jnp.concatenate([wk, wv], axis=1).astype(BF16)
    kv, _ = rms_matmul(memn_in.reshape(bsz * m, d), mem_norm, w, jnp.zeros((d, LANES), BF16))
    k = kv[:, :d].reshape(bsz, m, d)
    v = kv[:, d:].reshape(bsz, m, d)
    return jnp.swapaxes(k, 1, 2).astype(BF16), v.astype(BF16)


def kernel(x, mem, mem_norm, final_norm, norm_mix, norm_xa, norm_ffn, xa_wq, xa_wk, xa_wv, xa_wo, moe_w_group, moe_b_group, moe_w_expert, moe_b_expert, moe_w_gate, moe_w_up, moe_w_down, ev_w_in, ev_sc_conv, ev_ssm_conv_w, ev_ssm_conv_b, ev_ssm_dt_bias, ev_ssm_a_log, ev_ssm_d, ev_ssm_norm, ev_w_out, od_w_in, od_gdn_conv, od_gdn_dt_bias, od_gdn_a_log, od_gdn_norm, od_w_out):
    bsz, length, d = x.shape
    n_tok = bsz * length
    depth = norm_mix.shape[0]
    h = x.reshape(n_tok, d)
    for layer in range(depth):
        i = layer // 2
        if layer % 2 == 0:
            w = ev_w_in[i]
            z0 = 3 * SC_DIM
            xbc0 = z0 + SSM_INNER
            w_conv = w[:, xbc0:xbc0 + SSM_XBC].astype(BF16)
            w_small = _pad_cols(w[:, xbc0 + SSM_XBC:]).astype(BF16)
            xbc = rms_matmul_conv(h, norm_mix[layer], w_conv, ev_ssm_conv_w[i], ev_ssm_conv_b[i], length)
            z, ya, small = rms_matmul_gated(h, norm_mix[layer], w[:, z0:xbc0].astype(BF16), w[:, :z0].astype(BF16),
                                            w_small, ev_sc_conv[i], length)
            small3 = small.reshape(bsz, length, LANES)
            smallt = jnp.swapaxes(small3[:, :, :16], 1, 2)
            yb = ssd_mixer(xbc.reshape(bsz, length, -1), z.reshape(bsz, length, -1), small3, smallt,
                           ev_ssm_dt_bias[i], ev_ssm_a_log[i], ev_ssm_d[i], ev_ssm_norm[i])
            w_out = ev_w_out[i].astype(BF16)
            split = SC_DIM
        else:
            w = od_w_in[i]
            qkv_w = 3 * GDN_HEADS * GDN_D
            z_end = qkv_w + GDN_HEADS * GDN_D
            w_conv = w[:, :qkv_w].astype(BF16)
            w_main = jnp.concatenate([w[:, qkv_w:z_end], w[:, z_end + 2 * GDN_HEADS:]], axis=1).astype(BF16)
            w_small = _pad_cols(w[:, z_end:z_end + 2 * GDN_HEADS]).astype(BF16)
            qkv = rms_matmul_conv(h, norm_mix[layer], w_conv, od_gdn_conv[i], jnp.zeros((qkv_w,), F32), length)
            y, small = rms_matmul(h, norm_mix[layer], w_main, w_small, tm=1024, tn=w_main.shape[1])
            y3 = y.reshape(bsz, length, -1)
            small3 = small.reshape(bsz, length, LANES)
            smallt = jnp.swapaxes(small3[:, :, :16], 1, 2)
            ya = gated_deltanet_mixer(qkv.reshape(bsz, length, -1), y3, small3, smallt, od_gdn_dt_bias[i],
                                      od_gdn_a_log[i], od_gdn_norm[i])
            yb = stick_breaking_mixer(y3, GDN_HEADS * GDN_D)
            w_out = od_w_out[i].astype(BF16)
            split = GDN_HEADS * GDN_D
        kt, v = _memory_kv(mem, mem_norm, xa_wk[layer], xa_wv[layer])
        w_hi, w_lo, bias = _router_weights(moe_w_group[layer], moe_b_group[layer], moe_w_expert[layer],
                                           moe_b_expert[layer])
        h, xn, route, counts = post_mixer(
            ya.reshape(n_tok, -1), yb.reshape(n_tok, -1), h, w_out[:split], w_out[split:], norm_xa[layer],
            xa_wq[layer].astype(BF16), kt, v, xa_wo[layer].astype(BF16), norm_ffn[layer], w_hi, w_lo, bias)
        h = _moe_layer(h, xn, route, counts, moe_w_gate, moe_w_up, moe_w_down, layer,
                       final_norm if layer == depth - 1 else None)
    return h.reshape(bsz, length, d)
```

```python
import functools

import jax
import jax.numpy as jnp
import numpy as np
from jax import lax
from jax.experimental import pallas as pl
from jax.experimental.pallas import tpu as pltpu
from jax.experimental.pallas import tpu_sc as plsc

F32 = jnp.float32
BF16 = jnp.bfloat16
EPS = 1e-6

D_MODEL = 1024
MEM_LEN = 256
SC_DIM = 512
SSM_HEADS = 16
SSM_HEAD_DIM = 64
SSM_INNER = 1024
SSM_GROUPS = 2
SSM_STATE = 128
SSM_XBC = SSM_INNER + 2 * SSM_GROUPS * SSM_STATE
SSD_CHUNK = 128
SSD_STEP_ROWS = 512
GDN_HEADS = 8
GDN_D = 128
GDN_CHUNK = 64
GDN_TILE = 128
GDN_STEP_ROWS = 512
SB_HEADS = 8
SB_HEAD_DIM = 64
SB_DIM = 512
SB_BLOCK = 128
SB_STEP_HEADS = 8
SB_EAGER_BLOCKS = 2
XA_HEADS = 4
XA_HEAD_DIM = 256
MOE_GROUPS = 4
MOE_PER_GROUP = 8
MOE_EXPERTS = 32
MOE_FF = 512
MOE_ROWS = 512
POST_GROUP_ROWS = 512
SC_CORES = 2
SC_SUBCORES = 16
SC_WORKERS = SC_CORES * SC_SUBCORES
SC_CHUNK = 64
HALO = 8
CONV_CHUNK = 512
LANES = 128
SB_LOG_ZERO = -104.0
VMEM_LIMIT = 56 * 1024 * 1024


def _cparams(*sem):
    return pltpu.CompilerParams(dimension_semantics=sem, vmem_limit_bytes=VMEM_LIMIT)


def _mm(a, b):
    return jnp.dot(a.astype(BF16), b.astype(BF16), preferred_element_type=F32)


def _mm_nt(a, b):
    return lax.dot_general(a.astype(BF16), b.astype(BF16), (((1,), (1,)), ((), ())),
                           preferred_element_type=F32)


def _split_bf16(x, n):
    parts, r = [], x
    for _ in range(n):
        p = r.astype(BF16)
        parts.append(p)
        r = r - p.astype(F32)
    return parts


def _mm_sel_rhs(x, sel, n=3):
    return sum(jnp.dot(p, sel, preferred_element_type=F32) for p in _split_bf16(x, n))


def _mm_sel_lhs(sel, x, n=3):
    return sum(jnp.dot(sel, p, preferred_element_type=F32) for p in _split_bf16(x, n))


def _spread_heads(x, first, n_heads, width):
    rows = x.shape[0]
    col = lambda h: jnp.broadcast_to(x[:, first + h:first + h + 1], (rows, LANES))
    if width == LANES:
        return jnp.concatenate([col(h) for h in range(n_heads)], axis=1)
    left = lax.broadcasted_iota(jnp.int32, (rows, LANES), 1) < width
    return jnp.concatenate([jnp.where(left, col(h), col(h + 1)) for h in range(0, n_heads, 2)], axis=1)


def _pack_halves(x):
    n = x.shape[1] // 2
    lo = pltpu.bitcast(x[:, :n].astype(BF16).astype(F32), jnp.int32)
    hi = pltpu.bitcast(x[:, n:].astype(BF16).astype(F32), jnp.int32)
    return lax.shift_right_logical(lo, 16) | (hi & jnp.int32(-65536))


def _unpack_halves(p):
    lo = pltpu.bitcast(lax.shift_left(p, 16), F32)
    hi = pltpu.bitcast(p & jnp.int32(-65536), F32)
    return jnp.concatenate([lo, hi], axis=1)


def _silu(x):
    return x * jax.nn.sigmoid(x)


def _softplus(x):
    return jnp.maximum(x, 0.0) + jnp.log(1.0 + jnp.exp(-jnp.abs(x)))


def _rms(x, g):
    return x * lax.rsqrt(jnp.mean(x * x, axis=-1, keepdims=True) + EPS) * g


def _rms_matmul_kernel(x_ref, g_ref, w_ref, ws_ref, o_ref, os_ref):
    xn = _rms(x_ref[...], g_ref[...]).astype(BF16)
    o_ref[...] = jnp.dot(xn, w_ref[...], preferred_element_type=F32)
    os_ref[...] = jnp.dot(xn, ws_ref[...], preferred_element_type=F32)


def rms_matmul(x, g, w, ws, tm=512, tn=512):
    m, k = x.shape
    n = w.shape[1]
    tm = min(tm, m)
    main, small = pl.pallas_call(
        _rms_matmul_kernel,
        grid=(n // tn, m // tm),
        in_specs=[
            pl.BlockSpec((tm, k), lambda j, i: (i, 0)),
            pl.BlockSpec((1, k), lambda j, i: (0, 0)),
            pl.BlockSpec((k, tn), lambda j, i: (0, j)),
            pl.BlockSpec((k, LANES), lambda j, i: (0, 0)),
        ],
        out_specs=[
            pl.BlockSpec((tm, tn), lambda j, i: (i, j)),
            pl.BlockSpec((None, tm, LANES), lambda j, i: (j, i, 0)),
        ],
        out_shape=[jax.ShapeDtypeStruct((m, n), F32), jax.ShapeDtypeStruct((n // tn, m, LANES), F32)],
        compiler_params=_cparams("parallel", "parallel"),
        name="rms_matmul",
    )(x, g.reshape(1, k), w, ws)
    return main, small[0]


def _causal_conv(ext_ref, w_ref, rows):
    width = w_ref.shape[0]
    ext = ext_ref[...]
    acc = None
    for j in range(width):
        shift = width - 1 - j
        moved = ext if shift == 0 else pltpu.roll(ext, shift, axis=0)
        term = w_ref[j:j + 1, :] * moved[HALO:HALO + rows, :]
        acc = term if acc is None else acc + term
    return acc


def _rms_matmul_conv_kernel(x_ref, g_ref, w_ref, cw_ref, cb_ref, o_ref, *ext_refs, tiles_per_seq):
    tm = x_ref.shape[0]
    starts_sequence = pl.program_id(1) % tiles_per_seq == 0

    @pl.when(starts_sequence)
    def _():
        for ext_ref in ext_refs:
            ext_ref[0:HALO, :] = jnp.zeros((HALO, CONV_CHUNK), F32)

    @pl.when(jnp.logical_not(starts_sequence))
    def _():
        for ext_ref in ext_refs:
            ext_ref[0:HALO, :] = ext_ref[tm:tm + HALO, :]

    xn = _rms(x_ref[...], g_ref[...]).astype(BF16)
    for c, ext_ref in enumerate(ext_refs):
        cols = slice(c * CONV_CHUNK, (c + 1) * CONV_CHUNK)
        ext_ref[HALO:, :] = jnp.dot(xn, w_ref[:, cols], preferred_element_type=F32)
        o_ref[:, cols] = _causal_conv(ext_ref, cw_ref.at[:, cols], tm) + cb_ref[:, cols]


def rms_matmul_conv(x, g, w, conv_w, conv_b, seq_len, tm=1024, tn=1536):
    m, k = x.shape
    n = w.shape[1]
    cols = lambda rows: pl.BlockSpec((rows, tn), lambda j, i: (0, j))
    return pl.pallas_call(
        functools.partial(_rms_matmul_conv_kernel, tiles_per_seq=seq_len // tm),
        grid=(n // tn, m // tm),
        in_specs=[
            pl.BlockSpec((tm, k), lambda j, i: (i, 0)),
            pl.BlockSpec((1, k), lambda j, i: (0, 0)),
            cols(k), cols(conv_w.shape[0]), cols(1),
        ],
        out_specs=pl.BlockSpec((tm, tn), lambda j, i: (i, j)),
        out_shape=jax.ShapeDtypeStruct((m, n), F32),
        scratch_shapes=[pltpu.VMEM((tm + HALO, CONV_CHUNK), F32)] * (tn // CONV_CHUNK),
        compiler_params=_cparams("arbitrary", "arbitrary"),
        name="rms_matmul_conv",
    )(x, g.reshape(1, k), w, conv_w, conv_b.reshape(1, n))


def _rms_matmul_gated_kernel(x_ref, g_ref, wz_ref, wbcx_ref, ws_ref, cw_ref, z_ref, ya_ref, os_ref, ext_ref,
                             *, tiles_per_seq):
    tm = x_ref.shape[0]
    starts_sequence = pl.program_id(0) % tiles_per_seq == 0

    @pl.when(starts_sequence)
    def _():
        ext_ref[0:HALO, :] = jnp.zeros((HALO, SC_DIM), F32)

    @pl.when(jnp.logical_not(starts_sequence))
    def _():
        ext_ref[0:HALO, :] = ext_ref[tm:tm + HALO, :]

    xn = _rms(x_ref[...], g_ref[...]).astype(BF16)
    z_ref[...] = jnp.dot(xn, wz_ref[...], preferred_element_type=F32)
    os_ref[...] = jnp.dot(xn, ws_ref[...], preferred_element_type=F32)
    bcx = jnp.dot(xn, wbcx_ref[...], preferred_element_type=F32)
    ext_ref[HALO:, :] = bcx[:, SC_DIM:2 * SC_DIM] * bcx[:, 2 * SC_DIM:]
    ya_ref[...] = bcx[:, :SC_DIM] * _causal_conv(ext_ref, cw_ref, tm)


def rms_matmul_gated(x, g, w_z, w_bcx, w_small, conv_w, seq_len, tm=1024):
    m, k = x.shape
    const = lambda a: pl.BlockSpec(a.shape, lambda i: (0,) * a.ndim)
    rows = lambda w: pl.BlockSpec((tm, w), lambda i: (i, 0))
    g = g.reshape(1, k)
    return pl.pallas_call(
        functools.partial(_rms_matmul_gated_kernel, tiles_per_seq=seq_len // tm),
        grid=(m // tm,),
        in_specs=[rows(k), const(g), const(w_z), const(w_bcx), const(w_small), const(conv_w)],
        out_specs=[rows(w_z.shape[1]), rows(SC_DIM), rows(LANES)],
        out_shape=[jax.ShapeDtypeStruct((m, w_z.shape[1]), F32), jax.ShapeDtypeStruct((m, SC_DIM), F32),
                   jax.ShapeDtypeStruct((m, LANES), F32)],
        scratch_shapes=[pltpu.VMEM((tm + HALO, SC_DIM), F32)],
        compiler_params=_cparams("arbitrary"),
        name="rms_matmul_gated",
    )(x, g, w_z, w_bcx, w_small, conv_w)


def _ssd_kernel(xbc_ref, z_ref, dt_ref, dtt_ref, dtb_r_ref, dtb_c_ref,
                alog_r_ref, alog_c_ref, d_ref, nw_ref, tri_ref, trit_ref,
                o_ref, s_ref):
    q = SSD_CHUNK

    @pl.when(pl.program_id(1) == 0)
    def _():
        s_ref[...] = jnp.zeros_like(s_ref)

    for sub in range(xbc_ref.shape[0] // q):
        rows = slice(sub * q, (sub + 1) * q)
        _ssd_chunk(_silu(xbc_ref[rows, :]), z_ref[rows, :], dt_ref[rows, :], dtt_ref[:, rows], dtb_r_ref, dtb_c_ref,
                   alog_r_ref, alog_c_ref, d_ref, nw_ref, tri_ref, trit_ref, o_ref.at[rows, :], s_ref)


def _ssd_chunk(xbc, z, dt_raw, dtt_raw, dtb_r_ref, dtb_c_ref, alog_r_ref, alog_c_ref, d_ref, nw_ref, tri_ref,
               trit_ref, o_ref, s_ref):
    q = SSD_CHUNK
    hpg = SSM_HEADS // SSM_GROUPS
    gw = hpg * SSM_HEAD_DIM
    xs = xbc[:, :SSM_INNER]
    bm = xbc[:, SSM_INNER:SSM_INNER + SSM_GROUPS * SSM_STATE]
    cm = xbc[:, SSM_INNER + SSM_GROUPS * SSM_STATE:]

    dt = _softplus(dt_raw + dtb_r_ref[...])
    acs = _mm_sel_lhs(tri_ref[...], dt * -jnp.exp(alog_r_ref[...]))
    dtt = _softplus(dtt_raw + dtb_c_ref[...])
    acst = _mm_sel_rhs(dtt * -jnp.exp(alog_c_ref[...]), trit_ref[...])
    dt_full = _spread_heads(dt, 0, SSM_HEADS, SSM_HEAD_DIM)
    acs_full = _spread_heads(acs, 0, SSM_HEADS, SSM_HEAD_DIM)
    acs_col = _spread_heads(acs, 0, SSM_HEADS, q)

    xdt = xs * dt_full
    acs_last = acs_full[q - 1:q, :]
    xw = xdt * jnp.exp(acs_last - acs_full)
    chunk_decay = jnp.exp(acs_last)

    row = lax.broadcasted_iota(jnp.int32, (q, q), 0)
    col = lax.broadcasted_iota(jnp.int32, (q, q), 1)
    causal = row >= col
    lane = lax.broadcasted_iota(jnp.int32, (q, 2 * SSM_HEAD_DIM), 1)

    y_diag, y_off = [], []
    for g in range(SSM_GROUPS):
        bm_g = bm[:, g * SSM_STATE:(g + 1) * SSM_STATE]
        cm_g = cm[:, g * SSM_STATE:(g + 1) * SSM_STATE]
        cb_g = _mm_nt(cm_g, bm_g)
        state = s_ref[g]
        y_off.append(_mm(cm_g, state))
        s_ref[g] = state * chunk_decay[:, g * gw:(g + 1) * gw] + _mm(bm_g.T, xw[:, g * gw:(g + 1) * gw])
        for pair in range(hpg // 2):
            h0 = g * hpg + 2 * pair
            xdt_pair = xdt[:, h0 * SSM_HEAD_DIM:(h0 + 2) * SSM_HEAD_DIM]
            weights = []
            for h in (h0, h0 + 1):
                seg = acs_col[:, h * q:(h + 1) * q] - acst[h:h + 1, :]
                weights.append(cb_g * jnp.where(causal, jnp.exp(seg), 0.0))
            both = _mm(jnp.concatenate(weights, axis=0), xdt_pair)
            y_diag.append(jnp.where(lane < SSM_HEAD_DIM, both[:q], both[q:]))
    y = (jnp.concatenate(y_diag, axis=1) + jnp.concatenate(y_off, axis=1) * jnp.exp(acs_full)
         + xs * d_ref[...])
    y = y * _silu(z)
    halves = []
    for g in range(SSM_GROUPS):
        yg = y[:, g * gw:(g + 1) * gw]
        halves.append(yg * lax.rsqrt(jnp.mean(yg * yg, axis=-1, keepdims=True) + EPS))
    o_ref[...] = jnp.concatenate(halves, axis=1) * nw_ref[...]


def _pad_lanes(v, fill=0.0):
    return jnp.pad(v.astype(F32), (0, LANES - v.shape[0]), constant_values=fill).reshape(1, LANES)


def _pad_col(v, rows=16):
    return jnp.pad(v.astype(F32), (0, rows - v.shape[0])).reshape(rows, 1)


def ssd_mixer(xbc3, y3, small3, smallt, dt_bias, a_log, d_skip, norm_w):
    bsz, length, _ = y3.shape
    q = SSD_CHUNK
    tri = jnp.asarray(np.tril(np.ones((q, q), np.float32)), BF16)
    trit = jnp.asarray(np.triu(np.ones((q, q), np.float32)), BF16)
    d_full = jnp.repeat(d_skip.astype(F32), SSM_HEAD_DIM).reshape(1, SSM_INNER)
    const = lambda a: pl.BlockSpec(a.shape, lambda b, c: (0,) * a.ndim)
    args = [_pad_lanes(dt_bias), _pad_col(dt_bias), _pad_lanes(a_log),
            _pad_col(a_log), d_full, norm_w.reshape(1, -1), tri, trit]
    rows = min(SSD_STEP_ROWS, length)
    return pl.pallas_call(
        _ssd_kernel,
        grid=(bsz, length // rows),
        in_specs=[
            pl.BlockSpec((None, rows, SSM_XBC), lambda b, c: (b, c, 0)),
            pl.BlockSpec((None, rows, SSM_INNER), lambda b, c: (b, c, 0)),
            pl.BlockSpec((None, rows, LANES), lambda b, c: (b, c, 0)),
            pl.BlockSpec((None, 16, rows), lambda b, c: (b, 0, c)),
        ] + [const(a) for a in args],
        out_specs=pl.BlockSpec((None, rows, SSM_INNER), lambda b, c: (b, c, 0)),
        out_shape=jax.ShapeDtypeStruct((bsz, length, SSM_INNER), F32),
        scratch_shapes=[pltpu.VMEM((SSM_GROUPS, SSM_STATE, SSM_INNER // SSM_GROUPS), F32)],
        compiler_params=_cparams("parallel", "arbitrary"),
        name="ssd_mixer",
    )(xbc3, y3, small3, smallt, *args)


def _unit_lower_inverse(mats, row, col):
    eye = jnp.where(row == col, 1.0, 0.0)
    blk = lambda n: (row >> (n.bit_length() - 1)) == (col >> (n.bit_length() - 1))
    size = row.shape[0]
    p = [jnp.where(blk(16), -a, 0.0) for a in mats]
    t = [eye + x for x in p]
    p = [_mm(x, x) for x in p]
    for _ in range(2):
        both = [_mm(jnp.concatenate([x, y], axis=0), x) for x, y in zip(p, t)]
        p = [b[:size] for b in both]
        t = [y + b[size:] for y, b in zip(t, both)]
    t = [y + _mm(y, x) for y, x in zip(t, p)]
    for n in (16, 32):
        band = blk(2 * n) & jnp.logical_not(blk(n))
        left = [_mm(y, jnp.where(band, a, 0.0)) for y, a in zip(t, mats)]
        t = [y - _mm(x, y) for y, x in zip(t, left)]
    return t


def _gdn_kernel(qkv_ref, z_ref, ab_ref, abt_ref, dtb_r_ref, dtb_c_ref, alog_r_ref,
                alog_c_ref, nw_ref, tri_ref, trit_ref, o_ref, s_ref):
    n = GDN_TILE

    @pl.when(pl.program_id(1) == 0)
    def _():
        s_ref[...] = jnp.zeros_like(s_ref)

    for sub in range(qkv_ref.shape[0] // n):
        rows = slice(sub * n, (sub + 1) * n)
        _gdn_tile(_silu(qkv_ref[rows, :]), z_ref[rows, :], ab_ref[rows, :], abt_ref[:, rows], dtb_r_ref, dtb_c_ref,
                  alog_r_ref, alog_c_ref, nw_ref, tri_ref, trit_ref, o_ref.at[rows, :], s_ref)


def _gdn_tile(qkv, z, ab, abt, dtb_r_ref, dtb_c_ref, alog_r_ref, alog_c_ref, nw_ref, tri_ref, trit_ref, o_ref, s_ref):
    n = GDN_TILE
    c = GDN_CHUNK
    d = GDN_D
    hd = GDN_HEADS * d
    g = -jnp.exp(alog_r_ref[...]) * _softplus(ab + dtb_r_ref[...])
    gc_full = _spread_heads(_mm_sel_lhs(tri_ref[...], g), 0, GDN_HEADS, d)
    beta_full = _spread_heads(jax.nn.sigmoid(ab), GDN_HEADS, GDN_HEADS, d)
    gt = -jnp.exp(alog_c_ref[...]) * _softplus(abt + dtb_c_ref[...])
    gct = _mm_sel_rhs(gt, trit_ref[...])

    row = lax.broadcasted_iota(jnp.int32, (n, n), 0)
    col = lax.broadcasted_iota(jnp.int32, (n, n), 1)
    same = (row >> (c.bit_length() - 1)) == (col >> (c.bit_length() - 1))
    incl = same & (row >= col)
    strict = same & (row > col)
    zeros_half = jnp.zeros((c, d), F32)

    heads = range(GDN_HEADS)
    sl = [slice(h * d, (h + 1) * d) for h in heads]
    l2n = lambda x: x * lax.rsqrt(jnp.sum(x * x, axis=-1, keepdims=True) + EPS)
    qn = [l2n(qkv[:, sl[h]]) * (d ** -0.5) for h in heads]
    kn = [l2n(qkv[:, hd + h * d:hd + (h + 1) * d]) for h in heads]
    vh = [qkv[:, 2 * hd + h * d:2 * hd + (h + 1) * d] for h in heads]
    gcol = [gc_full[:, sl[h]] for h in heads]
    beta = [beta_full[:, sl[h]] for h in heads]
    edec = [jnp.exp(gcol[h] - gct[h:h + 1, :]) for h in heads]
    egc = [jnp.exp(x) for x in gcol]
    kb = [kn[h] * beta[h] for h in heads]
    on_k = [_mm_nt(jnp.concatenate([kb[h], qn[h]], axis=0), kn[h]) for h in heads]
    lower = [jnp.where(strict, on_k[h][:n] * edec[h], 0.0) for h in heads]
    aqk = [jnp.where(incl, on_k[h][n:] * edec[h], 0.0) for h in heads]
    tinv = _unit_lower_inverse(lower, row, col)
    sol = [_mm(tinv[h], jnp.concatenate([vh[h] * beta[h], kb[h] * egc[h]], axis=1)) for h in heads]
    qd = [qn[h] * egc[h] for h in heads]
    glast = [(gcol[h][c - 1:c, :], gcol[h][n - 1:n, :]) for h in heads]
    kdt = [(kn[h] * jnp.exp(jnp.concatenate([jnp.broadcast_to(glast[h][0], (c, d)),
                                             jnp.broadcast_to(glast[h][1], (c, d))], axis=0) - gcol[h])).T
           for h in heads]
    s0 = [s_ref[h] for h in heads]
    on_s0 = [_mm(jnp.concatenate([sol[h][:c, d:], qd[h][:c]], axis=0), s0[h]) for h in heads]
    v0 = [sol[h][:c, :d] - on_s0[h][:c] for h in heads]
    s1 = [s0[h] * jnp.exp(glast[h][0]) + _mm(kdt[h], jnp.concatenate([v0[h], zeros_half], axis=0)) for h in heads]
    on_s1 = [_mm(jnp.concatenate([sol[h][c:, d:], qd[h][c:]], axis=0), s1[h]) for h in heads]
    v1 = [sol[h][c:, :d] - on_s1[h][:c] for h in heads]
    for h in heads:
        s_ref[h] = s1[h] * jnp.exp(glast[h][1]) + _mm(kdt[h], jnp.concatenate([zeros_half, v1[h]], axis=0))
    outs = []
    for h in heads:
        o = (jnp.concatenate([on_s0[h][c:], on_s1[h][c:]], axis=0)
             + _mm(aqk[h], jnp.concatenate([v0[h], v1[h]], axis=0)))
        o = o * lax.rsqrt(jnp.mean(o * o, axis=-1, keepdims=True) + EPS) * nw_ref[...]
        outs.append(o * _silu(z[:, sl[h]]))
    o_ref[...] = jnp.concatenate(outs, axis=1)


def gated_deltanet_mixer(qkv3, y3, small3, smallt, dt_bias, a_log, norm_w):
    bsz, length, _ = y3.shape
    n = GDN_TILE
    hd = GDN_HEADS * GDN_D
    idx = np.arange(n)
    same = (idx[:, None] // GDN_CHUNK) == (idx[None, :] // GDN_CHUNK)
    tri = jnp.asarray(same & (idx[:, None] >= idx[None, :]), BF16)
    trit = jnp.asarray(same & (idx[:, None] <= idx[None, :]), BF16)
    const = lambda a: pl.BlockSpec(a.shape, lambda b, c: (0,) * a.ndim)
    args = [_pad_lanes(dt_bias), _pad_col(dt_bias), _pad_lanes(a_log), _pad_col(a_log),
            norm_w.reshape(1, -1), tri, trit]
    rows = min(GDN_STEP_ROWS, length)
    return pl.pallas_call(
        _gdn_kernel,
        grid=(bsz, length // rows),
        in_specs=[
            pl.BlockSpec((None, rows, 3 * hd), lambda b, c: (b, c, 0)),
            pl.BlockSpec((None, rows, hd), lambda b, c: (b, c, 0)),
            pl.BlockSpec((None, rows, LANES), lambda b, c: (b, c, 0)),
            pl.BlockSpec((None, 16, rows), lambda b, c: (b, 0, c)),
        ] + [const(a) for a in args],
        out_specs=pl.BlockSpec((None, rows, hd), lambda b, c: (b, c, 0)),
        out_shape=jax.ShapeDtypeStruct((bsz, length, hd), F32),
        scratch_shapes=[pltpu.VMEM((GDN_HEADS, GDN_D, GDN_D), F32)],
        compiler_params=_cparams("parallel", "arbitrary"),
        name="gated_deltanet",
    )(qkv3, y3, small3, smallt, *args)


def _sb_kernel(q_ref, k_ref, v_ref, upper_ref, o_ref):
    blk = SB_BLOCK
    pair_w = 2 * SB_HEAD_DIM
    n_pairs = SB_STEP_HEADS // 2
    i = pl.program_id(2)
    q = q_ref[...] * (SB_HEAD_DIM ** -0.5)
    lane = lax.broadcasted_iota(jnp.int32, (blk, pair_w), 1)
    first_head = lane < SB_HEAD_DIM
    qs = []
    for p in range(n_pairs):
        q2 = q[:, p * pair_w:(p + 1) * pair_w]
        qs += [jnp.where(first_head, q2, 0.0).astype(BF16), jnp.where(first_head, 0.0, q2).astype(BF16)]
    row = lax.broadcasted_iota(jnp.int32, (blk, blk), 0)
    col = lax.broadcasted_iota(jnp.int32, (blk, blk), 1)
    earlier = col < row
    upper = upper_ref[...]
    heads = range(SB_STEP_HEADS)

    def local_part(kb, diagonal, exists=None):
        start = pl.multiple_of(kb * blk, blk)
        k = k_ref[pl.ds(start, blk), :].astype(BF16)
        v = v_ref[pl.ds(start, blk), :].astype(BF16)
        kp = [k[:, p * pair_w:(p + 1) * pair_w] for p in range(n_pairs)]
        vp = [v[:, p * pair_w:(p + 1) * pair_w] for p in range(n_pairs)]
        logits = [lax.dot_general(qs[h], kp[h // 2], (((1,), (1,)), ((), ())), preferred_element_type=F32)
                  for h in heads]
        keep = earlier if diagonal else None
        if exists is not None:
            keep = exists if keep is None else keep & exists
        log_keep = [-_softplus(x) for x in logits]
        if keep is not None:
            log_keep = [jnp.where(keep, x, 0.0) for x in log_keep]
        inside = [_mm_sel_rhs(x, upper, 2) for x in log_keep]
        totals = [jnp.sum(x, axis=-1, keepdims=True) for x in log_keep]
        return logits, log_keep, inside, totals, vp, keep

    def carried_part(local, accs, sticks):
        logits, log_keep, inside, totals, vp, keep = local
        w = [jnp.exp(logits[h] + log_keep[h] + inside[h] + sticks[h]) for h in heads]
        if keep is not None:
            w = [jnp.where(keep, x, 0.0) for x in w]
        pv = [jnp.dot(w[h].astype(BF16), vp[h // 2], preferred_element_type=F32) for h in heads]
        accs = tuple(accs[p] + jnp.where(first_head, pv[2 * p], pv[2 * p + 1]) for p in range(n_pairs))
        sticks = tuple(sticks[h] + totals[h] for h in heads)
        return accs, sticks

    accs = tuple(jnp.zeros((blk, pair_w), F32) for _ in range(n_pairs))
    sticks = tuple(jnp.zeros((blk, 1), F32) for _ in heads)
    eager = [local_part(i, True)]
    for back in range(1, SB_EAGER_BLOCKS + 1):
        eager.append(local_part(jnp.maximum(i - back, 0), False, exists=(row >= 0) & (i - back >= 0)))
    for local in eager:
        accs, sticks = carried_part(local, accs, sticks)

    def alive(state):
        kb, _, sticks = state
        longest = sticks[0]
        for s in sticks[1:]:
            longest = jnp.maximum(longest, s)
        return (kb >= 0) & (jnp.max(longest) > SB_LOG_ZERO)

    def body(state):
        kb, accs, sticks = state
        accs, sticks = carried_part(local_part(kb, False), accs, sticks)
        return kb - 1, accs, sticks

    _, accs, _ = lax.while_loop(alive, body, (i - 1 - SB_EAGER_BLOCKS, accs, sticks))
    o_ref[...] = jnp.concatenate(accs, axis=1)


def stick_breaking_mixer(y3, col0):
    bsz, length, _ = y3.shape
    blk = SB_BLOCK
    step_w = SB_STEP_HEADS * SB_HEAD_DIM
    steps = SB_DIM // step_w
    q0 = col0 // step_w
    idx = np.arange(blk)
    upper = jnp.asarray(idx[:, None] > idx[None, :], BF16)
    resident = lambda off: pl.BlockSpec((None, length, step_w), lambda b, p, i: (b, 0, q0 + off + p),
                                        pipeline_mode=pl.Buffered(1))
    return pl.pallas_call(
        _sb_kernel,
        grid=(bsz, steps, length // blk),
        in_specs=[
            pl.BlockSpec((None, blk, step_w), lambda b, p, i: (b, i, q0 + p)),
            resident(steps),
            resident(2 * steps),
            pl.BlockSpec((blk, blk), lambda b, p, i: (0, 0)),
        ],
        out_specs=pl.BlockSpec((None, blk, step_w), lambda b, p, i: (b, i, p)),
        out_shape=jax.ShapeDtypeStruct((bsz, length, SB_DIM), F32),
        compiler_params=_cparams("parallel", "parallel", "arbitrary"),
        name="stick_breaking",
    )(y3, y3, y3, upper)


def _mixer_out(a_ref, b_ref, h_ref, wa_ref, wb_ref, rows):
    return h_ref[rows, :] + (jnp.dot(a_ref[rows, :].astype(BF16), wa_ref[...], preferred_element_type=F32)
                             + jnp.dot(b_ref[rows, :].astype(BF16), wb_ref[...], preferred_element_type=F32))


def _cross_attention(hs, g_ref, wq_ref, kt_ref, v_ref, wo_ref):
    us = [_rms(h, g_ref[...]).astype(BF16) for h in hs]
    qs = [jnp.dot(u, wq_ref[...], preferred_element_type=F32) for u in us]
    heads = [[] for _ in hs]
    for hd in range(XA_HEADS):
        sl = slice(hd * XA_HEAD_DIM, (hd + 1) * XA_HEAD_DIM)
        ss = [jnp.dot(q[:, sl].astype(BF16), kt_ref[sl, :], preferred_element_type=F32) * (XA_HEAD_DIM ** -0.5)
              for q in qs]
        ps = [jnp.exp(s - jnp.max(s, axis=-1, keepdims=True)) for s in ss]
        ps = [p / jnp.sum(p, axis=-1, keepdims=True) for p in ps]
        for k, p in enumerate(ps):
            heads[k].append(jnp.dot(p.astype(BF16), v_ref[:, sl], preferred_element_type=F32))
    os_ = [jnp.concatenate(hk, axis=1).astype(BF16) for hk in heads]
    return [h + jnp.dot(o, wo_ref[...], preferred_element_type=F32) for h, o in zip(hs, os_)]


def _route(xn, whi_ref, wlo_ref, b_ref, before_ref, run_ref):
    x_hi = xn.astype(BF16)
    x_lo = (xn - x_hi.astype(F32)).astype(BF16)
    logits = (jnp.dot(x_hi, whi_ref[...], preferred_element_type=F32)
              + jnp.dot(x_lo, whi_ref[...], preferred_element_type=F32)
              + jnp.dot(x_hi, wlo_ref[...], preferred_element_type=F32) + b_ref[...])
    lane = lax.broadcasted_iota(jnp.int32, logits.shape, 1).astype(F32)
    neg = -1e30
    none = float(LANES)

    def top(vals):
        best = jnp.max(vals, axis=-1, keepdims=True)
        where = jnp.min(jnp.where(vals == best, lane, none), axis=-1, keepdims=True)
        return best, where

    gl = jnp.where(lane < MOE_GROUPS, logits, neg)
    gbest, gsel = top(gl)
    gprob = 1.0 / jnp.sum(jnp.exp(gl - gbest), axis=-1, keepdims=True)
    lo = MOE_GROUPS + gsel * MOE_PER_GROUP
    el = jnp.where((lane >= lo) & (lane < lo + MOE_PER_GROUP), logits, neg)
    m1, i1 = top(el)
    m2, i2 = top(jnp.where(lane == i1, neg, el))
    e = jnp.exp(m2 - m1)
    gate1 = gprob / (1.0 + e)
    gate2 = gprob * e / (1.0 + e)

    hot1 = lane == i1
    hot2 = lane == i2
    one1 = jnp.where(hot1, 1.0, 0.0)
    one2 = jnp.where(hot2, 1.0, 0.0)
    before = before_ref[...]
    prefix1 = jnp.dot(before, one1.astype(BF16), preferred_element_type=F32)
    prefix2 = jnp.dot(before, one2.astype(BF16), preferred_element_type=F32)
    total1 = jnp.sum(one1, axis=0, keepdims=True)
    running = run_ref[...]
    rank1 = jnp.sum(jnp.where(hot1, prefix1 + running, 0.0), axis=-1, keepdims=True)
    rank2 = jnp.sum(jnp.where(hot2, prefix2 + (running + total1), 0.0), axis=-1, keepdims=True)
    running = running + total1 + jnp.sum(one2, axis=0, keepdims=True)
    run_ref[...] = running

    fields = (i1 - MOE_GROUPS, i2 - MOE_GROUPS, gate1, gate2, rank1, rank2)
    out = jnp.zeros_like(logits)
    for k, val in enumerate(fields):
        out = jnp.where(lane == k, val, out)
    return out


def _post_mixer_kernel(a_ref, b_ref, h_ref, wa_ref, wb_ref, gxa_ref, wq_ref, kt_ref, v_ref, wo_ref,
                       gffn_ref, whi_ref, wlo_ref, bias_ref, before_ref,
                       h_out_ref, xn_ref, r_ref, cnt_ref, run_ref):
    @pl.when(pl.program_id(0) == 0)
    def _():
        run_ref[...] = jnp.zeros_like(run_ref)

    h = _mixer_out(a_ref, b_ref, h_ref, wa_ref, wb_ref, slice(None))
    h, = _cross_attention([h], gxa_ref, wq_ref, kt_ref, v_ref, wo_ref)
    h_out_ref[...] = h
    xn = _rms(h, gffn_ref[...])
    xn_ref[...] = _pack_halves(xn)
    group = before_ref.shape[0]
    for start in range(0, h_ref.shape[0], group):
        rows = slice(start, start + group)
        r_ref[rows, :] = _route(xn[rows, :], whi_ref, wlo_ref, bias_ref, before_ref, run_ref)
    cnt_ref[...] = run_ref[...]


def post_mixer(ya, yb, h, wa, wb, g_xa, wq, kt, v, wo, g_ffn, w_hi, w_lo, bias, tm=1024):
    m, d = h.shape
    tiles_per_batch = m // kt.shape[0] // tm
    idx = np.arange(min(POST_GROUP_ROWS, tm))
    before = jnp.asarray(idx[:, None] > idx[None, :], BF16)
    rows = lambda w: pl.BlockSpec((tm, w), lambda i: (i, 0))
    const = lambda a: pl.BlockSpec(a.shape, lambda i: (0,) * a.ndim, pipeline_mode=pl.Buffered(1))
    per_batch = lambda a: pl.BlockSpec((None,) + a.shape[1:], lambda i: (i // tiles_per_batch, 0, 0))
    g_xa, g_ffn = g_xa.reshape(1, d), g_ffn.reshape(1, d)
    return pl.pallas_call(
        _post_mixer_kernel,
        grid=(m // tm,),
        in_specs=[rows(ya.shape[1]), rows(yb.shape[1]), rows(d), const(wa), const(wb), const(g_xa), const(wq),
                  per_batch(kt), per_batch(v), const(wo), const(g_ffn), const(w_hi), const(w_lo), const(bias),
                  const(before)],
        out_specs=[rows(d), rows(d // 2), rows(LANES), pl.BlockSpec((1, LANES), lambda i: (0, 0))],
        out_shape=[jax.ShapeDtypeStruct((m, d), F32), jax.ShapeDtypeStruct((m, d // 2), jnp.int32),
                   jax.ShapeDtypeStruct((m, LANES), F32), jax.ShapeDtypeStruct((1, LANES), F32)],
        scratch_shapes=[pltpu.VMEM((1, LANES), F32)],
        compiler_params=_cparams("arbitrary"),
        name="post_mixer",
    )(ya, yb, h, wa, wb, g_xa, wq, kt, v, wo, g_ffn, w_hi, w_lo, bias, before)


def _expert_kernel(table_ref, x_ref, wg_hbm, wu_hbm, wd_hbm, o_ref,
                   wg32_ref, wu32_ref, wd32_ref, wgb_ref, wub_ref, wdb_ref, sem_ref, *, layer):
    i = pl.program_id(0)
    beid_ref, valid_ref, first_ref, slot_ref, next_ref = (table_ref.at[k] for k in range(5))
    valid = valid_ref[i]

    def weight_copies(expert, slot):
        return (pltpu.make_async_copy(wg_hbm.at[layer, expert], wg32_ref.at[slot], sem_ref.at[slot, 0]),
                pltpu.make_async_copy(wu_hbm.at[layer, expert], wu32_ref.at[slot], sem_ref.at[slot, 1]),
                pltpu.make_async_copy(wd_hbm.at[layer, expert], wd32_ref.at[slot], sem_ref.at[slot, 2]))

    @pl.when(i == 0)
    def _():
        for copy in weight_copies(beid_ref[0], 0):
            copy.start()

    @pl.when(first_ref[i] == 1)
    def _():
        slot = slot_ref[i]
        for copy in weight_copies(beid_ref[i], slot):
            copy.wait()
        wgb_ref[...] = wg32_ref[slot].astype(BF16)
        wub_ref[...] = wu32_ref[slot].astype(BF16)
        wdb_ref[...] = wd32_ref[slot].astype(BF16)

        @pl.when(next_ref[i] >= 0)
        def _():
            for copy in weight_copies(next_ref[i], 1 - slot):
                copy.start()

    half = MOE_ROWS // 2

    def ffn(n_halves):
        row = lax.broadcasted_iota(jnp.int32, (half, 2 * x_ref.shape[1]), 0)
        xs = [jnp.where(row + k * half < valid, _unpack_halves(x_ref[k * half:(k + 1) * half, :]), 0.0).astype(BF16)
              for k in range(n_halves)]
        gates = [jnp.dot(x, wgb_ref[...], preferred_element_type=F32) for x in xs]
        ups = [jnp.dot(x, wub_ref[...], preferred_element_type=F32) for x in xs]
        acts = [(_silu(g) * u).astype(BF16) for g, u in zip(gates, ups)]
        for k, act in enumerate(acts):
            o_ref[k * half:(k + 1) * half, :] = _pack_halves(jnp.dot(act, wdb_ref[...], preferred_element_type=F32))

    @pl.when(valid > half)
    def _():
        ffn(2)

    @pl.when((valid > 0) & (valid <= half))
    def _():
        ffn(1)
        o_ref[half:, :] = jnp.zeros((half, o_ref.shape[1]), o_ref.dtype)

    @pl.when(valid == 0)
    def _():
        o_ref[...] = jnp.zeros_like(o_ref)


def moe_experts(blocks, xs, w_gate, w_up, w_down, layer):
    n_slots, packed = xs.shape
    d = 2 * packed
    rows = MOE_ROWS
    ff = w_gate.shape[3]
    grid_spec = pltpu.PrefetchScalarGridSpec(
        num_scalar_prefetch=1,
        grid=(n_slots // rows,),
        in_specs=[
            pl.BlockSpec((rows, packed), lambda i, *_: (i, 0)),
            pl.BlockSpec(memory_space=pl.ANY),
            pl.BlockSpec(memory_space=pl.ANY),
            pl.BlockSpec(memory_space=pl.ANY),
        ],
        out_specs=pl.BlockSpec((rows, packed), lambda i, *_: (i, 0)),
        scratch_shapes=[pltpu.VMEM((2, d, ff), F32), pltpu.VMEM((2, d, ff), F32), pltpu.VMEM((2, ff, d), F32),
                        pltpu.VMEM((d, ff), BF16), pltpu.VMEM((d, ff), BF16), pltpu.VMEM((ff, d), BF16),
                        pltpu.SemaphoreType.DMA((2, 3))],
    )
    return pl.pallas_call(
        functools.partial(_expert_kernel, layer=layer),
        grid_spec=grid_spec,
        out_shape=jax.ShapeDtypeStruct((n_slots, packed), jnp.int32),
        compiler_params=_cparams("arbitrary"),
        name="moe_experts",
    )(blocks, xs, w_gate, w_up, w_down)


def _sc_mesh():
    return plsc.VectorSubcoreMesh(core_axis_name="c", subcore_axis_name="s",
                                  num_cores=SC_CORES, num_subcores=SC_SUBCORES)


def _sc_worker():
    return lax.axis_index("s") * SC_CORES + lax.axis_index("c")


def sc_scatter_rows(x, dest, n_slots):
    n_tok, d = x.shape
    per_worker = n_tok // SC_WORKERS
    n_chunks = per_worker // SC_CHUNK
    by_worker = dest.reshape(dest.shape[0] * SC_WORKERS, n_chunks, SC_CHUNK)

    @functools.partial(
        pl.kernel, mesh=_sc_mesh(), out_type=jax.ShapeDtypeStruct((n_slots, d), x.dtype),
        scratch_types=[pltpu.VMEM((n_chunks, SC_CHUNK), jnp.int32), pltpu.VMEM((n_chunks, SC_CHUNK), jnp.int32),
                       pltpu.VMEM((SC_CHUNK, d), x.dtype)],
        name="moe_scatter_rows")
    def scatter(x_hbm, dest_hbm, out_hbm, i0_v, i1_v, rows_v):
        wid = _sc_worker()
        pltpu.sync_copy(dest_hbm.at[wid], i0_v)
        pltpu.sync_copy(dest_hbm.at[SC_WORKERS + wid], i1_v)

        @pl.loop(0, n_chunks)
        def _(j):
            start = pl.multiple_of(wid * per_worker + j * SC_CHUNK, SC_CHUNK)
            pltpu.sync_copy(x_hbm.at[pl.ds(start, SC_CHUNK)], rows_v)
            pltpu.sync_copy(rows_v, out_hbm.at[i0_v.at[j]])
            pltpu.sync_copy(rows_v, out_hbm.at[i1_v.at[j]])

    return scatter(x, by_worker)


def sc_gather_rows(table, idx, n_out):
    d = table.shape[1]
    per_worker = n_out // SC_WORKERS
    n_chunks = per_worker // SC_CHUNK

    @functools.partial(
        pl.kernel, mesh=_sc_mesh(), out_type=jax.ShapeDtypeStruct((n_out, d), table.dtype),
        scratch_types=[pltpu.VMEM((n_chunks, SC_CHUNK), jnp.int32), pltpu.VMEM((SC_CHUNK, d), table.dtype)],
        name="moe_gather_rows")
    def gather(table_hbm, idx_hbm, out_hbm, idx_v, rows_v):
        wid = _sc_worker()
        pltpu.sync_copy(idx_hbm.at[wid], idx_v)

        @pl.loop(0, n_chunks)
        def _(j):
            start = pl.multiple_of(wid * per_worker + j * SC_CHUNK, SC_CHUNK)
            pltpu.sync_copy(table_hbm.at[idx_v.at[j]], rows_v)
            pltpu.sync_copy(rows_v, out_hbm.at[pl.ds(start, SC_CHUNK)])

    return gather(table, idx.reshape(-1, n_chunks, SC_CHUNK))


def _combine_kernel(h_ref, y0_ref, y1_ref, r_ref, g_ref, o_ref, *, final_norm):
    route = r_ref[...]
    h = h_ref[...] + (route[:, 2:3] * _unpack_halves(y0_ref[...]) + route[:, 3:4] * _unpack_halves(y1_ref[...]))
    o_ref[...] = _rms(h, g_ref[...]) if final_norm else h


def moe_combine(h, y01, route, g, final_norm, tm=1024):
    m, d = h.shape
    tm = min(tm, m)
    rows = lambda w: pl.BlockSpec((tm, w), lambda i: (i, 0))
    return pl.pallas_call(
        functools.partial(_combine_kernel, final_norm=final_norm),
        grid=(m // tm,),
        in_specs=[rows(d), rows(d // 2), pl.BlockSpec((tm, d // 2), lambda i: (i + m // tm, 0)), rows(LANES),
                  pl.BlockSpec((1, d), lambda i: (0, 0))],
        out_specs=rows(d),
        out_shape=jax.ShapeDtypeStruct((m, d), F32),
        compiler_params=_cparams("parallel"),
        name="moe_combine",
    )(h, y01, y01, route, g.reshape(1, d))


def _pad_cols(w):
    return jnp.pad(w, ((0, 0), (0, LANES - w.shape[1])))


def _plan_kernel(route_ref, cnt_ref, incl_ref, dest_ref, table_ref):
    f32_sum = lambda x, axis: jnp.sum(x, axis=axis, keepdims=True)
    lane = lax.broadcasted_iota(jnp.int32, (LANES, LANES), 1)
    sub = lax.broadcasted_iota(jnp.int32, (LANES, LANES), 0)
    incl = incl_ref[...]
    is_expert = (lane >= MOE_GROUPS) & (lane < MOE_GROUPS + MOE_EXPERTS)
    shift = MOE_ROWS.bit_length() - 1
    counts = jnp.broadcast_to(cnt_ref[...], (LANES, LANES)).astype(jnp.int32)
    padded = jnp.where(is_expert, ((counts + (MOE_ROWS - 1)) >> shift) << shift, 0)
    pad_end = _mm_sel_rhs(padded.astype(F32), incl)
    pad_start = pad_end - padded.astype(F32)

    route = route_ref[...]
    lane_t = lax.broadcasted_iota(jnp.int32, route.shape, 1)
    lane_f = lane_t.astype(F32)
    start_row = pad_start[0:1, :]
    slots = [f32_sum(jnp.where(lane_f == route[:, k:k + 1] + MOE_GROUPS, start_row, 0.0), 1) + route[:, 4 + k:5 + k]
             for k in range(2)]
    both = jnp.where(lane_t == 0, slots[0], jnp.where(lane_t == 1, slots[1], 0.0))
    dest_ref[...] = both.T[0:8, :].astype(jnp.int32)

    on_sub = lambda rows_equal: rows_equal.T
    expert_sub = (sub >= MOE_GROUPS) & (sub < MOE_GROUPS + MOE_EXPERTS)
    block_start = (lane * MOE_ROWS).astype(F32)
    eid = f32_sum(jnp.where(expert_sub & (on_sub(pad_end) <= block_start), 1.0, 0.0), 0)
    eid = jnp.minimum(eid, float(MOE_EXPERTS - 1))
    filled = on_sub(pad_start + counts.astype(F32))
    own = (sub - MOE_GROUPS).astype(F32) == eid
    valid = jnp.clip(f32_sum(jnp.where(own, filled, 0.0), 0) - block_start[0:1, :], 0.0, float(MOE_ROWS))
    eid_rows = jnp.broadcast_to(eid, (LANES, LANES))
    changed = (lane == 0) | (eid_rows != pltpu.roll(eid_rows, 1, axis=1))
    first = jnp.where((jnp.broadcast_to(valid, (LANES, LANES)) > 0) & changed, 1.0, 0.0)
    ordinal = _mm_sel_rhs(first, incl) - 1.0
    slot = ordinal - 2.0 * jnp.floor(ordinal * 0.5)
    later = (on_sub(first) > 0) & (sub > lane)
    nearest = jnp.min(jnp.where(later, sub, LANES), axis=0, keepdims=True)
    next_eid = f32_sum(jnp.where(sub == nearest, on_sub(eid_rows), 0.0), 0)
    next_eid = jnp.where(nearest < LANES, next_eid, -1.0)
    row8 = lax.broadcasted_iota(jnp.int32, (8, LANES), 0)
    table = jnp.zeros((8, LANES), F32)
    for k, val in enumerate((eid, valid, first[0:1, :], slot[0:1, :], next_eid)):
        table = jnp.where(row8 == k, val, table)
    table_ref[...] = table.astype(jnp.int32)


def moe_plan(route, counts, tm=2048):
    n_tok = route.shape[0]
    tm = min(tm, n_tok)
    idx = np.arange(LANES)
    incl = jnp.asarray(idx[:, None] <= idx[None, :], BF16)
    return pl.pallas_call(
        _plan_kernel,
        grid=(n_tok // tm,),
        in_specs=[pl.BlockSpec((tm, LANES), lambda i: (i, 0)), pl.BlockSpec((1, LANES), lambda i: (0, 0)),
                  pl.BlockSpec((LANES, LANES), lambda i: (0, 0))],
        out_specs=[pl.BlockSpec((8, tm), lambda i: (0, i)), pl.BlockSpec((8, LANES), lambda i: (0, 0))],
        out_shape=[jax.ShapeDtypeStruct((8, n_tok), jnp.int32), jax.ShapeDtypeStruct((8, LANES), jnp.int32)],
        compiler_params=_cparams("arbitrary"),
        name="moe_plan",
    )(route, counts, incl)


def _router_weights(w_group, b_group, w_expert, b_expert):
    w_r = _pad_cols(jnp.concatenate([w_group, w_expert], axis=1))
    w_hi = w_r.astype(BF16)
    w_lo = (w_r - w_hi.astype(F32)).astype(BF16)
    return w_hi, w_lo, _pad_lanes(jnp.concatenate([b_group, b_expert]))


def _moe_layer(h, xn, route, counts, w_gate, w_up, w_down, layer, final_g):
    n_tok, d = h.shape
    n_blocks = -(-(2 * n_tok + MOE_EXPERTS * (MOE_ROWS - 1)) // MOE_ROWS)
    dest, blocks = moe_plan(route, counts)
    xs = sc_scatter_rows(xn, dest, n_blocks * MOE_ROWS)
    ys = moe_experts(blocks, xs, w_gate, w_up, w_down, layer)
    y01 = sc_gather_rows(ys, dest, 2 * n_tok)
    g = jnp.ones((d,), F32) if final_g is None else final_g
    return moe_combine(h, y01, route, g, final_g is not None)


def _memory_kv(memn_in, mem_norm, wk, wv):
    bsz, m, d = memn_in.shape
    w = jnp.concatenate([wk, wv], axis=1).astype(BF16)
    kv, _ = rms_matmul(memn_in.reshape(bsz * m, d), mem_norm, w, jnp.zeros((d, LANES), BF16))
    k = kv[:, :d].reshape(bsz, m, d)
    v = kv[:, d:].reshape(bsz, m, d)
    return jnp.swapaxes(k, 1, 2).astype(BF16), v.astype(BF16)


def kernel(x, mem, mem_norm, final_norm, norm_mix, norm_xa, norm_ffn, xa_wq, xa_wk, xa_wv, xa_wo, moe_w_group, moe_b_group, moe_w_expert, moe_b_expert, moe_w_gate, moe_w_up, moe_w_down, ev_w_in, ev_sc_conv, ev_ssm_conv_w, ev_ssm_conv_b, ev_ssm_dt_bias, ev_ssm_a_log, ev_ssm_d, ev_ssm_norm, ev_w_out, od_w_in, od_gdn_conv, od_gdn_dt_bias, od_gdn_a_log, od_gdn_norm, od_w_out):
    bsz, length, d = x.shape
    n_tok = bsz * length
    depth = norm_mix.shape[0]
    h = x.reshape(n_tok, d)
    for layer in range(depth):
        i = layer // 2
        if layer % 2 == 0:
            w = ev_w_in[i]
            z0 = 3 * SC_DIM
            xbc0 = z0 + SSM_INNER
            w_conv = w[:, xbc0:xbc0 + SSM_XBC].astype(BF16)
            w_small = _pad_cols(w[:, xbc0 + SSM_XBC:]).astype(BF16)
            xbc = rms_matmul_conv(h, norm_mix[layer], w_conv, ev_ssm_conv_w[i], ev_ssm_conv_b[i], length)
            z, ya, small = rms_matmul_gated(h, norm_mix[layer], w[:, z0:xbc0].astype(BF16), w[:, :z0].astype(BF16),
                                            w_small, ev_sc_conv[i], length)
            small3 = small.reshape(bsz, length, LANES)
            smallt = jnp.swapaxes(small3[:, :, :16], 1, 2)
            yb = ssd_mixer(xbc.reshape(bsz, length, -1), z.reshape(bsz, length, -1), small3, smallt,
                           ev_ssm_dt_bias[i], ev_ssm_a_log[i], ev_ssm_d[i], ev_ssm_norm[i])
            w_out = ev_w_out[i].astype(BF16)
            split = SC_DIM
        else:
            w = od_w_in[i]
            qkv_w = 3 * GDN_HEADS * GDN_D
            z_end = qkv_w + GDN_HEADS * GDN_D
            w_conv = w[:, :qkv_w].astype(BF16)
            w_main = jnp.concatenate([w[:, qkv_w:z_end], w[:, z_end + 2 * GDN_HEADS:]], axis=1).astype(BF16)
            w_small = _pad_cols(w[:, z_end:z_end + 2 * GDN_HEADS]).astype(BF16)
            qkv = rms_matmul_conv(h, norm_mix[layer], w_conv, od_gdn_conv[i], jnp.zeros((qkv_w,), F32), length)
            y, small = rms_matmul(h, norm_mix[layer], w_main, w_small, tm=1024, tn=w_main.shape[1])
            y3 = y.reshape(bsz, length, -1)
            small3 = small.reshape(bsz, length, LANES)
            smallt = jnp.swapaxes(small3[:, :, :16], 1, 2)
            ya = gated_deltanet_mixer(qkv.reshape(bsz, length, -1), y3, small3, smallt, od_gdn_dt_bias[i],
                                      od_gdn_a_log[i], od_gdn_norm[i])
            yb = stick_breaking_mixer(y3, GDN_HEADS * GDN_D)
            w_out = od_w_out[i].astype(BF16)
            split = GDN_HEADS * GDN_D
        kt, v = _memory_kv(mem, mem_norm, xa_wk[layer], xa_wv[layer])
        w_hi, w_lo, bias = _router_weights(moe_w_group[layer], moe_b_group[layer], moe_w_expert[layer],
                                           moe_b_expert[layer])
        h, xn, route, counts = post_mixer(
            ya.reshape(n_tok, -1), yb.reshape(n_tok, -1), h, w_out[:split], w_out[split:], norm_xa[layer],
            xa_wq[layer].astype(BF16), kt, v, xa_wo[layer].astype(BF16), norm_ffn[layer], w_hi, w_lo, bias)
        h = _moe_layer(h, xn, route, counts, moe_w_gate, moe_w_up, moe_w_down, layer,
                       final_norm if layer == depth - 1 else None)
    return h.reshape(bsz, length, d)
```

```python
import functools

import jax
import jax.numpy as jnp
import numpy as np
from jax import lax
from jax.experimental import pallas as pl
from jax.experimental.pallas import tpu as pltpu
from jax.experimental.pallas import tpu_sc as plsc

F32 = jnp.float32
BF16 = jnp.bfloat16
EPS = 1e-6

SC_DIM = 512
SSM_HEADS = 16
SSM_HEAD_DIM = 64
SSM_INNER = 1024
SSM_GROUPS = 2
SSM_STATE = 128
SSM_XBC = SSM_INNER + 2 * SSM_GROUPS * SSM_STATE
SSD_CHUNK = 128
SSD_STEP_ROWS = 512
GDN_HEADS = 8
GDN_D = 128
GDN_CHUNK = 64
GDN_TILE = 128
GDN_STEP_ROWS = 512
SB_HEAD_DIM = 64
SB_DIM = 512
SB_BLOCK = 128
SB_STEP_HEADS = 8
SB_EAGER_BLOCKS = 2
XA_HEADS = 4
XA_HEAD_DIM = 256
MOE_GROUPS = 4
MOE_PER_GROUP = 8
MOE_EXPERTS = 32
MOE_ROWS = 512
POST_GROUP_ROWS = 512
SC_CORES = 2
SC_SUBCORES = 16
SC_WORKERS = SC_CORES * SC_SUBCORES
SC_CHUNK = 64
HALO = 8
CONV_CHUNK = 512
LANES = 128
SB_LOG_ZERO = -104.0
VMEM_LIMIT = 56 * 1024 * 1024


def _cparams(*sem):
    return pltpu.CompilerParams(dimension_semantics=sem, vmem_limit_bytes=VMEM_LIMIT)


def _mm(a, b):
    return jnp.dot(a.astype(BF16), b.astype(BF16), preferred_element_type=F32)


def _mm_nt(a, b):
    return lax.dot_general(a.astype(BF16), b.astype(BF16), (((1,), (1,)), ((), ())),
                           preferred_element_type=F32)


def _split_bf16(x, n):
    parts, r = [], x
    for _ in range(n):
        p = r.astype(BF16)
        parts.append(p)
        r = r - p.astype(F32)
    return parts


def _mm_sel_rhs(x, sel, n=3):
    return sum(jnp.dot(p, sel, preferred_element_type=F32) for p in _split_bf16(x, n))


def _mm_sel_lhs(sel, x, n=3):
    return sum(jnp.dot(sel, p, preferred_element_type=F32) for p in _split_bf16(x, n))


def _spread_heads(x, first, n_heads, width):
    rows = x.shape[0]
    col = lambda h: jnp.broadcast_to(x[:, first + h:first + h + 1], (rows, LANES))
    if width == LANES:
        return jnp.concatenate([col(h) for h in range(n_heads)], axis=1)
    left = lax.broadcasted_iota(jnp.int32, (rows, LANES), 1) < width
    return jnp.concatenate([jnp.where(left, col(h), col(h + 1)) for h in range(0, n_heads, 2)], axis=1)


def _pack_halves(x):
    n = x.shape[1] // 2
    lo = pltpu.bitcast(x[:, :n].astype(BF16).astype(F32), jnp.int32)
    hi = pltpu.bitcast(x[:, n:].astype(BF16).astype(F32), jnp.int32)
    return lax.shift_right_logical(lo, 16) | (hi & jnp.int32(-65536))


def _unpack_halves(p):
    lo = pltpu.bitcast(lax.shift_left(p, 16), F32)
    hi = pltpu.bitcast(p & jnp.int32(-65536), F32)
    return jnp.concatenate([lo, hi], axis=1)


def _silu(x):
    return x * jax.nn.sigmoid(x)


def _softplus(x):
    return jnp.maximum(x, 0.0) + jnp.log(1.0 + jnp.exp(-jnp.abs(x)))


def _rms(x, g):
    return x * lax.rsqrt(jnp.mean(x * x, axis=-1, keepdims=True) + EPS) * g


def _rms_matmul_kernel(x_ref, g_ref, w_ref, ws_ref, o_ref, os_ref):
    xn = _rms(x_ref[...], g_ref[...]).astype(BF16)
    o_ref[...] = jnp.dot(xn, w_ref[...], preferred_element_type=F32)
    os_ref[...] = jnp.dot(xn, ws_ref[...], preferred_element_type=F32)


def rms_matmul(x, g, w, ws, tm=512, tn=512):
    m, k = x.shape
    n = w.shape[1]
    tm = min(tm, m)
    main, small = pl.pallas_call(
        _rms_matmul_kernel,
        grid=(n // tn, m // tm),
        in_specs=[
            pl.BlockSpec((tm, k), lambda j, i: (i, 0)),
            pl.BlockSpec((1, k), lambda j, i: (0, 0)),
            pl.BlockSpec((k, tn), lambda j, i: (0, j)),
            pl.BlockSpec((k, LANES), lambda j, i: (0, 0)),
        ],
        out_specs=[
            pl.BlockSpec((tm, tn), lambda j, i: (i, j)),
            pl.BlockSpec((None, tm, LANES), lambda j, i: (j, i, 0)),
        ],
        out_shape=[jax.ShapeDtypeStruct((m, n), F32), jax.ShapeDtypeStruct((n // tn, m, LANES), F32)],
        compiler_params=_cparams("parallel", "parallel"),
        name="rms_matmul",
    )(x, g.reshape(1, k), w, ws)
    return main, small[0]


def _causal_conv(ext_ref, w_ref, rows):
    width = w_ref.shape[0]
    ext = ext_ref[...]
    acc = None
    for j in range(width):
        shift = width - 1 - j
        moved = ext if shift == 0 else pltpu.roll(ext, shift, axis=0)
        term = w_ref[j:j + 1, :] * moved[HALO:HALO + rows, :]
        acc = term if acc is None else acc + term
    return acc


def _rms_matmul_conv_kernel(x_ref, g_ref, w_ref, cw_ref, cb_ref, o_ref, *ext_refs, tiles_per_seq):
    tm = x_ref.shape[0]
    starts_sequence = pl.program_id(1) % tiles_per_seq == 0

    @pl.when(starts_sequence)
    def _():
        for ext_ref in ext_refs:
            ext_ref[0:HALO, :] = jnp.zeros((HALO, CONV_CHUNK), F32)

    @pl.when(jnp.logical_not(starts_sequence))
    def _():
        for ext_ref in ext_refs:
            ext_ref[0:HALO, :] = ext_ref[tm:tm + HALO, :]

    xn = _rms(x_ref[...], g_ref[...]).astype(BF16)
    for c, ext_ref in enumerate(ext_refs):
        cols = slice(c * CONV_CHUNK, (c + 1) * CONV_CHUNK)
        ext_ref[HALO:, :] = jnp.dot(xn, w_ref[:, cols], preferred_element_type=F32)
        o_ref[:, cols] = _causal_conv(ext_ref, cw_ref.at[:, cols], tm) + cb_ref[:, cols]


def rms_matmul_conv(x, g, w, conv_w, conv_b, seq_len, tm=1024, tn=1536):
    m, k = x.shape
    n = w.shape[1]
    cols = lambda rows: pl.BlockSpec((rows, tn), lambda j, i: (0, j))
    return pl.pallas_call(
        functools.partial(_rms_matmul_conv_kernel, tiles_per_seq=seq_len // tm),
        grid=(n // tn, m // tm),
        in_specs=[
            pl.BlockSpec((tm, k), lambda j, i: (i, 0)),
            pl.BlockSpec((1, k), lambda j, i: (0, 0)),
            cols(k), cols(conv_w.shape[0]), cols(1),
        ],
        out_specs=pl.BlockSpec((tm, tn), lambda j, i: (i, j)),
        out_shape=jax.ShapeDtypeStruct((m, n), F32),
        scratch_shapes=[pltpu.VMEM((tm + HALO, CONV_CHUNK), F32)] * (tn // CONV_CHUNK),
        compiler_params=_cparams("arbitrary", "arbitrary"),
        name="rms_matmul_conv",
    )(x, g.reshape(1, k), w, conv_w, conv_b.reshape(1, n))


def _rms_matmul_gated_kernel(x_ref, g_ref, wz_ref, wbcx_ref, ws_ref, cw_ref, z_ref, ya_ref, os_ref, ext_ref,
                             *, tiles_per_seq):
    tm = x_ref.shape[0]
    starts_sequence = pl.program_id(0) % tiles_per_seq == 0

    @pl.when(starts_sequence)
    def _():
        ext_ref[0:HALO, :] = jnp.zeros((HALO, SC_DIM), F32)

    @pl.when(jnp.logical_not(starts_sequence))
    def _():
        ext_ref[0:HALO, :] = ext_ref[tm:tm + HALO, :]

    xn = _rms(x_ref[...], g_ref[...]).astype(BF16)
    z_ref[...] = jnp.dot(xn, wz_ref[...], preferred_element_type=F32)
    os_ref[...] = jnp.dot(xn, ws_ref[...], preferred_element_type=F32)
    bcx = jnp.dot(xn, wbcx_ref[...], preferred_element_type=F32)
    ext_ref[HALO:, :] = bcx[:, SC_DIM:2 * SC_DIM] * bcx[:, 2 * SC_DIM:]
    ya_ref[...] = bcx[:, :SC_DIM] * _causal_conv(ext_ref, cw_ref, tm)


def rms_matmul_gated(x, g, w_z, w_bcx, w_small, conv_w, seq_len, tm=1024):
    m, k = x.shape
    const = lambda a: pl.BlockSpec(a.shape, lambda i: (0,) * a.ndim)
    rows = lambda w: pl.BlockSpec((tm, w), lambda i: (i, 0))
    g = g.reshape(1, k)
    return pl.pallas_call(
        functools.partial(_rms_matmul_gated_kernel, tiles_per_seq=seq_len // tm),
        grid=(m // tm,),
        in_specs=[rows(k), const(g), const(w_z), const(w_bcx), const(w_small), const(conv_w)],
        out_specs=[rows(w_z.shape[1]), rows(SC_DIM), rows(LANES)],
        out_shape=[jax.ShapeDtypeStruct((m, w_z.shape[1]), F32), jax.ShapeDtypeStruct((m, SC_DIM), F32),
                   jax.ShapeDtypeStruct((m, LANES), F32)],
        scratch_shapes=[pltpu.VMEM((tm + HALO, SC_DIM), F32)],
        compiler_params=_cparams("arbitrary"),
        name="rms_matmul_gated",
    )(x, g, w_z, w_bcx, w_small, conv_w)


def _ssd_kernel(xbc_ref, z_ref, dt_ref, dtt_ref, dtb_r_ref, dtb_c_ref,
                alog_r_ref, alog_c_ref, d_ref, nw_ref, tri_ref, trit_ref,
                o_ref, s_ref):
    q = SSD_CHUNK

    @pl.when(pl.program_id(1) == 0)
    def _():
        s_ref[...] = jnp.zeros_like(s_ref)

    for sub in range(xbc_ref.shape[0] // q):
        rows = slice(sub * q, (sub + 1) * q)
        _ssd_chunk(_silu(xbc_ref[rows, :]), z_ref[rows, :], dt_ref[rows, :], dtt_ref[:, rows], dtb_r_ref, dtb_c_ref,
                   alog_r_ref, alog_c_ref, d_ref, nw_ref, tri_ref, trit_ref, o_ref.at[rows, :], s_ref)


def _ssd_chunk(xbc, z, dt_raw, dtt_raw, dtb_r_ref, dtb_c_ref, alog_r_ref, alog_c_ref, d_ref, nw_ref, tri_ref,
               trit_ref, o_ref, s_ref):
    q = SSD_CHUNK
    hpg = SSM_HEADS // SSM_GROUPS
    gw = hpg * SSM_HEAD_DIM
    xs = xbc[:, :SSM_INNER]
    bm = xbc[:, SSM_INNER:SSM_INNER + SSM_GROUPS * SSM_STATE]
    cm = xbc[:, SSM_INNER + SSM_GROUPS * SSM_STATE:]

    dt = _softplus(dt_raw + dtb_r_ref[...])
    acs = _mm_sel_lhs(tri_ref[...], dt * -jnp.exp(alog_r_ref[...]))
    dtt = _softplus(dtt_raw + dtb_c_ref[...])
    acst = _mm_sel_rhs(dtt * -jnp.exp(alog_c_ref[...]), trit_ref[...])
    dt_full = _spread_heads(dt, 0, SSM_HEADS, SSM_HEAD_DIM)
    acs_full = _spread_heads(acs, 0, SSM_HEADS, SSM_HEAD_DIM)
    acs_col = _spread_heads(acs, 0, SSM_HEADS, q)

    xdt = xs * dt_full
    acs_last = acs_full[q - 1:q, :]
    xw = xdt * jnp.exp(acs_last - acs_full)
    chunk_decay = jnp.exp(acs_last)

    row = lax.broadcasted_iota(jnp.int32, (q, q), 0)
    col = lax.broadcasted_iota(jnp.int32, (q, q), 1)
    causal = row >= col
    lane = lax.broadcasted_iota(jnp.int32, (q, 2 * SSM_HEAD_DIM), 1)

    y_diag, y_off = [], []
    for g in range(SSM_GROUPS):
        bm_g = bm[:, g * SSM_STATE:(g + 1) * SSM_STATE]
        cm_g = cm[:, g * SSM_STATE:(g + 1) * SSM_STATE]
        cb_g = _mm_nt(cm_g, bm_g)
        state = s_ref[g]
        y_off.append(_mm(cm_g, state))
        s_ref[g] = state * chunk_decay[:, g * gw:(g + 1) * gw] + _mm(bm_g.T, xw[:, g * gw:(g + 1) * gw])
        for pair in range(hpg // 2):
            h0 = g * hpg + 2 * pair
            xdt_pair = xdt[:, h0 * SSM_HEAD_DIM:(h0 + 2) * SSM_HEAD_DIM]
            weights = []
            for h in (h0, h0 + 1):
                seg = acs_col[:, h * q:(h + 1) * q] - acst[h:h + 1, :]
                weights.append(cb_g * jnp.where(causal, jnp.exp(seg), 0.0))
            both = _mm(jnp.concatenate(weights, axis=0), xdt_pair)
            y_diag.append(jnp.where(lane < SSM_HEAD_DIM, both[:q], both[q:]))
    y = (jnp.concatenate(y_diag, axis=1) + jnp.concatenate(y_off, axis=1) * jnp.exp(acs_full)
         + xs * d_ref[...])
    y = y * _silu(z)
    halves = []
    for g in range(SSM_GROUPS):
        yg = y[:, g * gw:(g + 1) * gw]
        halves.append(yg * lax.rsqrt(jnp.mean(yg * yg, axis=-1, keepdims=True) + EPS))
    o_ref[...] = jnp.concatenate(halves, axis=1) * nw_ref[...]


def _pad_lanes(v, fill=0.0):
    return jnp.pad(v.astype(F32), (0, LANES - v.shape[0]), constant_values=fill).reshape(1, LANES)


def _pad_col(v, rows=16):
    return jnp.pad(v.astype(F32), (0, rows - v.shape[0])).reshape(rows, 1)


def ssd_mixer(xbc3, y3, small3, smallt, dt_bias, a_log, d_skip, norm_w):
    bsz, length, _ = y3.shape
    q = SSD_CHUNK
    tri = jnp.asarray(np.tril(np.ones((q, q), np.float32)), BF16)
    trit = jnp.asarray(np.triu(np.ones((q, q), np.float32)), BF16)
    d_full = jnp.repeat(d_skip.astype(F32), SSM_HEAD_DIM).reshape(1, SSM_INNER)
    const = lambda a: pl.BlockSpec(a.shape, lambda b, c: (0,) * a.ndim)
    args = [_pad_lanes(dt_bias), _pad_col(dt_bias), _pad_lanes(a_log),
            _pad_col(a_log), d_full, norm_w.reshape(1, -1), tri, trit]
    rows = min(SSD_STEP_ROWS, length)
    return pl.pallas_call(
        _ssd_kernel,
        grid=(bsz, length // rows),
        in_specs=[
            pl.BlockSpec((None, rows, SSM_XBC), lambda b, c: (b, c, 0)),
            pl.BlockSpec((None, rows, SSM_INNER), lambda b, c: (b, c, 0)),
            pl.BlockSpec((None, rows, LANES), lambda b, c: (b, c, 0)),
            pl.BlockSpec((None, 16, rows), lambda b, c: (b, 0, c)),
        ] + [const(a) for a in args],
        out_specs=pl.BlockSpec((None, rows, SSM_INNER), lambda b, c: (b, c, 0)),
        out_shape=jax.ShapeDtypeStruct((bsz, length, SSM_INNER), F32),
        scratch_shapes=[pltpu.VMEM((SSM_GROUPS, SSM_STATE, SSM_INNER // SSM_GROUPS), F32)],
        compiler_params=_cparams("parallel", "arbitrary"),
        name="ssd_mixer",
    )(xbc3, y3, small3, smallt, *args)


def _unit_lower_inverse(mats, row, col):
    eye = jnp.where(row == col, 1.0, 0.0)
    blk = lambda n: (row >> (n.bit_length() - 1)) == (col >> (n.bit_length() - 1))
    size = row.shape[0]
    p = [jnp.where(blk(16), -a, 0.0) for a in mats]
    t = [eye + x for x in p]
    p = [_mm(x, x) for x in p]
    for _ in range(2):
        both = [_mm(jnp.concatenate([x, y], axis=0), x) for x, y in zip(p, t)]
        p = [b[:size] for b in both]
        t = [y + b[size:] for y, b in zip(t, both)]
    t = [y + _mm(y, x) for y, x in zip(t, p)]
    for n in (16, 32):
        band = blk(2 * n) & jnp.logical_not(blk(n))
        left = [_mm(y, jnp.where(band, a, 0.0)) for y, a in zip(t, mats)]
        t = [y - _mm(x, y) for y, x in zip(t, left)]
    return t


def _gdn_kernel(qkv_ref, z_ref, ab_ref, abt_ref, dtb_r_ref, dtb_c_ref, alog_r_ref,
                alog_c_ref, nw_ref, tri_ref, trit_ref, o_ref, s_ref):
    n = GDN_TILE

    @pl.when(pl.program_id(1) == 0)
    def _():
        s_ref[...] = jnp.zeros_like(s_ref)

    for sub in range(qkv_ref.shape[0] // n):
        rows = slice(sub * n, (sub + 1) * n)
        _gdn_tile(_silu(qkv_ref[rows, :]), z_ref[rows, :], ab_ref[rows, :], abt_ref[:, rows], dtb_r_ref, dtb_c_ref,
                  alog_r_ref, alog_c_ref, nw_ref, tri_ref, trit_ref, o_ref.at[rows, :], s_ref)


def _gdn_tile(qkv, z, ab, abt, dtb_r_ref, dtb_c_ref, alog_r_ref, alog_c_ref, nw_ref, tri_ref, trit_ref, o_ref, s_ref):
    n = GDN_TILE
    c = GDN_CHUNK
    d = GDN_D
    hd = GDN_HEADS * d
    g = -jnp.exp(alog_r_ref[...]) * _softplus(ab + dtb_r_ref[...])
    gc_full = _spread_heads(_mm_sel_lhs(tri_ref[...], g), 0, GDN_HEADS, d)
    beta_full = _spread_heads(jax.nn.sigmoid(ab), GDN_HEADS, GDN_HEADS, d)
    gt = -jnp.exp(alog_c_ref[...]) * _softplus(abt + dtb_c_ref[...])
    gct = _mm_sel_rhs(gt, trit_ref[...])

    row = lax.broadcasted_iota(jnp.int32, (n, n), 0)
    col = lax.broadcasted_iota(jnp.int32, (n, n), 1)
    same = (row >> (c.bit_length() - 1)) == (col >> (c.bit_length() - 1))
    incl = same & (row >= col)
    strict = same & (row > col)
    zeros_half = jnp.zeros((c, d), F32)

    heads = range(GDN_HEADS)
    sl = [slice(h * d, (h + 1) * d) for h in heads]
    l2n = lambda x: x * lax.rsqrt(jnp.sum(x * x, axis=-1, keepdims=True) + EPS)
    qn = [l2n(qkv[:, sl[h]]) * (d ** -0.5) for h in heads]
    kn = [l2n(qkv[:, hd + h * d:hd + (h + 1) * d]) for h in heads]
    vh = [qkv[:, 2 * hd + h * d:2 * hd + (h + 1) * d] for h in heads]
    gcol = [gc_full[:, sl[h]] for h in heads]
    beta = [beta_full[:, sl[h]] for h in heads]
    edec = [jnp.exp(gcol[h] - gct[h:h + 1, :]) for h in heads]
    egc = [jnp.exp(x) for x in gcol]
    kb = [kn[h] * beta[h] for h in heads]
    on_k = [_mm_nt(jnp.concatenate([kb[h], qn[h]], axis=0), kn[h]) for h in heads]
    lower = [jnp.where(strict, on_k[h][:n] * edec[h], 0.0) for h in heads]
    aqk = [jnp.where(incl, on_k[h][n:] * edec[h], 0.0) for h in heads]
    tinv = _unit_lower_inverse(lower, row, col)
    sol = [_mm(tinv[h], jnp.concatenate([vh[h] * beta[h], kb[h] * egc[h]], axis=1)) for h in heads]
    qd = [qn[h] * egc[h] for h in heads]
    glast = [(gcol[h][c - 1:c, :], gcol[h][n - 1:n, :]) for h in heads]
    kdt = [(kn[h] * jnp.exp(jnp.concatenate([jnp.broadcast_to(glast[h][0], (c, d)),
                                             jnp.broadcast_to(glast[h][1], (c, d))], axis=0) - gcol[h])).T
           for h in heads]
    s0 = [s_ref[h] for h in heads]
    on_s0 = [_mm(jnp.concatenate([sol[h][:c, d:], qd[h][:c]], axis=0), s0[h]) for h in heads]
    v0 = [sol[h][:c, :d] - on_s0[h][:c] for h in heads]
    s1 = [s0[h] * jnp.exp(glast[h][0]) + _mm(kdt[h], jnp.concatenate([v0[h], zeros_half], axis=0)) for h in heads]
    on_s1 = [_mm(jnp.concatenate([sol[h][c:, d:], qd[h][c:]], axis=0), s1[h]) for h in heads]
    v1 = [sol[h][c:, :d] - on_s1[h][:c] for h in heads]
    for h in heads:
        s_ref[h] = s1[h] * jnp.exp(glast[h][1]) + _mm(kdt[h], jnp.concatenate([zeros_half, v1[h]], axis=0))
    outs = []
    for h in heads:
        o = (jnp.concatenate([on_s0[h][c:], on_s1[h][c:]], axis=0)
             + _mm(aqk[h], jnp.concatenate([v0[h], v1[h]], axis=0)))
        o = o * lax.rsqrt(jnp.mean(o * o, axis=-1, keepdims=True) + EPS) * nw_ref[...]
        outs.append(o * _silu(z[:, sl[h]]))
    o_ref[...] = jnp.concatenate(outs, axis=1)


def gated_deltanet_mixer(qkv3, y3, small3, smallt, dt_bias, a_log, norm_w):
    bsz, length, _ = y3.shape
    n = GDN_TILE
    hd = GDN_HEADS * GDN_D
    idx = np.arange(n)
    same = (idx[:, None] // GDN_CHUNK) == (idx[None, :] // GDN_CHUNK)
    tri = jnp.asarray(same & (idx[:, None] >= idx[None, :]), BF16)
    trit = jnp.asarray(same & (idx[:, None] <= idx[None, :]), BF16)
    const = lambda a: pl.BlockSpec(a.shape, lambda b, c: (0,) * a.ndim)
    args = [_pad_lanes(dt_bias), _pad_col(dt_bias), _pad_lanes(a_log), _pad_col(a_log),
            norm_w.reshape(1, -1), tri, trit]
    rows = min(GDN_STEP_ROWS, length)
    return pl.pallas_call(
        _gdn_kernel,
        grid=(bsz, length // rows),
        in_specs=[
            pl.BlockSpec((None, rows, 3 * hd), lambda b, c: (b, c, 0)),
            pl.BlockSpec((None, rows, hd), lambda b, c: (b, c, 0)),
            pl.BlockSpec((None, rows, LANES), lambda b, c: (b, c, 0)),
            pl.BlockSpec((None, 16, rows), lambda b, c: (b, 0, c)),
        ] + [const(a) for a in args],
        out_specs=pl.BlockSpec((None, rows, hd), lambda b, c: (b, c, 0)),
        out_shape=jax.ShapeDtypeStruct((bsz, length, hd), F32),
        scratch_shapes=[pltpu.VMEM((GDN_HEADS, GDN_D, GDN_D), F32)],
        compiler_params=_cparams("parallel", "arbitrary"),
        name="gated_deltanet",
    )(qkv3, y3, small3, smallt, *args)


def _sb_kernel(q_ref, k_ref, v_ref, upper_ref, o_ref):
    blk = SB_BLOCK
    pair_w = 2 * SB_HEAD_DIM
    n_pairs = SB_STEP_HEADS // 2
    i = pl.program_id(2)
    q = q_ref[...] * (SB_HEAD_DIM ** -0.5)
    lane = lax.broadcasted_iota(jnp.int32, (blk, pair_w), 1)
    first_head = lane < SB_HEAD_DIM
    qs = []
    for p in range(n_pairs):
        q2 = q[:, p * pair_w:(p + 1) * pair_w]
        qs += [jnp.where(first_head, q2, 0.0).astype(BF16), jnp.where(first_head, 0.0, q2).astype(BF16)]
    row = lax.broadcasted_iota(jnp.int32, (blk, blk), 0)
    col = lax.broadcasted_iota(jnp.int32, (blk, blk), 1)
    earlier = col < row
    upper = upper_ref[...]
    heads = range(SB_STEP_HEADS)

    def local_part(kb, diagonal, exists=None):
        start = pl.multiple_of(kb * blk, blk)
        k = k_ref[pl.ds(start, blk), :].astype(BF16)
        v = v_ref[pl.ds(start, blk), :].astype(BF16)
        kp = [k[:, p * pair_w:(p + 1) * pair_w] for p in range(n_pairs)]
        vp = [v[:, p * pair_w:(p + 1) * pair_w] for p in range(n_pairs)]
        logits = [lax.dot_general(qs[h], kp[h // 2], (((1,), (1,)), ((), ())), preferred_element_type=F32)
                  for h in heads]
        keep = earlier if diagonal else None
        if exists is not None:
            keep = exists if keep is None else keep & exists
        log_keep = [-_softplus(x) for x in logits]
        if keep is not None:
            log_keep = [jnp.where(keep, x, 0.0) for x in log_keep]
        inside = [_mm_sel_rhs(x, upper, 2) for x in log_keep]
        totals = [jnp.sum(x, axis=-1, keepdims=True) for x in log_keep]
        return logits, log_keep, inside, totals, vp, keep

    def carried_part(local, accs, sticks):
        logits, log_keep, inside, totals, vp, keep = local
        w = [jnp.exp(logits[h] + log_keep[h] + inside[h] + sticks[h]) for h in heads]
        if keep is not None:
            w = [jnp.where(keep, x, 0.0) for x in w]
        pv = [jnp.dot(w[h].astype(BF16), vp[h // 2], preferred_element_type=F32) for h in heads]
        accs = tuple(accs[p] + jnp.where(first_head, pv[2 * p], pv[2 * p + 1]) for p in range(n_pairs))
        sticks = tuple(sticks[h] + totals[h] for h in heads)
        return accs, sticks

    accs = tuple(jnp.zeros((blk, pair_w), F32) for _ in range(n_pairs))
    sticks = tuple(jnp.zeros((blk, 1), F32) for _ in heads)
    eager = [local_part(i, True)]
    for back in range(1, SB_EAGER_BLOCKS + 1):
        eager.append(local_part(jnp.maximum(i - back, 0), False, exists=(row >= 0) & (i - back >= 0)))
    for local in eager:
        accs, sticks = carried_part(local, accs, sticks)

    def alive(state):
        kb, _, sticks = state
        longest = sticks[0]
        for s in sticks[1:]:
            longest = jnp.maximum(longest, s)
        return (kb >= 0) & (jnp.max(longest) > SB_LOG_ZERO)

    def body(state):
        kb, accs, sticks = state
        accs, sticks = carried_part(local_part(kb, False), accs, sticks)
        return kb - 1, accs, sticks

    _, accs, _ = lax.while_loop(alive, body, (i - 1 - SB_EAGER_BLOCKS, accs, sticks))
    o_ref[...] = jnp.concatenate(accs, axis=1)


def stick_breaking_mixer(y3, col0):
    bsz, length, _ = y3.shape
    blk = SB_BLOCK
    step_w = SB_STEP_HEADS * SB_HEAD_DIM
    steps = SB_DIM // step_w
    q0 = col0 // step_w
    idx = np.arange(blk)
    upper = jnp.asarray(idx[:, None] > idx[None, :], BF16)
    resident = lambda off: pl.BlockSpec((None, length, step_w), lambda b, p, i: (b, 0, q0 + off + p),
                                        pipeline_mode=pl.Buffered(1))
    return pl.pallas_call(
        _sb_kernel,
        grid=(bsz, steps, length // blk),
        in_specs=[
            pl.BlockSpec((None, blk, step_w), lambda b, p, i: (b, i, q0 + p)),
            resident(steps),
            resident(2 * steps),
            pl.BlockSpec((blk, blk), lambda b, p, i: (0, 0)),
        ],
        out_specs=pl.BlockSpec((None, blk, step_w), lambda b, p, i: (b, i, p)),
        out_shape=jax.ShapeDtypeStruct((bsz, length, SB_DIM), F32),
        compiler_params=_cparams("parallel", "parallel", "arbitrary"),
        name="stick_breaking",
    )(y3, y3, y3, upper)


def _mixer_out(a_ref, b_ref, h_ref, wa_ref, wb_ref, rows):
    return h_ref[rows, :] + (jnp.dot(a_ref[rows, :].astype(BF16), wa_ref[...], preferred_element_type=F32)
                             + jnp.dot(b_ref[rows, :].astype(BF16), wb_ref[...], preferred_element_type=F32))


def _cross_attention(hs, g_ref, wq_ref, kt_ref, v_ref, wo_ref):
    us = [_rms(h, g_ref[...]).astype(BF16) for h in hs]
    qs = [jnp.dot(u, wq_ref[...], preferred_element_type=F32) for u in us]
    heads = [[] for _ in hs]
    for hd in range(XA_HEADS):
        sl = slice(hd * XA_HEAD_DIM, (hd + 1) * XA_HEAD_DIM)
        ss = [jnp.dot(q[:, sl].astype(BF16), kt_ref[sl, :], preferred_element_type=F32) * (XA_HEAD_DIM ** -0.5)
              for q in qs]
        ps = [jnp.exp(s - jnp.max(s, axis=-1, keepdims=True)) for s in ss]
        ps = [p / jnp.sum(p, axis=-1, keepdims=True) for p in ps]
        for k, p in enumerate(ps):
            heads[k].append(jnp.dot(p.astype(BF16), v_ref[:, sl], preferred_element_type=F32))
    os_ = [jnp.concatenate(hk, axis=1).astype(BF16) for hk in heads]
    return [h + jnp.dot(o, wo_ref[...], preferred_element_type=F32) for h, o in zip(hs, os_)]


def _route(xn, whi_ref, wlo_ref, b_ref, before_ref, run_ref):
    x_hi = xn.astype(BF16)
    x_lo = (xn - x_hi.astype(F32)).astype(BF16)
    logits = (jnp.dot(x_hi, whi_ref[...], preferred_element_type=F32)
              + jnp.dot(x_lo, whi_ref[...], preferred_element_type=F32)
              + jnp.dot(x_hi, wlo_ref[...], preferred_element_type=F32) + b_ref[...])
    lane = lax.broadcasted_iota(jnp.int32, logits.shape, 1).astype(F32)
    neg = -1e30
    none = float(LANES)

    def top(vals):
        best = jnp.max(vals, axis=-1, keepdims=True)
        where = jnp.min(jnp.where(vals == best, lane, none), axis=-1, keepdims=True)
        return best, where

    gl = jnp.where(lane < MOE_GROUPS, logits, neg)
    gbest, gsel = top(gl)
    gprob = 1.0 / jnp.sum(jnp.exp(gl - gbest), axis=-1, keepdims=True)
    lo = MOE_GROUPS + gsel * MOE_PER_GROUP
    el = jnp.where((lane >= lo) & (lane < lo + MOE_PER_GROUP), logits, neg)
    m1, i1 = top(el)
    m2, i2 = top(jnp.where(lane == i1, neg, el))
    e = jnp.exp(m2 - m1)
    gate1 = gprob / (1.0 + e)
    gate2 = gprob * e / (1.0 + e)

    hot1 = lane == i1
    hot2 = lane == i2
    one1 = jnp.where(hot1, 1.0, 0.0)
    one2 = jnp.where(hot2, 1.0, 0.0)
    before = before_ref[...]
    prefix1 = jnp.dot(before, one1.astype(BF16), preferred_element_type=F32)
    prefix2 = jnp.dot(before, one2.astype(BF16), preferred_element_type=F32)
    total1 = jnp.sum(one1, axis=0, keepdims=True)
    running = run_ref[...]
    rank1 = jnp.sum(jnp.where(hot1, prefix1 + running, 0.0), axis=-1, keepdims=True)
    rank2 = jnp.sum(jnp.where(hot2, prefix2 + (running + total1), 0.0), axis=-1, keepdims=True)
    running = running + total1 + jnp.sum(one2, axis=0, keepdims=True)
    run_ref[...] = running

    fields = (i1 - MOE_GROUPS, i2 - MOE_GROUPS, gate1, gate2, rank1, rank2)
    out = jnp.zeros_like(logits)
    for k, val in enumerate(fields):
        out = jnp.where(lane == k, val, out)
    return out


def _post_mixer_kernel(a_ref, b_ref, h_ref, wa_ref, wb_ref, gxa_ref, wq_ref, kt_ref, v_ref, wo_ref,
                       gffn_ref, whi_ref, wlo_ref, bias_ref, before_ref,
                       h_out_ref, xn_ref, r_ref, cnt_ref, run_ref):
    @pl.when(pl.program_id(0) == 0)
    def _():
        run_ref[...] = jnp.zeros_like(run_ref)

    h = _mixer_out(a_ref, b_ref, h_ref, wa_ref, wb_ref, slice(None))
    h, = _cross_attention([h], gxa_ref, wq_ref, kt_ref, v_ref, wo_ref)
    h_out_ref[...] = h
    xn = _rms(h, gffn_ref[...])
    xn_ref[...] = _pack_halves(xn)
    group = before_ref.shape[0]
    for start in range(0, h_ref.shape[0], group):
        rows = slice(start, start + group)
        r_ref[rows, :] = _route(xn[rows, :], whi_ref, wlo_ref, bias_ref, before_ref, run_ref)
    cnt_ref[...] = run_ref[...]


def post_mixer(ya, yb, h, wa, wb, g_xa, wq, kt, v, wo, g_ffn, w_hi, w_lo, bias, tm=1024):
    m, d = h.shape
    tiles_per_batch = m // kt.shape[0] // tm
    idx = np.arange(min(POST_GROUP_ROWS, tm))
    before = jnp.asarray(idx[:, None] > idx[None, :], BF16)
    rows = lambda w: pl.BlockSpec((tm, w), lambda i: (i, 0))
    const = lambda a: pl.BlockSpec(a.shape, lambda i: (0,) * a.ndim, pipeline_mode=pl.Buffered(1))
    per_batch = lambda a: pl.BlockSpec((None,) + a.shape[1:], lambda i: (i // tiles_per_batch, 0, 0))
    g_xa, g_ffn = g_xa.reshape(1, d), g_ffn.reshape(1, d)
    return pl.pallas_call(
        _post_mixer_kernel,
        grid=(m // tm,),
        in_specs=[rows(ya.shape[1]), rows(yb.shape[1]), rows(d), const(wa), const(wb), const(g_xa), const(wq),
                  per_batch(kt), per_batch(v), const(wo), const(g_ffn), const(w_hi), const(w_lo), const(bias),
                  const(before)],
        out_specs=[rows(d), rows(d // 2), rows(LANES), pl.BlockSpec((1, LANES), lambda i: (0, 0))],
        out_shape=[jax.ShapeDtypeStruct((m, d), F32), jax.ShapeDtypeStruct((m, d // 2), jnp.int32),
                   jax.ShapeDtypeStruct((m, LANES), F32), jax.ShapeDtypeStruct((1, LANES), F32)],
        scratch_shapes=[pltpu.VMEM((1, LANES), F32)],
        compiler_params=_cparams("arbitrary"),
        name="post_mixer",
    )(ya, yb, h, wa, wb, g_xa, wq, kt, v, wo, g_ffn, w_hi, w_lo, bias, before)


def _expert_kernel(table_ref, x_ref, wg_hbm, wu_hbm, wd_hbm, o_ref,
                   wg32_ref, wu32_ref, wd32_ref, wgb_ref, wub_ref, wdb_ref, sem_ref, *, layer):
    i = pl.program_id(0)
    beid_ref, valid_ref, first_ref, slot_ref, next_ref = (table_ref.at[k] for k in range(5))
    valid = valid_ref[i]

    def weight_copies(expert, slot):
        return (pltpu.make_async_copy(wg_hbm.at[layer, expert], wg32_ref.at[slot], sem_ref.at[slot, 0]),
                pltpu.make_async_copy(wu_hbm.at[layer, expert], wu32_ref.at[slot], sem_ref.at[slot, 1]),
                pltpu.make_async_copy(wd_hbm.at[layer, expert], wd32_ref.at[slot], sem_ref.at[slot, 2]))

    @pl.when(i == 0)
    def _():
        for copy in weight_copies(beid_ref[0], 0):
            copy.start()

    @pl.when(first_ref[i] == 1)
    def _():
        slot = slot_ref[i]
        for copy in weight_copies(beid_ref[i], slot):
            copy.wait()
        wgb_ref[...] = wg32_ref[slot].astype(BF16)
        wub_ref[...] = wu32_ref[slot].astype(BF16)
        wdb_ref[...] = wd32_ref[slot].astype(BF16)

        @pl.when(next_ref[i] >= 0)
        def _():
            for copy in weight_copies(next_ref[i], 1 - slot):
                copy.start()

    half = MOE_ROWS // 2

    def ffn(n_halves):
        row = lax.broadcasted_iota(jnp.int32, (half, 2 * x_ref.shape[1]), 0)
        xs = [jnp.where(row + k * half < valid, _unpack_halves(x_ref[k * half:(k + 1) * half, :]), 0.0).astype(BF16)
              for k in range(n_halves)]
        gates = [jnp.dot(x, wgb_ref[...], preferred_element_type=F32) for x in xs]
        ups = [jnp.dot(x, wub_ref[...], preferred_element_type=F32) for x in xs]
        acts = [(_silu(g) * u).astype(BF16) for g, u in zip(gates, ups)]
        for k, act in enumerate(acts):
            o_ref[k * half:(k + 1) * half, :] = _pack_halves(jnp.dot(act, wdb_ref[...], preferred_element_type=F32))

    @pl.when(valid > half)
    def _():
        ffn(2)

    @pl.when((valid > 0) & (valid <= half))
    def _():
        ffn(1)
        o_ref[half:, :] = jnp.zeros((half, o_ref.shape[1]), o_ref.dtype)

    @pl.when(valid == 0)
    def _():
        o_ref[...] = jnp.zeros_like(o_ref)


def moe_experts(blocks, xs, w_gate, w_up, w_down, layer):
    n_slots, packed = xs.shape
    d = 2 * packed
    rows = MOE_ROWS
    ff = w_gate.shape[3]
    grid_spec = pltpu.PrefetchScalarGridSpec(
        num_scalar_prefetch=1,
        grid=(n_slots // rows,),
        in_specs=[
            pl.BlockSpec((rows, packed), lambda i, *_: (i, 0)),
            pl.BlockSpec(memory_space=pl.ANY),
            pl.BlockSpec(memory_space=pl.ANY),
            pl.BlockSpec(memory_space=pl.ANY),
        ],
        out_specs=pl.BlockSpec((rows, packed), lambda i, *_: (i, 0)),
        scratch_shapes=[pltpu.VMEM((2, d, ff), F32), pltpu.VMEM((2, d, ff), F32), pltpu.VMEM((2, ff, d), F32),
                        pltpu.VMEM((d, ff), BF16), pltpu.VMEM((d, ff), BF16), pltpu.VMEM((ff, d), BF16),
                        pltpu.SemaphoreType.DMA((2, 3))],
    )
    return pl.pallas_call(
        functools.partial(_expert_kernel, layer=layer),
        grid_spec=grid_spec,
        out_shape=jax.ShapeDtypeStruct((n_slots, packed), jnp.int32),
        compiler_params=_cparams("arbitrary"),
        name="moe_experts",
    )(blocks, xs, w_gate, w_up, w_down)


def _sc_mesh():
    return plsc.VectorSubcoreMesh(core_axis_name="c", subcore_axis_name="s",
                                  num_cores=SC_CORES, num_subcores=SC_SUBCORES)


def _sc_worker():
    return lax.axis_index("s") * SC_CORES + lax.axis_index("c")


def _sc_double_buffered(n_chunks, fetch, drain):
    assert n_chunks % 2 == 0
    start = lambda copies: [c.start() for c in copies]
    wait = lambda copies: [c.wait() for c in copies]
    start(fetch(0, 0))

    @pl.loop(0, n_chunks, step=2)
    def _(j):
        wait(fetch(j, 0))

        @pl.when(j > 0)
        def _():
            wait(drain(j - 1, 1))

        start(fetch(j + 1, 1))
        start(drain(j, 0))
        wait(fetch(j + 1, 1))
        wait(drain(j, 0))

        @pl.when(j + 2 < n_chunks)
        def _():
            start(fetch(j + 2, 0))

        start(drain(j + 1, 1))

    wait(drain(n_chunks - 1, 1))


def sc_scatter_rows(x, dest, n_slots):
    n_tok, d = x.shape
    per_worker = n_tok // SC_WORKERS
    n_chunks = per_worker // SC_CHUNK
    by_worker = dest.reshape(dest.shape[0] * SC_WORKERS, n_chunks, SC_CHUNK)

    @functools.partial(
        pl.kernel, mesh=_sc_mesh(), out_type=jax.ShapeDtypeStruct((n_slots, d), x.dtype),
        scratch_types=[pltpu.VMEM((n_chunks, SC_CHUNK), jnp.int32), pltpu.VMEM((n_chunks, SC_CHUNK), jnp.int32),
                       pltpu.VMEM((2, SC_CHUNK, d), x.dtype), pltpu.SemaphoreType.DMA((2, 3))],
        name="moe_scatter_rows")
    def scatter(x_hbm, dest_hbm, out_hbm, i0_v, i1_v, rows_v, sem):
        wid = _sc_worker()
        pltpu.sync_copy(dest_hbm.at[wid], i0_v)
        pltpu.sync_copy(dest_hbm.at[SC_WORKERS + wid], i1_v)

        def fetch(j, buf):
            start = pl.multiple_of(wid * per_worker + j * SC_CHUNK, SC_CHUNK)
            return [pltpu.make_async_copy(x_hbm.at[pl.ds(start, SC_CHUNK)], rows_v.at[buf], sem.at[buf, 0])]

        def drain(j, buf):
            return [pltpu.make_async_copy(rows_v.at[buf], out_hbm.at[i0_v.at[j]], sem.at[buf, 1]),
                    pltpu.make_async_copy(rows_v.at[buf], out_hbm.at[i1_v.at[j]], sem.at[buf, 2])]

        _sc_double_buffered(n_chunks, fetch, drain)

    return scatter(x, by_worker)


def sc_gather_rows(table, idx, n_out):
    d = table.shape[1]
    per_worker = n_out // SC_WORKERS
    n_chunks = per_worker // SC_CHUNK

    @functools.partial(
        pl.kernel, mesh=_sc_mesh(), out_type=jax.ShapeDtypeStruct((n_out, d), table.dtype),
        scratch_types=[pltpu.VMEM((n_chunks, SC_CHUNK), jnp.int32), pltpu.VMEM((2, SC_CHUNK, d), table.dtype),
                       pltpu.SemaphoreType.DMA((2, 2))],
        name="moe_gather_rows")
    def gather(table_hbm, idx_hbm, out_hbm, idx_v, rows_v, sem):
        wid = _sc_worker()
        pltpu.sync_copy(idx_hbm.at[wid], idx_v)

        def fetch(j, buf):
            return [pltpu.make_async_copy(table_hbm.at[idx_v.at[j]], rows_v.at[buf], sem.at[buf, 0])]

        def drain(j, buf):
            start = pl.multiple_of(wid * per_worker + j * SC_CHUNK, SC_CHUNK)
            return [pltpu.make_async_copy(rows_v.at[buf], out_hbm.at[pl.ds(start, SC_CHUNK)], sem.at[buf, 1])]

        _sc_double_buffered(n_chunks, fetch, drain)

    return gather(table, idx.reshape(-1, n_chunks, SC_CHUNK))


def _combine_kernel(h_ref, y0_ref, y1_ref, r_ref, g_ref, o_ref, *, final_norm):
    route = r_ref[...]
    h = h_ref[...] + (route[:, 2:3] * _unpack_halves(y0_ref[...]) + route[:, 3:4] * _unpack_halves(y1_ref[...]))
    o_ref[...] = _rms(h, g_ref[...]) if final_norm else h


def moe_combine(h, y01, route, g, final_norm, tm=1024):
    m, d = h.shape
    tm = min(tm, m)
    rows = lambda w: pl.BlockSpec((tm, w), lambda i: (i, 0))
    return pl.pallas_call(
        functools.partial(_combine_kernel, final_norm=final_norm),
        grid=(m // tm,),
        in_specs=[rows(d), rows(d // 2), pl.BlockSpec((tm, d // 2), lambda i: (i + m // tm, 0)), rows(LANES),
                  pl.BlockSpec((1, d), lambda i: (0, 0))],
        out_specs=rows(d),
        out_shape=jax.ShapeDtypeStruct((m, d), F32),
        compiler_params=_cparams("parallel"),
        name="moe_combine",
    )(h, y01, y01, route, g.reshape(1, d))


def _pad_cols(w):
    return jnp.pad(w, ((0, 0), (0, LANES - w.shape[1])))


def _plan_kernel(route_ref, cnt_ref, incl_ref, dest_ref, table_ref):
    f32_sum = lambda x, axis: jnp.sum(x, axis=axis, keepdims=True)
    lane = lax.broadcasted_iota(jnp.int32, (LANES, LANES), 1)
    sub = lax.broadcasted_iota(jnp.int32, (LANES, LANES), 0)
    incl = incl_ref[...]
    is_expert = (lane >= MOE_GROUPS) & (lane < MOE_GROUPS + MOE_EXPERTS)
    shift = MOE_ROWS.bit_length() - 1
    counts = jnp.broadcast_to(cnt_ref[...], (LANES, LANES)).astype(jnp.int32)
    padded = jnp.where(is_expert, ((counts + (MOE_ROWS - 1)) >> shift) << shift, 0)
    pad_end = _mm_sel_rhs(padded.astype(F32), incl)
    pad_start = pad_end - padded.astype(F32)

    route = route_ref[...]
    lane_t = lax.broadcasted_iota(jnp.int32, route.shape, 1)
    lane_f = lane_t.astype(F32)
    start_row = pad_start[0:1, :]
    slots = [f32_sum(jnp.where(lane_f == route[:, k:k + 1] + MOE_GROUPS, start_row, 0.0), 1) + route[:, 4 + k:5 + k]
             for k in range(2)]
    both = jnp.where(lane_t == 0, slots[0], jnp.where(lane_t == 1, slots[1], 0.0))
    dest_ref[...] = both.T[0:8, :].astype(jnp.int32)

    on_sub = lambda rows_equal: rows_equal.T
    expert_sub = (sub >= MOE_GROUPS) & (sub < MOE_GROUPS + MOE_EXPERTS)
    block_start = (lane * MOE_ROWS).astype(F32)
    eid = f32_sum(jnp.where(expert_sub & (on_sub(pad_end) <= block_start), 1.0, 0.0), 0)
    eid = jnp.minimum(eid, float(MOE_EXPERTS - 1))
    filled = on_sub(pad_start + counts.astype(F32))
    own = (sub - MOE_GROUPS).astype(F32) == eid
    valid = jnp.clip(f32_sum(jnp.where(own, filled, 0.0), 0) - block_start[0:1, :], 0.0, float(MOE_ROWS))
    eid_rows = jnp.broadcast_to(eid, (LANES, LANES))
    changed = (lane == 0) | (eid_rows != pltpu.roll(eid_rows, 1, axis=1))
    first = jnp.where((jnp.broadcast_to(valid, (LANES, LANES)) > 0) & changed, 1.0, 0.0)
    ordinal = _mm_sel_rhs(first, incl) - 1.0
    slot = ordinal - 2.0 * jnp.floor(ordinal * 0.5)
    later = (on_sub(first) > 0) & (sub > lane)
    nearest = jnp.min(jnp.where(later, sub, LANES), axis=0, keepdims=True)
    next_eid = f32_sum(jnp.where(sub == nearest, on_sub(eid_rows), 0.0), 0)
    next_eid = jnp.where(nearest < LANES, next_eid, -1.0)
    row8 = lax.broadcasted_iota(jnp.int32, (8, LANES), 0)
    table = jnp.zeros((8, LANES), F32)
    for k, val in enumerate((eid, valid, first[0:1, :], slot[0:1, :], next_eid)):
        table = jnp.where(row8 == k, val, table)
    table_ref[...] = table.astype(jnp.int32)


def moe_plan(route, counts, tm=2048):
    n_tok = route.shape[0]
    tm = min(tm, n_tok)
    idx = np.arange(LANES)
    incl = jnp.asarray(idx[:, None] <= idx[None, :], BF16)
    return pl.pallas_call(
        _plan_kernel,
        grid=(n_tok // tm,),
        in_specs=[pl.BlockSpec((tm, LANES), lambda i: (i, 0)), pl.BlockSpec((1, LANES), lambda i: (0, 0)),
                  pl.BlockSpec((LANES, LANES), lambda i: (0, 0))],
        out_specs=[pl.BlockSpec((8, tm), lambda i: (0, i)), pl.BlockSpec((8, LANES), lambda i: (0, 0))],
        out_shape=[jax.ShapeDtypeStruct((8, n_tok), jnp.int32), jax.ShapeDtypeStruct((8, LANES), jnp.int32)],
        compiler_params=_cparams("arbitrary"),
        name="moe_plan",
    )(route, counts, incl)


def _router_weights(w_group, b_group, w_expert, b_expert):
    w_r = _pad_cols(jnp.concatenate([w_group, w_expert], axis=1))
    w_hi = w_r.astype(BF16)
    w_lo = (w_r - w_hi.astype(F32)).astype(BF16)
    return w_hi, w_lo, _pad_lanes(jnp.concatenate([b_group, b_expert]))


def _moe_layer(h, xn, route, counts, w_gate, w_up, w_down, layer, final_g):
    n_tok, d = h.shape
    n_blocks = -(-(2 * n_tok + MOE_EXPERTS * (MOE_ROWS - 1)) // MOE_ROWS)
    dest, blocks = moe_plan(route, counts)
    xs = sc_scatter_rows(xn, dest, n_blocks * MOE_ROWS)
    ys = moe_experts(blocks, xs, w_gate, w_up, w_down, layer)
    y01 = sc_gather_rows(ys, dest, 2 * n_tok)
    g = jnp.ones((d,), F32) if final_g is None else final_g
    return moe_combine(h, y01, route, g, final_g is not None)


def _memory_kv(memn_in, mem_norm, wk, wv):
    bsz, m, d = memn_in.shape
    w = jnp.concatenate([wk, wv], axis=1).astype(BF16)
    kv, _ = rms_matmul(memn_in.reshape(bsz * m, d), mem_norm, w, jnp.zeros((d, LANES), BF16))
    k = kv[:, :d].reshape(bsz, m, d)
    v = kv[:, d:].reshape(bsz, m, d)
    return jnp.swapaxes(k, 1, 2).astype(BF16), v.astype(BF16)


def kernel(x, mem, mem_norm, final_norm, norm_mix, norm_xa, norm_ffn, xa_wq, xa_wk, xa_wv, xa_wo, moe_w_group, moe_b_group, moe_w_expert, moe_b_expert, moe_w_gate, moe_w_up, moe_w_down, ev_w_in, ev_sc_conv, ev_ssm_conv_w, ev_ssm_conv_b, ev_ssm_dt_bias, ev_ssm_a_log, ev_ssm_d, ev_ssm_norm, ev_w_out, od_w_in, od_gdn_conv, od_gdn_dt_bias, od_gdn_a_log, od_gdn_norm, od_w_out):
    bsz, length, d = x.shape
    n_tok = bsz * length
    depth = norm_mix.shape[0]
    h = x.reshape(n_tok, d)
    for layer in range(depth):
        i = layer // 2
        if layer % 2 == 0:
            w = ev_w_in[i]
            z0 = 3 * SC_DIM
            xbc0 = z0 + SSM_INNER
            w_conv = w[:, xbc0:xbc0 + SSM_XBC].astype(BF16)
            w_small = _pad_cols(w[:, xbc0 + SSM_XBC:]).astype(BF16)
            xbc = rms_matmul_conv(h, norm_mix[layer], w_conv, ev_ssm_conv_w[i], ev_ssm_conv_b[i], length)
            z, ya, small = rms_matmul_gated(h, norm_mix[layer], w[:, z0:xbc0].astype(BF16), w[:, :z0].astype(BF16),
                                            w_small, ev_sc_conv[i], length)
            small3 = small.reshape(bsz, length, LANES)
            smallt = jnp.swapaxes(small3[:, :, :16], 1, 2)
            yb = ssd_mixer(xbc.reshape(bsz, length, -1), z.reshape(bsz, length, -1), small3, smallt,
                           ev_ssm_dt_bias[i], ev_ssm_a_log[i], ev_ssm_d[i], ev_ssm_norm[i])
            w_out = ev_w_out[i].astype(BF16)
            split = SC_DIM
        else:
            w = od_w_in[i]
            qkv_w = 3 * GDN_HEADS * GDN_D
            z_end = qkv_w + GDN_HEADS * GDN_D
            w_conv = w[:, :qkv_w].astype(BF16)
            w_main = jnp.concatenate([w[:, qkv_w:z_end], w[:, z_end + 2 * GDN_HEADS:]], axis=1).astype(BF16)
            w_small = _pad_cols(w[:, z_end:z_end + 2 * GDN_HEADS]).astype(BF16)
            qkv = rms_matmul_conv(h, norm_mix[layer], w_conv, od_gdn_conv[i], jnp.zeros((qkv_w,), F32), length)
            y, small = rms_matmul(h, norm_mix[layer], w_main, w_small, tm=1024, tn=w_main.shape[1])
            y3 = y.reshape(bsz, length, -1)
            small3 = small.reshape(bsz, length, LANES)
            smallt = jnp.swapaxes(small3[:, :, :16], 1, 2)
            ya = gated_deltanet_mixer(qkv.reshape(bsz, length, -1), y3, small3, smallt, od_gdn_dt_bias[i],
                                      od_gdn_a_log[i], od_gdn_norm[i])
            yb = stick_breaking_mixer(y3, GDN_HEADS * GDN_D)
            w_out = od_w_out[i].astype(BF16)
            split = GDN_HEADS * GDN_D
        kt, v = _memory_kv(mem, mem_norm, xa_wk[layer], xa_wv[layer])
        w_hi, w_lo, bias = _router_weights(moe_w_group[layer], moe_b_group[layer], moe_w_expert[layer],
                                           moe_b_expert[layer])
        h, xn, route, counts = post_mixer(
            ya.reshape(n_tok, -1), yb.reshape(n_tok, -1), h, w_out[:split], w_out[split:], norm_xa[layer],
            xa_wq[layer].astype(BF16), kt, v, xa_wo[layer].astype(BF16), norm_ffn[layer], w_hi, w_lo, bias)
        h = _moe_layer(h, xn, route, counts, moe_w_gate, moe_w_up, moe_w_down, layer,
                       final_norm if layer == depth - 1 else None)
    return h.reshape(bsz, length, d)
```

```python
import functools

import jax
import jax.numpy as jnp
import numpy as np
from jax import lax
from jax.experimental import pallas as pl
from jax.experimental.pallas import tpu as pltpu
from jax.experimental.pallas import tpu_sc as plsc

F32 = jnp.float32
BF16 = jnp.bfloat16
EPS = 1e-6

SC_DIM = 512
SSM_HEADS = 16
SSM_HEAD_DIM = 64
SSM_INNER = 1024
SSM_GROUPS = 2
SSM_STATE = 128
SSM_XBC = SSM_INNER + 2 * SSM_GROUPS * SSM_STATE
SSD_CHUNK = 128
SSD_STEP_ROWS = 512
GDN_HEADS = 8
GDN_D = 128
GDN_CHUNK = 64
GDN_TILE = 128
GDN_STEP_ROWS = 512
SB_HEAD_DIM = 64
SB_DIM = 512
SB_BLOCK = 128
SB_STEP_HEADS = 8
SB_EAGER_BLOCKS = 2
XA_HEADS = 4
XA_HEAD_DIM = 256
MOE_GROUPS = 4
MOE_PER_GROUP = 8
MOE_EXPERTS = 32
MOE_ROWS = 512
POST_GROUP_ROWS = 512
SC_CORES = 2
SC_SUBCORES = 16
SC_WORKERS = SC_CORES * SC_SUBCORES
SC_CHUNK = 64
HALO = 8
CONV_CHUNK = 512
LANES = 128
SB_LOG_ZERO = -104.0
VMEM_LIMIT = 56 * 1024 * 1024


def _cparams(*sem):
    return pltpu.CompilerParams(dimension_semantics=sem, vmem_limit_bytes=VMEM_LIMIT)


def _mm(a, b):
    return jnp.dot(a.astype(BF16), b.astype(BF16), preferred_element_type=F32)


def _mm_nt(a, b):
    return lax.dot_general(a.astype(BF16), b.astype(BF16), (((1,), (1,)), ((), ())),
                           preferred_element_type=F32)


def _split_bf16(x, n):
    parts, r = [], x
    for _ in range(n):
        p = r.astype(BF16)
        parts.append(p)
        r = r - p.astype(F32)
    return parts


def _mm_sel_rhs(x, sel, n=3):
    return sum(jnp.dot(p, sel, preferred_element_type=F32) for p in _split_bf16(x, n))


def _mm_sel_lhs(sel, x, n=3):
    return sum(jnp.dot(sel, p, preferred_element_type=F32) for p in _split_bf16(x, n))


def _spread_heads(x, first, n_heads, width):
    rows = x.shape[0]
    col = lambda h: jnp.broadcast_to(x[:, first + h:first + h + 1], (rows, LANES))
    if width == LANES:
        return jnp.concatenate([col(h) for h in range(n_heads)], axis=1)
    left = lax.broadcasted_iota(jnp.int32, (rows, LANES), 1) < width
    return jnp.concatenate([jnp.where(left, col(h), col(h + 1)) for h in range(0, n_heads, 2)], axis=1)


def _pack_halves(x):
    n = x.shape[1] // 2
    lo = pltpu.bitcast(x[:, :n].astype(BF16).astype(F32), jnp.int32)
    hi = pltpu.bitcast(x[:, n:].astype(BF16).astype(F32), jnp.int32)
    return lax.shift_right_logical(lo, 16) | (hi & jnp.int32(-65536))


def _unpack_halves(p):
    lo = pltpu.bitcast(lax.shift_left(p, 16), F32)
    hi = pltpu.bitcast(p & jnp.int32(-65536), F32)
    return jnp.concatenate([lo, hi], axis=1)


def _silu(x):
    return x * jax.nn.sigmoid(x)


def _softplus(x):
    return jnp.maximum(x, 0.0) + jnp.log(1.0 + jnp.exp(-jnp.abs(x)))


def _rms(x, g):
    return x * lax.rsqrt(jnp.mean(x * x, axis=-1, keepdims=True) + EPS) * g


def _rms_matmul_kernel(x_ref, g_ref, w_ref, ws_ref, o_ref, os_ref):
    xn = _rms(x_ref[...], g_ref[...]).astype(BF16)
    o_ref[...] = jnp.dot(xn, w_ref[...], preferred_element_type=F32)
    os_ref[...] = jnp.dot(xn, ws_ref[...], preferred_element_type=F32)


def rms_matmul(x, g, w, ws, tm=512, tn=512):
    m, k = x.shape
    n = w.shape[1]
    tm = min(tm, m)
    main, small = pl.pallas_call(
        _rms_matmul_kernel,
        grid=(n // tn, m // tm),
        in_specs=[
            pl.BlockSpec((tm, k), lambda j, i: (i, 0)),
            pl.BlockSpec((1, k), lambda j, i: (0, 0)),
            pl.BlockSpec((k, tn), lambda j, i: (0, j)),
            pl.BlockSpec((k, LANES), lambda j, i: (0, 0)),
        ],
        out_specs=[
            pl.BlockSpec((tm, tn), lambda j, i: (i, j)),
            pl.BlockSpec((None, tm, LANES), lambda j, i: (j, i, 0)),
        ],
        out_shape=[jax.ShapeDtypeStruct((m, n), F32), jax.ShapeDtypeStruct((n // tn, m, LANES), F32)],
        compiler_params=_cparams("parallel", "parallel"),
        name="rms_matmul",
    )(x, g.reshape(1, k), w, ws)
    return main, small[0]


def _causal_conv(ext_ref, w_ref, rows):
    width = w_ref.shape[0]
    ext = ext_ref[...]
    acc = None
    for j in range(width):
        shift = width - 1 - j
        moved = ext if shift == 0 else pltpu.roll(ext, shift, axis=0)
        term = w_ref[j:j + 1, :] * moved[HALO:HALO + rows, :]
        acc = term if acc is None else acc + term
    return acc


def _rms_matmul_conv_kernel(x_ref, g_ref, w_ref, cw_ref, cb_ref, o_ref, *ext_refs, tiles_per_seq):
    tm = x_ref.shape[0]
    starts_sequence = pl.program_id(1) % tiles_per_seq == 0

    @pl.when(starts_sequence)
    def _():
        for ext_ref in ext_refs:
            ext_ref[0:HALO, :] = jnp.zeros((HALO, CONV_CHUNK), F32)

    @pl.when(jnp.logical_not(starts_sequence))
    def _():
        for ext_ref in ext_refs:
            ext_ref[0:HALO, :] = ext_ref[tm:tm + HALO, :]

    xn = _rms(x_ref[...], g_ref[...]).astype(BF16)
    for c, ext_ref in enumerate(ext_refs):
        cols = slice(c * CONV_CHUNK, (c + 1) * CONV_CHUNK)
        ext_ref[HALO:, :] = jnp.dot(xn, w_ref[:, cols], preferred_element_type=F32)
        o_ref[:, cols] = _causal_conv(ext_ref, cw_ref.at[:, cols], tm) + cb_ref[:, cols]


def rms_matmul_conv(x, g, w, conv_w, conv_b, seq_len, tm=1024, tn=1536):
    m, k = x.shape
    n = w.shape[1]
    cols = lambda rows: pl.BlockSpec((rows, tn), lambda j, i: (0, j))
    return pl.pallas_call(
        functools.partial(_rms_matmul_conv_kernel, tiles_per_seq=seq_len // tm),
        grid=(n // tn, m // tm),
        in_specs=[
            pl.BlockSpec((tm, k), lambda j, i: (i, 0)),
            pl.BlockSpec((1, k), lambda j, i: (0, 0)),
            cols(k), cols(conv_w.shape[0]), cols(1),
        ],
        out_specs=pl.BlockSpec((tm, tn), lambda j, i: (i, j)),
        out_shape=jax.ShapeDtypeStruct((m, n), F32),
        scratch_shapes=[pltpu.VMEM((tm + HALO, CONV_CHUNK), F32)] * (tn // CONV_CHUNK),
        compiler_params=_cparams("arbitrary", "arbitrary"),
        name="rms_matmul_conv",
    )(x, g.reshape(1, k), w, conv_w, conv_b.reshape(1, n))


def _rms_matmul_gated_kernel(x_ref, g_ref, wz_ref, wbcx_ref, ws_ref, cw_ref, z_ref, ya_ref, os_ref, ext_ref,
                             *, tiles_per_seq):
    tm = x_ref.shape[0]
    starts_sequence = pl.program_id(0) % tiles_per_seq == 0

    @pl.when(starts_sequence)
    def _():
        ext_ref[0:HALO, :] = jnp.zeros((HALO, SC_DIM), F32)

    @pl.when(jnp.logical_not(starts_sequence))
    def _():
        ext_ref[0:HALO, :] = ext_ref[tm:tm + HALO, :]

    xn = _rms(x_ref[...], g_ref[...]).astype(BF16)
    z_ref[...] = jnp.dot(xn, wz_ref[...], preferred_element_type=F32)
    os_ref[...] = jnp.dot(xn, ws_ref[...], preferred_element_type=F32)
    bcx = jnp.dot(xn, wbcx_ref[...], preferred_element_type=F32)
    ext_ref[HALO:, :] = bcx[:, SC_DIM:2 * SC_DIM] * bcx[:, 2 * SC_DIM:]
    ya_ref[...] = (bcx[:, :SC_DIM] * _causal_conv(ext_ref, cw_ref, tm)).astype(ya_ref.dtype)


def rms_matmul_gated(x, g, w_z, w_bcx, w_small, conv_w, seq_len, tm=1024):
    m, k = x.shape
    const = lambda a: pl.BlockSpec(a.shape, lambda i: (0,) * a.ndim)
    rows = lambda w: pl.BlockSpec((tm, w), lambda i: (i, 0))
    g = g.reshape(1, k)
    return pl.pallas_call(
        functools.partial(_rms_matmul_gated_kernel, tiles_per_seq=seq_len // tm),
        grid=(m // tm,),
        in_specs=[rows(k), const(g), const(w_z), const(w_bcx), const(w_small), const(conv_w)],
        out_specs=[rows(w_z.shape[1]), rows(SC_DIM), rows(LANES)],
        out_shape=[jax.ShapeDtypeStruct((m, w_z.shape[1]), F32), jax.ShapeDtypeStruct((m, SC_DIM), BF16),
                   jax.ShapeDtypeStruct((m, LANES), F32)],
        scratch_shapes=[pltpu.VMEM((tm + HALO, SC_DIM), F32)],
        compiler_params=_cparams("arbitrary"),
        name="rms_matmul_gated",
    )(x, g, w_z, w_bcx, w_small, conv_w)


def _ssd_kernel(xbc_ref, z_ref, dt_ref, dtt_ref, dtb_r_ref, dtb_c_ref,
                alog_r_ref, alog_c_ref, d_ref, nw_ref, tri_ref, trit_ref,
                o_ref, s_ref):
    q = SSD_CHUNK

    @pl.when(pl.program_id(1) == 0)
    def _():
        s_ref[...] = jnp.zeros_like(s_ref)

    for sub in range(xbc_ref.shape[0] // q):
        rows = slice(sub * q, (sub + 1) * q)
        _ssd_chunk(_silu(xbc_ref[rows, :]), z_ref[rows, :], dt_ref[rows, :], dtt_ref[:, rows], dtb_r_ref, dtb_c_ref,
                   alog_r_ref, alog_c_ref, d_ref, nw_ref, tri_ref, trit_ref, o_ref.at[rows, :], s_ref)


def _ssd_chunk(xbc, z, dt_raw, dtt_raw, dtb_r_ref, dtb_c_ref, alog_r_ref, alog_c_ref, d_ref, nw_ref, tri_ref,
               trit_ref, o_ref, s_ref):
    q = SSD_CHUNK
    hpg = SSM_HEADS // SSM_GROUPS
    gw = hpg * SSM_HEAD_DIM
    xs = xbc[:, :SSM_INNER]
    bm = xbc[:, SSM_INNER:SSM_INNER + SSM_GROUPS * SSM_STATE]
    cm = xbc[:, SSM_INNER + SSM_GROUPS * SSM_STATE:]

    dt = _softplus(dt_raw + dtb_r_ref[...])
    acs = _mm_sel_lhs(tri_ref[...], dt * -jnp.exp(alog_r_ref[...]))
    dtt = _softplus(dtt_raw + dtb_c_ref[...])
    acst = _mm_sel_rhs(dtt * -jnp.exp(alog_c_ref[...]), trit_ref[...])
    dt_full = _spread_heads(dt, 0, SSM_HEADS, SSM_HEAD_DIM)
    acs_full = _spread_heads(acs, 0, SSM_HEADS, SSM_HEAD_DIM)
    acs_col = _spread_heads(acs, 0, SSM_HEADS, q)

    xdt = xs * dt_full
    acs_last = acs_full[q - 1:q, :]
    xw = xdt * jnp.exp(acs_last - acs_full)
    chunk_decay = jnp.exp(acs_last)

    row = lax.broadcasted_iota(jnp.int32, (q, q), 0)
    col = lax.broadcasted_iota(jnp.int32, (q, q), 1)
    causal = row >= col
    lane = lax.broadcasted_iota(jnp.int32, (q, 2 * SSM_HEAD_DIM), 1)

    y_diag, y_off = [], []
    for g in range(SSM_GROUPS):
        bm_g = bm[:, g * SSM_STATE:(g + 1) * SSM_STATE]
        cm_g = cm[:, g * SSM_STATE:(g + 1) * SSM_STATE]
        cb_g = _mm_nt(cm_g, bm_g)
        state = s_ref[g]
        y_off.append(_mm(cm_g, state))
        s_ref[g] = state * chunk_decay[:, g * gw:(g + 1) * gw] + _mm(bm_g.T, xw[:, g * gw:(g + 1) * gw])
        for pair in range(hpg // 2):
            h0 = g * hpg + 2 * pair
            xdt_pair = xdt[:, h0 * SSM_HEAD_DIM:(h0 + 2) * SSM_HEAD_DIM]
            weights = []
            for h in (h0, h0 + 1):
                seg = acs_col[:, h * q:(h + 1) * q] - acst[h:h + 1, :]
                weights.append(cb_g * jnp.where(causal, jnp.exp(seg), 0.0))
            both = _mm(jnp.concatenate(weights, axis=0), xdt_pair)
            y_diag.append(jnp.where(lane < SSM_HEAD_DIM, both[:q], both[q:]))
    y = (jnp.concatenate(y_diag, axis=1) + jnp.concatenate(y_off, axis=1) * jnp.exp(acs_full)
         + xs * d_ref[...])
    y = y * _silu(z)
    halves = []
    for g in range(SSM_GROUPS):
        yg = y[:, g * gw:(g + 1) * gw]
        halves.append(yg * lax.rsqrt(jnp.mean(yg * yg, axis=-1, keepdims=True) + EPS))
    o_ref[...] = (jnp.concatenate(halves, axis=1) * nw_ref[...]).astype(o_ref.dtype)


def _pad_lanes(v, fill=0.0):
    return jnp.pad(v.astype(F32), (0, LANES - v.shape[0]), constant_values=fill).reshape(1, LANES)


def _pad_col(v, rows=16):
    return jnp.pad(v.astype(F32), (0, rows - v.shape[0])).reshape(rows, 1)


def ssd_mixer(xbc3, y3, small3, smallt, dt_bias, a_log, d_skip, norm_w):
    bsz, length, _ = y3.shape
    q = SSD_CHUNK
    tri = jnp.asarray(np.tril(np.ones((q, q), np.float32)), BF16)
    trit = jnp.asarray(np.triu(np.ones((q, q), np.float32)), BF16)
    d_full = jnp.repeat(d_skip.astype(F32), SSM_HEAD_DIM).reshape(1, SSM_INNER)
    const = lambda a: pl.BlockSpec(a.shape, lambda b, c: (0,) * a.ndim)
    args = [_pad_lanes(dt_bias), _pad_col(dt_bias), _pad_lanes(a_log),
            _pad_col(a_log), d_full, norm_w.reshape(1, -1), tri, trit]
    rows = min(SSD_STEP_ROWS, length)
    return pl.pallas_call(
        _ssd_kernel,
        grid=(bsz, length // rows),
        in_specs=[
            pl.BlockSpec((None, rows, SSM_XBC), lambda b, c: (b, c, 0)),
            pl.BlockSpec((None, rows, SSM_INNER), lambda b, c: (b, c, 0)),
            pl.BlockSpec((None, rows, LANES), lambda b, c: (b, c, 0)),
            pl.BlockSpec((None, 16, rows), lambda b, c: (b, 0, c)),
        ] + [const(a) for a in args],
        out_specs=pl.BlockSpec((None, rows, SSM_INNER), lambda b, c: (b, c, 0)),
        out_shape=jax.ShapeDtypeStruct((bsz, length, SSM_INNER), BF16),
        scratch_shapes=[pltpu.VMEM((SSM_GROUPS, SSM_STATE, SSM_INNER // SSM_GROUPS), F32)],
        compiler_params=_cparams("parallel", "arbitrary"),
        name="ssd_mixer",
    )(xbc3, y3, small3, smallt, *args)


def _unit_lower_inverse(mats, row, col):
    eye = jnp.where(row == col, 1.0, 0.0)
    blk = lambda n: (row >> (n.bit_length() - 1)) == (col >> (n.bit_length() - 1))
    size = row.shape[0]
    p = [jnp.where(blk(16), -a, 0.0) for a in mats]
    t = [eye + x for x in p]
    p = [_mm(x, x) for x in p]
    for _ in range(2):
        both = [_mm(jnp.concatenate([x, y], axis=0), x) for x, y in zip(p, t)]
        p = [b[:size] for b in both]
        t = [y + b[size:] for y, b in zip(t, both)]
    t = [y + _mm(y, x) for y, x in zip(t, p)]
    for n in (16, 32):
        band = blk(2 * n) & jnp.logical_not(blk(n))
        left = [_mm(y, jnp.where(band, a, 0.0)) for y, a in zip(t, mats)]
        t = [y - _mm(x, y) for y, x in zip(t, left)]
    return t


def _gdn_kernel(qkv_ref, z_ref, ab_ref, abt_ref, dtb_r_ref, dtb_c_ref, alog_r_ref,
                alog_c_ref, nw_ref, tri_ref, trit_ref, o_ref, s_ref):
    n = GDN_TILE

    @pl.when(pl.program_id(1) == 0)
    def _():
        s_ref[...] = jnp.zeros_like(s_ref)

    for sub in range(qkv_ref.shape[0] // n):
        rows = slice(sub * n, (sub + 1) * n)
        _gdn_tile(_silu(qkv_ref[rows, :]), z_ref[rows, :], ab_ref[rows, :], abt_ref[:, rows], dtb_r_ref, dtb_c_ref,
                  alog_r_ref, alog_c_ref, nw_ref, tri_ref, trit_ref, o_ref.at[rows, :], s_ref)


def _gdn_tile(qkv, z, ab, abt, dtb_r_ref, dtb_c_ref, alog_r_ref, alog_c_ref, nw_ref, tri_ref, trit_ref, o_ref, s_ref):
    n = GDN_TILE
    c = GDN_CHUNK
    d = GDN_D
    hd = GDN_HEADS * d
    g = -jnp.exp(alog_r_ref[...]) * _softplus(ab + dtb_r_ref[...])
    gc_full = _spread_heads(_mm_sel_lhs(tri_ref[...], g), 0, GDN_HEADS, d)
    beta_full = _spread_heads(jax.nn.sigmoid(ab), GDN_HEADS, GDN_HEADS, d)
    gt = -jnp.exp(alog_c_ref[...]) * _softplus(abt + dtb_c_ref[...])
    gct = _mm_sel_rhs(gt, trit_ref[...])

    row = lax.broadcasted_iota(jnp.int32, (n, n), 0)
    col = lax.broadcasted_iota(jnp.int32, (n, n), 1)
    same = (row >> (c.bit_length() - 1)) == (col >> (c.bit_length() - 1))
    incl = same & (row >= col)
    strict = same & (row > col)
    zeros_half = jnp.zeros((c, d), F32)

    heads = range(GDN_HEADS)
    sl = [slice(h * d, (h + 1) * d) for h in heads]
    l2n = lambda x: x * lax.rsqrt(jnp.sum(x * x, axis=-1, keepdims=True) + EPS)
    qn = [l2n(qkv[:, sl[h]]) * (d ** -0.5) for h in heads]
    kn = [l2n(qkv[:, hd + h * d:hd + (h + 1) * d]) for h in heads]
    vh = [qkv[:, 2 * hd + h * d:2 * hd + (h + 1) * d] for h in heads]
    gcol = [gc_full[:, sl[h]] for h in heads]
    beta = [beta_full[:, sl[h]] for h in heads]
    edec = [jnp.exp(gcol[h] - gct[h:h + 1, :]) for h in heads]
    egc = [jnp.exp(x) for x in gcol]
    kb = [kn[h] * beta[h] for h in heads]
    on_k = [_mm_nt(jnp.concatenate([kb[h], qn[h]], axis=0), kn[h]) for h in heads]
    lower = [jnp.where(strict, on_k[h][:n] * edec[h], 0.0) for h in heads]
    aqk = [jnp.where(incl, on_k[h][n:] * edec[h], 0.0) for h in heads]
    tinv = _unit_lower_inverse(lower, row, col)
    sol = [_mm(tinv[h], jnp.concatenate([vh[h] * beta[h], kb[h] * egc[h]], axis=1)) for h in heads]
    qd = [qn[h] * egc[h] for h in heads]
    glast = [(gcol[h][c - 1:c, :], gcol[h][n - 1:n, :]) for h in heads]
    kdt = [(kn[h] * jnp.exp(jnp.concatenate([jnp.broadcast_to(glast[h][0], (c, d)),
                                             jnp.broadcast_to(glast[h][1], (c, d))], axis=0) - gcol[h])).T
           for h in heads]
    s0 = [s_ref[h] for h in heads]
    on_s0 = [_mm(jnp.concatenate([sol[h][:c, d:], qd[h][:c]], axis=0), s0[h]) for h in heads]
    v0 = [sol[h][:c, :d] - on_s0[h][:c] for h in heads]
    s1 = [s0[h] * jnp.exp(glast[h][0]) + _mm(kdt[h], jnp.concatenate([v0[h], zeros_half], axis=0)) for h in heads]
    on_s1 = [_mm(jnp.concatenate([sol[h][c:, d:], qd[h][c:]], axis=0), s1[h]) for h in heads]
    v1 = [sol[h][c:, :d] - on_s1[h][:c] for h in heads]
    for h in heads:
        s_ref[h] = s1[h] * jnp.exp(glast[h][1]) + _mm(kdt[h], jnp.concatenate([zeros_half, v1[h]], axis=0))
    outs = []
    for h in heads:
        o = (jnp.concatenate([on_s0[h][c:], on_s1[h][c:]], axis=0)
             + _mm(aqk[h], jnp.concatenate([v0[h], v1[h]], axis=0)))
        o = o * lax.rsqrt(jnp.mean(o * o, axis=-1, keepdims=True) + EPS) * nw_ref[...]
        outs.append(o * _silu(z[:, sl[h]]))
    o_ref[...] = jnp.concatenate(outs, axis=1).astype(o_ref.dtype)


def gated_deltanet_mixer(qkv3, y3, small3, smallt, dt_bias, a_log, norm_w):
    bsz, length, _ = y3.shape
    n = GDN_TILE
    hd = GDN_HEADS * GDN_D
    idx = np.arange(n)
    same = (idx[:, None] // GDN_CHUNK) == (idx[None, :] // GDN_CHUNK)
    tri = jnp.asarray(same & (idx[:, None] >= idx[None, :]), BF16)
    trit = jnp.asarray(same & (idx[:, None] <= idx[None, :]), BF16)
    const = lambda a: pl.BlockSpec(a.shape, lambda b, c: (0,) * a.ndim)
    args = [_pad_lanes(dt_bias), _pad_col(dt_bias), _pad_lanes(a_log), _pad_col(a_log),
            norm_w.reshape(1, -1), tri, trit]
    rows = min(GDN_STEP_ROWS, length)
    return pl.pallas_call(
        _gdn_kernel,
        grid=(bsz, length // rows),
        in_specs=[
            pl.BlockSpec((None, rows, 3 * hd), lambda b, c: (b, c, 0)),
            pl.BlockSpec((None, rows, hd), lambda b, c: (b, c, 0)),
            pl.BlockSpec((None, rows, LANES), lambda b, c: (b, c, 0)),
            pl.BlockSpec((None, 16, rows), lambda b, c: (b, 0, c)),
        ] + [const(a) for a in args],
        out_specs=pl.BlockSpec((None, rows, hd), lambda b, c: (b, c, 0)),
        out_shape=jax.ShapeDtypeStruct((bsz, length, hd), BF16),
        scratch_shapes=[pltpu.VMEM((GDN_HEADS, GDN_D, GDN_D), F32)],
        compiler_params=_cparams("parallel", "arbitrary"),
        name="gated_deltanet",
    )(qkv3, y3, small3, smallt, *args)


def _sb_kernel(q_ref, k_ref, v_ref, upper_ref, o_ref):
    blk = SB_BLOCK
    pair_w = 2 * SB_HEAD_DIM
    n_pairs = SB_STEP_HEADS // 2
    i = pl.program_id(2)
    q = q_ref[...] * (SB_HEAD_DIM ** -0.5)
    lane = lax.broadcasted_iota(jnp.int32, (blk, pair_w), 1)
    first_head = lane < SB_HEAD_DIM
    qs = []
    for p in range(n_pairs):
        q2 = q[:, p * pair_w:(p + 1) * pair_w]
        qs += [jnp.where(first_head, q2, 0.0).astype(BF16), jnp.where(first_head, 0.0, q2).astype(BF16)]
    row = lax.broadcasted_iota(jnp.int32, (blk, blk), 0)
    col = lax.broadcasted_iota(jnp.int32, (blk, blk), 1)
    earlier = col < row
    upper = upper_ref[...]
    heads = range(SB_STEP_HEADS)

    def local_part(kb, diagonal, exists=None):
        start = pl.multiple_of(kb * blk, blk)
        k = k_ref[pl.ds(start, blk), :].astype(BF16)
        v = v_ref[pl.ds(start, blk), :].astype(BF16)
        kp = [k[:, p * pair_w:(p + 1) * pair_w] for p in range(n_pairs)]
        vp = [v[:, p * pair_w:(p + 1) * pair_w] for p in range(n_pairs)]
        logits = [lax.dot_general(qs[h], kp[h // 2], (((1,), (1,)), ((), ())), preferred_element_type=F32)
                  for h in heads]
        keep = earlier if diagonal else None
        if exists is not None:
            keep = exists if keep is None else keep & exists
        log_keep = [-_softplus(x) for x in logits]
        if keep is not None:
            log_keep = [jnp.where(keep, x, 0.0) for x in log_keep]
        inside = [_mm_sel_rhs(x, upper, 2) for x in log_keep]
        totals = [jnp.sum(x, axis=-1, keepdims=True) for x in log_keep]
        return logits, log_keep, inside, totals, vp, keep

    def carried_part(local, accs, sticks):
        logits, log_keep, inside, totals, vp, keep = local
        w = [jnp.exp(logits[h] + log_keep[h] + inside[h] + sticks[h]) for h in heads]
        if keep is not None:
            w = [jnp.where(keep, x, 0.0) for x in w]
        pv = [jnp.dot(w[h].astype(BF16), vp[h // 2], preferred_element_type=F32) for h in heads]
        accs = tuple(accs[p] + jnp.where(first_head, pv[2 * p], pv[2 * p + 1]) for p in range(n_pairs))
        sticks = tuple(sticks[h] + totals[h] for h in heads)
        return accs, sticks

    accs = tuple(jnp.zeros((blk, pair_w), F32) for _ in range(n_pairs))
    sticks = tuple(jnp.zeros((blk, 1), F32) for _ in heads)
    eager = [local_part(i, True)]
    for back in range(1, SB_EAGER_BLOCKS + 1):
        eager.append(local_part(jnp.maximum(i - back, 0), False, exists=(row >= 0) & (i - back >= 0)))
    for local in eager:
        accs, sticks = carried_part(local, accs, sticks)

    def alive(state):
        kb, _, sticks = state
        longest = sticks[0]
        for s in sticks[1:]:
            longest = jnp.maximum(longest, s)
        return (kb >= 0) & (jnp.max(longest) > SB_LOG_ZERO)

    def body(state):
        kb, accs, sticks = state
        accs, sticks = carried_part(local_part(kb, False), accs, sticks)
        return kb - 1, accs, sticks

    _, accs, _ = lax.while_loop(alive, body, (i - 1 - SB_EAGER_BLOCKS, accs, sticks))
    o_ref[...] = jnp.concatenate(accs, axis=1).astype(o_ref.dtype)


def stick_breaking_mixer(y3, col0):
    bsz, length, _ = y3.shape
    blk = SB_BLOCK
    step_w = SB_STEP_HEADS * SB_HEAD_DIM
    steps = SB_DIM // step_w
    q0 = col0 // step_w
    idx = np.arange(blk)
    upper = jnp.asarray(idx[:, None] > idx[None, :], BF16)
    resident = lambda off: pl.BlockSpec((None, length, step_w), lambda b, p, i: (b, 0, q0 + off + p),
                                        pipeline_mode=pl.Buffered(1))
    return pl.pallas_call(
        _sb_kernel,
        grid=(bsz, steps, length // blk),
        in_specs=[
            pl.BlockSpec((None, blk, step_w), lambda b, p, i: (b, i, q0 + p)),
            resident(steps),
            resident(2 * steps),
            pl.BlockSpec((blk, blk), lambda b, p, i: (0, 0)),
        ],
        out_specs=pl.BlockSpec((None, blk, step_w), lambda b, p, i: (b, i, p)),
        out_shape=jax.ShapeDtypeStruct((bsz, length, SB_DIM), BF16),
        compiler_params=_cparams("parallel", "parallel", "arbitrary"),
        name="stick_breaking",
    )(y3, y3, y3, upper)


def _mixer_out(a_ref, b_ref, h_ref, wa_ref, wb_ref, rows):
    return h_ref[rows, :] + (jnp.dot(a_ref[rows, :].astype(BF16), wa_ref[...], preferred_element_type=F32)
                             + jnp.dot(b_ref[rows, :].astype(BF16), wb_ref[...], preferred_element_type=F32))


def _cross_attention(hs, g_ref, wq_ref, kt_ref, v_ref, wo_ref):
    us = [_rms(h, g_ref[...]).astype(BF16) for h in hs]
    qs = [jnp.dot(u, wq_ref[...], preferred_element_type=F32) for u in us]
    heads = [[] for _ in hs]
    for hd in range(XA_HEADS):
        sl = slice(hd * XA_HEAD_DIM, (hd + 1) * XA_HEAD_DIM)
        ss = [jnp.dot(q[:, sl].astype(BF16), kt_ref[sl, :], preferred_element_type=F32) * (XA_HEAD_DIM ** -0.5)
              for q in qs]
        ps = [jnp.exp(s - jnp.max(s, axis=-1, keepdims=True)) for s in ss]
        ps = [p / jnp.sum(p, axis=-1, keepdims=True) for p in ps]
        for k, p in enumerate(ps):
            heads[k].append(jnp.dot(p.astype(BF16), v_ref[:, sl], preferred_element_type=F32))
    os_ = [jnp.concatenate(hk, axis=1).astype(BF16) for hk in heads]
    return [h + jnp.dot(o, wo_ref[...], preferred_element_type=F32) for h, o in zip(hs, os_)]


def _route(xn, whi_ref, wlo_ref, b_ref, before_ref, run_ref):
    x_hi = xn.astype(BF16)
    x_lo = (xn - x_hi.astype(F32)).astype(BF16)
    logits = (jnp.dot(x_hi, whi_ref[...], preferred_element_type=F32)
              + jnp.dot(x_lo, whi_ref[...], preferred_element_type=F32)
              + jnp.dot(x_hi, wlo_ref[...], preferred_element_type=F32) + b_ref[...])
    lane = lax.broadcasted_iota(jnp.int32, logits.shape, 1).astype(F32)
    neg = -1e30
    none = float(LANES)

    def top(vals):
        best = jnp.max(vals, axis=-1, keepdims=True)
        where = jnp.min(jnp.where(vals == best, lane, none), axis=-1, keepdims=True)
        return best, where

    gl = jnp.where(lane < MOE_GROUPS, logits, neg)
    gbest, gsel = top(gl)
    gprob = 1.0 / jnp.sum(jnp.exp(gl - gbest), axis=-1, keepdims=True)
    lo = MOE_GROUPS + gsel * MOE_PER_GROUP
    el = jnp.where((lane >= lo) & (lane < lo + MOE_PER_GROUP), logits, neg)
    m1, i1 = top(el)
    m2, i2 = top(jnp.where(lane == i1, neg, el))
    e = jnp.exp(m2 - m1)
    gate1 = gprob / (1.0 + e)
    gate2 = gprob * e / (1.0 + e)

    hot1 = lane == i1
    hot2 = lane == i2
    one1 = jnp.where(hot1, 1.0, 0.0)
    one2 = jnp.where(hot2, 1.0, 0.0)
    before = before_ref[...]
    prefix1 = jnp.dot(before, one1.astype(BF16), preferred_element_type=F32)
    prefix2 = jnp.dot(before, one2.astype(BF16), preferred_element_type=F32)
    total1 = jnp.sum(one1, axis=0, keepdims=True)
    running = run_ref[...]
    rank1 = jnp.sum(jnp.where(hot1, prefix1 + running, 0.0), axis=-1, keepdims=True)
    rank2 = jnp.sum(jnp.where(hot2, prefix2 + (running + total1), 0.0), axis=-1, keepdims=True)
    running = running + total1 + jnp.sum(one2, axis=0, keepdims=True)
    run_ref[...] = running

    fields = (i1 - MOE_GROUPS, i2 - MOE_GROUPS, gate1, gate2, rank1, rank2)
    out = jnp.zeros_like(logits)
    for k, val in enumerate(fields):
        out = jnp.where(lane == k, val, out)
    return out


def _post_mixer_kernel(a_ref, b_ref, h_ref, wa_ref, wb_ref, gxa_ref, wq_ref, kt_ref, v_ref, wo_ref,
                       gffn_ref, whi_ref, wlo_ref, bias_ref, before_ref,
                       h_out_ref, xn_ref, r_ref, cnt_ref, run_ref):
    @pl.when(pl.program_id(0) == 0)
    def _():
        run_ref[...] = jnp.zeros_like(run_ref)

    h = _mixer_out(a_ref, b_ref, h_ref, wa_ref, wb_ref, slice(None))
    h, = _cross_attention([h], gxa_ref, wq_ref, kt_ref, v_ref, wo_ref)
    h_out_ref[...] = h
    xn = _rms(h, gffn_ref[...])
    xn_ref[...] = _pack_halves(xn)
    group = before_ref.shape[0]
    for start in range(0, h_ref.shape[0], group):
        rows = slice(start, start + group)
        r_ref[rows, :] = _route(xn[rows, :], whi_ref, wlo_ref, bias_ref, before_ref, run_ref)
    cnt_ref[...] = run_ref[...]


def post_mixer(ya, yb, h, wa, wb, g_xa, wq, kt, v, wo, g_ffn, w_hi, w_lo, bias, tm=1024):
    m, d = h.shape
    tiles_per_batch = m // kt.shape[0] // tm
    idx = np.arange(min(POST_GROUP_ROWS, tm))
    before = jnp.asarray(idx[:, None] > idx[None, :], BF16)
    rows = lambda w: pl.BlockSpec((tm, w), lambda i: (i, 0))
    const = lambda a: pl.BlockSpec(a.shape, lambda i: (0,) * a.ndim, pipeline_mode=pl.Buffered(1))
    per_batch = lambda a: pl.BlockSpec((None,) + a.shape[1:], lambda i: (i // tiles_per_batch, 0, 0))
    g_xa, g_ffn = g_xa.reshape(1, d), g_ffn.reshape(1, d)
    return pl.pallas_call(
        _post_mixer_kernel,
        grid=(m // tm,),
        in_specs=[rows(ya.shape[1]), rows(yb.shape[1]), rows(d), const(wa), const(wb), const(g_xa), const(wq),
                  per_batch(kt), per_batch(v), const(wo), const(g_ffn), const(w_hi), const(w_lo), const(bias),
                  const(before)],
        out_specs=[rows(d), rows(d // 2), rows(LANES), pl.BlockSpec((1, LANES), lambda i: (0, 0))],
        out_shape=[jax.ShapeDtypeStruct((m, d), F32), jax.ShapeDtypeStruct((m, d // 2), jnp.int32),
                   jax.ShapeDtypeStruct((m, LANES), F32), jax.ShapeDtypeStruct((1, LANES), F32)],
        scratch_shapes=[pltpu.VMEM((1, LANES), F32)],
        compiler_params=_cparams("arbitrary"),
        name="post_mixer",
    )(ya, yb, h, wa, wb, g_xa, wq, kt, v, wo, g_ffn, w_hi, w_lo, bias, before)


def _expert_kernel(table_ref, x_ref, wg_hbm, wu_hbm, wd_hbm, o_ref,
                   wg32_ref, wu32_ref, wd32_ref, wgb_ref, wub_ref, wdb_ref, sem_ref, *, layer):
    i = pl.program_id(0)
    beid_ref, valid_ref, first_ref, slot_ref, next_ref = (table_ref.at[k] for k in range(5))
    valid = valid_ref[i]

    def weight_copies(expert, slot):
        return (pltpu.make_async_copy(wg_hbm.at[layer, expert], wg32_ref.at[slot], sem_ref.at[slot, 0]),
                pltpu.make_async_copy(wu_hbm.at[layer, expert], wu32_ref.at[slot], sem_ref.at[slot, 1]),
                pltpu.make_async_copy(wd_hbm.at[layer, expert], wd32_ref.at[slot], sem_ref.at[slot, 2]))

    @pl.when(i == 0)
    def _():
        for copy in weight_copies(beid_ref[0], 0):
            copy.start()

    @pl.when(first_ref[i] == 1)
    def _():
        slot = slot_ref[i]
        for copy in weight_copies(beid_ref[i], slot):
            copy.wait()
        wgb_ref[...] = wg32_ref[slot].astype(BF16)
        wub_ref[...] = wu32_ref[slot].astype(BF16)
        wdb_ref[...] = wd32_ref[slot].astype(BF16)

        @pl.when(next_ref[i] >= 0)
        def _():
            for copy in weight_copies(next_ref[i], 1 - slot):
                copy.start()

    half = MOE_ROWS // 2

    def ffn(n_halves):
        row = lax.broadcasted_iota(jnp.int32, (half, 2 * x_ref.shape[1]), 0)
        xs = [jnp.where(row + k * half < valid, _unpack_halves(x_ref[k * half:(k + 1) * half, :]), 0.0).astype(BF16)
              for k in range(n_halves)]
        gates = [jnp.dot(x, wgb_ref[...], preferred_element_type=F32) for x in xs]
        ups = [jnp.dot(x, wub_ref[...], preferred_element_type=F32) for x in xs]
        acts = [(_silu(g) * u).astype(BF16) for g, u in zip(gates, ups)]
        for k, act in enumerate(acts):
            o_ref[k * half:(k + 1) * half, :] = _pack_halves(jnp.dot(act, wdb_ref[...], preferred_element_type=F32))

    @pl.when(valid > half)
    def _():
        ffn(2)

    @pl.when((valid > 0) & (valid <= half))
    def _():
        ffn(1)
        o_ref[half:, :] = jnp.zeros((half, o_ref.shape[1]), o_ref.dtype)

    @pl.when(valid == 0)
    def _():
        o_ref[...] = jnp.zeros_like(o_ref)


def moe_experts(blocks, xs, w_gate, w_up, w_down, layer):
    n_slots, packed = xs.shape
    d = 2 * packed
    rows = MOE_ROWS
    ff = w_gate.shape[3]
    grid_spec = pltpu.PrefetchScalarGridSpec(
        num_scalar_prefetch=1,
        grid=(n_slots // rows,),
        in_specs=[
            pl.BlockSpec((rows, packed), lambda i, *_: (i, 0)),
            pl.BlockSpec(memory_space=pl.ANY),
            pl.BlockSpec(memory_space=pl.ANY),
            pl.BlockSpec(memory_space=pl.ANY),
        ],
        out_specs=pl.BlockSpec((rows, packed), lambda i, *_: (i, 0)),
        scratch_shapes=[pltpu.VMEM((2, d, ff), F32), pltpu.VMEM((2, d, ff), F32), pltpu.VMEM((2, ff, d), F32),
                        pltpu.VMEM((d, ff), BF16), pltpu.VMEM((d, ff), BF16), pltpu.VMEM((ff, d), BF16),
                        pltpu.SemaphoreType.DMA((2, 3))],
    )
    return pl.pallas_call(
        functools.partial(_expert_kernel, layer=layer),
        grid_spec=grid_spec,
        out_shape=jax.ShapeDtypeStruct((n_slots, packed), jnp.int32),
        compiler_params=_cparams("arbitrary"),
        name="moe_experts",
    )(blocks, xs, w_gate, w_up, w_down)


def _sc_mesh():
    return plsc.VectorSubcoreMesh(core_axis_name="c", subcore_axis_name="s",
                                  num_cores=SC_CORES, num_subcores=SC_SUBCORES)


def _sc_worker():
    return lax.axis_index("s") * SC_CORES + lax.axis_index("c")


def _sc_double_buffered(n_chunks, fetch, drain):
    assert n_chunks % 2 == 0
    start = lambda copies: [c.start() for c in copies]
    wait = lambda copies: [c.wait() for c in copies]
    start(fetch(0, 0))

    @pl.loop(0, n_chunks, step=2)
    def _(j):
        wait(fetch(j, 0))

        @pl.when(j > 0)
        def _():
            wait(drain(j - 1, 1))

        start(fetch(j + 1, 1))
        start(drain(j, 0))
        wait(fetch(j + 1, 1))
        wait(drain(j, 0))

        @pl.when(j + 2 < n_chunks)
        def _():
            start(fetch(j + 2, 0))

        start(drain(j + 1, 1))

    wait(drain(n_chunks - 1, 1))


def sc_scatter_rows(x, dest, n_slots):
    n_tok, d = x.shape
    per_worker = n_tok // SC_WORKERS
    n_chunks = per_worker // SC_CHUNK
    by_worker = dest.reshape(dest.shape[0] * SC_WORKERS, n_chunks, SC_CHUNK)

    @functools.partial(
        pl.kernel, mesh=_sc_mesh(), out_type=jax.ShapeDtypeStruct((n_slots, d), x.dtype),
        scratch_types=[pltpu.VMEM((n_chunks, SC_CHUNK), jnp.int32), pltpu.VMEM((n_chunks, SC_CHUNK), jnp.int32),
                       pltpu.VMEM((2, SC_CHUNK, d), x.dtype), pltpu.SemaphoreType.DMA((2, 3))],
        name="moe_scatter_rows")
    def scatter(x_hbm, dest_hbm, out_hbm, i0_v, i1_v, rows_v, sem):
        wid = _sc_worker()
        pltpu.sync_copy(dest_hbm.at[wid], i0_v)
        pltpu.sync_copy(dest_hbm.at[SC_WORKERS + wid], i1_v)

        def fetch(j, buf):
            start = pl.multiple_of(wid * per_worker + j * SC_CHUNK, SC_CHUNK)
            return [pltpu.make_async_copy(x_hbm.at[pl.ds(start, SC_CHUNK)], rows_v.at[buf], sem.at[buf, 0])]

        def drain(j, buf):
            return [pltpu.make_async_copy(rows_v.at[buf], out_hbm.at[i0_v.at[j]], sem.at[buf, 1]),
                    pltpu.make_async_copy(rows_v.at[buf], out_hbm.at[i1_v.at[j]], sem.at[buf, 2])]

        _sc_double_buffered(n_chunks, fetch, drain)

    return scatter(x, by_worker)


def sc_gather_rows(table, idx, n_out):
    d = table.shape[1]
    per_worker = n_out // SC_WORKERS
    n_chunks = per_worker // SC_CHUNK

    @functools.partial(
        pl.kernel, mesh=_sc_mesh(), out_type=jax.ShapeDtypeStruct((n_out, d), table.dtype),
        scratch_types=[pltpu.VMEM((n_chunks, SC_CHUNK), jnp.int32), pltpu.VMEM((2, SC_CHUNK, d), table.dtype),
                       pltpu.SemaphoreType.DMA((2, 2))],
        name="moe_gather_rows")
    def gather(table_hbm, idx_hbm, out_hbm, idx_v, rows_v, sem):
        wid = _sc_worker()
        pltpu.sync_copy(idx_hbm.at[wid], idx_v)

        def fetch(j, buf):
            return [pltpu.make_async_copy(table_hbm.at[idx_v.at[j]], rows_v.at[buf], sem.at[buf, 0])]

        def drain(j, buf):
            start = pl.multiple_of(wid * per_worker + j * SC_CHUNK, SC_CHUNK)
            return [pltpu.make_async_copy(rows_v.at[buf], out_hbm.at[pl.ds(start, SC_CHUNK)], sem.at[buf, 1])]

        _sc_double_buffered(n_chunks, fetch, drain)

    return gather(table, idx.reshape(-1, n_chunks, SC_CHUNK))


def _combine_kernel(h_ref, y0_ref, y1_ref, r_ref, g_ref, o_ref, *, final_norm):
    route = r_ref[...]
    h = h_ref[...] + (route[:, 2:3] * _unpack_halves(y0_ref[...]) + route[:, 3:4] * _unpack_halves(y1_ref[...]))
    o_ref[...] = _rms(h, g_ref[...]) if final_norm else h


def moe_combine(h, y01, route, g, final_norm, tm=1024):
    m, d = h.shape
    tm = min(tm, m)
    rows = lambda w: pl.BlockSpec((tm, w), lambda i: (i, 0))
    return pl.pallas_call(
        functools.partial(_combine_kernel, final_norm=final_norm),
        grid=(m // tm,),
        in_specs=[rows(d), rows(d // 2), pl.BlockSpec((tm, d // 2), lambda i: (i + m // tm, 0)), rows(LANES),
                  pl.BlockSpec((1, d), lambda i: (0, 0))],
        out_specs=rows(d),
        out_shape=jax.ShapeDtypeStruct((m, d), F32),
        compiler_params=_cparams("parallel"),
        name="moe_combine",
    )(h, y01, y01, route, g.reshape(1, d))


def _pad_cols(w):
    return jnp.pad(w, ((0, 0), (0, LANES - w.shape[1])))


def _plan_kernel(route_ref, cnt_ref, incl_ref, dest_ref, table_ref):
    f32_sum = lambda x, axis: jnp.sum(x, axis=axis, keepdims=True)
    lane = lax.broadcasted_iota(jnp.int32, (LANES, LANES), 1)
    sub = lax.broadcasted_iota(jnp.int32, (LANES, LANES), 0)
    incl = incl_ref[...]
    is_expert = (lane >= MOE_GROUPS) & (lane < MOE_GROUPS + MOE_EXPERTS)
    shift = MOE_ROWS.bit_length() - 1
    counts = jnp.broadcast_to(cnt_ref[...], (LANES, LANES)).astype(jnp.int32)
    padded = jnp.where(is_expert, ((counts + (MOE_ROWS - 1)) >> shift) << shift, 0)
    pad_end = _mm_sel_rhs(padded.astype(F32), incl)
    pad_start = pad_end - padded.astype(F32)

    route = route_ref[...]
    lane_t = lax.broadcasted_iota(jnp.int32, route.shape, 1)
    lane_f = lane_t.astype(F32)
    start_row = pad_start[0:1, :]
    slots = [f32_sum(jnp.where(lane_f == route[:, k:k + 1] + MOE_GROUPS, start_row, 0.0), 1) + route[:, 4 + k:5 + k]
             for k in range(2)]
    both = jnp.where(lane_t == 0, slots[0], jnp.where(lane_t == 1, slots[1], 0.0))
    dest_ref[...] = both.T[0:8, :].astype(jnp.int32)

    on_sub = lambda rows_equal: rows_equal.T
    expert_sub = (sub >= MOE_GROUPS) & (sub < MOE_GROUPS + MOE_EXPERTS)
    block_start = (lane * MOE_ROWS).astype(F32)
    eid = f32_sum(jnp.where(expert_sub & (on_sub(pad_end) <= block_start), 1.0, 0.0), 0)
    eid = jnp.minimum(eid, float(MOE_EXPERTS - 1))
    filled = on_sub(pad_start + counts.astype(F32))
    own = (sub - MOE_GROUPS).astype(F32) == eid
    valid = jnp.clip(f32_sum(jnp.where(own, filled, 0.0), 0) - block_start[0:1, :], 0.0, float(MOE_ROWS))
    eid_rows = jnp.broadcast_to(eid, (LANES, LANES))
    changed = (lane == 0) | (eid_rows != pltpu.roll(eid_rows, 1, axis=1))
    first = jnp.where((jnp.broadcast_to(valid, (LANES, LANES)) > 0) & changed, 1.0, 0.0)
    ordinal = _mm_sel_rhs(first, incl) - 1.0
    slot = ordinal - 2.0 * jnp.floor(ordinal * 0.5)
    later = (on_sub(first) > 0) & (sub > lane)
    nearest = jnp.min(jnp.where(later, sub, LANES), axis=0, keepdims=True)
    next_eid = f32_sum(jnp.where(sub == nearest, on_sub(eid_rows), 0.0), 0)
    next_eid = jnp.where(nearest < LANES, next_eid, -1.0)
    row8 = lax.broadcasted_iota(jnp.int32, (8, LANES), 0)
    table = jnp.zeros((8, LANES), F32)
    for k, val in enumerate((eid, valid, first[0:1, :], slot[0:1, :], next_eid)):
        table = jnp.where(row8 == k, val, table)
    table_ref[...] = table.astype(jnp.int32)


def moe_plan(route, counts, tm=2048):
    n_tok = route.shape[0]
    tm = min(tm, n_tok)
    idx = np.arange(LANES)
    incl = jnp.asarray(idx[:, None] <= idx[None, :], BF16)
    return pl.pallas_call(
        _plan_kernel,
        grid=(n_tok // tm,),
        in_specs=[pl.BlockSpec((tm, LANES), lambda i: (i, 0)), pl.BlockSpec((1, LANES), lambda i: (0, 0)),
                  pl.BlockSpec((LANES, LANES), lambda i: (0, 0))],
        out_specs=[pl.BlockSpec((8, tm), lambda i: (0, i)), pl.BlockSpec((8, LANES), lambda i: (0, 0))],
        out_shape=[jax.ShapeDtypeStruct((8, n_tok), jnp.int32), jax.ShapeDtypeStruct((8, LANES), jnp.int32)],
        compiler_params=_cparams("arbitrary"),
        name="moe_plan",
    )(route, counts, incl)


def _router_weights(w_group, b_group, w_expert, b_expert):
    w_r = _pad_cols(jnp.concatenate([w_group, w_expert], axis=1))
    w_hi = w_r.astype(BF16)
    w_lo = (w_r - w_hi.astype(F32)).astype(BF16)
    return w_hi, w_lo, _pad_lanes(jnp.concatenate([b_group, b_expert]))


def _moe_layer(h, xn, route, counts, w_gate, w_up, w_down, layer, final_g):
    n_tok, d = h.shape
    n_blocks = -(-(2 * n_tok + MOE_EXPERTS * (MOE_ROWS - 1)) // MOE_ROWS)
    dest, blocks = moe_plan(route, counts)
    xs = sc_scatter_rows(xn, dest, n_blocks * MOE_ROWS)
    ys = moe_experts(blocks, xs, w_gate, w_up, w_down, layer)
    y01 = sc_gather_rows(ys, dest, 2 * n_tok)
    g = jnp.ones((d,), F32) if final_g is None else final_g
    return moe_combine(h, y01, route, g, final_g is not None)


def _memory_kv(memn_in, mem_norm, wk, wv):
    bsz, m, d = memn_in.shape
    w = jnp.concatenate([wk, wv], axis=1).astype(BF16)
    kv, _ = rms_matmul(memn_in.reshape(bsz * m, d), mem_norm, w, jnp.zeros((d, LANES), BF16))
    k = kv[:, :d].reshape(bsz, m, d)
    v = kv[:, d:].reshape(bsz, m, d)
    return jnp.swapaxes(k, 1, 2).astype(BF16), v.astype(BF16)


def kernel(x, mem, mem_norm, final_norm, norm_mix, norm_xa, norm_ffn, xa_wq, xa_wk, xa_wv, xa_wo, moe_w_group, moe_b_group, moe_w_expert, moe_b_expert, moe_w_gate, moe_w_up, moe_w_down, ev_w_in, ev_sc_conv, ev_ssm_conv_w, ev_ssm_conv_b, ev_ssm_dt_bias, ev_ssm_a_log, ev_ssm_d, ev_ssm_norm, ev_w_out, od_w_in, od_gdn_conv, od_gdn_dt_bias, od_gdn_a_log, od_gdn_norm, od_w_out):
    bsz, length, d = x.shape
    n_tok = bsz * length
    depth = norm_mix.shape[0]
    h = x.reshape(n_tok, d)
    for layer in range(depth):
        i = layer // 2
        if layer % 2 == 0:
            w = ev_w_in[i]
            z0 = 3 * SC_DIM
            xbc0 = z0 + SSM_INNER
            w_conv = w[:, xbc0:xbc0 + SSM_XBC].astype(BF16)
            w_small = _pad_cols(w[:, xbc0 + SSM_XBC:]).astype(BF16)
            xbc = rms_matmul_conv(h, norm_mix[layer], w_conv, ev_ssm_conv_w[i], ev_ssm_conv_b[i], length)
            z, ya, small = rms_matmul_gated(h, norm_mix[layer], w[:, z0:xbc0].astype(BF16), w[:, :z0].astype(BF16),
                                            w_small, ev_sc_conv[i], length)
            small3 = small.reshape(bsz, length, LANES)
            smallt = jnp.swapaxes(small3[:, :, :16], 1, 2)
            yb = ssd_mixer(xbc.reshape(bsz, length, -1), z.reshape(bsz, length, -1), small3, smallt,
                           ev_ssm_dt_bias[i], ev_ssm_a_log[i], ev_ssm_d[i], ev_ssm_norm[i])
            w_out = ev_w_out[i].astype(BF16)
            split = SC_DIM
        else:
            w = od_w_in[i]
            qkv_w = 3 * GDN_HEADS * GDN_D
            z_end = qkv_w + GDN_HEADS * GDN_D
            w_conv = w[:, :qkv_w].astype(BF16)
            w_main = jnp.concatenate([w[:, qkv_w:z_end], w[:, z_end + 2 * GDN_HEADS:]], axis=1).astype(BF16)
            w_small = _pad_cols(w[:, z_end:z_end + 2 * GDN_HEADS]).astype(BF16)
            qkv = rms_matmul_conv(h, norm_mix[layer], w_conv, od_gdn_conv[i], jnp.zeros((qkv_w,), F32), length)
            y, small = rms_matmul(h, norm_mix[layer], w_main, w_small, tm=1024, tn=w_main.shape[1])
            y3 = y.reshape(bsz, length, -1)
            small3 = small.reshape(bsz, length, LANES)
            smallt = jnp.swapaxes(small3[:, :, :16], 1, 2)
            ya = gated_deltanet_mixer(qkv.reshape(bsz, length, -1), y3, small3, smallt, od_gdn_dt_bias[i],
                                      od_gdn_a_log[i], od_gdn_norm[i])
            yb = stick_breaking_mixer(y3, GDN_HEADS * GDN_D)
            w_out = od_w_out[i].astype(BF16)
            split = GDN_HEADS * GDN_D
        kt, v = _memory_kv(mem, mem_norm, xa_wk[layer], xa_wv[layer])
        w_hi, w_lo, bias = _router_weights(moe_w_group[layer], moe_b_group[layer], moe_w_expert[layer],
                                           moe_b_expert[layer])
        h, xn, route, counts = post_mixer(
            ya.reshape(n_tok, -1), yb.reshape(n_tok, -1), h, w_out[:split], w_out[split:], norm_xa[layer],
            xa_wq[layer].astype(BF16), kt, v, xa_wo[layer].astype(BF16), norm_ffn[layer], w_hi, w_lo, bias)
        h = _moe_layer(h, xn, route, counts, moe_w_gate, moe_w_up, moe_w_down, layer,
                       final_norm if layer == depth - 1 else None)
    return h.reshape(bsz, length, d)
```

```python
import functools

import jax
import jax.numpy as jnp
import numpy as np
from jax import lax
from jax.experimental import pallas as pl
from jax.experimental.pallas import tpu as pltpu
from jax.experimental.pallas import tpu_sc as plsc

F32 = jnp.float32
BF16 = jnp.bfloat16
EPS = 1e-6

SC_DIM = 512
SSM_HEADS = 16
SSM_HEAD_DIM = 64
SSM_INNER = 1024
SSM_GROUPS = 2
SSM_STATE = 128
SSM_XBC = SSM_INNER + 2 * SSM_GROUPS * SSM_STATE
SSD_CHUNK = 128
SSD_STEP_ROWS = 512
GDN_HEADS = 8
GDN_D = 128
GDN_CHUNK = 64
GDN_TILE = 128
GDN_STEP_ROWS = 512
SB_HEAD_DIM = 64
SB_DIM = 512
SB_BLOCK = 128
SB_STEP_HEADS = 8
SB_EAGER_BLOCKS = 2
XA_HEADS = 4
XA_HEAD_DIM = 256
MOE_GROUPS = 4
MOE_PER_GROUP = 8
MOE_EXPERTS = 32
MOE_ROWS = 512
POST_GROUP_ROWS = 512
SC_CORES = 2
SC_SUBCORES = 16
SC_WORKERS = SC_CORES * SC_SUBCORES
SC_CHUNK = 64
HALO = 8
CONV_CHUNK = 512
LANES = 128
SB_LOG_ZERO = -104.0
VMEM_LIMIT = 56 * 1024 * 1024


def _cparams(*sem):
    return pltpu.CompilerParams(dimension_semantics=sem, vmem_limit_bytes=VMEM_LIMIT)


def _mm(a, b):
    return jnp.dot(a.astype(BF16), b.astype(BF16), preferred_element_type=F32)


def _mm_nt(a, b):
    return lax.dot_general(a.astype(BF16), b.astype(BF16), (((1,), (1,)), ((), ())),
                           preferred_element_type=F32)


def _split_bf16(x, n):
    parts, r = [], x
    for _ in range(n):
        p = r.astype(BF16)
        parts.append(p)
        r = r - p.astype(F32)
    return parts


def _mm_sel_rhs(x, sel, n=3):
    return sum(jnp.dot(p, sel, preferred_element_type=F32) for p in _split_bf16(x, n))


def _mm_sel_lhs(sel, x, n=3):
    return sum(jnp.dot(sel, p, preferred_element_type=F32) for p in _split_bf16(x, n))


def _spread_heads(x, first, n_heads, width):
    rows = x.shape[0]
    col = lambda h: jnp.broadcast_to(x[:, first + h:first + h + 1], (rows, LANES))
    if width == LANES:
        return jnp.concatenate([col(h) for h in range(n_heads)], axis=1)
    left = lax.broadcasted_iota(jnp.int32, (rows, LANES), 1) < width
    return jnp.concatenate([jnp.where(left, col(h), col(h + 1)) for h in range(0, n_heads, 2)], axis=1)


def _pack_halves(x):
    n = x.shape[1] // 2
    lo = pltpu.bitcast(x[:, :n].astype(BF16).astype(F32), jnp.int32)
    hi = pltpu.bitcast(x[:, n:].astype(BF16).astype(F32), jnp.int32)
    return lax.shift_right_logical(lo, 16) | (hi & jnp.int32(-65536))


def _unpack_halves(p):
    lo = pltpu.bitcast(lax.shift_left(p, 16), F32)
    hi = pltpu.bitcast(p & jnp.int32(-65536), F32)
    return jnp.concatenate([lo, hi], axis=1)


def _silu(x):
    return x * jax.nn.sigmoid(x)


def _softplus(x):
    return jnp.maximum(x, 0.0) + jnp.log(1.0 + jnp.exp(-jnp.abs(x)))


def _rms(x, g):
    return x * lax.rsqrt(jnp.mean(x * x, axis=-1, keepdims=True) + EPS) * g


def _rms_matmul_kernel(x_ref, g_ref, w_ref, ws_ref, o_ref, os_ref):
    xn = _rms(x_ref[...], g_ref[...]).astype(BF16)
    o_ref[...] = jnp.dot(xn, w_ref[...], preferred_element_type=F32)
    os_ref[...] = jnp.dot(xn, ws_ref[...], preferred_element_type=F32)


def rms_matmul(x, g, w, ws, tm=512, tn=512):
    m, k = x.shape
    n = w.shape[1]
    tm = min(tm, m)
    main, small = pl.pallas_call(
        _rms_matmul_kernel,
        grid=(n // tn, m // tm),
        in_specs=[
            pl.BlockSpec((tm, k), lambda j, i: (i, 0)),
            pl.BlockSpec((1, k), lambda j, i: (0, 0)),
            pl.BlockSpec((k, tn), lambda j, i: (0, j)),
            pl.BlockSpec((k, LANES), lambda j, i: (0, 0)),
        ],
        out_specs=[
            pl.BlockSpec((tm, tn), lambda j, i: (i, j)),
            pl.BlockSpec((None, tm, LANES), lambda j, i: (j, i, 0)),
        ],
        out_shape=[jax.ShapeDtypeStruct((m, n), F32), jax.ShapeDtypeStruct((n // tn, m, LANES), F32)],
        compiler_params=_cparams("parallel", "parallel"),
        name="rms_matmul",
    )(x, g.reshape(1, k), w, ws)
    return main, small[0]


def _causal_conv(ext_ref, w_ref, rows):
    width = w_ref.shape[0]
    ext = ext_ref[...]
    acc = None
    for j in range(width):
        shift = width - 1 - j
        moved = ext if shift == 0 else pltpu.roll(ext, shift, axis=0)
        term = w_ref[j:j + 1, :] * moved[HALO:HALO + rows, :]
        acc = term if acc is None else acc + term
    return acc


def _rms_matmul_conv_kernel(x_ref, g_ref, w_ref, cw_ref, cb_ref, o_ref, *ext_refs, tiles_per_seq):
    tm = x_ref.shape[0]
    starts_sequence = pl.program_id(1) % tiles_per_seq == 0

    @pl.when(starts_sequence)
    def _():
        for ext_ref in ext_refs:
            ext_ref[0:HALO, :] = jnp.zeros((HALO, CONV_CHUNK), F32)

    @pl.when(jnp.logical_not(starts_sequence))
    def _():
        for ext_ref in ext_refs:
            ext_ref[0:HALO, :] = ext_ref[tm:tm + HALO, :]

    xn = _rms(x_ref[...], g_ref[...]).astype(BF16)
    for c, ext_ref in enumerate(ext_refs):
        cols = slice(c * CONV_CHUNK, (c + 1) * CONV_CHUNK)
        ext_ref[HALO:, :] = jnp.dot(xn, w_ref[:, cols], preferred_element_type=F32)
        o_ref[:, cols] = _causal_conv(ext_ref, cw_ref.at[:, cols], tm) + cb_ref[:, cols]


def rms_matmul_conv(x, g, w, conv_w, conv_b, seq_len, tm=1024, tn=1536):
    m, k = x.shape
    n = w.shape[1]
    cols = lambda rows: pl.BlockSpec((rows, tn), lambda j, i: (0, j))
    return pl.pallas_call(
        functools.partial(_rms_matmul_conv_kernel, tiles_per_seq=seq_len // tm),
        grid=(n // tn, m // tm),
        in_specs=[
            pl.BlockSpec((tm, k), lambda j, i: (i, 0)),
            pl.BlockSpec((1, k), lambda j, i: (0, 0)),
            cols(k), cols(conv_w.shape[0]), cols(1),
        ],
        out_specs=pl.BlockSpec((tm, tn), lambda j, i: (i, j)),
        out_shape=jax.ShapeDtypeStruct((m, n), F32),
        scratch_shapes=[pltpu.VMEM((tm + HALO, CONV_CHUNK), F32)] * (tn // CONV_CHUNK),
        compiler_params=_cparams("arbitrary", "arbitrary"),
        name="rms_matmul_conv",
    )(x, g.reshape(1, k), w, conv_w, conv_b.reshape(1, n))


def _rms_matmul_gated_kernel(x_ref, g_ref, wz_ref, wbcx_ref, ws_ref, cw_ref, z_ref, ya_ref, os_ref, ext_ref,
                             *, tiles_per_seq):
    tm = x_ref.shape[0]
    starts_sequence = pl.program_id(0) % tiles_per_seq == 0

    @pl.when(starts_sequence)
    def _():
        ext_ref[0:HALO, :] = jnp.zeros((HALO, SC_DIM), F32)

    @pl.when(jnp.logical_not(starts_sequence))
    def _():
        ext_ref[0:HALO, :] = ext_ref[tm:tm + HALO, :]

    xn = _rms(x_ref[...], g_ref[...]).astype(BF16)
    z_ref[...] = jnp.dot(xn, wz_ref[...], preferred_element_type=F32)
    os_ref[...] = jnp.dot(xn, ws_ref[...], preferred_element_type=F32)
    bcx = jnp.dot(xn, wbcx_ref[...], preferred_element_type=F32)
    ext_ref[HALO:, :] = bcx[:, SC_DIM:2 * SC_DIM] * bcx[:, 2 * SC_DIM:]
    ya_ref[...] = (bcx[:, :SC_DIM] * _causal_conv(ext_ref, cw_ref, tm)).astype(ya_ref.dtype)


def rms_matmul_gated(x, g, w_z, w_bcx, w_small, conv_w, seq_len, tm=1024):
    m, k = x.shape
    const = lambda a: pl.BlockSpec(a.shape, lambda i: (0,) * a.ndim)
    rows = lambda w: pl.BlockSpec((tm, w), lambda i: (i, 0))
    g = g.reshape(1, k)
    return pl.pallas_call(
        functools.partial(_rms_matmul_gated_kernel, tiles_per_seq=seq_len // tm),
        grid=(m // tm,),
        in_specs=[rows(k), const(g), const(w_z), const(w_bcx), const(w_small), const(conv_w)],
        out_specs=[rows(w_z.shape[1]), rows(SC_DIM), rows(LANES)],
        out_shape=[jax.ShapeDtypeStruct((m, w_z.shape[1]), F32), jax.ShapeDtypeStruct((m, SC_DIM), BF16),
                   jax.ShapeDtypeStruct((m, LANES), F32)],
        scratch_shapes=[pltpu.VMEM((tm + HALO, SC_DIM), F32)],
        compiler_params=_cparams("arbitrary"),
        name="rms_matmul_gated",
    )(x, g, w_z, w_bcx, w_small, conv_w)


def _ssd_kernel(xbc_ref, z_ref, dt_ref, dtt_ref, dtb_r_ref, dtb_c_ref,
                alog_r_ref, alog_c_ref, d_ref, nw_ref, tri_ref, trit_ref,
                o_ref, s_ref):
    q = SSD_CHUNK

    @pl.when(pl.program_id(1) == 0)
    def _():
        s_ref[...] = jnp.zeros_like(s_ref)

    for sub in range(xbc_ref.shape[0] // q):
        rows = slice(sub * q, (sub + 1) * q)
        _ssd_chunk(_silu(xbc_ref[rows, :]), z_ref[rows, :], dt_ref[rows, :], dtt_ref[:, rows], dtb_r_ref, dtb_c_ref,
                   alog_r_ref, alog_c_ref, d_ref, nw_ref, tri_ref, trit_ref, o_ref.at[rows, :], s_ref)


def _ssd_chunk(xbc, z, dt_raw, dtt_raw, dtb_r_ref, dtb_c_ref, alog_r_ref, alog_c_ref, d_ref, nw_ref, tri_ref,
               trit_ref, o_ref, s_ref):
    q = SSD_CHUNK
    hpg = SSM_HEADS // SSM_GROUPS
    gw = hpg * SSM_HEAD_DIM
    xs = xbc[:, :SSM_INNER]
    bm = xbc[:, SSM_INNER:SSM_INNER + SSM_GROUPS * SSM_STATE]
    cm = xbc[:, SSM_INNER + SSM_GROUPS * SSM_STATE:]

    dt = _softplus(dt_raw + dtb_r_ref[...])
    acs = _mm_sel_lhs(tri_ref[...], dt * -jnp.exp(alog_r_ref[...]))
    dtt = _softplus(dtt_raw + dtb_c_ref[...])
    acst = _mm_sel_rhs(dtt * -jnp.exp(alog_c_ref[...]), trit_ref[...])
    dt_full = _spread_heads(dt, 0, SSM_HEADS, SSM_HEAD_DIM)
    acs_full = _spread_heads(acs, 0, SSM_HEADS, SSM_HEAD_DIM)
    acs_col = _spread_heads(acs, 0, SSM_HEADS, q)

    xdt = xs * dt_full
    acs_last = acs_full[q - 1:q, :]
    xw = xdt * jnp.exp(acs_last - acs_full)
    chunk_decay = jnp.exp(acs_last)

    row = lax.broadcasted_iota(jnp.int32, (q, q), 0)
    col = lax.broadcasted_iota(jnp.int32, (q, q), 1)
    causal = row >= col
    lane = lax.broadcasted_iota(jnp.int32, (q, 2 * SSM_HEAD_DIM), 1)

    y_diag, y_off = [], []
    for g in range(SSM_GROUPS):
        bm_g = bm[:, g * SSM_STATE:(g + 1) * SSM_STATE]
        cm_g = cm[:, g * SSM_STATE:(g + 1) * SSM_STATE]
        cb_g = _mm_nt(cm_g, bm_g)
        state = s_ref[g]
        y_off.append(_mm(cm_g, state))
        s_ref[g] = state * chunk_decay[:, g * gw:(g + 1) * gw] + _mm(bm_g.T, xw[:, g * gw:(g + 1) * gw])
        for pair in range(hpg // 2):
            h0 = g * hpg + 2 * pair
            xdt_pair = xdt[:, h0 * SSM_HEAD_DIM:(h0 + 2) * SSM_HEAD_DIM]
            weights = []
            for h in (h0, h0 + 1):
                seg = acs_col[:, h * q:(h + 1) * q] - acst[h:h + 1, :]
                weights.append(cb_g * jnp.where(causal, jnp.exp(seg), 0.0))
            both = _mm(jnp.concatenate(weights, axis=0), xdt_pair)
            y_diag.append(jnp.where(lane < SSM_HEAD_DIM, both[:q], both[q:]))
    y = (jnp.concatenate(y_diag, axis=1) + jnp.concatenate(y_off, axis=1) * jnp.exp(acs_full)
         + xs * d_ref[...])
    y = y * _silu(z)
    halves = []
    for g in range(SSM_GROUPS):
        yg = y[:, g * gw:(g + 1) * gw]
        halves.append(yg * lax.rsqrt(jnp.mean(yg * yg, axis=-1, keepdims=True) + EPS))
    o_ref[...] = (jnp.concatenate(halves, axis=1) * nw_ref[...]).astype(o_ref.dtype)


def _pad_lanes(v, fill=0.0):
    return jnp.pad(v.astype(F32), (0, LANES - v.shape[0]), constant_values=fill).reshape(1, LANES)


def _pad_col(v, rows=16):
    return jnp.pad(v.astype(F32), (0, rows - v.shape[0])).reshape(rows, 1)


def ssd_mixer(xbc3, y3, small3, smallt, dt_bias, a_log, d_skip, norm_w):
    bsz, length, _ = y3.shape
    q = SSD_CHUNK
    tri = jnp.asarray(np.tril(np.ones((q, q), np.float32)), BF16)
    trit = jnp.asarray(np.triu(np.ones((q, q), np.float32)), BF16)
    d_full = jnp.repeat(d_skip.astype(F32), SSM_HEAD_DIM).reshape(1, SSM_INNER)
    const = lambda a: pl.BlockSpec(a.shape, lambda b, c: (0,) * a.ndim)
    args = [_pad_lanes(dt_bias), _pad_col(dt_bias), _pad_lanes(a_log),
            _pad_col(a_log), d_full, norm_w.reshape(1, -1), tri, trit]
    rows = min(SSD_STEP_ROWS, length)
    return pl.pallas_call(
        _ssd_kernel,
        grid=(bsz, length // rows),
        in_specs=[
            pl.BlockSpec((None, rows, SSM_XBC), lambda b, c: (b, c, 0)),
            pl.BlockSpec((None, rows, SSM_INNER), lambda b, c: (b, c, 0)),
            pl.BlockSpec((None, rows, LANES), lambda b, c: (b, c, 0)),
            pl.BlockSpec((None, 16, rows), lambda b, c: (b, 0, c)),
        ] + [const(a) for a in args],
        out_specs=pl.BlockSpec((None, rows, SSM_INNER), lambda b, c: (b, c, 0)),
        out_shape=jax.ShapeDtypeStruct((bsz, length, SSM_INNER), BF16),
        scratch_shapes=[pltpu.VMEM((SSM_GROUPS, SSM_STATE, SSM_INNER // SSM_GROUPS), F32)],
        compiler_params=_cparams("parallel", "arbitrary"),
        name="ssd_mixer",
    )(xbc3, y3, small3, smallt, *args)


def _unit_lower_inverse(mats, row, col):
    eye = jnp.where(row == col, 1.0, 0.0)
    blk = lambda n: (row >> (n.bit_length() - 1)) == (col >> (n.bit_length() - 1))
    size = row.shape[0]
    p = [jnp.where(blk(16), -a, 0.0) for a in mats]
    t = [eye + x for x in p]
    p = [_mm(x, x) for x in p]
    for _ in range(2):
        both = [_mm(jnp.concatenate([x, y], axis=0), x) for x, y in zip(p, t)]
        p = [b[:size] for b in both]
        t = [y + b[size:] for y, b in zip(t, both)]
    t = [y + _mm(y, x) for y, x in zip(t, p)]
    for n in (16, 32):
        band = blk(2 * n) & jnp.logical_not(blk(n))
        left = [_mm(y, jnp.where(band, a, 0.0)) for y, a in zip(t, mats)]
        t = [y - _mm(x, y) for y, x in zip(t, left)]
    return t


def _gdn_kernel(qkv_ref, z_ref, ab_ref, abt_ref, dtb_r_ref, dtb_c_ref, alog_r_ref,
                alog_c_ref, nw_ref, tri_ref, trit_ref, o_ref, s_ref):
    n = GDN_TILE

    @pl.when(pl.program_id(1) == 0)
    def _():
        s_ref[...] = jnp.zeros_like(s_ref)

    for sub in range(qkv_ref.shape[0] // n):
        rows = slice(sub * n, (sub + 1) * n)
        _gdn_tile(_silu(qkv_ref[rows, :]), z_ref[rows, :], ab_ref[rows, :], abt_ref[:, rows], dtb_r_ref, dtb_c_ref,
                  alog_r_ref, alog_c_ref, nw_ref, tri_ref, trit_ref, o_ref.at[rows, :], s_ref)


def _gdn_tile(qkv, z, ab, abt, dtb_r_ref, dtb_c_ref, alog_r_ref, alog_c_ref, nw_ref, tri_ref, trit_ref, o_ref, s_ref):
    n = GDN_TILE
    c = GDN_CHUNK
    d = GDN_D
    hd = GDN_HEADS * d
    g = -jnp.exp(alog_r_ref[...]) * _softplus(ab + dtb_r_ref[...])
    gc_full = _spread_heads(_mm_sel_lhs(tri_ref[...], g), 0, GDN_HEADS, d)
    beta_full = _spread_heads(jax.nn.sigmoid(ab), GDN_HEADS, GDN_HEADS, d)
    gt = -jnp.exp(alog_c_ref[...]) * _softplus(abt + dtb_c_ref[...])
    gct = _mm_sel_rhs(gt, trit_ref[...])

    row = lax.broadcasted_iota(jnp.int32, (n, n), 0)
    col = lax.broadcasted_iota(jnp.int32, (n, n), 1)
    same = (row >> (c.bit_length() - 1)) == (col >> (c.bit_length() - 1))
    incl = same & (row >= col)
    strict = same & (row > col)
    zeros_half = jnp.zeros((c, d), F32)

    heads = range(GDN_HEADS)
    sl = [slice(h * d, (h + 1) * d) for h in heads]
    l2n = lambda x: x * lax.rsqrt(jnp.sum(x * x, axis=-1, keepdims=True) + EPS)
    qn = [l2n(qkv[:, sl[h]]) * (d ** -0.5) for h in heads]
    kn = [l2n(qkv[:, hd + h * d:hd + (h + 1) * d]) for h in heads]
    vh = [qkv[:, 2 * hd + h * d:2 * hd + (h + 1) * d] for h in heads]
    gcol = [gc_full[:, sl[h]] for h in heads]
    beta = [beta_full[:, sl[h]] for h in heads]
    edec = [jnp.exp(gcol[h] - gct[h:h + 1, :]) for h in heads]
    egc = [jnp.exp(x) for x in gcol]
    kb = [kn[h] * beta[h] for h in heads]
    on_k = [_mm_nt(jnp.concatenate([kb[h], qn[h]], axis=0), kn[h]) for h in heads]
    lower = [jnp.where(strict, on_k[h][:n] * edec[h], 0.0) for h in heads]
    aqk = [jnp.where(incl, on_k[h][n:] * edec[h], 0.0) for h in heads]
    tinv = _unit_lower_inverse(lower, row, col)
    sol = [_mm(tinv[h], jnp.concatenate([vh[h] * beta[h], kb[h] * egc[h]], axis=1)) for h in heads]
    qd = [qn[h] * egc[h] for h in heads]
    glast = [(gcol[h][c - 1:c, :], gcol[h][n - 1:n, :]) for h in heads]
    kdt = [(kn[h] * jnp.exp(jnp.concatenate([jnp.broadcast_to(glast[h][0], (c, d)),
                                             jnp.broadcast_to(glast[h][1], (c, d))], axis=0) - gcol[h])).T
           for h in heads]
    s0 = [s_ref[h] for h in heads]
    on_s0 = [_mm(jnp.concatenate([sol[h][:c, d:], qd[h][:c]], axis=0), s0[h]) for h in heads]
    v0 = [sol[h][:c, :d] - on_s0[h][:c] for h in heads]
    s1 = [s0[h] * jnp.exp(glast[h][0]) + _mm(kdt[h], jnp.concatenate([v0[h], zeros_half], axis=0)) for h in heads]
    on_s1 = [_mm(jnp.concatenate([sol[h][c:, d:], qd[h][c:]], axis=0), s1[h]) for h in heads]
    v1 = [sol[h][c:, :d] - on_s1[h][:c] for h in heads]
    for h in heads:
        s_ref[h] = s1[h] * jnp.exp(glast[h][1]) + _mm(kdt[h], jnp.concatenate([zeros_half, v1[h]], axis=0))
    outs = []
    for h in heads:
        o = (jnp.concatenate([on_s0[h][c:], on_s1[h][c:]], axis=0)
             + _mm(aqk[h], jnp.concatenate([v0[h], v1[h]], axis=0)))
        o = o * lax.rsqrt(jnp.mean(o * o, axis=-1, keepdims=True) + EPS) * nw_ref[...]
        outs.append(o * _silu(z[:, sl[h]]))
    o_ref[...] = jnp.concatenate(outs, axis=1).astype(o_ref.dtype)


def gated_deltanet_mixer(qkv3, y3, small3, smallt, dt_bias, a_log, norm_w):
    bsz, length, _ = y3.shape
    n = GDN_TILE
    hd = GDN_HEADS * GDN_D
    idx = np.arange(n)
    same = (idx[:, None] // GDN_CHUNK) == (idx[None, :] // GDN_CHUNK)
    tri = jnp.asarray(same & (idx[:, None] >= idx[None, :]), BF16)
    trit = jnp.asarray(same & (idx[:, None] <= idx[None, :]), BF16)
    const = lambda a: pl.BlockSpec(a.shape, lambda b, c: (0,) * a.ndim)
    args = [_pad_lanes(dt_bias), _pad_col(dt_bias), _pad_lanes(a_log), _pad_col(a_log),
            norm_w.reshape(1, -1), tri, trit]
    rows = min(GDN_STEP_ROWS, length)
    return pl.pallas_call(
        _gdn_kernel,
        grid=(bsz, length // rows),
        in_specs=[
            pl.BlockSpec((None, rows, 3 * hd), lambda b, c: (b, c, 0)),
            pl.BlockSpec((None, rows, hd), lambda b, c: (b, c, 0)),
            pl.BlockSpec((None, rows, LANES), lambda b, c: (b, c, 0)),
            pl.BlockSpec((None, 16, rows), lambda b, c: (b, 0, c)),
        ] + [const(a) for a in args],
        out_specs=pl.BlockSpec((None, rows, hd), lambda b, c: (b, c, 0)),
        out_shape=jax.ShapeDtypeStruct((bsz, length, hd), BF16),
        scratch_shapes=[pltpu.VMEM((GDN_HEADS, GDN_D, GDN_D), F32)],
        compiler_params=_cparams("parallel", "arbitrary"),
        name="gated_deltanet",
    )(qkv3, y3, small3, smallt, *args)


def _sb_kernel(q_ref, k_ref, v_ref, upper_ref, o_ref):
    blk = SB_BLOCK
    pair_w = 2 * SB_HEAD_DIM
    n_pairs = SB_STEP_HEADS // 2
    i = pl.program_id(2)
    q = q_ref[...] * (SB_HEAD_DIM ** -0.5)
    lane = lax.broadcasted_iota(jnp.int32, (blk, pair_w), 1)
    first_head = lane < SB_HEAD_DIM
    qs = []
    for p in range(n_pairs):
        q2 = q[:, p * pair_w:(p + 1) * pair_w]
        qs += [jnp.where(first_head, q2, 0.0).astype(BF16), jnp.where(first_head, 0.0, q2).astype(BF16)]
    row = lax.broadcasted_iota(jnp.int32, (blk, blk), 0)
    col = lax.broadcasted_iota(jnp.int32, (blk, blk), 1)
    earlier = col < row
    upper = upper_ref[...]
    heads = range(SB_STEP_HEADS)

    def local_part(kb, diagonal, exists=None):
        start = pl.multiple_of(kb * blk, blk)
        k = k_ref[pl.ds(start, blk), :].astype(BF16)
        v = v_ref[pl.ds(start, blk), :].astype(BF16)
        kp = [k[:, p * pair_w:(p + 1) * pair_w] for p in range(n_pairs)]
        vp = [v[:, p * pair_w:(p + 1) * pair_w] for p in range(n_pairs)]
        logits = [lax.dot_general(qs[h], kp[h // 2], (((1,), (1,)), ((), ())), preferred_element_type=F32)
                  for h in heads]
        keep = earlier if diagonal else None
        if exists is not None:
            keep = exists if keep is None else keep & exists
        log_keep = [-_softplus(x) for x in logits]
        if keep is not None:
            log_keep = [jnp.where(keep, x, 0.0) for x in log_keep]
        inside = [_mm_sel_rhs(x, upper, 2) for x in log_keep]
        totals = [jnp.sum(x, axis=-1, keepdims=True) for x in log_keep]
        return logits, log_keep, inside, totals, vp, keep

    def carried_part(local, accs, sticks):
        logits, log_keep, inside, totals, vp, keep = local
        w = [jnp.exp(logits[h] + log_keep[h] + inside[h] + sticks[h]) for h in heads]
        if keep is not None:
            w = [jnp.where(keep, x, 0.0) for x in w]
        pv = [jnp.dot(w[h].astype(BF16), vp[h // 2], preferred_element_type=F32) for h in heads]
        accs = tuple(accs[p] + jnp.where(first_head, pv[2 * p], pv[2 * p + 1]) for p in range(n_pairs))
        sticks = tuple(sticks[h] + totals[h] for h in heads)
        return accs, sticks

    accs = tuple(jnp.zeros((blk, pair_w), F32) for _ in range(n_pairs))
    sticks = tuple(jnp.zeros((blk, 1), F32) for _ in heads)
    eager = [local_part(i, True)]
    for back in range(1, SB_EAGER_BLOCKS + 1):
        eager.append(local_part(jnp.maximum(i - back, 0), False, exists=(row >= 0) & (i - back >= 0)))
    for local in eager:
        accs, sticks = carried_part(local, accs, sticks)

    def alive(state):
        kb, _, sticks = state
        longest = sticks[0]
        for s in sticks[1:]:
            longest = jnp.maximum(longest, s)
        return (kb >= 0) & (jnp.max(longest) > SB_LOG_ZERO)

    def body(state):
        kb, accs, sticks = state
        accs, sticks = carried_part(local_part(kb, False), accs, sticks)
        return kb - 1, accs, sticks

    _, accs, _ = lax.while_loop(alive, body, (i - 1 - SB_EAGER_BLOCKS, accs, sticks))
    o_ref[...] = jnp.concatenate(accs, axis=1).astype(o_ref.dtype)


def stick_breaking_mixer(y3, col0):
    bsz, length, _ = y3.shape
    blk = SB_BLOCK
    step_w = SB_STEP_HEADS * SB_HEAD_DIM
    steps = SB_DIM // step_w
    q0 = col0 // step_w
    idx = np.arange(blk)
    upper = jnp.asarray(idx[:, None] > idx[None, :], BF16)
    resident = lambda off: pl.BlockSpec((None, length, step_w), lambda b, p, i: (b, 0, q0 + off + p),
                                        pipeline_mode=pl.Buffered(1))
    return pl.pallas_call(
        _sb_kernel,
        grid=(bsz, steps, length // blk),
        in_specs=[
            pl.BlockSpec((None, blk, step_w), lambda b, p, i: (b, i, q0 + p)),
            resident(steps),
            resident(2 * steps),
            pl.BlockSpec((blk, blk), lambda b, p, i: (0, 0)),
        ],
        out_specs=pl.BlockSpec((None, blk, step_w), lambda b, p, i: (b, i, p)),
        out_shape=jax.ShapeDtypeStruct((bsz, length, SB_DIM), BF16),
        compiler_params=_cparams("parallel", "parallel", "arbitrary"),
        name="stick_breaking",
    )(y3, y3, y3, upper)


def _mixer_out(a_ref, b_ref, h_ref, wa_ref, wb_ref, rows):
    return h_ref[rows, :] + (jnp.dot(a_ref[rows, :].astype(BF16), wa_ref[...], preferred_element_type=F32)
                             + jnp.dot(b_ref[rows, :].astype(BF16), wb_ref[...], preferred_element_type=F32))


def _cross_attention(h, g_ref, wq_ref, kt_ref, v_ref, wo_ref):
    u = _rms(h, g_ref[...]).astype(BF16)
    q = jnp.dot(u, wq_ref[...], preferred_element_type=F32)
    heads = []
    for hd in range(XA_HEADS):
        sl = slice(hd * XA_HEAD_DIM, (hd + 1) * XA_HEAD_DIM)
        s = jnp.dot(q[:, sl].astype(BF16), kt_ref[sl, :], preferred_element_type=F32)
        p = jnp.exp(s - jnp.max(s, axis=-1, keepdims=True))
        p = p * (1.0 / jnp.sum(p, axis=-1, keepdims=True))
        heads.append(jnp.dot(p.astype(BF16), v_ref[:, sl], preferred_element_type=F32))
    o = jnp.concatenate(heads, axis=1).astype(BF16)
    return h + jnp.dot(o, wo_ref[...], preferred_element_type=F32)


def _route(xn, whi_ref, wlo_ref, b_ref, before_ref, run_ref):
    x_hi = xn.astype(BF16)
    x_lo = (xn - x_hi.astype(F32)).astype(BF16)
    wide = jnp.dot(x_hi, jnp.concatenate([whi_ref[...], wlo_ref[...]], axis=1), preferred_element_type=F32)
    logits = (wide[:, :LANES] + jnp.dot(x_lo, whi_ref[...], preferred_element_type=F32)
              + wide[:, LANES:] + b_ref[...])
    lane = lax.broadcasted_iota(jnp.int32, logits.shape, 1).astype(F32)
    neg = -1e30
    none = float(LANES)

    def top(vals):
        best = jnp.max(vals, axis=-1, keepdims=True)
        where = jnp.min(jnp.where(vals == best, lane, none), axis=-1, keepdims=True)
        return best, where

    gl = jnp.where(lane < MOE_GROUPS, logits, neg)
    gbest, gsel = top(gl)
    gprob = 1.0 / jnp.sum(jnp.exp(gl - gbest), axis=-1, keepdims=True)
    lo = MOE_GROUPS + gsel * MOE_PER_GROUP
    el = jnp.where((lane >= lo) & (lane < lo + MOE_PER_GROUP), logits, neg)
    m1, i1 = top(el)
    m2, i2 = top(jnp.where(lane == i1, neg, el))
    e = jnp.exp(m2 - m1)
    gate1 = gprob / (1.0 + e)
    gate2 = gprob * e / (1.0 + e)

    hot1 = lane == i1
    hot2 = lane == i2
    one1 = jnp.where(hot1, 1.0, 0.0)
    one2 = jnp.where(hot2, 1.0, 0.0)
    prefix = jnp.dot(before_ref[...], jnp.concatenate([one1, one2], axis=1).astype(BF16), preferred_element_type=F32)
    prefix1, prefix2 = prefix[:, :LANES], prefix[:, LANES:]
    total1 = jnp.sum(one1, axis=0, keepdims=True)
    running = run_ref[...]
    rank1 = jnp.sum(jnp.where(hot1, prefix1 + running, 0.0), axis=-1, keepdims=True)
    rank2 = jnp.sum(jnp.where(hot2, prefix2 + (running + total1), 0.0), axis=-1, keepdims=True)
    running = running + total1 + jnp.sum(one2, axis=0, keepdims=True)
    run_ref[...] = running

    fields = (i1 - MOE_GROUPS, i2 - MOE_GROUPS, gate1, gate2, rank1, rank2)
    out = jnp.zeros_like(logits)
    for k, val in enumerate(fields):
        out = jnp.where(lane == k, val, out)
    return out


def _post_mixer_kernel(a_ref, b_ref, h_ref, wa_ref, wb_ref, gxa_ref, wq_ref, kt_ref, v_ref, wo_ref,
                       gffn_ref, whi_ref, wlo_ref, bias_ref, before_ref,
                       h_out_ref, xn_ref, r_ref, cnt_ref, run_ref):
    @pl.when(pl.program_id(0) == 0)
    def _():
        run_ref[...] = jnp.zeros_like(run_ref)

    h = _mixer_out(a_ref, b_ref, h_ref, wa_ref, wb_ref, slice(None))
    h = _cross_attention(h, gxa_ref, wq_ref, kt_ref, v_ref, wo_ref)
    h_out_ref[...] = h
    xn = _rms(h, gffn_ref[...])
    xn_ref[...] = _pack_halves(xn)
    group = before_ref.shape[0]
    for start in range(0, h_ref.shape[0], group):
        rows = slice(start, start + group)
        r_ref[rows, :] = _route(xn[rows, :], whi_ref, wlo_ref, bias_ref, before_ref, run_ref)
    cnt_ref[...] = run_ref[...]


def post_mixer(ya, yb, h, wa, wb, g_xa, wq, kt, v, wo, g_ffn, w_hi, w_lo, bias, tm=1024):
    m, d = h.shape
    tiles_per_batch = m // kt.shape[0] // tm
    idx = np.arange(min(POST_GROUP_ROWS, tm))
    before = jnp.asarray(idx[:, None] > idx[None, :], BF16)
    rows = lambda w: pl.BlockSpec((tm, w), lambda i: (i, 0))
    const = lambda a: pl.BlockSpec(a.shape, lambda i: (0,) * a.ndim, pipeline_mode=pl.Buffered(1))
    per_batch = lambda a: pl.BlockSpec((None,) + a.shape[1:], lambda i: (i // tiles_per_batch, 0, 0))
    g_xa, g_ffn = g_xa.reshape(1, d), g_ffn.reshape(1, d)
    return pl.pallas_call(
        _post_mixer_kernel,
        grid=(m // tm,),
        in_specs=[rows(ya.shape[1]), rows(yb.shape[1]), rows(d), const(wa), const(wb), const(g_xa), const(wq),
                  per_batch(kt), per_batch(v), const(wo), const(g_ffn), const(w_hi), const(w_lo), const(bias),
                  const(before)],
        out_specs=[rows(d), rows(d // 2), rows(LANES), pl.BlockSpec((1, LANES), lambda i: (0, 0))],
        out_shape=[jax.ShapeDtypeStruct((m, d), F32), jax.ShapeDtypeStruct((m, d // 2), jnp.int32),
                   jax.ShapeDtypeStruct((m, LANES), F32), jax.ShapeDtypeStruct((1, LANES), F32)],
        scratch_shapes=[pltpu.VMEM((1, LANES), F32)],
        compiler_params=_cparams("arbitrary"),
        name="post_mixer",
    )(ya, yb, h, wa, wb, g_xa, wq, kt, v, wo, g_ffn, w_hi, w_lo, bias, before)


def _expert_kernel(table_ref, x_ref, wg_hbm, wu_hbm, wd_hbm, o_ref,
                   wg32_ref, wu32_ref, wd32_ref, wgb_ref, wub_ref, wdb_ref, sem_ref, *, layer):
    i = pl.program_id(0)
    beid_ref, valid_ref, first_ref, slot_ref, next_ref = (table_ref.at[k] for k in range(5))
    valid = valid_ref[i]

    def weight_copies(expert, slot):
        return (pltpu.make_async_copy(wg_hbm.at[layer, expert], wg32_ref.at[slot], sem_ref.at[slot, 0]),
                pltpu.make_async_copy(wu_hbm.at[layer, expert], wu32_ref.at[slot], sem_ref.at[slot, 1]),
                pltpu.make_async_copy(wd_hbm.at[layer, expert], wd32_ref.at[slot], sem_ref.at[slot, 2]))

    @pl.when(i == 0)
    def _():
        for copy in weight_copies(beid_ref[0], 0):
            copy.start()

    @pl.when(first_ref[i] == 1)
    def _():
        slot = slot_ref[i]
        for copy in weight_copies(beid_ref[i], slot):
            copy.wait()
        wgb_ref[...] = wg32_ref[slot].astype(BF16)
        wub_ref[...] = wu32_ref[slot].astype(BF16)
        wdb_ref[...] = wd32_ref[slot].astype(BF16)

        @pl.when(next_ref[i] >= 0)
        def _():
            for copy in weight_copies(next_ref[i], 1 - slot):
                copy.start()

    half = MOE_ROWS // 2

    def ffn(n_halves):
        row = lax.broadcasted_iota(jnp.int32, (half, 2 * x_ref.shape[1]), 0)
        xs = [jnp.where(row + k * half < valid, _unpack_halves(x_ref[k * half:(k + 1) * half, :]), 0.0).astype(BF16)
              for k in range(n_halves)]
        gates = [jnp.dot(x, wgb_ref[...], preferred_element_type=F32) for x in xs]
        ups = [jnp.dot(x, wub_ref[...], preferred_element_type=F32) for x in xs]
        acts = [(_silu(g) * u).astype(BF16) for g, u in zip(gates, ups)]
        for k, act in enumerate(acts):
            o_ref[k * half:(k + 1) * half, :] = _pack_halves(jnp.dot(act, wdb_ref[...], preferred_element_type=F32))

    @pl.when(valid > half)
    def _():
        ffn(2)

    @pl.when((valid > 0) & (valid <= half))
    def _():
        ffn(1)
        o_ref[half:, :] = jnp.zeros((half, o_ref.shape[1]), o_ref.dtype)

    @pl.when(valid == 0)
    def _():
        o_ref[...] = jnp.zeros_like(o_ref)


def moe_experts(blocks, xs, w_gate, w_up, w_down, layer):
    n_slots, packed = xs.shape
    d = 2 * packed
    rows = MOE_ROWS
    ff = w_gate.shape[3]
    grid_spec = pltpu.PrefetchScalarGridSpec(
        num_scalar_prefetch=1,
        grid=(n_slots // rows,),
        in_specs=[
            pl.BlockSpec((rows, packed), lambda i, *_: (i, 0)),
            pl.BlockSpec(memory_space=pl.ANY),
            pl.BlockSpec(memory_space=pl.ANY),
            pl.BlockSpec(memory_space=pl.ANY),
        ],
        out_specs=pl.BlockSpec((rows, packed), lambda i, *_: (i, 0)),
        scratch_shapes=[pltpu.VMEM((2, d, ff), F32), pltpu.VMEM((2, d, ff), F32), pltpu.VMEM((2, ff, d), F32),
                        pltpu.VMEM((d, ff), BF16), pltpu.VMEM((d, ff), BF16), pltpu.VMEM((ff, d), BF16),
                        pltpu.SemaphoreType.DMA((2, 3))],
    )
    return pl.pallas_call(
        functools.partial(_expert_kernel, layer=layer),
        grid_spec=grid_spec,
        out_shape=jax.ShapeDtypeStruct((n_slots, packed), jnp.int32),
        compiler_params=_cparams("arbitrary"),
        name="moe_experts",
    )(blocks, xs, w_gate, w_up, w_down)


def _sc_mesh():
    return plsc.VectorSubcoreMesh(core_axis_name="c", subcore_axis_name="s",
                                  num_cores=SC_CORES, num_subcores=SC_SUBCORES)


def _sc_worker():
    return lax.axis_index("s") * SC_CORES + lax.axis_index("c")


def _sc_double_buffered(n_chunks, fetch, drain):
    assert n_chunks % 2 == 0
    start = lambda copies: [c.start() for c in copies]
    wait = lambda copies: [c.wait() for c in copies]
    start(fetch(0, 0))

    @pl.loop(0, n_chunks, step=2)
    def _(j):
        wait(fetch(j, 0))

        @pl.when(j > 0)
        def _():
            wait(drain(j - 1, 1))

        start(fetch(j + 1, 1))
        start(drain(j, 0))
        wait(fetch(j + 1, 1))
        wait(drain(j, 0))

        @pl.when(j + 2 < n_chunks)
        def _():
            start(fetch(j + 2, 0))

        start(drain(j + 1, 1))

    wait(drain(n_chunks - 1, 1))


def sc_scatter_rows(x, dest, n_slots):
    n_tok, d = x.shape
    per_worker = n_tok // SC_WORKERS
    n_chunks = per_worker // SC_CHUNK
    by_worker = dest.reshape(dest.shape[0] * SC_WORKERS, n_chunks, SC_CHUNK)

    @functools.partial(
        pl.kernel, mesh=_sc_mesh(), out_type=jax.ShapeDtypeStruct((n_slots, d), x.dtype),
        scratch_types=[pltpu.VMEM((n_chunks, SC_CHUNK), jnp.int32), pltpu.VMEM((n_chunks, SC_CHUNK), jnp.int32),
                       pltpu.VMEM((2, SC_CHUNK, d), x.dtype), pltpu.SemaphoreType.DMA((2, 3))],
        name="moe_scatter_rows")
    def scatter(x_hbm, dest_hbm, out_hbm, i0_v, i1_v, rows_v, sem):
        wid = _sc_worker()
        pltpu.sync_copy(dest_hbm.at[wid], i0_v)
        pltpu.sync_copy(dest_hbm.at[SC_WORKERS + wid], i1_v)

        def fetch(j, buf):
            start = pl.multiple_of(wid * per_worker + j * SC_CHUNK, SC_CHUNK)
            return [pltpu.make_async_copy(x_hbm.at[pl.ds(start, SC_CHUNK)], rows_v.at[buf], sem.at[buf, 0])]

        def drain(j, buf):
            return [pltpu.make_async_copy(rows_v.at[buf], out_hbm.at[i0_v.at[j]], sem.at[buf, 1]),
                    pltpu.make_async_copy(rows_v.at[buf], out_hbm.at[i1_v.at[j]], sem.at[buf, 2])]

        _sc_double_buffered(n_chunks, fetch, drain)

    return scatter(x, by_worker)


def sc_gather_rows(table, idx, n_out):
    d = table.shape[1]
    per_worker = n_out // SC_WORKERS
    n_chunks = per_worker // SC_CHUNK

    @functools.partial(
        pl.kernel, mesh=_sc_mesh(), out_type=jax.ShapeDtypeStruct((n_out, d), table.dtype),
        scratch_types=[pltpu.VMEM((n_chunks, SC_CHUNK), jnp.int32), pltpu.VMEM((2, SC_CHUNK, d), table.dtype),
                       pltpu.SemaphoreType.DMA((2, 2))],
        name="moe_gather_rows")
    def gather(table_hbm, idx_hbm, out_hbm, idx_v, rows_v, sem):
        wid = _sc_worker()
        pltpu.sync_copy(idx_hbm.at[wid], idx_v)

        def fetch(j, buf):
            return [pltpu.make_async_copy(table_hbm.at[idx_v.at[j]], rows_v.at[buf], sem.at[buf, 0])]

        def drain(j, buf):
            start = pl.multiple_of(wid * per_worker + j * SC_CHUNK, SC_CHUNK)
            return [pltpu.make_async_copy(rows_v.at[buf], out_hbm.at[pl.ds(start, SC_CHUNK)], sem.at[buf, 1])]

        _sc_double_buffered(n_chunks, fetch, drain)

    return gather(table, idx.reshape(-1, n_chunks, SC_CHUNK))


def _combine_kernel(h_ref, y0_ref, y1_ref, r_ref, g_ref, o_ref, *, final_norm):
    route = r_ref[...]
    h = h_ref[...] + (route[:, 2:3] * _unpack_halves(y0_ref[...]) + route[:, 3:4] * _unpack_halves(y1_ref[...]))
    o_ref[...] = _rms(h, g_ref[...]) if final_norm else h


def moe_combine(h, y01, route, g, final_norm, tm=1024):
    m, d = h.shape
    tm = min(tm, m)
    rows = lambda w: pl.BlockSpec((tm, w), lambda i: (i, 0))
    return pl.pallas_call(
        functools.partial(_combine_kernel, final_norm=final_norm),
        grid=(m // tm,),
        in_specs=[rows(d), rows(d // 2), pl.BlockSpec((tm, d // 2), lambda i: (i + m // tm, 0)), rows(LANES),
                  pl.BlockSpec((1, d), lambda i: (0, 0))],
        out_specs=rows(d),
        out_shape=jax.ShapeDtypeStruct((m, d), F32),
        compiler_params=_cparams("parallel"),
        name="moe_combine",
    )(h, y01, y01, route, g.reshape(1, d))


def _pad_cols(w):
    return jnp.pad(w, ((0, 0), (0, LANES - w.shape[1])))


def _plan_kernel(route_ref, cnt_ref, incl_ref, dest_ref, table_ref):
    f32_sum = lambda x, axis: jnp.sum(x, axis=axis, keepdims=True)
    lane = lax.broadcasted_iota(jnp.int32, (LANES, LANES), 1)
    sub = lax.broadcasted_iota(jnp.int32, (LANES, LANES), 0)
    incl = incl_ref[...]
    is_expert = (lane >= MOE_GROUPS) & (lane < MOE_GROUPS + MOE_EXPERTS)
    shift = MOE_ROWS.bit_length() - 1
    counts = jnp.broadcast_to(cnt_ref[...], (LANES, LANES)).astype(jnp.int32)
    padded = jnp.where(is_expert, ((counts + (MOE_ROWS - 1)) >> shift) << shift, 0)
    pad_end = _mm_sel_rhs(padded.astype(F32), incl)
    pad_start = pad_end - padded.astype(F32)

    route = route_ref[...]
    lane_t = lax.broadcasted_iota(jnp.int32, route.shape, 1)
    lane_f = lane_t.astype(F32)
    start_row = pad_start[0:1, :]
    slots = [f32_sum(jnp.where(lane_f == route[:, k:k + 1] + MOE_GROUPS, start_row, 0.0), 1) + route[:, 4 + k:5 + k]
             for k in range(2)]
    both = jnp.where(lane_t == 0, slots[0], jnp.where(lane_t == 1, slots[1], 0.0))
    dest_ref[...] = both.T[0:8, :].astype(jnp.int32)

    on_sub = lambda rows_equal: rows_equal.T
    expert_sub = (sub >= MOE_GROUPS) & (sub < MOE_GROUPS + MOE_EXPERTS)
    block_start = (lane * MOE_ROWS).astype(F32)
    eid = f32_sum(jnp.where(expert_sub & (on_sub(pad_end) <= block_start), 1.0, 0.0), 0)
    eid = jnp.minimum(eid, float(MOE_EXPERTS - 1))
    filled = on_sub(pad_start + counts.astype(F32))
    own = (sub - MOE_GROUPS).astype(F32) == eid
    valid = jnp.clip(f32_sum(jnp.where(own, filled, 0.0), 0) - block_start[0:1, :], 0.0, float(MOE_ROWS))
    eid_rows = jnp.broadcast_to(eid, (LANES, LANES))
    changed = (lane == 0) | (eid_rows != pltpu.roll(eid_rows, 1, axis=1))
    first = jnp.where((jnp.broadcast_to(valid, (LANES, LANES)) > 0) & changed, 1.0, 0.0)
    ordinal = _mm_sel_rhs(first, incl) - 1.0
    slot = ordinal - 2.0 * jnp.floor(ordinal * 0.5)
    later = (on_sub(first) > 0) & (sub > lane)
    nearest = jnp.min(jnp.where(later, sub, LANES), axis=0, keepdims=True)
    next_eid = f32_sum(jnp.where(sub == nearest, on_sub(eid_rows), 0.0), 0)
    next_eid = jnp.where(nearest < LANES, next_eid, -1.0)
    row8 = lax.broadcasted_iota(jnp.int32, (8, LANES), 0)
    table = jnp.zeros((8, LANES), F32)
    for k, val in enumerate((eid, valid, first[0:1, :], slot[0:1, :], next_eid)):
        table = jnp.where(row8 == k, val, table)
    table_ref[...] = table.astype(jnp.int32)


def moe_plan(route, counts, tm=4096):
    n_tok = route.shape[0]
    tm = min(tm, n_tok)
    idx = np.arange(LANES)
    incl = jnp.asarray(idx[:, None] <= idx[None, :], BF16)
    return pl.pallas_call(
        _plan_kernel,
        grid=(n_tok // tm,),
        in_specs=[pl.BlockSpec((tm, LANES), lambda i: (i, 0)), pl.BlockSpec((1, LANES), lambda i: (0, 0)),
                  pl.BlockSpec((LANES, LANES), lambda i: (0, 0))],
        out_specs=[pl.BlockSpec((8, tm), lambda i: (0, i)), pl.BlockSpec((8, LANES), lambda i: (0, 0))],
        out_shape=[jax.ShapeDtypeStruct((8, n_tok), jnp.int32), jax.ShapeDtypeStruct((8, LANES), jnp.int32)],
        compiler_params=_cparams("arbitrary"),
        name="moe_plan",
    )(route, counts, incl)


def _router_weights(w_group, b_group, w_expert, b_expert):
    w_r = _pad_cols(jnp.concatenate([w_group, w_expert], axis=1))
    w_hi = w_r.astype(BF16)
    w_lo = (w_r - w_hi.astype(F32)).astype(BF16)
    return w_hi, w_lo, _pad_lanes(jnp.concatenate([b_group, b_expert]))


def _moe_layer(h, xn, route, counts, w_gate, w_up, w_down, layer, final_g):
    n_tok, d = h.shape
    n_blocks = -(-(2 * n_tok + MOE_EXPERTS * (MOE_ROWS - 1)) // MOE_ROWS)
    dest, blocks = moe_plan(route, counts)
    xs = sc_scatter_rows(xn, dest, n_blocks * MOE_ROWS)
    ys = moe_experts(blocks, xs, w_gate, w_up, w_down, layer)
    y01 = sc_gather_rows(ys, dest, 2 * n_tok)
    g = jnp.ones((d,), F32) if final_g is None else final_g
    return moe_combine(h, y01, route, g, final_g is not None)


def _memory_kv(memn_in, mem_norm, wk, wv):
    bsz, m, d = memn_in.shape
    w = jnp.concatenate([wk, wv], axis=1).astype(BF16)
    kv, _ = rms_matmul(memn_in.reshape(bsz * m, d), mem_norm, w, jnp.zeros((d, LANES), BF16))
    k = kv[:, :d].reshape(bsz, m, d)
    v = kv[:, d:].reshape(bsz, m, d)
    return (jnp.swapaxes(k, 1, 2) * XA_HEAD_DIM ** -0.5).astype(BF16), v.astype(BF16)


def kernel(x, mem, mem_norm, final_norm, norm_mix, norm_xa, norm_ffn, xa_wq, xa_wk, xa_wv, xa_wo, moe_w_group, moe_b_group, moe_w_expert, moe_b_expert, moe_w_gate, moe_w_up, moe_w_down, ev_w_in, ev_sc_conv, ev_ssm_conv_w, ev_ssm_conv_b, ev_ssm_dt_bias, ev_ssm_a_log, ev_ssm_d, ev_ssm_norm, ev_w_out, od_w_in, od_gdn_conv, od_gdn_dt_bias, od_gdn_a_log, od_gdn_norm, od_w_out):
    bsz, length, d = x.shape
    n_tok = bsz * length
    depth = norm_mix.shape[0]
    h = x.reshape(n_tok, d)
    for layer in range(depth):
        i = layer // 2
        if layer % 2 == 0:
            w = ev_w_in[i]
            z0 = 3 * SC_DIM
            xbc0 = z0 + SSM_INNER
            w_conv = w[:, xbc0:xbc0 + SSM_XBC].astype(BF16)
            w_small = _pad_cols(w[:, xbc0 + SSM_XBC:]).astype(BF16)
            xbc = rms_matmul_conv(h, norm_mix[layer], w_conv, ev_ssm_conv_w[i], ev_ssm_conv_b[i], length)
            z, ya, small = rms_matmul_gated(h, norm_mix[layer], w[:, z0:xbc0].astype(BF16), w[:, :z0].astype(BF16),
                                            w_small, ev_sc_conv[i], length)
            small3 = small.reshape(bsz, length, LANES)
            smallt = jnp.swapaxes(small3[:, :, :16], 1, 2)
            yb = ssd_mixer(xbc.reshape(bsz, length, -1), z.reshape(bsz, length, -1), small3, smallt,
                           ev_ssm_dt_bias[i], ev_ssm_a_log[i], ev_ssm_d[i], ev_ssm_norm[i])
            w_out = ev_w_out[i].astype(BF16)
            split = SC_DIM
        else:
            w = od_w_in[i]
            qkv_w = 3 * GDN_HEADS * GDN_D
            z_end = qkv_w + GDN_HEADS * GDN_D
            w_conv = w[:, :qkv_w].astype(BF16)
            w_main = jnp.concatenate([w[:, qkv_w:z_end], w[:, z_end + 2 * GDN_HEADS:]], axis=1).astype(BF16)
            w_small = _pad_cols(w[:, z_end:z_end + 2 * GDN_HEADS]).astype(BF16)
            qkv = rms_matmul_conv(h, norm_mix[layer], w_conv, od_gdn_conv[i], jnp.zeros((qkv_w,), F32), length)
            y, small = rms_matmul(h, norm_mix[layer], w_main, w_small, tm=1024, tn=w_main.shape[1])
            y3 = y.reshape(bsz, length, -1)
            small3 = small.reshape(bsz, length, LANES)
            smallt = jnp.swapaxes(small3[:, :, :16], 1, 2)
            ya = gated_deltanet_mixer(qkv.reshape(bsz, length, -1), y3, small3, smallt, od_gdn_dt_bias[i],
                                      od_gdn_a_log[i], od_gdn_norm[i])
            yb = stick_breaking_mixer(y3, GDN_HEADS * GDN_D)
            w_out = od_w_out[i].astype(BF16)
            split = GDN_HEADS * GDN_D
        kt, v = _memory_kv(mem, mem_norm, xa_wk[layer], xa_wv[layer])
        w_hi, w_lo, bias = _router_weights(moe_w_group[layer], moe_b_group[layer], moe_w_expert[layer],
                                           moe_b_expert[layer])
        h, xn, route, counts = post_mixer(
            ya.reshape(n_tok, -1), yb.reshape(n_tok, -1), h, w_out[:split], w_out[split:], norm_xa[layer],
            xa_wq[layer].astype(BF16), kt, v, xa_wo[layer].astype(BF16), norm_ffn[layer], w_hi, w_lo, bias)
        h = _moe_layer(h, xn, route, counts, moe_w_gate, moe_w_up, moe_w_down, layer,
                       final_norm if layer == depth - 1 else None)
    return h.reshape(bsz, length, d)
```

```python
import functools

import jax
import jax.numpy as jnp
import numpy as np
from jax import lax
from jax.experimental import pallas as pl
from jax.experimental.pallas import tpu as pltpu
from jax.experimental.pallas import tpu_sc as plsc

F32 = jnp.float32
BF16 = jnp.bfloat16
EPS = 1e-6

SC_DIM = 512
SSM_HEADS = 16
SSM_HEAD_DIM = 64
SSM_INNER = 1024
SSM_GROUPS = 2
SSM_STATE = 128
SSM_XBC = SSM_INNER + 2 * SSM_GROUPS * SSM_STATE
SSD_CHUNK = 128
SSD_STEP_ROWS = 512
GDN_HEADS = 8
GDN_D = 128
GDN_CHUNK = 64
GDN_TILE = 128
GDN_STEP_ROWS = 512
SB_HEAD_DIM = 64
SB_DIM = 512
SB_BLOCK = 128
SB_STEP_HEADS = 8
SB_EAGER_BLOCKS = 2
XA_HEADS = 4
XA_HEAD_DIM = 256
MOE_GROUPS = 4
MOE_PER_GROUP = 8
MOE_EXPERTS = 32
MOE_ROWS = 1024
MOE_PART_ROWS = 256
POST_GROUP_ROWS = 512
SC_CORES = 2
SC_SUBCORES = 16
SC_WORKERS = SC_CORES * SC_SUBCORES
SC_CHUNK = 64
HALO = 8
CONV_CHUNK = 512
LANES = 128
SB_LOG_ZERO = -104.0
VMEM_LIMIT = 56 * 1024 * 1024


def _cparams(*sem):
    return pltpu.CompilerParams(dimension_semantics=sem, vmem_limit_bytes=VMEM_LIMIT)


def _mm(a, b):
    return jnp.dot(a.astype(BF16), b.astype(BF16), preferred_element_type=F32)


def _mm_nt(a, b):
    return lax.dot_general(a.astype(BF16), b.astype(BF16), (((1,), (1,)), ((), ())),
                           preferred_element_type=F32)


def _split_bf16(x, n):
    parts, r = [], x
    for _ in range(n):
        p = r.astype(BF16)
        parts.append(p)
        r = r - p.astype(F32)
    return parts


def _mm_sel_rhs(x, sel, n=3):
    return sum(jnp.dot(p, sel, preferred_element_type=F32) for p in _split_bf16(x, n))


def _mm_sel_lhs(sel, x, n=3):
    return sum(jnp.dot(sel, p, preferred_element_type=F32) for p in _split_bf16(x, n))


def _spread_heads(x, first, n_heads, width):
    rows = x.shape[0]
    col = lambda h: jnp.broadcast_to(x[:, first + h:first + h + 1], (rows, LANES))
    if width == LANES:
        return jnp.concatenate([col(h) for h in range(n_heads)], axis=1)
    left = lax.broadcasted_iota(jnp.int32, (rows, LANES), 1) < width
    return jnp.concatenate([jnp.where(left, col(h), col(h + 1)) for h in range(0, n_heads, 2)], axis=1)


def _pack_halves(x):
    n = x.shape[1] // 2
    lo = pltpu.bitcast(x[:, :n].astype(BF16).astype(F32), jnp.int32)
    hi = pltpu.bitcast(x[:, n:].astype(BF16).astype(F32), jnp.int32)
    return lax.shift_right_logical(lo, 16) | (hi & jnp.int32(-65536))


def _unpack_halves(p):
    lo = pltpu.bitcast(lax.shift_left(p, 16), F32)
    hi = pltpu.bitcast(p & jnp.int32(-65536), F32)
    return jnp.concatenate([lo, hi], axis=1)


def _silu(x):
    return x * jax.nn.sigmoid(x)


def _softplus(x):
    return jnp.maximum(x, 0.0) + jnp.log(1.0 + jnp.exp(-jnp.abs(x)))


def _rms(x, g):
    return x * lax.rsqrt(jnp.mean(x * x, axis=-1, keepdims=True) + EPS) * g


def _rms_matmul_kernel(x_ref, g_ref, w_ref, ws_ref, o_ref, os_ref):
    xn = _rms(x_ref[...], g_ref[...]).astype(BF16)
    o_ref[...] = jnp.dot(xn, w_ref[...], preferred_element_type=F32)
    os_ref[...] = jnp.dot(xn, ws_ref[...], preferred_element_type=F32)


def rms_matmul(x, g, w, ws, tm=512, tn=512):
    m, k = x.shape
    n = w.shape[1]
    tm = min(tm, m)
    main, small = pl.pallas_call(
        _rms_matmul_kernel,
        grid=(n // tn, m // tm),
        in_specs=[
            pl.BlockSpec((tm, k), lambda j, i: (i, 0)),
            pl.BlockSpec((1, k), lambda j, i: (0, 0)),
            pl.BlockSpec((k, tn), lambda j, i: (0, j)),
            pl.BlockSpec((k, LANES), lambda j, i: (0, 0)),
        ],
        out_specs=[
            pl.BlockSpec((tm, tn), lambda j, i: (i, j)),
            pl.BlockSpec((None, tm, LANES), lambda j, i: (j, i, 0)),
        ],
        out_shape=[jax.ShapeDtypeStruct((m, n), F32), jax.ShapeDtypeStruct((n // tn, m, LANES), F32)],
        compiler_params=_cparams("parallel", "parallel"),
        name="rms_matmul",
    )(x, g.reshape(1, k), w, ws)
    return main, small[0]


def _causal_conv(ext_ref, w_ref, rows):
    width = w_ref.shape[0]
    ext = ext_ref[...]
    acc = None
    for j in range(width):
        shift = width - 1 - j
        moved = ext if shift == 0 else pltpu.roll(ext, shift, axis=0)
        term = w_ref[j:j + 1, :] * moved[HALO:HALO + rows, :]
        acc = term if acc is None else acc + term
    return acc


def _rms_matmul_conv_kernel(x_ref, g_ref, w_ref, cw_ref, cb_ref, o_ref, *ext_refs, tiles_per_seq):
    tm = x_ref.shape[0]
    starts_sequence = pl.program_id(1) % tiles_per_seq == 0

    @pl.when(starts_sequence)
    def _():
        for ext_ref in ext_refs:
            ext_ref[0:HALO, :] = jnp.zeros((HALO, CONV_CHUNK), F32)

    @pl.when(jnp.logical_not(starts_sequence))
    def _():
        for ext_ref in ext_refs:
            ext_ref[0:HALO, :] = ext_ref[tm:tm + HALO, :]

    xn = _rms(x_ref[...], g_ref[...]).astype(BF16)
    for c, ext_ref in enumerate(ext_refs):
        cols = slice(c * CONV_CHUNK, (c + 1) * CONV_CHUNK)
        ext_ref[HALO:, :] = jnp.dot(xn, w_ref[:, cols], preferred_element_type=F32)
        o_ref[:, cols] = _causal_conv(ext_ref, cw_ref.at[:, cols], tm) + cb_ref[:, cols]


def rms_matmul_conv(x, g, w, conv_w, conv_b, seq_len, tm=1024, tn=1536):
    m, k = x.shape
    n = w.shape[1]
    cols = lambda rows: pl.BlockSpec((rows, tn), lambda j, i: (0, j))
    return pl.pallas_call(
        functools.partial(_rms_matmul_conv_kernel, tiles_per_seq=seq_len // tm),
        grid=(n // tn, m // tm),
        in_specs=[
            pl.BlockSpec((tm, k), lambda j, i: (i, 0)),
            pl.BlockSpec((1, k), lambda j, i: (0, 0)),
            cols(k), cols(conv_w.shape[0]), cols(1),
        ],
        out_specs=pl.BlockSpec((tm, tn), lambda j, i: (i, j)),
        out_shape=jax.ShapeDtypeStruct((m, n), F32),
        scratch_shapes=[pltpu.VMEM((tm + HALO, CONV_CHUNK), F32)] * (tn // CONV_CHUNK),
        compiler_params=_cparams("arbitrary", "arbitrary"),
        name="rms_matmul_conv",
    )(x, g.reshape(1, k), w, conv_w, conv_b.reshape(1, n))


def _rms_matmul_gated_kernel(x_ref, g_ref, wz_ref, wbcx_ref, ws_ref, cw_ref, z_ref, ya_ref, os_ref, ext_ref,
                             *, tiles_per_seq):
    tm = x_ref.shape[0]
    starts_sequence = pl.program_id(0) % tiles_per_seq == 0

    @pl.when(starts_sequence)
    def _():
        ext_ref[0:HALO, :] = jnp.zeros((HALO, SC_DIM), F32)

    @pl.when(jnp.logical_not(starts_sequence))
    def _():
        ext_ref[0:HALO, :] = ext_ref[tm:tm + HALO, :]

    xn = _rms(x_ref[...], g_ref[...]).astype(BF16)
    z_ref[...] = jnp.dot(xn, wz_ref[...], preferred_element_type=F32)
    os_ref[...] = jnp.dot(xn, ws_ref[...], preferred_element_type=F32)
    bcx = jnp.dot(xn, wbcx_ref[...], preferred_element_type=F32)
    ext_ref[HALO:, :] = bcx[:, SC_DIM:2 * SC_DIM] * bcx[:, 2 * SC_DIM:]
    ya_ref[...] = (bcx[:, :SC_DIM] * _causal_conv(ext_ref, cw_ref, tm)).astype(ya_ref.dtype)


def rms_matmul_gated(x, g, w_z, w_bcx, w_small, conv_w, seq_len, tm=1024):
    m, k = x.shape
    const = lambda a: pl.BlockSpec(a.shape, lambda i: (0,) * a.ndim)
    rows = lambda w: pl.BlockSpec((tm, w), lambda i: (i, 0))
    g = g.reshape(1, k)
    return pl.pallas_call(
        functools.partial(_rms_matmul_gated_kernel, tiles_per_seq=seq_len // tm),
        grid=(m // tm,),
        in_specs=[rows(k), const(g), const(w_z), const(w_bcx), const(w_small), const(conv_w)],
        out_specs=[rows(w_z.shape[1]), rows(SC_DIM), rows(LANES)],
        out_shape=[jax.ShapeDtypeStruct((m, w_z.shape[1]), F32), jax.ShapeDtypeStruct((m, SC_DIM), BF16),
                   jax.ShapeDtypeStruct((m, LANES), F32)],
        scratch_shapes=[pltpu.VMEM((tm + HALO, SC_DIM), F32)],
        compiler_params=_cparams("arbitrary"),
        name="rms_matmul_gated",
    )(x, g, w_z, w_bcx, w_small, conv_w)


def _ssd_kernel(xbc_ref, z_ref, dt_ref, dtt_ref, dtb_r_ref, dtb_c_ref,
                alog_r_ref, alog_c_ref, d_ref, nw_ref, tri_ref, trit_ref,
                o_ref, s_ref):
    q = SSD_CHUNK

    @pl.when(pl.program_id(1) == 0)
    def _():
        s_ref[...] = jnp.zeros_like(s_ref)

    for sub in range(xbc_ref.shape[0] // q):
        rows = slice(sub * q, (sub + 1) * q)
        _ssd_chunk(_silu(xbc_ref[rows, :]), z_ref[rows, :], dt_ref[rows, :], dtt_ref[:, rows], dtb_r_ref, dtb_c_ref,
                   alog_r_ref, alog_c_ref, d_ref, nw_ref, tri_ref, trit_ref, o_ref.at[rows, :], s_ref)


def _ssd_chunk(xbc, z, dt_raw, dtt_raw, dtb_r_ref, dtb_c_ref, alog_r_ref, alog_c_ref, d_ref, nw_ref, tri_ref,
               trit_ref, o_ref, s_ref):
    q = SSD_CHUNK
    hpg = SSM_HEADS // SSM_GROUPS
    gw = hpg * SSM_HEAD_DIM
    xs = xbc[:, :SSM_INNER]
    bm = xbc[:, SSM_INNER:SSM_INNER + SSM_GROUPS * SSM_STATE]
    cm = xbc[:, SSM_INNER + SSM_GROUPS * SSM_STATE:]

    dt = _softplus(dt_raw + dtb_r_ref[...])
    acs = _mm_sel_lhs(tri_ref[...], dt * -jnp.exp(alog_r_ref[...]))
    dtt = _softplus(dtt_raw + dtb_c_ref[...])
    acst = _mm_sel_rhs(dtt * -jnp.exp(alog_c_ref[...]), trit_ref[...])
    dt_full = _spread_heads(dt, 0, SSM_HEADS, SSM_HEAD_DIM)
    acs_full = _spread_heads(acs, 0, SSM_HEADS, SSM_HEAD_DIM)
    acs_col = _spread_heads(acs, 0, SSM_HEADS, q)

    xdt = xs * dt_full
    acs_last = acs_full[q - 1:q, :]
    xw = xdt * jnp.exp(acs_last - acs_full)
    chunk_decay = jnp.exp(acs_last)

    row = lax.broadcasted_iota(jnp.int32, (q, q), 0)
    col = lax.broadcasted_iota(jnp.int32, (q, q), 1)
    causal = row >= col
    lane = lax.broadcasted_iota(jnp.int32, (q, 2 * SSM_HEAD_DIM), 1)

    y_diag, y_off = [], []
    for g in range(SSM_GROUPS):
        bm_g = bm[:, g * SSM_STATE:(g + 1) * SSM_STATE]
        cm_g = cm[:, g * SSM_STATE:(g + 1) * SSM_STATE]
        cb_g = _mm_nt(cm_g, bm_g)
        state = s_ref[g]
        y_off.append(_mm(cm_g, state))
        s_ref[g] = state * chunk_decay[:, g * gw:(g + 1) * gw] + _mm(bm_g.T, xw[:, g * gw:(g + 1) * gw])
        for pair in range(hpg // 2):
            h0 = g * hpg + 2 * pair
            xdt_pair = xdt[:, h0 * SSM_HEAD_DIM:(h0 + 2) * SSM_HEAD_DIM]
            weights = []
            for h in (h0, h0 + 1):
                seg = acs_col[:, h * q:(h + 1) * q] - acst[h:h + 1, :]
                weights.append(cb_g * jnp.where(causal, jnp.exp(seg), 0.0))
            both = _mm(jnp.concatenate(weights, axis=0), xdt_pair)
            y_diag.append(jnp.where(lane < SSM_HEAD_DIM, both[:q], both[q:]))
    y = (jnp.concatenate(y_diag, axis=1) + jnp.concatenate(y_off, axis=1) * jnp.exp(acs_full)
         + xs * d_ref[...])
    y = y * _silu(z)
    halves = []
    for g in range(SSM_GROUPS):
        yg = y[:, g * gw:(g + 1) * gw]
        halves.append(yg * lax.rsqrt(jnp.mean(yg * yg, axis=-1, keepdims=True) + EPS))
    o_ref[...] = (jnp.concatenate(halves, axis=1) * nw_ref[...]).astype(o_ref.dtype)


def _pad_lanes(v, fill=0.0):
    return jnp.pad(v.astype(F32), (0, LANES - v.shape[0]), constant_values=fill).reshape(1, LANES)


def _pad_col(v, rows=16):
    return jnp.pad(v.astype(F32), (0, rows - v.shape[0])).reshape(rows, 1)


def ssd_mixer(xbc3, y3, small3, smallt, dt_bias, a_log, d_skip, norm_w):
    bsz, length, _ = y3.shape
    q = SSD_CHUNK
    tri = jnp.asarray(np.tril(np.ones((q, q), np.float32)), BF16)
    trit = jnp.asarray(np.triu(np.ones((q, q), np.float32)), BF16)
    d_full = jnp.repeat(d_skip.astype(F32), SSM_HEAD_DIM).reshape(1, SSM_INNER)
    const = lambda a: pl.BlockSpec(a.shape, lambda b, c: (0,) * a.ndim)
    args = [_pad_lanes(dt_bias), _pad_col(dt_bias), _pad_lanes(a_log),
            _pad_col(a_log), d_full, norm_w.reshape(1, -1), tri, trit]
    rows = min(SSD_STEP_ROWS, length)
    return pl.pallas_call(
        _ssd_kernel,
        grid=(bsz, length // rows),
        in_specs=[
            pl.BlockSpec((None, rows, SSM_XBC), lambda b, c: (b, c, 0)),
            pl.BlockSpec((None, rows, SSM_INNER), lambda b, c: (b, c, 0)),
            pl.BlockSpec((None, rows, LANES), lambda b, c: (b, c, 0)),
            pl.BlockSpec((None, 16, rows), lambda b, c: (b, 0, c)),
        ] + [const(a) for a in args],
        out_specs=pl.BlockSpec((None, rows, SSM_INNER), lambda b, c: (b, c, 0)),
        out_shape=jax.ShapeDtypeStruct((bsz, length, SSM_INNER), BF16),
        scratch_shapes=[pltpu.VMEM((SSM_GROUPS, SSM_STATE, SSM_INNER // SSM_GROUPS), F32)],
        compiler_params=_cparams("parallel", "arbitrary"),
        name="ssd_mixer",
    )(xbc3, y3, small3, smallt, *args)


def _unit_lower_inverse(mats, row, col):
    eye = jnp.where(row == col, 1.0, 0.0)
    blk = lambda n: (row >> (n.bit_length() - 1)) == (col >> (n.bit_length() - 1))
    size = row.shape[0]
    p = [jnp.where(blk(16), -a, 0.0) for a in mats]
    t = [eye + x for x in p]
    p = [_mm(x, x) for x in p]
    for _ in range(2):
        both = [_mm(jnp.concatenate([x, y], axis=0), x) for x, y in zip(p, t)]
        p = [b[:size] for b in both]
        t = [y + b[size:] for y, b in zip(t, both)]
    t = [y + _mm(y, x) for y, x in zip(t, p)]
    for n in (16, 32):
        band = blk(2 * n) & jnp.logical_not(blk(n))
        left = [_mm(y, jnp.where(band, a, 0.0)) for y, a in zip(t, mats)]
        t = [y - _mm(x, y) for y, x in zip(t, left)]
    return t


def _gdn_kernel(qkv_ref, z_ref, ab_ref, abt_ref, dtb_r_ref, dtb_c_ref, alog_r_ref,
                alog_c_ref, nw_ref, tri_ref, trit_ref, o_ref, s_ref):
    n = GDN_TILE

    @pl.when(pl.program_id(1) == 0)
    def _():
        s_ref[...] = jnp.zeros_like(s_ref)

    for sub in range(qkv_ref.shape[0] // n):
        rows = slice(sub * n, (sub + 1) * n)
        _gdn_tile(_silu(qkv_ref[rows, :]), z_ref[rows, :], ab_ref[rows, :], abt_ref[:, rows], dtb_r_ref, dtb_c_ref,
                  alog_r_ref, alog_c_ref, nw_ref, tri_ref, trit_ref, o_ref.at[rows, :], s_ref)


def _gdn_tile(qkv, z, ab, abt, dtb_r_ref, dtb_c_ref, alog_r_ref, alog_c_ref, nw_ref, tri_ref, trit_ref, o_ref, s_ref):
    n = GDN_TILE
    c = GDN_CHUNK
    d = GDN_D
    hd = GDN_HEADS * d
    g = -jnp.exp(alog_r_ref[...]) * _softplus(ab + dtb_r_ref[...])
    gc_full = _spread_heads(_mm_sel_lhs(tri_ref[...], g), 0, GDN_HEADS, d)
    beta_full = _spread_heads(jax.nn.sigmoid(ab), GDN_HEADS, GDN_HEADS, d)
    gt = -jnp.exp(alog_c_ref[...]) * _softplus(abt + dtb_c_ref[...])
    gct = _mm_sel_rhs(gt, trit_ref[...])

    row = lax.broadcasted_iota(jnp.int32, (n, n), 0)
    col = lax.broadcasted_iota(jnp.int32, (n, n), 1)
    same = (row >> (c.bit_length() - 1)) == (col >> (c.bit_length() - 1))
    incl = same & (row >= col)
    strict = same & (row > col)
    zeros_half = jnp.zeros((c, d), F32)

    heads = range(GDN_HEADS)
    sl = [slice(h * d, (h + 1) * d) for h in heads]
    l2n = lambda x: x * lax.rsqrt(jnp.sum(x * x, axis=-1, keepdims=True) + EPS)
    qn = [l2n(qkv[:, sl[h]]) * (d ** -0.5) for h in heads]
    kn = [l2n(qkv[:, hd + h * d:hd + (h + 1) * d]) for h in heads]
    vh = [qkv[:, 2 * hd + h * d:2 * hd + (h + 1) * d] for h in heads]
    gcol = [gc_full[:, sl[h]] for h in heads]
    beta = [beta_full[:, sl[h]] for h in heads]
    edec = [jnp.exp(gcol[h] - gct[h:h + 1, :]) for h in heads]
    egc = [jnp.exp(x) for x in gcol]
    kb = [kn[h] * beta[h] for h in heads]
    on_k = [_mm_nt(jnp.concatenate([kb[h], qn[h]], axis=0), kn[h]) for h in heads]
    lower = [jnp.where(strict, on_k[h][:n] * edec[h], 0.0) for h in heads]
    aqk = [jnp.where(incl, on_k[h][n:] * edec[h], 0.0) for h in heads]
    tinv = _unit_lower_inverse(lower, row, col)
    sol = [_mm(tinv[h], jnp.concatenate([vh[h] * beta[h], kb[h] * egc[h]], axis=1)) for h in heads]
    qd = [qn[h] * egc[h] for h in heads]
    glast = [(gcol[h][c - 1:c, :], gcol[h][n - 1:n, :]) for h in heads]
    kdt = [(kn[h] * jnp.exp(jnp.concatenate([jnp.broadcast_to(glast[h][0], (c, d)),
                                             jnp.broadcast_to(glast[h][1], (c, d))], axis=0) - gcol[h])).T
           for h in heads]
    s0 = [s_ref[h] for h in heads]
    on_s0 = [_mm(jnp.concatenate([sol[h][:c, d:], qd[h][:c]], axis=0), s0[h]) for h in heads]
    v0 = [sol[h][:c, :d] - on_s0[h][:c] for h in heads]
    s1 = [s0[h] * jnp.exp(glast[h][0]) + _mm(kdt[h], jnp.concatenate([v0[h], zeros_half], axis=0)) for h in heads]
    on_s1 = [_mm(jnp.concatenate([sol[h][c:, d:], qd[h][c:]], axis=0), s1[h]) for h in heads]
    v1 = [sol[h][c:, :d] - on_s1[h][:c] for h in heads]
    for h in heads:
        s_ref[h] = s1[h] * jnp.exp(glast[h][1]) + _mm(kdt[h], jnp.concatenate([zeros_half, v1[h]], axis=0))
    outs = []
    for h in heads:
        o = (jnp.concatenate([on_s0[h][c:], on_s1[h][c:]], axis=0)
             + _mm(aqk[h], jnp.concatenate([v0[h], v1[h]], axis=0)))
        o = o * lax.rsqrt(jnp.mean(o * o, axis=-1, keepdims=True) + EPS) * nw_ref[...]
        outs.append(o * _silu(z[:, sl[h]]))
    o_ref[...] = jnp.concatenate(outs, axis=1).astype(o_ref.dtype)


def gated_deltanet_mixer(qkv3, y3, small3, smallt, dt_bias, a_log, norm_w):
    bsz, length, _ = y3.shape
    n = GDN_TILE
    hd = GDN_HEADS * GDN_D
    idx = np.arange(n)
    same = (idx[:, None] // GDN_CHUNK) == (idx[None, :] // GDN_CHUNK)
    tri = jnp.asarray(same & (idx[:, None] >= idx[None, :]), BF16)
    trit = jnp.asarray(same & (idx[:, None] <= idx[None, :]), BF16)
    const = lambda a: pl.BlockSpec(a.shape, lambda b, c: (0,) * a.ndim)
    args = [_pad_lanes(dt_bias), _pad_col(dt_bias), _pad_lanes(a_log), _pad_col(a_log),
            norm_w.reshape(1, -1), tri, trit]
    rows = min(GDN_STEP_ROWS, length)
    return pl.pallas_call(
        _gdn_kernel,
        grid=(bsz, length // rows),
        in_specs=[
            pl.BlockSpec((None, rows, 3 * hd), lambda b, c: (b, c, 0)),
            pl.BlockSpec((None, rows, hd), lambda b, c: (b, c, 0)),
            pl.BlockSpec((None, rows, LANES), lambda b, c: (b, c, 0)),
            pl.BlockSpec((None, 16, rows), lambda b, c: (b, 0, c)),
        ] + [const(a) for a in args],
        out_specs=pl.BlockSpec((None, rows, hd), lambda b, c: (b, c, 0)),
        out_shape=jax.ShapeDtypeStruct((bsz, length, hd), BF16),
        scratch_shapes=[pltpu.VMEM((GDN_HEADS, GDN_D, GDN_D), F32)],
        compiler_params=_cparams("parallel", "arbitrary"),
        name="gated_deltanet",
    )(qkv3, y3, small3, smallt, *args)


def _sb_kernel(q_ref, k_ref, v_ref, upper_ref, o_ref):
    blk = SB_BLOCK
    pair_w = 2 * SB_HEAD_DIM
    n_pairs = SB_STEP_HEADS // 2
    i = pl.program_id(2)
    q = q_ref[...] * (SB_HEAD_DIM ** -0.5)
    lane = lax.broadcasted_iota(jnp.int32, (blk, pair_w), 1)
    first_head = lane < SB_HEAD_DIM
    qs = []
    for p in range(n_pairs):
        q2 = q[:, p * pair_w:(p + 1) * pair_w]
        qs += [jnp.where(first_head, q2, 0.0).astype(BF16), jnp.where(first_head, 0.0, q2).astype(BF16)]
    row = lax.broadcasted_iota(jnp.int32, (blk, blk), 0)
    col = lax.broadcasted_iota(jnp.int32, (blk, blk), 1)
    earlier = col < row
    upper = upper_ref[...]
    heads = range(SB_STEP_HEADS)

    def local_part(kb, diagonal, exists=None):
        start = pl.multiple_of(kb * blk, blk)
        k = k_ref[pl.ds(start, blk), :].astype(BF16)
        v = v_ref[pl.ds(start, blk), :].astype(BF16)
        kp = [k[:, p * pair_w:(p + 1) * pair_w] for p in range(n_pairs)]
        vp = [v[:, p * pair_w:(p + 1) * pair_w] for p in range(n_pairs)]
        logits = [lax.dot_general(qs[h], kp[h // 2], (((1,), (1,)), ((), ())), preferred_element_type=F32)
                  for h in heads]
        keep = earlier if diagonal else None
        if exists is not None:
            keep = exists if keep is None else keep & exists
        log_keep = [-_softplus(x) for x in logits]
        if keep is not None:
            log_keep = [jnp.where(keep, x, 0.0) for x in log_keep]
        inside = [_mm_sel_rhs(x, upper, 2) for x in log_keep]
        totals = [jnp.sum(x, axis=-1, keepdims=True) for x in log_keep]
        return logits, log_keep, inside, totals, vp, keep

    def carried_part(local, accs, sticks):
        logits, log_keep, inside, totals, vp, keep = local
        w = [jnp.exp(logits[h] + log_keep[h] + inside[h] + sticks[h]) for h in heads]
        if keep is not None:
            w = [jnp.where(keep, x, 0.0) for x in w]
        pv = [jnp.dot(w[h].astype(BF16), vp[h // 2], preferred_element_type=F32) for h in heads]
        accs = tuple(accs[p] + jnp.where(first_head, pv[2 * p], pv[2 * p + 1]) for p in range(n_pairs))
        sticks = tuple(sticks[h] + totals[h] for h in heads)
        return accs, sticks

    accs = tuple(jnp.zeros((blk, pair_w), F32) for _ in range(n_pairs))
    sticks = tuple(jnp.zeros((blk, 1), F32) for _ in heads)
    eager = [local_part(i, True)]
    for back in range(1, SB_EAGER_BLOCKS + 1):
        eager.append(local_part(jnp.maximum(i - back, 0), False, exists=(row >= 0) & (i - back >= 0)))
    for local in eager:
        accs, sticks = carried_part(local, accs, sticks)

    def alive(state):
        kb, _, sticks = state
        longest = sticks[0]
        for s in sticks[1:]:
            longest = jnp.maximum(longest, s)
        return (kb >= 0) & (jnp.max(longest) > SB_LOG_ZERO)

    def body(state):
        kb, accs, sticks = state
        accs, sticks = carried_part(local_part(kb, False), accs, sticks)
        return kb - 1, accs, sticks

    _, accs, _ = lax.while_loop(alive, body, (i - 1 - SB_EAGER_BLOCKS, accs, sticks))
    o_ref[...] = jnp.concatenate(accs, axis=1).astype(o_ref.dtype)


def stick_breaking_mixer(y3, col0):
    bsz, length, _ = y3.shape
    blk = SB_BLOCK
    step_w = SB_STEP_HEADS * SB_HEAD_DIM
    steps = SB_DIM // step_w
    q0 = col0 // step_w
    idx = np.arange(blk)
    upper = jnp.asarray(idx[:, None] > idx[None, :], BF16)
    resident = lambda off: pl.BlockSpec((None, length, step_w), lambda b, p, i: (b, 0, q0 + off + p),
                                        pipeline_mode=pl.Buffered(1))
    return pl.pallas_call(
        _sb_kernel,
        grid=(bsz, steps, length // blk),
        in_specs=[
            pl.BlockSpec((None, blk, step_w), lambda b, p, i: (b, i, q0 + p)),
            resident(steps),
            resident(2 * steps),
            pl.BlockSpec((blk, blk), lambda b, p, i: (0, 0)),
        ],
        out_specs=pl.BlockSpec((None, blk, step_w), lambda b, p, i: (b, i, p)),
        out_shape=jax.ShapeDtypeStruct((bsz, length, SB_DIM), BF16),
        compiler_params=_cparams("parallel", "parallel", "arbitrary"),
        name="stick_breaking",
    )(y3, y3, y3, upper)


def _mixer_out(a_ref, b_ref, h_ref, wa_ref, wb_ref, rows):
    return h_ref[rows, :] + (jnp.dot(a_ref[rows, :].astype(BF16), wa_ref[...], preferred_element_type=F32)
                             + jnp.dot(b_ref[rows, :].astype(BF16), wb_ref[...], preferred_element_type=F32))


def _cross_attention(h, g_ref, wq_ref, kt_ref, v_ref, wo_ref):
    u = _rms(h, g_ref[...]).astype(BF16)
    q = jnp.dot(u, wq_ref[...], preferred_element_type=F32)
    heads = []
    for hd in range(XA_HEADS):
        sl = slice(hd * XA_HEAD_DIM, (hd + 1) * XA_HEAD_DIM)
        s = jnp.dot(q[:, sl].astype(BF16), kt_ref[sl, :], preferred_element_type=F32)
        p = jnp.exp(s - jnp.max(s, axis=-1, keepdims=True))
        p = p * (1.0 / jnp.sum(p, axis=-1, keepdims=True))
        heads.append(jnp.dot(p.astype(BF16), v_ref[:, sl], preferred_element_type=F32))
    o = jnp.concatenate(heads, axis=1).astype(BF16)
    return h + jnp.dot(o, wo_ref[...], preferred_element_type=F32)


def _route(xn, whi_ref, wlo_ref, b_ref, before_ref, run_ref):
    x_hi = xn.astype(BF16)
    x_lo = (xn - x_hi.astype(F32)).astype(BF16)
    wide = jnp.dot(x_hi, jnp.concatenate([whi_ref[...], wlo_ref[...]], axis=1), preferred_element_type=F32)
    logits = (wide[:, :LANES] + jnp.dot(x_lo, whi_ref[...], preferred_element_type=F32)
              + wide[:, LANES:] + b_ref[...])
    lane = lax.broadcasted_iota(jnp.int32, logits.shape, 1).astype(F32)
    neg = -1e30
    none = float(LANES)

    def top(vals):
        best = jnp.max(vals, axis=-1, keepdims=True)
        where = jnp.min(jnp.where(vals == best, lane, none), axis=-1, keepdims=True)
        return best, where

    gl = jnp.where(lane < MOE_GROUPS, logits, neg)
    gbest, gsel = top(gl)
    gprob = 1.0 / jnp.sum(jnp.exp(gl - gbest), axis=-1, keepdims=True)
    lo = MOE_GROUPS + gsel * MOE_PER_GROUP
    el = jnp.where((lane >= lo) & (lane < lo + MOE_PER_GROUP), logits, neg)
    m1, i1 = top(el)
    m2, i2 = top(jnp.where(lane == i1, neg, el))
    e = jnp.exp(m2 - m1)
    gate1 = gprob / (1.0 + e)
    gate2 = gprob * e / (1.0 + e)

    hot1 = lane == i1
    hot2 = lane == i2
    one1 = jnp.where(hot1, 1.0, 0.0)
    one2 = jnp.where(hot2, 1.0, 0.0)
    prefix = jnp.dot(before_ref[...], jnp.concatenate([one1, one2], axis=1).astype(BF16), preferred_element_type=F32)
    prefix1, prefix2 = prefix[:, :LANES], prefix[:, LANES:]
    total1 = jnp.sum(one1, axis=0, keepdims=True)
    running = run_ref[...]
    rank1 = jnp.sum(jnp.where(hot1, prefix1 + running, 0.0), axis=-1, keepdims=True)
    rank2 = jnp.sum(jnp.where(hot2, prefix2 + (running + total1), 0.0), axis=-1, keepdims=True)
    running = running + total1 + jnp.sum(one2, axis=0, keepdims=True)
    run_ref[...] = running

    fields = (i1 - MOE_GROUPS, i2 - MOE_GROUPS, gate1, gate2, rank1, rank2)
    out = jnp.zeros_like(logits)
    for k, val in enumerate(fields):
        out = jnp.where(lane == k, val, out)
    return out


def _post_mixer_kernel(a_ref, b_ref, h_ref, wa_ref, wb_ref, gxa_ref, wq_ref, kt_ref, v_ref, wo_ref,
                       gffn_ref, whi_ref, wlo_ref, bias_ref, before_ref,
                       h_out_ref, xn_ref, r_ref, cnt_ref, run_ref):
    @pl.when(pl.program_id(0) == 0)
    def _():
        run_ref[...] = jnp.zeros_like(run_ref)

    h = _mixer_out(a_ref, b_ref, h_ref, wa_ref, wb_ref, slice(None))
    h = _cross_attention(h, gxa_ref, wq_ref, kt_ref, v_ref, wo_ref)
    h_out_ref[...] = h
    xn = _rms(h, gffn_ref[...])
    xn_ref[...] = _pack_halves(xn)
    group = before_ref.shape[0]
    for start in range(0, h_ref.shape[0], group):
        rows = slice(start, start + group)
        r_ref[rows, :] = _route(xn[rows, :], whi_ref, wlo_ref, bias_ref, before_ref, run_ref)
    cnt_ref[...] = run_ref[...]


def post_mixer(ya, yb, h, wa, wb, g_xa, wq, kt, v, wo, g_ffn, w_hi, w_lo, bias, tm=1024):
    m, d = h.shape
    tiles_per_batch = m // kt.shape[0] // tm
    idx = np.arange(min(POST_GROUP_ROWS, tm))
    before = jnp.asarray(idx[:, None] > idx[None, :], BF16)
    rows = lambda w: pl.BlockSpec((tm, w), lambda i: (i, 0))
    const = lambda a: pl.BlockSpec(a.shape, lambda i: (0,) * a.ndim, pipeline_mode=pl.Buffered(1))
    per_batch = lambda a: pl.BlockSpec((None,) + a.shape[1:], lambda i: (i // tiles_per_batch, 0, 0))
    g_xa, g_ffn = g_xa.reshape(1, d), g_ffn.reshape(1, d)
    return pl.pallas_call(
        _post_mixer_kernel,
        grid=(m // tm,),
        in_specs=[rows(ya.shape[1]), rows(yb.shape[1]), rows(d), const(wa), const(wb), const(g_xa), const(wq),
                  per_batch(kt), per_batch(v), const(wo), const(g_ffn), const(w_hi), const(w_lo), const(bias),
                  const(before)],
        out_specs=[rows(d), rows(d // 2), rows(LANES), pl.BlockSpec((1, LANES), lambda i: (0, 0))],
        out_shape=[jax.ShapeDtypeStruct((m, d), F32), jax.ShapeDtypeStruct((m, d // 2), jnp.int32),
                   jax.ShapeDtypeStruct((m, LANES), F32), jax.ShapeDtypeStruct((1, LANES), F32)],
        scratch_shapes=[pltpu.VMEM((1, LANES), F32)],
        compiler_params=_cparams("arbitrary"),
        name="post_mixer",
    )(ya, yb, h, wa, wb, g_xa, wq, kt, v, wo, g_ffn, w_hi, w_lo, bias, before)


def _expert_kernel(table_ref, x_ref, wg_hbm, wu_hbm, wd_hbm, o_ref,
                   wg32_ref, wu32_ref, wd32_ref, wgb_ref, wub_ref, wdb_ref, sem_ref, *, layer):
    i = pl.program_id(0)
    beid_ref, valid_ref, first_ref, slot_ref, next_ref = (table_ref.at[k] for k in range(5))
    valid = valid_ref[i]

    def weight_copies(expert, slot):
        return (pltpu.make_async_copy(wg_hbm.at[layer, expert], wg32_ref.at[slot], sem_ref.at[slot, 0]),
                pltpu.make_async_copy(wu_hbm.at[layer, expert], wu32_ref.at[slot], sem_ref.at[slot, 1]),
                pltpu.make_async_copy(wd_hbm.at[layer, expert], wd32_ref.at[slot], sem_ref.at[slot, 2]))

    @pl.when(i == 0)
    def _():
        for copy in weight_copies(beid_ref[0], 0):
            copy.start()

    @pl.when(first_ref[i] == 1)
    def _():
        slot = slot_ref[i]
        for copy in weight_copies(beid_ref[i], slot):
            copy.wait()
        wgb_ref[...] = wg32_ref[slot].astype(BF16)
        wub_ref[...] = wu32_ref[slot].astype(BF16)
        wdb_ref[...] = wd32_ref[slot].astype(BF16)

        @pl.when(next_ref[i] >= 0)
        def _():
            for copy in weight_copies(next_ref[i], 1 - slot):
                copy.start()

    part = MOE_PART_ROWS
    n_parts = MOE_ROWS // part

    def ffn(used):
        row = lax.broadcasted_iota(jnp.int32, (part, 2 * x_ref.shape[1]), 0)
        xs = [jnp.where(row + k * part < valid, _unpack_halves(x_ref[k * part:(k + 1) * part, :]), 0.0).astype(BF16)
              for k in range(used)]
        gates = [jnp.dot(x, wgb_ref[...], preferred_element_type=F32) for x in xs]
        ups = [jnp.dot(x, wub_ref[...], preferred_element_type=F32) for x in xs]
        acts = [(_silu(g) * u).astype(BF16) for g, u in zip(gates, ups)]
        for k, act in enumerate(acts):
            o_ref[k * part:(k + 1) * part, :] = _pack_halves(jnp.dot(act, wdb_ref[...], preferred_element_type=F32))
        if used < n_parts:
            o_ref[used * part:, :] = jnp.zeros(((n_parts - used) * part, o_ref.shape[1]), o_ref.dtype)

    for used in range(1, n_parts + 1):
        @pl.when((valid > (used - 1) * part) & (valid <= used * part))
        def _(used=used):
            ffn(used)

    @pl.when(valid == 0)
    def _():
        o_ref[...] = jnp.zeros_like(o_ref)


def moe_experts(blocks, xs, w_gate, w_up, w_down, layer):
    n_slots, packed = xs.shape
    d = 2 * packed
    rows = MOE_ROWS
    ff = w_gate.shape[3]
    grid_spec = pltpu.PrefetchScalarGridSpec(
        num_scalar_prefetch=1,
        grid=(n_slots // rows,),
        in_specs=[
            pl.BlockSpec((rows, packed), lambda i, *_: (i, 0)),
            pl.BlockSpec(memory_space=pl.ANY),
            pl.BlockSpec(memory_space=pl.ANY),
            pl.BlockSpec(memory_space=pl.ANY),
        ],
        out_specs=pl.BlockSpec((rows, packed), lambda i, *_: (i, 0)),
        scratch_shapes=[pltpu.VMEM((2, d, ff), F32), pltpu.VMEM((2, d, ff), F32), pltpu.VMEM((2, ff, d), F32),
                        pltpu.VMEM((d, ff), BF16), pltpu.VMEM((d, ff), BF16), pltpu.VMEM((ff, d), BF16),
                        pltpu.SemaphoreType.DMA((2, 3))],
    )
    return pl.pallas_call(
        functools.partial(_expert_kernel, layer=layer),
        grid_spec=grid_spec,
        out_shape=jax.ShapeDtypeStruct((n_slots, packed), jnp.int32),
        compiler_params=_cparams("arbitrary"),
        name="moe_experts",
    )(blocks, xs, w_gate, w_up, w_down)


def _sc_mesh():
    return plsc.VectorSubcoreMesh(core_axis_name="c", subcore_axis_name="s",
                                  num_cores=SC_CORES, num_subcores=SC_SUBCORES)


def _sc_worker():
    return lax.axis_index("s") * SC_CORES + lax.axis_index("c")


def _sc_double_buffered(n_chunks, fetch, drain):
    assert n_chunks % 2 == 0
    start = lambda copies: [c.start() for c in copies]
    wait = lambda copies: [c.wait() for c in copies]
    start(fetch(0, 0))

    @pl.loop(0, n_chunks, step=2)
    def _(j):
        wait(fetch(j, 0))

        @pl.when(j > 0)
        def _():
            wait(drain(j - 1, 1))

        start(fetch(j + 1, 1))
        start(drain(j, 0))
        wait(fetch(j + 1, 1))
        wait(drain(j, 0))

        @pl.when(j + 2 < n_chunks)
        def _():
            start(fetch(j + 2, 0))

        start(drain(j + 1, 1))

    wait(drain(n_chunks - 1, 1))


def sc_scatter_rows(x, dest, n_slots):
    n_tok, d = x.shape
    per_worker = n_tok // SC_WORKERS
    n_chunks = per_worker // SC_CHUNK
    by_worker = dest.reshape(dest.shape[0] * SC_WORKERS, n_chunks, SC_CHUNK)

    @functools.partial(
        pl.kernel, mesh=_sc_mesh(), out_type=jax.ShapeDtypeStruct((n_slots, d), x.dtype),
        scratch_types=[pltpu.VMEM((n_chunks, SC_CHUNK), jnp.int32), pltpu.VMEM((n_chunks, SC_CHUNK), jnp.int32),
                       pltpu.VMEM((2, SC_CHUNK, d), x.dtype), pltpu.SemaphoreType.DMA((2, 3))],
        name="moe_scatter_rows")
    def scatter(x_hbm, dest_hbm, out_hbm, i0_v, i1_v, rows_v, sem):
        wid = _sc_worker()
        pltpu.sync_copy(dest_hbm.at[wid], i0_v)
        pltpu.sync_copy(dest_hbm.at[SC_WORKERS + wid], i1_v)

        def fetch(j, buf):
            start = pl.multiple_of(wid * per_worker + j * SC_CHUNK, SC_CHUNK)
            return [pltpu.make_async_copy(x_hbm.at[pl.ds(start, SC_CHUNK)], rows_v.at[buf], sem.at[buf, 0])]

        def drain(j, buf):
            return [pltpu.make_async_copy(rows_v.at[buf], out_hbm.at[i0_v.at[j]], sem.at[buf, 1]),
                    pltpu.make_async_copy(rows_v.at[buf], out_hbm.at[i1_v.at[j]], sem.at[buf, 2])]

        _sc_double_buffered(n_chunks, fetch, drain)

    return scatter(x, by_worker)


def sc_gather_rows(table, idx, n_out):
    d = table.shape[1]
    per_worker = n_out // SC_WORKERS
    n_chunks = per_worker // SC_CHUNK

    @functools.partial(
        pl.kernel, mesh=_sc_mesh(), out_type=jax.ShapeDtypeStruct((n_out, d), table.dtype),
        scratch_types=[pltpu.VMEM((n_chunks, SC_CHUNK), jnp.int32), pltpu.VMEM((2, SC_CHUNK, d), table.dtype),
                       pltpu.SemaphoreType.DMA((2, 2))],
        name="moe_gather_rows")
    def gather(table_hbm, idx_hbm, out_hbm, idx_v, rows_v, sem):
        wid = _sc_worker()
        pltpu.sync_copy(idx_hbm.at[wid], idx_v)

        def fetch(j, buf):
            return [pltpu.make_async_copy(table_hbm.at[idx_v.at[j]], rows_v.at[buf], sem.at[buf, 0])]

        def drain(j, buf):
            start = pl.multiple_of(wid * per_worker + j * SC_CHUNK, SC_CHUNK)
            return [pltpu.make_async_copy(rows_v.at[buf], out_hbm.at[pl.ds(start, SC_CHUNK)], sem.at[buf, 1])]

        _sc_double_buffered(n_chunks, fetch, drain)

    return gather(table, idx.reshape(-1, n_chunks, SC_CHUNK))


def _combine_kernel(h_ref, y0_ref, y1_ref, r_ref, g_ref, o_ref, *, final_norm):
    route = r_ref[...]
    h = h_ref[...] + (route[:, 2:3] * _unpack_halves(y0_ref[...]) + route[:, 3:4] * _unpack_halves(y1_ref[...]))
    o_ref[...] = _rms(h, g_ref[...]) if final_norm else h


def moe_combine(h, y01, route, g, final_norm, tm=1024):
    m, d = h.shape
    tm = min(tm, m)
    rows = lambda w: pl.BlockSpec((tm, w), lambda i: (i, 0))
    return pl.pallas_call(
        functools.partial(_combine_kernel, final_norm=final_norm),
        grid=(m // tm,),
        in_specs=[rows(d), rows(d // 2), pl.BlockSpec((tm, d // 2), lambda i: (i + m // tm, 0)), rows(LANES),
                  pl.BlockSpec((1, d), lambda i: (0, 0))],
        out_specs=rows(d),
        out_shape=jax.ShapeDtypeStruct((m, d), F32),
        compiler_params=_cparams("parallel"),
        name="moe_combine",
    )(h, y01, y01, route, g.reshape(1, d))


def _pad_cols(w):
    return jnp.pad(w, ((0, 0), (0, LANES - w.shape[1])))


def _plan_kernel(route_ref, cnt_ref, incl_ref, dest_ref, table_ref):
    f32_sum = lambda x, axis: jnp.sum(x, axis=axis, keepdims=True)
    lane = lax.broadcasted_iota(jnp.int32, (LANES, LANES), 1)
    sub = lax.broadcasted_iota(jnp.int32, (LANES, LANES), 0)
    incl = incl_ref[...]
    is_expert = (lane >= MOE_GROUPS) & (lane < MOE_GROUPS + MOE_EXPERTS)
    shift = MOE_ROWS.bit_length() - 1
    counts = jnp.broadcast_to(cnt_ref[...], (LANES, LANES)).astype(jnp.int32)
    padded = jnp.where(is_expert, ((counts + (MOE_ROWS - 1)) >> shift) << shift, 0)
    pad_end = _mm_sel_rhs(padded.astype(F32), incl)
    pad_start = pad_end - padded.astype(F32)

    route = route_ref[...]
    lane_t = lax.broadcasted_iota(jnp.int32, route.shape, 1)
    lane_f = lane_t.astype(F32)
    start_row = pad_start[0:1, :]
    slots = [f32_sum(jnp.where(lane_f == route[:, k:k + 1] + MOE_GROUPS, start_row, 0.0), 1) + route[:, 4 + k:5 + k]
             for k in range(2)]
    both = jnp.where(lane_t == 0, slots[0], jnp.where(lane_t == 1, slots[1], 0.0))
    dest_ref[...] = both.T[0:8, :].astype(jnp.int32)

    on_sub = lambda rows_equal: rows_equal.T
    expert_sub = (sub >= MOE_GROUPS) & (sub < MOE_GROUPS + MOE_EXPERTS)
    block_start = (lane * MOE_ROWS).astype(F32)
    eid = f32_sum(jnp.where(expert_sub & (on_sub(pad_end) <= block_start), 1.0, 0.0), 0)
    eid = jnp.minimum(eid, float(MOE_EXPERTS - 1))
    filled = on_sub(pad_start + counts.astype(F32))
    own = (sub - MOE_GROUPS).astype(F32) == eid
    valid = jnp.clip(f32_sum(jnp.where(own, filled, 0.0), 0) - block_start[0:1, :], 0.0, float(MOE_ROWS))
    eid_rows = jnp.broadcast_to(eid, (LANES, LANES))
    changed = (lane == 0) | (eid_rows != pltpu.roll(eid_rows, 1, axis=1))
    first = jnp.where((jnp.broadcast_to(valid, (LANES, LANES)) > 0) & changed, 1.0, 0.0)
    ordinal = _mm_sel_rhs(first, incl) - 1.0
    slot = ordinal - 2.0 * jnp.floor(ordinal * 0.5)
    later = (on_sub(first) > 0) & (sub > lane)
    nearest = jnp.min(jnp.where(later, sub, LANES), axis=0, keepdims=True)
    next_eid = f32_sum(jnp.where(sub == nearest, on_sub(eid_rows), 0.0), 0)
    next_eid = jnp.where(nearest < LANES, next_eid, -1.0)
    row8 = lax.broadcasted_iota(jnp.int32, (8, LANES), 0)
    table = jnp.zeros((8, LANES), F32)
    for k, val in enumerate((eid, valid, first[0:1, :], slot[0:1, :], next_eid)):
        table = jnp.where(row8 == k, val, table)
    table_ref[...] = table.astype(jnp.int32)


def moe_plan(route, counts, tm=4096):
    n_tok = route.shape[0]
    tm = min(tm, n_tok)
    idx = np.arange(LANES)
    incl = jnp.asarray(idx[:, None] <= idx[None, :], BF16)
    return pl.pallas_call(
        _plan_kernel,
        grid=(n_tok // tm,),
        in_specs=[pl.BlockSpec((tm, LANES), lambda i: (i, 0)), pl.BlockSpec((1, LANES), lambda i: (0, 0)),
                  pl.BlockSpec((LANES, LANES), lambda i: (0, 0))],
        out_specs=[pl.BlockSpec((8, tm), lambda i: (0, i)), pl.BlockSpec((8, LANES), lambda i: (0, 0))],
        out_shape=[jax.ShapeDtypeStruct((8, n_tok), jnp.int32), jax.ShapeDtypeStruct((8, LANES), jnp.int32)],
        compiler_params=_cparams("arbitrary"),
        name="moe_plan",
    )(route, counts, incl)


def _router_weights(w_group, b_group, w_expert, b_expert):
    w_r = _pad_cols(jnp.concatenate([w_group, w_expert], axis=1))
    w_hi = w_r.astype(BF16)
    w_lo = (w_r - w_hi.astype(F32)).astype(BF16)
    return w_hi, w_lo, _pad_lanes(jnp.concatenate([b_group, b_expert]))


def _moe_layer(h, xn, route, counts, w_gate, w_up, w_down, layer, final_g):
    n_tok, d = h.shape
    n_blocks = -(-(2 * n_tok + MOE_EXPERTS * (MOE_ROWS - 1)) // MOE_ROWS)
    dest, blocks = moe_plan(route, counts)
    xs = sc_scatter_rows(xn, dest, n_blocks * MOE_ROWS)
    ys = moe_experts(blocks, xs, w_gate, w_up, w_down, layer)
    y01 = sc_gather_rows(ys, dest, 2 * n_tok)
    g = jnp.ones((d,), F32) if final_g is None else final_g
    return moe_combine(h, y01, route, g, final_g is not None)


def _memory_kv(memn_in, mem_norm, wk, wv):
    bsz, m, d = memn_in.shape
    w = jnp.concatenate([wk, wv], axis=1).astype(BF16)
    kv, _ = rms_matmul(memn_in.reshape(bsz * m, d), mem_norm, w, jnp.zeros((d, LANES), BF16))
    k = kv[:, :d].reshape(bsz, m, d)
    v = kv[:, d:].reshape(bsz, m, d)
    return (jnp.swapaxes(k, 1, 2) * XA_HEAD_DIM ** -0.5).astype(BF16), v.astype(BF16)


def kernel(x, mem, mem_norm, final_norm, norm_mix, norm_xa, norm_ffn, xa_wq, xa_wk, xa_wv, xa_wo, moe_w_group, moe_b_group, moe_w_expert, moe_b_expert, moe_w_gate, moe_w_up, moe_w_down, ev_w_in, ev_sc_conv, ev_ssm_conv_w, ev_ssm_conv_b, ev_ssm_dt_bias, ev_ssm_a_log, ev_ssm_d, ev_ssm_norm, ev_w_out, od_w_in, od_gdn_conv, od_gdn_dt_bias, od_gdn_a_log, od_gdn_norm, od_w_out):
    bsz, length, d = x.shape
    n_tok = bsz * length
    depth = norm_mix.shape[0]
    h = x.reshape(n_tok, d)
    for layer in range(depth):
        i = layer // 2
        if layer % 2 == 0:
            w = ev_w_in[i]
            z0 = 3 * SC_DIM
            xbc0 = z0 + SSM_INNER
            w_conv = w[:, xbc0:xbc0 + SSM_XBC].astype(BF16)
            w_small = _pad_cols(w[:, xbc0 + SSM_XBC:]).astype(BF16)
            xbc = rms_matmul_conv(h, norm_mix[layer], w_conv, ev_ssm_conv_w[i], ev_ssm_conv_b[i], length)
            z, ya, small = rms_matmul_gated(h, norm_mix[layer], w[:, z0:xbc0].astype(BF16), w[:, :z0].astype(BF16),
                                            w_small, ev_sc_conv[i], length)
            small3 = small.reshape(bsz, length, LANES)
            smallt = jnp.swapaxes(small3[:, :, :16], 1, 2)
            yb = ssd_mixer(xbc.reshape(bsz, length, -1), z.reshape(bsz, length, -1), small3, smallt,
                           ev_ssm_dt_bias[i], ev_ssm_a_log[i], ev_ssm_d[i], ev_ssm_norm[i])
            w_out = ev_w_out[i].astype(BF16)
            split = SC_DIM
        else:
            w = od_w_in[i]
            qkv_w = 3 * GDN_HEADS * GDN_D
            z_end = qkv_w + GDN_HEADS * GDN_D
            w_conv = w[:, :qkv_w].astype(BF16)
            w_main = jnp.concatenate([w[:, qkv_w:z_end], w[:, z_end + 2 * GDN_HEADS:]], axis=1).astype(BF16)
            w_small = _pad_cols(w[:, z_end:z_end + 2 * GDN_HEADS]).astype(BF16)
            qkv = rms_matmul_conv(h, norm_mix[layer], w_conv, od_gdn_conv[i], jnp.zeros((qkv_w,), F32), length)
            y, small = rms_matmul(h, norm_mix[layer], w_main, w_small, tm=1024, tn=w_main.shape[1])
            y3 = y.reshape(bsz, length, -1)
            small3 = small.reshape(bsz, length, LANES)
            smallt = jnp.swapaxes(small3[:, :, :16], 1, 2)
            ya = gated_deltanet_mixer(qkv.reshape(bsz, length, -1), y3, small3, smallt, od_gdn_dt_bias[i],
                                      od_gdn_a_log[i], od_gdn_norm[i])
            yb = stick_breaking_mixer(y3, GDN_HEADS * GDN_D)
            w_out = od_w_out[i].astype(BF16)
            split = GDN_HEADS * GDN_D
        kt, v = _memory_kv(mem, mem_norm, xa_wk[layer], xa_wv[layer])
        w_hi, w_lo, bias = _router_weights(moe_w_group[layer], moe_b_group[layer], moe_w_expert[layer],
                                           moe_b_expert[layer])
        h, xn, route, counts = post_mixer(
            ya.reshape(n_tok, -1), yb.reshape(n_tok, -1), h, w_out[:split], w_out[split:], norm_xa[layer],
            xa_wq[layer].astype(BF16), kt, v, xa_wo[layer].astype(BF16), norm_ffn[layer], w_hi, w_lo, bias)
        h = _moe_layer(h, xn, route, counts, moe_w_gate, moe_w_up, moe_w_down, layer,
                       final_norm if layer == depth - 1 else None)
    return h.reshape(bsz, length, d)
```

```python
import functools

import jax
import jax.numpy as jnp
import numpy as np
from jax import lax
from jax.experimental import pallas as pl
from jax.experimental.pallas import tpu as pltpu
from jax.experimental.pallas import tpu_sc as plsc

F32 = jnp.float32
BF16 = jnp.bfloat16
EPS = 1e-6

SC_DIM = 512
SSM_HEADS = 16
SSM_HEAD_DIM = 64
SSM_INNER = 1024
SSM_GROUPS = 2
SSM_STATE = 128
SSM_XBC = SSM_INNER + 2 * SSM_GROUPS * SSM_STATE
SSD_CHUNK = 128
SSD_STEP_ROWS = 512
GDN_HEADS = 8
GDN_D = 128
GDN_CHUNK = 64
GDN_TILE = 128
GDN_STEP_ROWS = 512
SB_HEAD_DIM = 64
SB_DIM = 512
SB_BLOCK = 128
SB_STEP_HEADS = 8
SB_STEP_ROWS = 512
SB_EAGER_BLOCKS = 2
XA_HEADS = 4
XA_HEAD_DIM = 256
MOE_GROUPS = 4
MOE_PER_GROUP = 8
MOE_EXPERTS = 32
MOE_ROWS = 512
POST_GROUP_ROWS = 512
SC_CORES = 2
SC_SUBCORES = 16
SC_WORKERS = SC_CORES * SC_SUBCORES
SC_CHUNK = 64
HALO = 8
CONV_CHUNK = 512
LANES = 128
SB_LOG_ZERO = -104.0
VMEM_LIMIT = 56 * 1024 * 1024


def _cparams(*sem):
    return pltpu.CompilerParams(dimension_semantics=sem, vmem_limit_bytes=VMEM_LIMIT)


def _mm(a, b):
    return jnp.dot(a.astype(BF16), b.astype(BF16), preferred_element_type=F32)


def _mm_nt(a, b):
    return lax.dot_general(a.astype(BF16), b.astype(BF16), (((1,), (1,)), ((), ())),
                           preferred_element_type=F32)


def _split_bf16(x, n):
    parts, r = [], x
    for _ in range(n):
        p = r.astype(BF16)
        parts.append(p)
        r = r - p.astype(F32)
    return parts


def _mm_sel_rhs(x, sel, n=3):
    return sum(jnp.dot(p, sel, preferred_element_type=F32) for p in _split_bf16(x, n))


def _mm_sel_lhs(sel, x, n=3):
    return sum(jnp.dot(sel, p, preferred_element_type=F32) for p in _split_bf16(x, n))


def _spread_heads(x, first, n_heads, width):
    rows = x.shape[0]
    col = lambda h: jnp.broadcast_to(x[:, first + h:first + h + 1], (rows, LANES))
    if width == LANES:
        return jnp.concatenate([col(h) for h in range(n_heads)], axis=1)
    left = lax.broadcasted_iota(jnp.int32, (rows, LANES), 1) < width
    return jnp.concatenate([jnp.where(left, col(h), col(h + 1)) for h in range(0, n_heads, 2)], axis=1)


def _pack_halves(x):
    n = x.shape[1] // 2
    lo = pltpu.bitcast(x[:, :n].astype(BF16).astype(F32), jnp.int32)
    hi = pltpu.bitcast(x[:, n:].astype(BF16).astype(F32), jnp.int32)
    return lax.shift_right_logical(lo, 16) | (hi & jnp.int32(-65536))


def _unpack_halves(p):
    lo = pltpu.bitcast(lax.shift_left(p, 16), F32)
    hi = pltpu.bitcast(p & jnp.int32(-65536), F32)
    return jnp.concatenate([lo, hi], axis=1)


def _silu(x):
    return x * jax.nn.sigmoid(x)


def _softplus(x):
    return jnp.maximum(x, 0.0) + jnp.log(1.0 + jnp.exp(-jnp.abs(x)))


def _rms(x, g):
    return x * lax.rsqrt(jnp.mean(x * x, axis=-1, keepdims=True) + EPS) * g


def _rms_matmul_kernel(x_ref, g_ref, w_ref, ws_ref, o_ref, os_ref):
    xn = _rms(x_ref[...], g_ref[...]).astype(BF16)
    o_ref[...] = jnp.dot(xn, w_ref[...], preferred_element_type=F32)
    os_ref[...] = jnp.dot(xn, ws_ref[...], preferred_element_type=F32)


def rms_matmul(x, g, w, ws, tm=512, tn=512):
    m, k = x.shape
    n = w.shape[1]
    tm = min(tm, m)
    main, small = pl.pallas_call(
        _rms_matmul_kernel,
        grid=(n // tn, m // tm),
        in_specs=[
            pl.BlockSpec((tm, k), lambda j, i: (i, 0)),
            pl.BlockSpec((1, k), lambda j, i: (0, 0)),
            pl.BlockSpec((k, tn), lambda j, i: (0, j)),
            pl.BlockSpec((k, LANES), lambda j, i: (0, 0)),
        ],
        out_specs=[
            pl.BlockSpec((tm, tn), lambda j, i: (i, j)),
            pl.BlockSpec((None, tm, LANES), lambda j, i: (j, i, 0)),
        ],
        out_shape=[jax.ShapeDtypeStruct((m, n), F32), jax.ShapeDtypeStruct((n // tn, m, LANES), F32)],
        compiler_params=_cparams("parallel", "parallel"),
        name="rms_matmul",
    )(x, g.reshape(1, k), w, ws)
    return main, small[0]


def _causal_conv(ext_ref, w_ref, rows):
    width = w_ref.shape[0]
    ext = ext_ref[...]
    acc = None
    for j in range(width):
        shift = width - 1 - j
        moved = ext if shift == 0 else pltpu.roll(ext, shift, axis=0)
        term = w_ref[j:j + 1, :] * moved[HALO:HALO + rows, :]
        acc = term if acc is None else acc + term
    return acc


def _rms_matmul_conv_kernel(x_ref, g_ref, w_ref, cw_ref, cb_ref, o_ref, *ext_refs, tiles_per_seq):
    tm = x_ref.shape[0]
    starts_sequence = pl.program_id(1) % tiles_per_seq == 0

    @pl.when(starts_sequence)
    def _():
        for ext_ref in ext_refs:
            ext_ref[0:HALO, :] = jnp.zeros((HALO, CONV_CHUNK), F32)

    @pl.when(jnp.logical_not(starts_sequence))
    def _():
        for ext_ref in ext_refs:
            ext_ref[0:HALO, :] = ext_ref[tm:tm + HALO, :]

    xn = _rms(x_ref[...], g_ref[...]).astype(BF16)
    for c, ext_ref in enumerate(ext_refs):
        cols = slice(c * CONV_CHUNK, (c + 1) * CONV_CHUNK)
        ext_ref[HALO:, :] = jnp.dot(xn, w_ref[:, cols], preferred_element_type=F32)
        o_ref[:, cols] = _causal_conv(ext_ref, cw_ref.at[:, cols], tm) + cb_ref[:, cols]


def rms_matmul_conv(x, g, w, conv_w, conv_b, seq_len, tm=1024, tn=1536):
    m, k = x.shape
    n = w.shape[1]
    cols = lambda rows: pl.BlockSpec((rows, tn), lambda j, i: (0, j))
    return pl.pallas_call(
        functools.partial(_rms_matmul_conv_kernel, tiles_per_seq=seq_len // tm),
        grid=(n // tn, m // tm),
        in_specs=[
            pl.BlockSpec((tm, k), lambda j, i: (i, 0)),
            pl.BlockSpec((1, k), lambda j, i: (0, 0)),
            cols(k), cols(conv_w.shape[0]), cols(1),
        ],
        out_specs=pl.BlockSpec((tm, tn), lambda j, i: (i, j)),
        out_shape=jax.ShapeDtypeStruct((m, n), F32),
        scratch_shapes=[pltpu.VMEM((tm + HALO, CONV_CHUNK), F32)] * (tn // CONV_CHUNK),
        compiler_params=_cparams("arbitrary", "arbitrary"),
        name="rms_matmul_conv",
    )(x, g.reshape(1, k), w, conv_w, conv_b.reshape(1, n))


def _rms_matmul_gated_kernel(x_ref, g_ref, wz_ref, wbcx_ref, ws_ref, cw_ref, z_ref, ya_ref, os_ref, ext_ref,
                             *, tiles_per_seq):
    tm = x_ref.shape[0]
    starts_sequence = pl.program_id(0) % tiles_per_seq == 0

    @pl.when(starts_sequence)
    def _():
        ext_ref[0:HALO, :] = jnp.zeros((HALO, SC_DIM), F32)

    @pl.when(jnp.logical_not(starts_sequence))
    def _():
        ext_ref[0:HALO, :] = ext_ref[tm:tm + HALO, :]

    xn = _rms(x_ref[...], g_ref[...]).astype(BF16)
    z_ref[...] = jnp.dot(xn, wz_ref[...], preferred_element_type=F32)
    os_ref[...] = jnp.dot(xn, ws_ref[...], preferred_element_type=F32)
    bcx = jnp.dot(xn, wbcx_ref[...], preferred_element_type=F32)
    ext_ref[HALO:, :] = bcx[:, SC_DIM:2 * SC_DIM] * bcx[:, 2 * SC_DIM:]
    ya_ref[...] = (bcx[:, :SC_DIM] * _causal_conv(ext_ref, cw_ref, tm)).astype(ya_ref.dtype)


def rms_matmul_gated(x, g, w_z, w_bcx, w_small, conv_w, seq_len, tm=1024):
    m, k = x.shape
    const = lambda a: pl.BlockSpec(a.shape, lambda i: (0,) * a.ndim)
    rows = lambda w: pl.BlockSpec((tm, w), lambda i: (i, 0))
    g = g.reshape(1, k)
    return pl.pallas_call(
        functools.partial(_rms_matmul_gated_kernel, tiles_per_seq=seq_len // tm),
        grid=(m // tm,),
        in_specs=[rows(k), const(g), const(w_z), const(w_bcx), const(w_small), const(conv_w)],
        out_specs=[rows(w_z.shape[1]), rows(SC_DIM), rows(LANES)],
        out_shape=[jax.ShapeDtypeStruct((m, w_z.shape[1]), F32), jax.ShapeDtypeStruct((m, SC_DIM), BF16),
                   jax.ShapeDtypeStruct((m, LANES), F32)],
        scratch_shapes=[pltpu.VMEM((tm + HALO, SC_DIM), F32)],
        compiler_params=_cparams("arbitrary"),
        name="rms_matmul_gated",
    )(x, g, w_z, w_bcx, w_small, conv_w)


def _ssd_kernel(xbc_ref, z_ref, dt_ref, dtt_ref, dtb_r_ref, dtb_c_ref,
                alog_r_ref, alog_c_ref, d_ref, nw_ref, tri_ref, trit_ref,
                o_ref, s_ref):
    q = SSD_CHUNK

    @pl.when(pl.program_id(1) == 0)
    def _():
        s_ref[...] = jnp.zeros_like(s_ref)

    for sub in range(xbc_ref.shape[0] // q):
        rows = slice(sub * q, (sub + 1) * q)
        _ssd_chunk(_silu(xbc_ref[rows, :]), z_ref[rows, :], dt_ref[rows, :], dtt_ref[:, rows], dtb_r_ref, dtb_c_ref,
                   alog_r_ref, alog_c_ref, d_ref, nw_ref, tri_ref, trit_ref, o_ref.at[rows, :], s_ref)


def _ssd_chunk(xbc, z, dt_raw, dtt_raw, dtb_r_ref, dtb_c_ref, alog_r_ref, alog_c_ref, d_ref, nw_ref, tri_ref,
               trit_ref, o_ref, s_ref):
    q = SSD_CHUNK
    hpg = SSM_HEADS // SSM_GROUPS
    gw = hpg * SSM_HEAD_DIM
    xs = xbc[:, :SSM_INNER]
    bm = xbc[:, SSM_INNER:SSM_INNER + SSM_GROUPS * SSM_STATE]
    cm = xbc[:, SSM_INNER + SSM_GROUPS * SSM_STATE:]

    dt = _softplus(dt_raw + dtb_r_ref[...])
    acs = _mm_sel_lhs(tri_ref[...], dt * -jnp.exp(alog_r_ref[...]))
    dtt = _softplus(dtt_raw + dtb_c_ref[...])
    acst = _mm_sel_rhs(dtt * -jnp.exp(alog_c_ref[...]), trit_ref[...])
    dt_full = _spread_heads(dt, 0, SSM_HEADS, SSM_HEAD_DIM)
    acs_full = _spread_heads(acs, 0, SSM_HEADS, SSM_HEAD_DIM)
    acs_col = _spread_heads(acs, 0, SSM_HEADS, q)

    xdt = xs * dt_full
    acs_last = acs_full[q - 1:q, :]
    xw = xdt * jnp.exp(acs_last - acs_full)
    chunk_decay = jnp.exp(acs_last)

    row = lax.broadcasted_iota(jnp.int32, (q, q), 0)
    col = lax.broadcasted_iota(jnp.int32, (q, q), 1)
    causal = row >= col
    lane = lax.broadcasted_iota(jnp.int32, (q, 2 * SSM_HEAD_DIM), 1)

    y_diag, y_off = [], []
    for g in range(SSM_GROUPS):
        bm_g = bm[:, g * SSM_STATE:(g + 1) * SSM_STATE]
        cm_g = cm[:, g * SSM_STATE:(g + 1) * SSM_STATE]
        cb_g = _mm_nt(cm_g, bm_g)
        state = s_ref[g]
        y_off.append(_mm(cm_g, state))
        s_ref[g] = state * chunk_decay[:, g * gw:(g + 1) * gw] + _mm(bm_g.T, xw[:, g * gw:(g + 1) * gw])
        for pair in range(hpg // 2):
            h0 = g * hpg + 2 * pair
            xdt_pair = xdt[:, h0 * SSM_HEAD_DIM:(h0 + 2) * SSM_HEAD_DIM]
            weights = []
            for h in (h0, h0 + 1):
                seg = acs_col[:, h * q:(h + 1) * q] - acst[h:h + 1, :]
                weights.append(cb_g * jnp.where(causal, jnp.exp(seg), 0.0))
            both = _mm(jnp.concatenate(weights, axis=0), xdt_pair)
            y_diag.append(jnp.where(lane < SSM_HEAD_DIM, both[:q], both[q:]))
    y = (jnp.concatenate(y_diag, axis=1) + jnp.concatenate(y_off, axis=1) * jnp.exp(acs_full)
         + xs * d_ref[...])
    y = y * _silu(z)
    halves = []
    for g in range(SSM_GROUPS):
        yg = y[:, g * gw:(g + 1) * gw]
        halves.append(yg * lax.rsqrt(jnp.mean(yg * yg, axis=-1, keepdims=True) + EPS))
    o_ref[...] = (jnp.concatenate(halves, axis=1) * nw_ref[...]).astype(o_ref.dtype)


def _pad_lanes(v, fill=0.0):
    return jnp.pad(v.astype(F32), (0, LANES - v.shape[0]), constant_values=fill).reshape(1, LANES)


def _pad_col(v, rows=16):
    return jnp.pad(v.astype(F32), (0, rows - v.shape[0])).reshape(rows, 1)


def ssd_mixer(xbc3, y3, small3, smallt, dt_bias, a_log, d_skip, norm_w):
    bsz, length, _ = y3.shape
    q = SSD_CHUNK
    tri = jnp.asarray(np.tril(np.ones((q, q), np.float32)), BF16)
    trit = jnp.asarray(np.triu(np.ones((q, q), np.float32)), BF16)
    d_full = jnp.repeat(d_skip.astype(F32), SSM_HEAD_DIM).reshape(1, SSM_INNER)
    const = lambda a: pl.BlockSpec(a.shape, lambda b, c: (0,) * a.ndim)
    args = [_pad_lanes(dt_bias), _pad_col(dt_bias), _pad_lanes(a_log),
            _pad_col(a_log), d_full, norm_w.reshape(1, -1), tri, trit]
    rows = min(SSD_STEP_ROWS, length)
    return pl.pallas_call(
        _ssd_kernel,
        grid=(bsz, length // rows),
        in_specs=[
            pl.BlockSpec((None, rows, SSM_XBC), lambda b, c: (b, c, 0)),
            pl.BlockSpec((None, rows, SSM_INNER), lambda b, c: (b, c, 0)),
            pl.BlockSpec((None, rows, LANES), lambda b, c: (b, c, 0)),
            pl.BlockSpec((None, 16, rows), lambda b, c: (b, 0, c)),
        ] + [const(a) for a in args],
        out_specs=pl.BlockSpec((None, rows, SSM_INNER), lambda b, c: (b, c, 0)),
        out_shape=jax.ShapeDtypeStruct((bsz, length, SSM_INNER), BF16),
        scratch_shapes=[pltpu.VMEM((SSM_GROUPS, SSM_STATE, SSM_INNER // SSM_GROUPS), F32)],
        compiler_params=_cparams("parallel", "arbitrary"),
        name="ssd_mixer",
    )(xbc3, y3, small3, smallt, *args)


def _unit_lower_inverse(mats, row, col):
    eye = jnp.where(row == col, 1.0, 0.0)
    blk = lambda n: (row >> (n.bit_length() - 1)) == (col >> (n.bit_length() - 1))
    size = row.shape[0]
    p = [jnp.where(blk(16), -a, 0.0) for a in mats]
    t = [eye + x for x in p]
    p = [_mm(x, x) for x in p]
    for _ in range(2):
        both = [_mm(jnp.concatenate([x, y], axis=0), x) for x, y in zip(p, t)]
        p = [b[:size] for b in both]
        t = [y + b[size:] for y, b in zip(t, both)]
    t = [y + _mm(y, x) for y, x in zip(t, p)]
    for n in (16, 32):
        band = blk(2 * n) & jnp.logical_not(blk(n))
        left = [_mm(y, jnp.where(band, a, 0.0)) for y, a in zip(t, mats)]
        t = [y - _mm(x, y) for y, x in zip(t, left)]
    return t


def _gdn_kernel(qkv_ref, z_ref, ab_ref, abt_ref, dtb_r_ref, dtb_c_ref, alog_r_ref,
                alog_c_ref, nw_ref, tri_ref, trit_ref, o_ref, s_ref):
    n = GDN_TILE

    @pl.when(pl.program_id(1) == 0)
    def _():
        s_ref[...] = jnp.zeros_like(s_ref)

    for sub in range(qkv_ref.shape[0] // n):
        rows = slice(sub * n, (sub + 1) * n)
        _gdn_tile(_silu(qkv_ref[rows, :]), z_ref[rows, :], ab_ref[rows, :], abt_ref[:, rows], dtb_r_ref, dtb_c_ref,
                  alog_r_ref, alog_c_ref, nw_ref, tri_ref, trit_ref, o_ref.at[rows, :], s_ref)


def _gdn_tile(qkv, z, ab, abt, dtb_r_ref, dtb_c_ref, alog_r_ref, alog_c_ref, nw_ref, tri_ref, trit_ref, o_ref, s_ref):
    n = GDN_TILE
    c = GDN_CHUNK
    d = GDN_D
    hd = GDN_HEADS * d
    g = -jnp.exp(alog_r_ref[...]) * _softplus(ab + dtb_r_ref[...])
    gc_full = _spread_heads(_mm_sel_lhs(tri_ref[...], g), 0, GDN_HEADS, d)
    beta_full = _spread_heads(jax.nn.sigmoid(ab), GDN_HEADS, GDN_HEADS, d)
    gt = -jnp.exp(alog_c_ref[...]) * _softplus(abt + dtb_c_ref[...])
    gct = _mm_sel_rhs(gt, trit_ref[...])

    row = lax.broadcasted_iota(jnp.int32, (n, n), 0)
    col = lax.broadcasted_iota(jnp.int32, (n, n), 1)
    same = (row >> (c.bit_length() - 1)) == (col >> (c.bit_length() - 1))
    incl = same & (row >= col)
    strict = same & (row > col)
    zeros_half = jnp.zeros((c, d), F32)

    heads = range(GDN_HEADS)
    sl = [slice(h * d, (h + 1) * d) for h in heads]
    l2n = lambda x: x * lax.rsqrt(jnp.sum(x * x, axis=-1, keepdims=True) + EPS)
    qn = [l2n(qkv[:, sl[h]]) * (d ** -0.5) for h in heads]
    kn = [l2n(qkv[:, hd + h * d:hd + (h + 1) * d]) for h in heads]
    vh = [qkv[:, 2 * hd + h * d:2 * hd + (h + 1) * d] for h in heads]
    gcol = [gc_full[:, sl[h]] for h in heads]
    beta = [beta_full[:, sl[h]] for h in heads]
    edec = [jnp.exp(gcol[h] - gct[h:h + 1, :]) for h in heads]
    egc = [jnp.exp(x) for x in gcol]
    kb = [kn[h] * beta[h] for h in heads]
    on_k = [_mm_nt(jnp.concatenate([kb[h], qn[h]], axis=0), kn[h]) for h in heads]
    lower = [jnp.where(strict, on_k[h][:n] * edec[h], 0.0) for h in heads]
    aqk = [jnp.where(incl, on_k[h][n:] * edec[h], 0.0) for h in heads]
    tinv = _unit_lower_inverse(lower, row, col)
    sol = [_mm(tinv[h], jnp.concatenate([vh[h] * beta[h], kb[h] * egc[h]], axis=1)) for h in heads]
    qd = [qn[h] * egc[h] for h in heads]
    glast = [(gcol[h][c - 1:c, :], gcol[h][n - 1:n, :]) for h in heads]
    kdt = [(kn[h] * jnp.exp(jnp.concatenate([jnp.broadcast_to(glast[h][0], (c, d)),
                                             jnp.broadcast_to(glast[h][1], (c, d))], axis=0) - gcol[h])).T
           for h in heads]
    s0 = [s_ref[h] for h in heads]
    on_s0 = [_mm(jnp.concatenate([sol[h][:c, d:], qd[h][:c]], axis=0), s0[h]) for h in heads]
    v0 = [sol[h][:c, :d] - on_s0[h][:c] for h in heads]
    s1 = [s0[h] * jnp.exp(glast[h][0]) + _mm(kdt[h], jnp.concatenate([v0[h], zeros_half], axis=0)) for h in heads]
    on_s1 = [_mm(jnp.concatenate([sol[h][c:, d:], qd[h][c:]], axis=0), s1[h]) for h in heads]
    v1 = [sol[h][c:, :d] - on_s1[h][:c] for h in heads]
    for h in heads:
        s_ref[h] = s1[h] * jnp.exp(glast[h][1]) + _mm(kdt[h], jnp.concatenate([zeros_half, v1[h]], axis=0))
    outs = []
    for h in heads:
        o = (jnp.concatenate([on_s0[h][c:], on_s1[h][c:]], axis=0)
             + _mm(aqk[h], jnp.concatenate([v0[h], v1[h]], axis=0)))
        o = o * lax.rsqrt(jnp.mean(o * o, axis=-1, keepdims=True) + EPS) * nw_ref[...]
        outs.append(o * _silu(z[:, sl[h]]))
    o_ref[...] = jnp.concatenate(outs, axis=1).astype(o_ref.dtype)


def gated_deltanet_mixer(qkv3, y3, small3, smallt, dt_bias, a_log, norm_w):
    bsz, length, _ = y3.shape
    n = GDN_TILE
    hd = GDN_HEADS * GDN_D
    idx = np.arange(n)
    same = (idx[:, None] // GDN_CHUNK) == (idx[None, :] // GDN_CHUNK)
    tri = jnp.asarray(same & (idx[:, None] >= idx[None, :]), BF16)
    trit = jnp.asarray(same & (idx[:, None] <= idx[None, :]), BF16)
    const = lambda a: pl.BlockSpec(a.shape, lambda b, c: (0,) * a.ndim)
    args = [_pad_lanes(dt_bias), _pad_col(dt_bias), _pad_lanes(a_log), _pad_col(a_log),
            norm_w.reshape(1, -1), tri, trit]
    rows = min(GDN_STEP_ROWS, length)
    return pl.pallas_call(
        _gdn_kernel,
        grid=(bsz, length // rows),
        in_specs=[
            pl.BlockSpec((None, rows, 3 * hd), lambda b, c: (b, c, 0)),
            pl.BlockSpec((None, rows, hd), lambda b, c: (b, c, 0)),
            pl.BlockSpec((None, rows, LANES), lambda b, c: (b, c, 0)),
            pl.BlockSpec((None, 16, rows), lambda b, c: (b, 0, c)),
        ] + [const(a) for a in args],
        out_specs=pl.BlockSpec((None, rows, hd), lambda b, c: (b, c, 0)),
        out_shape=jax.ShapeDtypeStruct((bsz, length, hd), BF16),
        scratch_shapes=[pltpu.VMEM((GDN_HEADS, GDN_D, GDN_D), F32)],
        compiler_params=_cparams("parallel", "arbitrary"),
        name="gated_deltanet",
    )(qkv3, y3, small3, smallt, *args)


def _sb_kernel(q_ref, k_ref, v_ref, upper_ref, o_ref):
    blk = SB_BLOCK
    n_sub = q_ref.shape[0] // blk
    first = pl.program_id(2) * n_sub
    parts = [_sb_query_block(first + s, q_ref[s * blk:(s + 1) * blk, :], k_ref, v_ref, upper_ref[...])
             for s in range(n_sub)]
    for s, finish in enumerate(parts):
        o_ref[s * blk:(s + 1) * blk, :] = finish().astype(o_ref.dtype)


def _sb_query_block(i, q, k_ref, v_ref, upper):
    blk = SB_BLOCK
    pair_w = 2 * SB_HEAD_DIM
    n_pairs = SB_STEP_HEADS // 2
    q = q * (SB_HEAD_DIM ** -0.5)
    lane = lax.broadcasted_iota(jnp.int32, (blk, pair_w), 1)
    first_head = lane < SB_HEAD_DIM
    qs = []
    for p in range(n_pairs):
        q2 = q[:, p * pair_w:(p + 1) * pair_w]
        qs += [jnp.where(first_head, q2, 0.0).astype(BF16), jnp.where(first_head, 0.0, q2).astype(BF16)]
    row = lax.broadcasted_iota(jnp.int32, (blk, blk), 0)
    col = lax.broadcasted_iota(jnp.int32, (blk, blk), 1)
    earlier = col < row
    heads = range(SB_STEP_HEADS)

    def local_part(kb, diagonal, exists=None):
        start = pl.multiple_of(kb * blk, blk)
        k = k_ref[pl.ds(start, blk), :].astype(BF16)
        v = v_ref[pl.ds(start, blk), :].astype(BF16)
        kp = [k[:, p * pair_w:(p + 1) * pair_w] for p in range(n_pairs)]
        vp = [v[:, p * pair_w:(p + 1) * pair_w] for p in range(n_pairs)]
        logits = [lax.dot_general(qs[h], kp[h // 2], (((1,), (1,)), ((), ())), preferred_element_type=F32)
                  for h in heads]
        keep = earlier if diagonal else None
        if exists is not None:
            keep = exists if keep is None else keep & exists
        log_keep = [-_softplus(x) for x in logits]
        if keep is not None:
            log_keep = [jnp.where(keep, x, 0.0) for x in log_keep]
        inside = [_mm_sel_rhs(x, upper, 2) for x in log_keep]
        totals = [jnp.sum(x, axis=-1, keepdims=True) for x in log_keep]
        return logits, log_keep, inside, totals, vp, keep

    def carried_part(local, accs, sticks):
        logits, log_keep, inside, totals, vp, keep = local
        w = [jnp.exp(logits[h] + log_keep[h] + inside[h] + sticks[h]) for h in heads]
        if keep is not None:
            w = [jnp.where(keep, x, 0.0) for x in w]
        pv = [jnp.dot(w[h].astype(BF16), vp[h // 2], preferred_element_type=F32) for h in heads]
        accs = tuple(accs[p] + jnp.where(first_head, pv[2 * p], pv[2 * p + 1]) for p in range(n_pairs))
        sticks = tuple(sticks[h] + totals[h] for h in heads)
        return accs, sticks

    accs = tuple(jnp.zeros((blk, pair_w), F32) for _ in range(n_pairs))
    sticks = tuple(jnp.zeros((blk, 1), F32) for _ in heads)
    eager = [local_part(i, True)]
    for back in range(1, SB_EAGER_BLOCKS + 1):
        eager.append(local_part(jnp.maximum(i - back, 0), False, exists=(row >= 0) & (i - back >= 0)))
    for local in eager:
        accs, sticks = carried_part(local, accs, sticks)

    def alive(state):
        kb, _, sticks = state
        longest = sticks[0]
        for s in sticks[1:]:
            longest = jnp.maximum(longest, s)
        return (kb >= 0) & (jnp.max(longest) > SB_LOG_ZERO)

    def body(state):
        kb, accs, sticks = state
        accs, sticks = carried_part(local_part(kb, False), accs, sticks)
        return kb - 1, accs, sticks

    def finish():
        _, done, _ = lax.while_loop(alive, body, (i - 1 - SB_EAGER_BLOCKS, accs, sticks))
        return jnp.concatenate(done, axis=1)

    return finish


def stick_breaking_mixer(y3, col0):
    bsz, length, _ = y3.shape
    blk = SB_BLOCK
    step_w = SB_STEP_HEADS * SB_HEAD_DIM
    steps = SB_DIM // step_w
    q0 = col0 // step_w
    idx = np.arange(blk)
    upper = jnp.asarray(idx[:, None] > idx[None, :], BF16)
    resident = lambda off: pl.BlockSpec((None, length, step_w), lambda b, p, i: (b, 0, q0 + off + p),
                                        pipeline_mode=pl.Buffered(1))
    rows = min(SB_STEP_ROWS, length)
    return pl.pallas_call(
        _sb_kernel,
        grid=(bsz, steps, length // rows),
        in_specs=[
            pl.BlockSpec((None, rows, step_w), lambda b, p, i: (b, i, q0 + p)),
            resident(steps),
            resident(2 * steps),
            pl.BlockSpec((blk, blk), lambda b, p, i: (0, 0)),
        ],
        out_specs=pl.BlockSpec((None, rows, step_w), lambda b, p, i: (b, i, p)),
        out_shape=jax.ShapeDtypeStruct((bsz, length, SB_DIM), BF16),
        compiler_params=_cparams("parallel", "parallel", "arbitrary"),
        name="stick_breaking",
    )(y3, y3, y3, upper)


def _mixer_out(a_ref, b_ref, h_ref, wa_ref, wb_ref, rows):
    return h_ref[rows, :] + (jnp.dot(a_ref[rows, :].astype(BF16), wa_ref[...], preferred_element_type=F32)
                             + jnp.dot(b_ref[rows, :].astype(BF16), wb_ref[...], preferred_element_type=F32))


def _cross_attention(h, g_ref, wq_ref, kt_ref, v_ref, wo_ref):
    u = _rms(h, g_ref[...]).astype(BF16)
    q = jnp.dot(u, wq_ref[...], preferred_element_type=F32)
    heads = []
    for hd in range(XA_HEADS):
        sl = slice(hd * XA_HEAD_DIM, (hd + 1) * XA_HEAD_DIM)
        s = jnp.dot(q[:, sl].astype(BF16), kt_ref[sl, :], preferred_element_type=F32)
        p = jnp.exp(s - jnp.max(s, axis=-1, keepdims=True))
        p = p * (1.0 / jnp.sum(p, axis=-1, keepdims=True))
        heads.append(jnp.dot(p.astype(BF16), v_ref[:, sl], preferred_element_type=F32))
    o = jnp.concatenate(heads, axis=1).astype(BF16)
    return h + jnp.dot(o, wo_ref[...], preferred_element_type=F32)


def _route(xn, whi_ref, wlo_ref, b_ref, before_ref, run_ref):
    x_hi = xn.astype(BF16)
    x_lo = (xn - x_hi.astype(F32)).astype(BF16)
    wide = jnp.dot(x_hi, jnp.concatenate([whi_ref[...], wlo_ref[...]], axis=1), preferred_element_type=F32)
    logits = (wide[:, :LANES] + jnp.dot(x_lo, whi_ref[...], preferred_element_type=F32)
              + wide[:, LANES:] + b_ref[...])
    lane = lax.broadcasted_iota(jnp.int32, logits.shape, 1).astype(F32)
    neg = -1e30
    none = float(LANES)

    def top(vals):
        best = jnp.max(vals, axis=-1, keepdims=True)
        where = jnp.min(jnp.where(vals == best, lane, none), axis=-1, keepdims=True)
        return best, where

    gl = jnp.where(lane < MOE_GROUPS, logits, neg)
    gbest, gsel = top(gl)
    gprob = 1.0 / jnp.sum(jnp.exp(gl - gbest), axis=-1, keepdims=True)
    lo = MOE_GROUPS + gsel * MOE_PER_GROUP
    el = jnp.where((lane >= lo) & (lane < lo + MOE_PER_GROUP), logits, neg)
    m1, i1 = top(el)
    m2, i2 = top(jnp.where(lane == i1, neg, el))
    e = jnp.exp(m2 - m1)
    gate1 = gprob / (1.0 + e)
    gate2 = gprob * e / (1.0 + e)

    hot1 = lane == i1
    hot2 = lane == i2
    one1 = jnp.where(hot1, 1.0, 0.0)
    one2 = jnp.where(hot2, 1.0, 0.0)
    prefix = jnp.dot(before_ref[...], jnp.concatenate([one1, one2], axis=1).astype(BF16), preferred_element_type=F32)
    prefix1, prefix2 = prefix[:, :LANES], prefix[:, LANES:]
    total1 = jnp.sum(one1, axis=0, keepdims=True)
    running = run_ref[...]
    rank1 = jnp.sum(jnp.where(hot1, prefix1 + running, 0.0), axis=-1, keepdims=True)
    rank2 = jnp.sum(jnp.where(hot2, prefix2 + (running + total1), 0.0), axis=-1, keepdims=True)
    running = running + total1 + jnp.sum(one2, axis=0, keepdims=True)
    run_ref[...] = running

    fields = (i1 - MOE_GROUPS, i2 - MOE_GROUPS, gate1, gate2, rank1, rank2)
    out = jnp.zeros_like(logits)
    for k, val in enumerate(fields):
        out = jnp.where(lane == k, val, out)
    return out


def _post_mixer_kernel(a_ref, b_ref, h_ref, wa_ref, wb_ref, gxa_ref, wq_ref, kt_ref, v_ref, wo_ref,
                       gffn_ref, whi_ref, wlo_ref, bias_ref, before_ref,
                       h_out_ref, xn_ref, r_ref, cnt_ref, run_ref):
    @pl.when(pl.program_id(0) == 0)
    def _():
        run_ref[...] = jnp.zeros_like(run_ref)

    h = _mixer_out(a_ref, b_ref, h_ref, wa_ref, wb_ref, slice(None))
    h = _cross_attention(h, gxa_ref, wq_ref, kt_ref, v_ref, wo_ref)
    h_out_ref[...] = h
    xn = _rms(h, gffn_ref[...])
    xn_ref[...] = _pack_halves(xn)
    group = before_ref.shape[0]
    for start in range(0, h_ref.shape[0], group):
        rows = slice(start, start + group)
        r_ref[rows, :] = _route(xn[rows, :], whi_ref, wlo_ref, bias_ref, before_ref, run_ref)
    cnt_ref[...] = run_ref[...]


def post_mixer(ya, yb, h, wa, wb, g_xa, wq, kt, v, wo, g_ffn, w_hi, w_lo, bias, tm=1024):
    m, d = h.shape
    tiles_per_batch = m // kt.shape[0] // tm
    idx = np.arange(min(POST_GROUP_ROWS, tm))
    before = jnp.asarray(idx[:, None] > idx[None, :], BF16)
    rows = lambda w: pl.BlockSpec((tm, w), lambda i: (i, 0))
    const = lambda a: pl.BlockSpec(a.shape, lambda i: (0,) * a.ndim, pipeline_mode=pl.Buffered(1))
    per_batch = lambda a: pl.BlockSpec((None,) + a.shape[1:], lambda i: (i // tiles_per_batch, 0, 0))
    g_xa, g_ffn = g_xa.reshape(1, d), g_ffn.reshape(1, d)
    return pl.pallas_call(
        _post_mixer_kernel,
        grid=(m // tm,),
        in_specs=[rows(ya.shape[1]), rows(yb.shape[1]), rows(d), const(wa), const(wb), const(g_xa), const(wq),
                  per_batch(kt), per_batch(v), const(wo), const(g_ffn), const(w_hi), const(w_lo), const(bias),
                  const(before)],
        out_specs=[rows(d), rows(d // 2), rows(LANES), pl.BlockSpec((1, LANES), lambda i: (0, 0))],
        out_shape=[jax.ShapeDtypeStruct((m, d), F32), jax.ShapeDtypeStruct((m, d // 2), jnp.int32),
                   jax.ShapeDtypeStruct((m, LANES), F32), jax.ShapeDtypeStruct((1, LANES), F32)],
        scratch_shapes=[pltpu.VMEM((1, LANES), F32)],
        compiler_params=_cparams("arbitrary"),
        name="post_mixer",
    )(ya, yb, h, wa, wb, g_xa, wq, kt, v, wo, g_ffn, w_hi, w_lo, bias, before)


def _expert_kernel(table_ref, x_ref, wg_hbm, wu_hbm, wd_hbm, o_ref,
                   wg32_ref, wu32_ref, wd32_ref, wgb_ref, wub_ref, wdb_ref, sem_ref, *, layer):
    i = pl.program_id(0)
    beid_ref, valid_ref, first_ref, slot_ref, next_ref = (table_ref.at[k] for k in range(5))
    valid = valid_ref[i]

    def weight_copies(expert, slot):
        return (pltpu.make_async_copy(wg_hbm.at[layer, expert], wg32_ref.at[slot], sem_ref.at[slot, 0]),
                pltpu.make_async_copy(wu_hbm.at[layer, expert], wu32_ref.at[slot], sem_ref.at[slot, 1]),
                pltpu.make_async_copy(wd_hbm.at[layer, expert], wd32_ref.at[slot], sem_ref.at[slot, 2]))

    @pl.when(i == 0)
    def _():
        for copy in weight_copies(beid_ref[0], 0):
            copy.start()

    @pl.when(first_ref[i] == 1)
    def _():
        slot = slot_ref[i]
        for copy in weight_copies(beid_ref[i], slot):
            copy.wait()
        wgb_ref[...] = wg32_ref[slot].astype(BF16)
        wub_ref[...] = wu32_ref[slot].astype(BF16)
        wdb_ref[...] = wd32_ref[slot].astype(BF16)

        @pl.when(next_ref[i] >= 0)
        def _():
            for copy in weight_copies(next_ref[i], 1 - slot):
                copy.start()

    half = MOE_ROWS // 2

    def ffn(n_halves):
        row = lax.broadcasted_iota(jnp.int32, (half, 2 * x_ref.shape[1]), 0)
        xs = [jnp.where(row + k * half < valid, _unpack_halves(x_ref[k * half:(k + 1) * half, :]), 0.0).astype(BF16)
              for k in range(n_halves)]
        gates = [jnp.dot(x, wgb_ref[...], preferred_element_type=F32) for x in xs]
        ups = [jnp.dot(x, wub_ref[...], preferred_element_type=F32) for x in xs]
        acts = [(_silu(g) * u).astype(BF16) for g, u in zip(gates, ups)]
        for k, act in enumerate(acts):
            o_ref[k * half:(k + 1) * half, :] = _pack_halves(jnp.dot(act, wdb_ref[...], preferred_element_type=F32))

    @pl.when(valid > half)
    def _():
        ffn(2)

    @pl.when((valid > 0) & (valid <= half))
    def _():
        ffn(1)
        o_ref[half:, :] = jnp.zeros((half, o_ref.shape[1]), o_ref.dtype)

    @pl.when(valid == 0)
    def _():
        o_ref[...] = jnp.zeros_like(o_ref)


def moe_experts(blocks, xs, w_gate, w_up, w_down, layer):
    n_slots, packed = xs.shape
    d = 2 * packed
    rows = MOE_ROWS
    ff = w_gate.shape[3]
    grid_spec = pltpu.PrefetchScalarGridSpec(
        num_scalar_prefetch=1,
        grid=(n_slots // rows,),
        in_specs=[
            pl.BlockSpec((rows, packed), lambda i, *_: (i, 0)),
            pl.BlockSpec(memory_space=pl.ANY),
            pl.BlockSpec(memory_space=pl.ANY),
            pl.BlockSpec(memory_space=pl.ANY),
        ],
        out_specs=pl.BlockSpec((rows, packed), lambda i, *_: (i, 0)),
        scratch_shapes=[pltpu.VMEM((2, d, ff), F32), pltpu.VMEM((2, d, ff), F32), pltpu.VMEM((2, ff, d), F32),
                        pltpu.VMEM((d, ff), BF16), pltpu.VMEM((d, ff), BF16), pltpu.VMEM((ff, d), BF16),
                        pltpu.SemaphoreType.DMA((2, 3))],
    )
    return pl.pallas_call(
        functools.partial(_expert_kernel, layer=layer),
        grid_spec=grid_spec,
        out_shape=jax.ShapeDtypeStruct((n_slots, packed), jnp.int32),
        compiler_params=_cparams("arbitrary"),
        name="moe_experts",
    )(blocks, xs, w_gate, w_up, w_down)


def _sc_mesh():
    return plsc.VectorSubcoreMesh(core_axis_name="c", subcore_axis_name="s",
                                  num_cores=SC_CORES, num_subcores=SC_SUBCORES)


def _sc_worker():
    return lax.axis_index("s") * SC_CORES + lax.axis_index("c")


def _sc_double_buffered(n_chunks, fetch, drain):
    assert n_chunks % 2 == 0
    start = lambda copies: [c.start() for c in copies]
    wait = lambda copies: [c.wait() for c in copies]
    start(fetch(0, 0))

    @pl.loop(0, n_chunks, step=2)
    def _(j):
        wait(fetch(j, 0))

        @pl.when(j > 0)
        def _():
            wait(drain(j - 1, 1))

        start(fetch(j + 1, 1))
        start(drain(j, 0))
        wait(fetch(j + 1, 1))
        wait(drain(j, 0))

        @pl.when(j + 2 < n_chunks)
        def _():
            start(fetch(j + 2, 0))

        start(drain(j + 1, 1))

    wait(drain(n_chunks - 1, 1))


def sc_scatter_rows(x, dest, n_slots):
    n_tok, d = x.shape
    per_worker = n_tok // SC_WORKERS
    n_chunks = per_worker // SC_CHUNK
    by_worker = dest.reshape(dest.shape[0] * SC_WORKERS, n_chunks, SC_CHUNK)

    @functools.partial(
        pl.kernel, mesh=_sc_mesh(), out_type=jax.ShapeDtypeStruct((n_slots, d), x.dtype),
        scratch_types=[pltpu.VMEM((n_chunks, SC_CHUNK), jnp.int32), pltpu.VMEM((n_chunks, SC_CHUNK), jnp.int32),
                       pltpu.VMEM((2, SC_CHUNK, d), x.dtype), pltpu.SemaphoreType.DMA((2, 3))],
        name="moe_scatter_rows")
    def scatter(x_hbm, dest_hbm, out_hbm, i0_v, i1_v, rows_v, sem):
        wid = _sc_worker()
        pltpu.sync_copy(dest_hbm.at[wid], i0_v)
        pltpu.sync_copy(dest_hbm.at[SC_WORKERS + wid], i1_v)

        def fetch(j, buf):
            start = pl.multiple_of(wid * per_worker + j * SC_CHUNK, SC_CHUNK)
            return [pltpu.make_async_copy(x_hbm.at[pl.ds(start, SC_CHUNK)], rows_v.at[buf], sem.at[buf, 0])]

        def drain(j, buf):
            return [pltpu.make_async_copy(rows_v.at[buf], out_hbm.at[i0_v.at[j]], sem.at[buf, 1]),
                    pltpu.make_async_copy(rows_v.at[buf], out_hbm.at[i1_v.at[j]], sem.at[buf, 2])]

        _sc_double_buffered(n_chunks, fetch, drain)

    return scatter(x, by_worker)


def sc_gather_rows(table, idx, n_out):
    d = table.shape[1]
    per_worker = n_out // SC_WORKERS
    n_chunks = per_worker // SC_CHUNK

    @functools.partial(
        pl.kernel, mesh=_sc_mesh(), out_type=jax.ShapeDtypeStruct((n_out, d), table.dtype),
        scratch_types=[pltpu.VMEM((n_chunks, SC_CHUNK), jnp.int32), pltpu.VMEM((2, SC_CHUNK, d), table.dtype),
                       pltpu.SemaphoreType.DMA((2, 2))],
        name="moe_gather_rows")
    def gather(table_hbm, idx_hbm, out_hbm, idx_v, rows_v, sem):
        wid = _sc_worker()
        pltpu.sync_copy(idx_hbm.at[wid], idx_v)

        def fetch(j, buf):
            return [pltpu.make_async_copy(table_hbm.at[idx_v.at[j]], rows_v.at[buf], sem.at[buf, 0])]

        def drain(j, buf):
            start = pl.multiple_of(wid * per_worker + j * SC_CHUNK, SC_CHUNK)
            return [pltpu.make_async_copy(rows_v.at[buf], out_hbm.at[pl.ds(start, SC_CHUNK)], sem.at[buf, 1])]

        _sc_double_buffered(n_chunks, fetch, drain)

    return gather(table, idx.reshape(-1, n_chunks, SC_CHUNK))


def _combine_kernel(h_ref, y0_ref, y1_ref, r_ref, g_ref, o_ref, *, final_norm):
    route = r_ref[...]
    h = h_ref[...] + (route[:, 2:3] * _unpack_halves(y0_ref[...]) + route[:, 3:4] * _unpack_halves(y1_ref[...]))
    o_ref[...] = _rms(h, g_ref[...]) if final_norm else h


def moe_combine(h, y01, route, g, final_norm, tm=1024):
    m, d = h.shape
    tm = min(tm, m)
    rows = lambda w: pl.BlockSpec((tm, w), lambda i: (i, 0))
    return pl.pallas_call(
        functools.partial(_combine_kernel, final_norm=final_norm),
        grid=(m // tm,),
        in_specs=[rows(d), rows(d // 2), pl.BlockSpec((tm, d // 2), lambda i: (i + m // tm, 0)), rows(LANES),
                  pl.BlockSpec((1, d), lambda i: (0, 0))],
        out_specs=rows(d),
        out_shape=jax.ShapeDtypeStruct((m, d), F32),
        compiler_params=_cparams("parallel"),
        name="moe_combine",
    )(h, y01, y01, route, g.reshape(1, d))


def _pad_cols(w):
    return jnp.pad(w, ((0, 0), (0, LANES - w.shape[1])))


def _plan_kernel(route_ref, cnt_ref, incl_ref, dest_ref, table_ref):
    f32_sum = lambda x, axis: jnp.sum(x, axis=axis, keepdims=True)
    lane = lax.broadcasted_iota(jnp.int32, (LANES, LANES), 1)
    sub = lax.broadcasted_iota(jnp.int32, (LANES, LANES), 0)
    incl = incl_ref[...]
    is_expert = (lane >= MOE_GROUPS) & (lane < MOE_GROUPS + MOE_EXPERTS)
    shift = MOE_ROWS.bit_length() - 1
    counts = jnp.broadcast_to(cnt_ref[...], (LANES, LANES)).astype(jnp.int32)
    padded = jnp.where(is_expert, ((counts + (MOE_ROWS - 1)) >> shift) << shift, 0)
    pad_end = _mm_sel_rhs(padded.astype(F32), incl)
    pad_start = pad_end - padded.astype(F32)

    route = route_ref[...]
    lane_t = lax.broadcasted_iota(jnp.int32, route.shape, 1)
    lane_f = lane_t.astype(F32)
    start_row = pad_start[0:1, :]
    slots = [f32_sum(jnp.where(lane_f == route[:, k:k + 1] + MOE_GROUPS, start_row, 0.0), 1) + route[:, 4 + k:5 + k]
             for k in range(2)]
    both = jnp.where(lane_t == 0, slots[0], jnp.where(lane_t == 1, slots[1], 0.0))
    dest_ref[...] = both.T[0:8, :].astype(jnp.int32)

    on_sub = lambda rows_equal: rows_equal.T
    expert_sub = (sub >= MOE_GROUPS) & (sub < MOE_GROUPS + MOE_EXPERTS)
    block_start = (lane * MOE_ROWS).astype(F32)
    eid = f32_sum(jnp.where(expert_sub & (on_sub(pad_end) <= block_start), 1.0, 0.0), 0)
    eid = jnp.minimum(eid, float(MOE_EXPERTS - 1))
    filled = on_sub(pad_start + counts.astype(F32))
    own = (sub - MOE_GROUPS).astype(F32) == eid
    valid = jnp.clip(f32_sum(jnp.where(own, filled, 0.0), 0) - block_start[0:1, :], 0.0, float(MOE_ROWS))
    eid_rows = jnp.broadcast_to(eid, (LANES, LANES))
    changed = (lane == 0) | (eid_rows != pltpu.roll(eid_rows, 1, axis=1))
    first = jnp.where((jnp.broadcast_to(valid, (LANES, LANES)) > 0) & changed, 1.0, 0.0)
    ordinal = _mm_sel_rhs(first, incl) - 1.0
    slot = ordinal - 2.0 * jnp.floor(ordinal * 0.5)
    later = (on_sub(first) > 0) & (sub > lane)
    nearest = jnp.min(jnp.where(later, sub, LANES), axis=0, keepdims=True)
    next_eid = f32_sum(jnp.where(sub == nearest, on_sub(eid_rows), 0.0), 0)
    next_eid = jnp.where(nearest < LANES, next_eid, -1.0)
    row8 = lax.broadcasted_iota(jnp.int32, (8, LANES), 0)
    table = jnp.zeros((8, LANES), F32)
    for k, val in enumerate((eid, valid, first[0:1, :], slot[0:1, :], next_eid)):
        table = jnp.where(row8 == k, val, table)
    table_ref[...] = table.astype(jnp.int32)


def moe_plan(route, counts, tm=4096):
    n_tok = route.shape[0]
    tm = min(tm, n_tok)
    idx = np.arange(LANES)
    incl = jnp.asarray(idx[:, None] <= idx[None, :], BF16)
    return pl.pallas_call(
        _plan_kernel,
        grid=(n_tok // tm,),
        in_specs=[pl.BlockSpec((tm, LANES), lambda i: (i, 0)), pl.BlockSpec((1, LANES), lambda i: (0, 0)),
                  pl.BlockSpec((LANES, LANES), lambda i: (0, 0))],
        out_specs=[pl.BlockSpec((8, tm), lambda i: (0, i)), pl.BlockSpec((8, LANES), lambda i: (0, 0))],
        out_shape=[jax.ShapeDtypeStruct((8, n_tok), jnp.int32), jax.ShapeDtypeStruct((8, LANES), jnp.int32)],
        compiler_params=_cparams("arbitrary"),
        name="moe_plan",
    )(route, counts, incl)


def _router_weights(w_group, b_group, w_expert, b_expert):
    w_r = _pad_cols(jnp.concatenate([w_group, w_expert], axis=1))
    w_hi = w_r.astype(BF16)
    w_lo = (w_r - w_hi.astype(F32)).astype(BF16)
    return w_hi, w_lo, _pad_lanes(jnp.concatenate([b_group, b_expert]))


def _moe_layer(h, xn, route, counts, w_gate, w_up, w_down, layer, final_g):
    n_tok, d = h.shape
    n_blocks = -(-(2 * n_tok + MOE_EXPERTS * (MOE_ROWS - 1)) // MOE_ROWS)
    dest, blocks = moe_plan(route, counts)
    xs = sc_scatter_rows(xn, dest, n_blocks * MOE_ROWS)
    ys = moe_experts(blocks, xs, w_gate, w_up, w_down, layer)
    y01 = sc_gather_rows(ys, dest, 2 * n_tok)
    g = jnp.ones((d,), F32) if final_g is None else final_g
    return moe_combine(h, y01, route, g, final_g is not None)


def _memory_kv(memn_in, mem_norm, wk, wv):
    bsz, m, d = memn_in.shape
    w = jnp.concatenate([wk, wv], axis=1).astype(BF16)
    kv, _ = rms_matmul(memn_in.reshape(bsz * m, d), mem_norm, w, jnp.zeros((d, LANES), BF16))
    k = kv[:, :d].reshape(bsz, m, d)
    v = kv[:, d:].reshape(bsz, m, d)
    return (jnp.swapaxes(k, 1, 2) * XA_HEAD_DIM ** -0.5).astype(BF16), v.astype(BF16)


def kernel(x, mem, mem_norm, final_norm, norm_mix, norm_xa, norm_ffn, xa_wq, xa_wk, xa_wv, xa_wo, moe_w_group, moe_b_group, moe_w_expert, moe_b_expert, moe_w_gate, moe_w_up, moe_w_down, ev_w_in, ev_sc_conv, ev_ssm_conv_w, ev_ssm_conv_b, ev_ssm_dt_bias, ev_ssm_a_log, ev_ssm_d, ev_ssm_norm, ev_w_out, od_w_in, od_gdn_conv, od_gdn_dt_bias, od_gdn_a_log, od_gdn_norm, od_w_out):
    bsz, length, d = x.shape
    n_tok = bsz * length
    depth = norm_mix.shape[0]
    h = x.reshape(n_tok, d)
    for layer in range(depth):
        i = layer // 2
        if layer % 2 == 0:
            w = ev_w_in[i]
            z0 = 3 * SC_DIM
            xbc0 = z0 + SSM_INNER
            w_conv = w[:, xbc0:xbc0 + SSM_XBC].astype(BF16)
            w_small = _pad_cols(w[:, xbc0 + SSM_XBC:]).astype(BF16)
            xbc = rms_matmul_conv(h, norm_mix[layer], w_conv, ev_ssm_conv_w[i], ev_ssm_conv_b[i], length)
            z, ya, small = rms_matmul_gated(h, norm_mix[layer], w[:, z0:xbc0].astype(BF16), w[:, :z0].astype(BF16),
                                            w_small, ev_sc_conv[i], length)
            small3 = small.reshape(bsz, length, LANES)
            smallt = jnp.swapaxes(small3[:, :, :16], 1, 2)
            yb = ssd_mixer(xbc.reshape(bsz, length, -1), z.reshape(bsz, length, -1), small3, smallt,
                           ev_ssm_dt_bias[i], ev_ssm_a_log[i], ev_ssm_d[i], ev_ssm_norm[i])
            w_out = ev_w_out[i].astype(BF16)
            split = SC_DIM
        else:
            w = od_w_in[i]
            qkv_w = 3 * GDN_HEADS * GDN_D
            z_end = qkv_w + GDN_HEADS * GDN_D
            w_conv = w[:, :qkv_w].astype(BF16)
            w_main = jnp.concatenate([w[:, qkv_w:z_end], w[:, z_end + 2 * GDN_HEADS:]], axis=1).astype(BF16)
            w_small = _pad_cols(w[:, z_end:z_end + 2 * GDN_HEADS]).astype(BF16)
            qkv = rms_matmul_conv(h, norm_mix[layer], w_conv, od_gdn_conv[i], jnp.zeros((qkv_w,), F32), length)
            y, small = rms_matmul(h, norm_mix[layer], w_main, w_small, tm=1024, tn=w_main.shape[1])
            y3 = y.reshape(bsz, length, -1)
            small3 = small.reshape(bsz, length, LANES)
            smallt = jnp.swapaxes(small3[:, :, :16], 1, 2)
            ya = gated_deltanet_mixer(qkv.reshape(bsz, length, -1), y3, small3, smallt, od_gdn_dt_bias[i],
                                      od_gdn_a_log[i], od_gdn_norm[i])
            yb = stick_breaking_mixer(y3, GDN_HEADS * GDN_D)
            w_out = od_w_out[i].astype(BF16)
            split = GDN_HEADS * GDN_D
        kt, v = _memory_kv(mem, mem_norm, xa_wk[layer], xa_wv[layer])
        w_hi, w_lo, bias = _router_weights(moe_w_group[layer], moe_b_group[layer], moe_w_expert[layer],
                                           moe_b_expert[layer])
        h, xn, route, counts = post_mixer(
            ya.reshape(n_tok, -1), yb.reshape(n_tok, -1), h, w_out[:split], w_out[split:], norm_xa[layer],
            xa_wq[layer].astype(BF16), kt, v, xa_wo[layer].astype(BF16), norm_ffn[layer], w_hi, w_lo, bias)
        h = _moe_layer(h, xn, route, counts, moe_w_gate, moe_w_up, moe_w_down, layer,
                       final_norm if layer == depth - 1 else None)
    return h.reshape(bsz, length, d)
```

```python
import functools

import jax
import jax.numpy as jnp
import numpy as np
from jax import lax
from jax.experimental import pallas as pl
from jax.experimental.pallas import tpu as pltpu
from jax.experimental.pallas import tpu_sc as plsc

F32 = jnp.float32
BF16 = jnp.bfloat16
EPS = 1e-6

SC_DIM = 512
SSM_HEADS = 16
SSM_HEAD_DIM = 64
SSM_INNER = 1024
SSM_GROUPS = 2
SSM_STATE = 128
SSM_XBC = SSM_INNER + 2 * SSM_GROUPS * SSM_STATE
SSD_CHUNK = 128
SSD_STEP_ROWS = 512
GDN_HEADS = 8
GDN_D = 128
GDN_CHUNK = 64
GDN_TILE = 128
GDN_STEP_ROWS = 512
SB_HEAD_DIM = 64
SB_DIM = 512
SB_BLOCK = 128
SB_STEP_HEADS = 8
SB_STEP_ROWS = 512
SB_EAGER_BLOCKS = 2
XA_HEADS = 4
XA_HEAD_DIM = 256
MOE_GROUPS = 4
MOE_PER_GROUP = 8
MOE_EXPERTS = 32
MOE_ROWS = 512
POST_GROUP_ROWS = 512
SC_CORES = 2
SC_SUBCORES = 16
SC_WORKERS = SC_CORES * SC_SUBCORES
SC_CHUNK = 64
HALO = 8
CONV_CHUNK = 512
LANES = 128
SB_LOG_ZERO = -104.0
VMEM_LIMIT = 56 * 1024 * 1024


def _cparams(*sem):
    return pltpu.CompilerParams(dimension_semantics=sem, vmem_limit_bytes=VMEM_LIMIT)


def _mm(a, b):
    return jnp.dot(a.astype(BF16), b.astype(BF16), preferred_element_type=F32)


def _mm_nt(a, b):
    return lax.dot_general(a.astype(BF16), b.astype(BF16), (((1,), (1,)), ((), ())),
                           preferred_element_type=F32)


def _split_bf16(x, n):
    parts, r = [], x
    for _ in range(n):
        p = r.astype(BF16)
        parts.append(p)
        r = r - p.astype(F32)
    return parts


def _mm_sel_rhs(x, sel, n=3):
    return sum(jnp.dot(p, sel, preferred_element_type=F32) for p in _split_bf16(x, n))


def _mm_sel_lhs(sel, x, n=3):
    return sum(jnp.dot(sel, p, preferred_element_type=F32) for p in _split_bf16(x, n))


def _spread_heads(x, first, n_heads, width):
    rows = x.shape[0]
    col = lambda h: jnp.broadcast_to(x[:, first + h:first + h + 1], (rows, LANES))
    if width == LANES:
        return jnp.concatenate([col(h) for h in range(n_heads)], axis=1)
    left = lax.broadcasted_iota(jnp.int32, (rows, LANES), 1) < width
    return jnp.concatenate([jnp.where(left, col(h), col(h + 1)) for h in range(0, n_heads, 2)], axis=1)


def _pack_halves(x):
    n = x.shape[1] // 2
    lo = pltpu.bitcast(x[:, :n].astype(BF16).astype(F32), jnp.int32)
    hi = pltpu.bitcast(x[:, n:].astype(BF16).astype(F32), jnp.int32)
    return lax.shift_right_logical(lo, 16) | (hi & jnp.int32(-65536))


def _unpack_halves(p):
    lo = pltpu.bitcast(lax.shift_left(p, 16), F32)
    hi = pltpu.bitcast(p & jnp.int32(-65536), F32)
    return jnp.concatenate([lo, hi], axis=1)


def _silu(x):
    half = 0.5 * x
    return half * jnp.tanh(half) + half


def _softplus(x):
    return jnp.maximum(x, 0.0) + jnp.log(1.0 + jnp.exp(-jnp.abs(x)))


def _rms(x, g):
    return x * lax.rsqrt(jnp.mean(x * x, axis=-1, keepdims=True) + EPS) * g


def _rms_matmul_kernel(x_ref, g_ref, w_ref, ws_ref, o_ref, os_ref):
    xn = _rms(x_ref[...], g_ref[...]).astype(BF16)
    o_ref[...] = jnp.dot(xn, w_ref[...], preferred_element_type=F32)
    os_ref[...] = jnp.dot(xn, ws_ref[...], preferred_element_type=F32)


def rms_matmul(x, g, w, ws, tm=512, tn=512):
    m, k = x.shape
    n = w.shape[1]
    tm = min(tm, m)
    main, small = pl.pallas_call(
        _rms_matmul_kernel,
        grid=(n // tn, m // tm),
        in_specs=[
            pl.BlockSpec((tm, k), lambda j, i: (i, 0)),
            pl.BlockSpec((1, k), lambda j, i: (0, 0)),
            pl.BlockSpec((k, tn), lambda j, i: (0, j)),
            pl.BlockSpec((k, LANES), lambda j, i: (0, 0)),
        ],
        out_specs=[
            pl.BlockSpec((tm, tn), lambda j, i: (i, j)),
            pl.BlockSpec((None, tm, LANES), lambda j, i: (j, i, 0)),
        ],
        out_shape=[jax.ShapeDtypeStruct((m, n), F32), jax.ShapeDtypeStruct((n // tn, m, LANES), F32)],
        compiler_params=_cparams("parallel", "parallel"),
        name="rms_matmul",
    )(x, g.reshape(1, k), w, ws)
    return main, small[0]


def _causal_conv(ext_ref, w_ref, rows):
    width = w_ref.shape[0]
    ext = ext_ref[...]
    acc = None
    for j in range(width):
        shift = width - 1 - j
        moved = ext if shift == 0 else pltpu.roll(ext, shift, axis=0)
        term = w_ref[j:j + 1, :] * moved[HALO:HALO + rows, :]
        acc = term if acc is None else acc + term
    return acc


def _rms_matmul_conv_kernel(x_ref, g_ref, w_ref, cw_ref, cb_ref, o_ref, *ext_refs, tiles_per_seq):
    tm = x_ref.shape[0]
    starts_sequence = pl.program_id(1) % tiles_per_seq == 0

    @pl.when(starts_sequence)
    def _():
        for ext_ref in ext_refs:
            ext_ref[0:HALO, :] = jnp.zeros((HALO, CONV_CHUNK), F32)

    @pl.when(jnp.logical_not(starts_sequence))
    def _():
        for ext_ref in ext_refs:
            ext_ref[0:HALO, :] = ext_ref[tm:tm + HALO, :]

    xn = _rms(x_ref[...], g_ref[...]).astype(BF16)
    for c, ext_ref in enumerate(ext_refs):
        cols = slice(c * CONV_CHUNK, (c + 1) * CONV_CHUNK)
        ext_ref[HALO:, :] = jnp.dot(xn, w_ref[:, cols], preferred_element_type=F32)
        o_ref[:, cols] = _causal_conv(ext_ref, cw_ref.at[:, cols], tm) + cb_ref[:, cols]


def rms_matmul_conv(x, g, w, conv_w, conv_b, seq_len, tm=1024, tn=1536):
    m, k = x.shape
    n = w.shape[1]
    cols = lambda rows: pl.BlockSpec((rows, tn), lambda j, i: (0, j))
    return pl.pallas_call(
        functools.partial(_rms_matmul_conv_kernel, tiles_per_seq=seq_len // tm),
        grid=(n // tn, m // tm),
        in_specs=[
            pl.BlockSpec((tm, k), lambda j, i: (i, 0)),
            pl.BlockSpec((1, k), lambda j, i: (0, 0)),
            cols(k), cols(conv_w.shape[0]), cols(1),
        ],
        out_specs=pl.BlockSpec((tm, tn), lambda j, i: (i, j)),
        out_shape=jax.ShapeDtypeStruct((m, n), F32),
        scratch_shapes=[pltpu.VMEM((tm + HALO, CONV_CHUNK), F32)] * (tn // CONV_CHUNK),
        compiler_params=_cparams("arbitrary", "arbitrary"),
        name="rms_matmul_conv",
    )(x, g.reshape(1, k), w, conv_w, conv_b.reshape(1, n))


def _rms_matmul_gated_kernel(x_ref, g_ref, wz_ref, wbcx_ref, ws_ref, cw_ref, z_ref, ya_ref, os_ref, ext_ref,
                             *, tiles_per_seq):
    tm = x_ref.shape[0]
    starts_sequence = pl.program_id(0) % tiles_per_seq == 0

    @pl.when(starts_sequence)
    def _():
        ext_ref[0:HALO, :] = jnp.zeros((HALO, SC_DIM), F32)

    @pl.when(jnp.logical_not(starts_sequence))
    def _():
        ext_ref[0:HALO, :] = ext_ref[tm:tm + HALO, :]

    xn = _rms(x_ref[...], g_ref[...]).astype(BF16)
    z_ref[...] = jnp.dot(xn, wz_ref[...], preferred_element_type=F32)
    os_ref[...] = jnp.dot(xn, ws_ref[...], preferred_element_type=F32)
    bcx = jnp.dot(xn, wbcx_ref[...], preferred_element_type=F32)
    ext_ref[HALO:, :] = bcx[:, SC_DIM:2 * SC_DIM] * bcx[:, 2 * SC_DIM:]
    ya_ref[...] = (bcx[:, :SC_DIM] * _causal_conv(ext_ref, cw_ref, tm)).astype(ya_ref.dtype)


def rms_matmul_gated(x, g, w_z, w_bcx, w_small, conv_w, seq_len, tm=1024):
    m, k = x.shape
    const = lambda a: pl.BlockSpec(a.shape, lambda i: (0,) * a.ndim)
    rows = lambda w: pl.BlockSpec((tm, w), lambda i: (i, 0))
    g = g.reshape(1, k)
    return pl.pallas_call(
        functools.partial(_rms_matmul_gated_kernel, tiles_per_seq=seq_len // tm),
        grid=(m // tm,),
        in_specs=[rows(k), const(g), const(w_z), const(w_bcx), const(w_small), const(conv_w)],
        out_specs=[rows(w_z.shape[1]), rows(SC_DIM), rows(LANES)],
        out_shape=[jax.ShapeDtypeStruct((m, w_z.shape[1]), F32), jax.ShapeDtypeStruct((m, SC_DIM), BF16),
                   jax.ShapeDtypeStruct((m, LANES), F32)],
        scratch_shapes=[pltpu.VMEM((tm + HALO, SC_DIM), F32)],
        compiler_params=_cparams("arbitrary"),
        name="rms_matmul_gated",
    )(x, g, w_z, w_bcx, w_small, conv_w)


def _ssd_kernel(xbc_ref, z_ref, dt_ref, dtt_ref, dtb_r_ref, dtb_c_ref,
                alog_r_ref, alog_c_ref, d_ref, nw_ref, tri_ref, trit_ref,
                o_ref, s_ref):
    q = SSD_CHUNK

    @pl.when(pl.program_id(1) == 0)
    def _():
        s_ref[...] = jnp.zeros_like(s_ref)

    for sub in range(xbc_ref.shape[0] // q):
        rows = slice(sub * q, (sub + 1) * q)
        _ssd_chunk(_silu(xbc_ref[rows, :]), z_ref[rows, :], dt_ref[rows, :], dtt_ref[:, rows], dtb_r_ref, dtb_c_ref,
                   alog_r_ref, alog_c_ref, d_ref, nw_ref, tri_ref, trit_ref, o_ref.at[rows, :], s_ref)


def _ssd_chunk(xbc, z, dt_raw, dtt_raw, dtb_r_ref, dtb_c_ref, alog_r_ref, alog_c_ref, d_ref, nw_ref, tri_ref,
               trit_ref, o_ref, s_ref):
    q = SSD_CHUNK
    hpg = SSM_HEADS // SSM_GROUPS
    gw = hpg * SSM_HEAD_DIM
    xs = xbc[:, :SSM_INNER]
    bm = xbc[:, SSM_INNER:SSM_INNER + SSM_GROUPS * SSM_STATE]
    cm = xbc[:, SSM_INNER + SSM_GROUPS * SSM_STATE:]

    dt = _softplus(dt_raw + dtb_r_ref[...])
    acs = _mm_sel_lhs(tri_ref[...], dt * -jnp.exp(alog_r_ref[...]))
    dtt = _softplus(dtt_raw + dtb_c_ref[...])
    acst = _mm_sel_rhs(dtt * -jnp.exp(alog_c_ref[...]), trit_ref[...])
    dt_full = _spread_heads(dt, 0, SSM_HEADS, SSM_HEAD_DIM)
    acs_full = _spread_heads(acs, 0, SSM_HEADS, SSM_HEAD_DIM)
    acs_col = _spread_heads(acs, 0, SSM_HEADS, q)

    xdt = xs * dt_full
    acs_last = acs_full[q - 1:q, :]
    xw = xdt * jnp.exp(acs_last - acs_full)
    chunk_decay = jnp.exp(acs_last)

    row = lax.broadcasted_iota(jnp.int32, (q, q), 0)
    col = lax.broadcasted_iota(jnp.int32, (q, q), 1)
    causal = row >= col
    lane = lax.broadcasted_iota(jnp.int32, (q, 2 * SSM_HEAD_DIM), 1)

    y_diag, y_off = [], []
    for g in range(SSM_GROUPS):
        bm_g = bm[:, g * SSM_STATE:(g + 1) * SSM_STATE]
        cm_g = cm[:, g * SSM_STATE:(g + 1) * SSM_STATE]
        cb_g = _mm_nt(cm_g, bm_g)
        state = s_ref[g]
        y_off.append(_mm(cm_g, state))
        s_ref[g] = state * chunk_decay[:, g * gw:(g + 1) * gw] + _mm(bm_g.T, xw[:, g * gw:(g + 1) * gw])
        for pair in range(hpg // 2):
            h0 = g * hpg + 2 * pair
            xdt_pair = xdt[:, h0 * SSM_HEAD_DIM:(h0 + 2) * SSM_HEAD_DIM]
            weights = []
            for h in (h0, h0 + 1):
                seg = acs_col[:, h * q:(h + 1) * q] - acst[h:h + 1, :]
                weights.append(cb_g * jnp.where(causal, jnp.exp(seg), 0.0))
            both = _mm(jnp.concatenate(weights, axis=0), xdt_pair)
            y_diag.append(jnp.where(lane < SSM_HEAD_DIM, both[:q], both[q:]))
    y = (jnp.concatenate(y_diag, axis=1) + jnp.concatenate(y_off, axis=1) * jnp.exp(acs_full)
         + xs * d_ref[...])
    y = y * _silu(z)
    halves = []
    for g in range(SSM_GROUPS):
        yg = y[:, g * gw:(g + 1) * gw]
        halves.append(yg * lax.rsqrt(jnp.mean(yg * yg, axis=-1, keepdims=True) + EPS))
    o_ref[...] = (jnp.concatenate(halves, axis=1) * nw_ref[...]).astype(o_ref.dtype)


def _pad_lanes(v, fill=0.0):
    return jnp.pad(v.astype(F32), (0, LANES - v.shape[0]), constant_values=fill).reshape(1, LANES)


def _pad_col(v, rows=16):
    return jnp.pad(v.astype(F32), (0, rows - v.shape[0])).reshape(rows, 1)


def ssd_mixer(xbc3, y3, small3, smallt, dt_bias, a_log, d_skip, norm_w):
    bsz, length, _ = y3.shape
    q = SSD_CHUNK
    tri = jnp.asarray(np.tril(np.ones((q, q), np.float32)), BF16)
    trit = jnp.asarray(np.triu(np.ones((q, q), np.float32)), BF16)
    d_full = jnp.repeat(d_skip.astype(F32), SSM_HEAD_DIM).reshape(1, SSM_INNER)
    const = lambda a: pl.BlockSpec(a.shape, lambda b, c: (0,) * a.ndim)
    args = [_pad_lanes(dt_bias), _pad_col(dt_bias), _pad_lanes(a_log),
            _pad_col(a_log), d_full, norm_w.reshape(1, -1), tri, trit]
    rows = min(SSD_STEP_ROWS, length)
    return pl.pallas_call(
        _ssd_kernel,
        grid=(bsz, length // rows),
        in_specs=[
            pl.BlockSpec((None, rows, SSM_XBC), lambda b, c: (b, c, 0)),
            pl.BlockSpec((None, rows, SSM_INNER), lambda b, c: (b, c, 0)),
            pl.BlockSpec((None, rows, LANES), lambda b, c: (b, c, 0)),
            pl.BlockSpec((None, 16, rows), lambda b, c: (b, 0, c)),
        ] + [const(a) for a in args],
        out_specs=pl.BlockSpec((None, rows, SSM_INNER), lambda b, c: (b, c, 0)),
        out_shape=jax.ShapeDtypeStruct((bsz, length, SSM_INNER), BF16),
        scratch_shapes=[pltpu.VMEM((SSM_GROUPS, SSM_STATE, SSM_INNER // SSM_GROUPS), F32)],
        compiler_params=_cparams("parallel", "arbitrary"),
        name="ssd_mixer",
    )(xbc3, y3, small3, smallt, *args)


def _unit_lower_inverse(mats, row, col):
    eye = jnp.where(row == col, 1.0, 0.0)
    blk = lambda n: (row >> (n.bit_length() - 1)) == (col >> (n.bit_length() - 1))
    size = row.shape[0]
    p = [jnp.where(blk(16), -a, 0.0) for a in mats]
    t = [eye + x for x in p]
    p = [_mm(x, x) for x in p]
    for _ in range(2):
        both = [_mm(jnp.concatenate([x, y], axis=0), x) for x, y in zip(p, t)]
        p = [b[:size] for b in both]
        t = [y + b[size:] for y, b in zip(t, both)]
    t = [y + _mm(y, x) for y, x in zip(t, p)]
    for n in (16, 32):
        band = blk(2 * n) & jnp.logical_not(blk(n))
        left = [_mm(y, jnp.where(band, a, 0.0)) for y, a in zip(t, mats)]
        t = [y - _mm(x, y) for y, x in zip(t, left)]
    return t


def _gdn_kernel(qkv_ref, z_ref, ab_ref, abt_ref, dtb_r_ref, dtb_c_ref, alog_r_ref,
                alog_c_ref, nw_ref, tri_ref, trit_ref, o_ref, s_ref):
    n = GDN_TILE

    @pl.when(pl.program_id(1) == 0)
    def _():
        s_ref[...] = jnp.zeros_like(s_ref)

    for sub in range(qkv_ref.shape[0] // n):
        rows = slice(sub * n, (sub + 1) * n)
        _gdn_tile(_silu(qkv_ref[rows, :]), z_ref[rows, :], ab_ref[rows, :], abt_ref[:, rows], dtb_r_ref, dtb_c_ref,
                  alog_r_ref, alog_c_ref, nw_ref, tri_ref, trit_ref, o_ref.at[rows, :], s_ref)


def _gdn_tile(qkv, z, ab, abt, dtb_r_ref, dtb_c_ref, alog_r_ref, alog_c_ref, nw_ref, tri_ref, trit_ref, o_ref, s_ref):
    n = GDN_TILE
    c = GDN_CHUNK
    d = GDN_D
    hd = GDN_HEADS * d
    g = -jnp.exp(alog_r_ref[...]) * _softplus(ab + dtb_r_ref[...])
    gc_full = _spread_heads(_mm_sel_lhs(tri_ref[...], g), 0, GDN_HEADS, d)
    beta_full = _spread_heads(jax.nn.sigmoid(ab), GDN_HEADS, GDN_HEADS, d)
    gt = -jnp.exp(alog_c_ref[...]) * _softplus(abt + dtb_c_ref[...])
    gct = _mm_sel_rhs(gt, trit_ref[...])

    row = lax.broadcasted_iota(jnp.int32, (n, n), 0)
    col = lax.broadcasted_iota(jnp.int32, (n, n), 1)
    same = (row >> (c.bit_length() - 1)) == (col >> (c.bit_length() - 1))
    incl = same & (row >= col)
    strict = same & (row > col)
    zeros_half = jnp.zeros((c, d), F32)

    heads = range(GDN_HEADS)
    sl = [slice(h * d, (h + 1) * d) for h in heads]
    l2n = lambda x: x * lax.rsqrt(jnp.sum(x * x, axis=-1, keepdims=True) + EPS)
    qn = [l2n(qkv[:, sl[h]]) * (d ** -0.5) for h in heads]
    kn = [l2n(qkv[:, hd + h * d:hd + (h + 1) * d]) for h in heads]
    vh = [qkv[:, 2 * hd + h * d:2 * hd + (h + 1) * d] for h in heads]
    gcol = [gc_full[:, sl[h]] for h in heads]
    beta = [beta_full[:, sl[h]] for h in heads]
    edec = [jnp.exp(gcol[h] - gct[h:h + 1, :]) for h in heads]
    egc = [jnp.exp(x) for x in gcol]
    kb = [kn[h] * beta[h] for h in heads]
    on_k = [_mm_nt(jnp.concatenate([kb[h], qn[h]], axis=0), kn[h]) for h in heads]
    lower = [jnp.where(strict, on_k[h][:n] * edec[h], 0.0) for h in heads]
    aqk = [jnp.where(incl, on_k[h][n:] * edec[h], 0.0) for h in heads]
    tinv = _unit_lower_inverse(lower, row, col)
    sol = [_mm(tinv[h], jnp.concatenate([vh[h] * beta[h], kb[h] * egc[h]], axis=1)) for h in heads]
    qd = [qn[h] * egc[h] for h in heads]
    glast = [(gcol[h][c - 1:c, :], gcol[h][n - 1:n, :]) for h in heads]
    kdt = [(kn[h] * jnp.exp(jnp.concatenate([jnp.broadcast_to(glast[h][0], (c, d)),
                                             jnp.broadcast_to(glast[h][1], (c, d))], axis=0) - gcol[h])).T
           for h in heads]
    s0 = [s_ref[h] for h in heads]
    on_s0 = [_mm(jnp.concatenate([sol[h][:c, d:], qd[h][:c]], axis=0), s0[h]) for h in heads]
    v0 = [sol[h][:c, :d] - on_s0[h][:c] for h in heads]
    s1 = [s0[h] * jnp.exp(glast[h][0]) + _mm(kdt[h], jnp.concatenate([v0[h], zeros_half], axis=0)) for h in heads]
    on_s1 = [_mm(jnp.concatenate([sol[h][c:, d:], qd[h][c:]], axis=0), s1[h]) for h in heads]
    v1 = [sol[h][c:, :d] - on_s1[h][:c] for h in heads]
    for h in heads:
        s_ref[h] = s1[h] * jnp.exp(glast[h][1]) + _mm(kdt[h], jnp.concatenate([zeros_half, v1[h]], axis=0))
    outs = []
    for h in heads:
        o = (jnp.concatenate([on_s0[h][c:], on_s1[h][c:]], axis=0)
             + _mm(aqk[h], jnp.concatenate([v0[h], v1[h]], axis=0)))
        o = o * lax.rsqrt(jnp.mean(o * o, axis=-1, keepdims=True) + EPS) * nw_ref[...]
        outs.append(o * _silu(z[:, sl[h]]))
    o_ref[...] = jnp.concatenate(outs, axis=1).astype(o_ref.dtype)


def gated_deltanet_mixer(qkv3, y3, small3, smallt, dt_bias, a_log, norm_w):
    bsz, length, _ = y3.shape
    n = GDN_TILE
    hd = GDN_HEADS * GDN_D
    idx = np.arange(n)
    same = (idx[:, None] // GDN_CHUNK) == (idx[None, :] // GDN_CHUNK)
    tri = jnp.asarray(same & (idx[:, None] >= idx[None, :]), BF16)
    trit = jnp.asarray(same & (idx[:, None] <= idx[None, :]), BF16)
    const = lambda a: pl.BlockSpec(a.shape, lambda b, c: (0,) * a.ndim)
    args = [_pad_lanes(dt_bias), _pad_col(dt_bias), _pad_lanes(a_log), _pad_col(a_log),
            norm_w.reshape(1, -1), tri, trit]
    rows = min(GDN_STEP_ROWS, length)
    return pl.pallas_call(
        _gdn_kernel,
        grid=(bsz, length // rows),
        in_specs=[
            pl.BlockSpec((None, rows, 3 * hd), lambda b, c: (b, c, 0)),
            pl.BlockSpec((None, rows, hd), lambda b, c: (b, c, 0)),
            pl.BlockSpec((None, rows, LANES), lambda b, c: (b, c, 0)),
            pl.BlockSpec((None, 16, rows), lambda b, c: (b, 0, c)),
        ] + [const(a) for a in args],
        out_specs=pl.BlockSpec((None, rows, hd), lambda b, c: (b, c, 0)),
        out_shape=jax.ShapeDtypeStruct((bsz, length, hd), BF16),
        scratch_shapes=[pltpu.VMEM((GDN_HEADS, GDN_D, GDN_D), F32)],
        compiler_params=_cparams("parallel", "arbitrary"),
        name="gated_deltanet",
    )(qkv3, y3, small3, smallt, *args)


def _sb_kernel(q_ref, k_ref, v_ref, upper_ref, o_ref):
    blk = SB_BLOCK
    n_sub = q_ref.shape[0] // blk
    first = pl.program_id(2) * n_sub
    parts = [_sb_query_block(first + s, q_ref[s * blk:(s + 1) * blk, :], k_ref, v_ref, upper_ref[...])
             for s in range(n_sub)]
    for s, finish in enumerate(parts):
        o_ref[s * blk:(s + 1) * blk, :] = finish().astype(o_ref.dtype)


def _sb_query_block(i, q, k_ref, v_ref, upper):
    blk = SB_BLOCK
    pair_w = 2 * SB_HEAD_DIM
    n_pairs = SB_STEP_HEADS // 2
    q = q * (SB_HEAD_DIM ** -0.5)
    lane = lax.broadcasted_iota(jnp.int32, (blk, pair_w), 1)
    first_head = lane < SB_HEAD_DIM
    qs = []
    for p in range(n_pairs):
        q2 = q[:, p * pair_w:(p + 1) * pair_w]
        qs += [jnp.where(first_head, q2, 0.0).astype(BF16), jnp.where(first_head, 0.0, q2).astype(BF16)]
    row = lax.broadcasted_iota(jnp.int32, (blk, blk), 0)
    col = lax.broadcasted_iota(jnp.int32, (blk, blk), 1)
    earlier = col < row
    heads = range(SB_STEP_HEADS)

    def local_part(kb, diagonal, exists=None):
        start = pl.multiple_of(kb * blk, blk)
        k = k_ref[pl.ds(start, blk), :].astype(BF16)
        v = v_ref[pl.ds(start, blk), :].astype(BF16)
        kp = [k[:, p * pair_w:(p + 1) * pair_w] for p in range(n_pairs)]
        vp = [v[:, p * pair_w:(p + 1) * pair_w] for p in range(n_pairs)]
        logits = [lax.dot_general(qs[h], kp[h // 2], (((1,), (1,)), ((), ())), preferred_element_type=F32)
                  for h in heads]
        keep = earlier if diagonal else None
        if exists is not None:
            keep = exists if keep is None else keep & exists
        log_keep = [-_softplus(x) for x in logits]
        if keep is not None:
            log_keep = [jnp.where(keep, x, 0.0) for x in log_keep]
        inside = [_mm_sel_rhs(x, upper, 2) for x in log_keep]
        totals = [jnp.sum(x, axis=-1, keepdims=True) for x in log_keep]
        return logits, log_keep, inside, totals, vp, keep

    def carried_part(local, accs, sticks):
        logits, log_keep, inside, totals, vp, keep = local
        w = [jnp.exp(logits[h] + log_keep[h] + inside[h] + sticks[h]) for h in heads]
        if keep is not None:
            w = [jnp.where(keep, x, 0.0) for x in w]
        pv = [jnp.dot(w[h].astype(BF16), vp[h // 2], preferred_element_type=F32) for h in heads]
        accs = tuple(accs[p] + jnp.where(first_head, pv[2 * p], pv[2 * p + 1]) for p in range(n_pairs))
        sticks = tuple(sticks[h] + totals[h] for h in heads)
        return accs, sticks

    accs = tuple(jnp.zeros((blk, pair_w), F32) for _ in range(n_pairs))
    sticks = tuple(jnp.zeros((blk, 1), F32) for _ in heads)
    eager = [local_part(i, True)]
    for back in range(1, SB_EAGER_BLOCKS + 1):
        eager.append(local_part(jnp.maximum(i - back, 0), False, exists=(row >= 0) & (i - back >= 0)))
    for local in eager:
        accs, sticks = carried_part(local, accs, sticks)

    def alive(state):
        kb, _, sticks = state
        longest = sticks[0]
        for s in sticks[1:]:
            longest = jnp.maximum(longest, s)
        return (kb >= 0) & (jnp.max(longest) > SB_LOG_ZERO)

    def body(state):
        kb, accs, sticks = state
        accs, sticks = carried_part(local_part(kb, False), accs, sticks)
        return kb - 1, accs, sticks

    def finish():
        _, done, _ = lax.while_loop(alive, body, (i - 1 - SB_EAGER_BLOCKS, accs, sticks))
        return jnp.concatenate(done, axis=1)

    return finish


def stick_breaking_mixer(y3, col0):
    bsz, length, _ = y3.shape
    blk = SB_BLOCK
    step_w = SB_STEP_HEADS * SB_HEAD_DIM
    steps = SB_DIM // step_w
    q0 = col0 // step_w
    idx = np.arange(blk)
    upper = jnp.asarray(idx[:, None] > idx[None, :], BF16)
    resident = lambda off: pl.BlockSpec((None, length, step_w), lambda b, p, i: (b, 0, q0 + off + p),
                                        pipeline_mode=pl.Buffered(1))
    rows = min(SB_STEP_ROWS, length)
    return pl.pallas_call(
        _sb_kernel,
        grid=(bsz, steps, length // rows),
        in_specs=[
            pl.BlockSpec((None, rows, step_w), lambda b, p, i: (b, i, q0 + p)),
            resident(steps),
            resident(2 * steps),
            pl.BlockSpec((blk, blk), lambda b, p, i: (0, 0)),
        ],
        out_specs=pl.BlockSpec((None, rows, step_w), lambda b, p, i: (b, i, p)),
        out_shape=jax.ShapeDtypeStruct((bsz, length, SB_DIM), BF16),
        compiler_params=_cparams("parallel", "parallel", "arbitrary"),
        name="stick_breaking",
    )(y3, y3, y3, upper)


def _mixer_out(a_ref, b_ref, h_ref, wa_ref, wb_ref, rows):
    return h_ref[rows, :] + (jnp.dot(a_ref[rows, :].astype(BF16), wa_ref[...], preferred_element_type=F32)
                             + jnp.dot(b_ref[rows, :].astype(BF16), wb_ref[...], preferred_element_type=F32))


def _cross_attention(h, g_ref, wq_ref, kt_ref, v_ref, wo_ref):
    u = _rms(h, g_ref[...]).astype(BF16)
    q = jnp.dot(u, wq_ref[...], preferred_element_type=F32)
    heads = []
    for hd in range(XA_HEADS):
        sl = slice(hd * XA_HEAD_DIM, (hd + 1) * XA_HEAD_DIM)
        s = jnp.dot(q[:, sl].astype(BF16), kt_ref[sl, :], preferred_element_type=F32)
        p = jnp.exp(s - jnp.max(s, axis=-1, keepdims=True))
        p = p * (1.0 / jnp.sum(p, axis=-1, keepdims=True))
        heads.append(jnp.dot(p.astype(BF16), v_ref[:, sl], preferred_element_type=F32))
    o = jnp.concatenate(heads, axis=1).astype(BF16)
    return h + jnp.dot(o, wo_ref[...], preferred_element_type=F32)


def _route(xn, whi_ref, wlo_ref, b_ref, before_ref, run_ref):
    x_hi = xn.astype(BF16)
    x_lo = (xn - x_hi.astype(F32)).astype(BF16)
    wide = jnp.dot(x_hi, jnp.concatenate([whi_ref[...], wlo_ref[...]], axis=1), preferred_element_type=F32)
    logits = (wide[:, :LANES] + jnp.dot(x_lo, whi_ref[...], preferred_element_type=F32)
              + wide[:, LANES:] + b_ref[...])
    lane = lax.broadcasted_iota(jnp.int32, logits.shape, 1).astype(F32)
    neg = -1e30
    none = float(LANES)

    def top(vals):
        best = jnp.max(vals, axis=-1, keepdims=True)
        where = jnp.min(jnp.where(vals == best, lane, none), axis=-1, keepdims=True)
        return best, where

    gl = jnp.where(lane < MOE_GROUPS, logits, neg)
    gbest, gsel = top(gl)
    gprob = 1.0 / jnp.sum(jnp.exp(gl - gbest), axis=-1, keepdims=True)
    lo = MOE_GROUPS + gsel * MOE_PER_GROUP
    el = jnp.where((lane >= lo) & (lane < lo + MOE_PER_GROUP), logits, neg)
    m1, i1 = top(el)
    m2, i2 = top(jnp.where(lane == i1, neg, el))
    e = jnp.exp(m2 - m1)
    gate1 = gprob / (1.0 + e)
    gate2 = gprob * e / (1.0 + e)

    hot1 = lane == i1
    hot2 = lane == i2
    one1 = jnp.where(hot1, 1.0, 0.0)
    one2 = jnp.where(hot2, 1.0, 0.0)
    prefix = jnp.dot(before_ref[...], jnp.concatenate([one1, one2], axis=1).astype(BF16), preferred_element_type=F32)
    prefix1, prefix2 = prefix[:, :LANES], prefix[:, LANES:]
    total1 = jnp.sum(one1, axis=0, keepdims=True)
    running = run_ref[...]
    rank1 = jnp.sum(jnp.where(hot1, prefix1 + running, 0.0), axis=-1, keepdims=True)
    rank2 = jnp.sum(jnp.where(hot2, prefix2 + (running + total1), 0.0), axis=-1, keepdims=True)
    running = running + total1 + jnp.sum(one2, axis=0, keepdims=True)
    run_ref[...] = running

    fields = (i1 - MOE_GROUPS, i2 - MOE_GROUPS, gate1, gate2, rank1, rank2)
    out = jnp.zeros_like(logits)
    for k, val in enumerate(fields):
        out = jnp.where(lane == k, val, out)
    return out


def _post_mixer_kernel(a_ref, b_ref, h_ref, wa_ref, wb_ref, gxa_ref, wq_ref, kt_ref, v_ref, wo_ref,
                       gffn_ref, whi_ref, wlo_ref, bias_ref, before_ref,
                       h_out_ref, xn_ref, r_ref, cnt_ref, run_ref):
    @pl.when(pl.program_id(0) == 0)
    def _():
        run_ref[...] = jnp.zeros_like(run_ref)

    h = _mixer_out(a_ref, b_ref, h_ref, wa_ref, wb_ref, slice(None))
    h = _cross_attention(h, gxa_ref, wq_ref, kt_ref, v_ref, wo_ref)
    h_out_ref[...] = h
    xn = _rms(h, gffn_ref[...])
    xn_ref[...] = _pack_halves(xn)
    group = before_ref.shape[0]
    for start in range(0, h_ref.shape[0], group):
        rows = slice(start, start + group)
        r_ref[rows, :] = _route(xn[rows, :], whi_ref, wlo_ref, bias_ref, before_ref, run_ref)
    cnt_ref[...] = run_ref[...]


def post_mixer(ya, yb, h, wa, wb, g_xa, wq, kt, v, wo, g_ffn, w_hi, w_lo, bias, tm=1024):
    m, d = h.shape
    tiles_per_batch = m // kt.shape[0] // tm
    idx = np.arange(min(POST_GROUP_ROWS, tm))
    before = jnp.asarray(idx[:, None] > idx[None, :], BF16)
    rows = lambda w: pl.BlockSpec((tm, w), lambda i: (i, 0))
    const = lambda a: pl.BlockSpec(a.shape, lambda i: (0,) * a.ndim, pipeline_mode=pl.Buffered(1))
    per_batch = lambda a: pl.BlockSpec((None,) + a.shape[1:], lambda i: (i // tiles_per_batch, 0, 0))
    g_xa, g_ffn = g_xa.reshape(1, d), g_ffn.reshape(1, d)
    return pl.pallas_call(
        _post_mixer_kernel,
        grid=(m // tm,),
        in_specs=[rows(ya.shape[1]), rows(yb.shape[1]), rows(d), const(wa), const(wb), const(g_xa), const(wq),
                  per_batch(kt), per_batch(v), const(wo), const(g_ffn), const(w_hi), const(w_lo), const(bias),
                  const(before)],
        out_specs=[rows(d), rows(d // 2), rows(LANES), pl.BlockSpec((1, LANES), lambda i: (0, 0))],
        out_shape=[jax.ShapeDtypeStruct((m, d), F32), jax.ShapeDtypeStruct((m, d // 2), jnp.int32),
                   jax.ShapeDtypeStruct((m, LANES), F32), jax.ShapeDtypeStruct((1, LANES), F32)],
        scratch_shapes=[pltpu.VMEM((1, LANES), F32)],
        compiler_params=_cparams("arbitrary"),
        name="post_mixer",
    )(ya, yb, h, wa, wb, g_xa, wq, kt, v, wo, g_ffn, w_hi, w_lo, bias, before)


def _expert_kernel(table_ref, x_ref, wg_hbm, wu_hbm, wd_hbm, o_ref,
                   wg32_ref, wu32_ref, wd32_ref, wgb_ref, wub_ref, wdb_ref, sem_ref, *, layer):
    i = pl.program_id(0)
    beid_ref, valid_ref, first_ref, slot_ref, next_ref = (table_ref.at[k] for k in range(5))
    valid = valid_ref[i]

    def weight_copies(expert, slot):
        return (pltpu.make_async_copy(wg_hbm.at[layer, expert], wg32_ref.at[slot], sem_ref.at[slot, 0]),
                pltpu.make_async_copy(wu_hbm.at[layer, expert], wu32_ref.at[slot], sem_ref.at[slot, 1]),
                pltpu.make_async_copy(wd_hbm.at[layer, expert], wd32_ref.at[slot], sem_ref.at[slot, 2]))

    @pl.when(i == 0)
    def _():
        for copy in weight_copies(beid_ref[0], 0):
            copy.start()

    @pl.when(first_ref[i] == 1)
    def _():
        slot = slot_ref[i]
        for copy in weight_copies(beid_ref[i], slot):
            copy.wait()
        wgb_ref[...] = wg32_ref[slot].astype(BF16)
        wub_ref[...] = wu32_ref[slot].astype(BF16)
        wdb_ref[...] = wd32_ref[slot].astype(BF16)

        @pl.when(next_ref[i] >= 0)
        def _():
            for copy in weight_copies(next_ref[i], 1 - slot):
                copy.start()

    half = MOE_ROWS // 2

    def ffn(n_halves):
        row = lax.broadcasted_iota(jnp.int32, (half, 2 * x_ref.shape[1]), 0)
        xs = [jnp.where(row + k * half < valid, _unpack_halves(x_ref[k * half:(k + 1) * half, :]), 0.0).astype(BF16)
              for k in range(n_halves)]
        gates = [jnp.dot(x, wgb_ref[...], preferred_element_type=F32) for x in xs]
        ups = [jnp.dot(x, wub_ref[...], preferred_element_type=F32) for x in xs]
        acts = [(_silu(g) * u).astype(BF16) for g, u in zip(gates, ups)]
        for k, act in enumerate(acts):
            o_ref[k * half:(k + 1) * half, :] = _pack_halves(jnp.dot(act, wdb_ref[...], preferred_element_type=F32))

    @pl.when(valid > half)
    def _():
        ffn(2)

    @pl.when((valid > 0) & (valid <= half))
    def _():
        ffn(1)
        o_ref[half:, :] = jnp.zeros((half, o_ref.shape[1]), o_ref.dtype)

    @pl.when(valid == 0)
    def _():
        o_ref[...] = jnp.zeros_like(o_ref)


def moe_experts(blocks, xs, w_gate, w_up, w_down, layer):
    n_slots, packed = xs.shape
    d = 2 * packed
    rows = MOE_ROWS
    ff = w_gate.shape[3]
    grid_spec = pltpu.PrefetchScalarGridSpec(
        num_scalar_prefetch=1,
        grid=(n_slots // rows,),
        in_specs=[
            pl.BlockSpec((rows, packed), lambda i, *_: (i, 0)),
            pl.BlockSpec(memory_space=pl.ANY),
            pl.BlockSpec(memory_space=pl.ANY),
            pl.BlockSpec(memory_space=pl.ANY),
        ],
        out_specs=pl.BlockSpec((rows, packed), lambda i, *_: (i, 0)),
        scratch_shapes=[pltpu.VMEM((2, d, ff), F32), pltpu.VMEM((2, d, ff), F32), pltpu.VMEM((2, ff, d), F32),
                        pltpu.VMEM((d, ff), BF16), pltpu.VMEM((d, ff), BF16), pltpu.VMEM((ff, d), BF16),
                        pltpu.SemaphoreType.DMA((2, 3))],
    )
    return pl.pallas_call(
        functools.partial(_expert_kernel, layer=layer),
        grid_spec=grid_spec,
        out_shape=jax.ShapeDtypeStruct((n_slots, packed), jnp.int32),
        compiler_params=_cparams("arbitrary"),
        name="moe_experts",
    )(blocks, xs, w_gate, w_up, w_down)


def _sc_mesh():
    return plsc.VectorSubcoreMesh(core_axis_name="c", subcore_axis_name="s",
                                  num_cores=SC_CORES, num_subcores=SC_SUBCORES)


def _sc_worker():
    return lax.axis_index("s") * SC_CORES + lax.axis_index("c")


def _sc_double_buffered(n_chunks, fetch, drain):
    assert n_chunks % 2 == 0
    start = lambda copies: [c.start() for c in copies]
    wait = lambda copies: [c.wait() for c in copies]
    start(fetch(0, 0))

    @pl.loop(0, n_chunks, step=2)
    def _(j):
        wait(fetch(j, 0))

        @pl.when(j > 0)
        def _():
            wait(drain(j - 1, 1))

        start(fetch(j + 1, 1))
        start(drain(j, 0))
        wait(fetch(j + 1, 1))
        wait(drain(j, 0))

        @pl.when(j + 2 < n_chunks)
        def _():
            start(fetch(j + 2, 0))

        start(drain(j + 1, 1))

    wait(drain(n_chunks - 1, 1))


def sc_scatter_rows(x, dest, n_slots):
    n_tok, d = x.shape
    per_worker = n_tok // SC_WORKERS
    n_chunks = per_worker // SC_CHUNK
    by_worker = dest.reshape(dest.shape[0] * SC_WORKERS, n_chunks, SC_CHUNK)

    @functools.partial(
        pl.kernel, mesh=_sc_mesh(), out_type=jax.ShapeDtypeStruct((n_slots, d), x.dtype),
        scratch_types=[pltpu.VMEM((n_chunks, SC_CHUNK), jnp.int32), pltpu.VMEM((n_chunks, SC_CHUNK), jnp.int32),
                       pltpu.VMEM((2, SC_CHUNK, d), x.dtype), pltpu.SemaphoreType.DMA((2, 3))],
        name="moe_scatter_rows")
    def scatter(x_hbm, dest_hbm, out_hbm, i0_v, i1_v, rows_v, sem):
        wid = _sc_worker()
        pltpu.sync_copy(dest_hbm.at[wid], i0_v)
        pltpu.sync_copy(dest_hbm.at[SC_WORKERS + wid], i1_v)

        def fetch(j, buf):
            start = pl.multiple_of(wid * per_worker + j * SC_CHUNK, SC_CHUNK)
            return [pltpu.make_async_copy(x_hbm.at[pl.ds(start, SC_CHUNK)], rows_v.at[buf], sem.at[buf, 0])]

        def drain(j, buf):
            return [pltpu.make_async_copy(rows_v.at[buf], out_hbm.at[i0_v.at[j]], sem.at[buf, 1]),
                    pltpu.make_async_copy(rows_v.at[buf], out_hbm.at[i1_v.at[j]], sem.at[buf, 2])]

        _sc_double_buffered(n_chunks, fetch, drain)

    return scatter(x, by_worker)


def sc_gather_rows(table, idx, n_out):
    d = table.shape[1]
    per_worker = n_out // SC_WORKERS
    n_chunks = per_worker // SC_CHUNK

    @functools.partial(
        pl.kernel, mesh=_sc_mesh(), out_type=jax.ShapeDtypeStruct((n_out, d), table.dtype),
        scratch_types=[pltpu.VMEM((n_chunks, SC_CHUNK), jnp.int32), pltpu.VMEM((2, SC_CHUNK, d), table.dtype),
                       pltpu.SemaphoreType.DMA((2, 2))],
        name="moe_gather_rows")
    def gather(table_hbm, idx_hbm, out_hbm, idx_v, rows_v, sem):
        wid = _sc_worker()
        pltpu.sync_copy(idx_hbm.at[wid], idx_v)

        def fetch(j, buf):
            return [pltpu.make_async_copy(table_hbm.at[idx_v.at[j]], rows_v.at[buf], sem.at[buf, 0])]

        def drain(j, buf):
            start = pl.multiple_of(wid * per_worker + j * SC_CHUNK, SC_CHUNK)
            return [pltpu.make_async_copy(rows_v.at[buf], out_hbm.at[pl.ds(start, SC_CHUNK)], sem.at[buf, 1])]

        _sc_double_buffered(n_chunks, fetch, drain)

    return gather(table, idx.reshape(-1, n_chunks, SC_CHUNK))


def _combine_kernel(h_ref, y0_ref, y1_ref, r_ref, g_ref, o_ref, *, final_norm):
    route = r_ref[...]
    h = h_ref[...] + (route[:, 2:3] * _unpack_halves(y0_ref[...]) + route[:, 3:4] * _unpack_halves(y1_ref[...]))
    o_ref[...] = _rms(h, g_ref[...]) if final_norm else h


def moe_combine(h, y01, route, g, final_norm, tm=1024):
    m, d = h.shape
    tm = min(tm, m)
    rows = lambda w: pl.BlockSpec((tm, w), lambda i: (i, 0))
    return pl.pallas_call(
        functools.partial(_combine_kernel, final_norm=final_norm),
        grid=(m // tm,),
        in_specs=[rows(d), rows(d // 2), pl.BlockSpec((tm, d // 2), lambda i: (i + m // tm, 0)), rows(LANES),
                  pl.BlockSpec((1, d), lambda i: (0, 0))],
        out_specs=rows(d),
        out_shape=jax.ShapeDtypeStruct((m, d), F32),
        compiler_params=_cparams("parallel"),
        name="moe_combine",
    )(h, y01, y01, route, g.reshape(1, d))


def _pad_cols(w):
    return jnp.pad(w, ((0, 0), (0, LANES - w.shape[1])))


def _plan_kernel(route_ref, cnt_ref, incl_ref, dest_ref, table_ref):
    f32_sum = lambda x, axis: jnp.sum(x, axis=axis, keepdims=True)
    lane = lax.broadcasted_iota(jnp.int32, (LANES, LANES), 1)
    sub = lax.broadcasted_iota(jnp.int32, (LANES, LANES), 0)
    incl = incl_ref[...]
    is_expert = (lane >= MOE_GROUPS) & (lane < MOE_GROUPS + MOE_EXPERTS)
    shift = MOE_ROWS.bit_length() - 1
    counts = jnp.broadcast_to(cnt_ref[...], (LANES, LANES)).astype(jnp.int32)
    padded = jnp.where(is_expert, ((counts + (MOE_ROWS - 1)) >> shift) << shift, 0)
    pad_end = _mm_sel_rhs(padded.astype(F32), incl)
    pad_start = pad_end - padded.astype(F32)

    route = route_ref[...]
    lane_t = lax.broadcasted_iota(jnp.int32, route.shape, 1)
    lane_f = lane_t.astype(F32)
    start_row = pad_start[0:1, :]
    slots = [f32_sum(jnp.where(lane_f == route[:, k:k + 1] + MOE_GROUPS, start_row, 0.0), 1) + route[:, 4 + k:5 + k]
             for k in range(2)]
    both = jnp.where(lane_t == 0, slots[0], jnp.where(lane_t == 1, slots[1], 0.0))
    dest_ref[...] = both.T[0:8, :].astype(jnp.int32)

    on_sub = lambda rows_equal: rows_equal.T
    expert_sub = (sub >= MOE_GROUPS) & (sub < MOE_GROUPS + MOE_EXPERTS)
    block_start = (lane * MOE_ROWS).astype(F32)
    eid = f32_sum(jnp.where(expert_sub & (on_sub(pad_end) <= block_start), 1.0, 0.0), 0)
    eid = jnp.minimum(eid, float(MOE_EXPERTS - 1))
    filled = on_sub(pad_start + counts.astype(F32))
    own = (sub - MOE_GROUPS).astype(F32) == eid
    valid = jnp.clip(f32_sum(jnp.where(own, filled, 0.0), 0) - block_start[0:1, :], 0.0, float(MOE_ROWS))
    eid_rows = jnp.broadcast_to(eid, (LANES, LANES))
    changed = (lane == 0) | (eid_rows != pltpu.roll(eid_rows, 1, axis=1))
    first = jnp.where((jnp.broadcast_to(valid, (LANES, LANES)) > 0) & changed, 1.0, 0.0)
    ordinal = _mm_sel_rhs(first, incl) - 1.0
    slot = ordinal - 2.0 * jnp.floor(ordinal * 0.5)
    later = (on_sub(first) > 0) & (sub > lane)
    nearest = jnp.min(jnp.where(later, sub, LANES), axis=0, keepdims=True)
    next_eid = f32_sum(jnp.where(sub == nearest, on_sub(eid_rows), 0.0), 0)
    next_eid = jnp.where(nearest < LANES, next_eid, -1.0)
    row8 = lax.broadcasted_iota(jnp.int32, (8, LANES), 0)
    table = jnp.zeros((8, LANES), F32)
    for k, val in enumerate((eid, valid, first[0:1, :], slot[0:1, :], next_eid)):
        table = jnp.where(row8 == k, val, table)
    table_ref[...] = table.astype(jnp.int32)


def moe_plan(route, counts, tm=4096):
    n_tok = route.shape[0]
    tm = min(tm, n_tok)
    idx = np.arange(LANES)
    incl = jnp.asarray(idx[:, None] <= idx[None, :], BF16)
    return pl.pallas_call(
        _plan_kernel,
        grid=(n_tok // tm,),
        in_specs=[pl.BlockSpec((tm, LANES), lambda i: (i, 0)), pl.BlockSpec((1, LANES), lambda i: (0, 0)),
                  pl.BlockSpec((LANES, LANES), lambda i: (0, 0))],
        out_specs=[pl.BlockSpec((8, tm), lambda i: (0, i)), pl.BlockSpec((8, LANES), lambda i: (0, 0))],
        out_shape=[jax.ShapeDtypeStruct((8, n_tok), jnp.int32), jax.ShapeDtypeStruct((8, LANES), jnp.int32)],
        compiler_params=_cparams("arbitrary"),
        name="moe_plan",
    )(route, counts, incl)


def _router_weights(w_group, b_group, w_expert, b_expert):
    w_r = _pad_cols(jnp.concatenate([w_group, w_expert], axis=1))
    w_hi = w_r.astype(BF16)
    w_lo = (w_r - w_hi.astype(F32)).astype(BF16)
    return w_hi, w_lo, _pad_lanes(jnp.concatenate([b_group, b_expert]))


def _moe_layer(h, xn, route, counts, w_gate, w_up, w_down, layer, final_g):
    n_tok, d = h.shape
    n_blocks = -(-(2 * n_tok + MOE_EXPERTS * (MOE_ROWS - 1)) // MOE_ROWS)
    dest, blocks = moe_plan(route, counts)
    xs = sc_scatter_rows(xn, dest, n_blocks * MOE_ROWS)
    ys = moe_experts(blocks, xs, w_gate, w_up, w_down, layer)
    y01 = sc_gather_rows(ys, dest, 2 * n_tok)
    g = jnp.ones((d,), F32) if final_g is None else final_g
    return moe_combine(h, y01, route, g, final_g is not None)


def _memory_kv(memn_in, mem_norm, wk, wv):
    bsz, m, d = memn_in.shape
    w = jnp.concatenate([wk, wv], axis=1).astype(BF16)
    kv, _ = rms_matmul(memn_in.reshape(bsz * m, d), mem_norm, w, jnp.zeros((d, LANES), BF16))
    k = kv[:, :d].reshape(bsz, m, d)
    v = kv[:, d:].reshape(bsz, m, d)
    return (jnp.swapaxes(k, 1, 2) * XA_HEAD_DIM ** -0.5).astype(BF16), v.astype(BF16)


def kernel(x, mem, mem_norm, final_norm, norm_mix, norm_xa, norm_ffn, xa_wq, xa_wk, xa_wv, xa_wo, moe_w_group, moe_b_group, moe_w_expert, moe_b_expert, moe_w_gate, moe_w_up, moe_w_down, ev_w_in, ev_sc_conv, ev_ssm_conv_w, ev_ssm_conv_b, ev_ssm_dt_bias, ev_ssm_a_log, ev_ssm_d, ev_ssm_norm, ev_w_out, od_w_in, od_gdn_conv, od_gdn_dt_bias, od_gdn_a_log, od_gdn_norm, od_w_out):
    bsz, length, d = x.shape
    n_tok = bsz * length
    depth = norm_mix.shape[0]
    h = x.reshape(n_tok, d)
    for layer in range(depth):
        i = layer // 2
        if layer % 2 == 0:
            w = ev_w_in[i]
            z0 = 3 * SC_DIM
            xbc0 = z0 + SSM_INNER
            w_conv = w[:, xbc0:xbc0 + SSM_XBC].astype(BF16)
            w_small = _pad_cols(w[:, xbc0 + SSM_XBC:]).astype(BF16)
            xbc = rms_matmul_conv(h, norm_mix[layer], w_conv, ev_ssm_conv_w[i], ev_ssm_conv_b[i], length)
            z, ya, small = rms_matmul_gated(h, norm_mix[layer], w[:, z0:xbc0].astype(BF16), w[:, :z0].astype(BF16),
                                            w_small, ev_sc_conv[i], length)
            small3 = small.reshape(bsz, length, LANES)
            smallt = jnp.swapaxes(small3[:, :, :16], 1, 2)
            yb = ssd_mixer(xbc.reshape(bsz, length, -1), z.reshape(bsz, length, -1), small3, smallt,
                           ev_ssm_dt_bias[i], ev_ssm_a_log[i], ev_ssm_d[i], ev_ssm_norm[i])
            w_out = ev_w_out[i].astype(BF16)
            split = SC_DIM
        else:
            w = od_w_in[i]
            qkv_w = 3 * GDN_HEADS * GDN_D
            z_end = qkv_w + GDN_HEADS * GDN_D
            w_conv = w[:, :qkv_w].astype(BF16)
            w_main = jnp.concatenate([w[:, qkv_w:z_end], w[:, z_end + 2 * GDN_HEADS:]], axis=1).astype(BF16)
            w_small = _pad_cols(w[:, z_end:z_end + 2 * GDN_HEADS]).astype(BF16)
            qkv = rms_matmul_conv(h, norm_mix[layer], w_conv, od_gdn_conv[i], jnp.zeros((qkv_w,), F32), length)
            y, small = rms_matmul(h, norm_mix[layer], w_main, w_small, tm=1024, tn=w_main.shape[1])
            y3 = y.reshape(bsz, length, -1)
            small3 = small.reshape(bsz, length, LANES)
            smallt = jnp.swapaxes(small3[:, :, :16], 1, 2)
            ya = gated_deltanet_mixer(qkv.reshape(bsz, length, -1), y3, small3, smallt, od_gdn_dt_bias[i],
                                      od_gdn_a_log[i], od_gdn_norm[i])
            yb = stick_breaking_mixer(y3, GDN_HEADS * GDN_D)
            w_out = od_w_out[i].astype(BF16)
            split = GDN_HEADS * GDN_D
        kt, v = _memory_kv(mem, mem_norm, xa_wk[layer], xa_wv[layer])
        w_hi, w_lo, bias = _router_weights(moe_w_group[layer], moe_b_group[layer], moe_w_expert[layer],
                                           moe_b_expert[layer])
        h, xn, route, counts = post_mixer(
            ya.reshape(n_tok, -1), yb.reshape(n_tok, -1), h, w_out[:split], w_out[split:], norm_xa[layer],
            xa_wq[layer].astype(BF16), kt, v, xa_wo[layer].astype(BF16), norm_ffn[layer], w_hi, w_lo, bias)
        h = _moe_layer(h, xn, route, counts, moe_w_gate, moe_w_up, moe_w_down, layer,
                       final_norm if layer == depth - 1 else None)
    return h.reshape(bsz, length, d)
```

```python
import functools

import jax
import jax.numpy as jnp
import numpy as np
from jax import lax
from jax.experimental import pallas as pl
from jax.experimental.pallas import tpu as pltpu
from jax.experimental.pallas import tpu_sc as plsc

F32 = jnp.float32
BF16 = jnp.bfloat16
EPS = 1e-6

SC_DIM = 512
SSM_HEADS = 16
SSM_HEAD_DIM = 64
SSM_INNER = 1024
SSM_GROUPS = 2
SSM_STATE = 128
SSM_XBC = SSM_INNER + 2 * SSM_GROUPS * SSM_STATE
SSD_CHUNK = 128
SSD_STEP_ROWS = 512
GDN_HEADS = 8
GDN_D = 128
GDN_CHUNK = 64
GDN_TILE = 128
GDN_STEP_ROWS = 512
SB_HEAD_DIM = 64
SB_DIM = 512
SB_BLOCK = 128
SB_STEP_HEADS = 8
SB_STEP_ROWS = 512
SB_EAGER_BLOCKS = 2
XA_HEADS = 4
XA_HEAD_DIM = 256
MOE_GROUPS = 4
MOE_PER_GROUP = 8
MOE_EXPERTS = 32
MOE_ROWS = 512
POST_GROUP_ROWS = 512
SC_CORES = 2
SC_SUBCORES = 16
SC_WORKERS = SC_CORES * SC_SUBCORES
SC_CHUNK = 64
HALO = 8
CONV_CHUNK = 512
LANES = 128
SB_LOG_ZERO = -104.0
VMEM_LIMIT = 56 * 1024 * 1024


def _cparams(*sem):
    return pltpu.CompilerParams(dimension_semantics=sem, vmem_limit_bytes=VMEM_LIMIT)


def _mm(a, b):
    return jnp.dot(a.astype(BF16), b.astype(BF16), preferred_element_type=F32)


def _mm_nt(a, b):
    return lax.dot_general(a.astype(BF16), b.astype(BF16), (((1,), (1,)), ((), ())),
                           preferred_element_type=F32)


def _split_bf16(x, n):
    parts, r = [], x
    for _ in range(n):
        p = r.astype(BF16)
        parts.append(p)
        r = r - p.astype(F32)
    return parts


def _mm_sel_rhs(x, sel, n=3):
    return sum(jnp.dot(p, sel, preferred_element_type=F32) for p in _split_bf16(x, n))


def _mm_sel_lhs(sel, x, n=3):
    return sum(jnp.dot(sel, p, preferred_element_type=F32) for p in _split_bf16(x, n))


def _spread_heads(x, first, n_heads, width):
    rows = x.shape[0]
    col = lambda h: jnp.broadcast_to(x[:, first + h:first + h + 1], (rows, LANES))
    if width == LANES:
        return jnp.concatenate([col(h) for h in range(n_heads)], axis=1)
    left = lax.broadcasted_iota(jnp.int32, (rows, LANES), 1) < width
    return jnp.concatenate([jnp.where(left, col(h), col(h + 1)) for h in range(0, n_heads, 2)], axis=1)


def _pack_halves(x):
    n = x.shape[1] // 2
    lo = pltpu.bitcast(x[:, :n].astype(BF16).astype(F32), jnp.int32)
    hi = pltpu.bitcast(x[:, n:].astype(BF16).astype(F32), jnp.int32)
    return lax.shift_right_logical(lo, 16) | (hi & jnp.int32(-65536))


def _unpack_halves(p):
    lo = pltpu.bitcast(lax.shift_left(p, 16), F32)
    hi = pltpu.bitcast(p & jnp.int32(-65536), F32)
    return jnp.concatenate([lo, hi], axis=1)


def _silu(x):
    half = 0.5 * x
    return half * jnp.tanh(half) + half


def _softplus(x):
    return jnp.maximum(x, 0.0) + jnp.log(1.0 + jnp.exp(-jnp.abs(x)))


def _rms(x, g):
    return x * lax.rsqrt(jnp.mean(x * x, axis=-1, keepdims=True) + EPS) * g


def _rms_matmul_kernel(x_ref, g_ref, w_ref, ws_ref, o_ref, os_ref):
    xn = _rms(x_ref[...], g_ref[...]).astype(BF16)
    o_ref[...] = _mm_nt(xn, w_ref[...])
    os_ref[...] = _mm_nt(xn, ws_ref[...])


def rms_matmul(x, g, w, ws, tm=512, tn=512):
    m, k = x.shape
    n = w.shape[0]
    tm = min(tm, m)
    main, small = pl.pallas_call(
        _rms_matmul_kernel,
        grid=(n // tn, m // tm),
        in_specs=[
            pl.BlockSpec((tm, k), lambda j, i: (i, 0)),
            pl.BlockSpec((1, k), lambda j, i: (0, 0)),
            pl.BlockSpec((tn, k), lambda j, i: (j, 0)),
            pl.BlockSpec((LANES, k), lambda j, i: (0, 0)),
        ],
        out_specs=[
            pl.BlockSpec((tm, tn), lambda j, i: (i, j)),
            pl.BlockSpec((None, tm, LANES), lambda j, i: (j, i, 0)),
        ],
        out_shape=[jax.ShapeDtypeStruct((m, n), F32), jax.ShapeDtypeStruct((n // tn, m, LANES), F32)],
        compiler_params=_cparams("parallel", "parallel"),
        name="rms_matmul",
    )(x, g.reshape(1, k), w, ws)
    return main, small[0]


def _causal_conv(ext_ref, w_ref, rows):
    width = w_ref.shape[0]
    ext = ext_ref[...]
    acc = None
    for j in range(width):
        shift = width - 1 - j
        moved = ext if shift == 0 else pltpu.roll(ext, shift, axis=0)
        term = w_ref[j:j + 1, :] * moved[HALO:HALO + rows, :]
        acc = term if acc is None else acc + term
    return acc


def _rms_matmul_conv_kernel(x_ref, g_ref, w_ref, cw_ref, cb_ref, o_ref, *ext_refs, tiles_per_seq):
    tm = x_ref.shape[0]
    starts_sequence = pl.program_id(1) % tiles_per_seq == 0

    @pl.when(starts_sequence)
    def _():
        for ext_ref in ext_refs:
            ext_ref[0:HALO, :] = jnp.zeros((HALO, CONV_CHUNK), F32)

    @pl.when(jnp.logical_not(starts_sequence))
    def _():
        for ext_ref in ext_refs:
            ext_ref[0:HALO, :] = ext_ref[tm:tm + HALO, :]

    xn = _rms(x_ref[...], g_ref[...]).astype(BF16)
    for c, ext_ref in enumerate(ext_refs):
        cols = slice(c * CONV_CHUNK, (c + 1) * CONV_CHUNK)
        ext_ref[HALO:, :] = _mm_nt(xn, w_ref[cols, :])
        o_ref[:, cols] = _causal_conv(ext_ref, cw_ref.at[:, cols], tm) + cb_ref[:, cols]


def rms_matmul_conv(x, g, w, conv_w, conv_b, seq_len, tm=1024, tn=1536):
    m, k = x.shape
    n = w.shape[0]
    cols = lambda rows: pl.BlockSpec((rows, tn), lambda j, i: (0, j))
    return pl.pallas_call(
        functools.partial(_rms_matmul_conv_kernel, tiles_per_seq=seq_len // tm),
        grid=(n // tn, m // tm),
        in_specs=[
            pl.BlockSpec((tm, k), lambda j, i: (i, 0)),
            pl.BlockSpec((1, k), lambda j, i: (0, 0)),
            pl.BlockSpec((tn, k), lambda j, i: (j, 0)), cols(conv_w.shape[0]), cols(1),
        ],
        out_specs=pl.BlockSpec((tm, tn), lambda j, i: (i, j)),
        out_shape=jax.ShapeDtypeStruct((m, n), F32),
        scratch_shapes=[pltpu.VMEM((tm + HALO, CONV_CHUNK), F32)] * (tn // CONV_CHUNK),
        compiler_params=_cparams("arbitrary", "arbitrary"),
        name="rms_matmul_conv",
    )(x, g.reshape(1, k), w, conv_w, conv_b.reshape(1, n))


def _rms_matmul_gated_kernel(x_ref, g_ref, wz_ref, wbcx_ref, ws_ref, cw_ref, z_ref, ya_ref, os_ref, ext_ref,
                             *, tiles_per_seq):
    tm = x_ref.shape[0]
    starts_sequence = pl.program_id(0) % tiles_per_seq == 0

    @pl.when(starts_sequence)
    def _():
        ext_ref[0:HALO, :] = jnp.zeros((HALO, SC_DIM), F32)

    @pl.when(jnp.logical_not(starts_sequence))
    def _():
        ext_ref[0:HALO, :] = ext_ref[tm:tm + HALO, :]

    xn = _rms(x_ref[...], g_ref[...]).astype(BF16)
    z_ref[...] = _mm_nt(xn, wz_ref[...])
    os_ref[...] = _mm_nt(xn, ws_ref[...])
    bcx = _mm_nt(xn, wbcx_ref[...])
    ext_ref[HALO:, :] = bcx[:, SC_DIM:2 * SC_DIM] * bcx[:, 2 * SC_DIM:]
    ya_ref[...] = (bcx[:, :SC_DIM] * _causal_conv(ext_ref, cw_ref, tm)).astype(ya_ref.dtype)


def rms_matmul_gated(x, g, w_z, w_bcx, w_small, conv_w, seq_len, tm=1024):
    m, k = x.shape
    const = lambda a: pl.BlockSpec(a.shape, lambda i: (0,) * a.ndim)
    rows = lambda w: pl.BlockSpec((tm, w), lambda i: (i, 0))
    g = g.reshape(1, k)
    return pl.pallas_call(
        functools.partial(_rms_matmul_gated_kernel, tiles_per_seq=seq_len // tm),
        grid=(m // tm,),
        in_specs=[rows(k), const(g), const(w_z), const(w_bcx), const(w_small), const(conv_w)],
        out_specs=[rows(w_z.shape[0]), rows(SC_DIM), rows(LANES)],
        out_shape=[jax.ShapeDtypeStruct((m, w_z.shape[0]), F32), jax.ShapeDtypeStruct((m, SC_DIM), BF16),
                   jax.ShapeDtypeStruct((m, LANES), F32)],
        scratch_shapes=[pltpu.VMEM((tm + HALO, SC_DIM), F32)],
        compiler_params=_cparams("arbitrary"),
        name="rms_matmul_gated",
    )(x, g, w_z, w_bcx, w_small, conv_w)


def _ssd_kernel(xbc_ref, z_ref, dt_ref, dtt_ref, dtb_r_ref, dtb_c_ref,
                alog_r_ref, alog_c_ref, d_ref, nw_ref, tri_ref, trit_ref,
                o_ref, s_ref):
    q = SSD_CHUNK

    @pl.when(pl.program_id(1) == 0)
    def _():
        s_ref[...] = jnp.zeros_like(s_ref)

    for sub in range(xbc_ref.shape[0] // q):
        rows = slice(sub * q, (sub + 1) * q)
        _ssd_chunk(_silu(xbc_ref[rows, :]), z_ref[rows, :], dt_ref[rows, :], dtt_ref[:, rows], dtb_r_ref, dtb_c_ref,
                   alog_r_ref, alog_c_ref, d_ref, nw_ref, tri_ref, trit_ref, o_ref.at[rows, :], s_ref)


def _ssd_chunk(xbc, z, dt_raw, dtt_raw, dtb_r_ref, dtb_c_ref, alog_r_ref, alog_c_ref, d_ref, nw_ref, tri_ref,
               trit_ref, o_ref, s_ref):
    q = SSD_CHUNK
    hpg = SSM_HEADS // SSM_GROUPS
    gw = hpg * SSM_HEAD_DIM
    xs = xbc[:, :SSM_INNER]
    bm = xbc[:, SSM_INNER:SSM_INNER + SSM_GROUPS * SSM_STATE]
    cm = xbc[:, SSM_INNER + SSM_GROUPS * SSM_STATE:]

    dt = _softplus(dt_raw + dtb_r_ref[...])
    acs = _mm_sel_lhs(tri_ref[...], dt * -jnp.exp(alog_r_ref[...]))
    dtt = _softplus(dtt_raw + dtb_c_ref[...])
    acst = _mm_sel_rhs(dtt * -jnp.exp(alog_c_ref[...]), trit_ref[...])
    dt_full = _spread_heads(dt, 0, SSM_HEADS, SSM_HEAD_DIM)
    acs_full = _spread_heads(acs, 0, SSM_HEADS, SSM_HEAD_DIM)
    acs_col = _spread_heads(acs, 0, SSM_HEADS, q)

    xdt = xs * dt_full
    acs_last = acs_full[q - 1:q, :]
    xw = xdt * jnp.exp(acs_last - acs_full)
    chunk_decay = jnp.exp(acs_last)

    row = lax.broadcasted_iota(jnp.int32, (q, q), 0)
    col = lax.broadcasted_iota(jnp.int32, (q, q), 1)
    causal = row >= col
    lane = lax.broadcasted_iota(jnp.int32, (q, 2 * SSM_HEAD_DIM), 1)

    y_diag, y_off = [], []
    for g in range(SSM_GROUPS):
        bm_g = bm[:, g * SSM_STATE:(g + 1) * SSM_STATE]
        cm_g = cm[:, g * SSM_STATE:(g + 1) * SSM_STATE]
        cb_g = _mm_nt(cm_g, bm_g)
        state = s_ref[g]
        y_off.append(_mm(cm_g, state))
        s_ref[g] = state * chunk_decay[:, g * gw:(g + 1) * gw] + _mm(bm_g.T, xw[:, g * gw:(g + 1) * gw])
        for pair in range(hpg // 2):
            h0 = g * hpg + 2 * pair
            xdt_pair = xdt[:, h0 * SSM_HEAD_DIM:(h0 + 2) * SSM_HEAD_DIM]
            weights = []
            for h in (h0, h0 + 1):
                seg = acs_col[:, h * q:(h + 1) * q] - acst[h:h + 1, :]
                weights.append(cb_g * jnp.where(causal, jnp.exp(seg), 0.0))
            both = _mm(jnp.concatenate(weights, axis=0), xdt_pair)
            y_diag.append(jnp.where(lane < SSM_HEAD_DIM, both[:q], both[q:]))
    y = (jnp.concatenate(y_diag, axis=1) + jnp.concatenate(y_off, axis=1) * jnp.exp(acs_full)
         + xs * d_ref[...])
    y = y * _silu(z)
    halves = []
    for g in range(SSM_GROUPS):
        yg = y[:, g * gw:(g + 1) * gw]
        halves.append(yg * lax.rsqrt(jnp.mean(yg * yg, axis=-1, keepdims=True) + EPS))
    o_ref[...] = (jnp.concatenate(halves, axis=1) * nw_ref[...]).astype(o_ref.dtype)


def _pad_lanes(v, fill=0.0):
    return jnp.pad(v.astype(F32), (0, LANES - v.shape[0]), constant_values=fill).reshape(1, LANES)


def _pad_col(v, rows=16):
    return jnp.pad(v.astype(F32), (0, rows - v.shape[0])).reshape(rows, 1)


def ssd_mixer(xbc3, y3, small3, smallt, dt_bias, a_log, d_skip, norm_w):
    bsz, length, _ = y3.shape
    q = SSD_CHUNK
    tri = jnp.asarray(np.tril(np.ones((q, q), np.float32)), BF16)
    trit = jnp.asarray(np.triu(np.ones((q, q), np.float32)), BF16)
    d_full = jnp.repeat(d_skip.astype(F32), SSM_HEAD_DIM).reshape(1, SSM_INNER)
    const = lambda a: pl.BlockSpec(a.shape, lambda b, c: (0,) * a.ndim)
    args = [_pad_lanes(dt_bias), _pad_col(dt_bias), _pad_lanes(a_log),
            _pad_col(a_log), d_full, norm_w.reshape(1, -1), tri, trit]
    rows = min(SSD_STEP_ROWS, length)
    return pl.pallas_call(
        _ssd_kernel,
        grid=(bsz, length // rows),
        in_specs=[
            pl.BlockSpec((None, rows, SSM_XBC), lambda b, c: (b, c, 0)),
            pl.BlockSpec((None, rows, SSM_INNER), lambda b, c: (b, c, 0)),
            pl.BlockSpec((None, rows, LANES), lambda b, c: (b, c, 0)),
            pl.BlockSpec((None, 16, rows), lambda b, c: (b, 0, c)),
        ] + [const(a) for a in args],
        out_specs=pl.BlockSpec((None, rows, SSM_INNER), lambda b, c: (b, c, 0)),
        out_shape=jax.ShapeDtypeStruct((bsz, length, SSM_INNER), BF16),
        scratch_shapes=[pltpu.VMEM((SSM_GROUPS, SSM_STATE, SSM_INNER // SSM_GROUPS), F32)],
        compiler_params=_cparams("parallel", "arbitrary"),
        name="ssd_mixer",
    )(xbc3, y3, small3, smallt, *args)


def _unit_lower_inverse(mats, row, col):
    eye = jnp.where(row == col, 1.0, 0.0)
    blk = lambda n: (row >> (n.bit_length() - 1)) == (col >> (n.bit_length() - 1))
    size = row.shape[0]
    p = [jnp.where(blk(16), -a, 0.0) for a in mats]
    t = [eye + x for x in p]
    p = [_mm(x, x) for x in p]
    for _ in range(2):
        both = [_mm(jnp.concatenate([x, y], axis=0), x) for x, y in zip(p, t)]
        p = [b[:size] for b in both]
        t = [y + b[size:] for y, b in zip(t, both)]
    t = [y + _mm(y, x) for y, x in zip(t, p)]
    for n in (16, 32):
        band = blk(2 * n) & jnp.logical_not(blk(n))
        left = [_mm(y, jnp.where(band, a, 0.0)) for y, a in zip(t, mats)]
        t = [y - _mm(x, y) for y, x in zip(t, left)]
    return t


def _gdn_kernel(qkv_ref, z_ref, ab_ref, abt_ref, dtb_r_ref, dtb_c_ref, alog_r_ref,
                alog_c_ref, nw_ref, tri_ref, trit_ref, o_ref, s_ref):
    n = GDN_TILE

    @pl.when(pl.program_id(1) == 0)
    def _():
        s_ref[...] = jnp.zeros_like(s_ref)

    for sub in range(qkv_ref.shape[0] // n):
        rows = slice(sub * n, (sub + 1) * n)
        _gdn_tile(_silu(qkv_ref[rows, :]), z_ref[rows, :], ab_ref[rows, :], abt_ref[:, rows], dtb_r_ref, dtb_c_ref,
                  alog_r_ref, alog_c_ref, nw_ref, tri_ref, trit_ref, o_ref.at[rows, :], s_ref)


def _gdn_tile(qkv, z, ab, abt, dtb_r_ref, dtb_c_ref, alog_r_ref, alog_c_ref, nw_ref, tri_ref, trit_ref, o_ref, s_ref):
    n = GDN_TILE
    c = GDN_CHUNK
    d = GDN_D
    hd = GDN_HEADS * d
    g = -jnp.exp(alog_r_ref[...]) * _softplus(ab + dtb_r_ref[...])
    gc_full = _spread_heads(_mm_sel_lhs(tri_ref[...], g), 0, GDN_HEADS, d)
    beta_full = _spread_heads(jax.nn.sigmoid(ab), GDN_HEADS, GDN_HEADS, d)
    gt = -jnp.exp(alog_c_ref[...]) * _softplus(abt + dtb_c_ref[...])
    gct = _mm_sel_rhs(gt, trit_ref[...])

    row = lax.broadcasted_iota(jnp.int32, (n, n), 0)
    col = lax.broadcasted_iota(jnp.int32, (n, n), 1)
    same = (row >> (c.bit_length() - 1)) == (col >> (c.bit_length() - 1))
    incl = same & (row >= col)
    strict = same & (row > col)
    zeros_half = jnp.zeros((c, d), F32)

    heads = range(GDN_HEADS)
    sl = [slice(h * d, (h + 1) * d) for h in heads]
    l2n = lambda x: x * lax.rsqrt(jnp.sum(x * x, axis=-1, keepdims=True) + EPS)
    qn = [l2n(qkv[:, sl[h]]) * (d ** -0.5) for h in heads]
    kn = [l2n(qkv[:, hd + h * d:hd + (h + 1) * d]) for h in heads]
    vh = [qkv[:, 2 * hd + h * d:2 * hd + (h + 1) * d] for h in heads]
    gcol = [gc_full[:, sl[h]] for h in heads]
    beta = [beta_full[:, sl[h]] for h in heads]
    edec = [jnp.exp(gcol[h] - gct[h:h + 1, :]) for h in heads]
    egc = [jnp.exp(x) for x in gcol]
    kb = [kn[h] * beta[h] for h in heads]
    on_k = [_mm_nt(jnp.concatenate([kb[h], qn[h]], axis=0), kn[h]) for h in heads]
    lower = [jnp.where(strict, on_k[h][:n] * edec[h], 0.0) for h in heads]
    aqk = [jnp.where(incl, on_k[h][n:] * edec[h], 0.0) for h in heads]
    tinv = _unit_lower_inverse(lower, row, col)
    sol = [_mm(tinv[h], jnp.concatenate([vh[h] * beta[h], kb[h] * egc[h]], axis=1)) for h in heads]
    qd = [qn[h] * egc[h] for h in heads]
    glast = [(gcol[h][c - 1:c, :], gcol[h][n - 1:n, :]) for h in heads]
    kdt = [(kn[h] * jnp.exp(jnp.concatenate([jnp.broadcast_to(glast[h][0], (c, d)),
                                             jnp.broadcast_to(glast[h][1], (c, d))], axis=0) - gcol[h])).T
           for h in heads]
    s0 = [s_ref[h] for h in heads]
    on_s0 = [_mm(jnp.concatenate([sol[h][:c, d:], qd[h][:c]], axis=0), s0[h]) for h in heads]
    v0 = [sol[h][:c, :d] - on_s0[h][:c] for h in heads]
    s1 = [s0[h] * jnp.exp(glast[h][0]) + _mm(kdt[h], jnp.concatenate([v0[h], zeros_half], axis=0)) for h in heads]
    on_s1 = [_mm(jnp.concatenate([sol[h][c:, d:], qd[h][c:]], axis=0), s1[h]) for h in heads]
    v1 = [sol[h][c:, :d] - on_s1[h][:c] for h in heads]
    for h in heads:
        s_ref[h] = s1[h] * jnp.exp(glast[h][1]) + _mm(kdt[h], jnp.concatenate([zeros_half, v1[h]], axis=0))
    outs = []
    for h in heads:
        o = (jnp.concatenate([on_s0[h][c:], on_s1[h][c:]], axis=0)
             + _mm(aqk[h], jnp.concatenate([v0[h], v1[h]], axis=0)))
        o = o * lax.rsqrt(jnp.mean(o * o, axis=-1, keepdims=True) + EPS) * nw_ref[...]
        outs.append(o * _silu(z[:, sl[h]]))
    o_ref[...] = jnp.concatenate(outs, axis=1).astype(o_ref.dtype)


def gated_deltanet_mixer(qkv3, y3, small3, smallt, dt_bias, a_log, norm_w):
    bsz, length, _ = y3.shape
    n = GDN_TILE
    hd = GDN_HEADS * GDN_D
    idx = np.arange(n)
    same = (idx[:, None] // GDN_CHUNK) == (idx[None, :] // GDN_CHUNK)
    tri = jnp.asarray(same & (idx[:, None] >= idx[None, :]), BF16)
    trit = jnp.asarray(same & (idx[:, None] <= idx[None, :]), BF16)
    const = lambda a: pl.BlockSpec(a.shape, lambda b, c: (0,) * a.ndim)
    args = [_pad_lanes(dt_bias), _pad_col(dt_bias), _pad_lanes(a_log), _pad_col(a_log),
            norm_w.reshape(1, -1), tri, trit]
    rows = min(GDN_STEP_ROWS, length)
    return pl.pallas_call(
        _gdn_kernel,
        grid=(bsz, length // rows),
        in_specs=[
            pl.BlockSpec((None, rows, 3 * hd), lambda b, c: (b, c, 0)),
            pl.BlockSpec((None, rows, hd), lambda b, c: (b, c, 0)),
            pl.BlockSpec((None, rows, LANES), lambda b, c: (b, c, 0)),
            pl.BlockSpec((None, 16, rows), lambda b, c: (b, 0, c)),
        ] + [const(a) for a in args],
        out_specs=pl.BlockSpec((None, rows, hd), lambda b, c: (b, c, 0)),
        out_shape=jax.ShapeDtypeStruct((bsz, length, hd), BF16),
        scratch_shapes=[pltpu.VMEM((GDN_HEADS, GDN_D, GDN_D), F32)],
        compiler_params=_cparams("parallel", "arbitrary"),
        name="gated_deltanet",
    )(qkv3, y3, small3, smallt, *args)


def _sb_kernel(q_ref, k_ref, v_ref, upper_ref, o_ref):
    blk = SB_BLOCK
    n_sub = q_ref.shape[0] // blk
    first = pl.program_id(2) * n_sub
    parts = [_sb_query_block(first + s, q_ref[s * blk:(s + 1) * blk, :], k_ref, v_ref, upper_ref[...])
             for s in range(n_sub)]
    for s, finish in enumerate(parts):
        o_ref[s * blk:(s + 1) * blk, :] = finish().astype(o_ref.dtype)


def _sb_query_block(i, q, k_ref, v_ref, upper):
    blk = SB_BLOCK
    pair_w = 2 * SB_HEAD_DIM
    n_pairs = SB_STEP_HEADS // 2
    q = q * (SB_HEAD_DIM ** -0.5)
    lane = lax.broadcasted_iota(jnp.int32, (blk, pair_w), 1)
    first_head = lane < SB_HEAD_DIM
    qs = []
    for p in range(n_pairs):
        q2 = q[:, p * pair_w:(p + 1) * pair_w]
        qs += [jnp.where(first_head, q2, 0.0).astype(BF16), jnp.where(first_head, 0.0, q2).astype(BF16)]
    row = lax.broadcasted_iota(jnp.int32, (blk, blk), 0)
    col = lax.broadcasted_iota(jnp.int32, (blk, blk), 1)
    earlier = col < row
    heads = range(SB_STEP_HEADS)

    def local_part(kb, diagonal, exists=None):
        start = pl.multiple_of(kb * blk, blk)
        k = k_ref[pl.ds(start, blk), :].astype(BF16)
        v = v_ref[pl.ds(start, blk), :].astype(BF16)
        kp = [k[:, p * pair_w:(p + 1) * pair_w] for p in range(n_pairs)]
        vp = [v[:, p * pair_w:(p + 1) * pair_w] for p in range(n_pairs)]
        logits = [lax.dot_general(qs[h], kp[h // 2], (((1,), (1,)), ((), ())), preferred_element_type=F32)
                  for h in heads]
        keep = earlier if diagonal else None
        if exists is not None:
            keep = exists if keep is None else keep & exists
        log_keep = [-_softplus(x) for x in logits]
        if keep is not None:
            log_keep = [jnp.where(keep, x, 0.0) for x in log_keep]
        inside = [_mm_sel_rhs(x, upper, 2) for x in log_keep]
        totals = [jnp.sum(x, axis=-1, keepdims=True) for x in log_keep]
        return logits, log_keep, inside, totals, vp, keep

    def carried_part(local, accs, sticks):
        logits, log_keep, inside, totals, vp, keep = local
        w = [jnp.exp(logits[h] + log_keep[h] + inside[h] + sticks[h]) for h in heads]
        if keep is not None:
            w = [jnp.where(keep, x, 0.0) for x in w]
        pv = [jnp.dot(w[h].astype(BF16), vp[h // 2], preferred_element_type=F32) for h in heads]
        accs = tuple(accs[p] + jnp.where(first_head, pv[2 * p], pv[2 * p + 1]) for p in range(n_pairs))
        sticks = tuple(sticks[h] + totals[h] for h in heads)
        return accs, sticks

    accs = tuple(jnp.zeros((blk, pair_w), F32) for _ in range(n_pairs))
    sticks = tuple(jnp.zeros((blk, 1), F32) for _ in heads)
    eager = [local_part(i, True)]
    for back in range(1, SB_EAGER_BLOCKS + 1):
        eager.append(local_part(jnp.maximum(i - back, 0), False, exists=(row >= 0) & (i - back >= 0)))
    for local in eager:
        accs, sticks = carried_part(local, accs, sticks)

    def alive(state):
        kb, _, sticks = state
        longest = sticks[0]
        for s in sticks[1:]:
            longest = jnp.maximum(longest, s)
        return (kb >= 0) & (jnp.max(longest) > SB_LOG_ZERO)

    def body(state):
        kb, accs, sticks = state
        accs, sticks = carried_part(local_part(kb, False), accs, sticks)
        return kb - 1, accs, sticks

    def finish():
        _, done, _ = lax.while_loop(alive, body, (i - 1 - SB_EAGER_BLOCKS, accs, sticks))
        return jnp.concatenate(done, axis=1)

    return finish


def stick_breaking_mixer(y3, col0):
    bsz, length, _ = y3.shape
    blk = SB_BLOCK
    step_w = SB_STEP_HEADS * SB_HEAD_DIM
    steps = SB_DIM // step_w
    q0 = col0 // step_w
    idx = np.arange(blk)
    upper = jnp.asarray(idx[:, None] > idx[None, :], BF16)
    resident = lambda off: pl.BlockSpec((None, length, step_w), lambda b, p, i: (b, 0, q0 + off + p),
                                        pipeline_mode=pl.Buffered(1))
    rows = min(SB_STEP_ROWS, length)
    return pl.pallas_call(
        _sb_kernel,
        grid=(bsz, steps, length // rows),
        in_specs=[
            pl.BlockSpec((None, rows, step_w), lambda b, p, i: (b, i, q0 + p)),
            resident(steps),
            resident(2 * steps),
            pl.BlockSpec((blk, blk), lambda b, p, i: (0, 0)),
        ],
        out_specs=pl.BlockSpec((None, rows, step_w), lambda b, p, i: (b, i, p)),
        out_shape=jax.ShapeDtypeStruct((bsz, length, SB_DIM), BF16),
        compiler_params=_cparams("parallel", "parallel", "arbitrary"),
        name="stick_breaking",
    )(y3, y3, y3, upper)


def _mixer_out(a_ref, b_ref, h_ref, wa_ref, wb_ref, rows):
    return h_ref[rows, :] + (jnp.dot(a_ref[rows, :].astype(BF16), wa_ref[...], preferred_element_type=F32)
                             + jnp.dot(b_ref[rows, :].astype(BF16), wb_ref[...], preferred_element_type=F32))


def _cross_attention(h, g_ref, wq_ref, kt_ref, v_ref, wo_ref):
    u = _rms(h, g_ref[...]).astype(BF16)
    q = jnp.dot(u, wq_ref[...], preferred_element_type=F32)
    heads = []
    for hd in range(XA_HEADS):
        sl = slice(hd * XA_HEAD_DIM, (hd + 1) * XA_HEAD_DIM)
        s = jnp.dot(q[:, sl].astype(BF16), kt_ref[sl, :], preferred_element_type=F32)
        p = jnp.exp(s - jnp.max(s, axis=-1, keepdims=True))
        p = p * (1.0 / jnp.sum(p, axis=-1, keepdims=True))
        heads.append(jnp.dot(p.astype(BF16), v_ref[:, sl], preferred_element_type=F32))
    o = jnp.concatenate(heads, axis=1).astype(BF16)
    return h + jnp.dot(o, wo_ref[...], preferred_element_type=F32)


def _route(xn, whi_ref, wlo_ref, b_ref, before_ref, run_ref):
    x_hi = xn.astype(BF16)
    x_lo = (xn - x_hi.astype(F32)).astype(BF16)
    wide = jnp.dot(x_hi, jnp.concatenate([whi_ref[...], wlo_ref[...]], axis=1), preferred_element_type=F32)
    logits = (wide[:, :LANES] + jnp.dot(x_lo, whi_ref[...], preferred_element_type=F32)
              + wide[:, LANES:] + b_ref[...])
    lane = lax.broadcasted_iota(jnp.int32, logits.shape, 1).astype(F32)
    neg = -1e30
    none = float(LANES)

    def top(vals):
        best = jnp.max(vals, axis=-1, keepdims=True)
        where = jnp.min(jnp.where(vals == best, lane, none), axis=-1, keepdims=True)
        return best, where

    gl = jnp.where(lane < MOE_GROUPS, logits, neg)
    gbest, gsel = top(gl)
    gprob = 1.0 / jnp.sum(jnp.exp(gl - gbest), axis=-1, keepdims=True)
    lo = MOE_GROUPS + gsel * MOE_PER_GROUP
    el = jnp.where((lane >= lo) & (lane < lo + MOE_PER_GROUP), logits, neg)
    m1, i1 = top(el)
    m2, i2 = top(jnp.where(lane == i1, neg, el))
    e = jnp.exp(m2 - m1)
    gate1 = gprob / (1.0 + e)
    gate2 = gprob * e / (1.0 + e)

    hot1 = lane == i1
    hot2 = lane == i2
    one1 = jnp.where(hot1, 1.0, 0.0)
    one2 = jnp.where(hot2, 1.0, 0.0)
    prefix = jnp.dot(before_ref[...], jnp.concatenate([one1, one2], axis=1).astype(BF16), preferred_element_type=F32)
    prefix1, prefix2 = prefix[:, :LANES], prefix[:, LANES:]
    total1 = jnp.sum(one1, axis=0, keepdims=True)
    running = run_ref[...]
    rank1 = jnp.sum(jnp.where(hot1, prefix1 + running, 0.0), axis=-1, keepdims=True)
    rank2 = jnp.sum(jnp.where(hot2, prefix2 + (running + total1), 0.0), axis=-1, keepdims=True)
    running = running + total1 + jnp.sum(one2, axis=0, keepdims=True)
    run_ref[...] = running

    fields = (i1 - MOE_GROUPS, i2 - MOE_GROUPS, gate1, gate2, rank1, rank2)
    out = jnp.zeros_like(logits)
    for k, val in enumerate(fields):
        out = jnp.where(lane == k, val, out)
    return out


def _post_mixer_kernel(a_ref, b_ref, h_ref, wa_ref, wb_ref, gxa_ref, wq_ref, kt_ref, v_ref, wo_ref,
                       gffn_ref, whi_ref, wlo_ref, bias_ref, before_ref,
                       h_out_ref, xn_ref, r_ref, cnt_ref, run_ref):
    @pl.when(pl.program_id(0) == 0)
    def _():
        run_ref[...] = jnp.zeros_like(run_ref)

    h = _mixer_out(a_ref, b_ref, h_ref, wa_ref, wb_ref, slice(None))
    h = _cross_attention(h, gxa_ref, wq_ref, kt_ref, v_ref, wo_ref)
    h_out_ref[...] = h
    xn = _rms(h, gffn_ref[...])
    xn_ref[...] = _pack_halves(xn)
    group = before_ref.shape[0]
    for start in range(0, h_ref.shape[0], group):
        rows = slice(start, start + group)
        r_ref[rows, :] = _route(xn[rows, :], whi_ref, wlo_ref, bias_ref, before_ref, run_ref)
    cnt_ref[...] = run_ref[...]


def post_mixer(ya, yb, h, wa, wb, g_xa, wq, kt, v, wo, g_ffn, w_hi, w_lo, bias, tm=1024):
    m, d = h.shape
    tiles_per_batch = m // kt.shape[0] // tm
    idx = np.arange(min(POST_GROUP_ROWS, tm))
    before = jnp.asarray(idx[:, None] > idx[None, :], BF16)
    rows = lambda w: pl.BlockSpec((tm, w), lambda i: (i, 0))
    const = lambda a: pl.BlockSpec(a.shape, lambda i: (0,) * a.ndim, pipeline_mode=pl.Buffered(1))
    per_batch = lambda a: pl.BlockSpec((None,) + a.shape[1:], lambda i: (i // tiles_per_batch, 0, 0))
    g_xa, g_ffn = g_xa.reshape(1, d), g_ffn.reshape(1, d)
    return pl.pallas_call(
        _post_mixer_kernel,
        grid=(m // tm,),
        in_specs=[rows(ya.shape[1]), rows(yb.shape[1]), rows(d), const(wa), const(wb), const(g_xa), const(wq),
                  per_batch(kt), per_batch(v), const(wo), const(g_ffn), const(w_hi), const(w_lo), const(bias),
                  const(before)],
        out_specs=[rows(d), rows(d // 2), rows(LANES), pl.BlockSpec((1, LANES), lambda i: (0, 0))],
        out_shape=[jax.ShapeDtypeStruct((m, d), F32), jax.ShapeDtypeStruct((m, d // 2), jnp.int32),
                   jax.ShapeDtypeStruct((m, LANES), F32), jax.ShapeDtypeStruct((1, LANES), F32)],
        scratch_shapes=[pltpu.VMEM((1, LANES), F32)],
        compiler_params=_cparams("arbitrary"),
        name="post_mixer",
    )(ya, yb, h, wa, wb, g_xa, wq, kt, v, wo, g_ffn, w_hi, w_lo, bias, before)


def _expert_kernel(table_ref, x_ref, wg_hbm, wu_hbm, wd_hbm, o_ref,
                   wg32_ref, wu32_ref, wd32_ref, wgb_ref, wub_ref, wdb_ref, sem_ref, *, layer):
    i = pl.program_id(0)
    beid_ref, valid_ref, first_ref, slot_ref, next_ref = (table_ref.at[k] for k in range(5))
    valid = valid_ref[i]

    def weight_copies(expert, slot):
        return (pltpu.make_async_copy(wg_hbm.at[layer, expert], wg32_ref.at[slot], sem_ref.at[slot, 0]),
                pltpu.make_async_copy(wu_hbm.at[layer, expert], wu32_ref.at[slot], sem_ref.at[slot, 1]),
                pltpu.make_async_copy(wd_hbm.at[layer, expert], wd32_ref.at[slot], sem_ref.at[slot, 2]))

    @pl.when(i == 0)
    def _():
        for copy in weight_copies(beid_ref[0], 0):
            copy.start()

    @pl.when(first_ref[i] == 1)
    def _():
        slot = slot_ref[i]
        for copy in weight_copies(beid_ref[i], slot):
            copy.wait()
        wgb_ref[...] = wg32_ref[slot].astype(BF16)
        wub_ref[...] = wu32_ref[slot].astype(BF16)
        wdb_ref[...] = wd32_ref[slot].astype(BF16)

        @pl.when(next_ref[i] >= 0)
        def _():
            for copy in weight_copies(next_ref[i], 1 - slot):
                copy.start()

    half = MOE_ROWS // 2

    def ffn(n_halves):
        row = lax.broadcasted_iota(jnp.int32, (half, 2 * x_ref.shape[1]), 0)
        xs = [jnp.where(row + k * half < valid, _unpack_halves(x_ref[k * half:(k + 1) * half, :]), 0.0).astype(BF16)
              for k in range(n_halves)]
        gates = [jnp.dot(x, wgb_ref[...], preferred_element_type=F32) for x in xs]
        ups = [jnp.dot(x, wub_ref[...], preferred_element_type=F32) for x in xs]
        acts = [(_silu(g) * u).astype(BF16) for g, u in zip(gates, ups)]
        for k, act in enumerate(acts):
            o_ref[k * half:(k + 1) * half, :] = _pack_halves(jnp.dot(act, wdb_ref[...], preferred_element_type=F32))

    @pl.when(valid > half)
    def _():
        ffn(2)

    @pl.when((valid > 0) & (valid <= half))
    def _():
        ffn(1)
        o_ref[half:, :] = jnp.zeros((half, o_ref.shape[1]), o_ref.dtype)

    @pl.when(valid == 0)
    def _():
        o_ref[...] = jnp.zeros_like(o_ref)


def moe_experts(blocks, xs, w_gate, w_up, w_down, layer):
    n_slots, packed = xs.shape
    d = 2 * packed
    rows = MOE_ROWS
    ff = w_gate.shape[3]
    grid_spec = pltpu.PrefetchScalarGridSpec(
        num_scalar_prefetch=1,
        grid=(n_slots // rows,),
        in_specs=[
            pl.BlockSpec((rows, packed), lambda i, *_: (i, 0)),
            pl.BlockSpec(memory_space=pl.ANY),
            pl.BlockSpec(memory_space=pl.ANY),
            pl.BlockSpec(memory_space=pl.ANY),
        ],
        out_specs=pl.BlockSpec((rows, packed), lambda i, *_: (i, 0)),
        scratch_shapes=[pltpu.VMEM((2, d, ff), F32), pltpu.VMEM((2, d, ff), F32), pltpu.VMEM((2, ff, d), F32),
                        pltpu.VMEM((d, ff), BF16), pltpu.VMEM((d, ff), BF16), pltpu.VMEM((ff, d), BF16),
                        pltpu.SemaphoreType.DMA((2, 3))],
    )
    return pl.pallas_call(
        functools.partial(_expert_kernel, layer=layer),
        grid_spec=grid_spec,
        out_shape=jax.ShapeDtypeStruct((n_slots, packed), jnp.int32),
        compiler_params=_cparams("arbitrary"),
        name="moe_experts",
    )(blocks, xs, w_gate, w_up, w_down)


def _sc_mesh():
    return plsc.VectorSubcoreMesh(core_axis_name="c", subcore_axis_name="s",
                                  num_cores=SC_CORES, num_subcores=SC_SUBCORES)


def _sc_worker():
    return lax.axis_index("s") * SC_CORES + lax.axis_index("c")


def _sc_double_buffered(n_chunks, fetch, drain):
    assert n_chunks % 2 == 0
    start = lambda copies: [c.start() for c in copies]
    wait = lambda copies: [c.wait() for c in copies]
    start(fetch(0, 0))

    @pl.loop(0, n_chunks, step=2)
    def _(j):
        wait(fetch(j, 0))

        @pl.when(j > 0)
        def _():
            wait(drain(j - 1, 1))

        start(fetch(j + 1, 1))
        start(drain(j, 0))
        wait(fetch(j + 1, 1))
        wait(drain(j, 0))

        @pl.when(j + 2 < n_chunks)
        def _():
            start(fetch(j + 2, 0))

        start(drain(j + 1, 1))

    wait(drain(n_chunks - 1, 1))


def sc_scatter_rows(x, dest, n_slots):
    n_tok, d = x.shape
    per_worker = n_tok // SC_WORKERS
    n_chunks = per_worker // SC_CHUNK
    by_worker = dest.reshape(dest.shape[0] * SC_WORKERS, n_chunks, SC_CHUNK)

    @functools.partial(
        pl.kernel, mesh=_sc_mesh(), out_type=jax.ShapeDtypeStruct((n_slots, d), x.dtype),
        scratch_types=[pltpu.VMEM((n_chunks, SC_CHUNK), jnp.int32), pltpu.VMEM((n_chunks, SC_CHUNK), jnp.int32),
                       pltpu.VMEM((2, SC_CHUNK, d), x.dtype), pltpu.SemaphoreType.DMA((2, 3))],
        name="moe_scatter_rows")
    def scatter(x_hbm, dest_hbm, out_hbm, i0_v, i1_v, rows_v, sem):
        wid = _sc_worker()
        pltpu.sync_copy(dest_hbm.at[wid], i0_v)
        pltpu.sync_copy(dest_hbm.at[SC_WORKERS + wid], i1_v)

        def fetch(j, buf):
            start = pl.multiple_of(wid * per_worker + j * SC_CHUNK, SC_CHUNK)
            return [pltpu.make_async_copy(x_hbm.at[pl.ds(start, SC_CHUNK)], rows_v.at[buf], sem.at[buf, 0])]

        def drain(j, buf):
            return [pltpu.make_async_copy(rows_v.at[buf], out_hbm.at[i0_v.at[j]], sem.at[buf, 1]),
                    pltpu.make_async_copy(rows_v.at[buf], out_hbm.at[i1_v.at[j]], sem.at[buf, 2])]

        _sc_double_buffered(n_chunks, fetch, drain)

    return scatter(x, by_worker)


def sc_gather_rows(table, idx, n_out):
    d = table.shape[1]
    per_worker = n_out // SC_WORKERS
    n_chunks = per_worker // SC_CHUNK

    @functools.partial(
        pl.kernel, mesh=_sc_mesh(), out_type=jax.ShapeDtypeStruct((n_out, d), table.dtype),
        scratch_types=[pltpu.VMEM((n_chunks, SC_CHUNK), jnp.int32), pltpu.VMEM((2, SC_CHUNK, d), table.dtype),
                       pltpu.SemaphoreType.DMA((2, 2))],
        name="moe_gather_rows")
    def gather(table_hbm, idx_hbm, out_hbm, idx_v, rows_v, sem):
        wid = _sc_worker()
        pltpu.sync_copy(idx_hbm.at[wid], idx_v)

        def fetch(j, buf):
            return [pltpu.make_async_copy(table_hbm.at[idx_v.at[j]], rows_v.at[buf], sem.at[buf, 0])]

        def drain(j, buf):
            start = pl.multiple_of(wid * per_worker + j * SC_CHUNK, SC_CHUNK)
            return [pltpu.make_async_copy(rows_v.at[buf], out_hbm.at[pl.ds(start, SC_CHUNK)], sem.at[buf, 1])]

        _sc_double_buffered(n_chunks, fetch, drain)

    return gather(table, idx.reshape(-1, n_chunks, SC_CHUNK))


def _combine_kernel(h_ref, y0_ref, y1_ref, r_ref, g_ref, o_ref, *, final_norm):
    route = r_ref[...]
    h = h_ref[...] + (route[:, 2:3] * _unpack_halves(y0_ref[...]) + route[:, 3:4] * _unpack_halves(y1_ref[...]))
    o_ref[...] = _rms(h, g_ref[...]) if final_norm else h


def moe_combine(h, y01, route, g, final_norm, tm=1024):
    m, d = h.shape
    tm = min(tm, m)
    rows = lambda w: pl.BlockSpec((tm, w), lambda i: (i, 0))
    return pl.pallas_call(
        functools.partial(_combine_kernel, final_norm=final_norm),
        grid=(m // tm,),
        in_specs=[rows(d), rows(d // 2), pl.BlockSpec((tm, d // 2), lambda i: (i + m // tm, 0)), rows(LANES),
                  pl.BlockSpec((1, d), lambda i: (0, 0))],
        out_specs=rows(d),
        out_shape=jax.ShapeDtypeStruct((m, d), F32),
        compiler_params=_cparams("parallel"),
        name="moe_combine",
    )(h, y01, y01, route, g.reshape(1, d))


def _pad_cols(w):
    return jnp.pad(w, ((0, 0), (0, LANES - w.shape[1])))


def _pad_rows(w):
    return jnp.pad(w, ((0, LANES - w.shape[0]), (0, 0)))


def _plan_kernel(route_ref, cnt_ref, incl_ref, dest_ref, table_ref):
    f32_sum = lambda x, axis: jnp.sum(x, axis=axis, keepdims=True)
    lane = lax.broadcasted_iota(jnp.int32, (LANES, LANES), 1)
    sub = lax.broadcasted_iota(jnp.int32, (LANES, LANES), 0)
    incl = incl_ref[...]
    is_expert = (lane >= MOE_GROUPS) & (lane < MOE_GROUPS + MOE_EXPERTS)
    shift = MOE_ROWS.bit_length() - 1
    counts = jnp.broadcast_to(cnt_ref[...], (LANES, LANES)).astype(jnp.int32)
    padded = jnp.where(is_expert, ((counts + (MOE_ROWS - 1)) >> shift) << shift, 0)
    pad_end = _mm_sel_rhs(padded.astype(F32), incl)
    pad_start = pad_end - padded.astype(F32)

    route = route_ref[...]
    lane_t = lax.broadcasted_iota(jnp.int32, route.shape, 1)
    lane_f = lane_t.astype(F32)
    start_row = pad_start[0:1, :]
    slots = [f32_sum(jnp.where(lane_f == route[:, k:k + 1] + MOE_GROUPS, start_row, 0.0), 1) + route[:, 4 + k:5 + k]
             for k in range(2)]
    both = jnp.where(lane_t == 0, slots[0], jnp.where(lane_t == 1, slots[1], 0.0))
    dest_ref[...] = both.T[0:8, :].astype(jnp.int32)

    on_sub = lambda rows_equal: rows_equal.T
    expert_sub = (sub >= MOE_GROUPS) & (sub < MOE_GROUPS + MOE_EXPERTS)
    block_start = (lane * MOE_ROWS).astype(F32)
    eid = f32_sum(jnp.where(expert_sub & (on_sub(pad_end) <= block_start), 1.0, 0.0), 0)
    eid = jnp.minimum(eid, float(MOE_EXPERTS - 1))
    filled = on_sub(pad_start + counts.astype(F32))
    own = (sub - MOE_GROUPS).astype(F32) == eid
    valid = jnp.clip(f32_sum(jnp.where(own, filled, 0.0), 0) - block_start[0:1, :], 0.0, float(MOE_ROWS))
    eid_rows = jnp.broadcast_to(eid, (LANES, LANES))
    changed = (lane == 0) | (eid_rows != pltpu.roll(eid_rows, 1, axis=1))
    first = jnp.where((jnp.broadcast_to(valid, (LANES, LANES)) > 0) & changed, 1.0, 0.0)
    ordinal = _mm_sel_rhs(first, incl) - 1.0
    slot = ordinal - 2.0 * jnp.floor(ordinal * 0.5)
    later = (on_sub(first) > 0) & (sub > lane)
    nearest = jnp.min(jnp.where(later, sub, LANES), axis=0, keepdims=True)
    next_eid = f32_sum(jnp.where(sub == nearest, on_sub(eid_rows), 0.0), 0)
    next_eid = jnp.where(nearest < LANES, next_eid, -1.0)
    row8 = lax.broadcasted_iota(jnp.int32, (8, LANES), 0)
    table = jnp.zeros((8, LANES), F32)
    for k, val in enumerate((eid, valid, first[0:1, :], slot[0:1, :], next_eid)):
        table = jnp.where(row8 == k, val, table)
    table_ref[...] = table.astype(jnp.int32)


def moe_plan(route, counts, tm=4096):
    n_tok = route.shape[0]
    tm = min(tm, n_tok)
    idx = np.arange(LANES)
    incl = jnp.asarray(idx[:, None] <= idx[None, :], BF16)
    return pl.pallas_call(
        _plan_kernel,
        grid=(n_tok // tm,),
        in_specs=[pl.BlockSpec((tm, LANES), lambda i: (i, 0)), pl.BlockSpec((1, LANES), lambda i: (0, 0)),
                  pl.BlockSpec((LANES, LANES), lambda i: (0, 0))],
        out_specs=[pl.BlockSpec((8, tm), lambda i: (0, i)), pl.BlockSpec((8, LANES), lambda i: (0, 0))],
        out_shape=[jax.ShapeDtypeStruct((8, n_tok), jnp.int32), jax.ShapeDtypeStruct((8, LANES), jnp.int32)],
        compiler_params=_cparams("arbitrary"),
        name="moe_plan",
    )(route, counts, incl)


def _router_weights(w_group, b_group, w_expert, b_expert):
    w_r = _pad_cols(jnp.concatenate([w_group, w_expert], axis=1))
    w_hi = w_r.astype(BF16)
    w_lo = (w_r - w_hi.astype(F32)).astype(BF16)
    return w_hi, w_lo, _pad_lanes(jnp.concatenate([b_group, b_expert]))


def _moe_layer(h, xn, route, counts, w_gate, w_up, w_down, layer, final_g):
    n_tok, d = h.shape
    n_blocks = -(-(2 * n_tok + MOE_EXPERTS * (MOE_ROWS - 1)) // MOE_ROWS)
    dest, blocks = moe_plan(route, counts)
    xs = sc_scatter_rows(xn, dest, n_blocks * MOE_ROWS)
    ys = moe_experts(blocks, xs, w_gate, w_up, w_down, layer)
    y01 = sc_gather_rows(ys, dest, 2 * n_tok)
    g = jnp.ones((d,), F32) if final_g is None else final_g
    return moe_combine(h, y01, route, g, final_g is not None)


def _memory_kv(memn_in, mem_norm, wk, wv):
    bsz, m, d = memn_in.shape
    w = jnp.concatenate([wk.T, wv.T], axis=0).astype(BF16)
    kv, _ = rms_matmul(memn_in.reshape(bsz * m, d), mem_norm, w, jnp.zeros((LANES, d), BF16))
    k = kv[:, :d].reshape(bsz, m, d)
    v = kv[:, d:].reshape(bsz, m, d)
    return (jnp.swapaxes(k, 1, 2) * XA_HEAD_DIM ** -0.5).astype(BF16), v.astype(BF16)


def kernel(x, mem, mem_norm, final_norm, norm_mix, norm_xa, norm_ffn, xa_wq, xa_wk, xa_wv, xa_wo, moe_w_group, moe_b_group, moe_w_expert, moe_b_expert, moe_w_gate, moe_w_up, moe_w_down, ev_w_in, ev_sc_conv, ev_ssm_conv_w, ev_ssm_conv_b, ev_ssm_dt_bias, ev_ssm_a_log, ev_ssm_d, ev_ssm_norm, ev_w_out, od_w_in, od_gdn_conv, od_gdn_dt_bias, od_gdn_a_log, od_gdn_norm, od_w_out):
    bsz, length, d = x.shape
    n_tok = bsz * length
    depth = norm_mix.shape[0]
    h = x.reshape(n_tok, d)
    for layer in range(depth):
        i = layer // 2
        if layer % 2 == 0:
            wt = jnp.swapaxes(ev_w_in[i], 0, 1)
            z0 = 3 * SC_DIM
            xbc0 = z0 + SSM_INNER
            w_conv = wt[xbc0:xbc0 + SSM_XBC].astype(BF16)
            w_small = _pad_rows(wt[xbc0 + SSM_XBC:]).astype(BF16)
            xbc = rms_matmul_conv(h, norm_mix[layer], w_conv, ev_ssm_conv_w[i], ev_ssm_conv_b[i], length)
            z, ya, small = rms_matmul_gated(h, norm_mix[layer], wt[z0:xbc0].astype(BF16), wt[:z0].astype(BF16),
                                            w_small, ev_sc_conv[i], length)
            small3 = small.reshape(bsz, length, LANES)
            smallt = jnp.swapaxes(small3[:, :, :16], 1, 2)
            yb = ssd_mixer(xbc.reshape(bsz, length, -1), z.reshape(bsz, length, -1), small3, smallt,
                           ev_ssm_dt_bias[i], ev_ssm_a_log[i], ev_ssm_d[i], ev_ssm_norm[i])
            w_out = ev_w_out[i].astype(BF16)
            split = SC_DIM
        else:
            wt = jnp.swapaxes(od_w_in[i], 0, 1)
            qkv_w = 3 * GDN_HEADS * GDN_D
            z_end = qkv_w + GDN_HEADS * GDN_D
            w_conv = wt[:qkv_w].astype(BF16)
            w_main = jnp.concatenate([wt[qkv_w:z_end], wt[z_end + 2 * GDN_HEADS:]], axis=0).astype(BF16)
            w_small = _pad_rows(wt[z_end:z_end + 2 * GDN_HEADS]).astype(BF16)
            qkv = rms_matmul_conv(h, norm_mix[layer], w_conv, od_gdn_conv[i], jnp.zeros((qkv_w,), F32), length)
            y, small = rms_matmul(h, norm_mix[layer], w_main, w_small, tm=1024, tn=w_main.shape[0])
            y3 = y.reshape(bsz, length, -1)
            small3 = small.reshape(bsz, length, LANES)
            smallt = jnp.swapaxes(small3[:, :, :16], 1, 2)
            ya = gated_deltanet_mixer(qkv.reshape(bsz, length, -1), y3, small3, smallt, od_gdn_dt_bias[i],
                                      od_gdn_a_log[i], od_gdn_norm[i])
            yb = stick_breaking_mixer(y3, GDN_HEADS * GDN_D)
            w_out = od_w_out[i].astype(BF16)
            split = GDN_HEADS * GDN_D
        kt, v = _memory_kv(mem, mem_norm, xa_wk[layer], xa_wv[layer])
        w_hi, w_lo, bias = _router_weights(moe_w_group[layer], moe_b_group[layer], moe_w_expert[layer],
                                           moe_b_expert[layer])
        h, xn, route, counts = post_mixer(
            ya.reshape(n_tok, -1), yb.reshape(n_tok, -1), h, w_out[:split], w_out[split:], norm_xa[layer],
            xa_wq[layer].astype(BF16), kt, v, xa_wo[layer].astype(BF16), norm_ffn[layer], w_hi, w_lo, bias)
        h = _moe_layer(h, xn, route, counts, moe_w_gate, moe_w_up, moe_w_down, layer,
                       final_norm if layer == depth - 1 else None)
    return h.reshape(bsz, length, d)
```

```python
import functools

import jax
import jax.numpy as jnp
import numpy as np
from jax import lax
from jax.experimental import pallas as pl
from jax.experimental.pallas import tpu as pltpu
from jax.experimental.pallas import tpu_sc as plsc

F32 = jnp.float32
BF16 = jnp.bfloat16
EPS = 1e-6

SC_DIM = 512
SSM_HEADS = 16
SSM_HEAD_DIM = 64
SSM_INNER = 1024
SSM_GROUPS = 2
SSM_STATE = 128
SSM_XBC = SSM_INNER + 2 * SSM_GROUPS * SSM_STATE
SSD_CHUNK = 128
SSD_STEP_ROWS = 512
GDN_HEADS = 8
GDN_D = 128
GDN_CHUNK = 64
GDN_TILE = 128
GDN_STEP_ROWS = 512
SB_HEAD_DIM = 64
SB_DIM = 512
SB_BLOCK = 128
SB_STEP_HEADS = 8
SB_STEP_ROWS = 512
SB_EAGER_BLOCKS = 2
XA_HEADS = 4
XA_HEAD_DIM = 256
MOE_GROUPS = 4
MOE_PER_GROUP = 8
MOE_EXPERTS = 32
MOE_ROWS = 512
POST_GROUP_ROWS = 512
SC_CORES = 2
SC_SUBCORES = 16
SC_WORKERS = SC_CORES * SC_SUBCORES
SC_CHUNK = 64
HALO = 8
CONV_CHUNK = 512
LANES = 128
SB_LOG_ZERO = -104.0
VMEM_LIMIT = 56 * 1024 * 1024


def _cparams(*sem):
    return pltpu.CompilerParams(dimension_semantics=sem, vmem_limit_bytes=VMEM_LIMIT)


def _mm(a, b):
    return jnp.dot(a.astype(BF16), b.astype(BF16), preferred_element_type=F32)


def _mm_nt(a, b):
    return lax.dot_general(a.astype(BF16), b.astype(BF16), (((1,), (1,)), ((), ())),
                           preferred_element_type=F32)


def _split_bf16(x, n):
    parts, r = [], x
    for _ in range(n):
        p = r.astype(BF16)
        parts.append(p)
        r = r - p.astype(F32)
    return parts


def _mm_sel_rhs(x, sel, n=3):
    return sum(jnp.dot(p, sel, preferred_element_type=F32) for p in _split_bf16(x, n))


def _mm_sel_lhs(sel, x, n=3):
    return sum(jnp.dot(sel, p, preferred_element_type=F32) for p in _split_bf16(x, n))


def _spread_heads(x, first, n_heads, width):
    rows = x.shape[0]
    col = lambda h: jnp.broadcast_to(x[:, first + h:first + h + 1], (rows, LANES))
    if width == LANES:
        return jnp.concatenate([col(h) for h in range(n_heads)], axis=1)
    left = lax.broadcasted_iota(jnp.int32, (rows, LANES), 1) < width
    return jnp.concatenate([jnp.where(left, col(h), col(h + 1)) for h in range(0, n_heads, 2)], axis=1)


def _pack_halves(x):
    n = x.shape[1] // 2
    lo = pltpu.bitcast(x[:, :n].astype(BF16).astype(F32), jnp.int32)
    hi = pltpu.bitcast(x[:, n:].astype(BF16).astype(F32), jnp.int32)
    return lax.shift_right_logical(lo, 16) | (hi & jnp.int32(-65536))


def _unpack_halves(p):
    lo = pltpu.bitcast(lax.shift_left(p, 16), F32)
    hi = pltpu.bitcast(p & jnp.int32(-65536), F32)
    return jnp.concatenate([lo, hi], axis=1)


def _silu(x):
    half = 0.5 * x
    return half * jnp.tanh(half) + half


def _softplus(x):
    return jnp.maximum(x, 0.0) + jnp.log(1.0 + jnp.exp(-jnp.abs(x)))


def _rms(x, g):
    return x * lax.rsqrt(jnp.mean(x * x, axis=-1, keepdims=True) + EPS) * g


def _rms_matmul_kernel(x_ref, g_ref, w_ref, ws_ref, o_ref, os_ref, *, transposed):
    mm = _mm_nt if transposed else _mm
    xn = _rms(x_ref[...], g_ref[...]).astype(BF16)
    o_ref[...] = mm(xn, w_ref[...])
    os_ref[...] = mm(xn, ws_ref[...])


def rms_matmul(x, g, w, ws, tm=512, tn=512, transposed=False):
    m, k = x.shape
    n = w.shape[0] if transposed else w.shape[1]
    tm = min(tm, m)
    w_specs = ([pl.BlockSpec((tn, k), lambda j, i: (j, 0)), pl.BlockSpec((LANES, k), lambda j, i: (0, 0))]
               if transposed else
               [pl.BlockSpec((k, tn), lambda j, i: (0, j)), pl.BlockSpec((k, LANES), lambda j, i: (0, 0))])
    main, small = pl.pallas_call(
        functools.partial(_rms_matmul_kernel, transposed=transposed),
        grid=(n // tn, m // tm),
        in_specs=[
            pl.BlockSpec((tm, k), lambda j, i: (i, 0)),
            pl.BlockSpec((1, k), lambda j, i: (0, 0)),
            *w_specs,
        ],
        out_specs=[
            pl.BlockSpec((tm, tn), lambda j, i: (i, j)),
            pl.BlockSpec((None, tm, LANES), lambda j, i: (j, i, 0)),
        ],
        out_shape=[jax.ShapeDtypeStruct((m, n), F32), jax.ShapeDtypeStruct((n // tn, m, LANES), F32)],
        compiler_params=_cparams("parallel", "parallel"),
        name="rms_matmul",
    )(x, g.reshape(1, k), w, ws)
    return main, small[0]


def _causal_conv(ext_ref, w_ref, rows):
    width = w_ref.shape[0]
    ext = ext_ref[...]
    acc = None
    for j in range(width):
        shift = width - 1 - j
        moved = ext if shift == 0 else pltpu.roll(ext, shift, axis=0)
        term = w_ref[j:j + 1, :] * moved[HALO:HALO + rows, :]
        acc = term if acc is None else acc + term
    return acc


def _rms_matmul_conv_kernel(x_ref, g_ref, w_ref, cw_ref, cb_ref, o_ref, *ext_refs, tiles_per_seq):
    tm = x_ref.shape[0]
    starts_sequence = pl.program_id(1) % tiles_per_seq == 0

    @pl.when(starts_sequence)
    def _():
        for ext_ref in ext_refs:
            ext_ref[0:HALO, :] = jnp.zeros((HALO, CONV_CHUNK), F32)

    @pl.when(jnp.logical_not(starts_sequence))
    def _():
        for ext_ref in ext_refs:
            ext_ref[0:HALO, :] = ext_ref[tm:tm + HALO, :]

    xn = _rms(x_ref[...], g_ref[...]).astype(BF16)
    for c, ext_ref in enumerate(ext_refs):
        cols = slice(c * CONV_CHUNK, (c + 1) * CONV_CHUNK)
        ext_ref[HALO:, :] = _mm_nt(xn, w_ref[cols, :])
        o_ref[:, cols] = _causal_conv(ext_ref, cw_ref.at[:, cols], tm) + cb_ref[:, cols]


def rms_matmul_conv(x, g, w, conv_w, conv_b, seq_len, tm=1024, tn=1536):
    m, k = x.shape
    n = conv_w.shape[1]
    cols = lambda rows: pl.BlockSpec((rows, tn), lambda j, i: (0, j))
    return pl.pallas_call(
        functools.partial(_rms_matmul_conv_kernel, tiles_per_seq=seq_len // tm),
        grid=(n // tn, m // tm),
        in_specs=[
            pl.BlockSpec((tm, k), lambda j, i: (i, 0)),
            pl.BlockSpec((1, k), lambda j, i: (0, 0)),
            pl.BlockSpec((tn, k), lambda j, i: (j, 0)), cols(conv_w.shape[0]), cols(1),
        ],
        out_specs=pl.BlockSpec((tm, tn), lambda j, i: (i, j)),
        out_shape=jax.ShapeDtypeStruct((m, n), F32),
        scratch_shapes=[pltpu.VMEM((tm + HALO, CONV_CHUNK), F32)] * (tn // CONV_CHUNK),
        compiler_params=_cparams("arbitrary", "arbitrary"),
        name="rms_matmul_conv",
    )(x, g.reshape(1, k), w, conv_w, conv_b.reshape(1, n))


def _rms_matmul_gated_kernel(x_ref, g_ref, wz_ref, wbcx_ref, ws_ref, cw_ref, z_ref, ya_ref, os_ref, ext_ref,
                             *, tiles_per_seq):
    tm = x_ref.shape[0]
    starts_sequence = pl.program_id(0) % tiles_per_seq == 0

    @pl.when(starts_sequence)
    def _():
        ext_ref[0:HALO, :] = jnp.zeros((HALO, SC_DIM), F32)

    @pl.when(jnp.logical_not(starts_sequence))
    def _():
        ext_ref[0:HALO, :] = ext_ref[tm:tm + HALO, :]

    xn = _rms(x_ref[...], g_ref[...]).astype(BF16)
    z_ref[...] = _mm_nt(xn, wz_ref[...])
    os_ref[...] = _mm_nt(xn, ws_ref[...])
    bcx = _mm_nt(xn, wbcx_ref[...])
    ext_ref[HALO:, :] = bcx[:, SC_DIM:2 * SC_DIM] * bcx[:, 2 * SC_DIM:]
    ya_ref[...] = (bcx[:, :SC_DIM] * _causal_conv(ext_ref, cw_ref, tm)).astype(ya_ref.dtype)


def rms_matmul_gated(x, g, w_z, w_bcx, w_small, conv_w, seq_len, tm=1024):
    m, k = x.shape
    const = lambda a: pl.BlockSpec(a.shape, lambda i: (0,) * a.ndim)
    rows = lambda w: pl.BlockSpec((tm, w), lambda i: (i, 0))
    g = g.reshape(1, k)
    return pl.pallas_call(
        functools.partial(_rms_matmul_gated_kernel, tiles_per_seq=seq_len // tm),
        grid=(m // tm,),
        in_specs=[rows(k), const(g), const(w_z), const(w_bcx), const(w_small), const(conv_w)],
        out_specs=[rows(w_z.shape[0]), rows(SC_DIM), rows(LANES)],
        out_shape=[jax.ShapeDtypeStruct((m, w_z.shape[0]), F32), jax.ShapeDtypeStruct((m, SC_DIM), BF16),
                   jax.ShapeDtypeStruct((m, LANES), F32)],
        scratch_shapes=[pltpu.VMEM((tm + HALO, SC_DIM), F32)],
        compiler_params=_cparams("arbitrary"),
        name="rms_matmul_gated",
    )(x, g, w_z, w_bcx, w_small, conv_w)


def _ssd_kernel(xbc_ref, z_ref, dt_ref, dtt_ref, dtb_r_ref, dtb_c_ref,
                alog_r_ref, alog_c_ref, d_ref, nw_ref, tri_ref, trit_ref,
                o_ref, s_ref):
    q = SSD_CHUNK

    @pl.when(pl.program_id(1) == 0)
    def _():
        s_ref[...] = jnp.zeros_like(s_ref)

    for sub in range(xbc_ref.shape[0] // q):
        rows = slice(sub * q, (sub + 1) * q)
        _ssd_chunk(_silu(xbc_ref[rows, :]), z_ref[rows, :], dt_ref[rows, :], dtt_ref[:, rows], dtb_r_ref, dtb_c_ref,
                   alog_r_ref, alog_c_ref, d_ref, nw_ref, tri_ref, trit_ref, o_ref.at[rows, :], s_ref)


def _ssd_chunk(xbc, z, dt_raw, dtt_raw, dtb_r_ref, dtb_c_ref, alog_r_ref, alog_c_ref, d_ref, nw_ref, tri_ref,
               trit_ref, o_ref, s_ref):
    q = SSD_CHUNK
    hpg = SSM_HEADS // SSM_GROUPS
    gw = hpg * SSM_HEAD_DIM
    xs = xbc[:, :SSM_INNER]
    bm = xbc[:, SSM_INNER:SSM_INNER + SSM_GROUPS * SSM_STATE]
    cm = xbc[:, SSM_INNER + SSM_GROUPS * SSM_STATE:]

    dt = _softplus(dt_raw + dtb_r_ref[...])
    acs = _mm_sel_lhs(tri_ref[...], dt * -jnp.exp(alog_r_ref[...]))
    dtt = _softplus(dtt_raw + dtb_c_ref[...])
    acst = _mm_sel_rhs(dtt * -jnp.exp(alog_c_ref[...]), trit_ref[...])
    dt_full = _spread_heads(dt, 0, SSM_HEADS, SSM_HEAD_DIM)
    acs_full = _spread_heads(acs, 0, SSM_HEADS, SSM_HEAD_DIM)
    acs_col = _spread_heads(acs, 0, SSM_HEADS, q)

    xdt = xs * dt_full
    acs_last = acs_full[q - 1:q, :]
    xw = xdt * jnp.exp(acs_last - acs_full)
    chunk_decay = jnp.exp(acs_last)

    row = lax.broadcasted_iota(jnp.int32, (q, q), 0)
    col = lax.broadcasted_iota(jnp.int32, (q, q), 1)
    causal = row >= col
    lane = lax.broadcasted_iota(jnp.int32, (q, 2 * SSM_HEAD_DIM), 1)

    y_diag, y_off = [], []
    for g in range(SSM_GROUPS):
        bm_g = bm[:, g * SSM_STATE:(g + 1) * SSM_STATE]
        cm_g = cm[:, g * SSM_STATE:(g + 1) * SSM_STATE]
        cb_g = _mm_nt(cm_g, bm_g)
        state = s_ref[g]
        y_off.append(_mm(cm_g, state))
        s_ref[g] = state * chunk_decay[:, g * gw:(g + 1) * gw] + _mm(bm_g.T, xw[:, g * gw:(g + 1) * gw])
        for pair in range(hpg // 2):
            h0 = g * hpg + 2 * pair
            xdt_pair = xdt[:, h0 * SSM_HEAD_DIM:(h0 + 2) * SSM_HEAD_DIM]
            weights = []
            for h in (h0, h0 + 1):
                seg = acs_col[:, h * q:(h + 1) * q] - acst[h:h + 1, :]
                weights.append(cb_g * jnp.where(causal, jnp.exp(seg), 0.0))
            both = _mm(jnp.concatenate(weights, axis=0), xdt_pair)
            y_diag.append(jnp.where(lane < SSM_HEAD_DIM, both[:q], both[q:]))
    y = (jnp.concatenate(y_diag, axis=1) + jnp.concatenate(y_off, axis=1) * jnp.exp(acs_full)
         + xs * d_ref[...])
    y = y * _silu(z)
    halves = []
    for g in range(SSM_GROUPS):
        yg = y[:, g * gw:(g + 1) * gw]
        halves.append(yg * lax.rsqrt(jnp.mean(yg * yg, axis=-1, keepdims=True) + EPS))
    o_ref[...] = (jnp.concatenate(halves, axis=1) * nw_ref[...]).astype(o_ref.dtype)


def _pad_lanes(v, fill=0.0):
    return jnp.pad(v.astype(F32), (0, LANES - v.shape[0]), constant_values=fill).reshape(1, LANES)


def _pad_col(v, rows=16):
    return jnp.pad(v.astype(F32), (0, rows - v.shape[0])).reshape(rows, 1)


def ssd_mixer(xbc3, y3, small3, smallt, dt_bias, a_log, d_skip, norm_w):
    bsz, length, _ = y3.shape
    q = SSD_CHUNK
    tri = jnp.asarray(np.tril(np.ones((q, q), np.float32)), BF16)
    trit = jnp.asarray(np.triu(np.ones((q, q), np.float32)), BF16)
    d_full = jnp.repeat(d_skip.astype(F32), SSM_HEAD_DIM).reshape(1, SSM_INNER)
    const = lambda a: pl.BlockSpec(a.shape, lambda b, c: (0,) * a.ndim)
    args = [_pad_lanes(dt_bias), _pad_col(dt_bias), _pad_lanes(a_log),
            _pad_col(a_log), d_full, norm_w.reshape(1, -1), tri, trit]
    rows = min(SSD_STEP_ROWS, length)
    return pl.pallas_call(
        _ssd_kernel,
        grid=(bsz, length // rows),
        in_specs=[
            pl.BlockSpec((None, rows, SSM_XBC), lambda b, c: (b, c, 0)),
            pl.BlockSpec((None, rows, SSM_INNER), lambda b, c: (b, c, 0)),
            pl.BlockSpec((None, rows, LANES), lambda b, c: (b, c, 0)),
            pl.BlockSpec((None, 16, rows), lambda b, c: (b, 0, c)),
        ] + [const(a) for a in args],
        out_specs=pl.BlockSpec((None, rows, SSM_INNER), lambda b, c: (b, c, 0)),
        out_shape=jax.ShapeDtypeStruct((bsz, length, SSM_INNER), BF16),
        scratch_shapes=[pltpu.VMEM((SSM_GROUPS, SSM_STATE, SSM_INNER // SSM_GROUPS), F32)],
        compiler_params=_cparams("parallel", "arbitrary"),
        name="ssd_mixer",
    )(xbc3, y3, small3, smallt, *args)


def _unit_lower_inverse(mats, row, col):
    eye = jnp.where(row == col, 1.0, 0.0)
    blk = lambda n: (row >> (n.bit_length() - 1)) == (col >> (n.bit_length() - 1))
    size = row.shape[0]
    p = [jnp.where(blk(16), -a, 0.0) for a in mats]
    t = [eye + x for x in p]
    p = [_mm(x, x) for x in p]
    for _ in range(2):
        both = [_mm(jnp.concatenate([x, y], axis=0), x) for x, y in zip(p, t)]
        p = [b[:size] for b in both]
        t = [y + b[size:] for y, b in zip(t, both)]
    t = [y + _mm(y, x) for y, x in zip(t, p)]
    for n in (16, 32):
        band = blk(2 * n) & jnp.logical_not(blk(n))
        left = [_mm(y, jnp.where(band, a, 0.0)) for y, a in zip(t, mats)]
        t = [y - _mm(x, y) for y, x in zip(t, left)]
    return t


def _gdn_kernel(qkv_ref, z_ref, ab_ref, abt_ref, dtb_r_ref, dtb_c_ref, alog_r_ref,
                alog_c_ref, nw_ref, tri_ref, trit_ref, o_ref, s_ref):
    n = GDN_TILE

    @pl.when(pl.program_id(1) == 0)
    def _():
        s_ref[...] = jnp.zeros_like(s_ref)

    for sub in range(qkv_ref.shape[0] // n):
        rows = slice(sub * n, (sub + 1) * n)
        _gdn_tile(_silu(qkv_ref[rows, :]), z_ref[rows, :], ab_ref[rows, :], abt_ref[:, rows], dtb_r_ref, dtb_c_ref,
                  alog_r_ref, alog_c_ref, nw_ref, tri_ref, trit_ref, o_ref.at[rows, :], s_ref)


def _gdn_tile(qkv, z, ab, abt, dtb_r_ref, dtb_c_ref, alog_r_ref, alog_c_ref, nw_ref, tri_ref, trit_ref, o_ref, s_ref):
    n = GDN_TILE
    c = GDN_CHUNK
    d = GDN_D
    hd = GDN_HEADS * d
    g = -jnp.exp(alog_r_ref[...]) * _softplus(ab + dtb_r_ref[...])
    gc_full = _spread_heads(_mm_sel_lhs(tri_ref[...], g), 0, GDN_HEADS, d)
    beta_full = _spread_heads(jax.nn.sigmoid(ab), GDN_HEADS, GDN_HEADS, d)
    gt = -jnp.exp(alog_c_ref[...]) * _softplus(abt + dtb_c_ref[...])
    gct = _mm_sel_rhs(gt, trit_ref[...])

    row = lax.broadcasted_iota(jnp.int32, (n, n), 0)
    col = lax.broadcasted_iota(jnp.int32, (n, n), 1)
    same = (row >> (c.bit_length() - 1)) == (col >> (c.bit_length() - 1))
    incl = same & (row >= col)
    strict = same & (row > col)
    zeros_half = jnp.zeros((c, d), F32)

    heads = range(GDN_HEADS)
    sl = [slice(h * d, (h + 1) * d) for h in heads]
    l2n = lambda x: x * lax.rsqrt(jnp.sum(x * x, axis=-1, keepdims=True) + EPS)
    qn = [l2n(qkv[:, sl[h]]) * (d ** -0.5) for h in heads]
    kn = [l2n(qkv[:, hd + h * d:hd + (h + 1) * d]) for h in heads]
    vh = [qkv[:, 2 * hd + h * d:2 * hd + (h + 1) * d] for h in heads]
    gcol = [gc_full[:, sl[h]] for h in heads]
    beta = [beta_full[:, sl[h]] for h in heads]
    edec = [jnp.exp(gcol[h] - gct[h:h + 1, :]) for h in heads]
    egc = [jnp.exp(x) for x in gcol]
    kb = [kn[h] * beta[h] for h in heads]
    on_k = [_mm_nt(jnp.concatenate([kb[h], qn[h]], axis=0), kn[h]) for h in heads]
    lower = [jnp.where(strict, on_k[h][:n] * edec[h], 0.0) for h in heads]
    aqk = [jnp.where(incl, on_k[h][n:] * edec[h], 0.0) for h in heads]
    tinv = _unit_lower_inverse(lower, row, col)
    sol = [_mm(tinv[h], jnp.concatenate([vh[h] * beta[h], kb[h] * egc[h]], axis=1)) for h in heads]
    qd = [qn[h] * egc[h] for h in heads]
    glast = [(gcol[h][c - 1:c, :], gcol[h][n - 1:n, :]) for h in heads]
    kdt = [(kn[h] * jnp.exp(jnp.concatenate([jnp.broadcast_to(glast[h][0], (c, d)),
                                             jnp.broadcast_to(glast[h][1], (c, d))], axis=0) - gcol[h])).T
           for h in heads]
    s0 = [s_ref[h] for h in heads]
    on_s0 = [_mm(jnp.concatenate([sol[h][:c, d:], qd[h][:c]], axis=0), s0[h]) for h in heads]
    v0 = [sol[h][:c, :d] - on_s0[h][:c] for h in heads]
    s1 = [s0[h] * jnp.exp(glast[h][0]) + _mm(kdt[h], jnp.concatenate([v0[h], zeros_half], axis=0)) for h in heads]
    on_s1 = [_mm(jnp.concatenate([sol[h][c:, d:], qd[h][c:]], axis=0), s1[h]) for h in heads]
    v1 = [sol[h][c:, :d] - on_s1[h][:c] for h in heads]
    for h in heads:
        s_ref[h] = s1[h] * jnp.exp(glast[h][1]) + _mm(kdt[h], jnp.concatenate([zeros_half, v1[h]], axis=0))
    outs = []
    for h in heads:
        o = (jnp.concatenate([on_s0[h][c:], on_s1[h][c:]], axis=0)
             + _mm(aqk[h], jnp.concatenate([v0[h], v1[h]], axis=0)))
        o = o * lax.rsqrt(jnp.mean(o * o, axis=-1, keepdims=True) + EPS) * nw_ref[...]
        outs.append(o * _silu(z[:, sl[h]]))
    o_ref[...] = jnp.concatenate(outs, axis=1).astype(o_ref.dtype)


def gated_deltanet_mixer(qkv3, y3, small3, smallt, dt_bias, a_log, norm_w):
    bsz, length, _ = y3.shape
    n = GDN_TILE
    hd = GDN_HEADS * GDN_D
    idx = np.arange(n)
    same = (idx[:, None] // GDN_CHUNK) == (idx[None, :] // GDN_CHUNK)
    tri = jnp.asarray(same & (idx[:, None] >= idx[None, :]), BF16)
    trit = jnp.asarray(same & (idx[:, None] <= idx[None, :]), BF16)
    const = lambda a: pl.BlockSpec(a.shape, lambda b, c: (0,) * a.ndim)
    args = [_pad_lanes(dt_bias), _pad_col(dt_bias), _pad_lanes(a_log), _pad_col(a_log),
            norm_w.reshape(1, -1), tri, trit]
    rows = min(GDN_STEP_ROWS, length)
    return pl.pallas_call(
        _gdn_kernel,
        grid=(bsz, length // rows),
        in_specs=[
            pl.BlockSpec((None, rows, 3 * hd), lambda b, c: (b, c, 0)),
            pl.BlockSpec((None, rows, hd), lambda b, c: (b, c, 0)),
            pl.BlockSpec((None, rows, LANES), lambda b, c: (b, c, 0)),
            pl.BlockSpec((None, 16, rows), lambda b, c: (b, 0, c)),
        ] + [const(a) for a in args],
        out_specs=pl.BlockSpec((None, rows, hd), lambda b, c: (b, c, 0)),
        out_shape=jax.ShapeDtypeStruct((bsz, length, hd), BF16),
        scratch_shapes=[pltpu.VMEM((GDN_HEADS, GDN_D, GDN_D), F32)],
        compiler_params=_cparams("parallel", "arbitrary"),
        name="gated_deltanet",
    )(qkv3, y3, small3, smallt, *args)


def _sb_kernel(q_ref, k_ref, v_ref, upper_ref, o_ref):
    blk = SB_BLOCK
    n_sub = q_ref.shape[0] // blk
    first = pl.program_id(2) * n_sub
    parts = [_sb_query_block(first + s, q_ref[s * blk:(s + 1) * blk, :], k_ref, v_ref, upper_ref[...])
             for s in range(n_sub)]
    for s, finish in enumerate(parts):
        o_ref[s * blk:(s + 1) * blk, :] = finish().astype(o_ref.dtype)


def _sb_query_block(i, q, k_ref, v_ref, upper):
    blk = SB_BLOCK
    pair_w = 2 * SB_HEAD_DIM
    n_pairs = SB_STEP_HEADS // 2
    q = q * (SB_HEAD_DIM ** -0.5)
    lane = lax.broadcasted_iota(jnp.int32, (blk, pair_w), 1)
    first_head = lane < SB_HEAD_DIM
    qs = []
    for p in range(n_pairs):
        q2 = q[:, p * pair_w:(p + 1) * pair_w]
        qs += [jnp.where(first_head, q2, 0.0).astype(BF16), jnp.where(first_head, 0.0, q2).astype(BF16)]
    row = lax.broadcasted_iota(jnp.int32, (blk, blk), 0)
    col = lax.broadcasted_iota(jnp.int32, (blk, blk), 1)
    earlier = col < row
    heads = range(SB_STEP_HEADS)

    def local_part(kb, diagonal, exists=None):
        start = pl.multiple_of(kb * blk, blk)
        k = k_ref[pl.ds(start, blk), :].astype(BF16)
        v = v_ref[pl.ds(start, blk), :].astype(BF16)
        kp = [k[:, p * pair_w:(p + 1) * pair_w] for p in range(n_pairs)]
        vp = [v[:, p * pair_w:(p + 1) * pair_w] for p in range(n_pairs)]
        logits = [lax.dot_general(qs[h], kp[h // 2], (((1,), (1,)), ((), ())), preferred_element_type=F32)
                  for h in heads]
        keep = earlier if diagonal else None
        if exists is not None:
            keep = exists if keep is None else keep & exists
        log_keep = [-_softplus(x) for x in logits]
        if keep is not None:
            log_keep = [jnp.where(keep, x, 0.0) for x in log_keep]
        inside = [_mm_sel_rhs(x, upper, 2) for x in log_keep]
        totals = [jnp.sum(x, axis=-1, keepdims=True) for x in log_keep]
        return logits, log_keep, inside, totals, vp, keep

    def carried_part(local, accs, sticks):
        logits, log_keep, inside, totals, vp, keep = local
        w = [jnp.exp(logits[h] + log_keep[h] + inside[h] + sticks[h]) for h in heads]
        if keep is not None:
            w = [jnp.where(keep, x, 0.0) for x in w]
        pv = [jnp.dot(w[h].astype(BF16), vp[h // 2], preferred_element_type=F32) for h in heads]
        accs = tuple(accs[p] + jnp.where(first_head, pv[2 * p], pv[2 * p + 1]) for p in range(n_pairs))
        sticks = tuple(sticks[h] + totals[h] for h in heads)
        return accs, sticks

    accs = tuple(jnp.zeros((blk, pair_w), F32) for _ in range(n_pairs))
    sticks = tuple(jnp.zeros((blk, 1), F32) for _ in heads)
    eager = [local_part(i, True)]
    for back in range(1, SB_EAGER_BLOCKS + 1):
        eager.append(local_part(jnp.maximum(i - back, 0), False, exists=(row >= 0) & (i - back >= 0)))
    for local in eager:
        accs, sticks = carried_part(local, accs, sticks)

    def alive(state):
        kb, _, sticks = state
        longest = sticks[0]
        for s in sticks[1:]:
            longest = jnp.maximum(longest, s)
        return (kb >= 0) & (jnp.max(longest) > SB_LOG_ZERO)

    def body(state):
        kb, accs, sticks = state
        accs, sticks = carried_part(local_part(kb, False), accs, sticks)
        return kb - 1, accs, sticks

    def finish():
        _, done, _ = lax.while_loop(alive, body, (i - 1 - SB_EAGER_BLOCKS, accs, sticks))
        return jnp.concatenate(done, axis=1)

    return finish


def stick_breaking_mixer(y3, col0):
    bsz, length, _ = y3.shape
    blk = SB_BLOCK
    step_w = SB_STEP_HEADS * SB_HEAD_DIM
    steps = SB_DIM // step_w
    q0 = col0 // step_w
    idx = np.arange(blk)
    upper = jnp.asarray(idx[:, None] > idx[None, :], BF16)
    resident = lambda off: pl.BlockSpec((None, length, step_w), lambda b, p, i: (b, 0, q0 + off + p),
                                        pipeline_mode=pl.Buffered(1))
    rows = min(SB_STEP_ROWS, length)
    return pl.pallas_call(
        _sb_kernel,
        grid=(bsz, steps, length // rows),
        in_specs=[
            pl.BlockSpec((None, rows, step_w), lambda b, p, i: (b, i, q0 + p)),
            resident(steps),
            resident(2 * steps),
            pl.BlockSpec((blk, blk), lambda b, p, i: (0, 0)),
        ],
        out_specs=pl.BlockSpec((None, rows, step_w), lambda b, p, i: (b, i, p)),
        out_shape=jax.ShapeDtypeStruct((bsz, length, SB_DIM), BF16),
        compiler_params=_cparams("parallel", "parallel", "arbitrary"),
        name="stick_breaking",
    )(y3, y3, y3, upper)


def _mixer_out(a_ref, b_ref, h_ref, wa_ref, wb_ref, rows):
    return h_ref[rows, :] + (jnp.dot(a_ref[rows, :].astype(BF16), wa_ref[...], preferred_element_type=F32)
                             + jnp.dot(b_ref[rows, :].astype(BF16), wb_ref[...], preferred_element_type=F32))


def _cross_attention(h, g_ref, wq_ref, kt_ref, v_ref, wo_ref):
    u = _rms(h, g_ref[...]).astype(BF16)
    q = jnp.dot(u, wq_ref[...], preferred_element_type=F32)
    heads = []
    for hd in range(XA_HEADS):
        sl = slice(hd * XA_HEAD_DIM, (hd + 1) * XA_HEAD_DIM)
        s = jnp.dot(q[:, sl].astype(BF16), kt_ref[sl, :], preferred_element_type=F32)
        p = jnp.exp(s - jnp.max(s, axis=-1, keepdims=True))
        p = p * (1.0 / jnp.sum(p, axis=-1, keepdims=True))
        heads.append(jnp.dot(p.astype(BF16), v_ref[:, sl], preferred_element_type=F32))
    o = jnp.concatenate(heads, axis=1).astype(BF16)
    return h + jnp.dot(o, wo_ref[...], preferred_element_type=F32)


def _route(xn, whi_ref, wlo_ref, b_ref, before_ref, run_ref):
    x_hi = xn.astype(BF16)
    x_lo = (xn - x_hi.astype(F32)).astype(BF16)
    wide = jnp.dot(x_hi, jnp.concatenate([whi_ref[...], wlo_ref[...]], axis=1), preferred_element_type=F32)
    logits = (wide[:, :LANES] + jnp.dot(x_lo, whi_ref[...], preferred_element_type=F32)
              + wide[:, LANES:] + b_ref[...])
    lane = lax.broadcasted_iota(jnp.int32, logits.shape, 1).astype(F32)
    neg = -1e30
    none = float(LANES)

    def top(vals):
        best = jnp.max(vals, axis=-1, keepdims=True)
        where = jnp.min(jnp.where(vals == best, lane, none), axis=-1, keepdims=True)
        return best, where

    gl = jnp.where(lane < MOE_GROUPS, logits, neg)
    gbest, gsel = top(gl)
    gprob = 1.0 / jnp.sum(jnp.exp(gl - gbest), axis=-1, keepdims=True)
    lo = MOE_GROUPS + gsel * MOE_PER_GROUP
    el = jnp.where((lane >= lo) & (lane < lo + MOE_PER_GROUP), logits, neg)
    m1, i1 = top(el)
    m2, i2 = top(jnp.where(lane == i1, neg, el))
    e = jnp.exp(m2 - m1)
    gate1 = gprob / (1.0 + e)
    gate2 = gprob * e / (1.0 + e)

    hot1 = lane == i1
    hot2 = lane == i2
    one1 = jnp.where(hot1, 1.0, 0.0)
    one2 = jnp.where(hot2, 1.0, 0.0)
    prefix = jnp.dot(before_ref[...], jnp.concatenate([one1, one2], axis=1).astype(BF16), preferred_element_type=F32)
    prefix1, prefix2 = prefix[:, :LANES], prefix[:, LANES:]
    total1 = jnp.sum(one1, axis=0, keepdims=True)
    running = run_ref[...]
    rank1 = jnp.sum(jnp.where(hot1, prefix1 + running, 0.0), axis=-1, keepdims=True)
    rank2 = jnp.sum(jnp.where(hot2, prefix2 + (running + total1), 0.0), axis=-1, keepdims=True)
    running = running + total1 + jnp.sum(one2, axis=0, keepdims=True)
    run_ref[...] = running

    fields = (i1 - MOE_GROUPS, i2 - MOE_GROUPS, gate1, gate2, rank1, rank2)
    out = jnp.zeros_like(logits)
    for k, val in enumerate(fields):
        out = jnp.where(lane == k, val, out)
    return out


def _post_mixer_kernel(a_ref, b_ref, h_ref, wa_ref, wb_ref, gxa_ref, wq_ref, kt_ref, v_ref, wo_ref,
                       gffn_ref, whi_ref, wlo_ref, bias_ref, before_ref,
                       h_out_ref, xn_ref, r_ref, cnt_ref, run_ref):
    @pl.when(pl.program_id(0) == 0)
    def _():
        run_ref[...] = jnp.zeros_like(run_ref)

    h = _mixer_out(a_ref, b_ref, h_ref, wa_ref, wb_ref, slice(None))
    h = _cross_attention(h, gxa_ref, wq_ref, kt_ref, v_ref, wo_ref)
    h_out_ref[...] = h
    xn = _rms(h, gffn_ref[...])
    xn_ref[...] = _pack_halves(xn)
    group = before_ref.shape[0]
    for start in range(0, h_ref.shape[0], group):
        rows = slice(start, start + group)
        r_ref[rows, :] = _route(xn[rows, :], whi_ref, wlo_ref, bias_ref, before_ref, run_ref)
    cnt_ref[...] = run_ref[...]


def post_mixer(ya, yb, h, wa, wb, g_xa, wq, kt, v, wo, g_ffn, w_hi, w_lo, bias, tm=1024):
    m, d = h.shape
    tiles_per_batch = m // kt.shape[0] // tm
    idx = np.arange(min(POST_GROUP_ROWS, tm))
    before = jnp.asarray(idx[:, None] > idx[None, :], BF16)
    rows = lambda w: pl.BlockSpec((tm, w), lambda i: (i, 0))
    const = lambda a: pl.BlockSpec(a.shape, lambda i: (0,) * a.ndim, pipeline_mode=pl.Buffered(1))
    per_batch = lambda a: pl.BlockSpec((None,) + a.shape[1:], lambda i: (i // tiles_per_batch, 0, 0))
    g_xa, g_ffn = g_xa.reshape(1, d), g_ffn.reshape(1, d)
    return pl.pallas_call(
        _post_mixer_kernel,
        grid=(m // tm,),
        in_specs=[rows(ya.shape[1]), rows(yb.shape[1]), rows(d), const(wa), const(wb), const(g_xa), const(wq),
                  per_batch(kt), per_batch(v), const(wo), const(g_ffn), const(w_hi), const(w_lo), const(bias),
                  const(before)],
        out_specs=[rows(d), rows(d // 2), rows(LANES), pl.BlockSpec((1, LANES), lambda i: (0, 0))],
        out_shape=[jax.ShapeDtypeStruct((m, d), F32), jax.ShapeDtypeStruct((m, d // 2), jnp.int32),
                   jax.ShapeDtypeStruct((m, LANES), F32), jax.ShapeDtypeStruct((1, LANES), F32)],
        scratch_shapes=[pltpu.VMEM((1, LANES), F32)],
        compiler_params=_cparams("arbitrary"),
        name="post_mixer",
    )(ya, yb, h, wa, wb, g_xa, wq, kt, v, wo, g_ffn, w_hi, w_lo, bias, before)


def _expert_kernel(table_ref, x_ref, wg_hbm, wu_hbm, wd_hbm, o_ref,
                   wg32_ref, wu32_ref, wd32_ref, wgb_ref, wub_ref, wdb_ref, sem_ref, *, layer):
    i = pl.program_id(0)
    beid_ref, valid_ref, first_ref, slot_ref, next_ref = (table_ref.at[k] for k in range(5))
    valid = valid_ref[i]

    def weight_copies(expert, slot):
        return (pltpu.make_async_copy(wg_hbm.at[layer, expert], wg32_ref.at[slot], sem_ref.at[slot, 0]),
                pltpu.make_async_copy(wu_hbm.at[layer, expert], wu32_ref.at[slot], sem_ref.at[slot, 1]),
                pltpu.make_async_copy(wd_hbm.at[layer, expert], wd32_ref.at[slot], sem_ref.at[slot, 2]))

    @pl.when(i == 0)
    def _():
        for copy in weight_copies(beid_ref[0], 0):
            copy.start()

    @pl.when(first_ref[i] == 1)
    def _():
        slot = slot_ref[i]
        for copy in weight_copies(beid_ref[i], slot):
            copy.wait()
        wgb_ref[...] = wg32_ref[slot].astype(BF16)
        wub_ref[...] = wu32_ref[slot].astype(BF16)
        wdb_ref[...] = wd32_ref[slot].astype(BF16)

        @pl.when(next_ref[i] >= 0)
        def _():
            for copy in weight_copies(next_ref[i], 1 - slot):
                copy.start()

    half = MOE_ROWS // 2

    def ffn(n_halves):
        row = lax.broadcasted_iota(jnp.int32, (half, 2 * x_ref.shape[1]), 0)
        xs = [jnp.where(row + k * half < valid, _unpack_halves(x_ref[k * half:(k + 1) * half, :]), 0.0).astype(BF16)
              for k in range(n_halves)]
        gates = [jnp.dot(x, wgb_ref[...], preferred_element_type=F32) for x in xs]
        ups = [jnp.dot(x, wub_ref[...], preferred_element_type=F32) for x in xs]
        acts = [(_silu(g) * u).astype(BF16) for g, u in zip(gates, ups)]
        for k, act in enumerate(acts):
            o_ref[k * half:(k + 1) * half, :] = _pack_halves(jnp.dot(act, wdb_ref[...], preferred_element_type=F32))

    @pl.when(valid > half)
    def _():
        ffn(2)

    @pl.when((valid > 0) & (valid <= half))
    def _():
        ffn(1)
        o_ref[half:, :] = jnp.zeros((half, o_ref.shape[1]), o_ref.dtype)

    @pl.when(valid == 0)
    def _():
        o_ref[...] = jnp.zeros_like(o_ref)


def moe_experts(blocks, xs, w_gate, w_up, w_down, layer):
    n_slots, packed = xs.shape
    d = 2 * packed
    rows = MOE_ROWS
    ff = w_gate.shape[3]
    grid_spec = pltpu.PrefetchScalarGridSpec(
        num_scalar_prefetch=1,
        grid=(n_slots // rows,),
        in_specs=[
            pl.BlockSpec((rows, packed), lambda i, *_: (i, 0)),
            pl.BlockSpec(memory_space=pl.ANY),
            pl.BlockSpec(memory_space=pl.ANY),
            pl.BlockSpec(memory_space=pl.ANY),
        ],
        out_specs=pl.BlockSpec((rows, packed), lambda i, *_: (i, 0)),
        scratch_shapes=[pltpu.VMEM((2, d, ff), F32), pltpu.VMEM((2, d, ff), F32), pltpu.VMEM((2, ff, d), F32),
                        pltpu.VMEM((d, ff), BF16), pltpu.VMEM((d, ff), BF16), pltpu.VMEM((ff, d), BF16),
                        pltpu.SemaphoreType.DMA((2, 3))],
    )
    return pl.pallas_call(
        functools.partial(_expert_kernel, layer=layer),
        grid_spec=grid_spec,
        out_shape=jax.ShapeDtypeStruct((n_slots, packed), jnp.int32),
        compiler_params=_cparams("arbitrary"),
        name="moe_experts",
    )(blocks, xs, w_gate, w_up, w_down)


def _sc_mesh():
    return plsc.VectorSubcoreMesh(core_axis_name="c", subcore_axis_name="s",
                                  num_cores=SC_CORES, num_subcores=SC_SUBCORES)


def _sc_worker():
    return lax.axis_index("s") * SC_CORES + lax.axis_index("c")


def _sc_double_buffered(n_chunks, fetch, drain):
    assert n_chunks % 2 == 0
    start = lambda copies: [c.start() for c in copies]
    wait = lambda copies: [c.wait() for c in copies]
    start(fetch(0, 0))

    @pl.loop(0, n_chunks, step=2)
    def _(j):
        wait(fetch(j, 0))

        @pl.when(j > 0)
        def _():
            wait(drain(j - 1, 1))

        start(fetch(j + 1, 1))
        start(drain(j, 0))
        wait(fetch(j + 1, 1))
        wait(drain(j, 0))

        @pl.when(j + 2 < n_chunks)
        def _():
            start(fetch(j + 2, 0))

        start(drain(j + 1, 1))

    wait(drain(n_chunks - 1, 1))


def sc_scatter_rows(x, dest, n_slots):
    n_tok, d = x.shape
    per_worker = n_tok // SC_WORKERS
    n_chunks = per_worker // SC_CHUNK
    by_worker = dest.reshape(dest.shape[0] * SC_WORKERS, n_chunks, SC_CHUNK)

    @functools.partial(
        pl.kernel, mesh=_sc_mesh(), out_type=jax.ShapeDtypeStruct((n_slots, d), x.dtype),
        scratch_types=[pltpu.VMEM((n_chunks, SC_CHUNK), jnp.int32), pltpu.VMEM((n_chunks, SC_CHUNK), jnp.int32),
                       pltpu.VMEM((2, SC_CHUNK, d), x.dtype), pltpu.SemaphoreType.DMA((2, 3))],
        name="moe_scatter_rows")
    def scatter(x_hbm, dest_hbm, out_hbm, i0_v, i1_v, rows_v, sem):
        wid = _sc_worker()
        pltpu.sync_copy(dest_hbm.at[wid], i0_v)
        pltpu.sync_copy(dest_hbm.at[SC_WORKERS + wid], i1_v)

        def fetch(j, buf):
            start = pl.multiple_of(wid * per_worker + j * SC_CHUNK, SC_CHUNK)
            return [pltpu.make_async_copy(x_hbm.at[pl.ds(start, SC_CHUNK)], rows_v.at[buf], sem.at[buf, 0])]

        def drain(j, buf):
            return [pltpu.make_async_copy(rows_v.at[buf], out_hbm.at[i0_v.at[j]], sem.at[buf, 1]),
                    pltpu.make_async_copy(rows_v.at[buf], out_hbm.at[i1_v.at[j]], sem.at[buf, 2])]

        _sc_double_buffered(n_chunks, fetch, drain)

    return scatter(x, by_worker)


def sc_gather_rows(table, idx, n_out):
    d = table.shape[1]
    per_worker = n_out // SC_WORKERS
    n_chunks = per_worker // SC_CHUNK

    @functools.partial(
        pl.kernel, mesh=_sc_mesh(), out_type=jax.ShapeDtypeStruct((n_out, d), table.dtype),
        scratch_types=[pltpu.VMEM((n_chunks, SC_CHUNK), jnp.int32), pltpu.VMEM((2, SC_CHUNK, d), table.dtype),
                       pltpu.SemaphoreType.DMA((2, 2))],
        name="moe_gather_rows")
    def gather(table_hbm, idx_hbm, out_hbm, idx_v, rows_v, sem):
        wid = _sc_worker()
        pltpu.sync_copy(idx_hbm.at[wid], idx_v)

        def fetch(j, buf):
            return [pltpu.make_async_copy(table_hbm.at[idx_v.at[j]], rows_v.at[buf], sem.at[buf, 0])]

        def drain(j, buf):
            start = pl.multiple_of(wid * per_worker + j * SC_CHUNK, SC_CHUNK)
            return [pltpu.make_async_copy(rows_v.at[buf], out_hbm.at[pl.ds(start, SC_CHUNK)], sem.at[buf, 1])]

        _sc_double_buffered(n_chunks, fetch, drain)

    return gather(table, idx.reshape(-1, n_chunks, SC_CHUNK))


def _combine_kernel(h_ref, y0_ref, y1_ref, r_ref, g_ref, o_ref, *, final_norm):
    route = r_ref[...]
    h = h_ref[...] + (route[:, 2:3] * _unpack_halves(y0_ref[...]) + route[:, 3:4] * _unpack_halves(y1_ref[...]))
    o_ref[...] = _rms(h, g_ref[...]) if final_norm else h


def moe_combine(h, y01, route, g, final_norm, tm=1024):
    m, d = h.shape
    tm = min(tm, m)
    rows = lambda w: pl.BlockSpec((tm, w), lambda i: (i, 0))
    return pl.pallas_call(
        functools.partial(_combine_kernel, final_norm=final_norm),
        grid=(m // tm,),
        in_specs=[rows(d), rows(d // 2), pl.BlockSpec((tm, d // 2), lambda i: (i + m // tm, 0)), rows(LANES),
                  pl.BlockSpec((1, d), lambda i: (0, 0))],
        out_specs=rows(d),
        out_shape=jax.ShapeDtypeStruct((m, d), F32),
        compiler_params=_cparams("parallel"),
        name="moe_combine",
    )(h, y01, y01, route, g.reshape(1, d))


def _pad_cols(w):
    return jnp.pad(w, ((0, 0), (0, LANES - w.shape[1])))


def _pad_rows(w):
    return jnp.pad(w, ((0, LANES - w.shape[0]), (0, 0)))


def _plan_kernel(route_ref, cnt_ref, incl_ref, dest_ref, table_ref):
    f32_sum = lambda x, axis: jnp.sum(x, axis=axis, keepdims=True)
    lane = lax.broadcasted_iota(jnp.int32, (LANES, LANES), 1)
    sub = lax.broadcasted_iota(jnp.int32, (LANES, LANES), 0)
    incl = incl_ref[...]
    is_expert = (lane >= MOE_GROUPS) & (lane < MOE_GROUPS + MOE_EXPERTS)
    shift = MOE_ROWS.bit_length() - 1
    counts = jnp.broadcast_to(cnt_ref[...], (LANES, LANES)).astype(jnp.int32)
    padded = jnp.where(is_expert, ((counts + (MOE_ROWS - 1)) >> shift) << shift, 0)
    pad_end = _mm_sel_rhs(padded.astype(F32), incl)
    pad_start = pad_end - padded.astype(F32)

    route = route_ref[...]
    lane_t = lax.broadcasted_iota(jnp.int32, route.shape, 1)
    lane_f = lane_t.astype(F32)
    start_row = pad_start[0:1, :]
    slots = [f32_sum(jnp.where(lane_f == route[:, k:k + 1] + MOE_GROUPS, start_row, 0.0), 1) + route[:, 4 + k:5 + k]
             for k in range(2)]
    both = jnp.where(lane_t == 0, slots[0], jnp.where(lane_t == 1, slots[1], 0.0))
    dest_ref[...] = both.T[0:8, :].astype(jnp.int32)

    on_sub = lambda rows_equal: rows_equal.T
    expert_sub = (sub >= MOE_GROUPS) & (sub < MOE_GROUPS + MOE_EXPERTS)
    block_start = (lane * MOE_ROWS).astype(F32)
    eid = f32_sum(jnp.where(expert_sub & (on_sub(pad_end) <= block_start), 1.0, 0.0), 0)
    eid = jnp.minimum(eid, float(MOE_EXPERTS - 1))
    filled = on_sub(pad_start + counts.astype(F32))
    own = (sub - MOE_GROUPS).astype(F32) == eid
    valid = jnp.clip(f32_sum(jnp.where(own, filled, 0.0), 0) - block_start[0:1, :], 0.0, float(MOE_ROWS))
    eid_rows = jnp.broadcast_to(eid, (LANES, LANES))
    changed = (lane == 0) | (eid_rows != pltpu.roll(eid_rows, 1, axis=1))
    first = jnp.where((jnp.broadcast_to(valid, (LANES, LANES)) > 0) & changed, 1.0, 0.0)
    ordinal = _mm_sel_rhs(first, incl) - 1.0
    slot = ordinal - 2.0 * jnp.floor(ordinal * 0.5)
    later = (on_sub(first) > 0) & (sub > lane)
    nearest = jnp.min(jnp.where(later, sub, LANES), axis=0, keepdims=True)
    next_eid = f32_sum(jnp.where(sub == nearest, on_sub(eid_rows), 0.0), 0)
    next_eid = jnp.where(nearest < LANES, next_eid, -1.0)
    row8 = lax.broadcasted_iota(jnp.int32, (8, LANES), 0)
    table = jnp.zeros((8, LANES), F32)
    for k, val in enumerate((eid, valid, first[0:1, :], slot[0:1, :], next_eid)):
        table = jnp.where(row8 == k, val, table)
    table_ref[...] = table.astype(jnp.int32)


def moe_plan(route, counts, tm=4096):
    n_tok = route.shape[0]
    tm = min(tm, n_tok)
    idx = np.arange(LANES)
    incl = jnp.asarray(idx[:, None] <= idx[None, :], BF16)
    return pl.pallas_call(
        _plan_kernel,
        grid=(n_tok // tm,),
        in_specs=[pl.BlockSpec((tm, LANES), lambda i: (i, 0)), pl.BlockSpec((1, LANES), lambda i: (0, 0)),
                  pl.BlockSpec((LANES, LANES), lambda i: (0, 0))],
        out_specs=[pl.BlockSpec((8, tm), lambda i: (0, i)), pl.BlockSpec((8, LANES), lambda i: (0, 0))],
        out_shape=[jax.ShapeDtypeStruct((8, n_tok), jnp.int32), jax.ShapeDtypeStruct((8, LANES), jnp.int32)],
        compiler_params=_cparams("arbitrary"),
        name="moe_plan",
    )(route, counts, incl)


def _router_weights(w_group, b_group, w_expert, b_expert):
    w_r = _pad_cols(jnp.concatenate([w_group, w_expert], axis=1))
    w_hi = w_r.astype(BF16)
    w_lo = (w_r - w_hi.astype(F32)).astype(BF16)
    return w_hi, w_lo, _pad_lanes(jnp.concatenate([b_group, b_expert]))


def _moe_layer(h, xn, route, counts, w_gate, w_up, w_down, layer, final_g):
    n_tok, d = h.shape
    n_blocks = -(-(2 * n_tok + MOE_EXPERTS * (MOE_ROWS - 1)) // MOE_ROWS)
    dest, blocks = moe_plan(route, counts)
    xs = sc_scatter_rows(xn, dest, n_blocks * MOE_ROWS)
    ys = moe_experts(blocks, xs, w_gate, w_up, w_down, layer)
    y01 = sc_gather_rows(ys, dest, 2 * n_tok)
    g = jnp.ones((d,), F32) if final_g is None else final_g
    return moe_combine(h, y01, route, g, final_g is not None)


def _memory_kv(memn_in, mem_norm, wk, wv):
    bsz, m, d = memn_in.shape
    w = jnp.concatenate([wk, wv], axis=1).astype(BF16)
    kv, _ = rms_matmul(memn_in.reshape(bsz * m, d), mem_norm, w, jnp.zeros((d, LANES), BF16))
    k = kv[:, :d].reshape(bsz, m, d)
    v = kv[:, d:].reshape(bsz, m, d)
    return (jnp.swapaxes(k, 1, 2) * XA_HEAD_DIM ** -0.5).astype(BF16), v.astype(BF16)


def kernel(x, mem, mem_norm, final_norm, norm_mix, norm_xa, norm_ffn, xa_wq, xa_wk, xa_wv, xa_wo, moe_w_group, moe_b_group, moe_w_expert, moe_b_expert, moe_w_gate, moe_w_up, moe_w_down, ev_w_in, ev_sc_conv, ev_ssm_conv_w, ev_ssm_conv_b, ev_ssm_dt_bias, ev_ssm_a_log, ev_ssm_d, ev_ssm_norm, ev_w_out, od_w_in, od_gdn_conv, od_gdn_dt_bias, od_gdn_a_log, od_gdn_norm, od_w_out):
    bsz, length, d = x.shape
    n_tok = bsz * length
    depth = norm_mix.shape[0]
    h = x.reshape(n_tok, d)
    for layer in range(depth):
        i = layer // 2
        if layer % 2 == 0:
            wt = jnp.swapaxes(ev_w_in[i], 0, 1).astype(BF16)
            z0 = 3 * SC_DIM
            xbc0 = z0 + SSM_INNER
            w_conv = wt[xbc0:xbc0 + SSM_XBC]
            w_small = _pad_rows(wt[xbc0 + SSM_XBC:])
            xbc = rms_matmul_conv(h, norm_mix[layer], w_conv, ev_ssm_conv_w[i], ev_ssm_conv_b[i], length)
            z, ya, small = rms_matmul_gated(h, norm_mix[layer], wt[z0:xbc0], wt[:z0], w_small, ev_sc_conv[i], length)
            small3 = small.reshape(bsz, length, LANES)
            smallt = jnp.swapaxes(small3[:, :, :16], 1, 2)
            yb = ssd_mixer(xbc.reshape(bsz, length, -1), z.reshape(bsz, length, -1), small3, smallt,
                           ev_ssm_dt_bias[i], ev_ssm_a_log[i], ev_ssm_d[i], ev_ssm_norm[i])
            w_out = ev_w_out[i].astype(BF16)
            split = SC_DIM
        else:
            wt = jnp.swapaxes(od_w_in[i], 0, 1).astype(BF16)
            qkv_w = 3 * GDN_HEADS * GDN_D
            z_end = qkv_w + GDN_HEADS * GDN_D
            w_main = jnp.concatenate([wt[qkv_w:z_end], wt[z_end + 2 * GDN_HEADS:]], axis=0)
            w_small = _pad_rows(wt[z_end:z_end + 2 * GDN_HEADS])
            qkv = rms_matmul_conv(h, norm_mix[layer], wt, od_gdn_conv[i], jnp.zeros((qkv_w,), F32), length)
            y, small = rms_matmul(h, norm_mix[layer], w_main, w_small, tm=1024, tn=w_main.shape[0], transposed=True)
            y3 = y.reshape(bsz, length, -1)
            small3 = small.reshape(bsz, length, LANES)
            smallt = jnp.swapaxes(small3[:, :, :16], 1, 2)
            ya = gated_deltanet_mixer(qkv.reshape(bsz, length, -1), y3, small3, smallt, od_gdn_dt_bias[i],
                                      od_gdn_a_log[i], od_gdn_norm[i])
            yb = stick_breaking_mixer(y3, GDN_HEADS * GDN_D)
            w_out = od_w_out[i].astype(BF16)
            split = GDN_HEADS * GDN_D
        kt, v = _memory_kv(mem, mem_norm, xa_wk[layer], xa_wv[layer])
        w_hi, w_lo, bias = _router_weights(moe_w_group[layer], moe_b_group[layer], moe_w_expert[layer],
                                           moe_b_expert[layer])
        h, xn, route, counts = post_mixer(
            ya.reshape(n_tok, -1), yb.reshape(n_tok, -1), h, w_out[:split], w_out[split:], norm_xa[layer],
            xa_wq[layer].astype(BF16), kt, v, xa_wo[layer].astype(BF16), norm_ffn[layer], w_hi, w_lo, bias)
        h = _moe_layer(h, xn, route, counts, moe_w_gate, moe_w_up, moe_w_down, layer,
                       final_norm if layer == depth - 1 else None)
    return h.reshape(bsz, length, d)
```

```python
import functools

import jax
import jax.numpy as jnp
import numpy as np
from jax import lax
from jax.experimental import pallas as pl
from jax.experimental.pallas import tpu as pltpu
from jax.experimental.pallas import tpu_sc as plsc

F32 = jnp.float32
BF16 = jnp.bfloat16
EPS = 1e-6

SC_DIM = 512
SSM_HEADS = 16
SSM_HEAD_DIM = 64
SSM_INNER = 1024
SSM_GROUPS = 2
SSM_STATE = 128
SSM_XBC = SSM_INNER + 2 * SSM_GROUPS * SSM_STATE
SSD_CHUNK = 128
SSD_STEP_ROWS = 512
GDN_HEADS = 8
GDN_D = 128
GDN_CHUNK = 64
GDN_TILE = 128
GDN_STEP_ROWS = 512
SB_HEAD_DIM = 64
SB_DIM = 512
SB_BLOCK = 128
SB_STEP_HEADS = 8
SB_STEP_ROWS = 512
SB_EAGER_BLOCKS = 2
XA_HEADS = 4
XA_HEAD_DIM = 256
MOE_GROUPS = 4
MOE_PER_GROUP = 8
MOE_EXPERTS = 32
MOE_ROWS = 512
POST_GROUP_ROWS = 512
SC_CORES = 2
SC_SUBCORES = 16
SC_WORKERS = SC_CORES * SC_SUBCORES
SC_CHUNK = 64
HALO = 8
CONV_CHUNK = 512
LANES = 128
SB_LOG_ZERO = -104.0
VMEM_LIMIT = 56 * 1024 * 1024


def _cparams(*sem):
    return pltpu.CompilerParams(dimension_semantics=sem, vmem_limit_bytes=VMEM_LIMIT)


def _mm(a, b):
    return jnp.dot(a.astype(BF16), b.astype(BF16), preferred_element_type=F32)


def _mm_nt(a, b):
    return lax.dot_general(a.astype(BF16), b.astype(BF16), (((1,), (1,)), ((), ())),
                           preferred_element_type=F32)


def _split_bf16(x, n):
    parts, r = [], x
    for _ in range(n):
        p = r.astype(BF16)
        parts.append(p)
        r = r - p.astype(F32)
    return parts


def _mm_sel_rhs(x, sel, n=3):
    return sum(jnp.dot(p, sel, preferred_element_type=F32) for p in _split_bf16(x, n))


def _mm_sel_lhs(sel, x, n=3):
    return sum(jnp.dot(sel, p, preferred_element_type=F32) for p in _split_bf16(x, n))


def _spread_heads(x, first, n_heads, width):
    rows = x.shape[0]
    col = lambda h: jnp.broadcast_to(x[:, first + h:first + h + 1], (rows, LANES))
    if width == LANES:
        return jnp.concatenate([col(h) for h in range(n_heads)], axis=1)
    left = lax.broadcasted_iota(jnp.int32, (rows, LANES), 1) < width
    return jnp.concatenate([jnp.where(left, col(h), col(h + 1)) for h in range(0, n_heads, 2)], axis=1)


def _pack_halves(x):
    n = x.shape[1] // 2
    lo = pltpu.bitcast(x[:, :n].astype(BF16).astype(F32), jnp.int32)
    hi = pltpu.bitcast(x[:, n:].astype(BF16).astype(F32), jnp.int32)
    return lax.shift_right_logical(lo, 16) | (hi & jnp.int32(-65536))


def _unpack_halves(p):
    lo = pltpu.bitcast(lax.shift_left(p, 16), F32)
    hi = pltpu.bitcast(p & jnp.int32(-65536), F32)
    return jnp.concatenate([lo, hi], axis=1)


def _silu(x):
    half = 0.5 * x
    return half * jnp.tanh(half) + half


def _softplus(x):
    return jnp.maximum(x, 0.0) + jnp.log(1.0 + jnp.exp(-jnp.abs(x)))


def _rms(x, g):
    return x * lax.rsqrt(jnp.mean(x * x, axis=-1, keepdims=True) + EPS) * g


def _rms_matmul_kernel(x_ref, g_ref, w_ref, ws_ref, o_ref, os_ref, *, transposed):
    mm = _mm_nt if transposed else _mm
    xn = _rms(x_ref[...], g_ref[...]).astype(BF16)
    o_ref[...] = mm(xn, w_ref[...])
    os_ref[...] = mm(xn, ws_ref[...])


def rms_matmul(x, g, w, ws, tm=512, tn=512, transposed=False):
    m, k = x.shape
    n = w.shape[0] if transposed else w.shape[1]
    tm = min(tm, m)
    w_specs = ([pl.BlockSpec((tn, k), lambda j, i: (j, 0)), pl.BlockSpec((LANES, k), lambda j, i: (0, 0))]
               if transposed else
               [pl.BlockSpec((k, tn), lambda j, i: (0, j)), pl.BlockSpec((k, LANES), lambda j, i: (0, 0))])
    main, small = pl.pallas_call(
        functools.partial(_rms_matmul_kernel, transposed=transposed),
        grid=(n // tn, m // tm),
        in_specs=[
            pl.BlockSpec((tm, k), lambda j, i: (i, 0)),
            pl.BlockSpec((1, k), lambda j, i: (0, 0)),
            *w_specs,
        ],
        out_specs=[
            pl.BlockSpec((tm, tn), lambda j, i: (i, j)),
            pl.BlockSpec((None, tm, LANES), lambda j, i: (j, i, 0)),
        ],
        out_shape=[jax.ShapeDtypeStruct((m, n), F32), jax.ShapeDtypeStruct((n // tn, m, LANES), F32)],
        compiler_params=_cparams("parallel", "parallel"),
        name="rms_matmul",
    )(x, g.reshape(1, k), w, ws)
    return main, small[0]


def _causal_conv(ext_ref, w_ref, rows):
    width = w_ref.shape[0]
    ext = ext_ref[...]
    acc = None
    for j in range(width):
        shift = width - 1 - j
        moved = ext if shift == 0 else pltpu.roll(ext, shift, axis=0)
        term = w_ref[j:j + 1, :] * moved[HALO:HALO + rows, :]
        acc = term if acc is None else acc + term
    return acc


def _rms_matmul_conv_kernel(x_ref, g_ref, w_ref, cw_ref, cb_ref, o_ref, *ext_refs, tiles_per_seq):
    tm = x_ref.shape[0]
    starts_sequence = pl.program_id(1) % tiles_per_seq == 0

    @pl.when(starts_sequence)
    def _():
        for ext_ref in ext_refs:
            ext_ref[0:HALO, :] = jnp.zeros((HALO, CONV_CHUNK), F32)

    @pl.when(jnp.logical_not(starts_sequence))
    def _():
        for ext_ref in ext_refs:
            ext_ref[0:HALO, :] = ext_ref[tm:tm + HALO, :]

    xn = _rms(x_ref[...], g_ref[...]).astype(BF16)
    for c, ext_ref in enumerate(ext_refs):
        cols = slice(c * CONV_CHUNK, (c + 1) * CONV_CHUNK)
        ext_ref[HALO:, :] = _mm_nt(xn, w_ref[cols, :])
        o_ref[:, cols] = _causal_conv(ext_ref, cw_ref.at[:, cols], tm) + cb_ref[:, cols]


def rms_matmul_conv(x, g, w, conv_w, conv_b, seq_len, tm=1024, tn=1536):
    m, k = x.shape
    n = conv_w.shape[1]
    cols = lambda rows: pl.BlockSpec((rows, tn), lambda j, i: (0, j))
    return pl.pallas_call(
        functools.partial(_rms_matmul_conv_kernel, tiles_per_seq=seq_len // tm),
        grid=(n // tn, m // tm),
        in_specs=[
            pl.BlockSpec((tm, k), lambda j, i: (i, 0)),
            pl.BlockSpec((1, k), lambda j, i: (0, 0)),
            pl.BlockSpec((tn, k), lambda j, i: (j, 0)), cols(conv_w.shape[0]), cols(1),
        ],
        out_specs=pl.BlockSpec((tm, tn), lambda j, i: (i, j)),
        out_shape=jax.ShapeDtypeStruct((m, n), F32),
        scratch_shapes=[pltpu.VMEM((tm + HALO, CONV_CHUNK), F32)] * (tn // CONV_CHUNK),
        compiler_params=_cparams("arbitrary", "arbitrary"),
        name="rms_matmul_conv",
    )(x, g.reshape(1, k), w, conv_w, conv_b.reshape(1, n))


def _rms_matmul_gated_kernel(x_ref, g_ref, wz_ref, wbcx_ref, ws_ref, cw_ref, z_ref, ya_ref, os_ref, ext_ref,
                             *, tiles_per_seq):
    tm = x_ref.shape[0]
    starts_sequence = pl.program_id(0) % tiles_per_seq == 0

    @pl.when(starts_sequence)
    def _():
        ext_ref[0:HALO, :] = jnp.zeros((HALO, SC_DIM), F32)

    @pl.when(jnp.logical_not(starts_sequence))
    def _():
        ext_ref[0:HALO, :] = ext_ref[tm:tm + HALO, :]

    xn = _rms(x_ref[...], g_ref[...]).astype(BF16)
    z_ref[...] = _mm_nt(xn, wz_ref[...])
    os_ref[...] = _mm_nt(xn, ws_ref[...])
    bcx = _mm_nt(xn, wbcx_ref[...])
    ext_ref[HALO:, :] = bcx[:, SC_DIM:2 * SC_DIM] * bcx[:, 2 * SC_DIM:]
    ya_ref[...] = (bcx[:, :SC_DIM] * _causal_conv(ext_ref, cw_ref, tm)).astype(ya_ref.dtype)


def rms_matmul_gated(x, g, w_z, w_bcx, w_small, conv_w, seq_len, tm=1024):
    m, k = x.shape
    const = lambda a: pl.BlockSpec(a.shape, lambda i: (0,) * a.ndim)
    rows = lambda w: pl.BlockSpec((tm, w), lambda i: (i, 0))
    g = g.reshape(1, k)
    return pl.pallas_call(
        functools.partial(_rms_matmul_gated_kernel, tiles_per_seq=seq_len // tm),
        grid=(m // tm,),
        in_specs=[rows(k), const(g), const(w_z), const(w_bcx), const(w_small), const(conv_w)],
        out_specs=[rows(w_z.shape[0]), rows(SC_DIM), rows(LANES)],
        out_shape=[jax.ShapeDtypeStruct((m, w_z.shape[0]), F32), jax.ShapeDtypeStruct((m, SC_DIM), BF16),
                   jax.ShapeDtypeStruct((m, LANES), F32)],
        scratch_shapes=[pltpu.VMEM((tm + HALO, SC_DIM), F32)],
        compiler_params=_cparams("arbitrary"),
        name="rms_matmul_gated",
    )(x, g, w_z, w_bcx, w_small, conv_w)


def _ssd_kernel(xbc_ref, z_ref, dt_ref, dtt_ref, dtb_r_ref, dtb_c_ref,
                alog_r_ref, alog_c_ref, d_ref, nw_ref, tri_ref, trit_ref,
                o_ref, s_ref):
    q = SSD_CHUNK

    @pl.when(pl.program_id(1) == 0)
    def _():
        s_ref[...] = jnp.zeros_like(s_ref)

    for sub in range(xbc_ref.shape[0] // q):
        rows = slice(sub * q, (sub + 1) * q)
        _ssd_chunk(_silu(xbc_ref[rows, :]), z_ref[rows, :], dt_ref[rows, :], dtt_ref[:, rows], dtb_r_ref, dtb_c_ref,
                   alog_r_ref, alog_c_ref, d_ref, nw_ref, tri_ref, trit_ref, o_ref.at[rows, :], s_ref)


def _ssd_chunk(xbc, z, dt_raw, dtt_raw, dtb_r_ref, dtb_c_ref, alog_r_ref, alog_c_ref, d_ref, nw_ref, tri_ref,
               trit_ref, o_ref, s_ref):
    q = SSD_CHUNK
    hpg = SSM_HEADS // SSM_GROUPS
    gw = hpg * SSM_HEAD_DIM
    xs = xbc[:, :SSM_INNER]
    bm = xbc[:, SSM_INNER:SSM_INNER + SSM_GROUPS * SSM_STATE]
    cm = xbc[:, SSM_INNER + SSM_GROUPS * SSM_STATE:]

    dt = _softplus(dt_raw + dtb_r_ref[...])
    acs = _mm_sel_lhs(tri_ref[...], dt * -jnp.exp(alog_r_ref[...]))
    dtt = _softplus(dtt_raw + dtb_c_ref[...])
    acst = _mm_sel_rhs(dtt * -jnp.exp(alog_c_ref[...]), trit_ref[...])
    dt_full = _spread_heads(dt, 0, SSM_HEADS, SSM_HEAD_DIM)
    acs_full = _spread_heads(acs, 0, SSM_HEADS, SSM_HEAD_DIM)
    acs_col = _spread_heads(acs, 0, SSM_HEADS, q)

    xdt = xs * dt_full
    acs_last = acs_full[q - 1:q, :]
    xw = xdt * jnp.exp(acs_last - acs_full)
    chunk_decay = jnp.exp(acs_last)

    row = lax.broadcasted_iota(jnp.int32, (q, q), 0)
    col = lax.broadcasted_iota(jnp.int32, (q, q), 1)
    causal = row >= col
    lane = lax.broadcasted_iota(jnp.int32, (q, 2 * SSM_HEAD_DIM), 1)

    y_diag, y_off = [], []
    for g in range(SSM_GROUPS):
        bm_g = bm[:, g * SSM_STATE:(g + 1) * SSM_STATE]
        cm_g = cm[:, g * SSM_STATE:(g + 1) * SSM_STATE]
        cb_g = _mm_nt(cm_g, bm_g)
        state = s_ref[g]
        y_off.append(_mm(cm_g, state))
        s_ref[g] = state * chunk_decay[:, g * gw:(g + 1) * gw] + _mm(bm_g.T, xw[:, g * gw:(g + 1) * gw])
        for pair in range(hpg // 2):
            h0 = g * hpg + 2 * pair
            xdt_pair = xdt[:, h0 * SSM_HEAD_DIM:(h0 + 2) * SSM_HEAD_DIM]
            weights = []
            for h in (h0, h0 + 1):
                seg = acs_col[:, h * q:(h + 1) * q] - acst[h:h + 1, :]
                weights.append(cb_g * jnp.where(causal, jnp.exp(seg), 0.0))
            both = _mm(jnp.concatenate(weights, axis=0), xdt_pair)
            y_diag.append(jnp.where(lane < SSM_HEAD_DIM, both[:q], both[q:]))
    y = (jnp.concatenate(y_diag, axis=1) + jnp.concatenate(y_off, axis=1) * jnp.exp(acs_full)
         + xs * d_ref[...])
    y = y * _silu(z)
    halves = []
    for g in range(SSM_GROUPS):
        yg = y[:, g * gw:(g + 1) * gw]
        halves.append(yg * lax.rsqrt(jnp.mean(yg * yg, axis=-1, keepdims=True) + EPS))
    o_ref[...] = (jnp.concatenate(halves, axis=1) * nw_ref[...]).astype(o_ref.dtype)


def _pad_lanes(v, fill=0.0):
    return jnp.pad(v.astype(F32), (0, LANES - v.shape[0]), constant_values=fill).reshape(1, LANES)


def _pad_col(v, rows=16):
    return jnp.pad(v.astype(F32), (0, rows - v.shape[0])).reshape(rows, 1)


def ssd_mixer(xbc3, y3, small3, smallt, dt_bias, a_log, d_skip, norm_w):
    bsz, length, _ = y3.shape
    q = SSD_CHUNK
    tri = jnp.asarray(np.tril(np.ones((q, q), np.float32)), BF16)
    trit = jnp.asarray(np.triu(np.ones((q, q), np.float32)), BF16)
    d_full = jnp.repeat(d_skip.astype(F32), SSM_HEAD_DIM).reshape(1, SSM_INNER)
    const = lambda a: pl.BlockSpec(a.shape, lambda b, c: (0,) * a.ndim)
    args = [_pad_lanes(dt_bias), _pad_col(dt_bias), _pad_lanes(a_log),
            _pad_col(a_log), d_full, norm_w.reshape(1, -1), tri, trit]
    rows = min(SSD_STEP_ROWS, length)
    return pl.pallas_call(
        _ssd_kernel,
        grid=(bsz, length // rows),
        in_specs=[
            pl.BlockSpec((None, rows, SSM_XBC), lambda b, c: (b, c, 0)),
            pl.BlockSpec((None, rows, SSM_INNER), lambda b, c: (b, c, 0)),
            pl.BlockSpec((None, rows, LANES), lambda b, c: (b, c, 0)),
            pl.BlockSpec((None, 16, rows), lambda b, c: (b, 0, c)),
        ] + [const(a) for a in args],
        out_specs=pl.BlockSpec((None, rows, SSM_INNER), lambda b, c: (b, c, 0)),
        out_shape=jax.ShapeDtypeStruct((bsz, length, SSM_INNER), BF16),
        scratch_shapes=[pltpu.VMEM((SSM_GROUPS, SSM_STATE, SSM_INNER // SSM_GROUPS), F32)],
        compiler_params=_cparams("parallel", "arbitrary"),
        name="ssd_mixer",
    )(xbc3, y3, small3, smallt, *args)


def _unit_lower_inverse(mats, row, col):
    eye = jnp.where(row == col, 1.0, 0.0)
    blk = lambda n: (row >> (n.bit_length() - 1)) == (col >> (n.bit_length() - 1))
    size = row.shape[0]
    p = [jnp.where(blk(16), -a, 0.0) for a in mats]
    t = [eye + x for x in p]
    p = [_mm(x, x) for x in p]
    for _ in range(2):
        both = [_mm(jnp.concatenate([x, y], axis=0), x) for x, y in zip(p, t)]
        p = [b[:size] for b in both]
        t = [y + b[size:] for y, b in zip(t, both)]
    t = [y + _mm(y, x) for y, x in zip(t, p)]
    for n in (16, 32):
        band = blk(2 * n) & jnp.logical_not(blk(n))
        left = [_mm(y, jnp.where(band, a, 0.0)) for y, a in zip(t, mats)]
        t = [y - _mm(x, y) for y, x in zip(t, left)]
    return t


def _gdn_kernel(qkv_ref, z_ref, ab_ref, abt_ref, dtb_r_ref, dtb_c_ref, alog_r_ref,
                alog_c_ref, nw_ref, tri_ref, trit_ref, o_ref, s_ref):
    n = GDN_TILE

    @pl.when(pl.program_id(1) == 0)
    def _():
        s_ref[...] = jnp.zeros_like(s_ref)

    for sub in range(qkv_ref.shape[0] // n):
        rows = slice(sub * n, (sub + 1) * n)
        _gdn_tile(_silu(qkv_ref[rows, :]), z_ref[rows, :], ab_ref[rows, :], abt_ref[:, rows], dtb_r_ref, dtb_c_ref,
                  alog_r_ref, alog_c_ref, nw_ref, tri_ref, trit_ref, o_ref.at[rows, :], s_ref)


def _gdn_tile(qkv, z, ab, abt, dtb_r_ref, dtb_c_ref, alog_r_ref, alog_c_ref, nw_ref, tri_ref, trit_ref, o_ref, s_ref):
    n = GDN_TILE
    c = GDN_CHUNK
    d = GDN_D
    hd = GDN_HEADS * d
    g = -jnp.exp(alog_r_ref[...]) * _softplus(ab + dtb_r_ref[...])
    gc_full = _spread_heads(_mm_sel_lhs(tri_ref[...], g), 0, GDN_HEADS, d)
    beta_full = _spread_heads(jax.nn.sigmoid(ab), GDN_HEADS, GDN_HEADS, d)
    gt = -jnp.exp(alog_c_ref[...]) * _softplus(abt + dtb_c_ref[...])
    gct = _mm_sel_rhs(gt, trit_ref[...])

    row = lax.broadcasted_iota(jnp.int32, (n, n), 0)
    col = lax.broadcasted_iota(jnp.int32, (n, n), 1)
    same = (row >> (c.bit_length() - 1)) == (col >> (c.bit_length() - 1))
    incl = same & (row >= col)
    strict = same & (row > col)
    zeros_half = jnp.zeros((c, d), F32)

    heads = range(GDN_HEADS)
    sl = [slice(h * d, (h + 1) * d) for h in heads]
    l2n = lambda x: x * lax.rsqrt(jnp.sum(x * x, axis=-1, keepdims=True) + EPS)
    qn = [l2n(qkv[:, sl[h]]) * (d ** -0.5) for h in heads]
    kn = [l2n(qkv[:, hd + h * d:hd + (h + 1) * d]) for h in heads]
    vh = [qkv[:, 2 * hd + h * d:2 * hd + (h + 1) * d] for h in heads]
    gcol = [gc_full[:, sl[h]] for h in heads]
    beta = [beta_full[:, sl[h]] for h in heads]
    edec = [jnp.exp(gcol[h] - gct[h:h + 1, :]) for h in heads]
    egc = [jnp.exp(x) for x in gcol]
    kb = [kn[h] * beta[h] for h in heads]
    on_k = [_mm_nt(jnp.concatenate([kb[h], qn[h]], axis=0), kn[h]) for h in heads]
    lower = [jnp.where(strict, on_k[h][:n] * edec[h], 0.0) for h in heads]
    aqk = [jnp.where(incl, on_k[h][n:] * edec[h], 0.0) for h in heads]
    tinv = _unit_lower_inverse(lower, row, col)
    sol = [_mm(tinv[h], jnp.concatenate([vh[h] * beta[h], kb[h] * egc[h]], axis=1)) for h in heads]
    qd = [qn[h] * egc[h] for h in heads]
    glast = [(gcol[h][c - 1:c, :], gcol[h][n - 1:n, :]) for h in heads]
    kdt = [(kn[h] * jnp.exp(jnp.concatenate([jnp.broadcast_to(glast[h][0], (c, d)),
                                             jnp.broadcast_to(glast[h][1], (c, d))], axis=0) - gcol[h])).T
           for h in heads]
    s0 = [s_ref[h] for h in heads]
    on_s0 = [_mm(jnp.concatenate([sol[h][:c, d:], qd[h][:c]], axis=0), s0[h]) for h in heads]
    v0 = [sol[h][:c, :d] - on_s0[h][:c] for h in heads]
    s1 = [s0[h] * jnp.exp(glast[h][0]) + _mm(kdt[h], jnp.concatenate([v0[h], zeros_half], axis=0)) for h in heads]
    on_s1 = [_mm(jnp.concatenate([sol[h][c:, d:], qd[h][c:]], axis=0), s1[h]) for h in heads]
    v1 = [sol[h][c:, :d] - on_s1[h][:c] for h in heads]
    for h in heads:
        s_ref[h] = s1[h] * jnp.exp(glast[h][1]) + _mm(kdt[h], jnp.concatenate([zeros_half, v1[h]], axis=0))
    outs = []
    for h in heads:
        o = (jnp.concatenate([on_s0[h][c:], on_s1[h][c:]], axis=0)
             + _mm(aqk[h], jnp.concatenate([v0[h], v1[h]], axis=0)))
        o = o * lax.rsqrt(jnp.mean(o * o, axis=-1, keepdims=True) + EPS) * nw_ref[...]
        outs.append(o * _silu(z[:, sl[h]]))
    o_ref[...] = jnp.concatenate(outs, axis=1).astype(o_ref.dtype)


def gated_deltanet_mixer(qkv3, y3, small3, smallt, dt_bias, a_log, norm_w):
    bsz, length, _ = y3.shape
    n = GDN_TILE
    hd = GDN_HEADS * GDN_D
    idx = np.arange(n)
    same = (idx[:, None] // GDN_CHUNK) == (idx[None, :] // GDN_CHUNK)
    tri = jnp.asarray(same & (idx[:, None] >= idx[None, :]), BF16)
    trit = jnp.asarray(same & (idx[:, None] <= idx[None, :]), BF16)
    const = lambda a: pl.BlockSpec(a.shape, lambda b, c: (0,) * a.ndim)
    args = [_pad_lanes(dt_bias), _pad_col(dt_bias), _pad_lanes(a_log), _pad_col(a_log),
            norm_w.reshape(1, -1), tri, trit]
    rows = min(GDN_STEP_ROWS, length)
    return pl.pallas_call(
        _gdn_kernel,
        grid=(bsz, length // rows),
        in_specs=[
            pl.BlockSpec((None, rows, 3 * hd), lambda b, c: (b, c, 0)),
            pl.BlockSpec((None, rows, hd), lambda b, c: (b, c, 0)),
            pl.BlockSpec((None, rows, LANES), lambda b, c: (b, c, 0)),
            pl.BlockSpec((None, 16, rows), lambda b, c: (b, 0, c)),
        ] + [const(a) for a in args],
        out_specs=pl.BlockSpec((None, rows, hd), lambda b, c: (b, c, 0)),
        out_shape=jax.ShapeDtypeStruct((bsz, length, hd), BF16),
        scratch_shapes=[pltpu.VMEM((GDN_HEADS, GDN_D, GDN_D), F32)],
        compiler_params=_cparams("parallel", "arbitrary"),
        name="gated_deltanet",
    )(qkv3, y3, small3, smallt, *args)


def _sb_kernel(q_ref, k_ref, v_ref, upper_ref, o_ref):
    blk = SB_BLOCK
    n_sub = q_ref.shape[0] // blk
    first = pl.program_id(2) * n_sub
    parts = [_sb_query_block(first + s, q_ref[s * blk:(s + 1) * blk, :], k_ref, v_ref, upper_ref[...])
             for s in range(n_sub)]
    for s, finish in enumerate(parts):
        o_ref[s * blk:(s + 1) * blk, :] = finish().astype(o_ref.dtype)


def _sb_query_block(i, q, k_ref, v_ref, upper):
    blk = SB_BLOCK
    pair_w = 2 * SB_HEAD_DIM
    n_pairs = SB_STEP_HEADS // 2
    q = q * (SB_HEAD_DIM ** -0.5)
    lane = lax.broadcasted_iota(jnp.int32, (blk, pair_w), 1)
    first_head = lane < SB_HEAD_DIM
    qs = []
    for p in range(n_pairs):
        q2 = q[:, p * pair_w:(p + 1) * pair_w]
        qs += [jnp.where(first_head, q2, 0.0).astype(BF16), jnp.where(first_head, 0.0, q2).astype(BF16)]
    row = lax.broadcasted_iota(jnp.int32, (blk, blk), 0)
    col = lax.broadcasted_iota(jnp.int32, (blk, blk), 1)
    earlier = col < row
    heads = range(SB_STEP_HEADS)

    def local_part(kb, diagonal, exists=None):
        start = pl.multiple_of(kb * blk, blk)
        k = k_ref[pl.ds(start, blk), :].astype(BF16)
        v = v_ref[pl.ds(start, blk), :].astype(BF16)
        kp = [k[:, p * pair_w:(p + 1) * pair_w] for p in range(n_pairs)]
        vp = [v[:, p * pair_w:(p + 1) * pair_w] for p in range(n_pairs)]
        logits = [lax.dot_general(qs[h], kp[h // 2], (((1,), (1,)), ((), ())), preferred_element_type=F32)
                  for h in heads]
        keep = earlier if diagonal else None
        if exists is not None:
            keep = exists if keep is None else keep & exists
        log_keep = [-_softplus(x) for x in logits]
        if keep is not None:
            log_keep = [jnp.where(keep, x, 0.0) for x in log_keep]
        inside = [_mm_sel_rhs(x, upper, 2) for x in log_keep]
        totals = [jnp.sum(x, axis=-1, keepdims=True) for x in log_keep]
        return logits, log_keep, inside, totals, vp, keep

    def carried_part(local, accs, sticks):
        logits, log_keep, inside, totals, vp, keep = local
        w = [jnp.exp(logits[h] + log_keep[h] + inside[h] + sticks[h]) for h in heads]
        if keep is not None:
            w = [jnp.where(keep, x, 0.0) for x in w]
        pv = [jnp.dot(w[h].astype(BF16), vp[h // 2], preferred_element_type=F32) for h in heads]
        accs = tuple(accs[p] + jnp.where(first_head, pv[2 * p], pv[2 * p + 1]) for p in range(n_pairs))
        sticks = tuple(sticks[h] + totals[h] for h in heads)
        return accs, sticks

    accs = tuple(jnp.zeros((blk, pair_w), F32) for _ in range(n_pairs))
    sticks = tuple(jnp.zeros((blk, 1), F32) for _ in heads)
    eager = [local_part(i, True)]
    for back in range(1, SB_EAGER_BLOCKS + 1):
        eager.append(local_part(jnp.maximum(i - back, 0), False, exists=(row >= 0) & (i - back >= 0)))
    for local in eager:
        accs, sticks = carried_part(local, accs, sticks)

    def alive(state):
        kb, _, sticks = state
        longest = sticks[0]
        for s in sticks[1:]:
            longest = jnp.maximum(longest, s)
        return (kb >= 0) & (jnp.max(longest) > SB_LOG_ZERO)

    def body(state):
        kb, accs, sticks = state
        accs, sticks = carried_part(local_part(kb, False), accs, sticks)
        return kb - 1, accs, sticks

    def finish():
        _, done, _ = lax.while_loop(alive, body, (i - 1 - SB_EAGER_BLOCKS, accs, sticks))
        return jnp.concatenate(done, axis=1)

    return finish


def stick_breaking_mixer(y3, col0):
    bsz, length, _ = y3.shape
    blk = SB_BLOCK
    step_w = SB_STEP_HEADS * SB_HEAD_DIM
    steps = SB_DIM // step_w
    q0 = col0 // step_w
    idx = np.arange(blk)
    upper = jnp.asarray(idx[:, None] > idx[None, :], BF16)
    resident = lambda off: pl.BlockSpec((None, length, step_w), lambda b, p, i: (b, 0, q0 + off + p),
                                        pipeline_mode=pl.Buffered(1))
    rows = min(SB_STEP_ROWS, length)
    return pl.pallas_call(
        _sb_kernel,
        grid=(bsz, steps, length // rows),
        in_specs=[
            pl.BlockSpec((None, rows, step_w), lambda b, p, i: (b, i, q0 + p)),
            resident(steps),
            resident(2 * steps),
            pl.BlockSpec((blk, blk), lambda b, p, i: (0, 0)),
        ],
        out_specs=pl.BlockSpec((None, rows, step_w), lambda b, p, i: (b, i, p)),
        out_shape=jax.ShapeDtypeStruct((bsz, length, SB_DIM), BF16),
        compiler_params=_cparams("parallel", "parallel", "arbitrary"),
        name="stick_breaking",
    )(y3, y3, y3, upper)


def _mixer_out(a_ref, b_ref, h_ref, wa_ref, wb_ref, rows):
    return h_ref[rows, :] + (jnp.dot(a_ref[rows, :].astype(BF16), wa_ref[...], preferred_element_type=F32)
                             + jnp.dot(b_ref[rows, :].astype(BF16), wb_ref[...], preferred_element_type=F32))


def _cross_attention(h, g_ref, wq_ref, kt_ref, v_ref, wo_ref):
    u = _rms(h, g_ref[...]).astype(BF16)
    q = jnp.dot(u, wq_ref[...], preferred_element_type=F32)
    heads = []
    for hd in range(XA_HEADS):
        sl = slice(hd * XA_HEAD_DIM, (hd + 1) * XA_HEAD_DIM)
        s = jnp.dot(q[:, sl].astype(BF16), kt_ref[sl, :], preferred_element_type=F32)
        p = jnp.exp(s - jnp.max(s, axis=-1, keepdims=True))
        p = p * (1.0 / jnp.sum(p, axis=-1, keepdims=True))
        heads.append(jnp.dot(p.astype(BF16), v_ref[:, sl], preferred_element_type=F32))
    o = jnp.concatenate(heads, axis=1).astype(BF16)
    return h + jnp.dot(o, wo_ref[...], preferred_element_type=F32)


def _route(xn, whi_ref, wlo_ref, b_ref, before_ref, run_ref):
    x_hi = xn.astype(BF16)
    x_lo = (xn - x_hi.astype(F32)).astype(BF16)
    wide = jnp.dot(x_hi, jnp.concatenate([whi_ref[...], wlo_ref[...]], axis=1), preferred_element_type=F32)
    logits = (wide[:, :LANES] + jnp.dot(x_lo, whi_ref[...], preferred_element_type=F32)
              + wide[:, LANES:] + b_ref[...])
    lane = lax.broadcasted_iota(jnp.int32, logits.shape, 1).astype(F32)
    neg = -1e30
    none = float(LANES)

    def top(vals):
        best = jnp.max(vals, axis=-1, keepdims=True)
        where = jnp.min(jnp.where(vals == best, lane, none), axis=-1, keepdims=True)
        return best, where

    gl = jnp.where(lane < MOE_GROUPS, logits, neg)
    gbest, gsel = top(gl)
    gprob = 1.0 / jnp.sum(jnp.exp(gl - gbest), axis=-1, keepdims=True)
    lo = MOE_GROUPS + gsel * MOE_PER_GROUP
    el = jnp.where((lane >= lo) & (lane < lo + MOE_PER_GROUP), logits, neg)
    m1, i1 = top(el)
    m2, i2 = top(jnp.where(lane == i1, neg, el))
    e = jnp.exp(m2 - m1)
    gate1 = gprob / (1.0 + e)
    gate2 = gprob * e / (1.0 + e)

    hot1 = lane == i1
    hot2 = lane == i2
    one1 = jnp.where(hot1, 1.0, 0.0)
    one2 = jnp.where(hot2, 1.0, 0.0)
    prefix = jnp.dot(before_ref[...], jnp.concatenate([one1, one2], axis=1).astype(BF16), preferred_element_type=F32)
    prefix1, prefix2 = prefix[:, :LANES], prefix[:, LANES:]
    total1 = jnp.sum(one1, axis=0, keepdims=True)
    running = run_ref[...]
    rank1 = jnp.sum(jnp.where(hot1, prefix1 + running, 0.0), axis=-1, keepdims=True)
    rank2 = jnp.sum(jnp.where(hot2, prefix2 + (running + total1), 0.0), axis=-1, keepdims=True)
    running = running + total1 + jnp.sum(one2, axis=0, keepdims=True)
    run_ref[...] = running

    fields = (i1 - MOE_GROUPS, i2 - MOE_GROUPS, gate1, gate2, rank1, rank2)
    out = jnp.zeros_like(logits)
    for k, val in enumerate(fields):
        out = jnp.where(lane == k, val, out)
    return out


def _post_mixer_kernel(a_ref, b_ref, h_ref, wa_ref, wb_ref, gxa_ref, wq_ref, kt_ref, v_ref, wo_ref,
                       gffn_ref, whi_ref, wlo_ref, bias_ref, before_ref,
                       h_out_ref, xn_ref, r_ref, cnt_ref, run_ref):
    @pl.when(pl.program_id(0) == 0)
    def _():
        run_ref[...] = jnp.zeros_like(run_ref)

    h = _mixer_out(a_ref, b_ref, h_ref, wa_ref, wb_ref, slice(None))
    h = _cross_attention(h, gxa_ref, wq_ref, kt_ref, v_ref, wo_ref)
    h_out_ref[...] = h
    xn = _rms(h, gffn_ref[...])
    xn_ref[...] = _pack_halves(xn)
    group = before_ref.shape[0]
    for start in range(0, h_ref.shape[0], group):
        rows = slice(start, start + group)
        r_ref[rows, :] = _route(xn[rows, :], whi_ref, wlo_ref, bias_ref, before_ref, run_ref)
    cnt_ref[...] = run_ref[...]


def post_mixer(ya, yb, h, wa, wb, g_xa, wq, kt, v, wo, g_ffn, w_hi, w_lo, bias, tm=1024):
    m, d = h.shape
    tiles_per_batch = m // kt.shape[0] // tm
    idx = np.arange(min(POST_GROUP_ROWS, tm))
    before = jnp.asarray(idx[:, None] > idx[None, :], BF16)
    rows = lambda w: pl.BlockSpec((tm, w), lambda i: (i, 0))
    const = lambda a: pl.BlockSpec(a.shape, lambda i: (0,) * a.ndim, pipeline_mode=pl.Buffered(1))
    per_batch = lambda a: pl.BlockSpec((None,) + a.shape[1:], lambda i: (i // tiles_per_batch, 0, 0))
    g_xa, g_ffn = g_xa.reshape(1, d), g_ffn.reshape(1, d)
    return pl.pallas_call(
        _post_mixer_kernel,
        grid=(m // tm,),
        in_specs=[rows(ya.shape[1]), rows(yb.shape[1]), rows(d), const(wa), const(wb), const(g_xa), const(wq),
                  per_batch(kt), per_batch(v), const(wo), const(g_ffn), const(w_hi), const(w_lo), const(bias),
                  const(before)],
        out_specs=[rows(d), rows(d // 2), rows(LANES), pl.BlockSpec((1, LANES), lambda i: (0, 0))],
        out_shape=[jax.ShapeDtypeStruct((m, d), F32), jax.ShapeDtypeStruct((m, d // 2), jnp.int32),
                   jax.ShapeDtypeStruct((m, LANES), F32), jax.ShapeDtypeStruct((1, LANES), F32)],
        scratch_shapes=[pltpu.VMEM((1, LANES), F32)],
        compiler_params=_cparams("arbitrary"),
        name="post_mixer",
    )(ya, yb, h, wa, wb, g_xa, wq, kt, v, wo, g_ffn, w_hi, w_lo, bias, before)


def _expert_kernel(table_ref, x_ref, wg_hbm, wu_hbm, wd_hbm, o_ref,
                   wg32_ref, wu32_ref, wd32_ref, wgb_ref, wub_ref, wdb_ref, sem_ref, *, layer):
    i = pl.program_id(0)
    beid_ref, valid_ref, first_ref, slot_ref, next_ref = (table_ref.at[k] for k in range(5))
    valid = valid_ref[i]

    def weight_copies(expert, slot):
        return (pltpu.make_async_copy(wg_hbm.at[layer, expert], wg32_ref.at[slot], sem_ref.at[slot, 0]),
                pltpu.make_async_copy(wu_hbm.at[layer, expert], wu32_ref.at[slot], sem_ref.at[slot, 1]),
                pltpu.make_async_copy(wd_hbm.at[layer, expert], wd32_ref.at[slot], sem_ref.at[slot, 2]))

    @pl.when(i == 0)
    def _():
        for copy in weight_copies(beid_ref[0], 0):
            copy.start()

    @pl.when(first_ref[i] == 1)
    def _():
        slot = slot_ref[i]
        for copy in weight_copies(beid_ref[i], slot):
            copy.wait()
        wgb_ref[...] = wg32_ref[slot].astype(BF16)
        wub_ref[...] = wu32_ref[slot].astype(BF16)
        wdb_ref[...] = wd32_ref[slot].astype(BF16)

        @pl.when(next_ref[i] >= 0)
        def _():
            for copy in weight_copies(next_ref[i], 1 - slot):
                copy.start()

    half = MOE_ROWS // 2

    def ffn(n_halves):
        row = lax.broadcasted_iota(jnp.int32, (half, 2 * x_ref.shape[1]), 0)
        xs = [jnp.where(row + k * half < valid, _unpack_halves(x_ref[k * half:(k + 1) * half, :]), 0.0).astype(BF16)
              for k in range(n_halves)]
        gates = [jnp.dot(x, wgb_ref[...], preferred_element_type=F32) for x in xs]
        ups = [jnp.dot(x, wub_ref[...], preferred_element_type=F32) for x in xs]
        acts = [(_silu(g) * u).astype(BF16) for g, u in zip(gates, ups)]
        for k, act in enumerate(acts):
            o_ref[k * half:(k + 1) * half, :] = _pack_halves(jnp.dot(act, wdb_ref[...], preferred_element_type=F32))

    @pl.when(valid > half)
    def _():
        ffn(2)

    @pl.when((valid > 0) & (valid <= half))
    def _():
        ffn(1)
        o_ref[half:, :] = jnp.zeros((half, o_ref.shape[1]), o_ref.dtype)

    @pl.when(valid == 0)
    def _():
        o_ref[...] = jnp.zeros_like(o_ref)


def moe_experts(blocks, xs, w_gate, w_up, w_down, layer):
    n_slots, packed = xs.shape
    d = 2 * packed
    rows = MOE_ROWS
    ff = w_gate.shape[3]
    grid_spec = pltpu.PrefetchScalarGridSpec(
        num_scalar_prefetch=1,
        grid=(n_slots // rows,),
        in_specs=[
            pl.BlockSpec((rows, packed), lambda i, *_: (i, 0)),
            pl.BlockSpec(memory_space=pl.ANY),
            pl.BlockSpec(memory_space=pl.ANY),
            pl.BlockSpec(memory_space=pl.ANY),
        ],
        out_specs=pl.BlockSpec((rows, packed), lambda i, *_: (i, 0)),
        scratch_shapes=[pltpu.VMEM((2, d, ff), F32), pltpu.VMEM((2, d, ff), F32), pltpu.VMEM((2, ff, d), F32),
                        pltpu.VMEM((d, ff), BF16), pltpu.VMEM((d, ff), BF16), pltpu.VMEM((ff, d), BF16),
                        pltpu.SemaphoreType.DMA((2, 3))],
    )
    return pl.pallas_call(
        functools.partial(_expert_kernel, layer=layer),
        grid_spec=grid_spec,
        out_shape=jax.ShapeDtypeStruct((n_slots, packed), jnp.int32),
        compiler_params=_cparams("arbitrary"),
        name="moe_experts",
    )(blocks, xs, w_gate, w_up, w_down)


def _sc_mesh():
    return plsc.VectorSubcoreMesh(core_axis_name="c", subcore_axis_name="s",
                                  num_cores=SC_CORES, num_subcores=SC_SUBCORES)


def _sc_worker():
    return lax.axis_index("s") * SC_CORES + lax.axis_index("c")


def _sc_double_buffered(n_chunks, fetch, drain):
    assert n_chunks % 2 == 0
    start = lambda copies: [c.start() for c in copies]
    wait = lambda copies: [c.wait() for c in copies]
    start(fetch(0, 0))

    @pl.loop(0, n_chunks, step=2)
    def _(j):
        wait(fetch(j, 0))

        @pl.when(j > 0)
        def _():
            wait(drain(j - 1, 1))

        start(fetch(j + 1, 1))
        start(drain(j, 0))
        wait(fetch(j + 1, 1))
        wait(drain(j, 0))

        @pl.when(j + 2 < n_chunks)
        def _():
            start(fetch(j + 2, 0))

        start(drain(j + 1, 1))

    wait(drain(n_chunks - 1, 1))


def sc_scatter_rows(x, dest, n_slots):
    n_tok, d = x.shape
    per_worker = n_tok // SC_WORKERS
    n_chunks = per_worker // SC_CHUNK
    by_worker = dest.reshape(dest.shape[0] * SC_WORKERS, n_chunks, SC_CHUNK)

    @functools.partial(
        pl.kernel, mesh=_sc_mesh(), out_type=jax.ShapeDtypeStruct((n_slots, d), x.dtype),
        scratch_types=[pltpu.VMEM((n_chunks, SC_CHUNK), jnp.int32), pltpu.VMEM((n_chunks, SC_CHUNK), jnp.int32),
                       pltpu.VMEM((2, SC_CHUNK, d), x.dtype), pltpu.SemaphoreType.DMA((2, 3))],
        name="moe_scatter_rows")
    def scatter(x_hbm, dest_hbm, out_hbm, i0_v, i1_v, rows_v, sem):
        wid = _sc_worker()
        pltpu.sync_copy(dest_hbm.at[wid], i0_v)
        pltpu.sync_copy(dest_hbm.at[SC_WORKERS + wid], i1_v)

        def fetch(j, buf):
            start = pl.multiple_of(wid * per_worker + j * SC_CHUNK, SC_CHUNK)
            return [pltpu.make_async_copy(x_hbm.at[pl.ds(start, SC_CHUNK)], rows_v.at[buf], sem.at[buf, 0])]

        def drain(j, buf):
            return [pltpu.make_async_copy(rows_v.at[buf], out_hbm.at[i0_v.at[j]], sem.at[buf, 1]),
                    pltpu.make_async_copy(rows_v.at[buf], out_hbm.at[i1_v.at[j]], sem.at[buf, 2])]

        _sc_double_buffered(n_chunks, fetch, drain)

    return scatter(x, by_worker)


def sc_gather_rows(table, idx, n_out):
    d = table.shape[1]
    per_worker = n_out // SC_WORKERS
    n_chunks = per_worker // SC_CHUNK

    @functools.partial(
        pl.kernel, mesh=_sc_mesh(), out_type=jax.ShapeDtypeStruct((n_out, d), table.dtype),
        scratch_types=[pltpu.VMEM((n_chunks, SC_CHUNK), jnp.int32), pltpu.VMEM((2, SC_CHUNK, d), table.dtype),
                       pltpu.SemaphoreType.DMA((2, 2))],
        name="moe_gather_rows")
    def gather(table_hbm, idx_hbm, out_hbm, idx_v, rows_v, sem):
        wid = _sc_worker()
        pltpu.sync_copy(idx_hbm.at[wid], idx_v)

        def fetch(j, buf):
            return [pltpu.make_async_copy(table_hbm.at[idx_v.at[j]], rows_v.at[buf], sem.at[buf, 0])]

        def drain(j, buf):
            start = pl.multiple_of(wid * per_worker + j * SC_CHUNK, SC_CHUNK)
            return [pltpu.make_async_copy(rows_v.at[buf], out_hbm.at[pl.ds(start, SC_CHUNK)], sem.at[buf, 1])]

        _sc_double_buffered(n_chunks, fetch, drain)

    return gather(table, idx.reshape(-1, n_chunks, SC_CHUNK))


def _combine_kernel(h_ref, y0_ref, y1_ref, r_ref, g_ref, o_ref, *, final_norm):
    route = r_ref[...]
    h = h_ref[...] + (route[:, 2:3] * _unpack_halves(y0_ref[...]) + route[:, 3:4] * _unpack_halves(y1_ref[...]))
    o_ref[...] = _rms(h, g_ref[...]) if final_norm else h


def moe_combine(h, y01, route, g, final_norm, tm=1024):
    m, d = h.shape
    tm = min(tm, m)
    rows = lambda w: pl.BlockSpec((tm, w), lambda i: (i, 0))
    return pl.pallas_call(
        functools.partial(_combine_kernel, final_norm=final_norm),
        grid=(m // tm,),
        in_specs=[rows(d), rows(d // 2), pl.BlockSpec((tm, d // 2), lambda i: (i + m // tm, 0)), rows(LANES),
                  pl.BlockSpec((1, d), lambda i: (0, 0))],
        out_specs=rows(d),
        out_shape=jax.ShapeDtypeStruct((m, d), F32),
        compiler_params=_cparams("parallel"),
        name="moe_combine",
    )(h, y01, y01, route, g.reshape(1, d))


def _pad_cols(w):
    return jnp.pad(w, ((0, 0), (0, LANES - w.shape[1])))


def _pad_rows(w):
    return jnp.pad(w, ((0, LANES - w.shape[0]), (0, 0)))


def _plan_kernel(route_ref, cnt_ref, incl_ref, dest_ref, table_ref):
    f32_sum = lambda x, axis: jnp.sum(x, axis=axis, keepdims=True)
    lane = lax.broadcasted_iota(jnp.int32, (LANES, LANES), 1)
    sub = lax.broadcasted_iota(jnp.int32, (LANES, LANES), 0)
    incl = incl_ref[...]
    is_expert = (lane >= MOE_GROUPS) & (lane < MOE_GROUPS + MOE_EXPERTS)
    shift = MOE_ROWS.bit_length() - 1
    counts = jnp.broadcast_to(cnt_ref[...], (LANES, LANES)).astype(jnp.int32)
    padded = jnp.where(is_expert, ((counts + (MOE_ROWS - 1)) >> shift) << shift, 0)
    pad_end = _mm_sel_rhs(padded.astype(F32), incl)
    pad_start = pad_end - padded.astype(F32)

    route = route_ref[...]
    lane_t = lax.broadcasted_iota(jnp.int32, route.shape, 1)
    lane_f = lane_t.astype(F32)
    start_row = pad_start[0:1, :]
    slots = [f32_sum(jnp.where(lane_f == route[:, k:k + 1] + MOE_GROUPS, start_row, 0.0), 1) + route[:, 4 + k:5 + k]
             for k in range(2)]
    both = jnp.where(lane_t == 0, slots[0], jnp.where(lane_t == 1, slots[1], 0.0))
    dest_ref[...] = both.T[0:8, :].astype(jnp.int32)

    on_sub = lambda rows_equal: rows_equal.T
    expert_sub = (sub >= MOE_GROUPS) & (sub < MOE_GROUPS + MOE_EXPERTS)
    block_start = (lane * MOE_ROWS).astype(F32)
    eid = f32_sum(jnp.where(expert_sub & (on_sub(pad_end) <= block_start), 1.0, 0.0), 0)
    eid = jnp.minimum(eid, float(MOE_EXPERTS - 1))
    filled = on_sub(pad_start + counts.astype(F32))
    own = (sub - MOE_GROUPS).astype(F32) == eid
    valid = jnp.clip(f32_sum(jnp.where(own, filled, 0.0), 0) - block_start[0:1, :], 0.0, float(MOE_ROWS))
    eid_rows = jnp.broadcast_to(eid, (LANES, LANES))
    changed = (lane == 0) | (eid_rows != pltpu.roll(eid_rows, 1, axis=1))
    first = jnp.where((jnp.broadcast_to(valid, (LANES, LANES)) > 0) & changed, 1.0, 0.0)
    ordinal = _mm_sel_rhs(first, incl) - 1.0
    slot = ordinal - 2.0 * jnp.floor(ordinal * 0.5)
    later = (on_sub(first) > 0) & (sub > lane)
    nearest = jnp.min(jnp.where(later, sub, LANES), axis=0, keepdims=True)
    next_eid = f32_sum(jnp.where(sub == nearest, on_sub(eid_rows), 0.0), 0)
    next_eid = jnp.where(nearest < LANES, next_eid, -1.0)
    row8 = lax.broadcasted_iota(jnp.int32, (8, LANES), 0)
    table = jnp.zeros((8, LANES), F32)
    for k, val in enumerate((eid, valid, first[0:1, :], slot[0:1, :], next_eid)):
        table = jnp.where(row8 == k, val, table)
    table_ref[...] = table.astype(jnp.int32)


def moe_plan(route, counts, tm=4096):
    n_tok = route.shape[0]
    tm = min(tm, n_tok)
    idx = np.arange(LANES)
    incl = jnp.asarray(idx[:, None] <= idx[None, :], BF16)
    return pl.pallas_call(
        _plan_kernel,
        grid=(n_tok // tm,),
        in_specs=[pl.BlockSpec((tm, LANES), lambda i: (i, 0)), pl.BlockSpec((1, LANES), lambda i: (0, 0)),
                  pl.BlockSpec((LANES, LANES), lambda i: (0, 0))],
        out_specs=[pl.BlockSpec((8, tm), lambda i: (0, i)), pl.BlockSpec((8, LANES), lambda i: (0, 0))],
        out_shape=[jax.ShapeDtypeStruct((8, n_tok), jnp.int32), jax.ShapeDtypeStruct((8, LANES), jnp.int32)],
        compiler_params=_cparams("arbitrary"),
        name="moe_plan",
    )(route, counts, incl)


def _router_weights(w_group, b_group, w_expert, b_expert):
    w_r = _pad_cols(jnp.concatenate([w_group, w_expert], axis=1))
    w_hi = w_r.astype(BF16)
    w_lo = (w_r - w_hi.astype(F32)).astype(BF16)
    return w_hi, w_lo, _pad_lanes(jnp.concatenate([b_group, b_expert]))


def _moe_layer(h, xn, route, counts, w_gate, w_up, w_down, layer, final_g, next_weights):
    n_tok, d = h.shape
    n_blocks = -(-(2 * n_tok + MOE_EXPERTS * (MOE_ROWS - 1)) // MOE_ROWS)
    dest, blocks = moe_plan(route, counts)
    xs = sc_scatter_rows(xn, dest, n_blocks * MOE_ROWS)
    if next_weights is not None:
        xs, next_weights = lax.optimization_barrier((xs, next_weights))
    ys = moe_experts(blocks, xs, w_gate, w_up, w_down, layer)
    y01 = sc_gather_rows(ys, dest, 2 * n_tok)
    g = jnp.ones((d,), F32) if final_g is None else final_g
    return moe_combine(h, y01, route, g, final_g is not None), next_weights


def _memory_kv(memn_in, mem_norm, wk, wv):
    bsz, m, d = memn_in.shape
    w = jnp.concatenate([wk, wv], axis=1).astype(BF16)
    kv, _ = rms_matmul(memn_in.reshape(bsz * m, d), mem_norm, w, jnp.zeros((d, LANES), BF16))
    k = kv[:, :d].reshape(bsz, m, d)
    v = kv[:, d:].reshape(bsz, m, d)
    return (jnp.swapaxes(k, 1, 2) * XA_HEAD_DIM ** -0.5).astype(BF16), v.astype(BF16)


def kernel(x, mem, mem_norm, final_norm, norm_mix, norm_xa, norm_ffn, xa_wq, xa_wk, xa_wv, xa_wo, moe_w_group, moe_b_group, moe_w_expert, moe_b_expert, moe_w_gate, moe_w_up, moe_w_down, ev_w_in, ev_sc_conv, ev_ssm_conv_w, ev_ssm_conv_b, ev_ssm_dt_bias, ev_ssm_a_log, ev_ssm_d, ev_ssm_norm, ev_w_out, od_w_in, od_gdn_conv, od_gdn_dt_bias, od_gdn_a_log, od_gdn_norm, od_w_out):
    bsz, length, d = x.shape
    n_tok = bsz * length
    depth = norm_mix.shape[0]
    h = x.reshape(n_tok, d)
    qkv_w = 3 * GDN_HEADS * GDN_D
    z_end = qkv_w + GDN_HEADS * GDN_D

    def in_proj_weights(layer):
        if layer % 2 == 0:
            return (jnp.swapaxes(ev_w_in[layer // 2], 0, 1).astype(BF16),)
        wt = jnp.swapaxes(od_w_in[layer // 2], 0, 1).astype(BF16)
        w_main = jnp.concatenate([wt[qkv_w:z_end], wt[z_end + 2 * GDN_HEADS:]], axis=0)
        return wt, w_main, _pad_rows(wt[z_end:z_end + 2 * GDN_HEADS])

    w_in = in_proj_weights(0)
    for layer in range(depth):
        i = layer // 2
        if layer % 2 == 0:
            wt, = w_in
            z0 = 3 * SC_DIM
            xbc0 = z0 + SSM_INNER
            w_conv = wt[xbc0:xbc0 + SSM_XBC]
            w_small = _pad_rows(wt[xbc0 + SSM_XBC:])
            xbc = rms_matmul_conv(h, norm_mix[layer], w_conv, ev_ssm_conv_w[i], ev_ssm_conv_b[i], length)
            z, ya, small = rms_matmul_gated(h, norm_mix[layer], wt[z0:xbc0], wt[:z0], w_small, ev_sc_conv[i], length)
            small3 = small.reshape(bsz, length, LANES)
            smallt = jnp.swapaxes(small3[:, :, :16], 1, 2)
            yb = ssd_mixer(xbc.reshape(bsz, length, -1), z.reshape(bsz, length, -1), small3, smallt,
                           ev_ssm_dt_bias[i], ev_ssm_a_log[i], ev_ssm_d[i], ev_ssm_norm[i])
            w_out = ev_w_out[i].astype(BF16)
            split = SC_DIM
        else:
            wt, w_main, w_small = w_in
            qkv = rms_matmul_conv(h, norm_mix[layer], wt, od_gdn_conv[i], jnp.zeros((qkv_w,), F32), length)
            y, small = rms_matmul(h, norm_mix[layer], w_main, w_small, tm=1024, tn=w_main.shape[0], transposed=True)
            y3 = y.reshape(bsz, length, -1)
            small3 = small.reshape(bsz, length, LANES)
            smallt = jnp.swapaxes(small3[:, :, :16], 1, 2)
            ya = gated_deltanet_mixer(qkv.reshape(bsz, length, -1), y3, small3, smallt, od_gdn_dt_bias[i],
                                      od_gdn_a_log[i], od_gdn_norm[i])
            yb = stick_breaking_mixer(y3, GDN_HEADS * GDN_D)
            w_out = od_w_out[i].astype(BF16)
            split = GDN_HEADS * GDN_D
        kt, v = _memory_kv(mem, mem_norm, xa_wk[layer], xa_wv[layer])
        w_hi, w_lo, bias = _router_weights(moe_w_group[layer], moe_b_group[layer], moe_w_expert[layer],
                                           moe_b_expert[layer])
        h, xn, route, counts = post_mixer(
            ya.reshape(n_tok, -1), yb.reshape(n_tok, -1), h, w_out[:split], w_out[split:], norm_xa[layer],
            xa_wq[layer].astype(BF16), kt, v, xa_wo[layer].astype(BF16), norm_ffn[layer], w_hi, w_lo, bias)
        last = layer == depth - 1
        h, w_in = _moe_layer(h, xn, route, counts, moe_w_gate, moe_w_up, moe_w_down, layer,
                             final_norm if last else None, None if last else in_proj_weights(layer + 1))
    return h.reshape(bsz, length, d)
```

```python
import functools

import jax
import jax.numpy as jnp
import numpy as np
from jax import lax
from jax.experimental import pallas as pl
from jax.experimental.pallas import tpu as pltpu
from jax.experimental.pallas import tpu_sc as plsc

F32 = jnp.float32
BF16 = jnp.bfloat16
EPS = 1e-6

SC_DIM = 512
SSM_HEADS = 16
SSM_HEAD_DIM = 64
SSM_INNER = 1024
SSM_GROUPS = 2
SSM_STATE = 128
SSM_XBC = SSM_INNER + 2 * SSM_GROUPS * SSM_STATE
SSD_CHUNK = 128
SSD_STEP_ROWS = 512
GDN_HEADS = 8
GDN_D = 128
GDN_CHUNK = 64
GDN_TILE = 128
GDN_STEP_ROWS = 512
SB_HEAD_DIM = 64
SB_DIM = 512
SB_BLOCK = 128
SB_STEP_HEADS = 8
SB_STEP_ROWS = 512
SB_EAGER_BLOCKS = 2
XA_HEADS = 4
XA_HEAD_DIM = 256
MOE_GROUPS = 4
MOE_PER_GROUP = 8
MOE_EXPERTS = 32
MOE_ROWS = 512
POST_GROUP_ROWS = 512
SC_CORES = 2
SC_SUBCORES = 16
SC_WORKERS = SC_CORES * SC_SUBCORES
SC_CHUNK = 64
HALO = 8
CONV_CHUNK = 512
LANES = 128
SB_LOG_ZERO = -104.0
VMEM_LIMIT = 56 * 1024 * 1024


def _cparams(*sem):
    return pltpu.CompilerParams(dimension_semantics=sem, vmem_limit_bytes=VMEM_LIMIT)


def _mm(a, b):
    return jnp.dot(a.astype(BF16), b.astype(BF16), preferred_element_type=F32)


def _mm_nt(a, b):
    return lax.dot_general(a.astype(BF16), b.astype(BF16), (((1,), (1,)), ((), ())),
                           preferred_element_type=F32)


def _split_bf16(x, n):
    parts, r = [], x
    for _ in range(n):
        p = r.astype(BF16)
        parts.append(p)
        r = r - p.astype(F32)
    return parts


def _mm_sel_rhs(x, sel, n=3):
    return sum(jnp.dot(p, sel, preferred_element_type=F32) for p in _split_bf16(x, n))


def _mm_sel_lhs(sel, x, n=3):
    return sum(jnp.dot(sel, p, preferred_element_type=F32) for p in _split_bf16(x, n))


def _spread_heads(x, first, n_heads, width):
    rows = x.shape[0]
    col = lambda h: jnp.broadcast_to(x[:, first + h:first + h + 1], (rows, LANES))
    if width == LANES:
        return jnp.concatenate([col(h) for h in range(n_heads)], axis=1)
    left = lax.broadcasted_iota(jnp.int32, (rows, LANES), 1) < width
    return jnp.concatenate([jnp.where(left, col(h), col(h + 1)) for h in range(0, n_heads, 2)], axis=1)


def _pack_halves(x):
    n = x.shape[1] // 2
    lo = pltpu.bitcast(x[:, :n].astype(BF16).astype(F32), jnp.int32)
    hi = pltpu.bitcast(x[:, n:].astype(BF16).astype(F32), jnp.int32)
    return lax.shift_right_logical(lo, 16) | (hi & jnp.int32(-65536))


def _unpack_halves(p):
    lo = pltpu.bitcast(lax.shift_left(p, 16), F32)
    hi = pltpu.bitcast(p & jnp.int32(-65536), F32)
    return jnp.concatenate([lo, hi], axis=1)


def _silu(x):
    half = 0.5 * x
    return half * jnp.tanh(half) + half


def _softplus(x):
    return jnp.maximum(x, 0.0) + jnp.log(1.0 + jnp.exp(-jnp.abs(x)))


def _rms(x, g):
    return x * lax.rsqrt(jnp.mean(x * x, axis=-1, keepdims=True) + EPS) * g


def _rms_matmul_kernel(x_ref, g_ref, w_ref, ws_ref, o_ref, os_ref, *, transposed):
    mm = _mm_nt if transposed else _mm
    xn = _rms(x_ref[...], g_ref[...]).astype(BF16)
    o_ref[...] = mm(xn, w_ref[...])
    os_ref[...] = mm(xn, ws_ref[...])


def rms_matmul(x, g, w, ws, tm=512, tn=512, transposed=False):
    m, k = x.shape
    n = w.shape[0] if transposed else w.shape[1]
    tm = min(tm, m)
    w_specs = ([pl.BlockSpec((tn, k), lambda j, i: (j, 0)), pl.BlockSpec((LANES, k), lambda j, i: (0, 0))]
               if transposed else
               [pl.BlockSpec((k, tn), lambda j, i: (0, j)), pl.BlockSpec((k, LANES), lambda j, i: (0, 0))])
    main, small = pl.pallas_call(
        functools.partial(_rms_matmul_kernel, transposed=transposed),
        grid=(n // tn, m // tm),
        in_specs=[
            pl.BlockSpec((tm, k), lambda j, i: (i, 0)),
            pl.BlockSpec((1, k), lambda j, i: (0, 0)),
            *w_specs,
        ],
        out_specs=[
            pl.BlockSpec((tm, tn), lambda j, i: (i, j)),
            pl.BlockSpec((None, tm, LANES), lambda j, i: (j, i, 0)),
        ],
        out_shape=[jax.ShapeDtypeStruct((m, n), F32), jax.ShapeDtypeStruct((n // tn, m, LANES), F32)],
        compiler_params=_cparams("parallel", "parallel"),
        name="rms_matmul",
    )(x, g.reshape(1, k), w, ws)
    return main, small[0]


def _causal_conv(ext_ref, w_ref, rows):
    width = w_ref.shape[0]
    ext = ext_ref[...]
    acc = None
    for j in range(width):
        shift = width - 1 - j
        moved = ext if shift == 0 else pltpu.roll(ext, shift, axis=0)
        term = w_ref[j:j + 1, :] * moved[HALO:HALO + rows, :]
        acc = term if acc is None else acc + term
    return acc


def _rms_matmul_conv_kernel(x_ref, g_ref, w_ref, cw_ref, cb_ref, o_ref, *ext_refs, tiles_per_seq):
    tm = x_ref.shape[0]
    starts_sequence = pl.program_id(1) % tiles_per_seq == 0

    @pl.when(starts_sequence)
    def _():
        for ext_ref in ext_refs:
            ext_ref[0:HALO, :] = jnp.zeros((HALO, CONV_CHUNK), F32)

    @pl.when(jnp.logical_not(starts_sequence))
    def _():
        for ext_ref in ext_refs:
            ext_ref[0:HALO, :] = ext_ref[tm:tm + HALO, :]

    xn = _rms(x_ref[...], g_ref[...]).astype(BF16)
    for c, ext_ref in enumerate(ext_refs):
        cols = slice(c * CONV_CHUNK, (c + 1) * CONV_CHUNK)
        ext_ref[HALO:, :] = jnp.dot(xn, w_ref[:, cols], preferred_element_type=F32)
        o_ref[:, cols] = _causal_conv(ext_ref, cw_ref.at[:, cols], tm) + cb_ref[:, cols]


def rms_matmul_conv(x, g, w, conv_w, conv_b, seq_len, tm=1024, tn=1536):
    m, k = x.shape
    n = w.shape[1]
    cols = lambda rows: pl.BlockSpec((rows, tn), lambda j, i: (0, j))
    return pl.pallas_call(
        functools.partial(_rms_matmul_conv_kernel, tiles_per_seq=seq_len // tm),
        grid=(n // tn, m // tm),
        in_specs=[
            pl.BlockSpec((tm, k), lambda j, i: (i, 0)),
            pl.BlockSpec((1, k), lambda j, i: (0, 0)),
            cols(k), cols(conv_w.shape[0]), cols(1),
        ],
        out_specs=pl.BlockSpec((tm, tn), lambda j, i: (i, j)),
        out_shape=jax.ShapeDtypeStruct((m, n), F32),
        scratch_shapes=[pltpu.VMEM((tm + HALO, CONV_CHUNK), F32)] * (tn // CONV_CHUNK),
        compiler_params=_cparams("arbitrary", "arbitrary"),
        name="rms_matmul_conv",
    )(x, g.reshape(1, k), w, conv_w, conv_b.reshape(1, n))


def _rms_matmul_gated_kernel(x_ref, g_ref, wz_ref, wbcx_ref, ws_ref, cw_ref, z_ref, ya_ref, os_ref, ext_ref,
                             *, tiles_per_seq):
    tm = x_ref.shape[0]
    starts_sequence = pl.program_id(0) % tiles_per_seq == 0

    @pl.when(starts_sequence)
    def _():
        ext_ref[0:HALO, :] = jnp.zeros((HALO, SC_DIM), F32)

    @pl.when(jnp.logical_not(starts_sequence))
    def _():
        ext_ref[0:HALO, :] = ext_ref[tm:tm + HALO, :]

    xn = _rms(x_ref[...], g_ref[...]).astype(BF16)
    z_ref[...] = _mm_nt(xn, wz_ref[...])
    os_ref[...] = _mm_nt(xn, ws_ref[...])
    bcx = _mm_nt(xn, wbcx_ref[...])
    ext_ref[HALO:, :] = bcx[:, SC_DIM:2 * SC_DIM] * bcx[:, 2 * SC_DIM:]
    ya_ref[...] = (bcx[:, :SC_DIM] * _causal_conv(ext_ref, cw_ref, tm)).astype(ya_ref.dtype)


def rms_matmul_gated(x, g, w_z, w_bcx, w_small, conv_w, seq_len, tm=1024):
    m, k = x.shape
    const = lambda a: pl.BlockSpec(a.shape, lambda i: (0,) * a.ndim)
    rows = lambda w: pl.BlockSpec((tm, w), lambda i: (i, 0))
    g = g.reshape(1, k)
    return pl.pallas_call(
        functools.partial(_rms_matmul_gated_kernel, tiles_per_seq=seq_len // tm),
        grid=(m // tm,),
        in_specs=[rows(k), const(g), const(w_z), const(w_bcx), const(w_small), const(conv_w)],
        out_specs=[rows(w_z.shape[0]), rows(SC_DIM), rows(LANES)],
        out_shape=[jax.ShapeDtypeStruct((m, w_z.shape[0]), F32), jax.ShapeDtypeStruct((m, SC_DIM), BF16),
                   jax.ShapeDtypeStruct((m, LANES), F32)],
        scratch_shapes=[pltpu.VMEM((tm + HALO, SC_DIM), F32)],
        compiler_params=_cparams("arbitrary"),
        name="rms_matmul_gated",
    )(x, g, w_z, w_bcx, w_small, conv_w)


def _ssd_kernel(xbc_ref, z_ref, dt_ref, dtt_ref, dtb_r_ref, dtb_c_ref,
                alog_r_ref, alog_c_ref, d_ref, nw_ref, tri_ref, trit_ref,
                o_ref, s_ref):
    q = SSD_CHUNK

    @pl.when(pl.program_id(1) == 0)
    def _():
        s_ref[...] = jnp.zeros_like(s_ref)

    for sub in range(xbc_ref.shape[0] // q):
        rows = slice(sub * q, (sub + 1) * q)
        _ssd_chunk(_silu(xbc_ref[rows, :]), z_ref[rows, :], dt_ref[rows, :], dtt_ref[:, rows], dtb_r_ref, dtb_c_ref,
                   alog_r_ref, alog_c_ref, d_ref, nw_ref, tri_ref, trit_ref, o_ref.at[rows, :], s_ref)


def _ssd_chunk(xbc, z, dt_raw, dtt_raw, dtb_r_ref, dtb_c_ref, alog_r_ref, alog_c_ref, d_ref, nw_ref, tri_ref,
               trit_ref, o_ref, s_ref):
    q = SSD_CHUNK
    hpg = SSM_HEADS // SSM_GROUPS
    gw = hpg * SSM_HEAD_DIM
    xs = xbc[:, :SSM_INNER]
    bm = xbc[:, SSM_INNER:SSM_INNER + SSM_GROUPS * SSM_STATE]
    cm = xbc[:, SSM_INNER + SSM_GROUPS * SSM_STATE:]

    dt = _softplus(dt_raw + dtb_r_ref[...])
    acs = _mm_sel_lhs(tri_ref[...], dt * -jnp.exp(alog_r_ref[...]))
    dtt = _softplus(dtt_raw + dtb_c_ref[...])
    acst = _mm_sel_rhs(dtt * -jnp.exp(alog_c_ref[...]), trit_ref[...])
    dt_full = _spread_heads(dt, 0, SSM_HEADS, SSM_HEAD_DIM)
    acs_full = _spread_heads(acs, 0, SSM_HEADS, SSM_HEAD_DIM)
    acs_col = _spread_heads(acs, 0, SSM_HEADS, q)

    xdt = xs * dt_full
    acs_last = acs_full[q - 1:q, :]
    xw = xdt * jnp.exp(acs_last - acs_full)
    chunk_decay = jnp.exp(acs_last)

    row = lax.broadcasted_iota(jnp.int32, (q, q), 0)
    col = lax.broadcasted_iota(jnp.int32, (q, q), 1)
    causal = row >= col
    lane = lax.broadcasted_iota(jnp.int32, (q, 2 * SSM_HEAD_DIM), 1)

    y_diag, y_off = [], []
    for g in range(SSM_GROUPS):
        bm_g = bm[:, g * SSM_STATE:(g + 1) * SSM_STATE]
        cm_g = cm[:, g * SSM_STATE:(g + 1) * SSM_STATE]
        cb_g = _mm_nt(cm_g, bm_g)
        state = s_ref[g]
        y_off.append(_mm(cm_g, state))
        s_ref[g] = state * chunk_decay[:, g * gw:(g + 1) * gw] + _mm(bm_g.T, xw[:, g * gw:(g + 1) * gw])
        for pair in range(hpg // 2):
            h0 = g * hpg + 2 * pair
            xdt_pair = xdt[:, h0 * SSM_HEAD_DIM:(h0 + 2) * SSM_HEAD_DIM]
            weights = []
            for h in (h0, h0 + 1):
                seg = acs_col[:, h * q:(h + 1) * q] - acst[h:h + 1, :]
                weights.append(cb_g * jnp.where(causal, jnp.exp(seg), 0.0))
            both = _mm(jnp.concatenate(weights, axis=0), xdt_pair)
            y_diag.append(jnp.where(lane < SSM_HEAD_DIM, both[:q], both[q:]))
    y = (jnp.concatenate(y_diag, axis=1) + jnp.concatenate(y_off, axis=1) * jnp.exp(acs_full)
         + xs * d_ref[...])
    y = y * _silu(z)
    halves = []
    for g in range(SSM_GROUPS):
        yg = y[:, g * gw:(g + 1) * gw]
        halves.append(yg * lax.rsqrt(jnp.mean(yg * yg, axis=-1, keepdims=True) + EPS))
    o_ref[...] = (jnp.concatenate(halves, axis=1) * nw_ref[...]).astype(o_ref.dtype)


def _pad_lanes(v, fill=0.0):
    return jnp.pad(v.astype(F32), (0, LANES - v.shape[0]), constant_values=fill).reshape(1, LANES)


def _pad_col(v, rows=16):
    return jnp.pad(v.astype(F32), (0, rows - v.shape[0])).reshape(rows, 1)


def ssd_mixer(xbc3, y3, small3, smallt, dt_bias, a_log, d_skip, norm_w):
    bsz, length, _ = y3.shape
    q = SSD_CHUNK
    tri = jnp.asarray(np.tril(np.ones((q, q), np.float32)), BF16)
    trit = jnp.asarray(np.triu(np.ones((q, q), np.float32)), BF16)
    d_full = jnp.repeat(d_skip.astype(F32), SSM_HEAD_DIM).reshape(1, SSM_INNER)
    const = lambda a: pl.BlockSpec(a.shape, lambda b, c: (0,) * a.ndim)
    args = [_pad_lanes(dt_bias), _pad_col(dt_bias), _pad_lanes(a_log),
            _pad_col(a_log), d_full, norm_w.reshape(1, -1), tri, trit]
    rows = min(SSD_STEP_ROWS, length)
    return pl.pallas_call(
        _ssd_kernel,
        grid=(bsz, length // rows),
        in_specs=[
            pl.BlockSpec((None, rows, SSM_XBC), lambda b, c: (b, c, 0)),
            pl.BlockSpec((None, rows, SSM_INNER), lambda b, c: (b, c, 0)),
            pl.BlockSpec((None, rows, LANES), lambda b, c: (b, c, 0)),
            pl.BlockSpec((None, 16, rows), lambda b, c: (b, 0, c)),
        ] + [const(a) for a in args],
        out_specs=pl.BlockSpec((None, rows, SSM_INNER), lambda b, c: (b, c, 0)),
        out_shape=jax.ShapeDtypeStruct((bsz, length, SSM_INNER), BF16),
        scratch_shapes=[pltpu.VMEM((SSM_GROUPS, SSM_STATE, SSM_INNER // SSM_GROUPS), F32)],
        compiler_params=_cparams("parallel", "arbitrary"),
        name="ssd_mixer",
    )(xbc3, y3, small3, smallt, *args)


def _unit_lower_inverse(mats, row, col):
    eye = jnp.where(row == col, 1.0, 0.0)
    blk = lambda n: (row >> (n.bit_length() - 1)) == (col >> (n.bit_length() - 1))
    size = row.shape[0]
    p = [jnp.where(blk(16), -a, 0.0) for a in mats]
    t = [eye + x for x in p]
    p = [_mm(x, x) for x in p]
    for _ in range(2):
        both = [_mm(jnp.concatenate([x, y], axis=0), x) for x, y in zip(p, t)]
        p = [b[:size] for b in both]
        t = [y + b[size:] for y, b in zip(t, both)]
    t = [y + _mm(y, x) for y, x in zip(t, p)]
    for n in (16, 32):
        band = blk(2 * n) & jnp.logical_not(blk(n))
        left = [_mm(y, jnp.where(band, a, 0.0)) for y, a in zip(t, mats)]
        t = [y - _mm(x, y) for y, x in zip(t, left)]
    return t


def _gdn_kernel(qkv_ref, z_ref, ab_ref, abt_ref, dtb_r_ref, dtb_c_ref, alog_r_ref,
                alog_c_ref, nw_ref, tri_ref, trit_ref, o_ref, s_ref):
    n = GDN_TILE

    @pl.when(pl.program_id(1) == 0)
    def _():
        s_ref[...] = jnp.zeros_like(s_ref)

    for sub in range(qkv_ref.shape[0] // n):
        rows = slice(sub * n, (sub + 1) * n)
        _gdn_tile(_silu(qkv_ref[rows, :]), z_ref[rows, :], ab_ref[rows, :], abt_ref[:, rows], dtb_r_ref, dtb_c_ref,
                  alog_r_ref, alog_c_ref, nw_ref, tri_ref, trit_ref, o_ref.at[rows, :], s_ref)


def _gdn_tile(qkv, z, ab, abt, dtb_r_ref, dtb_c_ref, alog_r_ref, alog_c_ref, nw_ref, tri_ref, trit_ref, o_ref, s_ref):
    n = GDN_TILE
    c = GDN_CHUNK
    d = GDN_D
    hd = GDN_HEADS * d
    g = -jnp.exp(alog_r_ref[...]) * _softplus(ab + dtb_r_ref[...])
    gc_full = _spread_heads(_mm_sel_lhs(tri_ref[...], g), 0, GDN_HEADS, d)
    beta_full = _spread_heads(jax.nn.sigmoid(ab), GDN_HEADS, GDN_HEADS, d)
    gt = -jnp.exp(alog_c_ref[...]) * _softplus(abt + dtb_c_ref[...])
    gct = _mm_sel_rhs(gt, trit_ref[...])

    row = lax.broadcasted_iota(jnp.int32, (n, n), 0)
    col = lax.broadcasted_iota(jnp.int32, (n, n), 1)
    same = (row >> (c.bit_length() - 1)) == (col >> (c.bit_length() - 1))
    incl = same & (row >= col)
    strict = same & (row > col)
    zeros_half = jnp.zeros((c, d), F32)

    heads = range(GDN_HEADS)
    sl = [slice(h * d, (h + 1) * d) for h in heads]
    l2n = lambda x: x * lax.rsqrt(jnp.sum(x * x, axis=-1, keepdims=True) + EPS)
    qn = [l2n(qkv[:, sl[h]]) * (d ** -0.5) for h in heads]
    kn = [l2n(qkv[:, hd + h * d:hd + (h + 1) * d]) for h in heads]
    vh = [qkv[:, 2 * hd + h * d:2 * hd + (h + 1) * d] for h in heads]
    gcol = [gc_full[:, sl[h]] for h in heads]
    beta = [beta_full[:, sl[h]] for h in heads]
    edec = [jnp.exp(gcol[h] - gct[h:h + 1, :]) for h in heads]
    egc = [jnp.exp(x) for x in gcol]
    kb = [kn[h] * beta[h] for h in heads]
    on_k = [_mm_nt(jnp.concatenate([kb[h], qn[h]], axis=0), kn[h]) for h in heads]
    lower = [jnp.where(strict, on_k[h][:n] * edec[h], 0.0) for h in heads]
    aqk = [jnp.where(incl, on_k[h][n:] * edec[h], 0.0) for h in heads]
    tinv = _unit_lower_inverse(lower, row, col)
    sol = [_mm(tinv[h], jnp.concatenate([vh[h] * beta[h], kb[h] * egc[h]], axis=1)) for h in heads]
    qd = [qn[h] * egc[h] for h in heads]
    glast = [(gcol[h][c - 1:c, :], gcol[h][n - 1:n, :]) for h in heads]
    kdt = [(kn[h] * jnp.exp(jnp.concatenate([jnp.broadcast_to(glast[h][0], (c, d)),
                                             jnp.broadcast_to(glast[h][1], (c, d))], axis=0) - gcol[h])).T
           for h in heads]
    s0 = [s_ref[h] for h in heads]
    on_s0 = [_mm(jnp.concatenate([sol[h][:c, d:], qd[h][:c]], axis=0), s0[h]) for h in heads]
    v0 = [sol[h][:c, :d] - on_s0[h][:c] for h in heads]
    s1 = [s0[h] * jnp.exp(glast[h][0]) + _mm(kdt[h], jnp.concatenate([v0[h], zeros_half], axis=0)) for h in heads]
    on_s1 = [_mm(jnp.concatenate([sol[h][c:, d:], qd[h][c:]], axis=0), s1[h]) for h in heads]
    v1 = [sol[h][c:, :d] - on_s1[h][:c] for h in heads]
    for h in heads:
        s_ref[h] = s1[h] * jnp.exp(glast[h][1]) + _mm(kdt[h], jnp.concatenate([zeros_half, v1[h]], axis=0))
    outs = []
    for h in heads:
        o = (jnp.concatenate([on_s0[h][c:], on_s1[h][c:]], axis=0)
             + _mm(aqk[h], jnp.concatenate([v0[h], v1[h]], axis=0)))
        o = o * lax.rsqrt(jnp.mean(o * o, axis=-1, keepdims=True) + EPS) * nw_ref[...]
        outs.append(o * _silu(z[:, sl[h]]))
    o_ref[...] = jnp.concatenate(outs, axis=1).astype(o_ref.dtype)


def gated_deltanet_mixer(qkv3, y3, small3, smallt, dt_bias, a_log, norm_w):
    bsz, length, _ = y3.shape
    n = GDN_TILE
    hd = GDN_HEADS * GDN_D
    idx = np.arange(n)
    same = (idx[:, None] // GDN_CHUNK) == (idx[None, :] // GDN_CHUNK)
    tri = jnp.asarray(same & (idx[:, None] >= idx[None, :]), BF16)
    trit = jnp.asarray(same & (idx[:, None] <= idx[None, :]), BF16)
    const = lambda a: pl.BlockSpec(a.shape, lambda b, c: (0,) * a.ndim)
    args = [_pad_lanes(dt_bias), _pad_col(dt_bias), _pad_lanes(a_log), _pad_col(a_log),
            norm_w.reshape(1, -1), tri, trit]
    rows = min(GDN_STEP_ROWS, length)
    return pl.pallas_call(
        _gdn_kernel,
        grid=(bsz, length // rows),
        in_specs=[
            pl.BlockSpec((None, rows, 3 * hd), lambda b, c: (b, c, 0)),
            pl.BlockSpec((None, rows, hd), lambda b, c: (b, c, 0)),
            pl.BlockSpec((None, rows, LANES), lambda b, c: (b, c, 0)),
            pl.BlockSpec((None, 16, rows), lambda b, c: (b, 0, c)),
        ] + [const(a) for a in args],
        out_specs=pl.BlockSpec((None, rows, hd), lambda b, c: (b, c, 0)),
        out_shape=jax.ShapeDtypeStruct((bsz, length, hd), BF16),
        scratch_shapes=[pltpu.VMEM((GDN_HEADS, GDN_D, GDN_D), F32)],
        compiler_params=_cparams("parallel", "arbitrary"),
        name="gated_deltanet",
    )(qkv3, y3, small3, smallt, *args)


def _sb_kernel(q_ref, k_ref, v_ref, upper_ref, o_ref):
    blk = SB_BLOCK
    n_sub = q_ref.shape[0] // blk
    first = pl.program_id(2) * n_sub
    parts = [_sb_query_block(first + s, q_ref[s * blk:(s + 1) * blk, :], k_ref, v_ref, upper_ref[...])
             for s in range(n_sub)]
    for s, finish in enumerate(parts):
        o_ref[s * blk:(s + 1) * blk, :] = finish().astype(o_ref.dtype)


def _sb_query_block(i, q, k_ref, v_ref, upper):
    blk = SB_BLOCK
    pair_w = 2 * SB_HEAD_DIM
    n_pairs = SB_STEP_HEADS // 2
    q = q * (SB_HEAD_DIM ** -0.5)
    lane = lax.broadcasted_iota(jnp.int32, (blk, pair_w), 1)
    first_head = lane < SB_HEAD_DIM
    qs = []
    for p in range(n_pairs):
        q2 = q[:, p * pair_w:(p + 1) * pair_w]
        qs += [jnp.where(first_head, q2, 0.0).astype(BF16), jnp.where(first_head, 0.0, q2).astype(BF16)]
    row = lax.broadcasted_iota(jnp.int32, (blk, blk), 0)
    col = lax.broadcasted_iota(jnp.int32, (blk, blk), 1)
    earlier = col < row
    heads = range(SB_STEP_HEADS)

    def local_part(kb, diagonal, exists=None):
        start = pl.multiple_of(kb * blk, blk)
        k = k_ref[pl.ds(start, blk), :].astype(BF16)
        v = v_ref[pl.ds(start, blk), :].astype(BF16)
        kp = [k[:, p * pair_w:(p + 1) * pair_w] for p in range(n_pairs)]
        vp = [v[:, p * pair_w:(p + 1) * pair_w] for p in range(n_pairs)]
        logits = [lax.dot_general(qs[h], kp[h // 2], (((1,), (1,)), ((), ())), preferred_element_type=F32)
                  for h in heads]
        keep = earlier if diagonal else None
        if exists is not None:
            keep = exists if keep is None else keep & exists
        log_keep = [-_softplus(x) for x in logits]
        if keep is not None:
            log_keep = [jnp.where(keep, x, 0.0) for x in log_keep]
        inside = [_mm_sel_rhs(x, upper, 2) for x in log_keep]
        totals = [jnp.sum(x, axis=-1, keepdims=True) for x in log_keep]
        return logits, log_keep, inside, totals, vp, keep

    def carried_part(local, accs, sticks):
        logits, log_keep, inside, totals, vp, keep = local
        w = [jnp.exp(logits[h] + log_keep[h] + inside[h] + sticks[h]) for h in heads]
        if keep is not None:
            w = [jnp.where(keep, x, 0.0) for x in w]
        pv = [jnp.dot(w[h].astype(BF16), vp[h // 2], preferred_element_type=F32) for h in heads]
        accs = tuple(accs[p] + jnp.where(first_head, pv[2 * p], pv[2 * p + 1]) for p in range(n_pairs))
        sticks = tuple(sticks[h] + totals[h] for h in heads)
        return accs, sticks

    accs = tuple(jnp.zeros((blk, pair_w), F32) for _ in range(n_pairs))
    sticks = tuple(jnp.zeros((blk, 1), F32) for _ in heads)
    eager = [local_part(i, True)]
    for back in range(1, SB_EAGER_BLOCKS + 1):
        eager.append(local_part(jnp.maximum(i - back, 0), False, exists=(row >= 0) & (i - back >= 0)))
    for local in eager:
        accs, sticks = carried_part(local, accs, sticks)

    def alive(state):
        kb, _, sticks = state
        longest = sticks[0]
        for s in sticks[1:]:
            longest = jnp.maximum(longest, s)
        return (kb >= 0) & (jnp.max(longest) > SB_LOG_ZERO)

    def body(state):
        kb, accs, sticks = state
        accs, sticks = carried_part(local_part(kb, False), accs, sticks)
        return kb - 1, accs, sticks

    def finish():
        _, done, _ = lax.while_loop(alive, body, (i - 1 - SB_EAGER_BLOCKS, accs, sticks))
        return jnp.concatenate(done, axis=1)

    return finish


def stick_breaking_mixer(y3, col0):
    bsz, length, _ = y3.shape
    blk = SB_BLOCK
    step_w = SB_STEP_HEADS * SB_HEAD_DIM
    steps = SB_DIM // step_w
    q0 = col0 // step_w
    idx = np.arange(blk)
    upper = jnp.asarray(idx[:, None] > idx[None, :], BF16)
    resident = lambda off: pl.BlockSpec((None, length, step_w), lambda b, p, i: (b, 0, q0 + off + p),
                                        pipeline_mode=pl.Buffered(1))
    rows = min(SB_STEP_ROWS, length)
    return pl.pallas_call(
        _sb_kernel,
        grid=(bsz, steps, length // rows),
        in_specs=[
            pl.BlockSpec((None, rows, step_w), lambda b, p, i: (b, i, q0 + p)),
            resident(steps),
            resident(2 * steps),
            pl.BlockSpec((blk, blk), lambda b, p, i: (0, 0)),
        ],
        out_specs=pl.BlockSpec((None, rows, step_w), lambda b, p, i: (b, i, p)),
        out_shape=jax.ShapeDtypeStruct((bsz, length, SB_DIM), BF16),
        compiler_params=_cparams("parallel", "parallel", "arbitrary"),
        name="stick_breaking",
    )(y3, y3, y3, upper)


def _mixer_out(a_ref, b_ref, h_ref, wa_ref, wb_ref, rows):
    return h_ref[rows, :] + (jnp.dot(a_ref[rows, :].astype(BF16), wa_ref[...], preferred_element_type=F32)
                             + jnp.dot(b_ref[rows, :].astype(BF16), wb_ref[...], preferred_element_type=F32))


def _cross_attention(h, g_ref, wq_ref, kt_ref, v_ref, wo_ref):
    u = _rms(h, g_ref[...]).astype(BF16)
    q = jnp.dot(u, wq_ref[...], preferred_element_type=F32)
    heads = []
    for hd in range(XA_HEADS):
        sl = slice(hd * XA_HEAD_DIM, (hd + 1) * XA_HEAD_DIM)
        s = jnp.dot(q[:, sl].astype(BF16), kt_ref[sl, :], preferred_element_type=F32)
        p = jnp.exp(s - jnp.max(s, axis=-1, keepdims=True))
        p = p * (1.0 / jnp.sum(p, axis=-1, keepdims=True))
        heads.append(jnp.dot(p.astype(BF16), v_ref[:, sl], preferred_element_type=F32))
    o = jnp.concatenate(heads, axis=1).astype(BF16)
    return h + jnp.dot(o, wo_ref[...], preferred_element_type=F32)


def _route(xn, whi_ref, wlo_ref, b_ref, before_ref, run_ref):
    x_hi = xn.astype(BF16)
    x_lo = (xn - x_hi.astype(F32)).astype(BF16)
    wide = jnp.dot(x_hi, jnp.concatenate([whi_ref[...], wlo_ref[...]], axis=1), preferred_element_type=F32)
    logits = (wide[:, :LANES] + jnp.dot(x_lo, whi_ref[...], preferred_element_type=F32)
              + wide[:, LANES:] + b_ref[...])
    lane = lax.broadcasted_iota(jnp.int32, logits.shape, 1).astype(F32)
    neg = -1e30
    none = float(LANES)

    def top(vals):
        best = jnp.max(vals, axis=-1, keepdims=True)
        where = jnp.min(jnp.where(vals == best, lane, none), axis=-1, keepdims=True)
        return best, where

    gl = jnp.where(lane < MOE_GROUPS, logits, neg)
    gbest, gsel = top(gl)
    gprob = 1.0 / jnp.sum(jnp.exp(gl - gbest), axis=-1, keepdims=True)
    lo = MOE_GROUPS + gsel * MOE_PER_GROUP
    el = jnp.where((lane >= lo) & (lane < lo + MOE_PER_GROUP), logits, neg)
    m1, i1 = top(el)
    m2, i2 = top(jnp.where(lane == i1, neg, el))
    e = jnp.exp(m2 - m1)
    gate1 = gprob / (1.0 + e)
    gate2 = gprob * e / (1.0 + e)

    hot1 = lane == i1
    hot2 = lane == i2
    one1 = jnp.where(hot1, 1.0, 0.0)
    one2 = jnp.where(hot2, 1.0, 0.0)
    prefix = jnp.dot(before_ref[...], jnp.concatenate([one1, one2], axis=1).astype(BF16), preferred_element_type=F32)
    prefix1, prefix2 = prefix[:, :LANES], prefix[:, LANES:]
    total1 = jnp.sum(one1, axis=0, keepdims=True)
    running = run_ref[...]
    rank1 = jnp.sum(jnp.where(hot1, prefix1 + running, 0.0), axis=-1, keepdims=True)
    rank2 = jnp.sum(jnp.where(hot2, prefix2 + (running + total1), 0.0), axis=-1, keepdims=True)
    running = running + total1 + jnp.sum(one2, axis=0, keepdims=True)
    run_ref[...] = running

    fields = (i1 - MOE_GROUPS, i2 - MOE_GROUPS, gate1, gate2, rank1, rank2)
    out = jnp.zeros_like(logits)
    for k, val in enumerate(fields):
        out = jnp.where(lane == k, val, out)
    return out


def _post_mixer_kernel(a_ref, b_ref, h_ref, wa_ref, wb_ref, gxa_ref, wq_ref, kt_ref, v_ref, wo_ref,
                       gffn_ref, whi_ref, wlo_ref, bias_ref, before_ref,
                       h_out_ref, xn_ref, r_ref, cnt_ref, run_ref):
    @pl.when(pl.program_id(0) == 0)
    def _():
        run_ref[...] = jnp.zeros_like(run_ref)

    h = _mixer_out(a_ref, b_ref, h_ref, wa_ref, wb_ref, slice(None))
    h = _cross_attention(h, gxa_ref, wq_ref, kt_ref, v_ref, wo_ref)
    h_out_ref[...] = h
    xn = _rms(h, gffn_ref[...])
    xn_ref[...] = _pack_halves(xn)
    group = before_ref.shape[0]
    for start in range(0, h_ref.shape[0], group):
        rows = slice(start, start + group)
        r_ref[rows, :] = _route(xn[rows, :], whi_ref, wlo_ref, bias_ref, before_ref, run_ref)
    cnt_ref[...] = run_ref[...]


def post_mixer(ya, yb, h, wa, wb, g_xa, wq, kt, v, wo, g_ffn, w_hi, w_lo, bias, tm=1024):
    m, d = h.shape
    tiles_per_batch = m // kt.shape[0] // tm
    idx = np.arange(min(POST_GROUP_ROWS, tm))
    before = jnp.asarray(idx[:, None] > idx[None, :], BF16)
    rows = lambda w: pl.BlockSpec((tm, w), lambda i: (i, 0))
    const = lambda a: pl.BlockSpec(a.shape, lambda i: (0,) * a.ndim, pipeline_mode=pl.Buffered(1))
    per_batch = lambda a: pl.BlockSpec((None,) + a.shape[1:], lambda i: (i // tiles_per_batch, 0, 0))
    g_xa, g_ffn = g_xa.reshape(1, d), g_ffn.reshape(1, d)
    return pl.pallas_call(
        _post_mixer_kernel,
        grid=(m // tm,),
        in_specs=[rows(ya.shape[1]), rows(yb.shape[1]), rows(d), const(wa), const(wb), const(g_xa), const(wq),
                  per_batch(kt), per_batch(v), const(wo), const(g_ffn), const(w_hi), const(w_lo), const(bias),
                  const(before)],
        out_specs=[rows(d), rows(d // 2), rows(LANES), pl.BlockSpec((1, LANES), lambda i: (0, 0))],
        out_shape=[jax.ShapeDtypeStruct((m, d), F32), jax.ShapeDtypeStruct((m, d // 2), jnp.int32),
                   jax.ShapeDtypeStruct((m, LANES), F32), jax.ShapeDtypeStruct((1, LANES), F32)],
        scratch_shapes=[pltpu.VMEM((1, LANES), F32)],
        compiler_params=_cparams("arbitrary"),
        name="post_mixer",
    )(ya, yb, h, wa, wb, g_xa, wq, kt, v, wo, g_ffn, w_hi, w_lo, bias, before)


def _expert_kernel(table_ref, x_ref, wg_hbm, wu_hbm, wd_hbm, o_ref,
                   wg32_ref, wu32_ref, wd32_ref, wgb_ref, wub_ref, wdb_ref, sem_ref, *, layer):
    i = pl.program_id(0)
    beid_ref, valid_ref, first_ref, slot_ref, next_ref = (table_ref.at[k] for k in range(5))
    valid = valid_ref[i]

    def weight_copies(expert, slot):
        return (pltpu.make_async_copy(wg_hbm.at[layer, expert], wg32_ref.at[slot], sem_ref.at[slot, 0]),
                pltpu.make_async_copy(wu_hbm.at[layer, expert], wu32_ref.at[slot], sem_ref.at[slot, 1]),
                pltpu.make_async_copy(wd_hbm.at[layer, expert], wd32_ref.at[slot], sem_ref.at[slot, 2]))

    @pl.when(i == 0)
    def _():
        for copy in weight_copies(beid_ref[0], 0):
            copy.start()

    @pl.when(first_ref[i] == 1)
    def _():
        slot = slot_ref[i]
        for copy in weight_copies(beid_ref[i], slot):
            copy.wait()
        wgb_ref[...] = wg32_ref[slot].astype(BF16)
        wub_ref[...] = wu32_ref[slot].astype(BF16)
        wdb_ref[...] = wd32_ref[slot].astype(BF16)

        @pl.when(next_ref[i] >= 0)
        def _():
            for copy in weight_copies(next_ref[i], 1 - slot):
                copy.start()

    half = MOE_ROWS // 2

    def ffn(n_halves):
        row = lax.broadcasted_iota(jnp.int32, (half, 2 * x_ref.shape[1]), 0)
        xs = [jnp.where(row + k * half < valid, _unpack_halves(x_ref[k * half:(k + 1) * half, :]), 0.0).astype(BF16)
              for k in range(n_halves)]
        gates = [jnp.dot(x, wgb_ref[...], preferred_element_type=F32) for x in xs]
        ups = [jnp.dot(x, wub_ref[...], preferred_element_type=F32) for x in xs]
        acts = [(_silu(g) * u).astype(BF16) for g, u in zip(gates, ups)]
        for k, act in enumerate(acts):
            o_ref[k * half:(k + 1) * half, :] = _pack_halves(jnp.dot(act, wdb_ref[...], preferred_element_type=F32))

    @pl.when(valid > half)
    def _():
        ffn(2)

    @pl.when((valid > 0) & (valid <= half))
    def _():
        ffn(1)
        o_ref[half:, :] = jnp.zeros((half, o_ref.shape[1]), o_ref.dtype)

    @pl.when(valid == 0)
    def _():
        o_ref[...] = jnp.zeros_like(o_ref)


def moe_experts(blocks, xs, w_gate, w_up, w_down, layer):
    n_slots, packed = xs.shape
    d = 2 * packed
    rows = MOE_ROWS
    ff = w_gate.shape[3]
    grid_spec = pltpu.PrefetchScalarGridSpec(
        num_scalar_prefetch=1,
        grid=(n_slots // rows,),
        in_specs=[
            pl.BlockSpec((rows, packed), lambda i, *_: (i, 0)),
            pl.BlockSpec(memory_space=pl.ANY),
            pl.BlockSpec(memory_space=pl.ANY),
            pl.BlockSpec(memory_space=pl.ANY),
        ],
        out_specs=pl.BlockSpec((rows, packed), lambda i, *_: (i, 0)),
        scratch_shapes=[pltpu.VMEM((2, d, ff), F32), pltpu.VMEM((2, d, ff), F32), pltpu.VMEM((2, ff, d), F32),
                        pltpu.VMEM((d, ff), BF16), pltpu.VMEM((d, ff), BF16), pltpu.VMEM((ff, d), BF16),
                        pltpu.SemaphoreType.DMA((2, 3))],
    )
    return pl.pallas_call(
        functools.partial(_expert_kernel, layer=layer),
        grid_spec=grid_spec,
        out_shape=jax.ShapeDtypeStruct((n_slots, packed), jnp.int32),
        compiler_params=_cparams("arbitrary"),
        name="moe_experts",
    )(blocks, xs, w_gate, w_up, w_down)


def _sc_mesh():
    return plsc.VectorSubcoreMesh(core_axis_name="c", subcore_axis_name="s",
                                  num_cores=SC_CORES, num_subcores=SC_SUBCORES)


def _sc_worker():
    return lax.axis_index("s") * SC_CORES + lax.axis_index("c")


def _sc_double_buffered(n_chunks, fetch, drain):
    assert n_chunks % 2 == 0
    start = lambda copies: [c.start() for c in copies]
    wait = lambda copies: [c.wait() for c in copies]
    start(fetch(0, 0))

    @pl.loop(0, n_chunks, step=2)
    def _(j):
        wait(fetch(j, 0))

        @pl.when(j > 0)
        def _():
            wait(drain(j - 1, 1))

        start(fetch(j + 1, 1))
        start(drain(j, 0))
        wait(fetch(j + 1, 1))
        wait(drain(j, 0))

        @pl.when(j + 2 < n_chunks)
        def _():
            start(fetch(j + 2, 0))

        start(drain(j + 1, 1))

    wait(drain(n_chunks - 1, 1))


def sc_scatter_rows(x, dest, n_slots):
    n_tok, d = x.shape
    per_worker = n_tok // SC_WORKERS
    n_chunks = per_worker // SC_CHUNK
    by_worker = dest.reshape(dest.shape[0] * SC_WORKERS, n_chunks, SC_CHUNK)

    @functools.partial(
        pl.kernel, mesh=_sc_mesh(), out_type=jax.ShapeDtypeStruct((n_slots, d), x.dtype),
        scratch_types=[pltpu.VMEM((n_chunks, SC_CHUNK), jnp.int32), pltpu.VMEM((n_chunks, SC_CHUNK), jnp.int32),
                       pltpu.VMEM((2, SC_CHUNK, d), x.dtype), pltpu.SemaphoreType.DMA((2, 3))],
        name="moe_scatter_rows")
    def scatter(x_hbm, dest_hbm, out_hbm, i0_v, i1_v, rows_v, sem):
        wid = _sc_worker()
        pltpu.sync_copy(dest_hbm.at[wid], i0_v)
        pltpu.sync_copy(dest_hbm.at[SC_WORKERS + wid], i1_v)

        def fetch(j, buf):
            start = pl.multiple_of(wid * per_worker + j * SC_CHUNK, SC_CHUNK)
            return [pltpu.make_async_copy(x_hbm.at[pl.ds(start, SC_CHUNK)], rows_v.at[buf], sem.at[buf, 0])]

        def drain(j, buf):
            return [pltpu.make_async_copy(rows_v.at[buf], out_hbm.at[i0_v.at[j]], sem.at[buf, 1]),
                    pltpu.make_async_copy(rows_v.at[buf], out_hbm.at[i1_v.at[j]], sem.at[buf, 2])]

        _sc_double_buffered(n_chunks, fetch, drain)

    return scatter(x, by_worker)


def sc_gather_rows(table, idx, n_out):
    d = table.shape[1]
    per_worker = n_out // SC_WORKERS
    n_chunks = per_worker // SC_CHUNK

    @functools.partial(
        pl.kernel, mesh=_sc_mesh(), out_type=jax.ShapeDtypeStruct((n_out, d), table.dtype),
        scratch_types=[pltpu.VMEM((n_chunks, SC_CHUNK), jnp.int32), pltpu.VMEM((2, SC_CHUNK, d), table.dtype),
                       pltpu.SemaphoreType.DMA((2, 2))],
        name="moe_gather_rows")
    def gather(table_hbm, idx_hbm, out_hbm, idx_v, rows_v, sem):
        wid = _sc_worker()
        pltpu.sync_copy(idx_hbm.at[wid], idx_v)

        def fetch(j, buf):
            return [pltpu.make_async_copy(table_hbm.at[idx_v.at[j]], rows_v.at[buf], sem.at[buf, 0])]

        def drain(j, buf):
            start = pl.multiple_of(wid * per_worker + j * SC_CHUNK, SC_CHUNK)
            return [pltpu.make_async_copy(rows_v.at[buf], out_hbm.at[pl.ds(start, SC_CHUNK)], sem.at[buf, 1])]

        _sc_double_buffered(n_chunks, fetch, drain)

    return gather(table, idx.reshape(-1, n_chunks, SC_CHUNK))


def _combine_kernel(h_ref, y0_ref, y1_ref, r_ref, g_ref, o_ref, *, final_norm):
    route = r_ref[...]
    h = h_ref[...] + (route[:, 2:3] * _unpack_halves(y0_ref[...]) + route[:, 3:4] * _unpack_halves(y1_ref[...]))
    o_ref[...] = _rms(h, g_ref[...]) if final_norm else h


def moe_combine(h, y01, route, g, final_norm, tm=1024):
    m, d = h.shape
    tm = min(tm, m)
    rows = lambda w: pl.BlockSpec((tm, w), lambda i: (i, 0))
    return pl.pallas_call(
        functools.partial(_combine_kernel, final_norm=final_norm),
        grid=(m // tm,),
        in_specs=[rows(d), rows(d // 2), pl.BlockSpec((tm, d // 2), lambda i: (i + m // tm, 0)), rows(LANES),
                  pl.BlockSpec((1, d), lambda i: (0, 0))],
        out_specs=rows(d),
        out_shape=jax.ShapeDtypeStruct((m, d), F32),
        compiler_params=_cparams("parallel"),
        name="moe_combine",
    )(h, y01, y01, route, g.reshape(1, d))


def _pad_cols(w):
    return jnp.pad(w, ((0, 0), (0, LANES - w.shape[1])))


def _pad_rows(w):
    return jnp.pad(w, ((0, LANES - w.shape[0]), (0, 0)))


def _plan_kernel(route_ref, cnt_ref, incl_ref, dest_ref, table_ref):
    f32_sum = lambda x, axis: jnp.sum(x, axis=axis, keepdims=True)
    lane = lax.broadcasted_iota(jnp.int32, (LANES, LANES), 1)
    sub = lax.broadcasted_iota(jnp.int32, (LANES, LANES), 0)
    incl = incl_ref[...]
    is_expert = (lane >= MOE_GROUPS) & (lane < MOE_GROUPS + MOE_EXPERTS)
    shift = MOE_ROWS.bit_length() - 1
    counts = jnp.broadcast_to(cnt_ref[...], (LANES, LANES)).astype(jnp.int32)
    padded = jnp.where(is_expert, ((counts + (MOE_ROWS - 1)) >> shift) << shift, 0)
    pad_end = _mm_sel_rhs(padded.astype(F32), incl)
    pad_start = pad_end - padded.astype(F32)

    route = route_ref[...]
    lane_t = lax.broadcasted_iota(jnp.int32, route.shape, 1)
    lane_f = lane_t.astype(F32)
    start_row = pad_start[0:1, :]
    slots = [f32_sum(jnp.where(lane_f == route[:, k:k + 1] + MOE_GROUPS, start_row, 0.0), 1) + route[:, 4 + k:5 + k]
             for k in range(2)]
    both = jnp.where(lane_t == 0, slots[0], jnp.where(lane_t == 1, slots[1], 0.0))
    dest_ref[...] = both.T[0:8, :].astype(jnp.int32)

    on_sub = lambda rows_equal: rows_equal.T
    expert_sub = (sub >= MOE_GROUPS) & (sub < MOE_GROUPS + MOE_EXPERTS)
    block_start = (lane * MOE_ROWS).astype(F32)
    eid = f32_sum(jnp.where(expert_sub & (on_sub(pad_end) <= block_start), 1.0, 0.0), 0)
    eid = jnp.minimum(eid, float(MOE_EXPERTS - 1))
    filled = on_sub(pad_start + counts.astype(F32))
    own = (sub - MOE_GROUPS).astype(F32) == eid
    valid = jnp.clip(f32_sum(jnp.where(own, filled, 0.0), 0) - block_start[0:1, :], 0.0, float(MOE_ROWS))
    eid_rows = jnp.broadcast_to(eid, (LANES, LANES))
    changed = (lane == 0) | (eid_rows != pltpu.roll(eid_rows, 1, axis=1))
    first = jnp.where((jnp.broadcast_to(valid, (LANES, LANES)) > 0) & changed, 1.0, 0.0)
    ordinal = _mm_sel_rhs(first, incl) - 1.0
    slot = ordinal - 2.0 * jnp.floor(ordinal * 0.5)
    later = (on_sub(first) > 0) & (sub > lane)
    nearest = jnp.min(jnp.where(later, sub, LANES), axis=0, keepdims=True)
    next_eid = f32_sum(jnp.where(sub == nearest, on_sub(eid_rows), 0.0), 0)
    next_eid = jnp.where(nearest < LANES, next_eid, -1.0)
    row8 = lax.broadcasted_iota(jnp.int32, (8, LANES), 0)
    table = jnp.zeros((8, LANES), F32)
    for k, val in enumerate((eid, valid, first[0:1, :], slot[0:1, :], next_eid)):
        table = jnp.where(row8 == k, val, table)
    table_ref[...] = table.astype(jnp.int32)


def moe_plan(route, counts, tm=4096):
    n_tok = route.shape[0]
    tm = min(tm, n_tok)
    idx = np.arange(LANES)
    incl = jnp.asarray(idx[:, None] <= idx[None, :], BF16)
    return pl.pallas_call(
        _plan_kernel,
        grid=(n_tok // tm,),
        in_specs=[pl.BlockSpec((tm, LANES), lambda i: (i, 0)), pl.BlockSpec((1, LANES), lambda i: (0, 0)),
                  pl.BlockSpec((LANES, LANES), lambda i: (0, 0))],
        out_specs=[pl.BlockSpec((8, tm), lambda i: (0, i)), pl.BlockSpec((8, LANES), lambda i: (0, 0))],
        out_shape=[jax.ShapeDtypeStruct((8, n_tok), jnp.int32), jax.ShapeDtypeStruct((8, LANES), jnp.int32)],
        compiler_params=_cparams("arbitrary"),
        name="moe_plan",
    )(route, counts, incl)


def _router_weights(w_group, b_group, w_expert, b_expert):
    w_r = _pad_cols(jnp.concatenate([w_group, w_expert], axis=1))
    w_hi = w_r.astype(BF16)
    w_lo = (w_r - w_hi.astype(F32)).astype(BF16)
    return w_hi, w_lo, _pad_lanes(jnp.concatenate([b_group, b_expert]))


def _moe_layer(h, xn, route, counts, w_gate, w_up, w_down, layer, final_g, next_weights):
    n_tok, d = h.shape
    n_blocks = -(-(2 * n_tok + MOE_EXPERTS * (MOE_ROWS - 1)) // MOE_ROWS)
    dest, blocks = moe_plan(route, counts)
    xs = sc_scatter_rows(xn, dest, n_blocks * MOE_ROWS)
    if next_weights is not None:
        xs, next_weights = lax.optimization_barrier((xs, next_weights))
    ys = moe_experts(blocks, xs, w_gate, w_up, w_down, layer)
    y01 = sc_gather_rows(ys, dest, 2 * n_tok)
    g = jnp.ones((d,), F32) if final_g is None else final_g
    return moe_combine(h, y01, route, g, final_g is not None), next_weights


def _memory_kv(memn_in, mem_norm, wk, wv):
    bsz, m, d = memn_in.shape
    w = jnp.concatenate([wk, wv], axis=1).astype(BF16)
    kv, _ = rms_matmul(memn_in.reshape(bsz * m, d), mem_norm, w, jnp.zeros((d, LANES), BF16))
    k = kv[:, :d].reshape(bsz, m, d)
    v = kv[:, d:].reshape(bsz, m, d)
    return (jnp.swapaxes(k, 1, 2) * XA_HEAD_DIM ** -0.5).astype(BF16), v.astype(BF16)


def kernel(x, mem, mem_norm, final_norm, norm_mix, norm_xa, norm_ffn, xa_wq, xa_wk, xa_wv, xa_wo, moe_w_group, moe_b_group, moe_w_expert, moe_b_expert, moe_w_gate, moe_w_up, moe_w_down, ev_w_in, ev_sc_conv, ev_ssm_conv_w, ev_ssm_conv_b, ev_ssm_dt_bias, ev_ssm_a_log, ev_ssm_d, ev_ssm_norm, ev_w_out, od_w_in, od_gdn_conv, od_gdn_dt_bias, od_gdn_a_log, od_gdn_norm, od_w_out):
    bsz, length, d = x.shape
    n_tok = bsz * length
    depth = norm_mix.shape[0]
    h = x.reshape(n_tok, d)
    qkv_w = 3 * GDN_HEADS * GDN_D
    z_end = qkv_w + GDN_HEADS * GDN_D

    def in_proj_weights(layer):
        if layer % 2 == 0:
            wt = jnp.swapaxes(ev_w_in[layer // 2], 0, 1).astype(BF16)
            xbc0 = 3 * SC_DIM + SSM_INNER
            return wt, wt[xbc0:xbc0 + SSM_XBC].T
        wt = jnp.swapaxes(od_w_in[layer // 2], 0, 1).astype(BF16)
        w_main = jnp.concatenate([wt[qkv_w:z_end], wt[z_end + 2 * GDN_HEADS:]], axis=0)
        return wt[:qkv_w].T, w_main, _pad_rows(wt[z_end:z_end + 2 * GDN_HEADS])

    w_in = in_proj_weights(0)
    for layer in range(depth):
        i = layer // 2
        if layer % 2 == 0:
            wt, w_conv = w_in
            z0 = 3 * SC_DIM
            xbc0 = z0 + SSM_INNER
            w_small = _pad_rows(wt[xbc0 + SSM_XBC:])
            xbc = rms_matmul_conv(h, norm_mix[layer], w_conv, ev_ssm_conv_w[i], ev_ssm_conv_b[i], length)
            z, ya, small = rms_matmul_gated(h, norm_mix[layer], wt[z0:xbc0], wt[:z0], w_small, ev_sc_conv[i], length)
            small3 = small.reshape(bsz, length, LANES)
            smallt = jnp.swapaxes(small3[:, :, :16], 1, 2)
            yb = ssd_mixer(xbc.reshape(bsz, length, -1), z.reshape(bsz, length, -1), small3, smallt,
                           ev_ssm_dt_bias[i], ev_ssm_a_log[i], ev_ssm_d[i], ev_ssm_norm[i])
            w_out = ev_w_out[i].astype(BF16)
            split = SC_DIM
        else:
            w_conv, w_main, w_small = w_in
            qkv = rms_matmul_conv(h, norm_mix[layer], w_conv, od_gdn_conv[i], jnp.zeros((qkv_w,), F32), length)
            y, small = rms_matmul(h, norm_mix[layer], w_main, w_small, tm=1024, tn=w_main.shape[0], transposed=True)
            y3 = y.reshape(bsz, length, -1)
            small3 = small.reshape(bsz, length, LANES)
            smallt = jnp.swapaxes(small3[:, :, :16], 1, 2)
            ya = gated_deltanet_mixer(qkv.reshape(bsz, length, -1), y3, small3, smallt, od_gdn_dt_bias[i],
                                      od_gdn_a_log[i], od_gdn_norm[i])
            yb = stick_breaking_mixer(y3, GDN_HEADS * GDN_D)
            w_out = od_w_out[i].astype(BF16)
            split = GDN_HEADS * GDN_D
        kt, v = _memory_kv(mem, mem_norm, xa_wk[layer], xa_wv[layer])
        w_hi, w_lo, bias = _router_weights(moe_w_group[layer], moe_b_group[layer], moe_w_expert[layer],
                                           moe_b_expert[layer])
        h, xn, route, counts = post_mixer(
            ya.reshape(n_tok, -1), yb.reshape(n_tok, -1), h, w_out[:split], w_out[split:], norm_xa[layer],
            xa_wq[layer].astype(BF16), kt, v, xa_wo[layer].astype(BF16), norm_ffn[layer], w_hi, w_lo, bias)
        last = layer == depth - 1
        h, w_in = _moe_layer(h, xn, route, counts, moe_w_gate, moe_w_up, moe_w_down, layer,
                             final_norm if last else None, None if last else in_proj_weights(layer + 1))
    return h.reshape(bsz, length, d)
```

```python
import functools

import jax
import jax.numpy as jnp
import numpy as np
from jax import lax
from jax.experimental import pallas as pl
from jax.experimental.pallas import tpu as pltpu
from jax.experimental.pallas import tpu_sc as plsc

F32 = jnp.float32
BF16 = jnp.bfloat16
EPS = 1e-6

SC_DIM = 512
SSM_HEADS = 16
SSM_HEAD_DIM = 64
SSM_INNER = 1024
SSM_GROUPS = 2
SSM_STATE = 128
SSM_XBC = SSM_INNER + 2 * SSM_GROUPS * SSM_STATE
SSD_CHUNK = 128
SSD_STEP_ROWS = 512
GDN_HEADS = 8
GDN_D = 128
GDN_CHUNK = 64
GDN_TILE = 128
GDN_STEP_ROWS = 512
SB_HEAD_DIM = 64
SB_DIM = 512
SB_BLOCK = 128
SB_STEP_HEADS = 8
SB_STEP_ROWS = 512
SB_EAGER_BLOCKS = 2
XA_HEADS = 4
XA_HEAD_DIM = 256
MOE_GROUPS = 4
MOE_PER_GROUP = 8
MOE_EXPERTS = 32
MOE_ROWS = 512
POST_GROUP_ROWS = 512
SC_CORES = 2
SC_SUBCORES = 16
SC_WORKERS = SC_CORES * SC_SUBCORES
SC_CHUNK = 64
HALO = 8
CONV_CHUNK = 512
LANES = 128
SB_LOG_ZERO = -104.0
VMEM_LIMIT = 56 * 1024 * 1024


def _cparams(*sem):
    return pltpu.CompilerParams(dimension_semantics=sem, vmem_limit_bytes=VMEM_LIMIT)


def _mm(a, b):
    return jnp.dot(a.astype(BF16), b.astype(BF16), preferred_element_type=F32)


def _mm_nt(a, b):
    return lax.dot_general(a.astype(BF16), b.astype(BF16), (((1,), (1,)), ((), ())),
                           preferred_element_type=F32)


def _split_bf16(x, n):
    parts, r = [], x
    for _ in range(n):
        p = r.astype(BF16)
        parts.append(p)
        r = r - p.astype(F32)
    return parts


def _mm_sel_rhs(x, sel, n=3):
    return sum(jnp.dot(p, sel, preferred_element_type=F32) for p in _split_bf16(x, n))


def _mm_sel_lhs(sel, x, n=3):
    return sum(jnp.dot(sel, p, preferred_element_type=F32) for p in _split_bf16(x, n))


def _spread_heads(x, first, n_heads, width):
    rows = x.shape[0]
    col = lambda h: jnp.broadcast_to(x[:, first + h:first + h + 1], (rows, LANES))
    if width == LANES:
        return jnp.concatenate([col(h) for h in range(n_heads)], axis=1)
    left = lax.broadcasted_iota(jnp.int32, (rows, LANES), 1) < width
    return jnp.concatenate([jnp.where(left, col(h), col(h + 1)) for h in range(0, n_heads, 2)], axis=1)


def _pack_halves(x):
    n = x.shape[1] // 2
    lo = pltpu.bitcast(x[:, :n].astype(BF16).astype(F32), jnp.int32)
    hi = pltpu.bitcast(x[:, n:].astype(BF16).astype(F32), jnp.int32)
    return lax.shift_right_logical(lo, 16) | (hi & jnp.int32(-65536))


def _unpack_halves(p):
    lo = pltpu.bitcast(lax.shift_left(p, 16), F32)
    hi = pltpu.bitcast(p & jnp.int32(-65536), F32)
    return jnp.concatenate([lo, hi], axis=1)


def _silu(x):
    half = 0.5 * x
    return half * jnp.tanh(half) + half


def _softplus(x):
    return jnp.maximum(x, 0.0) + jnp.log(1.0 + jnp.exp(-jnp.abs(x)))


def _rms(x, g):
    return x * lax.rsqrt(jnp.mean(x * x, axis=-1, keepdims=True) + EPS) * g


def _rms_matmul_kernel(x_ref, g_ref, w_ref, ws_ref, o_ref, os_ref, *, transposed):
    mm = _mm_nt if transposed else _mm
    xn = _rms(x_ref[...], g_ref[...]).astype(BF16)
    o_ref[...] = mm(xn, w_ref[...])
    os_ref[...] = mm(xn, ws_ref[...])


def rms_matmul(x, g, w, ws, tm=512, tn=512, transposed=False):
    m, k = x.shape
    n = w.shape[0] if transposed else w.shape[1]
    tm = min(tm, m)
    w_specs = ([pl.BlockSpec((tn, k), lambda j, i: (j, 0)), pl.BlockSpec((LANES, k), lambda j, i: (0, 0))]
               if transposed else
               [pl.BlockSpec((k, tn), lambda j, i: (0, j)), pl.BlockSpec((k, LANES), lambda j, i: (0, 0))])
    main, small = pl.pallas_call(
        functools.partial(_rms_matmul_kernel, transposed=transposed),
        grid=(n // tn, m // tm),
        in_specs=[
            pl.BlockSpec((tm, k), lambda j, i: (i, 0)),
            pl.BlockSpec((1, k), lambda j, i: (0, 0)),
            *w_specs,
        ],
        out_specs=[
            pl.BlockSpec((tm, tn), lambda j, i: (i, j)),
            pl.BlockSpec((None, tm, LANES), lambda j, i: (j, i, 0)),
        ],
        out_shape=[jax.ShapeDtypeStruct((m, n), F32), jax.ShapeDtypeStruct((n // tn, m, LANES), F32)],
        compiler_params=_cparams("parallel", "parallel"),
        name="rms_matmul",
    )(x, g.reshape(1, k), w, ws)
    return main, small[0]


def _causal_conv(ext_ref, w_ref, rows):
    width = w_ref.shape[0]
    ext = ext_ref[...]
    acc = None
    for j in range(width):
        shift = width - 1 - j
        moved = ext if shift == 0 else pltpu.roll(ext, shift, axis=0)
        term = w_ref[j:j + 1, :] * moved[HALO:HALO + rows, :]
        acc = term if acc is None else acc + term
    return acc


def _rms_matmul_conv_kernel(x_ref, g_ref, w_ref, cw_ref, cb_ref, o_ref, *ext_refs, tiles_per_seq):
    tm = x_ref.shape[0]
    starts_sequence = pl.program_id(1) % tiles_per_seq == 0

    @pl.when(starts_sequence)
    def _():
        for ext_ref in ext_refs:
            ext_ref[0:HALO, :] = jnp.zeros((HALO, CONV_CHUNK), F32)

    @pl.when(jnp.logical_not(starts_sequence))
    def _():
        for ext_ref in ext_refs:
            ext_ref[0:HALO, :] = ext_ref[tm:tm + HALO, :]

    xn = _rms(x_ref[...], g_ref[...]).astype(BF16)
    for c, ext_ref in enumerate(ext_refs):
        cols = slice(c * CONV_CHUNK, (c + 1) * CONV_CHUNK)
        ext_ref[HALO:, :] = jnp.dot(xn, w_ref[:, cols], preferred_element_type=F32)
        o_ref[:, cols] = _causal_conv(ext_ref, cw_ref.at[:, cols], tm) + cb_ref[:, cols]


def rms_matmul_conv(x, g, w, conv_w, conv_b, seq_len, tm=1024, tn=1536):
    m, k = x.shape
    n = w.shape[1]
    cols = lambda rows: pl.BlockSpec((rows, tn), lambda j, i: (0, j))
    return pl.pallas_call(
        functools.partial(_rms_matmul_conv_kernel, tiles_per_seq=seq_len // tm),
        grid=(n // tn, m // tm),
        in_specs=[
            pl.BlockSpec((tm, k), lambda j, i: (i, 0)),
            pl.BlockSpec((1, k), lambda j, i: (0, 0)),
            cols(k), cols(conv_w.shape[0]), cols(1),
        ],
        out_specs=pl.BlockSpec((tm, tn), lambda j, i: (i, j)),
        out_shape=jax.ShapeDtypeStruct((m, n), F32),
        scratch_shapes=[pltpu.VMEM((tm + HALO, CONV_CHUNK), F32)] * (tn // CONV_CHUNK),
        compiler_params=_cparams("arbitrary", "arbitrary"),
        name="rms_matmul_conv",
    )(x, g.reshape(1, k), w, conv_w, conv_b.reshape(1, n))


def _rms_matmul_gated_kernel(x_ref, g_ref, wz_ref, wbcx_ref, ws_ref, cw_ref, z_ref, ya_ref, os_ref, ext_ref,
                             *, tiles_per_seq):
    tm = x_ref.shape[0]
    starts_sequence = pl.program_id(0) % tiles_per_seq == 0

    @pl.when(starts_sequence)
    def _():
        ext_ref[0:HALO, :] = jnp.zeros((HALO, SC_DIM), F32)

    @pl.when(jnp.logical_not(starts_sequence))
    def _():
        ext_ref[0:HALO, :] = ext_ref[tm:tm + HALO, :]

    xn = _rms(x_ref[...], g_ref[...]).astype(BF16)
    z_ref[...] = _mm_nt(xn, wz_ref[...])
    os_ref[...] = _mm_nt(xn, ws_ref[...])
    bcx = _mm_nt(xn, wbcx_ref[...])
    ext_ref[HALO:, :] = bcx[:, SC_DIM:2 * SC_DIM] * bcx[:, 2 * SC_DIM:]
    ya_ref[...] = (bcx[:, :SC_DIM] * _causal_conv(ext_ref, cw_ref, tm)).astype(ya_ref.dtype)


def rms_matmul_gated(x, g, w_z, w_bcx, w_small, conv_w, seq_len, tm=1024):
    m, k = x.shape
    const = lambda a: pl.BlockSpec(a.shape, lambda i: (0,) * a.ndim)
    rows = lambda w: pl.BlockSpec((tm, w), lambda i: (i, 0))
    g = g.reshape(1, k)
    return pl.pallas_call(
        functools.partial(_rms_matmul_gated_kernel, tiles_per_seq=seq_len // tm),
        grid=(m // tm,),
        in_specs=[rows(k), const(g), const(w_z), const(w_bcx), const(w_small), const(conv_w)],
        out_specs=[rows(w_z.shape[0]), rows(SC_DIM), rows(LANES)],
        out_shape=[jax.ShapeDtypeStruct((m, w_z.shape[0]), F32), jax.ShapeDtypeStruct((m, SC_DIM), BF16),
                   jax.ShapeDtypeStruct((m, LANES), F32)],
        scratch_shapes=[pltpu.VMEM((tm + HALO, SC_DIM), F32)],
        compiler_params=_cparams("arbitrary"),
        name="rms_matmul_gated",
    )(x, g, w_z, w_bcx, w_small, conv_w)


def _ssd_kernel(xbc_ref, z_ref, dt_ref, dtt_ref, dtb_r_ref, dtb_c_ref,
                alog_r_ref, alog_c_ref, d_ref, nw_ref, tri_ref, trit_ref,
                o_ref, s_ref):
    q = SSD_CHUNK

    @pl.when(pl.program_id(1) == 0)
    def _():
        s_ref[...] = jnp.zeros_like(s_ref)

    for sub in range(xbc_ref.shape[0] // q):
        rows = slice(sub * q, (sub + 1) * q)
        _ssd_chunk(_silu(xbc_ref[rows, :]), z_ref[rows, :], dt_ref[rows, :], dtt_ref[:, rows], dtb_r_ref, dtb_c_ref,
                   alog_r_ref, alog_c_ref, d_ref, nw_ref, tri_ref, trit_ref, o_ref.at[rows, :], s_ref)


def _ssd_chunk(xbc, z, dt_raw, dtt_raw, dtb_r_ref, dtb_c_ref, alog_r_ref, alog_c_ref, d_ref, nw_ref, tri_ref,
               trit_ref, o_ref, s_ref):
    q = SSD_CHUNK
    hpg = SSM_HEADS // SSM_GROUPS
    gw = hpg * SSM_HEAD_DIM
    xs = xbc[:, :SSM_INNER]
    bm = xbc[:, SSM_INNER:SSM_INNER + SSM_GROUPS * SSM_STATE]
    cm = xbc[:, SSM_INNER + SSM_GROUPS * SSM_STATE:]

    dt = _softplus(dt_raw + dtb_r_ref[...])
    acs = _mm_sel_lhs(tri_ref[...], dt * -jnp.exp(alog_r_ref[...]))
    dtt = _softplus(dtt_raw + dtb_c_ref[...])
    acst = _mm_sel_rhs(dtt * -jnp.exp(alog_c_ref[...]), trit_ref[...])
    dt_full = _spread_heads(dt, 0, SSM_HEADS, SSM_HEAD_DIM)
    acs_full = _spread_heads(acs, 0, SSM_HEADS, SSM_HEAD_DIM)
    acs_col = _spread_heads(acs, 0, SSM_HEADS, q)

    xdt = xs * dt_full
    acs_last = acs_full[q - 1:q, :]
    xw = xdt * jnp.exp(acs_last - acs_full)
    chunk_decay = jnp.exp(acs_last)

    row = lax.broadcasted_iota(jnp.int32, (q, q), 0)
    col = lax.broadcasted_iota(jnp.int32, (q, q), 1)
    causal = row >= col
    lane = lax.broadcasted_iota(jnp.int32, (q, 2 * SSM_HEAD_DIM), 1)

    y_diag, y_off = [], []
    for g in range(SSM_GROUPS):
        bm_g = bm[:, g * SSM_STATE:(g + 1) * SSM_STATE]
        cm_g = cm[:, g * SSM_STATE:(g + 1) * SSM_STATE]
        cb_g = _mm_nt(cm_g, bm_g)
        state = s_ref[g]
        y_off.append(_mm(cm_g, state))
        s_ref[g] = state * chunk_decay[:, g * gw:(g + 1) * gw] + _mm(bm_g.T, xw[:, g * gw:(g + 1) * gw])
        for pair in range(hpg // 2):
            h0 = g * hpg + 2 * pair
            xdt_pair = xdt[:, h0 * SSM_HEAD_DIM:(h0 + 2) * SSM_HEAD_DIM]
            weights = []
            for h in (h0, h0 + 1):
                seg = acs_col[:, h * q:(h + 1) * q] - acst[h:h + 1, :]
                weights.append(cb_g * jnp.where(causal, jnp.exp(seg), 0.0))
            both = _mm(jnp.concatenate(weights, axis=0), xdt_pair)
            y_diag.append(jnp.where(lane < SSM_HEAD_DIM, both[:q], both[q:]))
    y = (jnp.concatenate(y_diag, axis=1) + jnp.concatenate(y_off, axis=1) * jnp.exp(acs_full)
         + xs * d_ref[...])
    y = y * _silu(z)
    halves = []
    for g in range(SSM_GROUPS):
        yg = y[:, g * gw:(g + 1) * gw]
        halves.append(yg * lax.rsqrt(jnp.mean(yg * yg, axis=-1, keepdims=True) + EPS))
    o_ref[...] = (jnp.concatenate(halves, axis=1) * nw_ref[...]).astype(o_ref.dtype)


def _pad_lanes(v, fill=0.0):
    return jnp.pad(v.astype(F32), (0, LANES - v.shape[0]), constant_values=fill).reshape(1, LANES)


def _pad_col(v, rows=16):
    return jnp.pad(v.astype(F32), (0, rows - v.shape[0])).reshape(rows, 1)


def ssd_mixer(xbc3, y3, small3, smallt, dt_bias, a_log, d_skip, norm_w):
    bsz, length, _ = y3.shape
    q = SSD_CHUNK
    tri = jnp.asarray(np.tril(np.ones((q, q), np.float32)), BF16)
    trit = jnp.asarray(np.triu(np.ones((q, q), np.float32)), BF16)
    d_full = jnp.repeat(d_skip.astype(F32), SSM_HEAD_DIM).reshape(1, SSM_INNER)
    const = lambda a: pl.BlockSpec(a.shape, lambda b, c: (0,) * a.ndim)
    args = [_pad_lanes(dt_bias), _pad_col(dt_bias), _pad_lanes(a_log),
            _pad_col(a_log), d_full, norm_w.reshape(1, -1), tri, trit]
    rows = min(SSD_STEP_ROWS, length)
    return pl.pallas_call(
        _ssd_kernel,
        grid=(bsz, length // rows),
        in_specs=[
            pl.BlockSpec((None, rows, SSM_XBC), lambda b, c: (b, c, 0)),
            pl.BlockSpec((None, rows, SSM_INNER), lambda b, c: (b, c, 0)),
            pl.BlockSpec((None, rows, LANES), lambda b, c: (b, c, 0)),
            pl.BlockSpec((None, 16, rows), lambda b, c: (b, 0, c)),
        ] + [const(a) for a in args],
        out_specs=pl.BlockSpec((None, rows, SSM_INNER), lambda b, c: (b, c, 0)),
        out_shape=jax.ShapeDtypeStruct((bsz, length, SSM_INNER), BF16),
        scratch_shapes=[pltpu.VMEM((SSM_GROUPS, SSM_STATE, SSM_INNER // SSM_GROUPS), F32)],
        compiler_params=_cparams("parallel", "arbitrary"),
        name="ssd_mixer",
    )(xbc3, y3, small3, smallt, *args)


def _unit_lower_inverse(mats, row, col):
    eye = jnp.where(row == col, 1.0, 0.0)
    blk = lambda n: (row >> (n.bit_length() - 1)) == (col >> (n.bit_length() - 1))
    size = row.shape[0]
    p = [jnp.where(blk(16), -a, 0.0) for a in mats]
    t = [eye + x for x in p]
    p = [_mm(x, x) for x in p]
    for _ in range(2):
        both = [_mm(jnp.concatenate([x, y], axis=0), x) for x, y in zip(p, t)]
        p = [b[:size] for b in both]
        t = [y + b[size:] for y, b in zip(t, both)]
    t = [y + _mm(y, x) for y, x in zip(t, p)]
    for n in (16, 32):
        band = blk(2 * n) & jnp.logical_not(blk(n))
        left = [_mm(y, jnp.where(band, a, 0.0)) for y, a in zip(t, mats)]
        t = [y - _mm(x, y) for y, x in zip(t, left)]
    return t


def _gdn_kernel(qkv_ref, z_ref, ab_ref, abt_ref, dtb_r_ref, dtb_c_ref, alog_r_ref,
                alog_c_ref, nw_ref, tri_ref, trit_ref, o_ref, s_ref):
    n = GDN_TILE

    @pl.when(pl.program_id(1) == 0)
    def _():
        s_ref[...] = jnp.zeros_like(s_ref)

    for sub in range(qkv_ref.shape[0] // n):
        rows = slice(sub * n, (sub + 1) * n)
        _gdn_tile(_silu(qkv_ref[rows, :]), z_ref[rows, :], ab_ref[rows, :], abt_ref[:, rows], dtb_r_ref, dtb_c_ref,
                  alog_r_ref, alog_c_ref, nw_ref, tri_ref, trit_ref, o_ref.at[rows, :], s_ref)


def _gdn_tile(qkv, z, ab, abt, dtb_r_ref, dtb_c_ref, alog_r_ref, alog_c_ref, nw_ref, tri_ref, trit_ref, o_ref, s_ref):
    n = GDN_TILE
    c = GDN_CHUNK
    d = GDN_D
    hd = GDN_HEADS * d
    g = -jnp.exp(alog_r_ref[...]) * _softplus(ab + dtb_r_ref[...])
    gc_full = _spread_heads(_mm_sel_lhs(tri_ref[...], g), 0, GDN_HEADS, d)
    beta_full = _spread_heads(jax.nn.sigmoid(ab), GDN_HEADS, GDN_HEADS, d)
    gt = -jnp.exp(alog_c_ref[...]) * _softplus(abt + dtb_c_ref[...])
    gct = _mm_sel_rhs(gt, trit_ref[...])

    row = lax.broadcasted_iota(jnp.int32, (n, n), 0)
    col = lax.broadcasted_iota(jnp.int32, (n, n), 1)
    same = (row >> (c.bit_length() - 1)) == (col >> (c.bit_length() - 1))
    incl = same & (row >= col)
    strict = same & (row > col)
    zeros_half = jnp.zeros((c, d), F32)

    heads = range(GDN_HEADS)
    sl = [slice(h * d, (h + 1) * d) for h in heads]
    l2n = lambda x: x * lax.rsqrt(jnp.sum(x * x, axis=-1, keepdims=True) + EPS)
    qn = [l2n(qkv[:, sl[h]]) * (d ** -0.5) for h in heads]
    kn = [l2n(qkv[:, hd + h * d:hd + (h + 1) * d]) for h in heads]
    vh = [qkv[:, 2 * hd + h * d:2 * hd + (h + 1) * d] for h in heads]
    gcol = [gc_full[:, sl[h]] for h in heads]
    beta = [beta_full[:, sl[h]] for h in heads]
    edec = [jnp.exp(gcol[h] - gct[h:h + 1, :]) for h in heads]
    egc = [jnp.exp(x) for x in gcol]
    kb = [kn[h] * beta[h] for h in heads]
    on_k = [_mm_nt(jnp.concatenate([kb[h], qn[h]], axis=0), kn[h]) for h in heads]
    lower = [jnp.where(strict, on_k[h][:n] * edec[h], 0.0) for h in heads]
    aqk = [jnp.where(incl, on_k[h][n:] * edec[h], 0.0) for h in heads]
    tinv = _unit_lower_inverse(lower, row, col)
    sol = [_mm(tinv[h], jnp.concatenate([vh[h] * beta[h], kb[h] * egc[h]], axis=1)) for h in heads]
    qd = [qn[h] * egc[h] for h in heads]
    glast = [(gcol[h][c - 1:c, :], gcol[h][n - 1:n, :]) for h in heads]
    kdt = [(kn[h] * jnp.exp(jnp.concatenate([jnp.broadcast_to(glast[h][0], (c, d)),
                                             jnp.broadcast_to(glast[h][1], (c, d))], axis=0) - gcol[h])).T
           for h in heads]
    s0 = [s_ref[h] for h in heads]
    on_s0 = [_mm(jnp.concatenate([sol[h][:c, d:], qd[h][:c]], axis=0), s0[h]) for h in heads]
    v0 = [sol[h][:c, :d] - on_s0[h][:c] for h in heads]
    s1 = [s0[h] * jnp.exp(glast[h][0]) + _mm(kdt[h], jnp.concatenate([v0[h], zeros_half], axis=0)) for h in heads]
    on_s1 = [_mm(jnp.concatenate([sol[h][c:, d:], qd[h][c:]], axis=0), s1[h]) for h in heads]
    v1 = [sol[h][c:, :d] - on_s1[h][:c] for h in heads]
    for h in heads:
        s_ref[h] = s1[h] * jnp.exp(glast[h][1]) + _mm(kdt[h], jnp.concatenate([zeros_half, v1[h]], axis=0))
    outs = []
    for h in heads:
        o = (jnp.concatenate([on_s0[h][c:], on_s1[h][c:]], axis=0)
             + _mm(aqk[h], jnp.concatenate([v0[h], v1[h]], axis=0)))
        o = o * lax.rsqrt(jnp.mean(o * o, axis=-1, keepdims=True) + EPS) * nw_ref[...]
        outs.append(o * _silu(z[:, sl[h]]))
    o_ref[...] = jnp.concatenate(outs, axis=1).astype(o_ref.dtype)


def gated_deltanet_mixer(qkv3, y3, small3, smallt, dt_bias, a_log, norm_w):
    bsz, length, _ = y3.shape
    n = GDN_TILE
    hd = GDN_HEADS * GDN_D
    idx = np.arange(n)
    same = (idx[:, None] // GDN_CHUNK) == (idx[None, :] // GDN_CHUNK)
    tri = jnp.asarray(same & (idx[:, None] >= idx[None, :]), BF16)
    trit = jnp.asarray(same & (idx[:, None] <= idx[None, :]), BF16)
    const = lambda a: pl.BlockSpec(a.shape, lambda b, c: (0,) * a.ndim)
    args = [_pad_lanes(dt_bias), _pad_col(dt_bias), _pad_lanes(a_log), _pad_col(a_log),
            norm_w.reshape(1, -1), tri, trit]
    rows = min(GDN_STEP_ROWS, length)
    return pl.pallas_call(
        _gdn_kernel,
        grid=(bsz, length // rows),
        in_specs=[
            pl.BlockSpec((None, rows, 3 * hd), lambda b, c: (b, c, 0)),
            pl.BlockSpec((None, rows, hd), lambda b, c: (b, c, 0)),
            pl.BlockSpec((None, rows, LANES), lambda b, c: (b, c, 0)),
            pl.BlockSpec((None, 16, rows), lambda b, c: (b, 0, c)),
        ] + [const(a) for a in args],
        out_specs=pl.BlockSpec((None, rows, hd), lambda b, c: (b, c, 0)),
        out_shape=jax.ShapeDtypeStruct((bsz, length, hd), BF16),
        scratch_shapes=[pltpu.VMEM((GDN_HEADS, GDN_D, GDN_D), F32)],
        compiler_params=_cparams("parallel", "arbitrary"),
        name="gated_deltanet",
    )(qkv3, y3, small3, smallt, *args)


def _sb_kernel(q_ref, k_ref, v_ref, upper_ref, o_ref):
    blk = SB_BLOCK
    n_sub = q_ref.shape[0] // blk
    first = pl.program_id(2) * n_sub
    parts = [_sb_query_block(first + s, q_ref[s * blk:(s + 1) * blk, :], k_ref, v_ref, upper_ref[...])
             for s in range(n_sub)]
    for s, finish in enumerate(parts):
        o_ref[s * blk:(s + 1) * blk, :] = finish().astype(o_ref.dtype)


def _sb_query_block(i, q, k_ref, v_ref, upper):
    blk = SB_BLOCK
    pair_w = 2 * SB_HEAD_DIM
    n_pairs = SB_STEP_HEADS // 2
    q = q * (SB_HEAD_DIM ** -0.5)
    lane = lax.broadcasted_iota(jnp.int32, (blk, pair_w), 1)
    first_head = lane < SB_HEAD_DIM
    qs = []
    for p in range(n_pairs):
        q2 = q[:, p * pair_w:(p + 1) * pair_w]
        qs += [jnp.where(first_head, q2, 0.0).astype(BF16), jnp.where(first_head, 0.0, q2).astype(BF16)]
    row = lax.broadcasted_iota(jnp.int32, (blk, blk), 0)
    col = lax.broadcasted_iota(jnp.int32, (blk, blk), 1)
    earlier = col < row
    heads = range(SB_STEP_HEADS)

    def local_part(kb, diagonal, exists=None):
        start = pl.multiple_of(kb * blk, blk)
        k = k_ref[pl.ds(start, blk), :].astype(BF16)
        v = v_ref[pl.ds(start, blk), :].astype(BF16)
        kp = [k[:, p * pair_w:(p + 1) * pair_w] for p in range(n_pairs)]
        vp = [v[:, p * pair_w:(p + 1) * pair_w] for p in range(n_pairs)]
        logits = [lax.dot_general(qs[h], kp[h // 2], (((1,), (1,)), ((), ())), preferred_element_type=F32)
                  for h in heads]
        keep = earlier if diagonal else None
        if exists is not None:
            keep = exists if keep is None else keep & exists
        log_keep = [-_softplus(x) for x in logits]
        if keep is not None:
            log_keep = [jnp.where(keep, x, 0.0) for x in log_keep]
        inside = [_mm_sel_rhs(x, upper, 2) for x in log_keep]
        totals = [jnp.sum(x, axis=-1, keepdims=True) for x in log_keep]
        return logits, log_keep, inside, totals, vp, keep

    def carried_part(local, accs, sticks):
        logits, log_keep, inside, totals, vp, keep = local
        w = [jnp.exp(logits[h] + log_keep[h] + inside[h] + sticks[h]) for h in heads]
        if keep is not None:
            w = [jnp.where(keep, x, 0.0) for x in w]
        pv = [jnp.dot(w[h].astype(BF16), vp[h // 2], preferred_element_type=F32) for h in heads]
        accs = tuple(accs[p] + jnp.where(first_head, pv[2 * p], pv[2 * p + 1]) for p in range(n_pairs))
        sticks = tuple(sticks[h] + totals[h] for h in heads)
        return accs, sticks

    accs = tuple(jnp.zeros((blk, pair_w), F32) for _ in range(n_pairs))
    sticks = tuple(jnp.zeros((blk, 1), F32) for _ in heads)
    eager = [local_part(i, True)]
    for back in range(1, SB_EAGER_BLOCKS + 1):
        eager.append(local_part(jnp.maximum(i - back, 0), False, exists=(row >= 0) & (i - back >= 0)))
    for local in eager:
        accs, sticks = carried_part(local, accs, sticks)

    def alive(state):
        kb, _, sticks = state
        longest = sticks[0]
        for s in sticks[1:]:
            longest = jnp.maximum(longest, s)
        return (kb >= 0) & (jnp.max(longest) > SB_LOG_ZERO)

    def body(state):
        kb, accs, sticks = state
        accs, sticks = carried_part(local_part(kb, False), accs, sticks)
        return kb - 1, accs, sticks

    def finish():
        _, done, _ = lax.while_loop(alive, body, (i - 1 - SB_EAGER_BLOCKS, accs, sticks))
        return jnp.concatenate(done, axis=1)

    return finish


def stick_breaking_mixer(y3, col0):
    bsz, length, _ = y3.shape
    blk = SB_BLOCK
    step_w = SB_STEP_HEADS * SB_HEAD_DIM
    steps = SB_DIM // step_w
    q0 = col0 // step_w
    idx = np.arange(blk)
    upper = jnp.asarray(idx[:, None] > idx[None, :], BF16)
    resident = lambda off: pl.BlockSpec((None, length, step_w), lambda b, p, i: (b, 0, q0 + off + p),
                                        pipeline_mode=pl.Buffered(1))
    rows = min(SB_STEP_ROWS, length)
    return pl.pallas_call(
        _sb_kernel,
        grid=(bsz, steps, length // rows),
        in_specs=[
            pl.BlockSpec((None, rows, step_w), lambda b, p, i: (b, i, q0 + p)),
            resident(steps),
            resident(2 * steps),
            pl.BlockSpec((blk, blk), lambda b, p, i: (0, 0)),
        ],
        out_specs=pl.BlockSpec((None, rows, step_w), lambda b, p, i: (b, i, p)),
        out_shape=jax.ShapeDtypeStruct((bsz, length, SB_DIM), BF16),
        compiler_params=_cparams("parallel", "parallel", "arbitrary"),
        name="stick_breaking",
    )(y3, y3, y3, upper)


def _mixer_out(a_ref, b_ref, h_ref, wa_ref, wb_ref, rows):
    return h_ref[rows, :] + (jnp.dot(a_ref[rows, :].astype(BF16), wa_ref[...], preferred_element_type=F32)
                             + jnp.dot(b_ref[rows, :].astype(BF16), wb_ref[...], preferred_element_type=F32))


def _cross_attention(h, g_ref, wq_ref, kt_ref, v_ref, wo_ref):
    u = _rms(h, g_ref[...]).astype(BF16)
    q = jnp.dot(u, wq_ref[...], preferred_element_type=F32)
    heads = []
    for hd in range(XA_HEADS):
        sl = slice(hd * XA_HEAD_DIM, (hd + 1) * XA_HEAD_DIM)
        s = jnp.dot(q[:, sl].astype(BF16), kt_ref[sl, :], preferred_element_type=F32)
        p = jnp.exp(s - jnp.max(s, axis=-1, keepdims=True))
        p = p * (1.0 / jnp.sum(p, axis=-1, keepdims=True))
        heads.append(jnp.dot(p.astype(BF16), v_ref[:, sl], preferred_element_type=F32))
    o = jnp.concatenate(heads, axis=1).astype(BF16)
    return h + jnp.dot(o, wo_ref[...], preferred_element_type=F32)


def _route(xn, whi_ref, wlo_ref, b_ref, before_ref, run_ref):
    x_hi = xn.astype(BF16)
    x_lo = (xn - x_hi.astype(F32)).astype(BF16)
    wide = jnp.dot(x_hi, jnp.concatenate([whi_ref[...], wlo_ref[...]], axis=1), preferred_element_type=F32)
    logits = (wide[:, :LANES] + jnp.dot(x_lo, whi_ref[...], preferred_element_type=F32)
              + wide[:, LANES:] + b_ref[...])
    lane = lax.broadcasted_iota(jnp.int32, logits.shape, 1).astype(F32)
    neg = -1e30
    none = float(LANES)

    def top(vals):
        best = jnp.max(vals, axis=-1, keepdims=True)
        where = jnp.min(jnp.where(vals == best, lane, none), axis=-1, keepdims=True)
        return best, where

    gl = jnp.where(lane < MOE_GROUPS, logits, neg)
    gbest, gsel = top(gl)
    gprob = 1.0 / jnp.sum(jnp.exp(gl - gbest), axis=-1, keepdims=True)
    lo = MOE_GROUPS + gsel * MOE_PER_GROUP
    el = jnp.where((lane >= lo) & (lane < lo + MOE_PER_GROUP), logits, neg)
    m1, i1 = top(el)
    m2, i2 = top(jnp.where(lane == i1, neg, el))
    e = jnp.exp(m2 - m1)
    gate1 = gprob / (1.0 + e)
    gate2 = gprob * e / (1.0 + e)

    hot1 = lane == i1
    hot2 = lane == i2
    one1 = jnp.where(hot1, 1.0, 0.0)
    one2 = jnp.where(hot2, 1.0, 0.0)
    prefix = jnp.dot(before_ref[...], jnp.concatenate([one1, one2], axis=1).astype(BF16), preferred_element_type=F32)
    prefix1, prefix2 = prefix[:, :LANES], prefix[:, LANES:]
    total1 = jnp.sum(one1, axis=0, keepdims=True)
    running = run_ref[...]
    rank1 = jnp.sum(jnp.where(hot1, prefix1 + running, 0.0), axis=-1, keepdims=True)
    rank2 = jnp.sum(jnp.where(hot2, prefix2 + (running + total1), 0.0), axis=-1, keepdims=True)
    running = running + total1 + jnp.sum(one2, axis=0, keepdims=True)
    run_ref[...] = running

    fields = (i1 - MOE_GROUPS, i2 - MOE_GROUPS, gate1, gate2, rank1, rank2)
    out = jnp.zeros_like(logits)
    for k, val in enumerate(fields):
        out = jnp.where(lane == k, val, out)
    return out


def _post_mixer_kernel(a_ref, b_ref, h_ref, wa_ref, wb_ref, gxa_ref, wq_ref, kt_ref, v_ref, wo_ref,
                       gffn_ref, whi_ref, wlo_ref, bias_ref, before_ref,
                       h_out_ref, xn_ref, r_ref, cnt_ref, run_ref):
    @pl.when(pl.program_id(0) == 0)
    def _():
        run_ref[...] = jnp.zeros_like(run_ref)

    h = _mixer_out(a_ref, b_ref, h_ref, wa_ref, wb_ref, slice(None))
    h = _cross_attention(h, gxa_ref, wq_ref, kt_ref, v_ref, wo_ref)
    h_out_ref[...] = h
    xn = _rms(h, gffn_ref[...])
    xn_ref[...] = _pack_halves(xn)
    group = before_ref.shape[0]
    for start in range(0, h_ref.shape[0], group):
        rows = slice(start, start + group)
        r_ref[rows, :] = _route(xn[rows, :], whi_ref, wlo_ref, bias_ref, before_ref, run_ref)
    cnt_ref[...] = run_ref[...]


def post_mixer(ya, yb, h, wa, wb, g_xa, wq, kt, v, wo, g_ffn, w_hi, w_lo, bias, tm=1024):
    m, d = h.shape
    tiles_per_batch = m // kt.shape[0] // tm
    idx = np.arange(min(POST_GROUP_ROWS, tm))
    before = jnp.asarray(idx[:, None] > idx[None, :], BF16)
    rows = lambda w: pl.BlockSpec((tm, w), lambda i: (i, 0))
    const = lambda a: pl.BlockSpec(a.shape, lambda i: (0,) * a.ndim, pipeline_mode=pl.Buffered(1))
    per_batch = lambda a: pl.BlockSpec((None,) + a.shape[1:], lambda i: (i // tiles_per_batch, 0, 0))
    g_xa, g_ffn = g_xa.reshape(1, d), g_ffn.reshape(1, d)
    return pl.pallas_call(
        _post_mixer_kernel,
        grid=(m // tm,),
        in_specs=[rows(ya.shape[1]), rows(yb.shape[1]), rows(d), const(wa), const(wb), const(g_xa), const(wq),
                  per_batch(kt), per_batch(v), const(wo), const(g_ffn), const(w_hi), const(w_lo), const(bias),
                  const(before)],
        out_specs=[rows(d), rows(d // 2), rows(LANES), pl.BlockSpec((1, LANES), lambda i: (0, 0))],
        out_shape=[jax.ShapeDtypeStruct((m, d), F32), jax.ShapeDtypeStruct((m, d // 2), jnp.int32),
                   jax.ShapeDtypeStruct((m, LANES), F32), jax.ShapeDtypeStruct((1, LANES), F32)],
        scratch_shapes=[pltpu.VMEM((1, LANES), F32)],
        compiler_params=_cparams("arbitrary"),
        name="post_mixer",
    )(ya, yb, h, wa, wb, g_xa, wq, kt, v, wo, g_ffn, w_hi, w_lo, bias, before)


def _expert_kernel(table_ref, x_ref, wg_hbm, wu_hbm, wd_hbm, o_ref,
                   wg32_ref, wu32_ref, wd32_ref, wgb_ref, wub_ref, wdb_ref, sem_ref, *, layer):
    i = pl.program_id(0)
    beid_ref, valid_ref, first_ref, slot_ref, next_ref = (table_ref.at[k] for k in range(5))
    valid = valid_ref[i]

    def weight_copies(expert, slot):
        return (pltpu.make_async_copy(wg_hbm.at[layer, expert], wg32_ref.at[slot], sem_ref.at[slot, 0]),
                pltpu.make_async_copy(wu_hbm.at[layer, expert], wu32_ref.at[slot], sem_ref.at[slot, 1]),
                pltpu.make_async_copy(wd_hbm.at[layer, expert], wd32_ref.at[slot], sem_ref.at[slot, 2]))

    @pl.when(i == 0)
    def _():
        for copy in weight_copies(beid_ref[0], 0):
            copy.start()

    @pl.when(first_ref[i] == 1)
    def _():
        slot = slot_ref[i]
        for copy in weight_copies(beid_ref[i], slot):
            copy.wait()
        wgb_ref[...] = wg32_ref[slot].astype(BF16)
        wub_ref[...] = wu32_ref[slot].astype(BF16)
        wdb_ref[...] = wd32_ref[slot].astype(BF16)

        @pl.when(next_ref[i] >= 0)
        def _():
            for copy in weight_copies(next_ref[i], 1 - slot):
                copy.start()

    half = MOE_ROWS // 2

    def ffn(n_halves):
        row = lax.broadcasted_iota(jnp.int32, (half, 2 * x_ref.shape[1]), 0)
        xs = [jnp.where(row + k * half < valid, _unpack_halves(x_ref[k * half:(k + 1) * half, :]), 0.0).astype(BF16)
              for k in range(n_halves)]
        gates = [jnp.dot(x, wgb_ref[...], preferred_element_type=F32) for x in xs]
        ups = [jnp.dot(x, wub_ref[...], preferred_element_type=F32) for x in xs]
        acts = [(_silu(g) * u).astype(BF16) for g, u in zip(gates, ups)]
        for k, act in enumerate(acts):
            o_ref[k * half:(k + 1) * half, :] = _pack_halves(jnp.dot(act, wdb_ref[...], preferred_element_type=F32))

    @pl.when(valid > half)
    def _():
        ffn(2)

    @pl.when((valid > 0) & (valid <= half))
    def _():
        ffn(1)
        o_ref[half:, :] = jnp.zeros((half, o_ref.shape[1]), o_ref.dtype)

    @pl.when(valid == 0)
    def _():
        o_ref[...] = jnp.zeros_like(o_ref)


def moe_experts(blocks, xs, w_gate, w_up, w_down, layer):
    n_slots, packed = xs.shape
    d = 2 * packed
    rows = MOE_ROWS
    ff = w_gate.shape[3]
    grid_spec = pltpu.PrefetchScalarGridSpec(
        num_scalar_prefetch=1,
        grid=(n_slots // rows,),
        in_specs=[
            pl.BlockSpec((rows, packed), lambda i, *_: (i, 0)),
            pl.BlockSpec(memory_space=pl.ANY),
            pl.BlockSpec(memory_space=pl.ANY),
            pl.BlockSpec(memory_space=pl.ANY),
        ],
        out_specs=pl.BlockSpec((rows, packed), lambda i, *_: (i, 0)),
        scratch_shapes=[pltpu.VMEM((2, d, ff), F32), pltpu.VMEM((2, d, ff), F32), pltpu.VMEM((2, ff, d), F32),
                        pltpu.VMEM((d, ff), BF16), pltpu.VMEM((d, ff), BF16), pltpu.VMEM((ff, d), BF16),
                        pltpu.SemaphoreType.DMA((2, 3))],
    )
    return pl.pallas_call(
        functools.partial(_expert_kernel, layer=layer),
        grid_spec=grid_spec,
        out_shape=jax.ShapeDtypeStruct((n_slots, packed), jnp.int32),
        compiler_params=_cparams("arbitrary"),
        name="moe_experts",
    )(blocks, xs, w_gate, w_up, w_down)


def _sc_mesh():
    return plsc.VectorSubcoreMesh(core_axis_name="c", subcore_axis_name="s",
                                  num_cores=SC_CORES, num_subcores=SC_SUBCORES)


def _sc_worker():
    return lax.axis_index("s") * SC_CORES + lax.axis_index("c")


def _sc_double_buffered(n_chunks, fetch, drain):
    assert n_chunks % 2 == 0
    start = lambda copies: [c.start() for c in copies]
    wait = lambda copies: [c.wait() for c in copies]
    start(fetch(0, 0))

    @pl.loop(0, n_chunks, step=2)
    def _(j):
        wait(fetch(j, 0))

        @pl.when(j > 0)
        def _():
            wait(drain(j - 1, 1))

        start(fetch(j + 1, 1))
        start(drain(j, 0))
        wait(fetch(j + 1, 1))
        wait(drain(j, 0))

        @pl.when(j + 2 < n_chunks)
        def _():
            start(fetch(j + 2, 0))

        start(drain(j + 1, 1))

    wait(drain(n_chunks - 1, 1))


def sc_scatter_rows(x, dest, n_slots):
    n_tok, d = x.shape
    per_worker = n_tok // SC_WORKERS
    n_chunks = per_worker // SC_CHUNK
    by_worker = dest.reshape(dest.shape[0] * SC_WORKERS, n_chunks, SC_CHUNK)

    @functools.partial(
        pl.kernel, mesh=_sc_mesh(), out_type=jax.ShapeDtypeStruct((n_slots, d), x.dtype),
        scratch_types=[pltpu.VMEM((n_chunks, SC_CHUNK), jnp.int32), pltpu.VMEM((n_chunks, SC_CHUNK), jnp.int32),
                       pltpu.VMEM((2, SC_CHUNK, d), x.dtype), pltpu.SemaphoreType.DMA((2, 3))],
        name="moe_scatter_rows")
    def scatter(x_hbm, dest_hbm, out_hbm, i0_v, i1_v, rows_v, sem):
        wid = _sc_worker()
        pltpu.sync_copy(dest_hbm.at[wid], i0_v)
        pltpu.sync_copy(dest_hbm.at[SC_WORKERS + wid], i1_v)

        def fetch(j, buf):
            start = pl.multiple_of(wid * per_worker + j * SC_CHUNK, SC_CHUNK)
            return [pltpu.make_async_copy(x_hbm.at[pl.ds(start, SC_CHUNK)], rows_v.at[buf], sem.at[buf, 0])]

        def drain(j, buf):
            return [pltpu.make_async_copy(rows_v.at[buf], out_hbm.at[i0_v.at[j]], sem.at[buf, 1]),
                    pltpu.make_async_copy(rows_v.at[buf], out_hbm.at[i1_v.at[j]], sem.at[buf, 2])]

        _sc_double_buffered(n_chunks, fetch, drain)

    return scatter(x, by_worker)


def sc_gather_rows(table, idx, n_out):
    d = table.shape[1]
    per_worker = n_out // SC_WORKERS
    n_chunks = per_worker // SC_CHUNK

    @functools.partial(
        pl.kernel, mesh=_sc_mesh(), out_type=jax.ShapeDtypeStruct((n_out, d), table.dtype),
        scratch_types=[pltpu.VMEM((n_chunks, SC_CHUNK), jnp.int32), pltpu.VMEM((2, SC_CHUNK, d), table.dtype),
                       pltpu.SemaphoreType.DMA((2, 2))],
        name="moe_gather_rows")
    def gather(table_hbm, idx_hbm, out_hbm, idx_v, rows_v, sem):
        wid = _sc_worker()
        pltpu.sync_copy(idx_hbm.at[wid], idx_v)

        def fetch(j, buf):
            return [pltpu.make_async_copy(table_hbm.at[idx_v.at[j]], rows_v.at[buf], sem.at[buf, 0])]

        def drain(j, buf):
            start = pl.multiple_of(wid * per_worker + j * SC_CHUNK, SC_CHUNK)
            return [pltpu.make_async_copy(rows_v.at[buf], out_hbm.at[pl.ds(start, SC_CHUNK)], sem.at[buf, 1])]

        _sc_double_buffered(n_chunks, fetch, drain)

    return gather(table, idx.reshape(-1, n_chunks, SC_CHUNK))


def _combine_kernel(h_ref, y0_ref, y1_ref, r_ref, g_ref, o_ref, *, final_norm):
    route = r_ref[...]
    h = h_ref[...] + (route[:, 2:3] * _unpack_halves(y0_ref[...]) + route[:, 3:4] * _unpack_halves(y1_ref[...]))
    o_ref[...] = _rms(h, g_ref[...]) if final_norm else h


def moe_combine(h, y01, route, g, final_norm, tm=1024):
    m, d = h.shape
    tm = min(tm, m)
    rows = lambda w: pl.BlockSpec((tm, w), lambda i: (i, 0))
    return pl.pallas_call(
        functools.partial(_combine_kernel, final_norm=final_norm),
        grid=(m // tm,),
        in_specs=[rows(d), rows(d // 2), pl.BlockSpec((tm, d // 2), lambda i: (i + m // tm, 0)), rows(LANES),
                  pl.BlockSpec((1, d), lambda i: (0, 0))],
        out_specs=rows(d),
        out_shape=jax.ShapeDtypeStruct((m, d), F32),
        compiler_params=_cparams("parallel"),
        name="moe_combine",
    )(h, y01, y01, route, g.reshape(1, d))


def _pad_cols(w):
    return jnp.pad(w, ((0, 0), (0, LANES - w.shape[1])))


def _pad_rows(w):
    return jnp.pad(w, ((0, LANES - w.shape[0]), (0, 0)))


def _plan_kernel(route_ref, cnt_ref, incl_ref, dest_ref, table_ref):
    f32_sum = lambda x, axis: jnp.sum(x, axis=axis, keepdims=True)
    lane = lax.broadcasted_iota(jnp.int32, (LANES, LANES), 1)
    sub = lax.broadcasted_iota(jnp.int32, (LANES, LANES), 0)
    incl = incl_ref[...]
    is_expert = (lane >= MOE_GROUPS) & (lane < MOE_GROUPS + MOE_EXPERTS)
    shift = MOE_ROWS.bit_length() - 1
    counts = jnp.broadcast_to(cnt_ref[...], (LANES, LANES)).astype(jnp.int32)
    padded = jnp.where(is_expert, ((counts + (MOE_ROWS - 1)) >> shift) << shift, 0)
    pad_end = _mm_sel_rhs(padded.astype(F32), incl)
    pad_start = pad_end - padded.astype(F32)

    route = route_ref[...]
    lane_t = lax.broadcasted_iota(jnp.int32, route.shape, 1)
    lane_f = lane_t.astype(F32)
    start_row = pad_start[0:1, :]
    slots = [f32_sum(jnp.where(lane_f == route[:, k:k + 1] + MOE_GROUPS, start_row, 0.0), 1) + route[:, 4 + k:5 + k]
             for k in range(2)]
    both = jnp.where(lane_t == 0, slots[0], jnp.where(lane_t == 1, slots[1], 0.0))
    dest_ref[...] = both.T[0:8, :].astype(jnp.int32)

    on_sub = lambda rows_equal: rows_equal.T
    expert_sub = (sub >= MOE_GROUPS) & (sub < MOE_GROUPS + MOE_EXPERTS)
    block_start = (lane * MOE_ROWS).astype(F32)
    eid = f32_sum(jnp.where(expert_sub & (on_sub(pad_end) <= block_start), 1.0, 0.0), 0)
    eid = jnp.minimum(eid, float(MOE_EXPERTS - 1))
    filled = on_sub(pad_start + counts.astype(F32))
    own = (sub - MOE_GROUPS).astype(F32) == eid
    valid = jnp.clip(f32_sum(jnp.where(own, filled, 0.0), 0) - block_start[0:1, :], 0.0, float(MOE_ROWS))
    eid_rows = jnp.broadcast_to(eid, (LANES, LANES))
    changed = (lane == 0) | (eid_rows != pltpu.roll(eid_rows, 1, axis=1))
    first = jnp.where((jnp.broadcast_to(valid, (LANES, LANES)) > 0) & changed, 1.0, 0.0)
    ordinal = _mm_sel_rhs(first, incl) - 1.0
    slot = ordinal - 2.0 * jnp.floor(ordinal * 0.5)
    later = (on_sub(first) > 0) & (sub > lane)
    nearest = jnp.min(jnp.where(later, sub, LANES), axis=0, keepdims=True)
    next_eid = f32_sum(jnp.where(sub == nearest, on_sub(eid_rows), 0.0), 0)
    next_eid = jnp.where(nearest < LANES, next_eid, -1.0)
    row8 = lax.broadcasted_iota(jnp.int32, (8, LANES), 0)
    table = jnp.zeros((8, LANES), F32)
    for k, val in enumerate((eid, valid, first[0:1, :], slot[0:1, :], next_eid)):
        table = jnp.where(row8 == k, val, table)
    table_ref[...] = table.astype(jnp.int32)


def moe_plan(route, counts, tm=4096):
    n_tok = route.shape[0]
    tm = min(tm, n_tok)
    idx = np.arange(LANES)
    incl = jnp.asarray(idx[:, None] <= idx[None, :], BF16)
    return pl.pallas_call(
        _plan_kernel,
        grid=(n_tok // tm,),
        in_specs=[pl.BlockSpec((tm, LANES), lambda i: (i, 0)), pl.BlockSpec((1, LANES), lambda i: (0, 0)),
                  pl.BlockSpec((LANES, LANES), lambda i: (0, 0))],
        out_specs=[pl.BlockSpec((8, tm), lambda i: (0, i)), pl.BlockSpec((8, LANES), lambda i: (0, 0))],
        out_shape=[jax.ShapeDtypeStruct((8, n_tok), jnp.int32), jax.ShapeDtypeStruct((8, LANES), jnp.int32)],
        compiler_params=_cparams("arbitrary"),
        name="moe_plan",
    )(route, counts, incl)


def _router_weights(w_group, b_group, w_expert, b_expert):
    w_r = _pad_cols(jnp.concatenate([w_group, w_expert], axis=1))
    w_hi = w_r.astype(BF16)
    w_lo = (w_r - w_hi.astype(F32)).astype(BF16)
    return w_hi, w_lo, _pad_lanes(jnp.concatenate([b_group, b_expert]))


def _moe_layer(h, xn, route, counts, w_gate, w_up, w_down, layer, final_g, next_weights):
    n_tok, d = h.shape
    n_blocks = -(-(2 * n_tok + MOE_EXPERTS * (MOE_ROWS - 1)) // MOE_ROWS)
    dest, blocks = moe_plan(route, counts)
    xs = sc_scatter_rows(xn, dest, n_blocks * MOE_ROWS)
    if next_weights is not None:
        xs, next_weights = lax.optimization_barrier((xs, next_weights))
    ys = moe_experts(blocks, xs, w_gate, w_up, w_down, layer)
    y01 = sc_gather_rows(ys, dest, 2 * n_tok)
    g = jnp.ones((d,), F32) if final_g is None else final_g
    return moe_combine(h, y01, route, g, final_g is not None), next_weights


def _memory_kv(memn_in, mem_norm, wk, wv):
    bsz, m, d = memn_in.shape
    depth = wk.shape[0]
    w = jnp.concatenate([wl for layer in range(depth) for wl in (wk[layer], wv[layer])], axis=1).astype(BF16)
    kv, _ = rms_matmul(memn_in.reshape(bsz * m, d), mem_norm, w, jnp.zeros((d, LANES), BF16))
    out = []
    for layer in range(depth):
        k = kv[:, 2 * layer * d:(2 * layer + 1) * d].reshape(bsz, m, d)
        v = kv[:, (2 * layer + 1) * d:(2 * layer + 2) * d].reshape(bsz, m, d)
        out.append(((jnp.swapaxes(k, 1, 2) * XA_HEAD_DIM ** -0.5).astype(BF16), v.astype(BF16)))
    return out


def kernel(x, mem, mem_norm, final_norm, norm_mix, norm_xa, norm_ffn, xa_wq, xa_wk, xa_wv, xa_wo, moe_w_group, moe_b_group, moe_w_expert, moe_b_expert, moe_w_gate, moe_w_up, moe_w_down, ev_w_in, ev_sc_conv, ev_ssm_conv_w, ev_ssm_conv_b, ev_ssm_dt_bias, ev_ssm_a_log, ev_ssm_d, ev_ssm_norm, ev_w_out, od_w_in, od_gdn_conv, od_gdn_dt_bias, od_gdn_a_log, od_gdn_norm, od_w_out):
    bsz, length, d = x.shape
    n_tok = bsz * length
    depth = norm_mix.shape[0]
    h = x.reshape(n_tok, d)
    qkv_w = 3 * GDN_HEADS * GDN_D
    z_end = qkv_w + GDN_HEADS * GDN_D

    def in_proj_weights(layer):
        if layer % 2 == 0:
            wt = jnp.swapaxes(ev_w_in[layer // 2], 0, 1).astype(BF16)
            xbc0 = 3 * SC_DIM + SSM_INNER
            return wt, wt[xbc0:xbc0 + SSM_XBC].T
        wt = jnp.swapaxes(od_w_in[layer // 2], 0, 1).astype(BF16)
        w_main = jnp.concatenate([wt[qkv_w:z_end], wt[z_end + 2 * GDN_HEADS:]], axis=0)
        return wt[:qkv_w].T, w_main, _pad_rows(wt[z_end:z_end + 2 * GDN_HEADS])

    w_in = in_proj_weights(0)
    memory_kv = _memory_kv(mem, mem_norm, xa_wk, xa_wv)
    for layer in range(depth):
        i = layer // 2
        if layer % 2 == 0:
            wt, w_conv = w_in
            z0 = 3 * SC_DIM
            xbc0 = z0 + SSM_INNER
            w_small = _pad_rows(wt[xbc0 + SSM_XBC:])
            xbc = rms_matmul_conv(h, norm_mix[layer], w_conv, ev_ssm_conv_w[i], ev_ssm_conv_b[i], length)
            z, ya, small = rms_matmul_gated(h, norm_mix[layer], wt[z0:xbc0], wt[:z0], w_small, ev_sc_conv[i], length)
            small3 = small.reshape(bsz, length, LANES)
            smallt = jnp.swapaxes(small3[:, :, :16], 1, 2)
            yb = ssd_mixer(xbc.reshape(bsz, length, -1), z.reshape(bsz, length, -1), small3, smallt,
                           ev_ssm_dt_bias[i], ev_ssm_a_log[i], ev_ssm_d[i], ev_ssm_norm[i])
            w_out = ev_w_out[i].astype(BF16)
            split = SC_DIM
        else:
            w_conv, w_main, w_small = w_in
            qkv = rms_matmul_conv(h, norm_mix[layer], w_conv, od_gdn_conv[i], jnp.zeros((qkv_w,), F32), length)
            y, small = rms_matmul(h, norm_mix[layer], w_main, w_small, tm=1024, tn=w_main.shape[0], transposed=True)
            y3 = y.reshape(bsz, length, -1)
            small3 = small.reshape(bsz, length, LANES)
            smallt = jnp.swapaxes(small3[:, :, :16], 1, 2)
            ya = gated_deltanet_mixer(qkv.reshape(bsz, length, -1), y3, small3, smallt, od_gdn_dt_bias[i],
                                      od_gdn_a_log[i], od_gdn_norm[i])
            yb = stick_breaking_mixer(y3, GDN_HEADS * GDN_D)
            w_out = od_w_out[i].astype(BF16)
            split = GDN_HEADS * GDN_D
        kt, v = memory_kv[layer]
        w_hi, w_lo, bias = _router_weights(moe_w_group[layer], moe_b_group[layer], moe_w_expert[layer],
                                           moe_b_expert[layer])
        h, xn, route, counts = post_mixer(
            ya.reshape(n_tok, -1), yb.reshape(n_tok, -1), h, w_out[:split], w_out[split:], norm_xa[layer],
            xa_wq[layer].astype(BF16), kt, v, xa_wo[layer].astype(BF16), norm_ffn[layer], w_hi, w_lo, bias)
        last = layer == depth - 1
        h, w_in = _moe_layer(h, xn, route, counts, moe_w_gate, moe_w_up, moe_w_down, layer,
                             final_norm if last else None, None if last else in_proj_weights(layer + 1))
    return h.reshape(bsz, length, d)
```

```python
import functools

import jax
import jax.numpy as jnp
import numpy as np
from jax import lax
from jax.experimental import pallas as pl
from jax.experimental.pallas import tpu as pltpu
from jax.experimental.pallas import tpu_sc as plsc

F32 = jnp.float32
BF16 = jnp.bfloat16
EPS = 1e-6

SC_DIM = 512
SSM_HEADS = 16
SSM_HEAD_DIM = 64
SSM_INNER = 1024
SSM_GROUPS = 2
SSM_STATE = 128
SSM_XBC = SSM_INNER + 2 * SSM_GROUPS * SSM_STATE
SSD_CHUNK = 128
SSD_STEP_ROWS = 512
GDN_HEADS = 8
GDN_D = 128
GDN_CHUNK = 64
GDN_TILE = 128
GDN_STEP_ROWS = 512
SB_HEAD_DIM = 64
SB_DIM = 512
SB_BLOCK = 128
SB_STEP_HEADS = 8
SB_STEP_ROWS = 512
SB_EAGER_BLOCKS = 2
XA_HEADS = 4
XA_HEAD_DIM = 256
MOE_GROUPS = 4
MOE_PER_GROUP = 8
MOE_EXPERTS = 32
MOE_ROWS = 512
POST_GROUP_ROWS = 512
SC_CORES = 2
SC_SUBCORES = 16
SC_WORKERS = SC_CORES * SC_SUBCORES
SC_CHUNK = 64
HALO = 8
CONV_CHUNK = 512
LANES = 128
SB_LOG_ZERO = -104.0
VMEM_LIMIT = 56 * 1024 * 1024


def _cparams(*sem):
    return pltpu.CompilerParams(dimension_semantics=sem, vmem_limit_bytes=VMEM_LIMIT)


def _mm(a, b):
    return jnp.dot(a.astype(BF16), b.astype(BF16), preferred_element_type=F32)


def _mm_nt(a, b):
    return lax.dot_general(a.astype(BF16), b.astype(BF16), (((1,), (1,)), ((), ())),
                           preferred_element_type=F32)


def _split_bf16(x, n):
    parts, r = [], x
    for _ in range(n):
        p = r.astype(BF16)
        parts.append(p)
        r = r - p.astype(F32)
    return parts


def _mm_sel_rhs(x, sel, n=3):
    return sum(jnp.dot(p, sel, preferred_element_type=F32) for p in _split_bf16(x, n))


def _mm_sel_lhs(sel, x, n=3):
    return sum(jnp.dot(sel, p, preferred_element_type=F32) for p in _split_bf16(x, n))


def _spread_heads(x, first, n_heads, width):
    rows = x.shape[0]
    col = lambda h: jnp.broadcast_to(x[:, first + h:first + h + 1], (rows, LANES))
    if width == LANES:
        return jnp.concatenate([col(h) for h in range(n_heads)], axis=1)
    left = lax.broadcasted_iota(jnp.int32, (rows, LANES), 1) < width
    return jnp.concatenate([jnp.where(left, col(h), col(h + 1)) for h in range(0, n_heads, 2)], axis=1)


def _pack_halves(x):
    n = x.shape[1] // 2
    lo = pltpu.bitcast(x[:, :n].astype(BF16).astype(F32), jnp.int32)
    hi = pltpu.bitcast(x[:, n:].astype(BF16).astype(F32), jnp.int32)
    return lax.shift_right_logical(lo, 16) | (hi & jnp.int32(-65536))


def _unpack_halves(p):
    lo = pltpu.bitcast(lax.shift_left(p, 16), F32)
    hi = pltpu.bitcast(p & jnp.int32(-65536), F32)
    return jnp.concatenate([lo, hi], axis=1)


def _silu(x):
    half = 0.5 * x
    return half * jnp.tanh(half) + half


def _softplus(x):
    return jnp.maximum(x, 0.0) + jnp.log(1.0 + jnp.exp(-jnp.abs(x)))


def _rms(x, g):
    return x * lax.rsqrt(jnp.mean(x * x, axis=-1, keepdims=True) + EPS) * g


def _rms_matmul_kernel(x_ref, g_ref, w_ref, ws_ref, o_ref, os_ref, *, transposed):
    mm = _mm_nt if transposed else _mm
    xn = _rms(x_ref[...], g_ref[...]).astype(BF16)
    o_ref[...] = mm(xn, w_ref[...])
    os_ref[...] = mm(xn, ws_ref[...])


def rms_matmul(x, g, w, ws, tm=512, tn=512, transposed=False):
    m, k = x.shape
    n = w.shape[0] if transposed else w.shape[1]
    tm = min(tm, m)
    w_specs = ([pl.BlockSpec((tn, k), lambda j, i: (j, 0)), pl.BlockSpec((LANES, k), lambda j, i: (0, 0))]
               if transposed else
               [pl.BlockSpec((k, tn), lambda j, i: (0, j)), pl.BlockSpec((k, LANES), lambda j, i: (0, 0))])
    main, small = pl.pallas_call(
        functools.partial(_rms_matmul_kernel, transposed=transposed),
        grid=(n // tn, m // tm),
        in_specs=[
            pl.BlockSpec((tm, k), lambda j, i: (i, 0)),
            pl.BlockSpec((1, k), lambda j, i: (0, 0)),
            *w_specs,
        ],
        out_specs=[
            pl.BlockSpec((tm, tn), lambda j, i: (i, j)),
            pl.BlockSpec((None, tm, LANES), lambda j, i: (j, i, 0)),
        ],
        out_shape=[jax.ShapeDtypeStruct((m, n), F32), jax.ShapeDtypeStruct((n // tn, m, LANES), F32)],
        compiler_params=_cparams("parallel", "parallel"),
        name="rms_matmul",
    )(x, g.reshape(1, k), w, ws)
    return main, small[0]


def _causal_conv(ext_ref, w_ref, rows):
    width = w_ref.shape[0]
    ext = ext_ref[...]
    acc = None
    for j in range(width):
        shift = width - 1 - j
        moved = ext if shift == 0 else pltpu.roll(ext, shift, axis=0)
        term = w_ref[j:j + 1, :] * moved[HALO:HALO + rows, :]
        acc = term if acc is None else acc + term
    return acc


def _rms_matmul_conv_kernel(x_ref, g_ref, w_ref, cw_ref, cb_ref, o_ref, *ext_refs, tiles_per_seq):
    tm = x_ref.shape[0]
    starts_sequence = pl.program_id(1) % tiles_per_seq == 0

    @pl.when(starts_sequence)
    def _():
        for ext_ref in ext_refs:
            ext_ref[0:HALO, :] = jnp.zeros((HALO, CONV_CHUNK), F32)

    @pl.when(jnp.logical_not(starts_sequence))
    def _():
        for ext_ref in ext_refs:
            ext_ref[0:HALO, :] = ext_ref[tm:tm + HALO, :]

    xn = _rms(x_ref[...], g_ref[...]).astype(BF16)
    for c, ext_ref in enumerate(ext_refs):
        cols = slice(c * CONV_CHUNK, (c + 1) * CONV_CHUNK)
        ext_ref[HALO:, :] = jnp.dot(xn, w_ref[:, cols], preferred_element_type=F32)
        o_ref[:, cols] = _causal_conv(ext_ref, cw_ref.at[:, cols], tm) + cb_ref[:, cols]


def rms_matmul_conv(x, g, w, conv_w, conv_b, seq_len, tm=1024, tn=1536):
    m, k = x.shape
    n = w.shape[1]
    cols = lambda rows: pl.BlockSpec((rows, tn), lambda j, i: (0, j))
    return pl.pallas_call(
        functools.partial(_rms_matmul_conv_kernel, tiles_per_seq=seq_len // tm),
        grid=(n // tn, m // tm),
        in_specs=[
            pl.BlockSpec((tm, k), lambda j, i: (i, 0)),
            pl.BlockSpec((1, k), lambda j, i: (0, 0)),
            cols(k), cols(conv_w.shape[0]), cols(1),
        ],
        out_specs=pl.BlockSpec((tm, tn), lambda j, i: (i, j)),
        out_shape=jax.ShapeDtypeStruct((m, n), F32),
        scratch_shapes=[pltpu.VMEM((tm + HALO, CONV_CHUNK), F32)] * (tn // CONV_CHUNK),
        compiler_params=_cparams("arbitrary", "arbitrary"),
        name="rms_matmul_conv",
    )(x, g.reshape(1, k), w, conv_w, conv_b.reshape(1, n))


def _rms_matmul_gated_kernel(x_ref, g_ref, wz_ref, wbcx_ref, ws_ref, cw_ref, z_ref, ya_ref, os_ref, ext_ref,
                             *, tiles_per_seq):
    tm = x_ref.shape[0]
    starts_sequence = pl.program_id(0) % tiles_per_seq == 0

    @pl.when(starts_sequence)
    def _():
        ext_ref[0:HALO, :] = jnp.zeros((HALO, SC_DIM), F32)

    @pl.when(jnp.logical_not(starts_sequence))
    def _():
        ext_ref[0:HALO, :] = ext_ref[tm:tm + HALO, :]

    xn = _rms(x_ref[...], g_ref[...]).astype(BF16)
    z_ref[...] = _mm_nt(xn, wz_ref[...])
    os_ref[...] = _mm_nt(xn, ws_ref[...])
    bcx = _mm_nt(xn, wbcx_ref[...])
    ext_ref[HALO:, :] = bcx[:, SC_DIM:2 * SC_DIM] * bcx[:, 2 * SC_DIM:]
    ya_ref[...] = (bcx[:, :SC_DIM] * _causal_conv(ext_ref, cw_ref, tm)).astype(ya_ref.dtype)


def rms_matmul_gated(x, g, w_z, w_bcx, w_small, conv_w, seq_len, tm=1024):
    m, k = x.shape
    const = lambda a: pl.BlockSpec(a.shape, lambda i: (0,) * a.ndim)
    rows = lambda w: pl.BlockSpec((tm, w), lambda i: (i, 0))
    g = g.reshape(1, k)
    return pl.pallas_call(
        functools.partial(_rms_matmul_gated_kernel, tiles_per_seq=seq_len // tm),
        grid=(m // tm,),
        in_specs=[rows(k), const(g), const(w_z), const(w_bcx), const(w_small), const(conv_w)],
        out_specs=[rows(w_z.shape[0]), rows(SC_DIM), rows(LANES)],
        out_shape=[jax.ShapeDtypeStruct((m, w_z.shape[0]), F32), jax.ShapeDtypeStruct((m, SC_DIM), BF16),
                   jax.ShapeDtypeStruct((m, LANES), F32)],
        scratch_shapes=[pltpu.VMEM((tm + HALO, SC_DIM), F32)],
        compiler_params=_cparams("arbitrary"),
        name="rms_matmul_gated",
    )(x, g, w_z, w_bcx, w_small, conv_w)


def _ssd_kernel(xbc_ref, z_ref, dt_ref, dtt_ref, dtb_r_ref, dtb_c_ref,
                alog_r_ref, alog_c_ref, d_ref, nw_ref, tri_ref, trit_ref,
                o_ref, s_ref):
    q = SSD_CHUNK

    @pl.when(pl.program_id(1) == 0)
    def _():
        s_ref[...] = jnp.zeros_like(s_ref)

    for sub in range(xbc_ref.shape[0] // q):
        rows = slice(sub * q, (sub + 1) * q)
        _ssd_chunk(_silu(xbc_ref[rows, :]), z_ref[rows, :], dt_ref[rows, :], dtt_ref[:, rows], dtb_r_ref, dtb_c_ref,
                   alog_r_ref, alog_c_ref, d_ref, nw_ref, tri_ref, trit_ref, o_ref.at[rows, :], s_ref)


def _ssd_chunk(xbc, z, dt_raw, dtt_raw, dtb_r_ref, dtb_c_ref, alog_r_ref, alog_c_ref, d_ref, nw_ref, tri_ref,
               trit_ref, o_ref, s_ref):
    q = SSD_CHUNK
    hpg = SSM_HEADS // SSM_GROUPS
    gw = hpg * SSM_HEAD_DIM
    xs = xbc[:, :SSM_INNER]
    bm = xbc[:, SSM_INNER:SSM_INNER + SSM_GROUPS * SSM_STATE]
    cm = xbc[:, SSM_INNER + SSM_GROUPS * SSM_STATE:]

    dt = _softplus(dt_raw + dtb_r_ref[...])
    acs = _mm_sel_lhs(tri_ref[...], dt * -jnp.exp(alog_r_ref[...]))
    dtt = _softplus(dtt_raw + dtb_c_ref[...])
    acst = _mm_sel_rhs(dtt * -jnp.exp(alog_c_ref[...]), trit_ref[...])
    dt_full = _spread_heads(dt, 0, SSM_HEADS, SSM_HEAD_DIM)
    acs_full = _spread_heads(acs, 0, SSM_HEADS, SSM_HEAD_DIM)
    acs_col = _spread_heads(acs, 0, SSM_HEADS, q)

    xdt = xs * dt_full
    acs_last = acs_full[q - 1:q, :]
    xw = xdt * jnp.exp(acs_last - acs_full)
    chunk_decay = jnp.exp(acs_last)

    row = lax.broadcasted_iota(jnp.int32, (q, q), 0)
    col = lax.broadcasted_iota(jnp.int32, (q, q), 1)
    causal = row >= col
    lane = lax.broadcasted_iota(jnp.int32, (q, 2 * SSM_HEAD_DIM), 1)

    y_diag, y_off = [], []
    for g in range(SSM_GROUPS):
        bm_g = bm[:, g * SSM_STATE:(g + 1) * SSM_STATE]
        cm_g = cm[:, g * SSM_STATE:(g + 1) * SSM_STATE]
        cb_g = _mm_nt(cm_g, bm_g)
        state = s_ref[g]
        y_off.append(_mm(cm_g, state))
        s_ref[g] = state * chunk_decay[:, g * gw:(g + 1) * gw] + _mm(bm_g.T, xw[:, g * gw:(g + 1) * gw])
        for pair in range(hpg // 2):
            h0 = g * hpg + 2 * pair
            xdt_pair = xdt[:, h0 * SSM_HEAD_DIM:(h0 + 2) * SSM_HEAD_DIM]
            weights = []
            for h in (h0, h0 + 1):
                seg = acs_col[:, h * q:(h + 1) * q] - acst[h:h + 1, :]
                weights.append(cb_g * jnp.where(causal, jnp.exp(seg), 0.0))
            both = _mm(jnp.concatenate(weights, axis=0), xdt_pair)
            y_diag.append(jnp.where(lane < SSM_HEAD_DIM, both[:q], both[q:]))
    y = (jnp.concatenate(y_diag, axis=1) + jnp.concatenate(y_off, axis=1) * jnp.exp(acs_full)
         + xs * d_ref[...])
    y = y * _silu(z)
    halves = []
    for g in range(SSM_GROUPS):
        yg = y[:, g * gw:(g + 1) * gw]
        halves.append(yg * lax.rsqrt(jnp.mean(yg * yg, axis=-1, keepdims=True) + EPS))
    o_ref[...] = (jnp.concatenate(halves, axis=1) * nw_ref[...]).astype(o_ref.dtype)


def _pad_lanes(v, fill=0.0):
    return jnp.pad(v.astype(F32), (0, LANES - v.shape[0]), constant_values=fill).reshape(1, LANES)


def _pad_col(v, rows=16):
    return jnp.pad(v.astype(F32), (0, rows - v.shape[0])).reshape(rows, 1)


def ssd_mixer(xbc3, y3, small3, smallt, dt_bias, a_log, d_skip, norm_w):
    bsz, length, _ = y3.shape
    q = SSD_CHUNK
    tri = jnp.asarray(np.tril(np.ones((q, q), np.float32)), BF16)
    trit = jnp.asarray(np.triu(np.ones((q, q), np.float32)), BF16)
    d_full = jnp.repeat(d_skip.astype(F32), SSM_HEAD_DIM).reshape(1, SSM_INNER)
    const = lambda a: pl.BlockSpec(a.shape, lambda b, c: (0,) * a.ndim)
    args = [_pad_lanes(dt_bias), _pad_col(dt_bias), _pad_lanes(a_log),
            _pad_col(a_log), d_full, norm_w.reshape(1, -1), tri, trit]
    rows = min(SSD_STEP_ROWS, length)
    return pl.pallas_call(
        _ssd_kernel,
        grid=(bsz, length // rows),
        in_specs=[
            pl.BlockSpec((None, rows, SSM_XBC), lambda b, c: (b, c, 0)),
            pl.BlockSpec((None, rows, SSM_INNER), lambda b, c: (b, c, 0)),
            pl.BlockSpec((None, rows, LANES), lambda b, c: (b, c, 0)),
            pl.BlockSpec((None, 16, rows), lambda b, c: (b, 0, c)),
        ] + [const(a) for a in args],
        out_specs=pl.BlockSpec((None, rows, SSM_INNER), lambda b, c: (b, c, 0)),
        out_shape=jax.ShapeDtypeStruct((bsz, length, SSM_INNER), BF16),
        scratch_shapes=[pltpu.VMEM((SSM_GROUPS, SSM_STATE, SSM_INNER // SSM_GROUPS), F32)],
        compiler_params=_cparams("parallel", "arbitrary"),
        name="ssd_mixer",
    )(xbc3, y3, small3, smallt, *args)


def _unit_lower_inverse(mats, row, col):
    eye = jnp.where(row == col, 1.0, 0.0)
    blk = lambda n: (row >> (n.bit_length() - 1)) == (col >> (n.bit_length() - 1))
    size = row.shape[0]
    p = [jnp.where(blk(16), -a, 0.0) for a in mats]
    t = [eye + x for x in p]
    p = [_mm(x, x) for x in p]
    for _ in range(2):
        both = [_mm(jnp.concatenate([x, y], axis=0), x) for x, y in zip(p, t)]
        p = [b[:size] for b in both]
        t = [y + b[size:] for y, b in zip(t, both)]
    t = [y + _mm(y, x) for y, x in zip(t, p)]
    for n in (16, 32):
        band = blk(2 * n) & jnp.logical_not(blk(n))
        left = [_mm(y, jnp.where(band, a, 0.0)) for y, a in zip(t, mats)]
        t = [y - _mm(x, y) for y, x in zip(t, left)]
    return t


def _gdn_kernel(qkv_ref, z_ref, ab_ref, abt_ref, dtb_r_ref, dtb_c_ref, alog_r_ref,
                alog_c_ref, nw_ref, tri_ref, trit_ref, o_ref, s_ref):
    n = GDN_TILE

    @pl.when(pl.program_id(1) == 0)
    def _():
        s_ref[...] = jnp.zeros_like(s_ref)

    for sub in range(qkv_ref.shape[0] // n):
        rows = slice(sub * n, (sub + 1) * n)
        _gdn_tile(_silu(qkv_ref[rows, :]), z_ref[rows, :], ab_ref[rows, :], abt_ref[:, rows], dtb_r_ref, dtb_c_ref,
                  alog_r_ref, alog_c_ref, nw_ref, tri_ref, trit_ref, o_ref.at[rows, :], s_ref)


def _gdn_tile(qkv, z, ab, abt, dtb_r_ref, dtb_c_ref, alog_r_ref, alog_c_ref, nw_ref, tri_ref, trit_ref, o_ref, s_ref):
    n = GDN_TILE
    c = GDN_CHUNK
    d = GDN_D
    hd = GDN_HEADS * d
    g = -jnp.exp(alog_r_ref[...]) * _softplus(ab + dtb_r_ref[...])
    gc_full = _spread_heads(_mm_sel_lhs(tri_ref[...], g), 0, GDN_HEADS, d)
    beta_full = _spread_heads(jax.nn.sigmoid(ab), GDN_HEADS, GDN_HEADS, d)
    gt = -jnp.exp(alog_c_ref[...]) * _softplus(abt + dtb_c_ref[...])
    gct = _mm_sel_rhs(gt, trit_ref[...])

    row = lax.broadcasted_iota(jnp.int32, (n, n), 0)
    col = lax.broadcasted_iota(jnp.int32, (n, n), 1)
    same = (row >> (c.bit_length() - 1)) == (col >> (c.bit_length() - 1))
    incl = same & (row >= col)
    strict = same & (row > col)
    zeros_half = jnp.zeros((c, d), F32)

    heads = range(GDN_HEADS)
    sl = [slice(h * d, (h + 1) * d) for h in heads]
    l2n = lambda x: x * lax.rsqrt(jnp.sum(x * x, axis=-1, keepdims=True) + EPS)
    qn = [l2n(qkv[:, sl[h]]) * (d ** -0.5) for h in heads]
    kn = [l2n(qkv[:, hd + h * d:hd + (h + 1) * d]) for h in heads]
    vh = [qkv[:, 2 * hd + h * d:2 * hd + (h + 1) * d] for h in heads]
    gcol = [gc_full[:, sl[h]] for h in heads]
    beta = [beta_full[:, sl[h]] for h in heads]
    edec = [jnp.exp(gcol[h] - gct[h:h + 1, :]) for h in heads]
    egc = [jnp.exp(x) for x in gcol]
    kb = [kn[h] * beta[h] for h in heads]
    on_k = [_mm_nt(jnp.concatenate([kb[h], qn[h]], axis=0), kn[h]) for h in heads]
    lower = [jnp.where(strict, on_k[h][:n] * edec[h], 0.0) for h in heads]
    aqk = [jnp.where(incl, on_k[h][n:] * edec[h], 0.0) for h in heads]
    tinv = _unit_lower_inverse(lower, row, col)
    sol = [_mm(tinv[h], jnp.concatenate([vh[h] * beta[h], kb[h] * egc[h]], axis=1)) for h in heads]
    qd = [qn[h] * egc[h] for h in heads]
    glast = [(gcol[h][c - 1:c, :], gcol[h][n - 1:n, :]) for h in heads]
    kdt = [(kn[h] * jnp.exp(jnp.concatenate([jnp.broadcast_to(glast[h][0], (c, d)),
                                             jnp.broadcast_to(glast[h][1], (c, d))], axis=0) - gcol[h])).T
           for h in heads]
    s0 = [s_ref[h] for h in heads]
    on_s0 = [_mm(jnp.concatenate([sol[h][:c, d:], qd[h][:c]], axis=0), s0[h]) for h in heads]
    v0 = [sol[h][:c, :d] - on_s0[h][:c] for h in heads]
    s1 = [s0[h] * jnp.exp(glast[h][0]) + _mm(kdt[h], jnp.concatenate([v0[h], zeros_half], axis=0)) for h in heads]
    on_s1 = [_mm(jnp.concatenate([sol[h][c:, d:], qd[h][c:]], axis=0), s1[h]) for h in heads]
    v1 = [sol[h][c:, :d] - on_s1[h][:c] for h in heads]
    for h in heads:
        s_ref[h] = s1[h] * jnp.exp(glast[h][1]) + _mm(kdt[h], jnp.concatenate([zeros_half, v1[h]], axis=0))
    outs = []
    for h in heads:
        o = (jnp.concatenate([on_s0[h][c:], on_s1[h][c:]], axis=0)
             + _mm(aqk[h], jnp.concatenate([v0[h], v1[h]], axis=0)))
        o = o * lax.rsqrt(jnp.mean(o * o, axis=-1, keepdims=True) + EPS) * nw_ref[...]
        outs.append(o * _silu(z[:, sl[h]]))
    o_ref[...] = jnp.concatenate(outs, axis=1).astype(o_ref.dtype)


def gated_deltanet_mixer(qkv3, y3, small3, smallt, dt_bias, a_log, norm_w):
    bsz, length, _ = y3.shape
    n = GDN_TILE
    hd = GDN_HEADS * GDN_D
    idx = np.arange(n)
    same = (idx[:, None] // GDN_CHUNK) == (idx[None, :] // GDN_CHUNK)
    tri = jnp.asarray(same & (idx[:, None] >= idx[None, :]), BF16)
    trit = jnp.asarray(same & (idx[:, None] <= idx[None, :]), BF16)
    const = lambda a: pl.BlockSpec(a.shape, lambda b, c: (0,) * a.ndim)
    args = [_pad_lanes(dt_bias), _pad_col(dt_bias), _pad_lanes(a_log), _pad_col(a_log),
            norm_w.reshape(1, -1), tri, trit]
    rows = min(GDN_STEP_ROWS, length)
    return pl.pallas_call(
        _gdn_kernel,
        grid=(bsz, length // rows),
        in_specs=[
            pl.BlockSpec((None, rows, 3 * hd), lambda b, c: (b, c, 0)),
            pl.BlockSpec((None, rows, hd), lambda b, c: (b, c, 0)),
            pl.BlockSpec((None, rows, LANES), lambda b, c: (b, c, 0)),
            pl.BlockSpec((None, 16, rows), lambda b, c: (b, 0, c)),
        ] + [const(a) for a in args],
        out_specs=pl.BlockSpec((None, rows, hd), lambda b, c: (b, c, 0)),
        out_shape=jax.ShapeDtypeStruct((bsz, length, hd), BF16),
        scratch_shapes=[pltpu.VMEM((GDN_HEADS, GDN_D, GDN_D), F32)],
        compiler_params=_cparams("parallel", "arbitrary"),
        name="gated_deltanet",
    )(qkv3, y3, small3, smallt, *args)


def _sb_kernel(q_ref, k_ref, v_ref, upper_ref, o_ref):
    blk = SB_BLOCK
    n_sub = q_ref.shape[0] // blk
    first = pl.program_id(2) * n_sub
    parts = [_sb_query_block(first + s, q_ref[s * blk:(s + 1) * blk, :], k_ref, v_ref, upper_ref[...])
             for s in range(n_sub)]
    for s, finish in enumerate(parts):
        o_ref[s * blk:(s + 1) * blk, :] = finish().astype(o_ref.dtype)


def _sb_query_block(i, q, k_ref, v_ref, upper):
    blk = SB_BLOCK
    pair_w = 2 * SB_HEAD_DIM
    n_pairs = SB_STEP_HEADS // 2
    q = q * (SB_HEAD_DIM ** -0.5)
    lane = lax.broadcasted_iota(jnp.int32, (blk, pair_w), 1)
    first_head = lane < SB_HEAD_DIM
    qs = []
    for p in range(n_pairs):
        q2 = q[:, p * pair_w:(p + 1) * pair_w]
        qs += [jnp.where(first_head, q2, 0.0).astype(BF16), jnp.where(first_head, 0.0, q2).astype(BF16)]
    row = lax.broadcasted_iota(jnp.int32, (blk, blk), 0)
    col = lax.broadcasted_iota(jnp.int32, (blk, blk), 1)
    earlier = col < row
    heads = range(SB_STEP_HEADS)

    def local_part(kb, diagonal, exists=None):
        start = pl.multiple_of(kb * blk, blk)
        k = k_ref[pl.ds(start, blk), :].astype(BF16)
        v = v_ref[pl.ds(start, blk), :].astype(BF16)
        kp = [k[:, p * pair_w:(p + 1) * pair_w] for p in range(n_pairs)]
        vp = [v[:, p * pair_w:(p + 1) * pair_w] for p in range(n_pairs)]
        logits = [lax.dot_general(qs[h], kp[h // 2], (((1,), (1,)), ((), ())), preferred_element_type=F32)
                  for h in heads]
        keep = earlier if diagonal else None
        if exists is not None:
            keep = exists if keep is None else keep & exists
        log_keep = [-_softplus(x) for x in logits]
        if keep is not None:
            log_keep = [jnp.where(keep, x, 0.0) for x in log_keep]
        inside = [_mm_sel_rhs(x, upper, 2) for x in log_keep]
        totals = [jnp.sum(x, axis=-1, keepdims=True) for x in log_keep]
        return logits, log_keep, inside, totals, vp, keep

    def carried_part(local, accs, sticks):
        logits, log_keep, inside, totals, vp, keep = local
        w = [jnp.exp(logits[h] + log_keep[h] + inside[h] + sticks[h]) for h in heads]
        if keep is not None:
            w = [jnp.where(keep, x, 0.0) for x in w]
        pv = [jnp.dot(w[h].astype(BF16), vp[h // 2], preferred_element_type=F32) for h in heads]
        accs = tuple(accs[p] + jnp.where(first_head, pv[2 * p], pv[2 * p + 1]) for p in range(n_pairs))
        sticks = tuple(sticks[h] + totals[h] for h in heads)
        return accs, sticks

    accs = tuple(jnp.zeros((blk, pair_w), F32) for _ in range(n_pairs))
    sticks = tuple(jnp.zeros((blk, 1), F32) for _ in heads)
    eager = [local_part(i, True)]
    for back in range(1, SB_EAGER_BLOCKS + 1):
        eager.append(local_part(jnp.maximum(i - back, 0), False, exists=(row >= 0) & (i - back >= 0)))
    for local in eager:
        accs, sticks = carried_part(local, accs, sticks)

    def alive(state):
        kb, _, sticks = state
        longest = sticks[0]
        for s in sticks[1:]:
            longest = jnp.maximum(longest, s)
        return (kb >= 0) & (jnp.max(longest) > SB_LOG_ZERO)

    def body(state):
        kb, accs, sticks = state
        accs, sticks = carried_part(local_part(kb, False), accs, sticks)
        return kb - 1, accs, sticks

    def finish():
        _, done, _ = lax.while_loop(alive, body, (i - 1 - SB_EAGER_BLOCKS, accs, sticks))
        return jnp.concatenate(done, axis=1)

    return finish


def stick_breaking_mixer(y3, col0):
    bsz, length, _ = y3.shape
    blk = SB_BLOCK
    step_w = SB_STEP_HEADS * SB_HEAD_DIM
    steps = SB_DIM // step_w
    q0 = col0 // step_w
    idx = np.arange(blk)
    upper = jnp.asarray(idx[:, None] > idx[None, :], BF16)
    resident = lambda off: pl.BlockSpec((None, length, step_w), lambda b, p, i: (b, 0, q0 + off + p),
                                        pipeline_mode=pl.Buffered(1))
    rows = min(SB_STEP_ROWS, length)
    return pl.pallas_call(
        _sb_kernel,
        grid=(bsz, steps, length // rows),
        in_specs=[
            pl.BlockSpec((None, rows, step_w), lambda b, p, i: (b, i, q0 + p)),
            resident(steps),
            resident(2 * steps),
            pl.BlockSpec((blk, blk), lambda b, p, i: (0, 0)),
        ],
        out_specs=pl.BlockSpec((None, rows, step_w), lambda b, p, i: (b, i, p)),
        out_shape=jax.ShapeDtypeStruct((bsz, length, SB_DIM), BF16),
        compiler_params=_cparams("parallel", "parallel", "arbitrary"),
        name="stick_breaking",
    )(y3, y3, y3, upper)


def _mixer_out(a_ref, b_ref, h_ref, w_ref, rows):
    split = a_ref.shape[1]
    return h_ref[rows, :] + (_mm(a_ref[rows, :], w_ref[:split, :]) + _mm(b_ref[rows, :], w_ref[split:, :]))


def _cross_attention(h, g_ref, wq_ref, kt_ref, v_ref, wo_ref):
    u = _rms(h, g_ref[...]).astype(BF16)
    q = _mm(u, wq_ref[...])
    heads = []
    for hd in range(XA_HEADS):
        sl = slice(hd * XA_HEAD_DIM, (hd + 1) * XA_HEAD_DIM)
        s = jnp.dot(q[:, sl].astype(BF16), kt_ref[sl, :], preferred_element_type=F32)
        p = jnp.exp(s - jnp.max(s, axis=-1, keepdims=True))
        p = p * (1.0 / jnp.sum(p, axis=-1, keepdims=True))
        heads.append(jnp.dot(p.astype(BF16), v_ref[:, sl], preferred_element_type=F32))
    o = jnp.concatenate(heads, axis=1).astype(BF16)
    return h + _mm(o, wo_ref[...])


def _route(xn, whi_ref, wlo_ref, b_ref, before_ref, run_ref):
    x_hi = xn.astype(BF16)
    x_lo = (xn - x_hi.astype(F32)).astype(BF16)
    wide = jnp.dot(x_hi, jnp.concatenate([whi_ref[...], wlo_ref[...]], axis=1), preferred_element_type=F32)
    logits = (wide[:, :LANES] + jnp.dot(x_lo, whi_ref[...], preferred_element_type=F32)
              + wide[:, LANES:] + b_ref[...])
    lane = lax.broadcasted_iota(jnp.int32, logits.shape, 1).astype(F32)
    neg = -1e30
    none = float(LANES)

    def top(vals):
        best = jnp.max(vals, axis=-1, keepdims=True)
        where = jnp.min(jnp.where(vals == best, lane, none), axis=-1, keepdims=True)
        return best, where

    gl = jnp.where(lane < MOE_GROUPS, logits, neg)
    gbest, gsel = top(gl)
    gprob = 1.0 / jnp.sum(jnp.exp(gl - gbest), axis=-1, keepdims=True)
    lo = MOE_GROUPS + gsel * MOE_PER_GROUP
    el = jnp.where((lane >= lo) & (lane < lo + MOE_PER_GROUP), logits, neg)
    m1, i1 = top(el)
    m2, i2 = top(jnp.where(lane == i1, neg, el))
    e = jnp.exp(m2 - m1)
    gate1 = gprob / (1.0 + e)
    gate2 = gprob * e / (1.0 + e)

    hot1 = lane == i1
    hot2 = lane == i2
    one1 = jnp.where(hot1, 1.0, 0.0)
    one2 = jnp.where(hot2, 1.0, 0.0)
    prefix = jnp.dot(before_ref[...], jnp.concatenate([one1, one2], axis=1).astype(BF16), preferred_element_type=F32)
    prefix1, prefix2 = prefix[:, :LANES], prefix[:, LANES:]
    total1 = jnp.sum(one1, axis=0, keepdims=True)
    running = run_ref[...]
    rank1 = jnp.sum(jnp.where(hot1, prefix1 + running, 0.0), axis=-1, keepdims=True)
    rank2 = jnp.sum(jnp.where(hot2, prefix2 + (running + total1), 0.0), axis=-1, keepdims=True)
    running = running + total1 + jnp.sum(one2, axis=0, keepdims=True)
    run_ref[...] = running

    fields = (i1 - MOE_GROUPS, i2 - MOE_GROUPS, gate1, gate2, rank1, rank2)
    out = jnp.zeros_like(logits)
    for k, val in enumerate(fields):
        out = jnp.where(lane == k, val, out)
    return out


def _post_mixer_kernel(a_ref, b_ref, h_ref, w_ref, gxa_ref, wq_ref, kt_ref, v_ref, wo_ref,
                       gffn_ref, whi_ref, wlo_ref, bias_ref, before_ref,
                       h_out_ref, xn_ref, r_ref, cnt_ref, run_ref):
    @pl.when(pl.program_id(0) == 0)
    def _():
        run_ref[...] = jnp.zeros_like(run_ref)

    h = _mixer_out(a_ref, b_ref, h_ref, w_ref, slice(None))
    h = _cross_attention(h, gxa_ref, wq_ref, kt_ref, v_ref, wo_ref)
    h_out_ref[...] = h
    xn = _rms(h, gffn_ref[...])
    xn_ref[...] = _pack_halves(xn)
    group = before_ref.shape[0]
    for start in range(0, h_ref.shape[0], group):
        rows = slice(start, start + group)
        r_ref[rows, :] = _route(xn[rows, :], whi_ref, wlo_ref, bias_ref, before_ref, run_ref)
    cnt_ref[...] = run_ref[...]


def post_mixer(ya, yb, h, w_out, w_out_index, g_xa, wq, kt, v, wo, g_ffn, w_hi, w_lo, bias, layer, tm=1024):
    m, d = h.shape
    tiles_per_batch = m // kt.shape[0] // tm
    idx = np.arange(min(POST_GROUP_ROWS, tm))
    before = jnp.asarray(idx[:, None] > idx[None, :], BF16)
    rows = lambda w: pl.BlockSpec((tm, w), lambda i: (i, 0))
    const = lambda a: pl.BlockSpec(a.shape, lambda i: (0,) * a.ndim, pipeline_mode=pl.Buffered(1))
    per_batch = lambda a: pl.BlockSpec((None,) + a.shape[1:], lambda i: (i // tiles_per_batch, 0, 0))
    picked = lambda a, k: pl.BlockSpec((None,) + a.shape[1:], lambda i: (k, 0, 0), pipeline_mode=pl.Buffered(1))
    g_xa, g_ffn = g_xa.reshape(1, d), g_ffn.reshape(1, d)
    return pl.pallas_call(
        _post_mixer_kernel,
        grid=(m // tm,),
        in_specs=[rows(ya.shape[1]), rows(yb.shape[1]), rows(d), picked(w_out, w_out_index), const(g_xa),
                  picked(wq, layer), per_batch(kt), per_batch(v), picked(wo, layer), const(g_ffn), const(w_hi),
                  const(w_lo), const(bias), const(before)],
        out_specs=[rows(d), rows(d // 2), rows(LANES), pl.BlockSpec((1, LANES), lambda i: (0, 0))],
        out_shape=[jax.ShapeDtypeStruct((m, d), F32), jax.ShapeDtypeStruct((m, d // 2), jnp.int32),
                   jax.ShapeDtypeStruct((m, LANES), F32), jax.ShapeDtypeStruct((1, LANES), F32)],
        scratch_shapes=[pltpu.VMEM((1, LANES), F32)],
        compiler_params=_cparams("arbitrary"),
        name="post_mixer",
    )(ya, yb, h, w_out, g_xa, wq, kt, v, wo, g_ffn, w_hi, w_lo, bias, before)


def _expert_kernel(table_ref, x_ref, wg_hbm, wu_hbm, wd_hbm, o_ref,
                   wg32_ref, wu32_ref, wd32_ref, wgb_ref, wub_ref, wdb_ref, sem_ref, *, layer):
    i = pl.program_id(0)
    beid_ref, valid_ref, first_ref, slot_ref, next_ref = (table_ref.at[k] for k in range(5))
    valid = valid_ref[i]

    def weight_copies(expert, slot):
        return (pltpu.make_async_copy(wg_hbm.at[layer, expert], wg32_ref.at[slot], sem_ref.at[slot, 0]),
                pltpu.make_async_copy(wu_hbm.at[layer, expert], wu32_ref.at[slot], sem_ref.at[slot, 1]),
                pltpu.make_async_copy(wd_hbm.at[layer, expert], wd32_ref.at[slot], sem_ref.at[slot, 2]))

    @pl.when(i == 0)
    def _():
        for copy in weight_copies(beid_ref[0], 0):
            copy.start()

    @pl.when(first_ref[i] == 1)
    def _():
        slot = slot_ref[i]
        for copy in weight_copies(beid_ref[i], slot):
            copy.wait()
        wgb_ref[...] = wg32_ref[slot].astype(BF16)
        wub_ref[...] = wu32_ref[slot].astype(BF16)
        wdb_ref[...] = wd32_ref[slot].astype(BF16)

        @pl.when(next_ref[i] >= 0)
        def _():
            for copy in weight_copies(next_ref[i], 1 - slot):
                copy.start()

    half = MOE_ROWS // 2

    def ffn(n_halves):
        row = lax.broadcasted_iota(jnp.int32, (half, 2 * x_ref.shape[1]), 0)
        xs = [jnp.where(row + k * half < valid, _unpack_halves(x_ref[k * half:(k + 1) * half, :]), 0.0).astype(BF16)
              for k in range(n_halves)]
        gates = [jnp.dot(x, wgb_ref[...], preferred_element_type=F32) for x in xs]
        ups = [jnp.dot(x, wub_ref[...], preferred_element_type=F32) for x in xs]
        acts = [(_silu(g) * u).astype(BF16) for g, u in zip(gates, ups)]
        for k, act in enumerate(acts):
            o_ref[k * half:(k + 1) * half, :] = _pack_halves(jnp.dot(act, wdb_ref[...], preferred_element_type=F32))

    @pl.when(valid > half)
    def _():
        ffn(2)

    @pl.when((valid > 0) & (valid <= half))
    def _():
        ffn(1)
        o_ref[half:, :] = jnp.zeros((half, o_ref.shape[1]), o_ref.dtype)

    @pl.when(valid == 0)
    def _():
        o_ref[...] = jnp.zeros_like(o_ref)


def moe_experts(blocks, xs, w_gate, w_up, w_down, layer):
    n_slots, packed = xs.shape
    d = 2 * packed
    rows = MOE_ROWS
    ff = w_gate.shape[3]
    grid_spec = pltpu.PrefetchScalarGridSpec(
        num_scalar_prefetch=1,
        grid=(n_slots // rows,),
        in_specs=[
            pl.BlockSpec((rows, packed), lambda i, *_: (i, 0)),
            pl.BlockSpec(memory_space=pl.ANY),
            pl.BlockSpec(memory_space=pl.ANY),
            pl.BlockSpec(memory_space=pl.ANY),
        ],
        out_specs=pl.BlockSpec((rows, packed), lambda i, *_: (i, 0)),
        scratch_shapes=[pltpu.VMEM((2, d, ff), F32), pltpu.VMEM((2, d, ff), F32), pltpu.VMEM((2, ff, d), F32),
                        pltpu.VMEM((d, ff), BF16), pltpu.VMEM((d, ff), BF16), pltpu.VMEM((ff, d), BF16),
                        pltpu.SemaphoreType.DMA((2, 3))],
    )
    return pl.pallas_call(
        functools.partial(_expert_kernel, layer=layer),
        grid_spec=grid_spec,
        out_shape=jax.ShapeDtypeStruct((n_slots, packed), jnp.int32),
        compiler_params=_cparams("arbitrary"),
        name="moe_experts",
    )(blocks, xs, w_gate, w_up, w_down)


def _sc_mesh():
    return plsc.VectorSubcoreMesh(core_axis_name="c", subcore_axis_name="s",
                                  num_cores=SC_CORES, num_subcores=SC_SUBCORES)


def _sc_worker():
    return lax.axis_index("s") * SC_CORES + lax.axis_index("c")


def _sc_double_buffered(n_chunks, fetch, drain):
    assert n_chunks % 2 == 0
    start = lambda copies: [c.start() for c in copies]
    wait = lambda copies: [c.wait() for c in copies]
    start(fetch(0, 0))

    @pl.loop(0, n_chunks, step=2)
    def _(j):
        wait(fetch(j, 0))

        @pl.when(j > 0)
        def _():
            wait(drain(j - 1, 1))

        start(fetch(j + 1, 1))
        start(drain(j, 0))
        wait(fetch(j + 1, 1))
        wait(drain(j, 0))

        @pl.when(j + 2 < n_chunks)
        def _():
            start(fetch(j + 2, 0))

        start(drain(j + 1, 1))

    wait(drain(n_chunks - 1, 1))


def sc_scatter_rows(x, dest, n_slots):
    n_tok, d = x.shape
    per_worker = n_tok // SC_WORKERS
    n_chunks = per_worker // SC_CHUNK
    by_worker = dest.reshape(dest.shape[0] * SC_WORKERS, n_chunks, SC_CHUNK)

    @functools.partial(
        pl.kernel, mesh=_sc_mesh(), out_type=jax.ShapeDtypeStruct((n_slots, d), x.dtype),
        scratch_types=[pltpu.VMEM((n_chunks, SC_CHUNK), jnp.int32), pltpu.VMEM((n_chunks, SC_CHUNK), jnp.int32),
                       pltpu.VMEM((2, SC_CHUNK, d), x.dtype), pltpu.SemaphoreType.DMA((2, 3))],
        name="moe_scatter_rows")
    def scatter(x_hbm, dest_hbm, out_hbm, i0_v, i1_v, rows_v, sem):
        wid = _sc_worker()
        pltpu.sync_copy(dest_hbm.at[wid], i0_v)
        pltpu.sync_copy(dest_hbm.at[SC_WORKERS + wid], i1_v)

        def fetch(j, buf):
            start = pl.multiple_of(wid * per_worker + j * SC_CHUNK, SC_CHUNK)
            return [pltpu.make_async_copy(x_hbm.at[pl.ds(start, SC_CHUNK)], rows_v.at[buf], sem.at[buf, 0])]

        def drain(j, buf):
            return [pltpu.make_async_copy(rows_v.at[buf], out_hbm.at[i0_v.at[j]], sem.at[buf, 1]),
                    pltpu.make_async_copy(rows_v.at[buf], out_hbm.at[i1_v.at[j]], sem.at[buf, 2])]

        _sc_double_buffered(n_chunks, fetch, drain)

    return scatter(x, by_worker)


def sc_gather_rows(table, idx, n_out):
    d = table.shape[1]
    per_worker = n_out // SC_WORKERS
    n_chunks = per_worker // SC_CHUNK

    @functools.partial(
        pl.kernel, mesh=_sc_mesh(), out_type=jax.ShapeDtypeStruct((n_out, d), table.dtype),
        scratch_types=[pltpu.VMEM((n_chunks, SC_CHUNK), jnp.int32), pltpu.VMEM((2, SC_CHUNK, d), table.dtype),
                       pltpu.SemaphoreType.DMA((2, 2))],
        name="moe_gather_rows")
    def gather(table_hbm, idx_hbm, out_hbm, idx_v, rows_v, sem):
        wid = _sc_worker()
        pltpu.sync_copy(idx_hbm.at[wid], idx_v)

        def fetch(j, buf):
            return [pltpu.make_async_copy(table_hbm.at[idx_v.at[j]], rows_v.at[buf], sem.at[buf, 0])]

        def drain(j, buf):
            start = pl.multiple_of(wid * per_worker + j * SC_CHUNK, SC_CHUNK)
            return [pltpu.make_async_copy(rows_v.at[buf], out_hbm.at[pl.ds(start, SC_CHUNK)], sem.at[buf, 1])]

        _sc_double_buffered(n_chunks, fetch, drain)

    return gather(table, idx.reshape(-1, n_chunks, SC_CHUNK))


def _combine_kernel(h_ref, y0_ref, y1_ref, r_ref, g_ref, o_ref, *, final_norm):
    route = r_ref[...]
    h = h_ref[...] + (route[:, 2:3] * _unpack_halves(y0_ref[...]) + route[:, 3:4] * _unpack_halves(y1_ref[...]))
    o_ref[...] = _rms(h, g_ref[...]) if final_norm else h


def moe_combine(h, y01, route, g, final_norm, tm=1024):
    m, d = h.shape
    tm = min(tm, m)
    rows = lambda w: pl.BlockSpec((tm, w), lambda i: (i, 0))
    return pl.pallas_call(
        functools.partial(_combine_kernel, final_norm=final_norm),
        grid=(m // tm,),
        in_specs=[rows(d), rows(d // 2), pl.BlockSpec((tm, d // 2), lambda i: (i + m // tm, 0)), rows(LANES),
                  pl.BlockSpec((1, d), lambda i: (0, 0))],
        out_specs=rows(d),
        out_shape=jax.ShapeDtypeStruct((m, d), F32),
        compiler_params=_cparams("parallel"),
        name="moe_combine",
    )(h, y01, y01, route, g.reshape(1, d))


def _pad_cols(w):
    return jnp.pad(w, ((0, 0), (0, LANES - w.shape[1])))


def _pad_rows(w):
    return jnp.pad(w, ((0, LANES - w.shape[0]), (0, 0)))


def _plan_kernel(route_ref, cnt_ref, incl_ref, dest_ref, table_ref):
    f32_sum = lambda x, axis: jnp.sum(x, axis=axis, keepdims=True)
    lane = lax.broadcasted_iota(jnp.int32, (LANES, LANES), 1)
    sub = lax.broadcasted_iota(jnp.int32, (LANES, LANES), 0)
    incl = incl_ref[...]
    is_expert = (lane >= MOE_GROUPS) & (lane < MOE_GROUPS + MOE_EXPERTS)
    shift = MOE_ROWS.bit_length() - 1
    counts = jnp.broadcast_to(cnt_ref[...], (LANES, LANES)).astype(jnp.int32)
    padded = jnp.where(is_expert, ((counts + (MOE_ROWS - 1)) >> shift) << shift, 0)
    pad_end = _mm_sel_rhs(padded.astype(F32), incl)
    pad_start = pad_end - padded.astype(F32)

    route = route_ref[...]
    lane_t = lax.broadcasted_iota(jnp.int32, route.shape, 1)
    lane_f = lane_t.astype(F32)
    start_row = pad_start[0:1, :]
    slots = [f32_sum(jnp.where(lane_f == route[:, k:k + 1] + MOE_GROUPS, start_row, 0.0), 1) + route[:, 4 + k:5 + k]
             for k in range(2)]
    both = jnp.where(lane_t == 0, slots[0], jnp.where(lane_t == 1, slots[1], 0.0))
    dest_ref[...] = both.T[0:8, :].astype(jnp.int32)

    on_sub = lambda rows_equal: rows_equal.T
    expert_sub = (sub >= MOE_GROUPS) & (sub < MOE_GROUPS + MOE_EXPERTS)
    block_start = (lane * MOE_ROWS).astype(F32)
    eid = f32_sum(jnp.where(expert_sub & (on_sub(pad_end) <= block_start), 1.0, 0.0), 0)
    eid = jnp.minimum(eid, float(MOE_EXPERTS - 1))
    filled = on_sub(pad_start + counts.astype(F32))
    own = (sub - MOE_GROUPS).astype(F32) == eid
    valid = jnp.clip(f32_sum(jnp.where(own, filled, 0.0), 0) - block_start[0:1, :], 0.0, float(MOE_ROWS))
    eid_rows = jnp.broadcast_to(eid, (LANES, LANES))
    changed = (lane == 0) | (eid_rows != pltpu.roll(eid_rows, 1, axis=1))
    first = jnp.where((jnp.broadcast_to(valid, (LANES, LANES)) > 0) & changed, 1.0, 0.0)
    ordinal = _mm_sel_rhs(first, incl) - 1.0
    slot = ordinal - 2.0 * jnp.floor(ordinal * 0.5)
    later = (on_sub(first) > 0) & (sub > lane)
    nearest = jnp.min(jnp.where(later, sub, LANES), axis=0, keepdims=True)
    next_eid = f32_sum(jnp.where(sub == nearest, on_sub(eid_rows), 0.0), 0)
    next_eid = jnp.where(nearest < LANES, next_eid, -1.0)
    row8 = lax.broadcasted_iota(jnp.int32, (8, LANES), 0)
    table = jnp.zeros((8, LANES), F32)
    for k, val in enumerate((eid, valid, first[0:1, :], slot[0:1, :], next_eid)):
        table = jnp.where(row8 == k, val, table)
    table_ref[...] = table.astype(jnp.int32)


def moe_plan(route, counts, tm=4096):
    n_tok = route.shape[0]
    tm = min(tm, n_tok)
    idx = np.arange(LANES)
    incl = jnp.asarray(idx[:, None] <= idx[None, :], BF16)
    return pl.pallas_call(
        _plan_kernel,
        grid=(n_tok // tm,),
        in_specs=[pl.BlockSpec((tm, LANES), lambda i: (i, 0)), pl.BlockSpec((1, LANES), lambda i: (0, 0)),
                  pl.BlockSpec((LANES, LANES), lambda i: (0, 0))],
        out_specs=[pl.BlockSpec((8, tm), lambda i: (0, i)), pl.BlockSpec((8, LANES), lambda i: (0, 0))],
        out_shape=[jax.ShapeDtypeStruct((8, n_tok), jnp.int32), jax.ShapeDtypeStruct((8, LANES), jnp.int32)],
        compiler_params=_cparams("arbitrary"),
        name="moe_plan",
    )(route, counts, incl)


def _router_weights(w_group, b_group, w_expert, b_expert):
    w_r = _pad_cols(jnp.concatenate([w_group, w_expert], axis=1))
    w_hi = w_r.astype(BF16)
    w_lo = (w_r - w_hi.astype(F32)).astype(BF16)
    return w_hi, w_lo, _pad_lanes(jnp.concatenate([b_group, b_expert]))


def _moe_layer(h, xn, route, counts, w_gate, w_up, w_down, layer, final_g, next_weights):
    n_tok, d = h.shape
    n_blocks = -(-(2 * n_tok + MOE_EXPERTS * (MOE_ROWS - 1)) // MOE_ROWS)
    dest, blocks = moe_plan(route, counts)
    xs = sc_scatter_rows(xn, dest, n_blocks * MOE_ROWS)
    if next_weights is not None:
        xs, next_weights = lax.optimization_barrier((xs, next_weights))
    ys = moe_experts(blocks, xs, w_gate, w_up, w_down, layer)
    y01 = sc_gather_rows(ys, dest, 2 * n_tok)
    g = jnp.ones((d,), F32) if final_g is None else final_g
    return moe_combine(h, y01, route, g, final_g is not None), next_weights


def _memory_kv(memn_in, mem_norm, wk, wv):
    bsz, m, d = memn_in.shape
    w = jnp.concatenate([wk, wv], axis=1).astype(BF16)
    kv, _ = rms_matmul(memn_in.reshape(bsz * m, d), mem_norm, w, jnp.zeros((d, LANES), BF16))
    k = kv[:, :d].reshape(bsz, m, d)
    v = kv[:, d:].reshape(bsz, m, d)
    return (jnp.swapaxes(k, 1, 2) * XA_HEAD_DIM ** -0.5).astype(BF16), v.astype(BF16)


def kernel(x, mem, mem_norm, final_norm, norm_mix, norm_xa, norm_ffn, xa_wq, xa_wk, xa_wv, xa_wo, moe_w_group, moe_b_group, moe_w_expert, moe_b_expert, moe_w_gate, moe_w_up, moe_w_down, ev_w_in, ev_sc_conv, ev_ssm_conv_w, ev_ssm_conv_b, ev_ssm_dt_bias, ev_ssm_a_log, ev_ssm_d, ev_ssm_norm, ev_w_out, od_w_in, od_gdn_conv, od_gdn_dt_bias, od_gdn_a_log, od_gdn_norm, od_w_out):
    bsz, length, d = x.shape
    n_tok = bsz * length
    depth = norm_mix.shape[0]
    h = x.reshape(n_tok, d)
    qkv_w = 3 * GDN_HEADS * GDN_D
    z_end = qkv_w + GDN_HEADS * GDN_D

    def in_proj_weights(layer):
        if layer % 2 == 0:
            wt = jnp.swapaxes(ev_w_in[layer // 2], 0, 1).astype(BF16)
            xbc0 = 3 * SC_DIM + SSM_INNER
            return wt, wt[xbc0:xbc0 + SSM_XBC].T
        wt = jnp.swapaxes(od_w_in[layer // 2], 0, 1).astype(BF16)
        w_main = jnp.concatenate([wt[qkv_w:z_end], wt[z_end + 2 * GDN_HEADS:]], axis=0)
        return wt[:qkv_w].T, w_main, _pad_rows(wt[z_end:z_end + 2 * GDN_HEADS])

    w_in = in_proj_weights(0)
    for layer in range(depth):
        i = layer // 2
        if layer % 2 == 0:
            wt, w_conv = w_in
            z0 = 3 * SC_DIM
            xbc0 = z0 + SSM_INNER
            w_small = _pad_rows(wt[xbc0 + SSM_XBC:])
            xbc = rms_matmul_conv(h, norm_mix[layer], w_conv, ev_ssm_conv_w[i], ev_ssm_conv_b[i], length)
            z, ya, small = rms_matmul_gated(h, norm_mix[layer], wt[z0:xbc0], wt[:z0], w_small, ev_sc_conv[i], length)
            small3 = small.reshape(bsz, length, LANES)
            smallt = jnp.swapaxes(small3[:, :, :16], 1, 2)
            yb = ssd_mixer(xbc.reshape(bsz, length, -1), z.reshape(bsz, length, -1), small3, smallt,
                           ev_ssm_dt_bias[i], ev_ssm_a_log[i], ev_ssm_d[i], ev_ssm_norm[i])
            w_out = ev_w_out
        else:
            w_conv, w_main, w_small = w_in
            qkv = rms_matmul_conv(h, norm_mix[layer], w_conv, od_gdn_conv[i], jnp.zeros((qkv_w,), F32), length)
            y, small = rms_matmul(h, norm_mix[layer], w_main, w_small, tm=1024, tn=w_main.shape[0], transposed=True)
            y3 = y.reshape(bsz, length, -1)
            small3 = small.reshape(bsz, length, LANES)
            smallt = jnp.swapaxes(small3[:, :, :16], 1, 2)
            ya = gated_deltanet_mixer(qkv.reshape(bsz, length, -1), y3, small3, smallt, od_gdn_dt_bias[i],
                                      od_gdn_a_log[i], od_gdn_norm[i])
            yb = stick_breaking_mixer(y3, GDN_HEADS * GDN_D)
            w_out = od_w_out
        kt, v = _memory_kv(mem, mem_norm, xa_wk[layer], xa_wv[layer])
        w_hi, w_lo, bias = _router_weights(moe_w_group[layer], moe_b_group[layer], moe_w_expert[layer],
                                           moe_b_expert[layer])
        h, xn, route, counts = post_mixer(
            ya.reshape(n_tok, -1), yb.reshape(n_tok, -1), h, w_out, i, norm_xa[layer],
            xa_wq, kt, v, xa_wo, norm_ffn[layer], w_hi, w_lo, bias, layer)
        last = layer == depth - 1
        h, w_in = _moe_layer(h, xn, route, counts, moe_w_gate, moe_w_up, moe_w_down, layer,
                             final_norm if last else None, None if last else in_proj_weights(layer + 1))
    return h.reshape(bsz, length, d)
```
